```python
import math
import jax, jax.numpy as jnp
from jax import lax
import numpy as np

D_MODEL = 1024
BATCH = 8
SEQ = 4096
DEPTH = 1

HEAD_DIM = 64
N_HEADS_MOBA = 8
N_HEADS_SB = 8
W_MOBA = N_HEADS_MOBA * HEAD_DIM
W_SB = N_HEADS_SB * HEAD_DIM
MOBA_BLOCK = 256
MOBA_TOPK = 3
MOBA_QCHUNK = 32
SB_QBLOCK = 128
REL_BUCKETS = 32
REL_MAX_DIST = 1024
N_GROUPS = 4
EXPERTS_PER_GROUP = 8
N_EXPERTS = N_GROUPS * EXPERTS_PER_GROUP
EXPERT_TOPK = 2
D_EXPERT = 512
MOE_ROW_BLOCK = 256
N_MOD = 6
RMS_EPS = 1e-6
IN_COLS = 3 * W_MOBA + 3 * W_SB + 2 * D_MODEL

kernel_name = "hybrid_moba_stickbreak_hmoe_block"


def rmsnorm(x, g):
    xf = x.astype(jnp.float32)
    y = xf * lax.rsqrt(jnp.mean(xf * xf, axis=-1, keepdims=True) + RMS_EPS)
    return (y * g.astype(jnp.float32)).astype(x.dtype)


def modulate(h, shift, scale):
    return h * (1 + scale) + shift


def rel_bucket(dist):
    n = jnp.maximum(dist, 0)
    max_exact = REL_BUCKETS // 2
    nf = jnp.maximum(n, 1).astype(jnp.float32)
    large = max_exact + (jnp.log(nf / max_exact) / math.log(REL_MAX_DIST / max_exact)
                         * (REL_BUCKETS - max_exact)).astype(jnp.int32)
    large = jnp.minimum(large, REL_BUCKETS - 1)
    return jnp.where(n < max_exact, n, large)


def moba_attention(q, k, v, rel_bias):
    B, H, S, Dh = q.shape
    n_blk = -(-S // MOBA_BLOCK)
    s_pad = n_blk * MOBA_BLOCK
    n_sel = min(MOBA_TOPK, n_blk)
    pad = ((0, 0), (0, 0), (0, s_pad - S), (0, 0))
    q, k, v = jnp.pad(q, pad), jnp.pad(k, pad), jnp.pad(v, pad)
    kb = k.reshape(B, H, n_blk, MOBA_BLOCK, Dh)
    vb = v.reshape(B, H, n_blk, MOBA_BLOCK, Dh)
    k_mean = jnp.mean(kb, axis=3)
    scale = Dh ** -0.5
    bi = jnp.arange(B)[:, None, None, None]
    hi = jnp.arange(H)[None, :, None, None]
    blk_ids = jnp.arange(n_blk)
    j_ids = jnp.arange(MOBA_BLOCK)

    def chunk(ci):
        q0 = ci * MOBA_QCHUNK
        own = q0 // MOBA_BLOCK
        qc = lax.dynamic_slice_in_dim(q, q0, MOBA_QCHUNK, axis=2)
        t = q0 + jnp.arange(MOBA_QCHUNK)
        gate = jnp.einsum('bhqd,bhnd->bhqn', qc, k_mean).astype(jnp.float32)
        gate = jnp.where(blk_ids < own, gate, -jnp.inf)
        _, sel = lax.top_k(gate, n_sel)
        slot_ok = jnp.arange(n_sel) < own
        k_sel = kb[bi, hi, sel]
        v_sel = vb[bi, hi, sel]
        s_sel = sel[..., None] * MOBA_BLOCK + j_ids
        bias_sel = rel_bias[hi[..., None], rel_bucket(t[None, None, :, None, None] - s_sel)]
        l_sel = jnp.einsum('bhqd,bhqkjd->bhqkj', qc, k_sel).astype(jnp.float32) * scale + bias_sel
        l_sel = jnp.where(slot_ok[:, None], l_sel, -jnp.inf)
        k_own = lax.dynamic_slice_in_dim(k, own * MOBA_BLOCK, MOBA_BLOCK, axis=2)
        v_own = lax.dynamic_slice_in_dim(v, own * MOBA_BLOCK, MOBA_BLOCK, axis=2)
        dist_own = t[:, None] - (own * MOBA_BLOCK + j_ids)[None, :]
        bias_own = rel_bias[:, rel_bucket(dist_own)]
        l_own = jnp.einsum('bhqd,bhjd->bhqj', qc, k_own).astype(jnp.float32) * scale + bias_own
        l_own = jnp.where(dist_own >= 0, l_own, -jnp.inf)
        logits = jnp.concatenate([l_sel.reshape(B, H, MOBA_QCHUNK, n_sel * MOBA_BLOCK), l_own], axis=-1)
        p = jax.nn.softmax(logits, axis=-1)
        p_sel = p[..., :n_sel * MOBA_BLOCK].reshape(B, H, MOBA_QCHUNK, n_sel, MOBA_BLOCK).astype(v.dtype)
        p_own = p[..., n_sel * MOBA_BLOCK:].astype(v.dtype)
        return (jnp.einsum('bhqkj,bhqkjd->bhqd', p_sel, v_sel)
                + jnp.einsum('bhqj,bhjd->bhqd', p_own, v_own))

    outs = lax.map(chunk, jnp.arange(s_pad // MOBA_QCHUNK))
    return outs.transpose(1, 2, 0, 3, 4).reshape(B, H, s_pad, Dh)[:, :, :S]


def stick_breaking_attention(q, k, v):
    B, H, S, Dh = q.shape
    scale = Dh ** -0.5
    s_pos = jnp.arange(S)

    def block(bq):
        q0 = bq * SB_QBLOCK
        qc = lax.dynamic_slice_in_dim(q, q0, SB_QBLOCK, axis=2)
        t = q0 + jnp.arange(SB_QBLOCK)
        z = jnp.einsum('bhqd,bhsd->bhqs', qc, k).astype(jnp.float32) * scale
        past = s_pos[None, :] < t[:, None]
        log_1m = jnp.where(past, jax.nn.log_sigmoid(-z), 0.0)
        after = lax.cumsum(log_1m, axis=3, reverse=True) - log_1m
        a = jnp.where(past, jnp.exp(jax.nn.log_sigmoid(z) + after), 0.0)
        return jnp.einsum('bhqs,bhsd->bhqd', a.astype(v.dtype), v)

    outs = lax.map(block, jnp.arange(S // SB_QBLOCK))
    return outs.transpose(1, 2, 0, 3, 4).reshape(B, H, S, Dh)


def hierarchical_moe(h, w_rg, b_rg, w_re, b_re, w_gate, w_up, w_down):
    B, S, D = h.shape
    N = B * S
    hf = h.reshape(N, D)
    g_prob = jax.nn.softmax((hf @ w_rg + b_rg).astype(jnp.float32), axis=-1)
    g_w, g_idx = lax.top_k(g_prob, 1)
    e_logits = (hf @ w_re + b_re).astype(jnp.float32).reshape(N, N_GROUPS, EXPERTS_PER_GROUP)
    e_logits = jnp.take_along_axis(e_logits, g_idx[:, :, None], axis=1)[:, 0]
    e_top, e_local = lax.top_k(e_logits, EXPERT_TOPK)
    e_w = jax.nn.softmax(e_top, axis=-1) * g_w
    e_id = g_idx * EXPERTS_PER_GROUP + e_local
    n_asg = N * EXPERT_TOPK
    flat_e = e_id.reshape(-1)
    flat_w = e_w.reshape(-1)
    flat_tok = jnp.repeat(jnp.arange(N, dtype=jnp.int32), EXPERT_TOPK)
    order = jnp.argsort(flat_e)
    se, stok, sw = flat_e[order], flat_tok[order], flat_w[order]
    counts = jnp.zeros(N_EXPERTS, jnp.int32).at[flat_e].add(1)
    start = jnp.cumsum(counts) - counts
    padded = (counts + MOE_ROW_BLOCK - 1) // MOE_ROW_BLOCK * MOE_ROW_BLOCK
    pend = jnp.cumsum(padded)
    pstart = pend - padded
    n_blocks = -(-n_asg // MOE_ROW_BLOCK) + N_EXPERTS
    n_rows = n_blocks * MOE_ROW_BLOCK
    dest = pstart[se] + jnp.arange(n_asg, dtype=jnp.int32) - start[se]
    row_tok = jnp.zeros(n_rows, jnp.int32).at[dest].set(stok)
    row_w = jnp.zeros(n_rows, jnp.float32).at[dest].set(sw)
    blk_e = jnp.minimum(jnp.searchsorted(pend, jnp.arange(n_blocks) * MOE_ROW_BLOCK, side='right'),
                        N_EXPERTS - 1)

    def expert_block(args):
        tok, w, e = args
        xb = hf[tok]
        hid = jax.nn.silu(xb @ w_gate[e]) * (xb @ w_up[e])
        return (hid @ w_down[e]) * w[:, None].astype(xb.dtype)

    y = lax.map(expert_block, (row_tok.reshape(n_blocks, MOE_ROW_BLOCK),
                               row_w.reshape(n_blocks, MOE_ROW_BLOCK), blk_e))
    out = jnp.zeros((N, D), h.dtype).at[row_tok].add(y.reshape(n_rows, D))
    return out.reshape(B, S, D)


def split_heads(t, n_heads):
    B, S, _ = t.shape
    return t.reshape(B, S, n_heads, HEAD_DIM).transpose(0, 2, 1, 3)


def merge_heads(t):
    B, H, S, Dh = t.shape
    return t.transpose(0, 2, 1, 3).reshape(B, S, H * Dh)


def setup_inputs(seed: int = 0) -> dict:
    key = jax.random.key(seed)
    ks = jax.random.split(key, 20)
    f = jnp.float32
    D = D_MODEL
    nrm = lambda k, shape, s: jax.random.normal(k, shape, f) * s
    return {
        "x": nrm(ks[0], (BATCH, SEQ, D), 1.0),
        "c": nrm(ks[1], (BATCH, D), 1.0),
        "w_ada": nrm(ks[2], (DEPTH, D, N_MOD * D), 0.5 * D ** -0.5),
        "b_ada": nrm(ks[3], (DEPTH, N_MOD * D), 0.01),
        "g_mix": 1.0 + nrm(ks[4], (DEPTH, D), 0.02),
        "w_in": nrm(ks[5], (DEPTH, D, IN_COLS), D ** -0.5),
        "g_q": 1.0 + nrm(ks[6], (DEPTH, HEAD_DIM), 0.02),
        "g_k": 1.0 + nrm(ks[7], (DEPTH, HEAD_DIM), 0.02),
        "rel_bias": nrm(ks[8], (N_HEADS_MOBA, REL_BUCKETS), 0.5),
        "w_br_moba": nrm(ks[9], (DEPTH, W_MOBA, D), W_MOBA ** -0.5),
        "w_br_sb": nrm(ks[10], (DEPTH, W_SB, D), W_SB ** -0.5),
        "w_out": nrm(ks[11], (DEPTH, D, D), D ** -0.5),
        "g_ffn": 1.0 + nrm(ks[12], (DEPTH, D), 0.02),
        "w_rg": nrm(ks[13], (DEPTH, D, N_GROUPS), D ** -0.5),
        "b_rg": nrm(ks[14], (DEPTH, N_GROUPS), 0.01),
        "w_re": nrm(ks[15], (DEPTH, D, N_EXPERTS), D ** -0.5),
        "b_re": nrm(ks[16], (DEPTH, N_EXPERTS), 0.01),
        "w_gate": nrm(ks[17], (DEPTH, N_EXPERTS, D, D_EXPERT), D ** -0.5),
        "w_up": nrm(ks[18], (DEPTH, N_EXPERTS, D, D_EXPERT), D ** -0.5),
        "w_down": nrm(ks[19], (DEPTH, N_EXPERTS, D_EXPERT, D), D_EXPERT ** -0.5),
    }


def reference(x, c, w_ada, b_ada, g_mix, w_in, g_q, g_k, rel_bias, w_br_moba, w_br_sb, w_out,
              g_ffn, w_rg, b_rg, w_re, b_re, w_gate, w_up, w_down):
    B, S, D = x.shape
    splits = list(np.cumsum([W_MOBA, W_MOBA, W_MOBA, W_SB, W_SB, W_SB, D_MODEL]))
    for l in range(DEPTH):
        mod = (jax.nn.silu(c) @ w_ada[l] + b_ada[l]).reshape(B, N_MOD, 1, D)
        sh_a, sc_a, gt_a, sh_f, sc_f, gt_f = (mod[:, i] for i in range(N_MOD))
        h = modulate(rmsnorm(x, g_mix[l]), sh_a, sc_a)
        proj = h @ w_in[l]
        qa, ka, va, qb, kb, vb, gate_a, gate_b = jnp.split(proj, splits, axis=-1)
        qa = rmsnorm(split_heads(qa, N_HEADS_MOBA), g_q[l])
        ka = rmsnorm(split_heads(ka, N_HEADS_MOBA), g_k[l])
        o_a = merge_heads(moba_attention(qa, ka, split_heads(va, N_HEADS_MOBA), rel_bias))
        o_b = merge_heads(stick_breaking_attention(split_heads(qb, N_HEADS_SB),
                                                   split_heads(kb, N_HEADS_SB),
                                                   split_heads(vb, N_HEADS_SB)))
        merged = (jax.nn.sigmoid(gate_a) * (o_a @ w_br_moba[l])
                  + jax.nn.sigmoid(gate_b) * (o_b @ w_br_sb[l]))
        x = x + gt_a * (merged @ w_out[l])
        h2 = modulate(rmsnorm(x, g_ffn[l]), sh_f, sc_f)
        x = x + gt_f * hierarchical_moe(h2, w_rg[l], b_rg[l], w_re[l], b_re[l],
                                        w_gate[l], w_up[l], w_down[l])
    return x
```

```python
import functools
import math

import numpy as np
import jax
import jax.numpy as jnp
from jax import lax
from jax.experimental import pallas as pl
from jax.experimental.pallas import tpu as pltpu

F32 = jnp.float32
BF16 = jnp.bfloat16
U32 = jnp.uint32
HIGHEST = lax.Precision.HIGHEST

LANES = 128
VMEM_LIMIT_BYTES = 56 * 1024 * 1024

HEAD_DIM = 64
N_HEADS = 8
HEADS_PER_STEP = LANES // HEAD_DIM
ATT_BLOCK = 256
MOBA_TOPK = 3
REL_BUCKETS = 32
REL_MAX_DIST = 1024
N_GROUPS = 4
EXPERTS_PER_GROUP = 8
N_EXPERTS = N_GROUPS * EXPERTS_PER_GROUP
N_MOD = 6
RMS_EPS = 1e-6
ROW_BLOCK = 256
TOKEN_TILE = 512
NEG_INF = float("-inf")
SB_EXIT_LOG = -110.0

_NT = (((1,), (1,)), ((), ()))


def _params(*semantics):
    return pltpu.CompilerParams(dimension_semantics=semantics, vmem_limit_bytes=VMEM_LIMIT_BYTES)


def _adaln_kernel(c_ref, w_ref, b_ref, o_ref):
    c = c_ref[...]
    s = c * jax.nn.sigmoid(c)
    o_ref[...] = jnp.dot(s, w_ref[...], preferred_element_type=F32, precision=HIGHEST) + b_ref[...]


def _adaln(c, w, b):
    bsz, d = c.shape
    n = w.shape[1]
    tn = 1536
    return pl.pallas_call(
        _adaln_kernel,
        grid=(n // tn,),
        in_specs=[
            pl.BlockSpec((bsz, d), lambda j: (0, 0)),
            pl.BlockSpec((d, tn), lambda j: (0, j)),
            pl.BlockSpec((1, tn), lambda j: (0, j)),
        ],
        out_specs=pl.BlockSpec((bsz, tn), lambda j: (0, j)),
        out_shape=jax.ShapeDtypeStruct((bsz, n), F32),
        compiler_params=_params("arbitrary"),
        name="adaln",
    )(c, w, b.reshape(1, n))


def _rms_modulate(x, g, shift, scale):
    y = x * lax.rsqrt(jnp.mean(x * x, axis=-1, keepdims=True) + RMS_EPS) * g
    return y * (1.0 + scale) + shift


def _inproj_kernel(x_ref, mod_ref, g_ref, w_ref, o_ref, *, col_chunk):
    h = _rms_modulate(x_ref[0], g_ref[...], mod_ref[0, 0:1, :], mod_ref[0, 1:2, :]).astype(BF16)
    for n in range(w_ref.shape[1] // col_chunk):
        cols = slice(n * col_chunk, (n + 1) * col_chunk)
        o_ref[0, :, cols] = jnp.dot(h, w_ref[:, cols], preferred_element_type=F32).astype(BF16)


def _inproj(x, mod, g, w_bf16):
    bsz, s, d = x.shape
    n = w_bf16.shape[1]
    tm = TOKEN_TILE
    return pl.pallas_call(
        functools.partial(_inproj_kernel, col_chunk=1024),
        grid=(bsz, s // tm),
        in_specs=[
            pl.BlockSpec((1, tm, d), lambda b, i: (b, i, 0)),
            pl.BlockSpec((1, N_MOD, d), lambda b, i: (b, 0, 0)),
            pl.BlockSpec((1, d), lambda b, i: (0, 0)),
            pl.BlockSpec((d, n), lambda b, i: (0, 0)),
        ],
        out_specs=pl.BlockSpec((1, tm, n), lambda b, i: (b, i, 0)),
        out_shape=jax.ShapeDtypeStruct((bsz, s, n), BF16),
        compiler_params=_params("arbitrary", "arbitrary"),
        name="inproj",
    )(x, mod, g.reshape(1, d), w_bf16)


def _indicator(cond, dtype):
    return jnp.where(cond, 1.0, 0.0).astype(dtype)


def _lane_index():
    return lax.broadcasted_iota(jnp.int32, (1, LANES), 1).astype(F32)


def _head_lane_mask():
    return lax.broadcasted_iota(jnp.int32, (1, LANES), 1) < HEAD_DIM


def _head_rmsnorm(t, g):
    first = _head_lane_mask()
    sq = t * t
    ss0 = jnp.sum(jnp.where(first, sq, 0.0), axis=-1, keepdims=True)
    ss1 = jnp.sum(jnp.where(first, 0.0, sq), axis=-1, keepdims=True)
    inv = jnp.where(first, lax.rsqrt(ss0 / HEAD_DIM + RMS_EPS), lax.rsqrt(ss1 / HEAD_DIM + RMS_EPS))
    return t * inv * g


def _split_heads(t):
    first = _head_lane_mask()
    zero = jnp.zeros_like(t)
    return jnp.where(first, t, zero), jnp.where(first, zero, t)


def _rel_bucket_table(max_dist):
    n = np.arange(max_dist + 1)
    max_exact = REL_BUCKETS // 2
    nf = np.maximum(n, 1).astype(np.float64)
    large = max_exact + (np.log(nf / max_exact) / math.log(REL_MAX_DIST / max_exact)
                         * (REL_BUCKETS - max_exact)).astype(np.int64)
    large = np.minimum(large, REL_BUCKETS - 1)
    return np.where(n < max_exact, n, large).astype(np.int32)


def _moba_kernel(q_ref, k_ref, v_ref, gq_ref, gk_ref, rb_ref, o_ref,
                 kn_ref, kmean_ref, toep_ref, acc_ref, m_ref, l_ref, *, n_blk):
    b = pl.program_id(1)
    qi = pl.program_id(2)
    blk = ATT_BLOCK
    lane = _lane_index()
    qi_f = qi.astype(F32)

    @pl.when(jnp.logical_and(b == 0, qi == 0))
    def _build_bias_tiles():
        for h in range(HEADS_PER_STEP):
            for d in range(n_blk):
                r = rb_ref[0, h:h + 1, (n_blk - d) * blk:(n_blk - d + 2) * blk]
                rolled = pltpu.roll(jnp.broadcast_to(r, (blk, 2 * blk)), blk + 1, 1, stride=1, stride_axis=0)
                toep_ref[h, d] = rolled[:, :blk]

    @pl.when(qi == 0)
    def _prep_keys():
        kmean_ref[...] = jnp.zeros_like(kmean_ref)
        for j in range(n_blk):
            kn = _head_rmsnorm(k_ref[0, j * blk:(j + 1) * blk, :].astype(F32), gk_ref[...])
            kn_ref[j * blk:(j + 1) * blk, :] = kn.astype(BF16)
            kmean_ref[j:j + 1, :] = jnp.mean(kn, axis=0, keepdims=True)

    qn = _head_rmsnorm(q_ref[0].astype(F32), gq_ref[...])
    qn_heads = _split_heads(qn)
    qs_heads = [(t * (HEAD_DIM ** -0.5)).astype(BF16) for t in qn_heads]

    sel = []
    for h in range(HEADS_PER_STEP):
        gate = lax.dot_general(qn_heads[h], kmean_ref[...], _NT, preferred_element_type=F32, precision=HIGHEST)
        g = jnp.where(lane < qi_f, gate, NEG_INF)
        chosen = jnp.zeros(g.shape, F32)
        for _ in range(MOBA_TOPK):
            top = jnp.max(g, axis=-1, keepdims=True)
            is_top = jnp.logical_and(g == top, top > NEG_INF)
            idx = jnp.min(jnp.where(is_top, lane, float(LANES)), axis=-1, keepdims=True)
            pick = lane == idx
            chosen = jnp.where(pick, 1.0, chosen)
            g = jnp.where(pick, NEG_INF, g)
        sel.append(chosen)

    row = lax.broadcasted_iota(jnp.int32, (blk, blk), 0)
    col = lax.broadcasted_iota(jnp.int32, (blk, blk), 1)
    causal = col <= row
    k_own = kn_ref[pl.ds(pl.multiple_of(qi * blk, blk), blk), :]
    v_own = v_ref[0, pl.ds(pl.multiple_of(qi * blk, blk), blk), :]
    for h in range(HEADS_PER_STEP):
        s = lax.dot_general(qs_heads[h], k_own, _NT, preferred_element_type=F32) + toep_ref[h, 0]
        s = jnp.where(causal, s, NEG_INF)
        m = jnp.max(s, axis=-1, keepdims=True)
        p = jnp.exp(s - m)
        m_ref[h] = m
        l_ref[h] = jnp.sum(p, axis=-1, keepdims=True)
        acc_ref[h] = jnp.dot(p.astype(BF16), v_own, preferred_element_type=F32)

    def past_block(j, carry):
        start = pl.multiple_of(j * blk, blk)
        k_j = kn_ref[pl.ds(start, blk), :]
        v_j = v_ref[0, pl.ds(start, blk), :]
        for h in range(HEADS_PER_STEP):
            picked = jnp.sum(jnp.where(lane == j.astype(F32), sel[h], 0.0), axis=-1, keepdims=True) > 0.0
            s = lax.dot_general(qs_heads[h], k_j, _NT, preferred_element_type=F32) + toep_ref[h, qi - j]
            s = jnp.where(picked, s, NEG_INF)
            m_old = m_ref[h]
            m_new = jnp.maximum(m_old, jnp.max(s, axis=-1, keepdims=True))
            alpha = jnp.exp(m_old - m_new)
            p = jnp.exp(s - m_new)
            m_ref[h] = m_new
            l_ref[h] = alpha * l_ref[h] + jnp.sum(p, axis=-1, keepdims=True)
            acc_ref[h] = alpha * acc_ref[h] + jnp.dot(p.astype(BF16), v_j, preferred_element_type=F32)
        return carry

    lax.fori_loop(0, qi, past_block, 0)

    first = _head_lane_mask()
    out = jnp.where(first, acc_ref[0] / l_ref[0], acc_ref[1] / l_ref[1])
    o_ref[0] = out.astype(o_ref.dtype)


def _moba(proj, g_q, g_k, rel_rev):
    bsz, s, _ = proj.shape
    blk = ATT_BLOCK
    n_blk = s // blk
    n_hp = N_HEADS // HEADS_PER_STEP
    w = N_HEADS * HEAD_DIM
    tile2 = lambda g: jnp.tile(g.reshape(1, HEAD_DIM), (1, HEADS_PER_STEP))
    return pl.pallas_call(
        functools.partial(_moba_kernel, n_blk=n_blk),
        grid=(n_hp, bsz, n_blk),
        in_specs=[
            pl.BlockSpec((1, blk, LANES), lambda hp, b, i: (b, i, hp)),
            pl.BlockSpec((1, s, LANES), lambda hp, b, i: (b, 0, n_hp + hp)),
            pl.BlockSpec((1, s, LANES), lambda hp, b, i: (b, 0, 2 * n_hp + hp)),
            pl.BlockSpec((1, LANES), lambda hp, b, i: (0, 0)),
            pl.BlockSpec((1, LANES), lambda hp, b, i: (0, 0)),
            pl.BlockSpec((1, HEADS_PER_STEP, rel_rev.shape[-1]), lambda hp, b, i: (hp, 0, 0)),
        ],
        out_specs=pl.BlockSpec((1, blk, LANES), lambda hp, b, i: (b, i, hp)),
        out_shape=jax.ShapeDtypeStruct((bsz, s, w), BF16),
        scratch_shapes=[
            pltpu.VMEM((s, LANES), BF16),
            pltpu.VMEM((LANES, LANES), F32),
            pltpu.VMEM((HEADS_PER_STEP, n_blk, blk, blk), F32),
            pltpu.VMEM((HEADS_PER_STEP, blk, LANES), F32),
            pltpu.VMEM((HEADS_PER_STEP, blk, 1), F32),
            pltpu.VMEM((HEADS_PER_STEP, blk, 1), F32),
        ],
        compiler_params=_params("arbitrary", "arbitrary", "arbitrary"),
        name="moba",
    )(proj, proj, proj, tile2(g_q), tile2(g_k), rel_rev)


def _sb_kernel(q_ref, k_ref, v_ref, o_ref, acc_ref, carry_ref):
    qi = pl.program_id(2)
    blk = ATT_BLOCK
    row = lax.broadcasted_iota(jnp.int32, (blk, blk), 0)
    col = lax.broadcasted_iota(jnp.int32, (blk, blk), 1)
    past = col < row
    later = _indicator(row > col, BF16)

    q_heads = _split_heads(q_ref[0].astype(F32) * (HEAD_DIM ** -0.5))
    q_heads = [t.astype(BF16) for t in q_heads]

    def tile(j, diagonal):
        start = pl.multiple_of(j * blk, blk)
        k_j = k_ref[0, pl.ds(start, blk), :]
        v_j = v_ref[0, pl.ds(start, blk), :]
        worst = None
        for h in range(HEADS_PER_STEP):
            z = lax.dot_general(q_heads[h], k_j, _NT, preferred_element_type=F32)
            log_1m = -(jnp.maximum(z, 0.0) + jnp.log(1.0 + jnp.exp(-jnp.abs(z))))
            masked = jnp.where(past, log_1m, 0.0) if diagonal else log_1m
            hi = masked.astype(BF16)
            lo = (masked - hi.astype(F32)).astype(BF16)
            suffix = (jnp.dot(hi, later, preferred_element_type=F32)
                      + jnp.dot(lo, later, preferred_element_type=F32))
            carry = 0.0 if diagonal else carry_ref[h]
            a = jnp.exp(z + log_1m + suffix + carry)
            if diagonal:
                a = jnp.where(past, a, 0.0)
            pv = jnp.dot(a.astype(BF16), v_j, preferred_element_type=F32)
            new_carry = carry + jnp.sum(masked, axis=-1, keepdims=True)
            carry_ref[h] = new_carry
            acc_ref[h] = pv if diagonal else acc_ref[h] + pv
            top = jnp.max(new_carry)
            worst = top if worst is None else jnp.maximum(worst, top)
        return worst

    worst0 = tile(qi, True)

    def cond(state):
        j, worst = state
        return jnp.logical_and(j >= 0, worst > SB_EXIT_LOG)

    def body(state):
        j, _ = state
        return j - 1, tile(j, False)

    lax.while_loop(cond, body, (qi - 1, worst0))

    o_ref[0] = jnp.where(_head_lane_mask(), acc_ref[0], acc_ref[1]).astype(o_ref.dtype)


def _stick_breaking(proj):
    bsz, s, _ = proj.shape
    blk = ATT_BLOCK
    n_hp = N_HEADS // HEADS_PER_STEP
    w = N_HEADS * HEAD_DIM
    base = 3 * n_hp
    return pl.pallas_call(
        _sb_kernel,
        grid=(n_hp, bsz, s // blk),
        in_specs=[
            pl.BlockSpec((1, blk, LANES), lambda hp, b, i: (b, i, base + hp)),
            pl.BlockSpec((1, s, LANES), lambda hp, b, i: (b, 0, base + n_hp + hp)),
            pl.BlockSpec((1, s, LANES), lambda hp, b, i: (b, 0, base + 2 * n_hp + hp)),
        ],
        out_specs=pl.BlockSpec((1, blk, LANES), lambda hp, b, i: (b, i, hp)),
        out_shape=jax.ShapeDtypeStruct((bsz, s, w), BF16),
        scratch_shapes=[
            pltpu.VMEM((HEADS_PER_STEP, blk, LANES), F32),
            pltpu.VMEM((HEADS_PER_STEP, blk, 1), F32),
        ],
        compiler_params=_params("arbitrary", "arbitrary", "arbitrary"),
        name="stickbrk",
    )(proj, proj, proj)


def _pack_rows(t):
    n = t.shape[1] // 2
    bits = pltpu.bitcast(t.astype(BF16).astype(F32), U32)
    return bits[:, :n] | (bits[:, n:] >> 16)


def _unpack_rows(p):
    hi = pltpu.bitcast(p & jnp.uint32(0xFFFF0000), F32)
    lo = pltpu.bitcast(p << 16, F32)
    return jnp.concatenate([hi, lo], axis=1)


def _merge_kernel(x_ref, oa_ref, ob_ref, ga_ref, gb_ref, mod_ref, wbm_ref, wbs_ref, wo_ref, gf_ref,
                  wr_ref, br_ref, x1_ref, h2_ref, lg_ref):
    ma = jnp.dot(oa_ref[0], wbm_ref[...], preferred_element_type=F32)
    mb = jnp.dot(ob_ref[0], wbs_ref[...], preferred_element_type=F32)
    merged = (jax.nn.sigmoid(ga_ref[0].astype(F32)) * ma + jax.nn.sigmoid(gb_ref[0].astype(F32)) * mb)
    t = jnp.dot(merged.astype(BF16), wo_ref[...], preferred_element_type=F32)
    x1 = x_ref[0] + mod_ref[0, 2:3, :] * t
    x1_ref[0] = x1
    h2 = _rms_modulate(x1, gf_ref[...], mod_ref[0, 3:4, :], mod_ref[0, 4:5, :])
    h2_ref[0] = _pack_rows(h2)
    lg_ref[0] = jnp.dot(h2, wr_ref[...], preferred_element_type=F32, precision=HIGHEST) + br_ref[...]


def _merge(x, o_a, o_b, proj, mod, wbm, wbs, wo, g_ffn, w_router, b_router):
    bsz, s, d = x.shape
    w = o_a.shape[-1]
    tm = TOKEN_TILE
    gate_blk = (proj.shape[-1] - 2 * d) // d
    const = lambda shape: pl.BlockSpec(shape, lambda b, i: (0,) * len(shape))
    tok = lambda width: pl.BlockSpec((1, tm, width), lambda b, i: (b, i, 0))
    return pl.pallas_call(
        _merge_kernel,
        grid=(bsz, s // tm),
        in_specs=[
            tok(d), tok(w), tok(w),
            pl.BlockSpec((1, tm, d), lambda b, i: (b, i, gate_blk)),
            pl.BlockSpec((1, tm, d), lambda b, i: (b, i, gate_blk + 1)),
            pl.BlockSpec((1, N_MOD, d), lambda b, i: (b, 0, 0)),
            const((w, d)), const((w, d)), const((d, d)), const((1, d)),
            const((d, LANES)), const((1, LANES)),
        ],
        out_specs=[tok(d), tok(d // 2), tok(LANES)],
        out_shape=[
            jax.ShapeDtypeStruct((bsz, s, d), F32),
            jax.ShapeDtypeStruct((bsz, s, d // 2), U32),
            jax.ShapeDtypeStruct((bsz, s, LANES), F32),
        ],
        compiler_params=_params("arbitrary", "arbitrary"),
        name="merge",
    )(x, o_a, o_b, proj, proj, mod, wbm, wbs, wo, g_ffn.reshape(1, d), w_router, b_router)


ROUTE_COLS = 8


def _route_kernel(lg_ref, rt_ref, cnt_ref, carry_ref):
    i = pl.program_id(0)
    tm = lg_ref.shape[0]
    lane = _lane_index()

    @pl.when(i == 0)
    def _init():
        carry_ref[...] = jnp.zeros_like(carry_ref)

    lg = lg_ref[...]

    def first_max(vals):
        top = jnp.max(vals, axis=-1, keepdims=True)
        idx = jnp.min(jnp.where(vals == top, lane, float(LANES)), axis=-1, keepdims=True)
        return top, idx

    g_logit = jnp.where(lane < N_GROUPS, lg, NEG_INF)
    g_top, g_idx = first_max(g_logit)
    g_w = 1.0 / jnp.sum(jnp.exp(g_logit - g_top), axis=-1, keepdims=True)
    lo = N_GROUPS + g_idx * EXPERTS_PER_GROUP
    e_logit = jnp.where(jnp.logical_and(lane >= lo, lane < lo + EXPERTS_PER_GROUP), lg, NEG_INF)
    e_top0, lane0 = first_max(e_logit)
    e_top1, lane1 = first_max(jnp.where(lane == lane0, NEG_INF, e_logit))
    z = jnp.exp(e_top1 - e_top0)
    w0 = g_w / (1.0 + z)
    w1 = g_w * z / (1.0 + z)

    hot = jnp.logical_or(lane == lane0, lane == lane1)
    row = lax.broadcasted_iota(jnp.int32, (tm, tm), 0)
    col = lax.broadcasted_iota(jnp.int32, (tm, tm), 1)
    before = jnp.dot(_indicator(col < row, BF16), _indicator(hot, BF16), preferred_element_type=F32) + carry_ref[...]
    rank0 = jnp.sum(jnp.where(lane == lane0, before, 0.0), axis=-1, keepdims=True)
    rank1 = jnp.sum(jnp.where(lane == lane1, before, 0.0), axis=-1, keepdims=True)
    carry_ref[...] += jnp.sum(_indicator(hot, F32), axis=0, keepdims=True)
    cnt_ref[...] = carry_ref[...]

    cols = [lane0 - N_GROUPS, lane1 - N_GROUPS, w0, w1, rank0, rank1]
    packed = jnp.zeros((tm, LANES), F32)
    for c, v in enumerate(cols):
        packed = jnp.where(lane == float(c), v, packed)
    rt_ref[...] = packed[:, :ROUTE_COLS]


def _route(logits):
    n = logits.shape[0]
    tm = TOKEN_TILE
    return pl.pallas_call(
        _route_kernel,
        grid=(n // tm,),
        in_specs=[pl.BlockSpec((tm, LANES), lambda i: (i, 0))],
        out_specs=[pl.BlockSpec((tm, ROUTE_COLS), lambda i: (i, 0)), pl.BlockSpec((1, LANES), lambda i: (0, 0))],
        out_shape=[jax.ShapeDtypeStruct((n, ROUTE_COLS), F32), jax.ShapeDtypeStruct((1, LANES), F32)],
        scratch_shapes=[pltpu.VMEM((1, LANES), F32)],
        compiler_params=_params("arbitrary"),
        name="route",
    )(logits)


def _dispatch_kernel(dest_ref, h_ref, xs_in_ref, xs_ref, sem):
    del xs_in_ref
    tm = h_ref.shape[0]

    def row_copy(t, k):
        return pltpu.make_async_copy(h_ref.at[pl.ds(t, 1), :], xs_ref.at[pl.ds(dest_ref[2 * t + k], 1), :], sem)

    def start(t, c):
        row_copy(t, 0).start()
        row_copy(t, 1).start()
        return c

    def wait(t, c):
        row_copy(t, 0).wait()
        row_copy(t, 1).wait()
        return c

    lax.fori_loop(0, tm, start, 0)
    lax.fori_loop(0, tm, wait, 0)


def _dispatch(dest_flat, h2_packed, n_rows):
    n, half = h2_packed.shape
    tm = TOKEN_TILE
    xs0 = jnp.zeros((n_rows, half), U32)
    return pl.pallas_call(
        _dispatch_kernel,
        grid=(n // tm,),
        in_specs=[
            pl.BlockSpec((2 * tm,), lambda i: (i,), memory_space=pltpu.SMEM),
            pl.BlockSpec((tm, half), lambda i: (i, 0)),
            pl.BlockSpec(memory_space=pl.ANY),
        ],
        out_specs=pl.BlockSpec(memory_space=pl.ANY),
        out_shape=jax.ShapeDtypeStruct((n_rows, half), U32),
        scratch_shapes=[pltpu.SemaphoreType.DMA(())],
        input_output_aliases={2: 0},
        compiler_params=_params("arbitrary"),
        name="dispatch",
    )(dest_flat, h2_packed, xs0)


def _expert_kernel(blk_e_ref, n_used_ref, xs_ref, wg_ref, wu_ref, wd_ref, ys_ref):
    del blk_e_ref
    i = pl.program_id(0)

    @pl.when(i < n_used_ref[0])
    def _compute():
        xb = _unpack_rows(xs_ref[...]).astype(BF16)
        g = jnp.dot(xb, wg_ref[0], preferred_element_type=F32)
        u = jnp.dot(xb, wu_ref[0], preferred_element_type=F32)
        hid = (g * jax.nn.sigmoid(g) * u).astype(BF16)
        ys_ref[...] = _pack_rows(jnp.dot(hid, wd_ref[0], preferred_element_type=F32))

    @pl.when(i >= n_used_ref[0])
    def _unused():
        ys_ref[...] = jnp.zeros_like(ys_ref)


def _experts(blk_e, n_used, xs, wg, wu, wd):
    n_rows, half = xs.shape
    _, d, de = wg.shape
    grid_spec = pltpu.PrefetchScalarGridSpec(
        num_scalar_prefetch=2,
        grid=(n_rows // ROW_BLOCK,),
        in_specs=[
            pl.BlockSpec((ROW_BLOCK, half), lambda i, e, u: (i, 0)),
            pl.BlockSpec((1, d, de), lambda i, e, u: (e[i], 0, 0)),
            pl.BlockSpec((1, d, de), lambda i, e, u: (e[i], 0, 0)),
            pl.BlockSpec((1, de, d), lambda i, e, u: (e[i], 0, 0)),
        ],
        out_specs=pl.BlockSpec((ROW_BLOCK, half), lambda i, e, u: (i, 0)),
    )
    return pl.pallas_call(
        _expert_kernel,
        grid_spec=grid_spec,
        out_shape=jax.ShapeDtypeStruct((n_rows, half), U32),
        compiler_params=_params("arbitrary"),
        name="experts",
    )(blk_e, n_used, xs, wg, wu, wd)


def _combine_kernel(dest_ref, x1_ref, mod_ref, rt_ref, ys_ref, o_ref, buf_ref, sem):
    tm = x1_ref.shape[1]

    def row_copy(t, k):
        return pltpu.make_async_copy(ys_ref.at[pl.ds(dest_ref[2 * t + k], 1), :],
                                     buf_ref.at[k, pl.ds(t, 1), :], sem)

    def start(t, c):
        row_copy(t, 0).start()
        row_copy(t, 1).start()
        return c

    def wait(t, c):
        row_copy(t, 0).wait()
        row_copy(t, 1).wait()
        return c

    lax.fori_loop(0, tm, start, 0)
    lax.fori_loop(0, tm, wait, 0)
    rt = rt_ref[0]
    moe = rt[:, 2:3] * _unpack_rows(buf_ref[0]) + rt[:, 3:4] * _unpack_rows(buf_ref[1])
    o_ref[0] = x1_ref[0] + mod_ref[0, 5:6, :] * moe


def _combine(dest_flat, x1, mod, rt, ys):
    bsz, s, d = x1.shape
    tm = TOKEN_TILE
    per_b = s // tm
    return pl.pallas_call(
        _combine_kernel,
        grid=(bsz, per_b),
        in_specs=[
            pl.BlockSpec((2 * tm,), lambda b, i: (b * per_b + i,), memory_space=pltpu.SMEM),
            pl.BlockSpec((1, tm, d), lambda b, i: (b, i, 0)),
            pl.BlockSpec((1, N_MOD, d), lambda b, i: (b, 0, 0)),
            pl.BlockSpec((1, tm, ROUTE_COLS), lambda b, i: (b, i, 0)),
            pl.BlockSpec(memory_space=pl.ANY),
        ],
        out_specs=pl.BlockSpec((1, tm, d), lambda b, i: (b, i, 0)),
        out_shape=jax.ShapeDtypeStruct((bsz, s, d), F32),
        scratch_shapes=[pltpu.VMEM((2, tm, d // 2), U32), pltpu.SemaphoreType.DMA(())],
        compiler_params=_params("arbitrary", "arbitrary"),
        name="combine",
    )(dest_flat, x1, mod, rt, ys)


def _layer(x, mod, g_mix, w_in, g_q, g_k, rel_bias, w_br_moba, w_br_sb, w_out, g_ffn,
           w_rg, b_rg, w_re, b_re, w_gate, w_up, w_down):
    bsz, s, d = x.shape
    n = bsz * s
    n_blk = s // ATT_BLOCK

    proj = _inproj(x, mod, g_mix, w_in.astype(BF16))

    top = (n_blk + 1) * ATT_BLOCK - 1
    dist = np.maximum(top - np.arange((n_blk + 2) * ATT_BLOCK), 0)
    rel_rev = rel_bias[:, _rel_bucket_table(top)[dist]].reshape(N_HEADS // HEADS_PER_STEP, HEADS_PER_STEP, -1)

    o_a = _moba(proj, g_q, g_k, rel_rev)
    o_b = _stick_breaking(proj)

    pad = LANES - N_GROUPS - N_EXPERTS
    w_router = jnp.concatenate([w_rg, w_re, jnp.zeros((d, pad), F32)], axis=1)
    b_router = jnp.concatenate([b_rg, b_re, jnp.zeros((pad,), F32)]).reshape(1, LANES)
    x1, h2, logits = _merge(x, o_a, o_b, proj, mod, w_br_moba.astype(BF16), w_br_sb.astype(BF16),
                            w_out.astype(BF16), g_ffn, w_router, b_router)

    rt, counts = _route(logits.reshape(n, LANES))

    cnt = counts[0, N_GROUPS:N_GROUPS + N_EXPERTS].astype(jnp.int32)
    padded = (cnt + ROW_BLOCK - 1) // ROW_BLOCK * ROW_BLOCK
    pend = jnp.cumsum(padded)
    pstart = pend - padded
    n_blocks = (n * 2) // ROW_BLOCK + N_EXPERTS
    blk_e = jnp.minimum(jnp.searchsorted(pend, jnp.arange(n_blocks, dtype=jnp.int32) * ROW_BLOCK, side="right"),
                        N_EXPERTS - 1).astype(jnp.int32)
    n_used = (pend[-1] // ROW_BLOCK).astype(jnp.int32).reshape(1)
    e_id = rt[:, 0:2].astype(jnp.int32)
    dest = (pstart[e_id] + rt[:, 4:6].astype(jnp.int32)).reshape(2 * n)

    xs = _dispatch(dest, h2.reshape(n, d // 2), n_blocks * ROW_BLOCK)
    ys = _experts(blk_e, n_used, xs, w_gate.astype(BF16), w_up.astype(BF16), w_down.astype(BF16))
    return _combine(dest, x1, mod, rt.reshape(bsz, s, ROUTE_COLS), ys)


def kernel(x, c, w_ada, b_ada, g_mix, w_in, g_q, g_k, rel_bias, w_br_moba, w_br_sb, w_out, g_ffn,
           w_rg, b_rg, w_re, b_re, w_gate, w_up, w_down):
    bsz, s, d = x.shape
    assert s % TOKEN_TILE == 0 and s % ATT_BLOCK == 0 and d % (2 * LANES) == 0
    for l in range(w_ada.shape[0]):
        mod = _adaln(c, w_ada[l], b_ada[l]).reshape(bsz, N_MOD, d)
        x = _layer(x, mod, g_mix[l], w_in[l], g_q[l], g_k[l], rel_bias, w_br_moba[l], w_br_sb[l], w_out[l],
                   g_ffn[l], w_rg[l], b_rg[l], w_re[l], b_re[l], w_gate[l], w_up[l], w_down[l])
    return x
```

```python
import functools
import math

import numpy as np
import jax
import jax.numpy as jnp
from jax import lax
from jax.experimental import pallas as pl
from jax.experimental.pallas import tpu as pltpu

F32 = jnp.float32
BF16 = jnp.bfloat16
U32 = jnp.uint32
HIGHEST = lax.Precision.HIGHEST

LANES = 128
VMEM_LIMIT_BYTES = 56 * 1024 * 1024

HEAD_DIM = 64
N_HEADS = 8
HEADS_PER_STEP = LANES // HEAD_DIM
ATT_BLOCK = 256
MOBA_TOPK = 3
REL_BUCKETS = 32
REL_MAX_DIST = 1024
N_GROUPS = 4
EXPERTS_PER_GROUP = 8
N_EXPERTS = N_GROUPS * EXPERTS_PER_GROUP
N_MOD = 6
RMS_EPS = 1e-6
ROW_BLOCK = 256
TOKEN_TILE = 512
NEG_INF = float("-inf")
MASK_LOGIT = -1e30
SB_EXIT_LOG = -110.0

_NT = (((1,), (1,)), ((), ()))


def _params(*semantics):
    return pltpu.CompilerParams(dimension_semantics=semantics, vmem_limit_bytes=VMEM_LIMIT_BYTES)


def _adaln_kernel(c_ref, w_ref, b_ref, o_ref):
    c = c_ref[...]
    s = c * jax.nn.sigmoid(c)
    o_ref[...] = jnp.dot(s, w_ref[...], preferred_element_type=F32, precision=HIGHEST) + b_ref[...]


def _adaln(c, w, b):
    bsz, d = c.shape
    n = w.shape[1]
    tn = 1536
    return pl.pallas_call(
        _adaln_kernel,
        grid=(n // tn,),
        in_specs=[
            pl.BlockSpec((bsz, d), lambda j: (0, 0)),
            pl.BlockSpec((d, tn), lambda j: (0, j)),
            pl.BlockSpec((1, tn), lambda j: (0, j)),
        ],
        out_specs=pl.BlockSpec((bsz, tn), lambda j: (0, j)),
        out_shape=jax.ShapeDtypeStruct((bsz, n), F32),
        compiler_params=_params("arbitrary"),
        name="adaln",
    )(c, w, b.reshape(1, n))


def _rms_modulate(x, g, shift, scale):
    y = x * lax.rsqrt(jnp.mean(x * x, axis=-1, keepdims=True) + RMS_EPS) * g
    return y * (1.0 + scale) + shift


def _inproj_kernel(x_ref, mod_ref, g_ref, w_ref, o_ref, *, col_chunk):
    h = _rms_modulate(x_ref[0], g_ref[...], mod_ref[0, 0:1, :], mod_ref[0, 1:2, :]).astype(BF16)
    for n in range(w_ref.shape[1] // col_chunk):
        cols = slice(n * col_chunk, (n + 1) * col_chunk)
        o_ref[0, :, cols] = jnp.dot(h, w_ref[:, cols], preferred_element_type=F32).astype(BF16)


def _inproj(x, mod, g, w_bf16):
    bsz, s, d = x.shape
    n = w_bf16.shape[1]
    tm = TOKEN_TILE
    return pl.pallas_call(
        functools.partial(_inproj_kernel, col_chunk=1024),
        grid=(bsz, s // tm),
        in_specs=[
            pl.BlockSpec((1, tm, d), lambda b, i: (b, i, 0)),
            pl.BlockSpec((1, N_MOD, d), lambda b, i: (b, 0, 0)),
            pl.BlockSpec((1, d), lambda b, i: (0, 0)),
            pl.BlockSpec((d, n), lambda b, i: (0, 0)),
        ],
        out_specs=pl.BlockSpec((1, tm, n), lambda b, i: (b, i, 0)),
        out_shape=jax.ShapeDtypeStruct((bsz, s, n), BF16),
        compiler_params=_params("arbitrary", "arbitrary"),
        name="inproj",
    )(x, mod, g.reshape(1, d), w_bf16)


def _indicator(cond, dtype):
    return jnp.where(cond, 1.0, 0.0).astype(dtype)


def _lane_index():
    return lax.broadcasted_iota(jnp.int32, (1, LANES), 1).astype(F32)


def _head_lane_mask():
    return lax.broadcasted_iota(jnp.int32, (1, LANES), 1) < HEAD_DIM


def _head_rmsnorm(t, g):
    first = _head_lane_mask()
    sq = t * t
    ss0 = jnp.sum(jnp.where(first, sq, 0.0), axis=-1, keepdims=True)
    ss1 = jnp.sum(jnp.where(first, 0.0, sq), axis=-1, keepdims=True)
    inv = jnp.where(first, lax.rsqrt(ss0 / HEAD_DIM + RMS_EPS), lax.rsqrt(ss1 / HEAD_DIM + RMS_EPS))
    return t * inv * g


def _split_heads(t):
    first = _head_lane_mask()
    zero = jnp.zeros_like(t)
    return jnp.where(first, t, zero), jnp.where(first, zero, t)


def _rel_bucket_table(max_dist):
    n = np.arange(max_dist + 1)
    max_exact = REL_BUCKETS // 2
    nf = np.maximum(n, 1).astype(np.float64)
    large = max_exact + (np.log(nf / max_exact) / math.log(REL_MAX_DIST / max_exact)
                         * (REL_BUCKETS - max_exact)).astype(np.int64)
    large = np.minimum(large, REL_BUCKETS - 1)
    return np.where(n < max_exact, n, large).astype(np.int32)


def _moba_kernel(q_ref, k_ref, v_ref, gq_ref, gk_ref, rb_ref, o_ref,
                 kaug_ref, kmean_ref, toep_ref, s_ref, mvec_ref, lvec_ref, acc_ref, *, n_blk):
    b = pl.program_id(1)
    qi = pl.program_id(2)
    blk = ATT_BLOCK
    lane = _lane_index()
    first = _head_lane_mask()
    tag_lane = lane - float(HEAD_DIM)
    qi_f = qi.astype(F32)

    @pl.when(jnp.logical_and(b == 0, qi == 0))
    def _build_bias_tiles():
        for h in range(HEADS_PER_STEP):
            for d in range(n_blk):
                r = rb_ref[0, h:h + 1, (n_blk - d) * blk:(n_blk - d + 2) * blk]
                rolled = pltpu.roll(jnp.broadcast_to(r, (blk, 2 * blk)), blk + 1, 1, stride=1, stride_axis=0)
                toep_ref[h, d] = rolled[:, :blk]

    @pl.when(qi == 0)
    def _prep_keys():
        kmean_ref[...] = jnp.zeros_like(kmean_ref)
        for j in range(n_blk):
            kn = _head_rmsnorm(k_ref[0, j * blk:(j + 1) * blk, :].astype(F32), gk_ref[...])
            tag = jnp.where(tag_lane == float(j), 1.0, 0.0)
            for h, kh in enumerate((kn, pltpu.roll(kn, HEAD_DIM, 1))):
                kaug_ref[h, j * blk:(j + 1) * blk, :] = jnp.where(first, kh, tag).astype(BF16)
                kmean_ref[h, HEAD_DIM + j:HEAD_DIM + j + 1, :] = jnp.where(
                    first, jnp.mean(kh, axis=0, keepdims=True), 0.0)

    qn = _head_rmsnorm(q_ref[0].astype(F32), gq_ref[...])
    q_aug = []
    for h, qh in enumerate((qn, pltpu.roll(qn, HEAD_DIM, 1))):
        qh = jnp.where(first, qh, 0.0)
        gate = lax.dot_general(qh, kmean_ref[h], _NT, preferred_element_type=F32, precision=HIGHEST)
        g = jnp.where(jnp.logical_and(tag_lane >= 0.0, tag_lane < qi_f), gate, NEG_INF)
        keep = jnp.where(tag_lane == qi_f, 1.0, jnp.zeros(g.shape, F32))
        for _ in range(MOBA_TOPK):
            top = jnp.max(g, axis=-1, keepdims=True)
            is_top = jnp.logical_and(g == top, top > NEG_INF)
            idx = jnp.min(jnp.where(is_top, lane, float(LANES)), axis=-1, keepdims=True)
            pick = lane == idx
            keep = jnp.where(pick, 1.0, keep)
            g = jnp.where(pick, NEG_INF, g)
        mask_logit = jnp.where(keep > 0.0, 0.0, MASK_LOGIT)
        q_aug.append(jnp.where(first, qh * (HEAD_DIM ** -0.5), mask_logit).astype(BF16))

    def lane_max(s):
        return jnp.maximum(s[:, :LANES], s[:, LANES:])

    def scores(h, j, n):
        k_t = kaug_ref[h, pl.ds(pl.multiple_of(j * blk, blk), n * blk), :]
        s = lax.dot_general(q_aug[h], k_t, _NT, preferred_element_type=F32)
        return [s[:, c * blk:(c + 1) * blk] + toep_ref[h, qi - j - c] for c in range(n)]

    row = lax.broadcasted_iota(jnp.int32, (blk, blk), 0)
    col = lax.broadcasted_iota(jnp.int32, (blk, blk), 1)
    for h in range(HEADS_PER_STEP):
        s = jnp.where(col <= row, scores(h, qi, 1)[0], NEG_INF)
        s_ref[h, qi] = s
        mvec_ref[h] = lane_max(s)

    def pass1(j, n):
        for h in range(HEADS_PER_STEP):
            tiles = scores(h, j, n)
            top = mvec_ref[h]
            for c, s in enumerate(tiles):
                s_ref[h, j + c] = s
                top = jnp.maximum(top, lane_max(s))
            mvec_ref[h] = top

    def pass1_pair(p, carry):
        pass1(2 * p, 2)
        return carry

    lax.fori_loop(0, lax.shift_right_logical(qi, 1), pass1_pair, 0)

    @pl.when((qi & 1) == 1)
    def _pass1_odd():
        pass1(qi - 1, 1)

    for h in range(HEADS_PER_STEP):
        m = jnp.max(mvec_ref[h], axis=-1, keepdims=True)
        mvec_ref[h] = jnp.broadcast_to(m, (blk, LANES))
        lvec_ref[h] = jnp.zeros((blk, LANES), F32)
        acc_ref[h] = jnp.zeros((blk, LANES), F32)

    def pass2(j, n):
        v_t = v_ref[0, pl.ds(pl.multiple_of(j * blk, blk), n * blk), :]
        for h in range(HEADS_PER_STEP):
            m = mvec_ref[h]
            parts = []
            for c in range(n):
                s = s_ref[h, j + c]
                parts += [jnp.exp(s[:, :LANES] - m), jnp.exp(s[:, LANES:] - m)]
            lvec_ref[h] += functools.reduce(lambda x, y: x + y, parts)
            p = jnp.concatenate(parts, axis=1).astype(BF16)
            acc_ref[h] += jnp.dot(p, v_t, preferred_element_type=F32)

    def pass2_pair(p, carry):
        pass2(2 * p, 2)
        return carry

    lax.fori_loop(0, lax.shift_right_logical(qi + 1, 1), pass2_pair, 0)

    @pl.when((qi & 1) == 0)
    def _pass2_odd():
        pass2(qi, 1)

    l0 = jnp.sum(lvec_ref[0], axis=-1, keepdims=True)
    l1 = jnp.sum(lvec_ref[1], axis=-1, keepdims=True)
    o_ref[0] = jnp.where(first, acc_ref[0] / l0, acc_ref[1] / l1).astype(o_ref.dtype)


def _moba(proj, g_q, g_k, rel_rev):
    bsz, s, _ = proj.shape
    blk = ATT_BLOCK
    n_blk = s // blk
    n_hp = N_HEADS // HEADS_PER_STEP
    w = N_HEADS * HEAD_DIM
    tile2 = lambda g: jnp.tile(g.reshape(1, HEAD_DIM), (1, HEADS_PER_STEP))
    return pl.pallas_call(
        functools.partial(_moba_kernel, n_blk=n_blk),
        grid=(n_hp, bsz, n_blk),
        in_specs=[
            pl.BlockSpec((1, blk, LANES), lambda hp, b, i: (b, i, hp)),
            pl.BlockSpec((1, s, LANES), lambda hp, b, i: (b, 0, n_hp + hp)),
            pl.BlockSpec((1, s, LANES), lambda hp, b, i: (b, 0, 2 * n_hp + hp)),
            pl.BlockSpec((1, LANES), lambda hp, b, i: (0, 0)),
            pl.BlockSpec((1, LANES), lambda hp, b, i: (0, 0)),
            pl.BlockSpec((1, HEADS_PER_STEP, rel_rev.shape[-1]), lambda hp, b, i: (hp, 0, 0)),
        ],
        out_specs=pl.BlockSpec((1, blk, LANES), lambda hp, b, i: (b, i, hp)),
        out_shape=jax.ShapeDtypeStruct((bsz, s, w), BF16),
        scratch_shapes=[
            pltpu.VMEM((HEADS_PER_STEP, s, LANES), BF16),
            pltpu.VMEM((HEADS_PER_STEP, LANES, LANES), F32),
            pltpu.VMEM((HEADS_PER_STEP, n_blk, blk, blk), F32),
            pltpu.VMEM((HEADS_PER_STEP, n_blk, blk, blk), F32),
            pltpu.VMEM((HEADS_PER_STEP, blk, LANES), F32),
            pltpu.VMEM((HEADS_PER_STEP, blk, LANES), F32),
            pltpu.VMEM((HEADS_PER_STEP, blk, LANES), F32),
        ],
        compiler_params=_params("arbitrary", "arbitrary", "arbitrary"),
        name="moba",
    )(proj, proj, proj, tile2(g_q), tile2(g_k), rel_rev)


def _sb_kernel(q_ref, k_ref, v_ref, o_ref, acc_ref, carry_ref):
    qi = pl.program_id(2)
    blk = ATT_BLOCK
    row = lax.broadcasted_iota(jnp.int32, (blk, blk), 0)
    col = lax.broadcasted_iota(jnp.int32, (blk, blk), 1)
    past = col < row
    later = _indicator(row > col, BF16)

    q_heads = _split_heads(q_ref[0].astype(F32) * (HEAD_DIM ** -0.5))
    q_heads = [t.astype(BF16) for t in q_heads]

    def walk(blocks, fresh):
        for h in range(HEADS_PER_STEP):
            carry = None if fresh else carry_ref[h]
            total = None if fresh else acc_ref[h]
            for j, diagonal in blocks:
                start = pl.multiple_of(j * blk, blk)
                z = lax.dot_general(q_heads[h], k_ref[0, pl.ds(start, blk), :], _NT, preferred_element_type=F32)
                log_1m = -(jnp.maximum(z, 0.0) + jnp.log(1.0 + jnp.exp(-jnp.abs(z))))
                masked = jnp.where(past, log_1m, 0.0) if diagonal else log_1m
                hi = masked.astype(BF16)
                lo = (masked - hi.astype(F32)).astype(BF16)
                after = (jnp.dot(hi, later, preferred_element_type=F32)
                         + jnp.dot(lo, later, preferred_element_type=F32))
                if carry is not None:
                    after = after + carry
                a = jnp.exp(z + log_1m + after)
                if diagonal:
                    a = jnp.where(past, a, 0.0)
                pv = jnp.dot(a.astype(BF16), v_ref[0, pl.ds(start, blk), :], preferred_element_type=F32)
                total = pv if total is None else total + pv
                row_sum = jnp.sum(masked, axis=-1, keepdims=True)
                carry = row_sum if carry is None else carry + row_sum
            carry_ref[h] = carry
            acc_ref[h] = total

    @pl.when(qi == 0)
    def _first_tile():
        walk([(qi, True)], True)

    @pl.when(qi > 0)
    def _own_and_previous():
        walk([(qi, True), (qi - 1, False)], True)

    def slowest_decay():
        return jnp.maximum(jnp.max(carry_ref[0]), jnp.max(carry_ref[1]))

    def cond(state):
        j, worst = state
        return jnp.logical_and(j >= 0, worst > SB_EXIT_LOG)

    def body(state):
        j, _ = state
        walk([(j, False)], False)
        return j - 1, slowest_decay()

    lax.while_loop(cond, body, (qi - 2, slowest_decay()))

    o_ref[0] = jnp.where(_head_lane_mask(), acc_ref[0], acc_ref[1]).astype(o_ref.dtype)


def _stick_breaking(proj):
    bsz, s, _ = proj.shape
    blk = ATT_BLOCK
    n_hp = N_HEADS // HEADS_PER_STEP
    w = N_HEADS * HEAD_DIM
    base = 3 * n_hp
    return pl.pallas_call(
        _sb_kernel,
        grid=(n_hp, bsz, s // blk),
        in_specs=[
            pl.BlockSpec((1, blk, LANES), lambda hp, b, i: (b, i, base + hp)),
            pl.BlockSpec((1, s, LANES), lambda hp, b, i: (b, 0, base + n_hp + hp)),
            pl.BlockSpec((1, s, LANES), lambda hp, b, i: (b, 0, base + 2 * n_hp + hp)),
        ],
        out_specs=pl.BlockSpec((1, blk, LANES), lambda hp, b, i: (b, i, hp)),
        out_shape=jax.ShapeDtypeStruct((bsz, s, w), BF16),
        scratch_shapes=[
            pltpu.VMEM((HEADS_PER_STEP, blk, LANES), F32),
            pltpu.VMEM((HEADS_PER_STEP, blk, 1), F32),
        ],
        compiler_params=_params("arbitrary", "arbitrary", "arbitrary"),
        name="stickbrk",
    )(proj, proj, proj)


def _pack_rows(t):
    n = t.shape[1] // 2
    bits = pltpu.bitcast(t.astype(BF16).astype(F32), U32)
    return bits[:, :n] | (bits[:, n:] >> 16)


def _unpack_rows(p):
    hi = pltpu.bitcast(p & jnp.uint32(0xFFFF0000), F32)
    lo = pltpu.bitcast(p << 16, F32)
    return jnp.concatenate([hi, lo], axis=1)


def _merge_kernel(x_ref, oa_ref, ob_ref, ga_ref, gb_ref, mod_ref, wbm_ref, wbs_ref, wo_ref, gf_ref,
                  wr_ref, br_ref, x1_ref, h2_ref, lg_ref):
    ma = jnp.dot(oa_ref[0], wbm_ref[...], preferred_element_type=F32)
    mb = jnp.dot(ob_ref[0], wbs_ref[...], preferred_element_type=F32)
    merged = (jax.nn.sigmoid(ga_ref[0].astype(F32)) * ma + jax.nn.sigmoid(gb_ref[0].astype(F32)) * mb)
    t = jnp.dot(merged.astype(BF16), wo_ref[...], preferred_element_type=F32)
    x1 = x_ref[0] + mod_ref[0, 2:3, :] * t
    x1_ref[0] = x1
    h2 = _rms_modulate(x1, gf_ref[...], mod_ref[0, 3:4, :], mod_ref[0, 4:5, :])
    h2_ref[0] = _pack_rows(h2)
    lg_ref[0] = jnp.dot(h2, wr_ref[...], preferred_element_type=F32, precision=HIGHEST) + br_ref[...]


def _merge(x, o_a, o_b, proj, mod, wbm, wbs, wo, g_ffn, w_router, b_router):
    bsz, s, d = x.shape
    w = o_a.shape[-1]
    tm = TOKEN_TILE
    gate_blk = (proj.shape[-1] - 2 * d) // d
    const = lambda shape: pl.BlockSpec(shape, lambda b, i: (0,) * len(shape))
    tok = lambda width: pl.BlockSpec((1, tm, width), lambda b, i: (b, i, 0))
    return pl.pallas_call(
        _merge_kernel,
        grid=(bsz, s // tm),
        in_specs=[
            tok(d), tok(w), tok(w),
            pl.BlockSpec((1, tm, d), lambda b, i: (b, i, gate_blk)),
            pl.BlockSpec((1, tm, d), lambda b, i: (b, i, gate_blk + 1)),
            pl.BlockSpec((1, N_MOD, d), lambda b, i: (b, 0, 0)),
            const((w, d)), const((w, d)), const((d, d)), const((1, d)),
            const((d, LANES)), const((1, LANES)),
        ],
        out_specs=[tok(d), tok(d // 2), tok(LANES)],
        out_shape=[
            jax.ShapeDtypeStruct((bsz, s, d), F32),
            jax.ShapeDtypeStruct((bsz, s, d // 2), U32),
            jax.ShapeDtypeStruct((bsz, s, LANES), F32),
        ],
        compiler_params=_params("arbitrary", "arbitrary"),
        name="merge",
    )(x, o_a, o_b, proj, proj, mod, wbm, wbs, wo, g_ffn.reshape(1, d), w_router, b_router)


ROUTE_COLS = 8


def _route_kernel(lg_ref, rt_ref, cnt_ref, carry_ref):
    i = pl.program_id(0)
    tm = lg_ref.shape[0]
    lane = _lane_index()

    @pl.when(i == 0)
    def _init():
        carry_ref[...] = jnp.zeros_like(carry_ref)

    lg = lg_ref[...]

    def first_max(vals):
        top = jnp.max(vals, axis=-1, keepdims=True)
        idx = jnp.min(jnp.where(vals == top, lane, float(LANES)), axis=-1, keepdims=True)
        return top, idx

    g_logit = jnp.where(lane < N_GROUPS, lg, NEG_INF)
    g_top, g_idx = first_max(g_logit)
    g_w = 1.0 / jnp.sum(jnp.exp(g_logit - g_top), axis=-1, keepdims=True)
    lo = N_GROUPS + g_idx * EXPERTS_PER_GROUP
    e_logit = jnp.where(jnp.logical_and(lane >= lo, lane < lo + EXPERTS_PER_GROUP), lg, NEG_INF)
    e_top0, lane0 = first_max(e_logit)
    e_top1, lane1 = first_max(jnp.where(lane == lane0, NEG_INF, e_logit))
    z = jnp.exp(e_top1 - e_top0)
    w0 = g_w / (1.0 + z)
    w1 = g_w * z / (1.0 + z)

    hot = jnp.logical_or(lane == lane0, lane == lane1)
    row = lax.broadcasted_iota(jnp.int32, (tm, tm), 0)
    col = lax.broadcasted_iota(jnp.int32, (tm, tm), 1)
    before = jnp.dot(_indicator(col < row, BF16), _indicator(hot, BF16), preferred_element_type=F32) + carry_ref[...]
    rank0 = jnp.sum(jnp.where(lane == lane0, before, 0.0), axis=-1, keepdims=True)
    rank1 = jnp.sum(jnp.where(lane == lane1, before, 0.0), axis=-1, keepdims=True)
    carry_ref[...] += jnp.sum(_indicator(hot, F32), axis=0, keepdims=True)
    cnt_ref[...] = carry_ref[...]

    cols = [lane0 - N_GROUPS, lane1 - N_GROUPS, w0, w1, rank0, rank1]
    packed = jnp.zeros((tm, LANES), F32)
    for c, v in enumerate(cols):
        packed = jnp.where(lane == float(c), v, packed)
    rt_ref[...] = packed[:, :ROUTE_COLS]


def _route(logits):
    n = logits.shape[0]
    tm = TOKEN_TILE
    return pl.pallas_call(
        _route_kernel,
        grid=(n // tm,),
        in_specs=[pl.BlockSpec((tm, LANES), lambda i: (i, 0))],
        out_specs=[pl.BlockSpec((tm, ROUTE_COLS), lambda i: (i, 0)), pl.BlockSpec((1, LANES), lambda i: (0, 0))],
        out_shape=[jax.ShapeDtypeStruct((n, ROUTE_COLS), F32), jax.ShapeDtypeStruct((1, LANES), F32)],
        scratch_shapes=[pltpu.VMEM((1, LANES), F32)],
        compiler_params=_params("arbitrary"),
        name="route",
    )(logits)


def _dispatch_kernel(dest_ref, h_ref, xs_in_ref, xs_ref, sem):
    del xs_in_ref
    tm = h_ref.shape[0]

    def row_copy(t, k):
        return pltpu.make_async_copy(h_ref.at[pl.ds(t, 1), :], xs_ref.at[pl.ds(dest_ref[2 * t + k], 1), :], sem)

    def start(t, c):
        row_copy(t, 0).start()
        row_copy(t, 1).start()
        return c

    def wait(t, c):
        row_copy(t, 0).wait()
        row_copy(t, 1).wait()
        return c

    lax.fori_loop(0, tm, start, 0)
    lax.fori_loop(0, tm, wait, 0)


def _dispatch(dest_flat, h2_packed, n_rows):
    n, half = h2_packed.shape
    tm = TOKEN_TILE
    xs0 = jnp.zeros((n_rows, half), U32)
    return pl.pallas_call(
        _dispatch_kernel,
        grid=(n // tm,),
        in_specs=[
            pl.BlockSpec((2 * tm,), lambda i: (i,), memory_space=pltpu.SMEM),
            pl.BlockSpec((tm, half), lambda i: (i, 0)),
            pl.BlockSpec(memory_space=pl.ANY),
        ],
        out_specs=pl.BlockSpec(memory_space=pl.ANY),
        out_shape=jax.ShapeDtypeStruct((n_rows, half), U32),
        scratch_shapes=[pltpu.SemaphoreType.DMA(())],
        input_output_aliases={2: 0},
        compiler_params=_params("arbitrary"),
        name="dispatch",
    )(dest_flat, h2_packed, xs0)


def _expert_kernel(blk_e_ref, n_used_ref, xs_ref, wg_ref, wu_ref, wd_ref, ys_ref):
    del blk_e_ref
    i = pl.program_id(0)

    @pl.when(i < n_used_ref[0])
    def _compute():
        xb = _unpack_rows(xs_ref[...]).astype(BF16)
        g = jnp.dot(xb, wg_ref[0], preferred_element_type=F32)
        u = jnp.dot(xb, wu_ref[0], preferred_element_type=F32)
        hid = (g * jax.nn.sigmoid(g) * u).astype(BF16)
        ys_ref[...] = _pack_rows(jnp.dot(hid, wd_ref[0], preferred_element_type=F32))

    @pl.when(i >= n_used_ref[0])
    def _unused():
        ys_ref[...] = jnp.zeros_like(ys_ref)


def _experts(blk_e, n_used, xs, wg, wu, wd):
    n_rows, half = xs.shape
    _, d, de = wg.shape
    grid_spec = pltpu.PrefetchScalarGridSpec(
        num_scalar_prefetch=2,
        grid=(n_rows // ROW_BLOCK,),
        in_specs=[
            pl.BlockSpec((ROW_BLOCK, half), lambda i, e, u: (i, 0)),
            pl.BlockSpec((1, d, de), lambda i, e, u: (e[i], 0, 0)),
            pl.BlockSpec((1, d, de), lambda i, e, u: (e[i], 0, 0)),
            pl.BlockSpec((1, de, d), lambda i, e, u: (e[i], 0, 0)),
        ],
        out_specs=pl.BlockSpec((ROW_BLOCK, half), lambda i, e, u: (i, 0)),
    )
    return pl.pallas_call(
        _expert_kernel,
        grid_spec=grid_spec,
        out_shape=jax.ShapeDtypeStruct((n_rows, half), U32),
        compiler_params=_params("arbitrary"),
        name="experts",
    )(blk_e, n_used, xs, wg, wu, wd)


def _combine_kernel(dest_ref, x1_ref, mod_ref, rt_ref, ys_ref, o_ref, buf_ref, sem):
    tm = x1_ref.shape[1]

    def row_copy(t, k):
        return pltpu.make_async_copy(ys_ref.at[pl.ds(dest_ref[2 * t + k], 1), :],
                                     buf_ref.at[k, pl.ds(t, 1), :], sem)

    def start(t, c):
        row_copy(t, 0).start()
        row_copy(t, 1).start()
        return c

    def wait(t, c):
        row_copy(t, 0).wait()
        row_copy(t, 1).wait()
        return c

    lax.fori_loop(0, tm, start, 0)
    lax.fori_loop(0, tm, wait, 0)
    rt = rt_ref[0]
    moe = rt[:, 2:3] * _unpack_rows(buf_ref[0]) + rt[:, 3:4] * _unpack_rows(buf_ref[1])
    o_ref[0] = x1_ref[0] + mod_ref[0, 5:6, :] * moe


def _combine(dest_flat, x1, mod, rt, ys):
    bsz, s, d = x1.shape
    tm = TOKEN_TILE
    per_b = s // tm
    return pl.pallas_call(
        _combine_kernel,
        grid=(bsz, per_b),
        in_specs=[
            pl.BlockSpec((2 * tm,), lambda b, i: (b * per_b + i,), memory_space=pltpu.SMEM),
            pl.BlockSpec((1, tm, d), lambda b, i: (b, i, 0)),
            pl.BlockSpec((1, N_MOD, d), lambda b, i: (b, 0, 0)),
            pl.BlockSpec((1, tm, ROUTE_COLS), lambda b, i: (b, i, 0)),
            pl.BlockSpec(memory_space=pl.ANY),
        ],
        out_specs=pl.BlockSpec((1, tm, d), lambda b, i: (b, i, 0)),
        out_shape=jax.ShapeDtypeStruct((bsz, s, d), F32),
        scratch_shapes=[pltpu.VMEM((2, tm, d // 2), U32), pltpu.SemaphoreType.DMA(())],
        compiler_params=_params("arbitrary", "arbitrary"),
        name="combine",
    )(dest_flat, x1, mod, rt, ys)


def _layer(x, mod, g_mix, w_in, g_q, g_k, rel_bias, w_br_moba, w_br_sb, w_out, g_ffn,
           w_rg, b_rg, w_re, b_re, w_gate, w_up, w_down):
    bsz, s, d = x.shape
    n = bsz * s
    n_blk = s // ATT_BLOCK

    proj = _inproj(x, mod, g_mix, w_in.astype(BF16))

    top = (n_blk + 1) * ATT_BLOCK - 1
    dist = np.maximum(top - np.arange((n_blk + 2) * ATT_BLOCK), 0)
    rel_rev = rel_bias[:, _rel_bucket_table(top)[dist]].reshape(N_HEADS // HEADS_PER_STEP, HEADS_PER_STEP, -1)

    o_a = _moba(proj, g_q, g_k, rel_rev)
    o_b = _stick_breaking(proj)

    pad = LANES - N_GROUPS - N_EXPERTS
    w_router = jnp.concatenate([w_rg, w_re, jnp.zeros((d, pad), F32)], axis=1)
    b_router = jnp.concatenate([b_rg, b_re, jnp.zeros((pad,), F32)]).reshape(1, LANES)
    x1, h2, logits = _merge(x, o_a, o_b, proj, mod, w_br_moba.astype(BF16), w_br_sb.astype(BF16),
                            w_out.astype(BF16), g_ffn, w_router, b_router)

    rt, counts = _route(logits.reshape(n, LANES))

    cnt = counts[0, N_GROUPS:N_GROUPS + N_EXPERTS].astype(jnp.int32)
    padded = (cnt + ROW_BLOCK - 1) // ROW_BLOCK * ROW_BLOCK
    pend = jnp.cumsum(padded)
    pstart = pend - padded
    n_blocks = (n * 2) // ROW_BLOCK + N_EXPERTS
    blk_row0 = jnp.arange(n_blocks, dtype=jnp.int32) * ROW_BLOCK
    blk_e = jnp.minimum(jnp.sum((blk_row0[:, None] >= pend[None, :]).astype(jnp.int32), axis=1), N_EXPERTS - 1)
    n_used = (pend[-1] // ROW_BLOCK).astype(jnp.int32).reshape(1)
    e_id = rt[:, 0:2].astype(jnp.int32)
    e_start = jnp.sum(jnp.where(e_id[:, :, None] == jnp.arange(N_EXPERTS, dtype=jnp.int32), pstart, 0), axis=-1)
    dest = (e_start + rt[:, 4:6].astype(jnp.int32)).reshape(2 * n)

    xs = _dispatch(dest, h2.reshape(n, d // 2), n_blocks * ROW_BLOCK)
    ys = _experts(blk_e, n_used, xs, w_gate.astype(BF16), w_up.astype(BF16), w_down.astype(BF16))
    return _combine(dest, x1, mod, rt.reshape(bsz, s, ROUTE_COLS), ys)


def kernel(x, c, w_ada, b_ada, g_mix, w_in, g_q, g_k, rel_bias, w_br_moba, w_br_sb, w_out, g_ffn,
           w_rg, b_rg, w_re, b_re, w_gate, w_up, w_down):
    bsz, s, d = x.shape
    assert s % TOKEN_TILE == 0 and s % ATT_BLOCK == 0 and d % (2 * LANES) == 0
    for l in range(w_ada.shape[0]):
        mod = _adaln(c, w_ada[l], b_ada[l]).reshape(bsz, N_MOD, d)
        x = _layer(x, mod, g_mix[l], w_in[l], g_q[l], g_k[l], rel_bias, w_br_moba[l], w_br_sb[l], w_out[l],
                   g_ffn[l], w_rg[l], b_rg[l], w_re[l], b_re[l], w_gate[l], w_up[l], w_down[l])
    return x
```

```python
import functools
import math

import numpy as np
import jax
import jax.numpy as jnp
from jax import lax
from jax.experimental import pallas as pl
from jax.experimental.pallas import tpu as pltpu

F32 = jnp.float32
BF16 = jnp.bfloat16
U32 = jnp.uint32
HIGHEST = lax.Precision.HIGHEST

LANES = 128
VMEM_LIMIT_BYTES = 56 * 1024 * 1024

HEAD_DIM = 64
N_HEADS = 8
HEADS_PER_STEP = LANES // HEAD_DIM
ATT_BLOCK = 256
MOBA_Q_TILE = 2 * ATT_BLOCK
MOBA_TOPK = 3
REL_BUCKETS = 32
REL_MAX_DIST = 1024
N_GROUPS = 4
EXPERTS_PER_GROUP = 8
N_EXPERTS = N_GROUPS * EXPERTS_PER_GROUP
N_MOD = 6
RMS_EPS = 1e-6
ROW_BLOCK = 256
TOKEN_TILE = 512
NEG_INF = float("-inf")
MASK_LOGIT = -1e30
SB_EXIT_LOG = -110.0

_NT = (((1,), (1,)), ((), ()))


def _params(*semantics):
    return pltpu.CompilerParams(dimension_semantics=semantics, vmem_limit_bytes=VMEM_LIMIT_BYTES)


def _adaln_kernel(c_ref, w_ref, b_ref, o_ref):
    c = c_ref[...]
    s = c * jax.nn.sigmoid(c)
    o_ref[...] = jnp.dot(s, w_ref[...], preferred_element_type=F32, precision=HIGHEST) + b_ref[...]


def _adaln(c, w, b):
    bsz, d = c.shape
    n = w.shape[1]
    tn = 1536
    return pl.pallas_call(
        _adaln_kernel,
        grid=(n // tn,),
        in_specs=[
            pl.BlockSpec((bsz, d), lambda j: (0, 0)),
            pl.BlockSpec((d, tn), lambda j: (0, j)),
            pl.BlockSpec((1, tn), lambda j: (0, j)),
        ],
        out_specs=pl.BlockSpec((bsz, tn), lambda j: (0, j)),
        out_shape=jax.ShapeDtypeStruct((bsz, n), F32),
        compiler_params=_params("arbitrary"),
        name="adaln",
    )(c, w, b.reshape(1, n))


def _rms_modulate(x, g, shift, scale):
    y = x * lax.rsqrt(jnp.mean(x * x, axis=-1, keepdims=True) + RMS_EPS) * g
    return y * (1.0 + scale) + shift


def _inproj_kernel(x_ref, mod_ref, g_ref, w_ref, o_ref, *, col_chunk):
    h = _rms_modulate(x_ref[0], g_ref[...], mod_ref[0, 0:1, :], mod_ref[0, 1:2, :]).astype(BF16)
    for n in range(w_ref.shape[1] // col_chunk):
        cols = slice(n * col_chunk, (n + 1) * col_chunk)
        o_ref[0, :, cols] = jnp.dot(h, w_ref[:, cols], preferred_element_type=F32).astype(BF16)


def _inproj(x, mod, g, w_bf16):
    bsz, s, d = x.shape
    n = w_bf16.shape[1]
    tm = TOKEN_TILE
    return pl.pallas_call(
        functools.partial(_inproj_kernel, col_chunk=1024),
        grid=(bsz, s // tm),
        in_specs=[
            pl.BlockSpec((1, tm, d), lambda b, i: (b, i, 0)),
            pl.BlockSpec((1, N_MOD, d), lambda b, i: (b, 0, 0)),
            pl.BlockSpec((1, d), lambda b, i: (0, 0)),
            pl.BlockSpec((d, n), lambda b, i: (0, 0)),
        ],
        out_specs=pl.BlockSpec((1, tm, n), lambda b, i: (b, i, 0)),
        out_shape=jax.ShapeDtypeStruct((bsz, s, n), BF16),
        compiler_params=_params("arbitrary", "arbitrary"),
        name="inproj",
    )(x, mod, g.reshape(1, d), w_bf16)


def _indicator(cond, dtype):
    return jnp.where(cond, 1.0, 0.0).astype(dtype)


def _lane_index():
    return lax.broadcasted_iota(jnp.int32, (1, LANES), 1).astype(F32)


def _head_lane_mask():
    return lax.broadcasted_iota(jnp.int32, (1, LANES), 1) < HEAD_DIM


def _head_rmsnorm(t, g):
    first = _head_lane_mask()
    sq = t * t
    ss0 = jnp.sum(jnp.where(first, sq, 0.0), axis=-1, keepdims=True)
    ss1 = jnp.sum(jnp.where(first, 0.0, sq), axis=-1, keepdims=True)
    inv = jnp.where(first, lax.rsqrt(ss0 / HEAD_DIM + RMS_EPS), lax.rsqrt(ss1 / HEAD_DIM + RMS_EPS))
    return t * inv * g


def _split_heads(t):
    first = _head_lane_mask()
    zero = jnp.zeros_like(t)
    return jnp.where(first, t, zero), jnp.where(first, zero, t)


def _rel_bucket_table(max_dist):
    n = np.arange(max_dist + 1)
    max_exact = REL_BUCKETS // 2
    nf = np.maximum(n, 1).astype(np.float64)
    large = max_exact + (np.log(nf / max_exact) / math.log(REL_MAX_DIST / max_exact)
                         * (REL_BUCKETS - max_exact)).astype(np.int64)
    large = np.minimum(large, REL_BUCKETS - 1)
    return np.where(n < max_exact, n, large).astype(np.int32)


def _moba_kernel(q_ref, k_ref, v_ref, gq_ref, gk_ref, rb_ref, o_ref,
                 kaug_ref, kmean_ref, toep_ref, s_ref, mvec_ref, lvec_ref, acc_ref, *, n_blk):
    b = pl.program_id(1)
    qi = pl.program_id(2)
    blk = ATT_BLOCK
    qt = q_ref.shape[1]
    halves = qt // blk
    lane = _lane_index()
    first = _head_lane_mask()
    tag_lane = lane - float(HEAD_DIM)

    @pl.when(jnp.logical_and(b == 0, qi == 0))
    def _build_bias_tiles():
        for h in range(HEADS_PER_STEP):
            for d in range(n_blk):
                r = rb_ref[0, h:h + 1, (n_blk - d) * blk:(n_blk - d + 2) * blk]
                rolled = pltpu.roll(jnp.broadcast_to(r, (blk, 2 * blk)), blk + 1, 1, stride=1, stride_axis=0)
                toep_ref[h, d] = rolled[:, :blk]

    @pl.when(qi == 0)
    def _prep_keys():
        kmean_ref[...] = jnp.zeros_like(kmean_ref)
        for j in range(n_blk):
            kn = _head_rmsnorm(k_ref[0, j * blk:(j + 1) * blk, :].astype(F32), gk_ref[...])
            tag = jnp.where(tag_lane == float(j), 1.0, 0.0)
            for h, kh in enumerate((kn, pltpu.roll(kn, HEAD_DIM, 1))):
                kaug_ref[h, j * blk:(j + 1) * blk, :] = jnp.where(first, kh, tag).astype(BF16)
                kmean_ref[h, j:j + 1, :] = jnp.where(first, jnp.mean(kh, axis=0, keepdims=True), 0.0)

    qn = _head_rmsnorm(q_ref[0].astype(F32), gq_ref[...])
    gate_rows = kmean_ref.shape[1]
    blk_row = lax.broadcasted_iota(jnp.int32, (gate_rows, qt), 0).astype(F32)
    own = (halves * qi + lax.shift_right_logical(lax.broadcasted_iota(jnp.int32, (gate_rows, qt), 1),
                                                 blk.bit_length() - 1)).astype(F32)
    q_aug = []
    for h, qh in enumerate((qn, pltpu.roll(qn, HEAD_DIM, 1))):
        qh = jnp.where(first, qh, 0.0)
        g = lax.dot_general(kmean_ref[h], qh, _NT, preferred_element_type=F32, precision=HIGHEST)
        g = jnp.where(blk_row < own, g, NEG_INF)
        keep = jnp.where(blk_row == own, 1.0, 0.0)
        for _ in range(MOBA_TOPK):
            top = jnp.max(g, axis=0, keepdims=True)
            is_top = jnp.logical_and(g == top, top > NEG_INF)
            idx = jnp.min(jnp.where(is_top, blk_row, float(gate_rows)), axis=0, keepdims=True)
            pick = blk_row == idx
            keep = jnp.where(pick, 1.0, keep)
            g = jnp.where(pick, NEG_INF, g)
        mask_t = jnp.where(keep > 0.0, 0.0, MASK_LOGIT)
        mask_logit = jnp.concatenate([jnp.zeros((HEAD_DIM, qt), F32), mask_t,
                                      jnp.zeros((LANES - HEAD_DIM - gate_rows, qt), F32)], axis=0).T
        q_aug.append(jnp.where(first, qh * (HEAD_DIM ** -0.5), mask_logit).astype(BF16))

    def lane_max(s):
        return jnp.maximum(s[:, :LANES], s[:, LANES:])

    def scores(h, t, dist):
        k_t = kaug_ref[h, pl.ds(pl.multiple_of(t * qt, qt), qt), :]
        s = lax.dot_general(q_aug[h], k_t, _NT, preferred_element_type=F32)
        tiles = []
        for kb in range(halves):
            rows = []
            for qb in range(halves):
                part = s[qb * blk:(qb + 1) * blk, kb * blk:(kb + 1) * blk]
                if isinstance(dist, int) and halves * dist + qb - kb < 0:
                    rows.append(part)
                else:
                    rows.append(part + toep_ref[h, halves * dist + qb - kb])
            tiles.append(jnp.concatenate(rows, axis=0))
        return tiles

    row = lax.broadcasted_iota(jnp.int32, (qt, blk), 0)
    col = lax.broadcasted_iota(jnp.int32, (qt, blk), 1)
    for h in range(HEADS_PER_STEP):
        top = None
        for kb, s in enumerate(scores(h, qi, 0)):
            s = jnp.where(col + kb * blk <= row, s, NEG_INF)
            s_ref[h, halves * qi + kb] = s
            top = lane_max(s) if top is None else jnp.maximum(top, lane_max(s))
        mvec_ref[h] = top

    def pass1(t, carry):
        for h in range(HEADS_PER_STEP):
            top = mvec_ref[h]
            for kb, s in enumerate(scores(h, t, qi - t)):
                s_ref[h, halves * t + kb] = s
                top = jnp.maximum(top, lane_max(s))
            mvec_ref[h] = top
        return carry

    lax.fori_loop(0, qi, pass1, 0)

    for h in range(HEADS_PER_STEP):
        m = jnp.max(mvec_ref[h], axis=-1, keepdims=True)
        mvec_ref[h] = jnp.broadcast_to(m, (qt, LANES))
        lvec_ref[h] = jnp.zeros((qt, LANES), F32)
        acc_ref[h] = jnp.zeros((qt, LANES), F32)

    def pass2(t, carry):
        v_t = v_ref[0, pl.ds(pl.multiple_of(t * qt, qt), qt), :]
        for h in range(HEADS_PER_STEP):
            m = mvec_ref[h]
            parts = []
            for kb in range(halves):
                s = s_ref[h, halves * t + kb]
                parts += [jnp.exp(s[:, :LANES] - m), jnp.exp(s[:, LANES:] - m)]
            lvec_ref[h] += functools.reduce(lambda x, y: x + y, parts)
            p = jnp.concatenate(parts, axis=1).astype(BF16)
            acc_ref[h] += jnp.dot(p, v_t, preferred_element_type=F32)
        return carry

    lax.fori_loop(0, qi + 1, pass2, 0)

    l0 = jnp.sum(lvec_ref[0], axis=-1, keepdims=True)
    l1 = jnp.sum(lvec_ref[1], axis=-1, keepdims=True)
    o_ref[0] = jnp.where(first, acc_ref[0] / l0, acc_ref[1] / l1).astype(o_ref.dtype)


def _moba(proj, g_q, g_k, rel_rev):
    bsz, s, _ = proj.shape
    blk = ATT_BLOCK
    n_blk = s // blk
    n_hp = N_HEADS // HEADS_PER_STEP
    w = N_HEADS * HEAD_DIM
    tile2 = lambda g: jnp.tile(g.reshape(1, HEAD_DIM), (1, HEADS_PER_STEP))
    qt = MOBA_Q_TILE
    return pl.pallas_call(
        functools.partial(_moba_kernel, n_blk=n_blk),
        grid=(n_hp, bsz, s // qt),
        in_specs=[
            pl.BlockSpec((1, qt, LANES), lambda hp, b, i: (b, i, hp)),
            pl.BlockSpec((1, s, LANES), lambda hp, b, i: (b, 0, n_hp + hp)),
            pl.BlockSpec((1, s, LANES), lambda hp, b, i: (b, 0, 2 * n_hp + hp)),
            pl.BlockSpec((1, LANES), lambda hp, b, i: (0, 0)),
            pl.BlockSpec((1, LANES), lambda hp, b, i: (0, 0)),
            pl.BlockSpec((1, HEADS_PER_STEP, rel_rev.shape[-1]), lambda hp, b, i: (hp, 0, 0)),
        ],
        out_specs=pl.BlockSpec((1, qt, LANES), lambda hp, b, i: (b, i, hp)),
        out_shape=jax.ShapeDtypeStruct((bsz, s, w), BF16),
        scratch_shapes=[
            pltpu.VMEM((HEADS_PER_STEP, s, LANES), BF16),
            pltpu.VMEM((HEADS_PER_STEP, -(-n_blk // 8) * 8, LANES), F32),
            pltpu.VMEM((HEADS_PER_STEP, n_blk, blk, blk), F32),
            pltpu.VMEM((HEADS_PER_STEP, n_blk, qt, blk), F32),
            pltpu.VMEM((HEADS_PER_STEP, qt, LANES), F32),
            pltpu.VMEM((HEADS_PER_STEP, qt, LANES), F32),
            pltpu.VMEM((HEADS_PER_STEP, qt, LANES), F32),
        ],
        compiler_params=_params("arbitrary", "arbitrary", "arbitrary"),
        name="moba",
    )(proj, proj, proj, tile2(g_q), tile2(g_k), rel_rev)


def _sb_kernel(q_ref, k_ref, v_ref, o_ref, acc_ref, carry_ref):
    qi = pl.program_id(2)
    blk = ATT_BLOCK
    row = lax.broadcasted_iota(jnp.int32, (blk, blk), 0)
    col = lax.broadcasted_iota(jnp.int32, (blk, blk), 1)
    past = col < row
    later = _indicator(row > col, BF16)

    q_heads = _split_heads(q_ref[0].astype(F32) * (HEAD_DIM ** -0.5))
    q_heads = [t.astype(BF16) for t in q_heads]

    def walk(blocks, fresh):
        for h in range(HEADS_PER_STEP):
            carry = None if fresh else carry_ref[h]
            total = None if fresh else acc_ref[h]
            for j, diagonal in blocks:
                start = pl.multiple_of(j * blk, blk)
                z = lax.dot_general(q_heads[h], k_ref[0, pl.ds(start, blk), :], _NT, preferred_element_type=F32)
                log_1m = -(jnp.maximum(z, 0.0) + jnp.log(1.0 + jnp.exp(-jnp.abs(z))))
                masked = jnp.where(past, log_1m, 0.0) if diagonal else log_1m
                hi = masked.astype(BF16)
                lo = (masked - hi.astype(F32)).astype(BF16)
                after = (jnp.dot(hi, later, preferred_element_type=F32)
                         + jnp.dot(lo, later, preferred_element_type=F32))
                if carry is not None:
                    after = after + carry
                a = jnp.exp(z + log_1m + after)
                if diagonal:
                    a = jnp.where(past, a, 0.0)
                pv = jnp.dot(a.astype(BF16), v_ref[0, pl.ds(start, blk), :], preferred_element_type=F32)
                total = pv if total is None else total + pv
                row_sum = jnp.sum(masked, axis=-1, keepdims=True)
                carry = row_sum if carry is None else carry + row_sum
            carry_ref[h] = carry
            acc_ref[h] = total

    @pl.when(qi == 0)
    def _first_tile():
        walk([(qi, True)], True)

    @pl.when(qi > 0)
    def _own_and_previous():
        walk([(qi, True), (qi - 1, False)], True)

    def slowest_decay():
        return jnp.maximum(jnp.max(carry_ref[0]), jnp.max(carry_ref[1]))

    def cond(state):
        j, worst = state
        return jnp.logical_and(j >= 0, worst > SB_EXIT_LOG)

    def body(state):
        j, _ = state
        walk([(j, False)], False)
        return j - 1, slowest_decay()

    lax.while_loop(cond, body, (qi - 2, slowest_decay()))

    o_ref[0] = jnp.where(_head_lane_mask(), acc_ref[0], acc_ref[1]).astype(o_ref.dtype)


def _stick_breaking(proj):
    bsz, s, _ = proj.shape
    blk = ATT_BLOCK
    n_hp = N_HEADS // HEADS_PER_STEP
    w = N_HEADS * HEAD_DIM
    base = 3 * n_hp
    return pl.pallas_call(
        _sb_kernel,
        grid=(n_hp, bsz, s // blk),
        in_specs=[
            pl.BlockSpec((1, blk, LANES), lambda hp, b, i: (b, i, base + hp)),
            pl.BlockSpec((1, s, LANES), lambda hp, b, i: (b, 0, base + n_hp + hp)),
            pl.BlockSpec((1, s, LANES), lambda hp, b, i: (b, 0, base + 2 * n_hp + hp)),
        ],
        out_specs=pl.BlockSpec((1, blk, LANES), lambda hp, b, i: (b, i, hp)),
        out_shape=jax.ShapeDtypeStruct((bsz, s, w), BF16),
        scratch_shapes=[
            pltpu.VMEM((HEADS_PER_STEP, blk, LANES), F32),
            pltpu.VMEM((HEADS_PER_STEP, blk, 1), F32),
        ],
        compiler_params=_params("arbitrary", "arbitrary", "arbitrary"),
        name="stickbrk",
    )(proj, proj, proj)


def _pack_rows(t):
    n = t.shape[1] // 2
    bits = pltpu.bitcast(t.astype(BF16).astype(F32), U32)
    return bits[:, :n] | (bits[:, n:] >> 16)


def _unpack_rows(p):
    hi = pltpu.bitcast(p & jnp.uint32(0xFFFF0000), F32)
    lo = pltpu.bitcast(p << 16, F32)
    return jnp.concatenate([hi, lo], axis=1)


def _merge_kernel(x_ref, oa_ref, ob_ref, ga_ref, gb_ref, mod_ref, wbm_ref, wbs_ref, wo_ref, gf_ref,
                  wrh_ref, wrl_ref, br_ref, x1_ref, h2_ref, lg_ref):
    ma = jnp.dot(oa_ref[0], wbm_ref[...], preferred_element_type=F32)
    mb = jnp.dot(ob_ref[0], wbs_ref[...], preferred_element_type=F32)
    merged = (jax.nn.sigmoid(ga_ref[0].astype(F32)) * ma + jax.nn.sigmoid(gb_ref[0].astype(F32)) * mb)
    t = jnp.dot(merged.astype(BF16), wo_ref[...], preferred_element_type=F32)
    x1 = x_ref[0] + mod_ref[0, 2:3, :] * t
    x1_ref[0] = x1
    h2 = _rms_modulate(x1, gf_ref[...], mod_ref[0, 3:4, :], mod_ref[0, 4:5, :])
    hi = h2.astype(BF16)
    h2_ref[0] = hi
    lo = (h2 - hi.astype(F32)).astype(BF16)
    lg_ref[0] = (jnp.dot(hi, wrh_ref[...], preferred_element_type=F32)
                 + jnp.dot(lo, wrh_ref[...], preferred_element_type=F32)
                 + jnp.dot(hi, wrl_ref[...], preferred_element_type=F32) + br_ref[...])


def _merge(x, o_a, o_b, proj, mod, wbm, wbs, wo, g_ffn, w_router, b_router):
    bsz, s, d = x.shape
    w = o_a.shape[-1]
    tm = TOKEN_TILE
    gate_blk = (proj.shape[-1] - 2 * d) // d
    const = lambda shape: pl.BlockSpec(shape, lambda b, i: (0,) * len(shape))
    tok = lambda width: pl.BlockSpec((1, tm, width), lambda b, i: (b, i, 0))
    wr_hi = w_router.astype(BF16)
    wr_lo = (w_router - wr_hi.astype(F32)).astype(BF16)
    return pl.pallas_call(
        _merge_kernel,
        grid=(bsz, s // tm),
        in_specs=[
            tok(d), tok(w), tok(w),
            pl.BlockSpec((1, tm, d), lambda b, i: (b, i, gate_blk)),
            pl.BlockSpec((1, tm, d), lambda b, i: (b, i, gate_blk + 1)),
            pl.BlockSpec((1, N_MOD, d), lambda b, i: (b, 0, 0)),
            const((w, d)), const((w, d)), const((d, d)), const((1, d)),
            const((d, LANES)), const((d, LANES)), const((1, LANES)),
        ],
        out_specs=[tok(d), tok(d), tok(LANES)],
        out_shape=[
            jax.ShapeDtypeStruct((bsz, s, d), F32),
            jax.ShapeDtypeStruct((bsz, s, d), BF16),
            jax.ShapeDtypeStruct((bsz, s, LANES), F32),
        ],
        compiler_params=_params("arbitrary", "arbitrary"),
        name="merge",
    )(x, o_a, o_b, proj, proj, mod, wbm, wbs, wo, g_ffn.reshape(1, d), wr_hi, wr_lo, b_router)


ROUTE_COLS = 8
SEG_ALIGN = 8
SEG_SHIFT = 3


def _local_rows(tm):
    return 2 * tm + N_EXPERTS * SEG_ALIGN


def _route_kernel(lg_ref, rt_ref, slot_t_ref, cnt_ref, before_ref, carry_ref):
    i = pl.program_id(0)
    tm = lg_ref.shape[0]
    lane = _lane_index()

    @pl.when(i == 0)
    def _init():
        carry_ref[...] = jnp.zeros_like(carry_ref)

    lg = lg_ref[...]

    def first_max(vals):
        top = jnp.max(vals, axis=-1, keepdims=True)
        idx = jnp.min(jnp.where(vals == top, lane, float(LANES)), axis=-1, keepdims=True)
        return top, idx

    g_logit = jnp.where(lane < N_GROUPS, lg, NEG_INF)
    g_top, g_idx = first_max(g_logit)
    g_w = 1.0 / jnp.sum(jnp.exp(g_logit - g_top), axis=-1, keepdims=True)
    lo = N_GROUPS + g_idx * EXPERTS_PER_GROUP
    e_logit = jnp.where(jnp.logical_and(lane >= lo, lane < lo + EXPERTS_PER_GROUP), lg, NEG_INF)
    e_top0, lane0 = first_max(e_logit)
    e_top1, lane1 = first_max(jnp.where(lane == lane0, NEG_INF, e_logit))
    z = jnp.exp(e_top1 - e_top0)
    w0 = g_w / (1.0 + z)
    w1 = g_w * z / (1.0 + z)

    hot = jnp.logical_or(lane == lane0, lane == lane1)
    row = lax.broadcasted_iota(jnp.int32, (tm, tm), 0)
    col = lax.broadcasted_iota(jnp.int32, (tm, tm), 1)
    earlier = jnp.dot(_indicator(col < row, BF16), _indicator(hot, BF16), preferred_element_type=F32)
    cnt = jnp.sum(_indicator(hot, F32), axis=0, keepdims=True)
    chunks = jnp.ceil(cnt * (1.0 / SEG_ALIGN))
    src = lax.broadcasted_iota(jnp.int32, (LANES, LANES), 0)
    dst = lax.broadcasted_iota(jnp.int32, (LANES, LANES), 1)
    seg_start = SEG_ALIGN * jnp.dot(jnp.broadcast_to(chunks, (8, LANES)).astype(BF16), _indicator(src < dst, BF16),
                                    preferred_element_type=F32)[0:1]
    local = seg_start + earlier
    slot0 = jnp.sum(jnp.where(lane == lane0, local, 0.0), axis=-1, keepdims=True)
    slot1 = jnp.sum(jnp.where(lane == lane1, local, 0.0), axis=-1, keepdims=True)
    cnt_ref[0] = SEG_ALIGN * chunks
    before_ref[0] = carry_ref[...]
    carry_ref[...] += SEG_ALIGN * chunks

    packed = jnp.zeros((tm, LANES), F32)
    for c, v in enumerate([slot0, slot1, w0, w1]):
        packed = jnp.where(lane == float(c), v, packed)
    rt_ref[...] = packed[:, :ROUTE_COLS]
    slot_t_ref[0] = packed.T[:ROUTE_COLS, :]


def _route(logits):
    n = logits.shape[0]
    tm = TOKEN_TILE
    tiles = n // tm
    per_tile = lambda rows, cols: (pl.BlockSpec((1, rows, cols), lambda i: (i, 0, 0)),
                                   jax.ShapeDtypeStruct((tiles, rows, cols), F32))
    specs, shapes = zip((pl.BlockSpec((tm, ROUTE_COLS), lambda i: (i, 0)), jax.ShapeDtypeStruct((n, ROUTE_COLS), F32)),
                        per_tile(ROUTE_COLS, tm), per_tile(1, LANES), per_tile(1, LANES))
    return pl.pallas_call(
        _route_kernel,
        grid=(tiles,),
        in_specs=[pl.BlockSpec((tm, LANES), lambda i: (i, 0))],
        out_specs=list(specs),
        out_shape=list(shapes),
        scratch_shapes=[pltpu.VMEM((1, LANES), F32)],
        compiler_params=_params("arbitrary"),
        name="route",
    )(logits)


def _for_each_chunk(tile, cnt_ref, row0_ref, visit):
    def per_expert(e, local):
        chunks = lax.shift_right_logical(cnt_ref[tile * N_EXPERTS + e], SEG_SHIFT)
        row0 = row0_ref[tile * N_EXPERTS + e]

        def per_chunk(c, carry):
            visit(pl.multiple_of(local + c * SEG_ALIGN, SEG_ALIGN), pl.multiple_of(row0 + c * SEG_ALIGN, SEG_ALIGN))
            return carry

        lax.fori_loop(0, chunks, per_chunk, 0)
        return local + chunks * SEG_ALIGN

    lax.fori_loop(0, N_EXPERTS, per_expert, 0)


def _dispatch_kernel(cnt_ref, row0_ref, slot_t_ref, h_ref, xs_in_ref, xs_ref, loc_ref, sem):
    del xs_in_ref
    tile = pl.program_id(0)
    n_loc, tm = loc_ref.shape[0], h_ref.shape[0]
    slot = lax.broadcasted_iota(jnp.int32, (n_loc, tm), 0).astype(F32)
    place = jnp.logical_or(slot == slot_t_ref[0, 0:1, :], slot == slot_t_ref[0, 1:2, :])
    loc_ref[...] = _pack_rows(jnp.dot(_indicator(place, BF16), h_ref[...], preferred_element_type=F32))

    def chunk_copy(local, row):
        return pltpu.make_async_copy(loc_ref.at[pl.ds(local, SEG_ALIGN), :], xs_ref.at[pl.ds(row, SEG_ALIGN), :], sem)

    _for_each_chunk(tile, cnt_ref, row0_ref, lambda l, r: chunk_copy(l, r).start())
    _for_each_chunk(tile, cnt_ref, row0_ref, lambda l, r: chunk_copy(l, r).wait())


def _dispatch(cnt_flat, row0_flat, slot_t, h2, n_rows):
    n, d = h2.shape
    tm = TOKEN_TILE
    xs0 = jnp.zeros((n_rows, d // 2), U32)
    grid_spec = pltpu.PrefetchScalarGridSpec(
        num_scalar_prefetch=2,
        grid=(n // tm,),
        in_specs=[
            pl.BlockSpec((1, ROUTE_COLS, tm), lambda i, c, r: (i, 0, 0)),
            pl.BlockSpec((tm, d), lambda i, c, r: (i, 0)),
            pl.BlockSpec(memory_space=pl.ANY),
        ],
        out_specs=pl.BlockSpec(memory_space=pl.ANY),
        scratch_shapes=[pltpu.VMEM((_local_rows(tm), d // 2), U32), pltpu.SemaphoreType.DMA(())],
    )
    return pl.pallas_call(
        _dispatch_kernel,
        grid_spec=grid_spec,
        out_shape=jax.ShapeDtypeStruct((n_rows, d // 2), U32),
        input_output_aliases={4: 0},
        compiler_params=_params("arbitrary"),
        name="dispatch",
    )(cnt_flat, row0_flat, slot_t, h2, xs0)


def _expert_kernel(blk_e_ref, n_used_ref, xs_ref, wg_ref, wu_ref, wd_ref, ys_ref):
    del blk_e_ref
    i = pl.program_id(0)

    @pl.when(i < n_used_ref[0])
    def _compute():
        xb = _unpack_rows(xs_ref[...]).astype(BF16)
        g = jnp.dot(xb, wg_ref[0], preferred_element_type=F32)
        u = jnp.dot(xb, wu_ref[0], preferred_element_type=F32)
        hid = (g * jax.nn.sigmoid(g) * u).astype(BF16)
        ys_ref[...] = _pack_rows(jnp.dot(hid, wd_ref[0], preferred_element_type=F32))

    @pl.when(i >= n_used_ref[0])
    def _unused():
        ys_ref[...] = jnp.zeros_like(ys_ref)


def _experts(blk_e, n_used, xs, wg, wu, wd):
    n_rows, half = xs.shape
    _, d, de = wg.shape
    grid_spec = pltpu.PrefetchScalarGridSpec(
        num_scalar_prefetch=2,
        grid=(n_rows // ROW_BLOCK,),
        in_specs=[
            pl.BlockSpec((ROW_BLOCK, half), lambda i, e, u: (i, 0)),
            pl.BlockSpec((1, d, de), lambda i, e, u: (e[i], 0, 0)),
            pl.BlockSpec((1, d, de), lambda i, e, u: (e[i], 0, 0)),
            pl.BlockSpec((1, de, d), lambda i, e, u: (e[i], 0, 0)),
        ],
        out_specs=pl.BlockSpec((ROW_BLOCK, half), lambda i, e, u: (i, 0)),
    )
    return pl.pallas_call(
        _expert_kernel,
        grid_spec=grid_spec,
        out_shape=jax.ShapeDtypeStruct((n_rows, half), U32),
        compiler_params=_params("arbitrary"),
        name="experts",
    )(blk_e, n_used, xs, wg, wu, wd)


def _combine_kernel(cnt_ref, row0_ref, x1_ref, mod_ref, rt_ref, ys_ref, o_ref, loc_ref, sem):
    tm = x1_ref.shape[1]
    n_loc = loc_ref.shape[0]
    tile = pl.program_id(0) * pl.num_programs(1) + pl.program_id(1)

    @pl.when(tile == 0)
    def _init():
        loc_ref[...] = jnp.zeros_like(loc_ref)

    def chunk_copy(local, row):
        return pltpu.make_async_copy(ys_ref.at[pl.ds(row, SEG_ALIGN), :], loc_ref.at[pl.ds(local, SEG_ALIGN), :], sem)

    _for_each_chunk(tile, cnt_ref, row0_ref, lambda l, r: chunk_copy(l, r).start())
    _for_each_chunk(tile, cnt_ref, row0_ref, lambda l, r: chunk_copy(l, r).wait())

    y = _unpack_rows(loc_ref[...]).astype(BF16)
    rt = rt_ref[0]
    slot = lax.broadcasted_iota(jnp.int32, (tm, n_loc), 1).astype(F32)
    y0 = jnp.dot(_indicator(slot == rt[:, 0:1], BF16), y, preferred_element_type=F32)
    y1 = jnp.dot(_indicator(slot == rt[:, 1:2], BF16), y, preferred_element_type=F32)
    o_ref[0] = x1_ref[0] + mod_ref[0, 5:6, :] * (rt[:, 2:3] * y0 + rt[:, 3:4] * y1)


def _combine(cnt_flat, row0_flat, x1, mod, rt, ys):
    bsz, s, d = x1.shape
    tm = TOKEN_TILE
    grid_spec = pltpu.PrefetchScalarGridSpec(
        num_scalar_prefetch=2,
        grid=(bsz, s // tm),
        in_specs=[
            pl.BlockSpec((1, tm, d), lambda b, i, c, r: (b, i, 0)),
            pl.BlockSpec((1, N_MOD, d), lambda b, i, c, r: (b, 0, 0)),
            pl.BlockSpec((1, tm, ROUTE_COLS), lambda b, i, c, r: (b, i, 0)),
            pl.BlockSpec(memory_space=pl.ANY),
        ],
        out_specs=pl.BlockSpec((1, tm, d), lambda b, i, c, r: (b, i, 0)),
        scratch_shapes=[pltpu.VMEM((_local_rows(tm), d // 2), U32), pltpu.SemaphoreType.DMA(())],
    )
    return pl.pallas_call(
        _combine_kernel,
        grid_spec=grid_spec,
        out_shape=jax.ShapeDtypeStruct((bsz, s, d), F32),
        compiler_params=_params("arbitrary", "arbitrary"),
        name="combine",
    )(cnt_flat, row0_flat, x1, mod, rt, ys)


def _layer(x, mod, g_mix, w_in, g_q, g_k, rel_bias, w_br_moba, w_br_sb, w_out, g_ffn,
           w_rg, b_rg, w_re, b_re, w_gate, w_up, w_down):
    bsz, s, d = x.shape
    n = bsz * s
    n_blk = s // ATT_BLOCK

    proj = _inproj(x, mod, g_mix, w_in.astype(BF16))

    top = (n_blk + 1) * ATT_BLOCK - 1
    dist = np.maximum(top - np.arange((n_blk + 2) * ATT_BLOCK), 0)
    rel_rev = rel_bias[:, _rel_bucket_table(top)[dist]].reshape(N_HEADS // HEADS_PER_STEP, HEADS_PER_STEP, -1)

    o_a = _moba(proj, g_q, g_k, rel_rev)
    o_b = _stick_breaking(proj)

    pad = LANES - N_GROUPS - N_EXPERTS
    w_router = jnp.concatenate([w_rg, w_re, jnp.zeros((d, pad), F32)], axis=1)
    b_router = jnp.concatenate([b_rg, b_re, jnp.zeros((pad,), F32)]).reshape(1, LANES)
    x1, h2, logits = _merge(x, o_a, o_b, proj, mod, w_br_moba.astype(BF16), w_br_sb.astype(BF16),
                            w_out.astype(BF16), g_ffn, w_router, b_router)

    rt, slot_t, tile_cnt, tile_before = _route(logits.reshape(n, LANES))

    experts = slice(N_GROUPS, N_GROUPS + N_EXPERTS)
    cnt = tile_cnt[:, 0, experts].astype(jnp.int32)
    total = jnp.sum(cnt, axis=0)
    padded = (total + ROW_BLOCK - 1) // ROW_BLOCK * ROW_BLOCK
    pend = jnp.cumsum(padded)
    pstart = pend - padded
    max_rows = 2 * n + cnt.shape[0] * N_EXPERTS * (SEG_ALIGN - 1) + N_EXPERTS * (ROW_BLOCK - 1)
    n_blocks = -(-max_rows // ROW_BLOCK)
    blk_row0 = jnp.arange(n_blocks, dtype=jnp.int32) * ROW_BLOCK
    blk_e = jnp.minimum(jnp.sum((blk_row0[:, None] >= pend[None, :]).astype(jnp.int32), axis=1), N_EXPERTS - 1)
    n_used = (pend[-1] // ROW_BLOCK).astype(jnp.int32).reshape(1)
    seg_row0 = (pstart[None, :] + tile_before[:, 0, experts].astype(jnp.int32)).reshape(-1)
    cnt_flat = cnt.reshape(-1)

    xs = _dispatch(cnt_flat, seg_row0, slot_t, h2.reshape(n, d), n_blocks * ROW_BLOCK)
    ys = _experts(blk_e, n_used, xs, w_gate.astype(BF16), w_up.astype(BF16), w_down.astype(BF16))
    return _combine(cnt_flat, seg_row0, x1, mod, rt.reshape(bsz, s, ROUTE_COLS), ys)


def kernel(x, c, w_ada, b_ada, g_mix, w_in, g_q, g_k, rel_bias, w_br_moba, w_br_sb, w_out, g_ffn,
           w_rg, b_rg, w_re, b_re, w_gate, w_up, w_down):
    bsz, s, d = x.shape
    assert s % TOKEN_TILE == 0 and s % MOBA_Q_TILE == 0 and d % (2 * LANES) == 0
    for l in range(w_ada.shape[0]):
        mod = _adaln(c, w_ada[l], b_ada[l]).reshape(bsz, N_MOD, d)
        x = _layer(x, mod, g_mix[l], w_in[l], g_q[l], g_k[l], rel_bias, w_br_moba[l], w_br_sb[l], w_out[l],
                   g_ffn[l], w_rg[l], b_rg[l], w_re[l], b_re[l], w_gate[l], w_up[l], w_down[l])
    return x
```

```python
import functools
import math

import numpy as np
import jax
import jax.numpy as jnp
from jax import lax
from jax.experimental import pallas as pl
from jax.experimental.pallas import tpu as pltpu

F32 = jnp.float32
BF16 = jnp.bfloat16
U32 = jnp.uint32
HIGHEST = lax.Precision.HIGHEST

LANES = 128
VMEM_LIMIT_BYTES = 56 * 1024 * 1024

HEAD_DIM = 64
N_HEADS = 8
HEADS_PER_STEP = LANES // HEAD_DIM
ATT_BLOCK = 256
MOBA_Q_TILE = 2 * ATT_BLOCK
MOBA_TOPK = 3
REL_BUCKETS = 32
REL_MAX_DIST = 1024
N_GROUPS = 4
EXPERTS_PER_GROUP = 8
N_EXPERTS = N_GROUPS * EXPERTS_PER_GROUP
N_MOD = 6
RMS_EPS = 1e-6
ROW_BLOCK = 256
TOKEN_TILE = 512
NEG_INF = float("-inf")
MASK_LOGIT = -1e30
SB_EXIT_LOG = -110.0

_NT = (((1,), (1,)), ((), ()))


def _params(*semantics):
    return pltpu.CompilerParams(dimension_semantics=semantics, vmem_limit_bytes=VMEM_LIMIT_BYTES)


def _adaln_kernel(c_ref, w_ref, b_ref, o_ref):
    c = c_ref[...]
    s = c * jax.nn.sigmoid(c)
    o_ref[...] = jnp.dot(s, w_ref[...], preferred_element_type=F32, precision=HIGHEST) + b_ref[...]


def _adaln(c, w, b):
    bsz, d = c.shape
    n = w.shape[1]
    tn = 1536
    return pl.pallas_call(
        _adaln_kernel,
        grid=(n // tn,),
        in_specs=[
            pl.BlockSpec((bsz, d), lambda j: (0, 0)),
            pl.BlockSpec((d, tn), lambda j: (0, j)),
            pl.BlockSpec((1, tn), lambda j: (0, j)),
        ],
        out_specs=pl.BlockSpec((bsz, tn), lambda j: (0, j)),
        out_shape=jax.ShapeDtypeStruct((bsz, n), F32),
        compiler_params=_params("arbitrary"),
        name="adaln",
    )(c, w, b.reshape(1, n))


def _rms_modulate(x, g, shift, scale):
    y = x * lax.rsqrt(jnp.mean(x * x, axis=-1, keepdims=True) + RMS_EPS) * g
    return y * (1.0 + scale) + shift


def _inproj_kernel(x_ref, mod_ref, g_ref, w_ref, o_ref, *, col_chunk):
    h = _rms_modulate(x_ref[0], g_ref[...], mod_ref[0, 0:1, :], mod_ref[0, 1:2, :]).astype(BF16)
    for n in range(w_ref.shape[1] // col_chunk):
        cols = slice(n * col_chunk, (n + 1) * col_chunk)
        o_ref[0, :, cols] = jnp.dot(h, w_ref[:, cols], preferred_element_type=F32).astype(BF16)


def _inproj(x, mod, g, w_bf16):
    bsz, s, d = x.shape
    n = w_bf16.shape[1]
    tm = TOKEN_TILE
    return pl.pallas_call(
        functools.partial(_inproj_kernel, col_chunk=1024),
        grid=(bsz, s // tm),
        in_specs=[
            pl.BlockSpec((1, tm, d), lambda b, i: (b, i, 0)),
            pl.BlockSpec((1, N_MOD, d), lambda b, i: (b, 0, 0)),
            pl.BlockSpec((1, d), lambda b, i: (0, 0)),
            pl.BlockSpec((d, n), lambda b, i: (0, 0)),
        ],
        out_specs=pl.BlockSpec((1, tm, n), lambda b, i: (b, i, 0)),
        out_shape=jax.ShapeDtypeStruct((bsz, s, n), BF16),
        compiler_params=_params("arbitrary", "arbitrary"),
        name="inproj",
    )(x, mod, g.reshape(1, d), w_bf16)


def _indicator(cond, dtype):
    return jnp.where(cond, 1.0, 0.0).astype(dtype)


def _lane_index():
    return lax.broadcasted_iota(jnp.int32, (1, LANES), 1).astype(F32)


def _head_lane_mask():
    return lax.broadcasted_iota(jnp.int32, (1, LANES), 1) < HEAD_DIM


def _head_rmsnorm(t, g):
    first = _head_lane_mask()
    sq = t * t
    ss0 = jnp.sum(jnp.where(first, sq, 0.0), axis=-1, keepdims=True)
    ss1 = jnp.sum(jnp.where(first, 0.0, sq), axis=-1, keepdims=True)
    inv = jnp.where(first, lax.rsqrt(ss0 / HEAD_DIM + RMS_EPS), lax.rsqrt(ss1 / HEAD_DIM + RMS_EPS))
    return t * inv * g


def _split_heads(t):
    first = _head_lane_mask()
    zero = jnp.zeros_like(t)
    return jnp.where(first, t, zero), jnp.where(first, zero, t)


def _rel_bucket_table(max_dist):
    n = np.arange(max_dist + 1)
    max_exact = REL_BUCKETS // 2
    nf = np.maximum(n, 1).astype(np.float64)
    large = max_exact + (np.log(nf / max_exact) / math.log(REL_MAX_DIST / max_exact)
                         * (REL_BUCKETS - max_exact)).astype(np.int64)
    large = np.minimum(large, REL_BUCKETS - 1)
    return np.where(n < max_exact, n, large).astype(np.int32)


def _moba_kernel(q_ref, k_ref, v_ref, gq_ref, gk_ref, rb_ref, o_ref,
                 kaug_ref, kmean_ref, toep_ref, s_ref, mvec_ref, lvec_ref, acc_ref, *, n_blk):
    b = pl.program_id(1)
    qi = pl.program_id(2)
    blk = ATT_BLOCK
    qt = q_ref.shape[1]
    halves = qt // blk
    lane = _lane_index()
    first = _head_lane_mask()
    tag_lane = lane - float(HEAD_DIM)

    @pl.when(jnp.logical_and(b == 0, qi == 0))
    def _build_bias_tiles():
        for h in range(HEADS_PER_STEP):
            for d in range(n_blk):
                r = rb_ref[0, h:h + 1, (n_blk - d) * blk:(n_blk - d + 2) * blk]
                rolled = pltpu.roll(jnp.broadcast_to(r, (blk, 2 * blk)), blk + 1, 1, stride=1, stride_axis=0)
                toep_ref[h, d] = rolled[:, :blk]

    @pl.when(qi == 0)
    def _prep_keys():
        kmean_ref[...] = jnp.zeros_like(kmean_ref)
        for j in range(n_blk):
            kn = _head_rmsnorm(k_ref[0, j * blk:(j + 1) * blk, :].astype(F32), gk_ref[...])
            tag = jnp.where(tag_lane == float(j), 1.0, 0.0)
            for h, kh in enumerate((kn, pltpu.roll(kn, HEAD_DIM, 1))):
                kaug_ref[h, j * blk:(j + 1) * blk, :] = jnp.where(first, kh, tag).astype(BF16)
                kmean_ref[h, j:j + 1, :] = jnp.where(first, jnp.mean(kh, axis=0, keepdims=True), 0.0)

    qn = _head_rmsnorm(q_ref[0].astype(F32), gq_ref[...])
    gate_rows = kmean_ref.shape[1]
    blk_row = lax.broadcasted_iota(jnp.int32, (gate_rows, qt), 0).astype(F32)
    own = (halves * qi + lax.shift_right_logical(lax.broadcasted_iota(jnp.int32, (gate_rows, qt), 1),
                                                 blk.bit_length() - 1)).astype(F32)
    q_aug = []
    for h, qh in enumerate((qn, pltpu.roll(qn, HEAD_DIM, 1))):
        qh = jnp.where(first, qh, 0.0)
        g = lax.dot_general(kmean_ref[h], qh, _NT, preferred_element_type=F32, precision=HIGHEST)
        g = jnp.where(blk_row < own, g, NEG_INF)
        keep = jnp.where(blk_row == own, 1.0, 0.0)
        for _ in range(MOBA_TOPK):
            top = jnp.max(g, axis=0, keepdims=True)
            is_top = jnp.logical_and(g == top, top > NEG_INF)
            idx = jnp.min(jnp.where(is_top, blk_row, float(gate_rows)), axis=0, keepdims=True)
            pick = blk_row == idx
            keep = jnp.where(pick, 1.0, keep)
            g = jnp.where(pick, NEG_INF, g)
        mask_t = jnp.where(keep > 0.0, 0.0, MASK_LOGIT)
        mask_logit = jnp.concatenate([jnp.zeros((HEAD_DIM, qt), F32), mask_t,
                                      jnp.zeros((LANES - HEAD_DIM - gate_rows, qt), F32)], axis=0).T
        q_aug.append(jnp.where(first, qh * (HEAD_DIM ** -0.5), mask_logit).astype(BF16))

    def lane_max(s):
        return jnp.maximum(s[:, :LANES], s[:, LANES:])

    def scores(h, t, dist):
        k_t = kaug_ref[h, pl.ds(pl.multiple_of(t * qt, qt), qt), :]
        s = lax.dot_general(q_aug[h], k_t, _NT, preferred_element_type=F32)
        tiles = []
        for kb in range(halves):
            rows = []
            for qb in range(halves):
                part = s[qb * blk:(qb + 1) * blk, kb * blk:(kb + 1) * blk]
                if isinstance(dist, int) and halves * dist + qb - kb < 0:
                    rows.append(part)
                else:
                    rows.append(part + toep_ref[h, halves * dist + qb - kb])
            tiles.append(jnp.concatenate(rows, axis=0))
        return tiles

    row = lax.broadcasted_iota(jnp.int32, (qt, blk), 0)
    col = lax.broadcasted_iota(jnp.int32, (qt, blk), 1)
    for h in range(HEADS_PER_STEP):
        top = None
        for kb, s in enumerate(scores(h, qi, 0)):
            s = jnp.where(col + kb * blk <= row, s, NEG_INF)
            s_ref[h, halves * qi + kb] = s
            top = lane_max(s) if top is None else jnp.maximum(top, lane_max(s))
        mvec_ref[h] = top

    def pass1(t, carry):
        for h in range(HEADS_PER_STEP):
            top = mvec_ref[h]
            for kb, s in enumerate(scores(h, t, qi - t)):
                s_ref[h, halves * t + kb] = s
                top = jnp.maximum(top, lane_max(s))
            mvec_ref[h] = top
        return carry

    lax.fori_loop(0, qi, pass1, 0)

    for h in range(HEADS_PER_STEP):
        m = jnp.max(mvec_ref[h], axis=-1, keepdims=True)
        mvec_ref[h] = jnp.broadcast_to(m, (qt, LANES))
        lvec_ref[h] = jnp.zeros((qt, LANES), F32)
        acc_ref[h] = jnp.zeros((qt, LANES), F32)

    def pass2(t, carry):
        v_t = v_ref[0, pl.ds(pl.multiple_of(t * qt, qt), qt), :]
        for h in range(HEADS_PER_STEP):
            m = mvec_ref[h]
            parts = []
            for kb in range(halves):
                s = s_ref[h, halves * t + kb]
                parts += [jnp.exp(s[:, :LANES] - m), jnp.exp(s[:, LANES:] - m)]
            lvec_ref[h] += functools.reduce(lambda x, y: x + y, parts)
            p = jnp.concatenate(parts, axis=1).astype(BF16)
            acc_ref[h] += jnp.dot(p, v_t, preferred_element_type=F32)
        return carry

    lax.fori_loop(0, qi + 1, pass2, 0)

    l0 = jnp.sum(lvec_ref[0], axis=-1, keepdims=True)
    l1 = jnp.sum(lvec_ref[1], axis=-1, keepdims=True)
    o_ref[0] = jnp.where(first, acc_ref[0] / l0, acc_ref[1] / l1).astype(o_ref.dtype)


def _moba(proj, g_q, g_k, rel_rev):
    bsz, s, _ = proj.shape
    blk = ATT_BLOCK
    n_blk = s // blk
    n_hp = N_HEADS // HEADS_PER_STEP
    w = N_HEADS * HEAD_DIM
    tile2 = lambda g: jnp.tile(g.reshape(1, HEAD_DIM), (1, HEADS_PER_STEP))
    qt = MOBA_Q_TILE
    return pl.pallas_call(
        functools.partial(_moba_kernel, n_blk=n_blk),
        grid=(n_hp, bsz, s // qt),
        in_specs=[
            pl.BlockSpec((1, qt, LANES), lambda hp, b, i: (b, i, hp)),
            pl.BlockSpec((1, s, LANES), lambda hp, b, i: (b, 0, n_hp + hp)),
            pl.BlockSpec((1, s, LANES), lambda hp, b, i: (b, 0, 2 * n_hp + hp)),
            pl.BlockSpec((1, LANES), lambda hp, b, i: (0, 0)),
            pl.BlockSpec((1, LANES), lambda hp, b, i: (0, 0)),
            pl.BlockSpec((1, HEADS_PER_STEP, rel_rev.shape[-1]), lambda hp, b, i: (hp, 0, 0)),
        ],
        out_specs=pl.BlockSpec((1, qt, LANES), lambda hp, b, i: (b, i, hp)),
        out_shape=jax.ShapeDtypeStruct((bsz, s, w), BF16),
        scratch_shapes=[
            pltpu.VMEM((HEADS_PER_STEP, s, LANES), BF16),
            pltpu.VMEM((HEADS_PER_STEP, -(-n_blk // 8) * 8, LANES), F32),
            pltpu.VMEM((HEADS_PER_STEP, n_blk, blk, blk), F32),
            pltpu.VMEM((HEADS_PER_STEP, n_blk, qt, blk), F32),
            pltpu.VMEM((HEADS_PER_STEP, qt, LANES), F32),
            pltpu.VMEM((HEADS_PER_STEP, qt, LANES), F32),
            pltpu.VMEM((HEADS_PER_STEP, qt, LANES), F32),
        ],
        compiler_params=_params("arbitrary", "arbitrary", "arbitrary"),
        name="moba",
    )(proj, proj, proj, tile2(g_q), tile2(g_k), rel_rev)


def _sb_kernel(q_ref, k_ref, v_ref, o_ref, acc_ref, carry_ref):
    qi = pl.program_id(2)
    blk = ATT_BLOCK
    row = lax.broadcasted_iota(jnp.int32, (blk, blk), 0)
    col = lax.broadcasted_iota(jnp.int32, (blk, blk), 1)
    past = col < row
    later = _indicator(row > col, BF16)

    q_heads = _split_heads(q_ref[0].astype(F32) * (HEAD_DIM ** -0.5))
    q_heads = [t.astype(BF16) for t in q_heads]

    def walk(blocks, fresh):
        for h in range(HEADS_PER_STEP):
            carry = None if fresh else carry_ref[h]
            total = None if fresh else acc_ref[h]
            for j, diagonal in blocks:
                start = pl.multiple_of(j * blk, blk)
                z = lax.dot_general(q_heads[h], k_ref[0, pl.ds(start, blk), :], _NT, preferred_element_type=F32)
                log_1m = -(jnp.maximum(z, 0.0) + jnp.log(1.0 + jnp.exp(-jnp.abs(z))))
                masked = jnp.where(past, log_1m, 0.0) if diagonal else log_1m
                hi = masked.astype(BF16)
                lo = (masked - hi.astype(F32)).astype(BF16)
                after = (jnp.dot(hi, later, preferred_element_type=F32)
                         + jnp.dot(lo, later, preferred_element_type=F32))
                row_sum = after[:, 0:1] + masked[:, 0:1]
                if carry is not None:
                    after = after + carry
                a = jnp.exp(z + log_1m + after)
                if diagonal:
                    a = jnp.where(past, a, 0.0)
                pv = jnp.dot(a.astype(BF16), v_ref[0, pl.ds(start, blk), :], preferred_element_type=F32)
                total = pv if total is None else total + pv
                carry = row_sum if carry is None else carry + row_sum
            carry_ref[h] = carry
            acc_ref[h] = total

    @pl.when(qi == 0)
    def _first_tile():
        walk([(qi, True)], True)

    @pl.when(qi > 0)
    def _own_and_previous():
        walk([(qi, True), (qi - 1, False)], True)

    def slowest_decay():
        return jnp.maximum(jnp.max(carry_ref[0]), jnp.max(carry_ref[1]))

    def cond(state):
        j, worst = state
        return jnp.logical_and(j >= 0, worst > SB_EXIT_LOG)

    def body(state):
        j, _ = state
        walk([(j, False)], False)
        return j - 1, slowest_decay()

    lax.while_loop(cond, body, (qi - 2, slowest_decay()))

    o_ref[0] = jnp.where(_head_lane_mask(), acc_ref[0], acc_ref[1]).astype(o_ref.dtype)


def _stick_breaking(proj):
    bsz, s, _ = proj.shape
    blk = ATT_BLOCK
    n_hp = N_HEADS // HEADS_PER_STEP
    w = N_HEADS * HEAD_DIM
    base = 3 * n_hp
    return pl.pallas_call(
        _sb_kernel,
        grid=(n_hp, bsz, s // blk),
        in_specs=[
            pl.BlockSpec((1, blk, LANES), lambda hp, b, i: (b, i, base + hp)),
            pl.BlockSpec((1, s, LANES), lambda hp, b, i: (b, 0, base + n_hp + hp)),
            pl.BlockSpec((1, s, LANES), lambda hp, b, i: (b, 0, base + 2 * n_hp + hp)),
        ],
        out_specs=pl.BlockSpec((1, blk, LANES), lambda hp, b, i: (b, i, hp)),
        out_shape=jax.ShapeDtypeStruct((bsz, s, w), BF16),
        scratch_shapes=[
            pltpu.VMEM((HEADS_PER_STEP, blk, LANES), F32),
            pltpu.VMEM((HEADS_PER_STEP, blk, 1), F32),
        ],
        compiler_params=_params("arbitrary", "arbitrary", "arbitrary"),
        name="stickbrk",
    )(proj, proj, proj)


def _pack_rows(t):
    n = t.shape[1] // 2
    bits = pltpu.bitcast(t.astype(BF16).astype(F32), U32)
    return bits[:, :n] | (bits[:, n:] >> 16)


def _unpack_rows(p):
    hi = pltpu.bitcast(p & jnp.uint32(0xFFFF0000), F32)
    lo = pltpu.bitcast(p << 16, F32)
    return jnp.concatenate([hi, lo], axis=1)


def _merge_kernel(x_ref, oa_ref, ob_ref, ga_ref, gb_ref, mod_ref, wbm_ref, wbs_ref, wo_ref, gf_ref,
                  wrh_ref, wrl_ref, br_ref, x1_ref, h2_ref, lg_ref):
    ma = jnp.dot(oa_ref[0], wbm_ref[...], preferred_element_type=F32)
    mb = jnp.dot(ob_ref[0], wbs_ref[...], preferred_element_type=F32)
    merged = (jax.nn.sigmoid(ga_ref[0].astype(F32)) * ma + jax.nn.sigmoid(gb_ref[0].astype(F32)) * mb)
    t = jnp.dot(merged.astype(BF16), wo_ref[...], preferred_element_type=F32)
    x1 = x_ref[0] + mod_ref[0, 2:3, :] * t
    x1_ref[0] = x1
    h2 = _rms_modulate(x1, gf_ref[...], mod_ref[0, 3:4, :], mod_ref[0, 4:5, :])
    hi = h2.astype(BF16)
    h2_ref[0] = hi
    lo = (h2 - hi.astype(F32)).astype(BF16)
    lg_ref[0] = (jnp.dot(hi, wrh_ref[...], preferred_element_type=F32)
                 + jnp.dot(lo, wrh_ref[...], preferred_element_type=F32)
                 + jnp.dot(hi, wrl_ref[...], preferred_element_type=F32) + br_ref[...])


def _merge(x, o_a, o_b, proj, mod, wbm, wbs, wo, g_ffn, w_router, b_router):
    bsz, s, d = x.shape
    w = o_a.shape[-1]
    tm = TOKEN_TILE
    gate_blk = (proj.shape[-1] - 2 * d) // d
    const = lambda shape: pl.BlockSpec(shape, lambda b, i: (0,) * len(shape))
    tok = lambda width: pl.BlockSpec((1, tm, width), lambda b, i: (b, i, 0))
    wr_hi = w_router.astype(BF16)
    wr_lo = (w_router - wr_hi.astype(F32)).astype(BF16)
    return pl.pallas_call(
        _merge_kernel,
        grid=(bsz, s // tm),
        in_specs=[
            tok(d), tok(w), tok(w),
            pl.BlockSpec((1, tm, d), lambda b, i: (b, i, gate_blk)),
            pl.BlockSpec((1, tm, d), lambda b, i: (b, i, gate_blk + 1)),
            pl.BlockSpec((1, N_MOD, d), lambda b, i: (b, 0, 0)),
            const((w, d)), const((w, d)), const((d, d)), const((1, d)),
            const((d, LANES)), const((d, LANES)), const((1, LANES)),
        ],
        out_specs=[tok(d), tok(d), tok(LANES)],
        out_shape=[
            jax.ShapeDtypeStruct((bsz, s, d), F32),
            jax.ShapeDtypeStruct((bsz, s, d), BF16),
            jax.ShapeDtypeStruct((bsz, s, LANES), F32),
        ],
        compiler_params=_params("arbitrary", "arbitrary"),
        name="merge",
    )(x, o_a, o_b, proj, proj, mod, wbm, wbs, wo, g_ffn.reshape(1, d), wr_hi, wr_lo, b_router)


ROUTE_COLS = 8
SEG_ALIGN = 8
SEG_SHIFT = 3


def _local_rows(tm):
    return 2 * tm + N_EXPERTS * SEG_ALIGN


def _route_kernel(lg_ref, rt_ref, slot_t_ref, cnt_ref, before_ref, carry_ref):
    i = pl.program_id(0)
    tm = lg_ref.shape[0]
    lane = _lane_index()

    @pl.when(i == 0)
    def _init():
        carry_ref[...] = jnp.zeros_like(carry_ref)

    lg = lg_ref[...]

    def first_max(vals):
        top = jnp.max(vals, axis=-1, keepdims=True)
        idx = jnp.min(jnp.where(vals == top, lane, float(LANES)), axis=-1, keepdims=True)
        return top, idx

    g_logit = jnp.where(lane < N_GROUPS, lg, NEG_INF)
    g_top, g_idx = first_max(g_logit)
    g_w = 1.0 / jnp.sum(jnp.exp(g_logit - g_top), axis=-1, keepdims=True)
    lo = N_GROUPS + g_idx * EXPERTS_PER_GROUP
    e_logit = jnp.where(jnp.logical_and(lane >= lo, lane < lo + EXPERTS_PER_GROUP), lg, NEG_INF)
    e_top0, lane0 = first_max(e_logit)
    e_top1, lane1 = first_max(jnp.where(lane == lane0, NEG_INF, e_logit))
    z = jnp.exp(e_top1 - e_top0)
    w0 = g_w / (1.0 + z)
    w1 = g_w * z / (1.0 + z)

    hot = jnp.logical_or(lane == lane0, lane == lane1)
    row = lax.broadcasted_iota(jnp.int32, (tm, tm), 0)
    col = lax.broadcasted_iota(jnp.int32, (tm, tm), 1)
    earlier = jnp.dot(_indicator(col < row, BF16), _indicator(hot, BF16), preferred_element_type=F32)
    cnt = jnp.sum(_indicator(hot, F32), axis=0, keepdims=True)
    chunks = jnp.ceil(cnt * (1.0 / SEG_ALIGN))
    src = lax.broadcasted_iota(jnp.int32, (LANES, LANES), 0)
    dst = lax.broadcasted_iota(jnp.int32, (LANES, LANES), 1)
    seg_start = SEG_ALIGN * jnp.dot(jnp.broadcast_to(chunks, (8, LANES)).astype(BF16), _indicator(src < dst, BF16),
                                    preferred_element_type=F32)[0:1]
    local = seg_start + earlier
    slot0 = jnp.sum(jnp.where(lane == lane0, local, 0.0), axis=-1, keepdims=True)
    slot1 = jnp.sum(jnp.where(lane == lane1, local, 0.0), axis=-1, keepdims=True)
    cnt_ref[0] = SEG_ALIGN * chunks
    before_ref[0] = carry_ref[...]
    carry_ref[...] += SEG_ALIGN * chunks

    packed = jnp.zeros((tm, LANES), F32)
    for c, v in enumerate([slot0, slot1, w0, w1]):
        packed = jnp.where(lane == float(c), v, packed)
    rt_ref[...] = packed[:, :ROUTE_COLS]
    slot_t_ref[0] = packed.T[:ROUTE_COLS, :]


def _route(logits):
    n = logits.shape[0]
    tm = TOKEN_TILE
    tiles = n // tm
    per_tile = lambda rows, cols: (pl.BlockSpec((1, rows, cols), lambda i: (i, 0, 0)),
                                   jax.ShapeDtypeStruct((tiles, rows, cols), F32))
    specs, shapes = zip((pl.BlockSpec((tm, ROUTE_COLS), lambda i: (i, 0)), jax.ShapeDtypeStruct((n, ROUTE_COLS), F32)),
                        per_tile(ROUTE_COLS, tm), per_tile(1, LANES), per_tile(1, LANES))
    return pl.pallas_call(
        _route_kernel,
        grid=(tiles,),
        in_specs=[pl.BlockSpec((tm, LANES), lambda i: (i, 0))],
        out_specs=list(specs),
        out_shape=list(shapes),
        scratch_shapes=[pltpu.VMEM((1, LANES), F32)],
        compiler_params=_params("arbitrary"),
        name="route",
    )(logits)


def _for_each_chunk(tile, cnt_ref, row0_ref, visit):
    def per_expert(e, local):
        chunks = lax.shift_right_logical(cnt_ref[tile * N_EXPERTS + e], SEG_SHIFT)
        row0 = row0_ref[tile * N_EXPERTS + e]

        def per_chunk(c, carry):
            visit(pl.multiple_of(local + c * SEG_ALIGN, SEG_ALIGN), pl.multiple_of(row0 + c * SEG_ALIGN, SEG_ALIGN))
            return carry

        lax.fori_loop(0, chunks, per_chunk, 0)
        return local + chunks * SEG_ALIGN

    lax.fori_loop(0, N_EXPERTS, per_expert, 0)


def _dispatch_kernel(cnt_ref, row0_ref, slot_t_ref, h_ref, xs_in_ref, xs_ref, loc_ref, sems):
    del xs_in_ref
    tile = pl.program_id(0)
    n_loc, tm = loc_ref.shape[1], h_ref.shape[0]
    buf = tile & 1

    slot = lax.broadcasted_iota(jnp.int32, (n_loc, tm), 0).astype(F32)
    place = jnp.logical_or(slot == slot_t_ref[0, 0:1, :], slot == slot_t_ref[0, 1:2, :])
    loc_ref[buf] = _pack_rows(jnp.dot(_indicator(place, BF16), h_ref[...], preferred_element_type=F32))

    def chunk_copy(b):
        return lambda local, row: pltpu.make_async_copy(
            loc_ref.at[b, pl.ds(local, SEG_ALIGN), :], xs_ref.at[pl.ds(row, SEG_ALIGN), :], sems.at[b])

    _for_each_chunk(tile, cnt_ref, row0_ref, lambda l, r: chunk_copy(buf)(l, r).start())

    @pl.when(tile > 0)
    def _previous_done():
        _for_each_chunk(tile - 1, cnt_ref, row0_ref, lambda l, r: chunk_copy(1 - buf)(l, r).wait())

    @pl.when(tile == pl.num_programs(0) - 1)
    def _last_done():
        _for_each_chunk(tile, cnt_ref, row0_ref, lambda l, r: chunk_copy(buf)(l, r).wait())


def _dispatch(cnt_flat, row0_flat, slot_t, h2, n_rows):
    n, d = h2.shape
    tm = TOKEN_TILE
    xs0 = jnp.zeros((n_rows, d // 2), U32)
    grid_spec = pltpu.PrefetchScalarGridSpec(
        num_scalar_prefetch=2,
        grid=(n // tm,),
        in_specs=[
            pl.BlockSpec((1, ROUTE_COLS, tm), lambda i, c, r: (i, 0, 0)),
            pl.BlockSpec((tm, d), lambda i, c, r: (i, 0)),
            pl.BlockSpec(memory_space=pl.ANY),
        ],
        out_specs=pl.BlockSpec(memory_space=pl.ANY),
        scratch_shapes=[pltpu.VMEM((2, _local_rows(tm), d // 2), U32), pltpu.SemaphoreType.DMA((2,))],
    )
    return pl.pallas_call(
        _dispatch_kernel,
        grid_spec=grid_spec,
        out_shape=jax.ShapeDtypeStruct((n_rows, d // 2), U32),
        input_output_aliases={4: 0},
        compiler_params=_params("arbitrary"),
        name="dispatch",
    )(cnt_flat, row0_flat, slot_t, h2, xs0)


def _expert_kernel(blk_e_ref, n_used_ref, xs_ref, wg_ref, wu_ref, wd_ref, ys_ref, wg_bf, wu_bf, wd_bf):
    i = pl.program_id(0)
    used = i < n_used_ref[0]

    @pl.when(jnp.logical_and(used, jnp.logical_or(i == 0, blk_e_ref[i] != blk_e_ref[jnp.maximum(i - 1, 0)])))
    def _new_expert():
        wg_bf[...] = wg_ref[0].astype(BF16)
        wu_bf[...] = wu_ref[0].astype(BF16)
        wd_bf[...] = wd_ref[0].astype(BF16)

    @pl.when(used)
    def _compute():
        xb = _unpack_rows(xs_ref[...]).astype(BF16)
        g = jnp.dot(xb, wg_bf[...], preferred_element_type=F32)
        u = jnp.dot(xb, wu_bf[...], preferred_element_type=F32)
        hid = (g * jax.nn.sigmoid(g) * u).astype(BF16)
        ys_ref[...] = _pack_rows(jnp.dot(hid, wd_bf[...], preferred_element_type=F32))

    @pl.when(i >= n_used_ref[0])
    def _unused():
        ys_ref[...] = jnp.zeros_like(ys_ref)


def _experts(blk_e, n_used, xs, wg, wu, wd):
    n_rows, half = xs.shape
    _, d, de = wg.shape
    grid_spec = pltpu.PrefetchScalarGridSpec(
        num_scalar_prefetch=2,
        grid=(n_rows // ROW_BLOCK,),
        in_specs=[
            pl.BlockSpec((ROW_BLOCK, half), lambda i, e, u: (i, 0)),
            pl.BlockSpec((1, d, de), lambda i, e, u: (e[i], 0, 0)),
            pl.BlockSpec((1, d, de), lambda i, e, u: (e[i], 0, 0)),
            pl.BlockSpec((1, de, d), lambda i, e, u: (e[i], 0, 0)),
        ],
        out_specs=pl.BlockSpec((ROW_BLOCK, half), lambda i, e, u: (i, 0)),
        scratch_shapes=[pltpu.VMEM((d, de), BF16), pltpu.VMEM((d, de), BF16), pltpu.VMEM((de, d), BF16)],
    )
    return pl.pallas_call(
        _expert_kernel,
        grid_spec=grid_spec,
        out_shape=jax.ShapeDtypeStruct((n_rows, half), U32),
        compiler_params=_params("arbitrary"),
        name="experts",
    )(blk_e, n_used, xs, wg, wu, wd)


def _combine_kernel(cnt_ref, row0_ref, x1_ref, mod_ref, rt_ref, ys_ref, o_ref, loc_ref, sems):
    tm = x1_ref.shape[1]
    n_loc = loc_ref.shape[1]
    tile = pl.program_id(0) * pl.num_programs(1) + pl.program_id(1)
    n_tiles = pl.num_programs(0) * pl.num_programs(1)
    buf = tile & 1

    def chunk_copy(b):
        return lambda local, row: pltpu.make_async_copy(
            ys_ref.at[pl.ds(row, SEG_ALIGN), :], loc_ref.at[b, pl.ds(local, SEG_ALIGN), :], sems.at[b])

    @pl.when(tile == 0)
    def _init():
        loc_ref[...] = jnp.zeros_like(loc_ref)
        _for_each_chunk(tile, cnt_ref, row0_ref, lambda l, r: chunk_copy(buf)(l, r).start())

    @pl.when(tile + 1 < n_tiles)
    def _fetch_next():
        _for_each_chunk(tile + 1, cnt_ref, row0_ref, lambda l, r: chunk_copy(1 - buf)(l, r).start())

    _for_each_chunk(tile, cnt_ref, row0_ref, lambda l, r: chunk_copy(buf)(l, r).wait())

    y = _unpack_rows(loc_ref[buf]).astype(BF16)
    rt = rt_ref[0]
    slot = lax.broadcasted_iota(jnp.int32, (tm, n_loc), 1).astype(F32)
    y0 = jnp.dot(_indicator(slot == rt[:, 0:1], BF16), y, preferred_element_type=F32)
    y1 = jnp.dot(_indicator(slot == rt[:, 1:2], BF16), y, preferred_element_type=F32)
    o_ref[0] = x1_ref[0] + mod_ref[0, 5:6, :] * (rt[:, 2:3] * y0 + rt[:, 3:4] * y1)


def _combine(cnt_flat, row0_flat, x1, mod, rt, ys):
    bsz, s, d = x1.shape
    tm = TOKEN_TILE
    grid_spec = pltpu.PrefetchScalarGridSpec(
        num_scalar_prefetch=2,
        grid=(bsz, s // tm),
        in_specs=[
            pl.BlockSpec((1, tm, d), lambda b, i, c, r: (b, i, 0)),
            pl.BlockSpec((1, N_MOD, d), lambda b, i, c, r: (b, 0, 0)),
            pl.BlockSpec((1, tm, ROUTE_COLS), lambda b, i, c, r: (b, i, 0)),
            pl.BlockSpec(memory_space=pl.ANY),
        ],
        out_specs=pl.BlockSpec((1, tm, d), lambda b, i, c, r: (b, i, 0)),
        scratch_shapes=[pltpu.VMEM((2, _local_rows(tm), d // 2), U32), pltpu.SemaphoreType.DMA((2,))],
    )
    return pl.pallas_call(
        _combine_kernel,
        grid_spec=grid_spec,
        out_shape=jax.ShapeDtypeStruct((bsz, s, d), F32),
        compiler_params=_params("arbitrary", "arbitrary"),
        name="combine",
    )(cnt_flat, row0_flat, x1, mod, rt, ys)


def _layer(x, mod, g_mix, w_in, g_q, g_k, rel_bias, w_br_moba, w_br_sb, w_out, g_ffn,
           w_rg, b_rg, w_re, b_re, w_gate, w_up, w_down):
    bsz, s, d = x.shape
    n = bsz * s
    n_blk = s // ATT_BLOCK

    proj = _inproj(x, mod, g_mix, w_in.astype(BF16))

    top = (n_blk + 1) * ATT_BLOCK - 1
    dist = np.maximum(top - np.arange((n_blk + 2) * ATT_BLOCK), 0)
    rel_rev = rel_bias[:, _rel_bucket_table(top)[dist]].reshape(N_HEADS // HEADS_PER_STEP, HEADS_PER_STEP, -1)

    o_a = _moba(proj, g_q, g_k, rel_rev)
    o_b = _stick_breaking(proj)

    pad = LANES - N_GROUPS - N_EXPERTS
    w_router = jnp.concatenate([w_rg, w_re, jnp.zeros((d, pad), F32)], axis=1)
    b_router = jnp.concatenate([b_rg, b_re, jnp.zeros((pad,), F32)]).reshape(1, LANES)
    x1, h2, logits = _merge(x, o_a, o_b, proj, mod, w_br_moba.astype(BF16), w_br_sb.astype(BF16),
                            w_out.astype(BF16), g_ffn, w_router, b_router)

    rt, slot_t, tile_cnt, tile_before = _route(logits.reshape(n, LANES))

    experts = slice(N_GROUPS, N_GROUPS + N_EXPERTS)
    cnt = tile_cnt[:, 0, experts].astype(jnp.int32)
    total = jnp.sum(cnt, axis=0)
    padded = (total + ROW_BLOCK - 1) // ROW_BLOCK * ROW_BLOCK
    pend = jnp.cumsum(padded)
    pstart = pend - padded
    max_rows = 2 * n + cnt.shape[0] * N_EXPERTS * (SEG_ALIGN - 1) + N_EXPERTS * (ROW_BLOCK - 1)
    n_blocks = -(-max_rows // ROW_BLOCK)
    blk_row0 = jnp.arange(n_blocks, dtype=jnp.int32) * ROW_BLOCK
    blk_e = jnp.minimum(jnp.sum((blk_row0[:, None] >= pend[None, :]).astype(jnp.int32), axis=1), N_EXPERTS - 1)
    n_used = (pend[-1] // ROW_BLOCK).astype(jnp.int32).reshape(1)
    seg_row0 = (pstart[None, :] + tile_before[:, 0, experts].astype(jnp.int32)).reshape(-1)
    cnt_flat = cnt.reshape(-1)

    xs = _dispatch(cnt_flat, seg_row0, slot_t, h2.reshape(n, d), n_blocks * ROW_BLOCK)
    ys = _experts(blk_e, n_used, xs, w_gate, w_up, w_down)
    return _combine(cnt_flat, seg_row0, x1, mod, rt.reshape(bsz, s, ROUTE_COLS), ys)


def kernel(x, c, w_ada, b_ada, g_mix, w_in, g_q, g_k, rel_bias, w_br_moba, w_br_sb, w_out, g_ffn,
           w_rg, b_rg, w_re, b_re, w_gate, w_up, w_down):
    bsz, s, d = x.shape
    assert s % TOKEN_TILE == 0 and s % MOBA_Q_TILE == 0 and d % (2 * LANES) == 0
    for l in range(w_ada.shape[0]):
        mod = _adaln(c, w_ada[l], b_ada[l]).reshape(bsz, N_MOD, d)
        x = _layer(x, mod, g_mix[l], w_in[l], g_q[l], g_k[l], rel_bias, w_br_moba[l], w_br_sb[l], w_out[l],
                   g_ffn[l], w_rg[l], b_rg[l], w_re[l], b_re[l], w_gate[l], w_up[l], w_down[l])
    return x
```

```python
import functools
import math

import numpy as np
import jax
import jax.numpy as jnp
from jax import lax
from jax.experimental import pallas as pl
from jax.experimental.pallas import tpu as pltpu

F32 = jnp.float32
BF16 = jnp.bfloat16
U32 = jnp.uint32
HIGHEST = lax.Precision.HIGHEST

LANES = 128
VMEM_LIMIT_BYTES = 56 * 1024 * 1024

HEAD_DIM = 64
N_HEADS = 8
HEADS_PER_STEP = LANES // HEAD_DIM
ATT_BLOCK = 256
MOBA_Q_TILE = 2 * ATT_BLOCK
SB_Q_BLOCKS = 2
MOBA_TOPK = 3
REL_BUCKETS = 32
REL_MAX_DIST = 1024
N_GROUPS = 4
EXPERTS_PER_GROUP = 8
N_EXPERTS = N_GROUPS * EXPERTS_PER_GROUP
N_MOD = 6
RMS_EPS = 1e-6
ROW_BLOCK = 256
TOKEN_TILE = 512
NEG_INF = float("-inf")
MASK_LOGIT = -1e30
SB_EXIT_LOG = -110.0

_NT = (((1,), (1,)), ((), ()))


def _params(*semantics):
    return pltpu.CompilerParams(dimension_semantics=semantics, vmem_limit_bytes=VMEM_LIMIT_BYTES)


def _adaln_kernel(c_ref, w_ref, b_ref, o_ref):
    c = c_ref[...]
    s = c * jax.nn.sigmoid(c)
    o_ref[...] = jnp.dot(s, w_ref[...], preferred_element_type=F32, precision=HIGHEST) + b_ref[...]


def _adaln(c, w, b):
    bsz, d = c.shape
    n = w.shape[1]
    tn = 1536
    return pl.pallas_call(
        _adaln_kernel,
        grid=(n // tn,),
        in_specs=[
            pl.BlockSpec((bsz, d), lambda j: (0, 0)),
            pl.BlockSpec((d, tn), lambda j: (0, j)),
            pl.BlockSpec((1, tn), lambda j: (0, j)),
        ],
        out_specs=pl.BlockSpec((bsz, tn), lambda j: (0, j)),
        out_shape=jax.ShapeDtypeStruct((bsz, n), F32),
        compiler_params=_params("arbitrary"),
        name="adaln",
    )(c, w, b.reshape(1, n))


def _rms_modulate(x, g, shift, scale):
    y = x * lax.rsqrt(jnp.mean(x * x, axis=-1, keepdims=True) + RMS_EPS) * g
    return y * (1.0 + scale) + shift


def _inproj_kernel(x_ref, mod_ref, g_ref, w_ref, o_ref, *, col_chunk):
    h = _rms_modulate(x_ref[0], g_ref[...], mod_ref[0, 0:1, :], mod_ref[0, 1:2, :]).astype(BF16)
    for n in range(w_ref.shape[1] // col_chunk):
        cols = slice(n * col_chunk, (n + 1) * col_chunk)
        o_ref[0, :, cols] = jnp.dot(h, w_ref[:, cols], preferred_element_type=F32).astype(BF16)


def _inproj(x, mod, g, w_bf16):
    bsz, s, d = x.shape
    n = w_bf16.shape[1]
    tm = TOKEN_TILE
    return pl.pallas_call(
        functools.partial(_inproj_kernel, col_chunk=1024),
        grid=(bsz, s // tm),
        in_specs=[
            pl.BlockSpec((1, tm, d), lambda b, i: (b, i, 0)),
            pl.BlockSpec((1, N_MOD, d), lambda b, i: (b, 0, 0)),
            pl.BlockSpec((1, d), lambda b, i: (0, 0)),
            pl.BlockSpec((d, n), lambda b, i: (0, 0)),
        ],
        out_specs=pl.BlockSpec((1, tm, n), lambda b, i: (b, i, 0)),
        out_shape=jax.ShapeDtypeStruct((bsz, s, n), BF16),
        compiler_params=_params("arbitrary", "arbitrary"),
        name="inproj",
    )(x, mod, g.reshape(1, d), w_bf16)


def _indicator(cond, dtype):
    return jnp.where(cond, 1.0, 0.0).astype(dtype)


def _lane_index():
    return lax.broadcasted_iota(jnp.int32, (1, LANES), 1).astype(F32)


def _head_lane_mask():
    return lax.broadcasted_iota(jnp.int32, (1, LANES), 1) < HEAD_DIM


def _head_rmsnorm(t, g):
    first = _head_lane_mask()
    sq = t * t
    ss0 = jnp.sum(jnp.where(first, sq, 0.0), axis=-1, keepdims=True)
    ss1 = jnp.sum(jnp.where(first, 0.0, sq), axis=-1, keepdims=True)
    inv = jnp.where(first, lax.rsqrt(ss0 / HEAD_DIM + RMS_EPS), lax.rsqrt(ss1 / HEAD_DIM + RMS_EPS))
    return t * inv * g


def _split_heads(t):
    first = _head_lane_mask()
    zero = jnp.zeros_like(t)
    return jnp.where(first, t, zero), jnp.where(first, zero, t)


def _rel_bucket_table(max_dist):
    n = np.arange(max_dist + 1)
    max_exact = REL_BUCKETS // 2
    nf = np.maximum(n, 1).astype(np.float64)
    large = max_exact + (np.log(nf / max_exact) / math.log(REL_MAX_DIST / max_exact)
                         * (REL_BUCKETS - max_exact)).astype(np.int64)
    large = np.minimum(large, REL_BUCKETS - 1)
    return np.where(n < max_exact, n, large).astype(np.int32)


def _moba_kernel(q_ref, k_ref, v_ref, gq_ref, gk_ref, rb_ref, o_ref,
                 kaug_ref, vaug_ref, kmean_ref, toep_ref, s_ref, mvec_ref, acc_ref, *, n_blk):
    b = pl.program_id(1)
    qi = pl.program_id(2)
    blk = ATT_BLOCK
    qt = q_ref.shape[1]
    halves = qt // blk
    lane = _lane_index()
    first = _head_lane_mask()
    tag_lane = lane - float(HEAD_DIM)

    @pl.when(jnp.logical_and(b == 0, qi == 0))
    def _build_bias_tiles():
        for h in range(HEADS_PER_STEP):
            for d in range(n_blk):
                r = rb_ref[0, h:h + 1, (n_blk - d) * blk:(n_blk - d + 2) * blk]
                rolled = pltpu.roll(jnp.broadcast_to(r, (blk, 2 * blk)), blk + 1, 1, stride=1, stride_axis=0)
                toep_ref[h, d] = rolled[:, :blk]

    @pl.when(qi == 0)
    def _prep_keys():
        kmean_ref[...] = jnp.zeros_like(kmean_ref)
        for j in range(n_blk):
            kn = _head_rmsnorm(k_ref[0, j * blk:(j + 1) * blk, :].astype(F32), gk_ref[...])
            tag = jnp.where(tag_lane == float(j), 1.0, 0.0)
            for h, kh in enumerate((kn, pltpu.roll(kn, HEAD_DIM, 1))):
                kaug_ref[h, j * blk:(j + 1) * blk, :] = jnp.where(first, kh, tag).astype(BF16)
                kmean_ref[h, j:j + 1, :] = jnp.where(first, jnp.mean(kh, axis=0, keepdims=True), 0.0)
            vb = v_ref[0, j * blk:(j + 1) * blk, :].astype(F32)
            for h, vh in enumerate((vb, pltpu.roll(vb, HEAD_DIM, 1))):
                vaug_ref[h, j * blk:(j + 1) * blk, :] = jnp.where(first, vh, 1.0).astype(BF16)

    qn = _head_rmsnorm(q_ref[0].astype(F32), gq_ref[...])
    gate_rows = kmean_ref.shape[1]
    blk_row = lax.broadcasted_iota(jnp.int32, (gate_rows, qt), 0).astype(F32)
    own = (halves * qi + lax.shift_right_logical(lax.broadcasted_iota(jnp.int32, (gate_rows, qt), 1),
                                                 blk.bit_length() - 1)).astype(F32)
    q_aug = []
    for h, qh in enumerate((qn, pltpu.roll(qn, HEAD_DIM, 1))):
        qh = jnp.where(first, qh, 0.0)
        km = kmean_ref[h]
        km_hi, q_hi = km.astype(BF16), qh.astype(BF16)
        km_lo, q_lo = (km - km_hi.astype(F32)).astype(BF16), (qh - q_hi.astype(F32)).astype(BF16)
        g = (lax.dot_general(km_hi, q_hi, _NT, preferred_element_type=F32)
             + lax.dot_general(km_lo, q_hi, _NT, preferred_element_type=F32)
             + lax.dot_general(km_hi, q_lo, _NT, preferred_element_type=F32))
        g = jnp.where(blk_row < own, g, NEG_INF)
        keep = jnp.where(blk_row == own, 1.0, 0.0)
        for _ in range(MOBA_TOPK):
            top = jnp.max(g, axis=0, keepdims=True)
            is_top = jnp.logical_and(g == top, top > NEG_INF)
            idx = jnp.min(jnp.where(is_top, blk_row, float(gate_rows)), axis=0, keepdims=True)
            pick = blk_row == idx
            keep = jnp.where(pick, 1.0, keep)
            g = jnp.where(pick, NEG_INF, g)
        mask_t = jnp.where(keep > 0.0, 0.0, MASK_LOGIT)
        mask_logit = jnp.concatenate([jnp.zeros((HEAD_DIM, qt), F32), mask_t,
                                      jnp.zeros((LANES - HEAD_DIM - gate_rows, qt), F32)], axis=0).T
        q_aug.append(jnp.where(first, qh * (HEAD_DIM ** -0.5), mask_logit).astype(BF16))

    def lane_max(s):
        return jnp.maximum(s[:, :LANES], s[:, LANES:])

    def scores(h, t, dist):
        k_t = kaug_ref[h, pl.ds(pl.multiple_of(t * qt, qt), qt), :]
        s = lax.dot_general(q_aug[h], k_t, _NT, preferred_element_type=F32)
        tiles = []
        for kb in range(halves):
            rows = []
            for qb in range(halves):
                part = s[qb * blk:(qb + 1) * blk, kb * blk:(kb + 1) * blk]
                if isinstance(dist, int) and halves * dist + qb - kb < 0:
                    rows.append(part)
                else:
                    rows.append(part + toep_ref[h, halves * dist + qb - kb])
            tiles.append(jnp.concatenate(rows, axis=0))
        return tiles

    row = lax.broadcasted_iota(jnp.int32, (qt, blk), 0)
    col = lax.broadcasted_iota(jnp.int32, (qt, blk), 1)
    for h in range(HEADS_PER_STEP):
        top = None
        for kb, s in enumerate(scores(h, qi, 0)):
            s = jnp.where(col + kb * blk <= row, s, NEG_INF)
            s_ref[h, halves * qi + kb] = s
            top = lane_max(s) if top is None else jnp.maximum(top, lane_max(s))
        mvec_ref[h] = top

    def pass1(t, carry):
        for h in range(HEADS_PER_STEP):
            top = mvec_ref[h]
            for kb, s in enumerate(scores(h, t, qi - t)):
                s_ref[h, halves * t + kb] = s
                top = jnp.maximum(top, lane_max(s))
            mvec_ref[h] = top
        return carry

    lax.fori_loop(0, qi, pass1, 0)

    for h in range(HEADS_PER_STEP):
        m = jnp.max(mvec_ref[h], axis=-1, keepdims=True)
        mvec_ref[h] = jnp.broadcast_to(m, (qt, LANES))
        acc_ref[h] = jnp.zeros((qt, LANES), F32)

    def pass2(t, carry):
        keys = pl.ds(pl.multiple_of(t * qt, qt), qt)
        for h in range(HEADS_PER_STEP):
            m = mvec_ref[h]
            parts = []
            for kb in range(halves):
                s = s_ref[h, halves * t + kb]
                parts += [jnp.exp(s[:, :LANES] - m), jnp.exp(s[:, LANES:] - m)]
            p = jnp.concatenate(parts, axis=1).astype(BF16)
            acc_ref[h] += jnp.dot(p, vaug_ref[h, keys, :], preferred_element_type=F32)
        return carry

    lax.fori_loop(0, qi + 1, pass2, 0)

    out = [acc_ref[h] / acc_ref[h][:, HEAD_DIM:HEAD_DIM + 1] for h in range(HEADS_PER_STEP)]
    o_ref[0] = jnp.where(first, out[0], pltpu.roll(out[1], HEAD_DIM, 1)).astype(o_ref.dtype)


def _moba(proj, g_q, g_k, rel_rev):
    bsz, s, _ = proj.shape
    blk = ATT_BLOCK
    n_blk = s // blk
    n_hp = N_HEADS // HEADS_PER_STEP
    w = N_HEADS * HEAD_DIM
    tile2 = lambda g: jnp.tile(g.reshape(1, HEAD_DIM), (1, HEADS_PER_STEP))
    qt = MOBA_Q_TILE
    return pl.pallas_call(
        functools.partial(_moba_kernel, n_blk=n_blk),
        grid=(n_hp, bsz, s // qt),
        in_specs=[
            pl.BlockSpec((1, qt, LANES), lambda hp, b, i: (b, i, hp)),
            pl.BlockSpec((1, s, LANES), lambda hp, b, i: (b, 0, n_hp + hp)),
            pl.BlockSpec((1, s, LANES), lambda hp, b, i: (b, 0, 2 * n_hp + hp)),
            pl.BlockSpec((1, LANES), lambda hp, b, i: (0, 0)),
            pl.BlockSpec((1, LANES), lambda hp, b, i: (0, 0)),
            pl.BlockSpec((1, HEADS_PER_STEP, rel_rev.shape[-1]), lambda hp, b, i: (hp, 0, 0)),
        ],
        out_specs=pl.BlockSpec((1, qt, LANES), lambda hp, b, i: (b, i, hp)),
        out_shape=jax.ShapeDtypeStruct((bsz, s, w), BF16),
        scratch_shapes=[
            pltpu.VMEM((HEADS_PER_STEP, s, LANES), BF16),
            pltpu.VMEM((HEADS_PER_STEP, s, LANES), BF16),
            pltpu.VMEM((HEADS_PER_STEP, -(-n_blk // 8) * 8, LANES), F32),
            pltpu.VMEM((HEADS_PER_STEP, n_blk, blk, blk), F32),
            pltpu.VMEM((HEADS_PER_STEP, n_blk, qt, blk), F32),
            pltpu.VMEM((HEADS_PER_STEP, qt, LANES), F32),
            pltpu.VMEM((HEADS_PER_STEP, qt, LANES), F32),
        ],
        compiler_params=_params("arbitrary", "arbitrary", "arbitrary"),
        name="moba",
    )(proj, proj, proj, tile2(g_q), tile2(g_k), rel_rev)


def _sb_kernel(q_ref, k_ref, v_ref, o_ref, acc_ref, carry_ref):
    ti = pl.program_id(2)
    blk = ATT_BLOCK
    n_sub = q_ref.shape[1] // blk
    first = n_sub * ti
    row = lax.broadcasted_iota(jnp.int32, (blk, blk), 0)
    col = lax.broadcasted_iota(jnp.int32, (blk, blk), 1)
    past = col < row
    later = _indicator(row > col, BF16)

    q_heads = []
    for sub in range(n_sub):
        heads = _split_heads(q_ref[0, sub * blk:(sub + 1) * blk, :].astype(F32) * (HEAD_DIM ** -0.5))
        q_heads.append([t.astype(BF16) for t in heads])

    def walk(jobs, fresh):
        for sub, blocks in jobs:
            for h in range(HEADS_PER_STEP):
                carry = None if fresh else carry_ref[sub, h]
                total = None if fresh else acc_ref[sub, h]
                for j, diagonal in blocks:
                    start = j * blk if isinstance(j, int) else pl.multiple_of(j * blk, blk)
                    z = lax.dot_general(q_heads[sub][h], k_ref[0, pl.ds(start, blk), :], _NT,
                                        preferred_element_type=F32)
                    log_1m = -(jnp.maximum(z, 0.0) + jnp.log(1.0 + jnp.exp(-jnp.abs(z))))
                    masked = jnp.where(past, log_1m, 0.0) if diagonal else log_1m
                    hi = masked.astype(BF16)
                    lo = (masked - hi.astype(F32)).astype(BF16)
                    after = (jnp.dot(hi, later, preferred_element_type=F32)
                             + jnp.dot(lo, later, preferred_element_type=F32))
                    row_sum = after[:, 0:1] + masked[:, 0:1]
                    if carry is not None:
                        after = after + carry
                    a = jnp.exp(z + log_1m + after)
                    if diagonal:
                        a = jnp.where(past, a, 0.0)
                    pv = jnp.dot(a.astype(BF16), v_ref[0, pl.ds(start, blk), :], preferred_element_type=F32)
                    total = pv if total is None else total + pv
                    carry = row_sum if carry is None else carry + row_sum
                carry_ref[sub, h] = carry
                acc_ref[sub, h] = total

    @pl.when(ti == 0)
    def _first_step():
        walk([(sub, [(sub, True)] + ([(sub - 1, False)] if sub else [])) for sub in range(n_sub)], True)

    @pl.when(ti > 0)
    def _own_and_previous():
        walk([(sub, [(first + sub, True), (first + sub - 1, False)]) for sub in range(n_sub)], True)

    for sub in range(n_sub):
        def slowest_decay(sub=sub):
            return jnp.maximum(jnp.max(carry_ref[sub, 0]), jnp.max(carry_ref[sub, 1]))

        def cond(state):
            j, worst = state
            return jnp.logical_and(j >= 0, worst > SB_EXIT_LOG)

        def body(state, sub=sub, slowest_decay=slowest_decay):
            j, _ = state
            walk([(sub, [(j, False)])], False)
            return j - 1, slowest_decay()

        lax.while_loop(cond, body, (first + sub - 2, slowest_decay()))
        o_ref[0, sub * blk:(sub + 1) * blk, :] = jnp.where(
            _head_lane_mask(), acc_ref[sub, 0], acc_ref[sub, 1]).astype(o_ref.dtype)


def _stick_breaking(proj):
    bsz, s, _ = proj.shape
    blk = ATT_BLOCK
    qt = SB_Q_BLOCKS * blk
    n_hp = N_HEADS // HEADS_PER_STEP
    w = N_HEADS * HEAD_DIM
    base = 3 * n_hp
    return pl.pallas_call(
        _sb_kernel,
        grid=(n_hp, bsz, s // qt),
        in_specs=[
            pl.BlockSpec((1, qt, LANES), lambda hp, b, i: (b, i, base + hp)),
            pl.BlockSpec((1, s, LANES), lambda hp, b, i: (b, 0, base + n_hp + hp)),
            pl.BlockSpec((1, s, LANES), lambda hp, b, i: (b, 0, base + 2 * n_hp + hp)),
        ],
        out_specs=pl.BlockSpec((1, qt, LANES), lambda hp, b, i: (b, i, hp)),
        out_shape=jax.ShapeDtypeStruct((bsz, s, w), BF16),
        scratch_shapes=[
            pltpu.VMEM((SB_Q_BLOCKS, HEADS_PER_STEP, blk, LANES), F32),
            pltpu.VMEM((SB_Q_BLOCKS, HEADS_PER_STEP, blk, 1), F32),
        ],
        compiler_params=_params("arbitrary", "arbitrary", "arbitrary"),
        name="stickbrk",
    )(proj, proj, proj)


def _pack_rows(t):
    n = t.shape[1] // 2
    bits = pltpu.bitcast(t.astype(BF16).astype(F32), U32)
    return bits[:, :n] | (bits[:, n:] >> 16)


def _unpack_rows(p):
    hi = pltpu.bitcast(p & jnp.uint32(0xFFFF0000), F32)
    lo = pltpu.bitcast(p << 16, F32)
    return jnp.concatenate([hi, lo], axis=1)


def _merge_kernel(x_ref, oa_ref, ob_ref, ga_ref, gb_ref, mod_ref, wbm_ref, wbs_ref, wo_ref, gf_ref,
                  wrh_ref, wrl_ref, br_ref, x1_ref, h2_ref, lg_ref):
    ma = jnp.dot(oa_ref[0], wbm_ref[...], preferred_element_type=F32)
    mb = jnp.dot(ob_ref[0], wbs_ref[...], preferred_element_type=F32)
    merged = (jax.nn.sigmoid(ga_ref[0].astype(F32)) * ma + jax.nn.sigmoid(gb_ref[0].astype(F32)) * mb)
    t = jnp.dot(merged.astype(BF16), wo_ref[...], preferred_element_type=F32)
    x1 = x_ref[0] + mod_ref[0, 2:3, :] * t
    x1_ref[0] = x1
    h2 = _rms_modulate(x1, gf_ref[...], mod_ref[0, 3:4, :], mod_ref[0, 4:5, :])
    hi = h2.astype(BF16)
    h2_ref[0] = hi
    lo = (h2 - hi.astype(F32)).astype(BF16)
    lg_ref[0] = (jnp.dot(hi, wrh_ref[...], preferred_element_type=F32)
                 + jnp.dot(lo, wrh_ref[...], preferred_element_type=F32)
                 + jnp.dot(hi, wrl_ref[...], preferred_element_type=F32) + br_ref[...])


def _merge(x, o_a, o_b, proj, mod, wbm, wbs, wo, g_ffn, w_router, b_router):
    bsz, s, d = x.shape
    w = o_a.shape[-1]
    tm = TOKEN_TILE
    gate_blk = (proj.shape[-1] - 2 * d) // d
    const = lambda shape: pl.BlockSpec(shape, lambda b, i: (0,) * len(shape))
    tok = lambda width: pl.BlockSpec((1, tm, width), lambda b, i: (b, i, 0))
    wr_hi = w_router.astype(BF16)
    wr_lo = (w_router - wr_hi.astype(F32)).astype(BF16)
    return pl.pallas_call(
        _merge_kernel,
        grid=(bsz, s // tm),
        in_specs=[
            tok(d), tok(w), tok(w),
            pl.BlockSpec((1, tm, d), lambda b, i: (b, i, gate_blk)),
            pl.BlockSpec((1, tm, d), lambda b, i: (b, i, gate_blk + 1)),
            pl.BlockSpec((1, N_MOD, d), lambda b, i: (b, 0, 0)),
            const((w, d)), const((w, d)), const((d, d)), const((1, d)),
            const((d, LANES)), const((d, LANES)), const((1, LANES)),
        ],
        out_specs=[tok(d), tok(d), tok(LANES)],
        out_shape=[
            jax.ShapeDtypeStruct((bsz, s, d), F32),
            jax.ShapeDtypeStruct((bsz, s, d), BF16),
            jax.ShapeDtypeStruct((bsz, s, LANES), F32),
        ],
        compiler_params=_params("arbitrary", "arbitrary"),
        name="merge",
    )(x, o_a, o_b, proj, proj, mod, wbm, wbs, wo, g_ffn.reshape(1, d), wr_hi, wr_lo, b_router)


ROUTE_COLS = 8
SEG_ALIGN = 8
SEG_SHIFT = 3


def _local_rows(tm):
    return 2 * tm + N_EXPERTS * SEG_ALIGN


def _route_kernel(lg_ref, rt_ref, slot_t_ref, cnt_ref, before_ref, carry_ref):
    i = pl.program_id(0)
    tm = lg_ref.shape[0]
    lane = _lane_index()

    @pl.when(i == 0)
    def _init():
        carry_ref[...] = jnp.zeros_like(carry_ref)

    lg = lg_ref[...]

    def first_max(vals):
        top = jnp.max(vals, axis=-1, keepdims=True)
        idx = jnp.min(jnp.where(vals == top, lane, float(LANES)), axis=-1, keepdims=True)
        return top, idx

    g_logit = jnp.where(lane < N_GROUPS, lg, NEG_INF)
    g_top, g_idx = first_max(g_logit)
    g_w = 1.0 / jnp.sum(jnp.exp(g_logit - g_top), axis=-1, keepdims=True)
    lo = N_GROUPS + g_idx * EXPERTS_PER_GROUP
    e_logit = jnp.where(jnp.logical_and(lane >= lo, lane < lo + EXPERTS_PER_GROUP), lg, NEG_INF)
    e_top0, lane0 = first_max(e_logit)
    e_top1, lane1 = first_max(jnp.where(lane == lane0, NEG_INF, e_logit))
    z = jnp.exp(e_top1 - e_top0)
    w0 = g_w / (1.0 + z)
    w1 = g_w * z / (1.0 + z)

    hot = jnp.logical_or(lane == lane0, lane == lane1)
    row = lax.broadcasted_iota(jnp.int32, (tm, tm), 0)
    col = lax.broadcasted_iota(jnp.int32, (tm, tm), 1)
    earlier = jnp.dot(_indicator(col < row, BF16), _indicator(hot, BF16), preferred_element_type=F32)
    cnt = jnp.sum(_indicator(hot, F32), axis=0, keepdims=True)
    chunks = jnp.ceil(cnt * (1.0 / SEG_ALIGN))
    src = lax.broadcasted_iota(jnp.int32, (LANES, LANES), 0)
    dst = lax.broadcasted_iota(jnp.int32, (LANES, LANES), 1)
    seg_start = SEG_ALIGN * jnp.dot(jnp.broadcast_to(chunks, (8, LANES)).astype(BF16), _indicator(src < dst, BF16),
                                    preferred_element_type=F32)[0:1]
    local = seg_start + earlier
    slot0 = jnp.sum(jnp.where(lane == lane0, local, 0.0), axis=-1, keepdims=True)
    slot1 = jnp.sum(jnp.where(lane == lane1, local, 0.0), axis=-1, keepdims=True)
    cnt_ref[0] = SEG_ALIGN * chunks
    before_ref[0] = carry_ref[...]
    carry_ref[...] += SEG_ALIGN * chunks

    packed = jnp.zeros((tm, LANES), F32)
    for c, v in enumerate([slot0, slot1, w0, w1]):
        packed = jnp.where(lane == float(c), v, packed)
    rt_ref[...] = packed[:, :ROUTE_COLS]
    slot_t_ref[0] = packed.T[:ROUTE_COLS, :]


def _route(logits):
    n = logits.shape[0]
    tm = TOKEN_TILE
    tiles = n // tm
    per_tile = lambda rows, cols: (pl.BlockSpec((1, rows, cols), lambda i: (i, 0, 0)),
                                   jax.ShapeDtypeStruct((tiles, rows, cols), F32))
    specs, shapes = zip((pl.BlockSpec((tm, ROUTE_COLS), lambda i: (i, 0)), jax.ShapeDtypeStruct((n, ROUTE_COLS), F32)),
                        per_tile(ROUTE_COLS, tm), per_tile(1, LANES), per_tile(1, LANES))
    return pl.pallas_call(
        _route_kernel,
        grid=(tiles,),
        in_specs=[pl.BlockSpec((tm, LANES), lambda i: (i, 0))],
        out_specs=list(specs),
        out_shape=list(shapes),
        scratch_shapes=[pltpu.VMEM((1, LANES), F32)],
        compiler_params=_params("arbitrary"),
        name="route",
    )(logits)


def _for_each_chunk(tile, cnt_ref, row0_ref, visit):
    def per_expert(e, local):
        chunks = lax.shift_right_logical(cnt_ref[tile * N_EXPERTS + e], SEG_SHIFT)
        row0 = row0_ref[tile * N_EXPERTS + e]

        def per_chunk(c, carry):
            visit(pl.multiple_of(local + c * SEG_ALIGN, SEG_ALIGN), pl.multiple_of(row0 + c * SEG_ALIGN, SEG_ALIGN))
            return carry

        lax.fori_loop(0, chunks, per_chunk, 0)
        return local + chunks * SEG_ALIGN

    lax.fori_loop(0, N_EXPERTS, per_expert, 0)


def _dispatch_kernel(cnt_ref, row0_ref, slot_t_ref, h_ref, xs_in_ref, xs_ref, loc_ref, sems):
    del xs_in_ref
    tile = pl.program_id(0)
    n_loc, tm = loc_ref.shape[1], h_ref.shape[0]
    buf = tile & 1

    slot = lax.broadcasted_iota(jnp.int32, (n_loc, tm), 0).astype(F32)
    place = jnp.logical_or(slot == slot_t_ref[0, 0:1, :], slot == slot_t_ref[0, 1:2, :])
    loc_ref[buf] = _pack_rows(jnp.dot(_indicator(place, BF16), h_ref[...], preferred_element_type=F32))

    def chunk_copy(b):
        return lambda local, row: pltpu.make_async_copy(
            loc_ref.at[b, pl.ds(local, SEG_ALIGN), :], xs_ref.at[pl.ds(row, SEG_ALIGN), :], sems.at[b])

    _for_each_chunk(tile, cnt_ref, row0_ref, lambda l, r: chunk_copy(buf)(l, r).start())

    @pl.when(tile > 0)
    def _previous_done():
        _for_each_chunk(tile - 1, cnt_ref, row0_ref, lambda l, r: chunk_copy(1 - buf)(l, r).wait())

    @pl.when(tile == pl.num_programs(0) - 1)
    def _last_done():
        _for_each_chunk(tile, cnt_ref, row0_ref, lambda l, r: chunk_copy(buf)(l, r).wait())


def _dispatch(cnt_flat, row0_flat, slot_t, h2, n_rows):
    n, d = h2.shape
    tm = TOKEN_TILE
    xs0 = jnp.zeros((n_rows, d // 2), U32)
    grid_spec = pltpu.PrefetchScalarGridSpec(
        num_scalar_prefetch=2,
        grid=(n // tm,),
        in_specs=[
            pl.BlockSpec((1, ROUTE_COLS, tm), lambda i, c, r: (i, 0, 0)),
            pl.BlockSpec((tm, d), lambda i, c, r: (i, 0)),
            pl.BlockSpec(memory_space=pl.ANY),
        ],
        out_specs=pl.BlockSpec(memory_space=pl.ANY),
        scratch_shapes=[pltpu.VMEM((2, _local_rows(tm), d // 2), U32), pltpu.SemaphoreType.DMA((2,))],
    )
    return pl.pallas_call(
        _dispatch_kernel,
        grid_spec=grid_spec,
        out_shape=jax.ShapeDtypeStruct((n_rows, d // 2), U32),
        input_output_aliases={4: 0},
        compiler_params=_params("arbitrary"),
        name="dispatch",
    )(cnt_flat, row0_flat, slot_t, h2, xs0)


def _expert_kernel(blk_e_ref, n_used_ref, xs_ref, wg_ref, wu_ref, wd_ref, ys_ref, wg_bf, wu_bf, wd_bf):
    i = pl.program_id(0)
    used = i < n_used_ref[0]

    @pl.when(jnp.logical_and(used, jnp.logical_or(i == 0, blk_e_ref[i] != blk_e_ref[jnp.maximum(i - 1, 0)])))
    def _new_expert():
        wg_bf[...] = wg_ref[0].astype(BF16)
        wu_bf[...] = wu_ref[0].astype(BF16)
        wd_bf[...] = wd_ref[0].astype(BF16)

    @pl.when(used)
    def _compute():
        xb = _unpack_rows(xs_ref[...]).astype(BF16)
        g = jnp.dot(xb, wg_bf[...], preferred_element_type=F32)
        u = jnp.dot(xb, wu_bf[...], preferred_element_type=F32)
        hid = (g * jax.nn.sigmoid(g) * u).astype(BF16)
        ys_ref[...] = _pack_rows(jnp.dot(hid, wd_bf[...], preferred_element_type=F32))

    @pl.when(i >= n_used_ref[0])
    def _unused():
        ys_ref[...] = jnp.zeros_like(ys_ref)


def _experts(blk_e, n_used, xs, wg, wu, wd):
    n_rows, half = xs.shape
    _, d, de = wg.shape
    grid_spec = pltpu.PrefetchScalarGridSpec(
        num_scalar_prefetch=2,
        grid=(n_rows // ROW_BLOCK,),
        in_specs=[
            pl.BlockSpec((ROW_BLOCK, half), lambda i, e, u: (i, 0)),
            pl.BlockSpec((1, d, de), lambda i, e, u: (e[i], 0, 0)),
            pl.BlockSpec((1, d, de), lambda i, e, u: (e[i], 0, 0)),
            pl.BlockSpec((1, de, d), lambda i, e, u: (e[i], 0, 0)),
        ],
        out_specs=pl.BlockSpec((ROW_BLOCK, half), lambda i, e, u: (i, 0)),
        scratch_shapes=[pltpu.VMEM((d, de), BF16), pltpu.VMEM((d, de), BF16), pltpu.VMEM((de, d), BF16)],
    )
    return pl.pallas_call(
        _expert_kernel,
        grid_spec=grid_spec,
        out_shape=jax.ShapeDtypeStruct((n_rows, half), U32),
        compiler_params=_params("arbitrary"),
        name="experts",
    )(blk_e, n_used, xs, wg, wu, wd)


def _combine_kernel(cnt_ref, row0_ref, x1_ref, mod_ref, rt_ref, ys_ref, o_ref, loc_ref, sems):
    tm = x1_ref.shape[1]
    n_loc = loc_ref.shape[1]
    tile = pl.program_id(0) * pl.num_programs(1) + pl.program_id(1)
    n_tiles = pl.num_programs(0) * pl.num_programs(1)
    buf = tile & 1

    def chunk_copy(b):
        return lambda local, row: pltpu.make_async_copy(
            ys_ref.at[pl.ds(row, SEG_ALIGN), :], loc_ref.at[b, pl.ds(local, SEG_ALIGN), :], sems.at[b])

    @pl.when(tile == 0)
    def _init():
        loc_ref[...] = jnp.zeros_like(loc_ref)
        _for_each_chunk(tile, cnt_ref, row0_ref, lambda l, r: chunk_copy(buf)(l, r).start())

    @pl.when(tile + 1 < n_tiles)
    def _fetch_next():
        _for_each_chunk(tile + 1, cnt_ref, row0_ref, lambda l, r: chunk_copy(1 - buf)(l, r).start())

    _for_each_chunk(tile, cnt_ref, row0_ref, lambda l, r: chunk_copy(buf)(l, r).wait())

    y = _unpack_rows(loc_ref[buf]).astype(BF16)
    rt = rt_ref[0]
    slot = lax.broadcasted_iota(jnp.int32, (tm, n_loc), 1).astype(F32)
    y0 = jnp.dot(_indicator(slot == rt[:, 0:1], BF16), y, preferred_element_type=F32)
    y1 = jnp.dot(_indicator(slot == rt[:, 1:2], BF16), y, preferred_element_type=F32)
    o_ref[0] = x1_ref[0] + mod_ref[0, 5:6, :] * (rt[:, 2:3] * y0 + rt[:, 3:4] * y1)


def _combine(cnt_flat, row0_flat, x1, mod, rt, ys):
    bsz, s, d = x1.shape
    tm = TOKEN_TILE
    grid_spec = pltpu.PrefetchScalarGridSpec(
        num_scalar_prefetch=2,
        grid=(bsz, s // tm),
        in_specs=[
            pl.BlockSpec((1, tm, d), lambda b, i, c, r: (b, i, 0)),
            pl.BlockSpec((1, N_MOD, d), lambda b, i, c, r: (b, 0, 0)),
            pl.BlockSpec((1, tm, ROUTE_COLS), lambda b, i, c, r: (b, i, 0)),
            pl.BlockSpec(memory_space=pl.ANY),
        ],
        out_specs=pl.BlockSpec((1, tm, d), lambda b, i, c, r: (b, i, 0)),
        scratch_shapes=[pltpu.VMEM((2, _local_rows(tm), d // 2), U32), pltpu.SemaphoreType.DMA((2,))],
    )
    return pl.pallas_call(
        _combine_kernel,
        grid_spec=grid_spec,
        out_shape=jax.ShapeDtypeStruct((bsz, s, d), F32),
        compiler_params=_params("arbitrary", "arbitrary"),
        name="combine",
    )(cnt_flat, row0_flat, x1, mod, rt, ys)


def _layer(x, mod, g_mix, w_in, g_q, g_k, rel_bias, w_br_moba, w_br_sb, w_out, g_ffn,
           w_rg, b_rg, w_re, b_re, w_gate, w_up, w_down):
    bsz, s, d = x.shape
    n = bsz * s
    n_blk = s // ATT_BLOCK

    proj = _inproj(x, mod, g_mix, w_in.astype(BF16))

    top = (n_blk + 1) * ATT_BLOCK - 1
    dist = np.maximum(top - np.arange((n_blk + 2) * ATT_BLOCK), 0)
    rel_rev = rel_bias[:, _rel_bucket_table(top)[dist]].reshape(N_HEADS // HEADS_PER_STEP, HEADS_PER_STEP, -1)

    o_a = _moba(proj, g_q, g_k, rel_rev)
    o_b = _stick_breaking(proj)

    pad = LANES - N_GROUPS - N_EXPERTS
    w_router = jnp.concatenate([w_rg, w_re, jnp.zeros((d, pad), F32)], axis=1)
    b_router = jnp.concatenate([b_rg, b_re, jnp.zeros((pad,), F32)]).reshape(1, LANES)
    x1, h2, logits = _merge(x, o_a, o_b, proj, mod, w_br_moba.astype(BF16), w_br_sb.astype(BF16),
                            w_out.astype(BF16), g_ffn, w_router, b_router)

    rt, slot_t, tile_cnt, tile_before = _route(logits.reshape(n, LANES))

    experts = slice(N_GROUPS, N_GROUPS + N_EXPERTS)
    cnt = tile_cnt[:, 0, experts].astype(jnp.int32)
    total = jnp.sum(cnt, axis=0)
    padded = (total + ROW_BLOCK - 1) // ROW_BLOCK * ROW_BLOCK
    pend = jnp.cumsum(padded)
    pstart = pend - padded
    max_rows = 2 * n + cnt.shape[0] * N_EXPERTS * (SEG_ALIGN - 1) + N_EXPERTS * (ROW_BLOCK - 1)
    n_blocks = -(-max_rows // ROW_BLOCK)
    blk_row0 = jnp.arange(n_blocks, dtype=jnp.int32) * ROW_BLOCK
    blk_e = jnp.minimum(jnp.sum((blk_row0[:, None] >= pend[None, :]).astype(jnp.int32), axis=1), N_EXPERTS - 1)
    n_used = (pend[-1] // ROW_BLOCK).astype(jnp.int32).reshape(1)
    seg_row0 = (pstart[None, :] + tile_before[:, 0, experts].astype(jnp.int32)).reshape(-1)
    cnt_flat = cnt.reshape(-1)

    xs = _dispatch(cnt_flat, seg_row0, slot_t, h2.reshape(n, d), n_blocks * ROW_BLOCK)
    ys = _experts(blk_e, n_used, xs, w_gate, w_up, w_down)
    return _combine(cnt_flat, seg_row0, x1, mod, rt.reshape(bsz, s, ROUTE_COLS), ys)


def kernel(x, c, w_ada, b_ada, g_mix, w_in, g_q, g_k, rel_bias, w_br_moba, w_br_sb, w_out, g_ffn,
           w_rg, b_rg, w_re, b_re, w_gate, w_up, w_down):
    bsz, s, d = x.shape
    assert s % TOKEN_TILE == 0 and s % MOBA_Q_TILE == 0 and d % (2 * LANES) == 0
    for l in range(w_ada.shape[0]):
        mod = _adaln(c, w_ada[l], b_ada[l]).reshape(bsz, N_MOD, d)
        x = _layer(x, mod, g_mix[l], w_in[l], g_q[l], g_k[l], rel_bias, w_br_moba[l], w_br_sb[l], w_out[l],
                   g_ffn[l], w_rg[l], b_rg[l], w_re[l], b_re[l], w_gate[l], w_up[l], w_down[l])
    return x
```

```python
import functools
import math

import numpy as np
import jax
import jax.numpy as jnp
from jax import lax
from jax.experimental import pallas as pl
from jax.experimental.pallas import tpu as pltpu

F32 = jnp.float32
BF16 = jnp.bfloat16
U32 = jnp.uint32
HIGHEST = lax.Precision.HIGHEST

LANES = 128
VMEM_LIMIT_BYTES = 56 * 1024 * 1024

HEAD_DIM = 64
N_HEADS = 8
HEADS_PER_STEP = LANES // HEAD_DIM
ATT_BLOCK = 256
MOBA_Q_TILE = 2 * ATT_BLOCK
SB_Q_BLOCKS = 2
MOBA_TOPK = 3
REL_BUCKETS = 32
REL_MAX_DIST = 1024
N_GROUPS = 4
EXPERTS_PER_GROUP = 8
N_EXPERTS = N_GROUPS * EXPERTS_PER_GROUP
N_MOD = 6
RMS_EPS = 1e-6
ROW_BLOCK = 512
TOKEN_TILE = 512
NEG_INF = float("-inf")
MASK_LOGIT = -1e30
SB_EXIT_LOG = -110.0

_NT = (((1,), (1,)), ((), ()))


def _params(*semantics):
    return pltpu.CompilerParams(dimension_semantics=semantics, vmem_limit_bytes=VMEM_LIMIT_BYTES)


def _adaln_kernel(c_ref, w_ref, b_ref, o_ref):
    c = c_ref[...]
    s = c * jax.nn.sigmoid(c)
    o_ref[...] = jnp.dot(s, w_ref[...], preferred_element_type=F32, precision=HIGHEST) + b_ref[...]


def _adaln(c, w, b):
    bsz, d = c.shape
    n = w.shape[1]
    tn = 1536
    return pl.pallas_call(
        _adaln_kernel,
        grid=(n // tn,),
        in_specs=[
            pl.BlockSpec((bsz, d), lambda j: (0, 0)),
            pl.BlockSpec((d, tn), lambda j: (0, j)),
            pl.BlockSpec((1, tn), lambda j: (0, j)),
        ],
        out_specs=pl.BlockSpec((bsz, tn), lambda j: (0, j)),
        out_shape=jax.ShapeDtypeStruct((bsz, n), F32),
        compiler_params=_params("arbitrary"),
        name="adaln",
    )(c, w, b.reshape(1, n))


def _rms_modulate(x, g, shift, scale):
    y = x * lax.rsqrt(jnp.mean(x * x, axis=-1, keepdims=True) + RMS_EPS) * g
    return y * (1.0 + scale) + shift


def _inproj_kernel(x_ref, mod_ref, g_ref, w_ref, o_ref, *, col_chunk):
    h = _rms_modulate(x_ref[0], g_ref[...], mod_ref[0, 0:1, :], mod_ref[0, 1:2, :]).astype(BF16)
    for n in range(w_ref.shape[1] // col_chunk):
        cols = slice(n * col_chunk, (n + 1) * col_chunk)
        o_ref[0, :, cols] = jnp.dot(h, w_ref[:, cols], preferred_element_type=F32).astype(BF16)


def _inproj(x, mod, g, w_bf16):
    bsz, s, d = x.shape
    n = w_bf16.shape[1]
    tm = TOKEN_TILE
    return pl.pallas_call(
        functools.partial(_inproj_kernel, col_chunk=1024),
        grid=(bsz, s // tm),
        in_specs=[
            pl.BlockSpec((1, tm, d), lambda b, i: (b, i, 0)),
            pl.BlockSpec((1, N_MOD, d), lambda b, i: (b, 0, 0)),
            pl.BlockSpec((1, d), lambda b, i: (0, 0)),
            pl.BlockSpec((d, n), lambda b, i: (0, 0)),
        ],
        out_specs=pl.BlockSpec((1, tm, n), lambda b, i: (b, i, 0)),
        out_shape=jax.ShapeDtypeStruct((bsz, s, n), BF16),
        compiler_params=_params("arbitrary", "arbitrary"),
        name="inproj",
    )(x, mod, g.reshape(1, d), w_bf16)


def _indicator(cond, dtype):
    return jnp.where(cond, 1.0, 0.0).astype(dtype)


def _lane_index():
    return lax.broadcasted_iota(jnp.int32, (1, LANES), 1).astype(F32)


def _head_lane_mask():
    return lax.broadcasted_iota(jnp.int32, (1, LANES), 1) < HEAD_DIM


def _head_rmsnorm(t, g):
    first = _head_lane_mask()
    sq = t * t
    ss0 = jnp.sum(jnp.where(first, sq, 0.0), axis=-1, keepdims=True)
    ss1 = jnp.sum(jnp.where(first, 0.0, sq), axis=-1, keepdims=True)
    inv = jnp.where(first, lax.rsqrt(ss0 / HEAD_DIM + RMS_EPS), lax.rsqrt(ss1 / HEAD_DIM + RMS_EPS))
    return t * inv * g


def _split_heads(t):
    first = _head_lane_mask()
    zero = jnp.zeros_like(t)
    return jnp.where(first, t, zero), jnp.where(first, zero, t)


def _rel_bucket_table(max_dist):
    n = np.arange(max_dist + 1)
    max_exact = REL_BUCKETS // 2
    nf = np.maximum(n, 1).astype(np.float64)
    large = max_exact + (np.log(nf / max_exact) / math.log(REL_MAX_DIST / max_exact)
                         * (REL_BUCKETS - max_exact)).astype(np.int64)
    large = np.minimum(large, REL_BUCKETS - 1)
    return np.where(n < max_exact, n, large).astype(np.int32)


def _moba_kernel(q_ref, k_ref, v_ref, gq_ref, gk_ref, rb_ref, o_ref,
                 kaug_ref, vaug_ref, kmean_ref, toep_ref, s_ref, mvec_ref, acc_ref, *, n_blk):
    b = pl.program_id(1)
    qi = pl.program_id(2)
    blk = ATT_BLOCK
    qt = q_ref.shape[1]
    halves = qt // blk
    lane = _lane_index()
    first = _head_lane_mask()
    tag_lane = lane - float(HEAD_DIM)

    @pl.when(jnp.logical_and(b == 0, qi == 0))
    def _build_bias_tiles():
        for h in range(HEADS_PER_STEP):
            for d in range(n_blk):
                r = rb_ref[0, h:h + 1, (n_blk - d) * blk:(n_blk - d + 2) * blk]
                rolled = pltpu.roll(jnp.broadcast_to(r, (blk, 2 * blk)), blk + 1, 1, stride=1, stride_axis=0)
                toep_ref[h, d] = rolled[:, :blk]

    @pl.when(qi == 0)
    def _prep_keys():
        kmean_ref[...] = jnp.zeros_like(kmean_ref)
        for j in range(n_blk):
            kn = _head_rmsnorm(k_ref[0, j * blk:(j + 1) * blk, :].astype(F32), gk_ref[...])
            tag = jnp.where(tag_lane == float(j), 1.0, 0.0)
            for h, kh in enumerate((kn, pltpu.roll(kn, HEAD_DIM, 1))):
                kaug_ref[h, j * blk:(j + 1) * blk, :] = jnp.where(first, kh, tag).astype(BF16)
                kmean_ref[h, j:j + 1, :] = jnp.where(first, jnp.mean(kh, axis=0, keepdims=True), 0.0)
            vb = v_ref[0, j * blk:(j + 1) * blk, :].astype(F32)
            for h, vh in enumerate((vb, pltpu.roll(vb, HEAD_DIM, 1))):
                vaug_ref[h, j * blk:(j + 1) * blk, :] = jnp.where(first, vh, 1.0).astype(BF16)

    qn = _head_rmsnorm(q_ref[0].astype(F32), gq_ref[...])
    gate_rows = kmean_ref.shape[1]
    blk_row = lax.broadcasted_iota(jnp.int32, (gate_rows, qt), 0).astype(F32)
    own = (halves * qi + lax.shift_right_logical(lax.broadcasted_iota(jnp.int32, (gate_rows, qt), 1),
                                                 blk.bit_length() - 1)).astype(F32)
    q_aug = []
    for h, qh in enumerate((qn, pltpu.roll(qn, HEAD_DIM, 1))):
        qh = jnp.where(first, qh, 0.0)
        km = kmean_ref[h]
        km_hi, q_hi = km.astype(BF16), qh.astype(BF16)
        km_lo, q_lo = (km - km_hi.astype(F32)).astype(BF16), (qh - q_hi.astype(F32)).astype(BF16)
        g = (lax.dot_general(km_hi, q_hi, _NT, preferred_element_type=F32)
             + lax.dot_general(km_lo, q_hi, _NT, preferred_element_type=F32)
             + lax.dot_general(km_hi, q_lo, _NT, preferred_element_type=F32))
        g = jnp.where(blk_row < own, g, NEG_INF)
        keep = jnp.where(blk_row == own, 1.0, 0.0)
        for _ in range(MOBA_TOPK):
            top = jnp.max(g, axis=0, keepdims=True)
            is_top = jnp.logical_and(g == top, top > NEG_INF)
            idx = jnp.min(jnp.where(is_top, blk_row, float(gate_rows)), axis=0, keepdims=True)
            pick = blk_row == idx
            keep = jnp.where(pick, 1.0, keep)
            g = jnp.where(pick, NEG_INF, g)
        mask_t = jnp.where(keep > 0.0, 0.0, MASK_LOGIT)
        mask_logit = jnp.concatenate([jnp.zeros((HEAD_DIM, qt), F32), mask_t,
                                      jnp.zeros((LANES - HEAD_DIM - gate_rows, qt), F32)], axis=0).T
        q_aug.append(jnp.where(first, qh * (HEAD_DIM ** -0.5), mask_logit).astype(BF16))

    def lane_max(s):
        return jnp.maximum(s[:, :LANES], s[:, LANES:])

    def scores(h, t, dist):
        k_t = kaug_ref[h, pl.ds(pl.multiple_of(t * qt, qt), qt), :]
        s = lax.dot_general(q_aug[h], k_t, _NT, preferred_element_type=F32)
        tiles = []
        for kb in range(halves):
            rows = []
            for qb in range(halves):
                part = s[qb * blk:(qb + 1) * blk, kb * blk:(kb + 1) * blk]
                if isinstance(dist, int) and halves * dist + qb - kb < 0:
                    rows.append(part)
                else:
                    rows.append(part + toep_ref[h, halves * dist + qb - kb])
            tiles.append(jnp.concatenate(rows, axis=0))
        return tiles

    row = lax.broadcasted_iota(jnp.int32, (qt, blk), 0)
    col = lax.broadcasted_iota(jnp.int32, (qt, blk), 1)
    for h in range(HEADS_PER_STEP):
        top = None
        for kb, s in enumerate(scores(h, qi, 0)):
            s = jnp.where(col + kb * blk <= row, s, NEG_INF)
            s_ref[h, halves * qi + kb] = s
            top = lane_max(s) if top is None else jnp.maximum(top, lane_max(s))
        mvec_ref[h] = top

    def pass1(t, carry):
        for h in range(HEADS_PER_STEP):
            top = mvec_ref[h]
            for kb, s in enumerate(scores(h, t, qi - t)):
                s_ref[h, halves * t + kb] = s
                top = jnp.maximum(top, lane_max(s))
            mvec_ref[h] = top
        return carry

    lax.fori_loop(0, qi, pass1, 0)

    for h in range(HEADS_PER_STEP):
        m = jnp.max(mvec_ref[h], axis=-1, keepdims=True)
        mvec_ref[h] = jnp.broadcast_to(m, (qt, LANES))
        acc_ref[h] = jnp.zeros((qt, LANES), F32)

    def pass2(t, carry):
        keys = pl.ds(pl.multiple_of(t * qt, qt), qt)
        for h in range(HEADS_PER_STEP):
            m = mvec_ref[h]
            parts = []
            for kb in range(halves):
                s = s_ref[h, halves * t + kb]
                parts += [jnp.exp(s[:, :LANES] - m), jnp.exp(s[:, LANES:] - m)]
            p = jnp.concatenate(parts, axis=1).astype(BF16)
            acc_ref[h] += jnp.dot(p, vaug_ref[h, keys, :], preferred_element_type=F32)
        return carry

    lax.fori_loop(0, qi + 1, pass2, 0)

    out = [acc_ref[h] / acc_ref[h][:, HEAD_DIM:HEAD_DIM + 1] for h in range(HEADS_PER_STEP)]
    o_ref[0] = jnp.where(first, out[0], pltpu.roll(out[1], HEAD_DIM, 1)).astype(o_ref.dtype)


def _moba(proj, g_q, g_k, rel_rev):
    bsz, s, _ = proj.shape
    blk = ATT_BLOCK
    n_blk = s // blk
    n_hp = N_HEADS // HEADS_PER_STEP
    w = N_HEADS * HEAD_DIM
    tile2 = lambda g: jnp.tile(g.reshape(1, HEAD_DIM), (1, HEADS_PER_STEP))
    qt = MOBA_Q_TILE
    return pl.pallas_call(
        functools.partial(_moba_kernel, n_blk=n_blk),
        grid=(n_hp, bsz, s // qt),
        in_specs=[
            pl.BlockSpec((1, qt, LANES), lambda hp, b, i: (b, i, hp)),
            pl.BlockSpec((1, s, LANES), lambda hp, b, i: (b, 0, n_hp + hp)),
            pl.BlockSpec((1, s, LANES), lambda hp, b, i: (b, 0, 2 * n_hp + hp)),
            pl.BlockSpec((1, LANES), lambda hp, b, i: (0, 0)),
            pl.BlockSpec((1, LANES), lambda hp, b, i: (0, 0)),
            pl.BlockSpec((1, HEADS_PER_STEP, rel_rev.shape[-1]), lambda hp, b, i: (hp, 0, 0)),
        ],
        out_specs=pl.BlockSpec((1, qt, LANES), lambda hp, b, i: (b, i, hp)),
        out_shape=jax.ShapeDtypeStruct((bsz, s, w), BF16),
        scratch_shapes=[
            pltpu.VMEM((HEADS_PER_STEP, s, LANES), BF16),
            pltpu.VMEM((HEADS_PER_STEP, s, LANES), BF16),
            pltpu.VMEM((HEADS_PER_STEP, -(-n_blk // 8) * 8, LANES), F32),
            pltpu.VMEM((HEADS_PER_STEP, n_blk, blk, blk), F32),
            pltpu.VMEM((HEADS_PER_STEP, n_blk, qt, blk), F32),
            pltpu.VMEM((HEADS_PER_STEP, qt, LANES), F32),
            pltpu.VMEM((HEADS_PER_STEP, qt, LANES), F32),
        ],
        compiler_params=_params("arbitrary", "arbitrary", "arbitrary"),
        name="moba",
    )(proj, proj, proj, tile2(g_q), tile2(g_k), rel_rev)


def _sb_kernel(q_ref, k_ref, v_ref, o_ref, acc_ref, carry_ref):
    ti = pl.program_id(2)
    blk = ATT_BLOCK
    n_sub = q_ref.shape[1] // blk
    first = n_sub * ti
    row = lax.broadcasted_iota(jnp.int32, (blk, blk), 0)
    col = lax.broadcasted_iota(jnp.int32, (blk, blk), 1)
    past = col < row
    later = _indicator(row > col, BF16)

    q_heads = []
    for sub in range(n_sub):
        heads = _split_heads(q_ref[0, sub * blk:(sub + 1) * blk, :].astype(F32) * (HEAD_DIM ** -0.5))
        q_heads.append([t.astype(BF16) for t in heads])

    def walk(jobs, fresh):
        for sub, blocks in jobs:
            for h in range(HEADS_PER_STEP):
                carry = None if fresh else carry_ref[sub, h]
                total = None if fresh else acc_ref[sub, h]
                for j, diagonal in blocks:
                    start = j * blk if isinstance(j, int) else pl.multiple_of(j * blk, blk)
                    z = lax.dot_general(q_heads[sub][h], k_ref[0, pl.ds(start, blk), :], _NT,
                                        preferred_element_type=F32)
                    log_1m = -(jnp.maximum(z, 0.0) + jnp.log(1.0 + jnp.exp(-jnp.abs(z))))
                    masked = jnp.where(past, log_1m, 0.0) if diagonal else log_1m
                    hi = masked.astype(BF16)
                    lo = (masked - hi.astype(F32)).astype(BF16)
                    after = (jnp.dot(hi, later, preferred_element_type=F32)
                             + jnp.dot(lo, later, preferred_element_type=F32))
                    row_sum = after[:, 0:1] + masked[:, 0:1]
                    if carry is not None:
                        after = after + carry
                    a = jnp.exp(z + log_1m + after)
                    if diagonal:
                        a = jnp.where(past, a, 0.0)
                    pv = jnp.dot(a.astype(BF16), v_ref[0, pl.ds(start, blk), :], preferred_element_type=F32)
                    total = pv if total is None else total + pv
                    carry = row_sum if carry is None else carry + row_sum
                carry_ref[sub, h] = carry
                acc_ref[sub, h] = total

    @pl.when(ti == 0)
    def _first_step():
        walk([(sub, [(sub, True)] + ([(sub - 1, False)] if sub else [])) for sub in range(n_sub)], True)

    @pl.when(ti > 0)
    def _own_and_previous():
        walk([(sub, [(first + sub, True), (first + sub - 1, False)]) for sub in range(n_sub)], True)

    for sub in range(n_sub):
        def slowest_decay(sub=sub):
            return jnp.maximum(jnp.max(carry_ref[sub, 0]), jnp.max(carry_ref[sub, 1]))

        def cond(state):
            j, worst = state
            return jnp.logical_and(j >= 0, worst > SB_EXIT_LOG)

        def body(state, sub=sub, slowest_decay=slowest_decay):
            j, _ = state
            walk([(sub, [(j, False)])], False)
            return j - 1, slowest_decay()

        lax.while_loop(cond, body, (first + sub - 2, slowest_decay()))
        o_ref[0, sub * blk:(sub + 1) * blk, :] = jnp.where(
            _head_lane_mask(), acc_ref[sub, 0], acc_ref[sub, 1]).astype(o_ref.dtype)


def _stick_breaking(proj):
    bsz, s, _ = proj.shape
    blk = ATT_BLOCK
    qt = SB_Q_BLOCKS * blk
    n_hp = N_HEADS // HEADS_PER_STEP
    w = N_HEADS * HEAD_DIM
    base = 3 * n_hp
    return pl.pallas_call(
        _sb_kernel,
        grid=(n_hp, bsz, s // qt),
        in_specs=[
            pl.BlockSpec((1, qt, LANES), lambda hp, b, i: (b, i, base + hp)),
            pl.BlockSpec((1, s, LANES), lambda hp, b, i: (b, 0, base + n_hp + hp)),
            pl.BlockSpec((1, s, LANES), lambda hp, b, i: (b, 0, base + 2 * n_hp + hp)),
        ],
        out_specs=pl.BlockSpec((1, qt, LANES), lambda hp, b, i: (b, i, hp)),
        out_shape=jax.ShapeDtypeStruct((bsz, s, w), BF16),
        scratch_shapes=[
            pltpu.VMEM((SB_Q_BLOCKS, HEADS_PER_STEP, blk, LANES), F32),
            pltpu.VMEM((SB_Q_BLOCKS, HEADS_PER_STEP, blk, 1), F32),
        ],
        compiler_params=_params("arbitrary", "arbitrary", "arbitrary"),
        name="stickbrk",
    )(proj, proj, proj)


def _pack_rows(t):
    n = t.shape[1] // 2
    bits = pltpu.bitcast(t.astype(BF16).astype(F32), U32)
    return bits[:, :n] | (bits[:, n:] >> 16)


def _unpack_rows(p):
    hi = pltpu.bitcast(p & jnp.uint32(0xFFFF0000), F32)
    lo = pltpu.bitcast(p << 16, F32)
    return jnp.concatenate([hi, lo], axis=1)


def _merge_kernel(x_ref, oa_ref, ob_ref, ga_ref, gb_ref, mod_ref, wbm_ref, wbs_ref, wo_ref, gf_ref,
                  wrh_ref, wrl_ref, br_ref, x1_ref, h2_ref, lg_ref):
    ma = jnp.dot(oa_ref[0], wbm_ref[...], preferred_element_type=F32)
    mb = jnp.dot(ob_ref[0], wbs_ref[...], preferred_element_type=F32)
    merged = (jax.nn.sigmoid(ga_ref[0].astype(F32)) * ma + jax.nn.sigmoid(gb_ref[0].astype(F32)) * mb)
    t = jnp.dot(merged.astype(BF16), wo_ref[...], preferred_element_type=F32)
    x1 = x_ref[0] + mod_ref[0, 2:3, :] * t
    x1_ref[0] = x1
    h2 = _rms_modulate(x1, gf_ref[...], mod_ref[0, 3:4, :], mod_ref[0, 4:5, :])
    hi = h2.astype(BF16)
    h2_ref[0] = hi
    lo = (h2 - hi.astype(F32)).astype(BF16)
    lg_ref[0] = (jnp.dot(hi, wrh_ref[...], preferred_element_type=F32)
                 + jnp.dot(lo, wrh_ref[...], preferred_element_type=F32)
                 + jnp.dot(hi, wrl_ref[...], preferred_element_type=F32) + br_ref[...])


def _merge(x, o_a, o_b, proj, mod, wbm, wbs, wo, g_ffn, w_router, b_router):
    bsz, s, d = x.shape
    w = o_a.shape[-1]
    tm = TOKEN_TILE
    gate_blk = (proj.shape[-1] - 2 * d) // d
    const = lambda shape: pl.BlockSpec(shape, lambda b, i: (0,) * len(shape))
    tok = lambda width: pl.BlockSpec((1, tm, width), lambda b, i: (b, i, 0))
    wr_hi = w_router.astype(BF16)
    wr_lo = (w_router - wr_hi.astype(F32)).astype(BF16)
    return pl.pallas_call(
        _merge_kernel,
        grid=(bsz, s // tm),
        in_specs=[
            tok(d), tok(w), tok(w),
            pl.BlockSpec((1, tm, d), lambda b, i: (b, i, gate_blk)),
            pl.BlockSpec((1, tm, d), lambda b, i: (b, i, gate_blk + 1)),
            pl.BlockSpec((1, N_MOD, d), lambda b, i: (b, 0, 0)),
            const((w, d)), const((w, d)), const((d, d)), const((1, d)),
            const((d, LANES)), const((d, LANES)), const((1, LANES)),
        ],
        out_specs=[tok(d), tok(d), tok(LANES)],
        out_shape=[
            jax.ShapeDtypeStruct((bsz, s, d), F32),
            jax.ShapeDtypeStruct((bsz, s, d), BF16),
            jax.ShapeDtypeStruct((bsz, s, LANES), F32),
        ],
        compiler_params=_params("arbitrary", "arbitrary"),
        name="merge",
    )(x, o_a, o_b, proj, proj, mod, wbm, wbs, wo, g_ffn.reshape(1, d), wr_hi, wr_lo, b_router)


ROUTE_COLS = 8
SEG_ALIGN = 8
SEG_SHIFT = 3


def _local_rows(tm):
    return 2 * tm + N_EXPERTS * SEG_ALIGN


def _route_kernel(lg_ref, rt_ref, slot_t_ref, cnt_ref, before_ref, carry_ref):
    i = pl.program_id(0)
    tm = lg_ref.shape[0]
    lane = _lane_index()

    @pl.when(i == 0)
    def _init():
        carry_ref[...] = jnp.zeros_like(carry_ref)

    lg = lg_ref[...]

    def first_max(vals):
        top = jnp.max(vals, axis=-1, keepdims=True)
        idx = jnp.min(jnp.where(vals == top, lane, float(LANES)), axis=-1, keepdims=True)
        return top, idx

    g_logit = jnp.where(lane < N_GROUPS, lg, NEG_INF)
    g_top, g_idx = first_max(g_logit)
    g_w = 1.0 / jnp.sum(jnp.exp(g_logit - g_top), axis=-1, keepdims=True)
    lo = N_GROUPS + g_idx * EXPERTS_PER_GROUP
    e_logit = jnp.where(jnp.logical_and(lane >= lo, lane < lo + EXPERTS_PER_GROUP), lg, NEG_INF)
    e_top0, lane0 = first_max(e_logit)
    e_top1, lane1 = first_max(jnp.where(lane == lane0, NEG_INF, e_logit))
    z = jnp.exp(e_top1 - e_top0)
    w0 = g_w / (1.0 + z)
    w1 = g_w * z / (1.0 + z)

    hot = jnp.logical_or(lane == lane0, lane == lane1)
    row = lax.broadcasted_iota(jnp.int32, (tm, tm), 0)
    col = lax.broadcasted_iota(jnp.int32, (tm, tm), 1)
    earlier = jnp.dot(_indicator(col < row, BF16), _indicator(hot, BF16), preferred_element_type=F32)
    cnt = jnp.sum(_indicator(hot, F32), axis=0, keepdims=True)
    chunks = jnp.ceil(cnt * (1.0 / SEG_ALIGN))
    src = lax.broadcasted_iota(jnp.int32, (LANES, LANES), 0)
    dst = lax.broadcasted_iota(jnp.int32, (LANES, LANES), 1)
    seg_start = SEG_ALIGN * jnp.dot(jnp.broadcast_to(chunks, (8, LANES)).astype(BF16), _indicator(src < dst, BF16),
                                    preferred_element_type=F32)[0:1]
    local = seg_start + earlier
    slot0 = jnp.sum(jnp.where(lane == lane0, local, 0.0), axis=-1, keepdims=True)
    slot1 = jnp.sum(jnp.where(lane == lane1, local, 0.0), axis=-1, keepdims=True)
    cnt_ref[0] = SEG_ALIGN * chunks
    before_ref[0] = carry_ref[...]
    carry_ref[...] += SEG_ALIGN * chunks

    packed = jnp.zeros((tm, LANES), F32)
    for c, v in enumerate([slot0, slot1, w0, w1]):
        packed = jnp.where(lane == float(c), v, packed)
    rt_ref[...] = packed[:, :ROUTE_COLS]
    slot_t_ref[0] = packed.T[:ROUTE_COLS, :]


def _route(logits):
    n = logits.shape[0]
    tm = TOKEN_TILE
    tiles = n // tm
    per_tile = lambda rows, cols: (pl.BlockSpec((1, rows, cols), lambda i: (i, 0, 0)),
                                   jax.ShapeDtypeStruct((tiles, rows, cols), F32))
    specs, shapes = zip((pl.BlockSpec((tm, ROUTE_COLS), lambda i: (i, 0)), jax.ShapeDtypeStruct((n, ROUTE_COLS), F32)),
                        per_tile(ROUTE_COLS, tm), per_tile(1, LANES), per_tile(1, LANES))
    return pl.pallas_call(
        _route_kernel,
        grid=(tiles,),
        in_specs=[pl.BlockSpec((tm, LANES), lambda i: (i, 0))],
        out_specs=list(specs),
        out_shape=list(shapes),
        scratch_shapes=[pltpu.VMEM((1, LANES), F32)],
        compiler_params=_params("arbitrary"),
        name="route",
    )(logits)


def _for_each_chunk(tile, n_ref, rows_ref, visit):
    max_chunks = rows_ref.shape[0] // n_ref.shape[0]

    def per_chunk(c, carry):
        visit(pl.multiple_of(c * SEG_ALIGN, SEG_ALIGN), pl.multiple_of(rows_ref[tile * max_chunks + c], SEG_ALIGN))
        return carry

    lax.fori_loop(0, n_ref[tile], per_chunk, 0)


def _dispatch_kernel(n_ref, rows_ref, slot_t_ref, h_ref, xs_in_ref, xs_ref, loc_ref, sems):
    del xs_in_ref
    tile = pl.program_id(0)
    n_loc, tm = loc_ref.shape[1], h_ref.shape[0]
    buf = tile & 1

    slot = lax.broadcasted_iota(jnp.int32, (n_loc, tm), 0).astype(F32)
    place = jnp.logical_or(slot == slot_t_ref[0, 0:1, :], slot == slot_t_ref[0, 1:2, :])
    loc_ref[buf] = _pack_rows(jnp.dot(_indicator(place, BF16), h_ref[...], preferred_element_type=F32))

    def chunk_copy(b):
        return lambda local, row: pltpu.make_async_copy(
            loc_ref.at[b, pl.ds(local, SEG_ALIGN), :], xs_ref.at[pl.ds(row, SEG_ALIGN), :], sems.at[b])

    _for_each_chunk(tile, n_ref, rows_ref, lambda l, r: chunk_copy(buf)(l, r).start())

    @pl.when(tile > 0)
    def _previous_done():
        _for_each_chunk(tile - 1, n_ref, rows_ref, lambda l, r: chunk_copy(1 - buf)(l, r).wait())

    @pl.when(tile == pl.num_programs(0) - 1)
    def _last_done():
        _for_each_chunk(tile, n_ref, rows_ref, lambda l, r: chunk_copy(buf)(l, r).wait())


def _dispatch(n_chunks, chunk_rows, slot_t, h2, n_rows):
    n, d = h2.shape
    tm = TOKEN_TILE
    xs0 = jnp.zeros((n_rows, d // 2), U32)
    grid_spec = pltpu.PrefetchScalarGridSpec(
        num_scalar_prefetch=2,
        grid=(n // tm,),
        in_specs=[
            pl.BlockSpec((1, ROUTE_COLS, tm), lambda i, c, r: (i, 0, 0)),
            pl.BlockSpec((tm, d), lambda i, c, r: (i, 0)),
            pl.BlockSpec(memory_space=pl.ANY),
        ],
        out_specs=pl.BlockSpec(memory_space=pl.ANY),
        scratch_shapes=[pltpu.VMEM((2, _local_rows(tm), d // 2), U32), pltpu.SemaphoreType.DMA((2,))],
    )
    return pl.pallas_call(
        _dispatch_kernel,
        grid_spec=grid_spec,
        out_shape=jax.ShapeDtypeStruct((n_rows, d // 2), U32),
        input_output_aliases={4: 0},
        compiler_params=_params("arbitrary"),
        name="dispatch",
    )(n_chunks, chunk_rows, slot_t, h2, xs0)


def _expert_kernel(blk_e_ref, n_used_ref, xs_ref, wg_ref, wu_ref, wd_ref, ys_ref, wg_bf, wu_bf, wd_bf):
    i = pl.program_id(0)
    used = i < n_used_ref[0]

    @pl.when(jnp.logical_and(used, jnp.logical_or(i == 0, blk_e_ref[i] != blk_e_ref[jnp.maximum(i - 1, 0)])))
    def _new_expert():
        wg_bf[...] = wg_ref[0].astype(BF16)
        wu_bf[...] = wu_ref[0].astype(BF16)
        wd_bf[...] = wd_ref[0].astype(BF16)

    @pl.when(used)
    def _compute():
        xb = _unpack_rows(xs_ref[...]).astype(BF16)
        g = jnp.dot(xb, wg_bf[...], preferred_element_type=F32)
        u = jnp.dot(xb, wu_bf[...], preferred_element_type=F32)
        hid = (g * jax.nn.sigmoid(g) * u).astype(BF16)
        ys_ref[...] = _pack_rows(jnp.dot(hid, wd_bf[...], preferred_element_type=F32))

    @pl.when(i >= n_used_ref[0])
    def _unused():
        ys_ref[...] = jnp.zeros_like(ys_ref)


def _experts(blk_e, n_used, xs, wg, wu, wd):
    n_rows, half = xs.shape
    _, d, de = wg.shape
    grid_spec = pltpu.PrefetchScalarGridSpec(
        num_scalar_prefetch=2,
        grid=(n_rows // ROW_BLOCK,),
        in_specs=[
            pl.BlockSpec((ROW_BLOCK, half), lambda i, e, u: (i, 0)),
            pl.BlockSpec((1, d, de), lambda i, e, u: (e[i], 0, 0)),
            pl.BlockSpec((1, d, de), lambda i, e, u: (e[i], 0, 0)),
            pl.BlockSpec((1, de, d), lambda i, e, u: (e[i], 0, 0)),
        ],
        out_specs=pl.BlockSpec((ROW_BLOCK, half), lambda i, e, u: (i, 0)),
        scratch_shapes=[pltpu.VMEM((d, de), BF16), pltpu.VMEM((d, de), BF16), pltpu.VMEM((de, d), BF16)],
    )
    return pl.pallas_call(
        _expert_kernel,
        grid_spec=grid_spec,
        out_shape=jax.ShapeDtypeStruct((n_rows, half), U32),
        compiler_params=_params("arbitrary"),
        name="experts",
    )(blk_e, n_used, xs, wg, wu, wd)


def _combine_kernel(n_ref, rows_ref, x1_ref, mod_ref, rt_ref, ys_ref, o_ref, loc_ref, sems):
    tm = x1_ref.shape[1]
    n_loc = loc_ref.shape[1]
    tile = pl.program_id(0) * pl.num_programs(1) + pl.program_id(1)
    n_tiles = pl.num_programs(0) * pl.num_programs(1)
    buf = tile & 1

    def chunk_copy(b):
        return lambda local, row: pltpu.make_async_copy(
            ys_ref.at[pl.ds(row, SEG_ALIGN), :], loc_ref.at[b, pl.ds(local, SEG_ALIGN), :], sems.at[b])

    @pl.when(tile == 0)
    def _init():
        loc_ref[...] = jnp.zeros_like(loc_ref)
        _for_each_chunk(tile, n_ref, rows_ref, lambda l, r: chunk_copy(buf)(l, r).start())

    @pl.when(tile + 1 < n_tiles)
    def _fetch_next():
        _for_each_chunk(tile + 1, n_ref, rows_ref, lambda l, r: chunk_copy(1 - buf)(l, r).start())

    _for_each_chunk(tile, n_ref, rows_ref, lambda l, r: chunk_copy(buf)(l, r).wait())

    y = _unpack_rows(loc_ref[buf]).astype(BF16)
    rt = rt_ref[0]
    slot = lax.broadcasted_iota(jnp.int32, (tm, n_loc), 1).astype(F32)
    y0 = jnp.dot(_indicator(slot == rt[:, 0:1], BF16), y, preferred_element_type=F32)
    y1 = jnp.dot(_indicator(slot == rt[:, 1:2], BF16), y, preferred_element_type=F32)
    o_ref[0] = x1_ref[0] + mod_ref[0, 5:6, :] * (rt[:, 2:3] * y0 + rt[:, 3:4] * y1)


def _combine(n_chunks, chunk_rows, x1, mod, rt, ys):
    bsz, s, d = x1.shape
    tm = TOKEN_TILE
    grid_spec = pltpu.PrefetchScalarGridSpec(
        num_scalar_prefetch=2,
        grid=(bsz, s // tm),
        in_specs=[
            pl.BlockSpec((1, tm, d), lambda b, i, c, r: (b, i, 0)),
            pl.BlockSpec((1, N_MOD, d), lambda b, i, c, r: (b, 0, 0)),
            pl.BlockSpec((1, tm, ROUTE_COLS), lambda b, i, c, r: (b, i, 0)),
            pl.BlockSpec(memory_space=pl.ANY),
        ],
        out_specs=pl.BlockSpec((1, tm, d), lambda b, i, c, r: (b, i, 0)),
        scratch_shapes=[pltpu.VMEM((2, _local_rows(tm), d // 2), U32), pltpu.SemaphoreType.DMA((2,))],
    )
    return pl.pallas_call(
        _combine_kernel,
        grid_spec=grid_spec,
        out_shape=jax.ShapeDtypeStruct((bsz, s, d), F32),
        compiler_params=_params("arbitrary", "arbitrary"),
        name="combine",
    )(n_chunks, chunk_rows, x1, mod, rt, ys)


def _layer(x, mod, g_mix, w_in, g_q, g_k, rel_bias, w_br_moba, w_br_sb, w_out, g_ffn,
           w_rg, b_rg, w_re, b_re, w_gate, w_up, w_down):
    bsz, s, d = x.shape
    n = bsz * s
    n_blk = s // ATT_BLOCK

    proj = _inproj(x, mod, g_mix, w_in.astype(BF16))

    top = (n_blk + 1) * ATT_BLOCK - 1
    dist = np.maximum(top - np.arange((n_blk + 2) * ATT_BLOCK), 0)
    rel_rev = rel_bias[:, _rel_bucket_table(top)[dist]].reshape(N_HEADS // HEADS_PER_STEP, HEADS_PER_STEP, -1)

    o_a = _moba(proj, g_q, g_k, rel_rev)
    o_b = _stick_breaking(proj)

    pad = LANES - N_GROUPS - N_EXPERTS
    w_router = jnp.concatenate([w_rg, w_re, jnp.zeros((d, pad), F32)], axis=1)
    b_router = jnp.concatenate([b_rg, b_re, jnp.zeros((pad,), F32)]).reshape(1, LANES)
    x1, h2, logits = _merge(x, o_a, o_b, proj, mod, w_br_moba.astype(BF16), w_br_sb.astype(BF16),
                            w_out.astype(BF16), g_ffn, w_router, b_router)

    rt, slot_t, tile_cnt, tile_before = _route(logits.reshape(n, LANES))

    experts = slice(N_GROUPS, N_GROUPS + N_EXPERTS)
    cnt = tile_cnt[:, 0, experts].astype(jnp.int32)
    total = jnp.sum(cnt, axis=0)
    padded = (total + ROW_BLOCK - 1) // ROW_BLOCK * ROW_BLOCK
    pend = jnp.cumsum(padded)
    pstart = pend - padded
    max_rows = 2 * n + cnt.shape[0] * N_EXPERTS * (SEG_ALIGN - 1) + N_EXPERTS * (ROW_BLOCK - 1)
    n_blocks = -(-max_rows // ROW_BLOCK)
    blk_row0 = jnp.arange(n_blocks, dtype=jnp.int32) * ROW_BLOCK
    blk_e = jnp.minimum(jnp.sum((blk_row0[:, None] >= pend[None, :]).astype(jnp.int32), axis=1), N_EXPERTS - 1)
    n_used = (pend[-1] // ROW_BLOCK).astype(jnp.int32).reshape(1)
    seg_row0 = pstart[None, :] + tile_before[:, 0, experts].astype(jnp.int32)
    chunks = cnt // SEG_ALIGN
    chunk_end = jnp.cumsum(chunks, axis=1)
    chunk_start = (chunk_end - chunks)[:, None, :]
    c = jnp.arange(_local_rows(TOKEN_TILE) // SEG_ALIGN, dtype=jnp.int32)[None, :, None]
    inside = jnp.logical_and(c >= chunk_start, c < chunk_end[:, None, :])
    chunk_rows = jnp.sum(jnp.where(inside, seg_row0[:, None, :] + SEG_ALIGN * (c - chunk_start), 0), axis=2).reshape(-1)
    n_chunks = chunk_end[:, -1]

    xs = _dispatch(n_chunks, chunk_rows, slot_t, h2.reshape(n, d), n_blocks * ROW_BLOCK)
    ys = _experts(blk_e, n_used, xs, w_gate, w_up, w_down)
    return _combine(n_chunks, chunk_rows, x1, mod, rt.reshape(bsz, s, ROUTE_COLS), ys)


def kernel(x, c, w_ada, b_ada, g_mix, w_in, g_q, g_k, rel_bias, w_br_moba, w_br_sb, w_out, g_ffn,
           w_rg, b_rg, w_re, b_re, w_gate, w_up, w_down):
    bsz, s, d = x.shape
    assert s % TOKEN_TILE == 0 and s % MOBA_Q_TILE == 0 and d % (2 * LANES) == 0
    for l in range(w_ada.shape[0]):
        mod = _adaln(c, w_ada[l], b_ada[l]).reshape(bsz, N_MOD, d)
        x = _layer(x, mod, g_mix[l], w_in[l], g_q[l], g_k[l], rel_bias, w_br_moba[l], w_br_sb[l], w_out[l],
                   g_ffn[l], w_rg[l], b_rg[l], w_re[l], b_re[l], w_gate[l], w_up[l], w_down[l])
    return x
```

```python
import functools
import math

import numpy as np
import jax
import jax.numpy as jnp
from jax import lax
from jax.experimental import pallas as pl
from jax.experimental.pallas import tpu as pltpu

F32 = jnp.float32
BF16 = jnp.bfloat16
U32 = jnp.uint32
HIGHEST = lax.Precision.HIGHEST

LANES = 128
VMEM_LIMIT_BYTES = 56 * 1024 * 1024

HEAD_DIM = 64
N_HEADS = 8
HEADS_PER_STEP = LANES // HEAD_DIM
ATT_BLOCK = 256
MOBA_Q_TILE = 2 * ATT_BLOCK
SB_Q_BLOCKS = 2
MOBA_TOPK = 3
REL_BUCKETS = 32
REL_MAX_DIST = 1024
N_GROUPS = 4
EXPERTS_PER_GROUP = 8
N_EXPERTS = N_GROUPS * EXPERTS_PER_GROUP
N_MOD = 6
RMS_EPS = 1e-6
ROW_BLOCK = 512
TOKEN_TILE = 512
NEG_INF = float("-inf")
MASK_LOGIT = -1e30
SB_EXIT_LOG = -110.0

_NT = (((1,), (1,)), ((), ()))


def _params(*semantics):
    return pltpu.CompilerParams(dimension_semantics=semantics, vmem_limit_bytes=VMEM_LIMIT_BYTES)


def _adaln_kernel(c_ref, w_ref, b_ref, o_ref):
    c = c_ref[...]
    s = c * jax.nn.sigmoid(c)
    o_ref[...] = jnp.dot(s, w_ref[...], preferred_element_type=F32, precision=HIGHEST) + b_ref[...]


def _adaln(c, w, b):
    bsz, d = c.shape
    n = w.shape[1]
    tn = 1536
    return pl.pallas_call(
        _adaln_kernel,
        grid=(n // tn,),
        in_specs=[
            pl.BlockSpec((bsz, d), lambda j: (0, 0)),
            pl.BlockSpec((d, tn), lambda j: (0, j)),
            pl.BlockSpec((1, tn), lambda j: (0, j)),
        ],
        out_specs=pl.BlockSpec((bsz, tn), lambda j: (0, j)),
        out_shape=jax.ShapeDtypeStruct((bsz, n), F32),
        compiler_params=_params("arbitrary"),
        name="adaln",
    )(c, w, b.reshape(1, n))


def _rms_modulate(x, g, shift, scale):
    y = x * lax.rsqrt(jnp.mean(x * x, axis=-1, keepdims=True) + RMS_EPS) * g
    return y * (1.0 + scale) + shift


def _inproj_kernel(x_ref, mod_ref, g_ref, w_ref, o_ref, *, col_chunk):
    h = _rms_modulate(x_ref[0], g_ref[...], mod_ref[0, 0:1, :], mod_ref[0, 1:2, :]).astype(BF16)
    for n in range(w_ref.shape[1] // col_chunk):
        cols = slice(n * col_chunk, (n + 1) * col_chunk)
        o_ref[0, :, cols] = jnp.dot(h, w_ref[:, cols], preferred_element_type=F32).astype(BF16)


def _inproj(x, mod, g, w_bf16):
    bsz, s, d = x.shape
    n = w_bf16.shape[1]
    tm = TOKEN_TILE
    return pl.pallas_call(
        functools.partial(_inproj_kernel, col_chunk=1024),
        grid=(bsz, s // tm),
        in_specs=[
            pl.BlockSpec((1, tm, d), lambda b, i: (b, i, 0)),
            pl.BlockSpec((1, N_MOD, d), lambda b, i: (b, 0, 0)),
            pl.BlockSpec((1, d), lambda b, i: (0, 0)),
            pl.BlockSpec((d, n), lambda b, i: (0, 0)),
        ],
        out_specs=pl.BlockSpec((1, tm, n), lambda b, i: (b, i, 0)),
        out_shape=jax.ShapeDtypeStruct((bsz, s, n), BF16),
        compiler_params=_params("arbitrary", "arbitrary"),
        name="inproj",
    )(x, mod, g.reshape(1, d), w_bf16)


def _indicator(cond, dtype):
    return jnp.where(cond, 1.0, 0.0).astype(dtype)


def _lane_index():
    return lax.broadcasted_iota(jnp.int32, (1, LANES), 1).astype(F32)


def _head_lane_mask():
    return lax.broadcasted_iota(jnp.int32, (1, LANES), 1) < HEAD_DIM


def _head_rmsnorm(t, g):
    first = _head_lane_mask()
    sq = t * t
    ss0 = jnp.sum(jnp.where(first, sq, 0.0), axis=-1, keepdims=True)
    ss1 = jnp.sum(jnp.where(first, 0.0, sq), axis=-1, keepdims=True)
    inv = jnp.where(first, lax.rsqrt(ss0 / HEAD_DIM + RMS_EPS), lax.rsqrt(ss1 / HEAD_DIM + RMS_EPS))
    return t * inv * g


def _split_heads(t):
    first = _head_lane_mask()
    zero = jnp.zeros_like(t)
    return jnp.where(first, t, zero), jnp.where(first, zero, t)


def _rel_bucket_table(max_dist):
    n = np.arange(max_dist + 1)
    max_exact = REL_BUCKETS // 2
    nf = np.maximum(n, 1).astype(np.float64)
    large = max_exact + (np.log(nf / max_exact) / math.log(REL_MAX_DIST / max_exact)
                         * (REL_BUCKETS - max_exact)).astype(np.int64)
    large = np.minimum(large, REL_BUCKETS - 1)
    return np.where(n < max_exact, n, large).astype(np.int32)


def _moba_kernel(q_ref, k_ref, v_ref, gq_ref, gk_ref, rb_ref, o_ref,
                 kaug_ref, vaug_ref, kmean_ref, toep_ref, s_ref, mvec_ref, acc_ref, *, n_blk):
    b = pl.program_id(1)
    qi = pl.program_id(2)
    blk = ATT_BLOCK
    qt = q_ref.shape[1]
    halves = qt // blk
    lane = _lane_index()
    first = _head_lane_mask()
    tag_lane = lane - float(HEAD_DIM)

    @pl.when(jnp.logical_and(b == 0, qi == 0))
    def _build_bias_tiles():
        for h in range(HEADS_PER_STEP):
            for d in range(n_blk):
                r = rb_ref[0, h:h + 1, (n_blk - d) * blk:(n_blk - d + 2) * blk]
                rolled = pltpu.roll(jnp.broadcast_to(r, (blk, 2 * blk)), blk + 1, 1, stride=1, stride_axis=0)
                toep_ref[h, d] = rolled[:, :blk]

    @pl.when(qi == 0)
    def _prep_keys():
        kmean_ref[...] = jnp.zeros_like(kmean_ref)
        for j in range(n_blk):
            kn = _head_rmsnorm(k_ref[0, j * blk:(j + 1) * blk, :].astype(F32), gk_ref[...])
            tag = jnp.where(tag_lane == float(j), 1.0, 0.0)
            for h, kh in enumerate((kn, pltpu.roll(kn, HEAD_DIM, 1))):
                kaug_ref[h, j * blk:(j + 1) * blk, :] = jnp.where(first, kh, tag).astype(BF16)
                kmean_ref[h, j:j + 1, :] = jnp.where(first, jnp.mean(kh, axis=0, keepdims=True), 0.0)
            vb = v_ref[0, j * blk:(j + 1) * blk, :].astype(F32)
            for h, vh in enumerate((vb, pltpu.roll(vb, HEAD_DIM, 1))):
                vaug_ref[h, j * blk:(j + 1) * blk, :] = jnp.where(first, vh, 1.0).astype(BF16)

    qn = _head_rmsnorm(q_ref[0].astype(F32), gq_ref[...])
    gate_rows = kmean_ref.shape[1]
    blk_row = lax.broadcasted_iota(jnp.int32, (gate_rows, qt), 0).astype(F32)
    own = (halves * qi + lax.shift_right_logical(lax.broadcasted_iota(jnp.int32, (gate_rows, qt), 1),
                                                 blk.bit_length() - 1)).astype(F32)
    q_aug = []
    for h, qh in enumerate((qn, pltpu.roll(qn, HEAD_DIM, 1))):
        qh = jnp.where(first, qh, 0.0)
        km = kmean_ref[h]
        km_hi, q_hi = km.astype(BF16), qh.astype(BF16)
        km_lo, q_lo = (km - km_hi.astype(F32)).astype(BF16), (qh - q_hi.astype(F32)).astype(BF16)
        g = (lax.dot_general(km_hi, q_hi, _NT, preferred_element_type=F32)
             + lax.dot_general(km_lo, q_hi, _NT, preferred_element_type=F32)
             + lax.dot_general(km_hi, q_lo, _NT, preferred_element_type=F32))
        g = jnp.where(blk_row < own, g, NEG_INF)
        keep = jnp.where(blk_row == own, 1.0, 0.0)
        for _ in range(MOBA_TOPK):
            top = jnp.max(g, axis=0, keepdims=True)
            is_top = jnp.logical_and(g == top, top > NEG_INF)
            idx = jnp.min(jnp.where(is_top, blk_row, float(gate_rows)), axis=0, keepdims=True)
            pick = blk_row == idx
            keep = jnp.where(pick, 1.0, keep)
            g = jnp.where(pick, NEG_INF, g)
        mask_t = jnp.where(keep > 0.0, 0.0, MASK_LOGIT)
        mask_logit = jnp.concatenate([jnp.zeros((HEAD_DIM, qt), F32), mask_t,
                                      jnp.zeros((LANES - HEAD_DIM - gate_rows, qt), F32)], axis=0).T
        q_aug.append(jnp.where(first, qh * (HEAD_DIM ** -0.5), mask_logit).astype(BF16))

    def lane_max(s):
        return jnp.maximum(s[:, :LANES], s[:, LANES:])

    def scores(h, t, dist):
        k_t = kaug_ref[h, pl.ds(pl.multiple_of(t * qt, qt), qt), :]
        s = lax.dot_general(q_aug[h], k_t, _NT, preferred_element_type=F32)
        tiles = []
        for kb in range(halves):
            rows = []
            for qb in range(halves):
                part = s[qb * blk:(qb + 1) * blk, kb * blk:(kb + 1) * blk]
                if isinstance(dist, int) and halves * dist + qb - kb < 0:
                    rows.append(part)
                else:
                    rows.append(part + toep_ref[h, halves * dist + qb - kb])
            tiles.append(jnp.concatenate(rows, axis=0))
        return tiles

    row = lax.broadcasted_iota(jnp.int32, (qt, blk), 0)
    col = lax.broadcasted_iota(jnp.int32, (qt, blk), 1)
    for h in range(HEADS_PER_STEP):
        top = None
        for kb, s in enumerate(scores(h, qi, 0)):
            s = jnp.where(col + kb * blk <= row, s, NEG_INF)
            s_ref[h, halves * qi + kb] = s
            top = lane_max(s) if top is None else jnp.maximum(top, lane_max(s))
        mvec_ref[h] = top

    def in_pairs(count, visit):
        def pair(p, carry):
            visit(2 * p, 2)
            return carry

        lax.fori_loop(0, lax.shift_right_logical(count, 1), pair, 0)

        @pl.when((count & 1) == 1)
        def _last():
            visit(count - 1, 1)

    def pass1(t0, n_tiles):
        for h in range(HEADS_PER_STEP):
            top = mvec_ref[h]
            for t in [t0 + u for u in range(n_tiles)]:
                for kb, s in enumerate(scores(h, t, qi - t)):
                    s_ref[h, halves * t + kb] = s
                    top = jnp.maximum(top, lane_max(s))
            mvec_ref[h] = top

    in_pairs(qi, pass1)

    for h in range(HEADS_PER_STEP):
        m = jnp.max(mvec_ref[h], axis=-1, keepdims=True)
        mvec_ref[h] = jnp.broadcast_to(m, (qt, LANES))
        acc_ref[h] = jnp.zeros((qt, LANES), F32)

    def pass2(t0, n_tiles):
        keys = pl.ds(pl.multiple_of(t0 * qt, qt), n_tiles * qt)
        for h in range(HEADS_PER_STEP):
            m = mvec_ref[h]
            parts = []
            for kb in range(n_tiles * halves):
                s = s_ref[h, halves * t0 + kb]
                parts += [jnp.exp(s[:, :LANES] - m), jnp.exp(s[:, LANES:] - m)]
            p = jnp.concatenate(parts, axis=1).astype(BF16)
            acc_ref[h] += jnp.dot(p, vaug_ref[h, keys, :], preferred_element_type=F32)

    in_pairs(qi + 1, pass2)

    out = [acc_ref[h] / acc_ref[h][:, HEAD_DIM:HEAD_DIM + 1] for h in range(HEADS_PER_STEP)]
    o_ref[0] = jnp.where(first, out[0], pltpu.roll(out[1], HEAD_DIM, 1)).astype(o_ref.dtype)


def _moba(proj, g_q, g_k, rel_rev):
    bsz, s, _ = proj.shape
    blk = ATT_BLOCK
    n_blk = s // blk
    n_hp = N_HEADS // HEADS_PER_STEP
    w = N_HEADS * HEAD_DIM
    tile2 = lambda g: jnp.tile(g.reshape(1, HEAD_DIM), (1, HEADS_PER_STEP))
    qt = MOBA_Q_TILE
    return pl.pallas_call(
        functools.partial(_moba_kernel, n_blk=n_blk),
        grid=(n_hp, bsz, s // qt),
        in_specs=[
            pl.BlockSpec((1, qt, LANES), lambda hp, b, i: (b, i, hp)),
            pl.BlockSpec((1, s, LANES), lambda hp, b, i: (b, 0, n_hp + hp)),
            pl.BlockSpec((1, s, LANES), lambda hp, b, i: (b, 0, 2 * n_hp + hp)),
            pl.BlockSpec((1, LANES), lambda hp, b, i: (0, 0)),
            pl.BlockSpec((1, LANES), lambda hp, b, i: (0, 0)),
            pl.BlockSpec((1, HEADS_PER_STEP, rel_rev.shape[-1]), lambda hp, b, i: (hp, 0, 0)),
        ],
        out_specs=pl.BlockSpec((1, qt, LANES), lambda hp, b, i: (b, i, hp)),
        out_shape=jax.ShapeDtypeStruct((bsz, s, w), BF16),
        scratch_shapes=[
            pltpu.VMEM((HEADS_PER_STEP, s, LANES), BF16),
            pltpu.VMEM((HEADS_PER_STEP, s, LANES), BF16),
            pltpu.VMEM((HEADS_PER_STEP, -(-n_blk // 8) * 8, LANES), F32),
            pltpu.VMEM((HEADS_PER_STEP, n_blk, blk, blk), F32),
            pltpu.VMEM((HEADS_PER_STEP, n_blk, qt, blk), F32),
            pltpu.VMEM((HEADS_PER_STEP, qt, LANES), F32),
            pltpu.VMEM((HEADS_PER_STEP, qt, LANES), F32),
        ],
        compiler_params=_params("arbitrary", "arbitrary", "arbitrary"),
        name="moba",
    )(proj, proj, proj, tile2(g_q), tile2(g_k), rel_rev)


def _sb_kernel(q_ref, k_ref, v_ref, o_ref, acc_ref, carry_ref):
    ti = pl.program_id(2)
    blk = ATT_BLOCK
    n_sub = q_ref.shape[1] // blk
    first = n_sub * ti
    row = lax.broadcasted_iota(jnp.int32, (blk, blk), 0)
    col = lax.broadcasted_iota(jnp.int32, (blk, blk), 1)
    past = col < row
    later = _indicator(row > col, BF16)

    q_heads = []
    for sub in range(n_sub):
        heads = _split_heads(q_ref[0, sub * blk:(sub + 1) * blk, :].astype(F32) * (HEAD_DIM ** -0.5))
        q_heads.append([t.astype(BF16) for t in heads])

    def walk(jobs, fresh):
        for sub, blocks in jobs:
            for h in range(HEADS_PER_STEP):
                carry = None if fresh else carry_ref[sub, h]
                total = None if fresh else acc_ref[sub, h]
                for j, diagonal in blocks:
                    start = j * blk if isinstance(j, int) else pl.multiple_of(j * blk, blk)
                    z = lax.dot_general(q_heads[sub][h], k_ref[0, pl.ds(start, blk), :], _NT,
                                        preferred_element_type=F32)
                    log_1m = -(jnp.maximum(z, 0.0) + jnp.log(1.0 + jnp.exp(-jnp.abs(z))))
                    masked = jnp.where(past, log_1m, 0.0) if diagonal else log_1m
                    hi = masked.astype(BF16)
                    lo = (masked - hi.astype(F32)).astype(BF16)
                    after = (jnp.dot(hi, later, preferred_element_type=F32)
                             + jnp.dot(lo, later, preferred_element_type=F32))
                    row_sum = after[:, 0:1] + masked[:, 0:1]
                    if carry is not None:
                        after = after + carry
                    a = jnp.exp(z + log_1m + after)
                    if diagonal:
                        a = jnp.where(past, a, 0.0)
                    pv = jnp.dot(a.astype(BF16), v_ref[0, pl.ds(start, blk), :], preferred_element_type=F32)
                    total = pv if total is None else total + pv
                    carry = row_sum if carry is None else carry + row_sum
                carry_ref[sub, h] = carry
                acc_ref[sub, h] = total

    @pl.when(ti == 0)
    def _first_step():
        walk([(sub, [(sub, True)] + ([(sub - 1, False)] if sub else [])) for sub in range(n_sub)], True)

    @pl.when(ti > 0)
    def _own_and_previous():
        walk([(sub, [(first + sub, True), (first + sub - 1, False)]) for sub in range(n_sub)], True)

    for sub in range(n_sub):
        def slowest_decay(sub=sub):
            return jnp.maximum(jnp.max(carry_ref[sub, 0]), jnp.max(carry_ref[sub, 1]))

        def cond(state):
            j, worst = state
            return jnp.logical_and(j >= 0, worst > SB_EXIT_LOG)

        def body(state, sub=sub, slowest_decay=slowest_decay):
            j, _ = state
            walk([(sub, [(j, False)])], False)
            return j - 1, slowest_decay()

        lax.while_loop(cond, body, (first + sub - 2, slowest_decay()))
        o_ref[0, sub * blk:(sub + 1) * blk, :] = jnp.where(
            _head_lane_mask(), acc_ref[sub, 0], acc_ref[sub, 1]).astype(o_ref.dtype)


def _stick_breaking(proj):
    bsz, s, _ = proj.shape
    blk = ATT_BLOCK
    qt = SB_Q_BLOCKS * blk
    n_hp = N_HEADS // HEADS_PER_STEP
    w = N_HEADS * HEAD_DIM
    base = 3 * n_hp
    return pl.pallas_call(
        _sb_kernel,
        grid=(n_hp, bsz, s // qt),
        in_specs=[
            pl.BlockSpec((1, qt, LANES), lambda hp, b, i: (b, i, base + hp)),
            pl.BlockSpec((1, s, LANES), lambda hp, b, i: (b, 0, base + n_hp + hp)),
            pl.BlockSpec((1, s, LANES), lambda hp, b, i: (b, 0, base + 2 * n_hp + hp)),
        ],
        out_specs=pl.BlockSpec((1, qt, LANES), lambda hp, b, i: (b, i, hp)),
        out_shape=jax.ShapeDtypeStruct((bsz, s, w), BF16),
        scratch_shapes=[
            pltpu.VMEM((SB_Q_BLOCKS, HEADS_PER_STEP, blk, LANES), F32),
            pltpu.VMEM((SB_Q_BLOCKS, HEADS_PER_STEP, blk, 1), F32),
        ],
        compiler_params=_params("arbitrary", "arbitrary", "arbitrary"),
        name="stickbrk",
    )(proj, proj, proj)


def _pack_rows(t):
    n = t.shape[1] // 2
    bits = pltpu.bitcast(t.astype(BF16).astype(F32), U32)
    return bits[:, :n] | (bits[:, n:] >> 16)


def _unpack_rows(p):
    hi = pltpu.bitcast(p & jnp.uint32(0xFFFF0000), F32)
    lo = pltpu.bitcast(p << 16, F32)
    return jnp.concatenate([hi, lo], axis=1)


def _merge_kernel(x_ref, oa_ref, ob_ref, ga_ref, gb_ref, mod_ref, wbm_ref, wbs_ref, wo_ref, gf_ref,
                  wrh_ref, wrl_ref, br_ref, x1_ref, h2_ref, lg_ref):
    ma = jnp.dot(oa_ref[0], wbm_ref[...], preferred_element_type=F32)
    mb = jnp.dot(ob_ref[0], wbs_ref[...], preferred_element_type=F32)
    merged = (jax.nn.sigmoid(ga_ref[0].astype(F32)) * ma + jax.nn.sigmoid(gb_ref[0].astype(F32)) * mb)
    t = jnp.dot(merged.astype(BF16), wo_ref[...], preferred_element_type=F32)
    x1 = x_ref[0] + mod_ref[0, 2:3, :] * t
    x1_ref[0] = x1
    h2 = _rms_modulate(x1, gf_ref[...], mod_ref[0, 3:4, :], mod_ref[0, 4:5, :])
    hi = h2.astype(BF16)
    h2_ref[0] = hi
    lo = (h2 - hi.astype(F32)).astype(BF16)
    lg_ref[0] = (jnp.dot(hi, wrh_ref[...], preferred_element_type=F32)
                 + jnp.dot(lo, wrh_ref[...], preferred_element_type=F32)
                 + jnp.dot(hi, wrl_ref[...], preferred_element_type=F32) + br_ref[...])


def _merge(x, o_a, o_b, proj, mod, wbm, wbs, wo, g_ffn, w_router, b_router):
    bsz, s, d = x.shape
    w = o_a.shape[-1]
    tm = TOKEN_TILE
    gate_blk = (proj.shape[-1] - 2 * d) // d
    const = lambda shape: pl.BlockSpec(shape, lambda b, i: (0,) * len(shape))
    tok = lambda width: pl.BlockSpec((1, tm, width), lambda b, i: (b, i, 0))
    wr_hi = w_router.astype(BF16)
    wr_lo = (w_router - wr_hi.astype(F32)).astype(BF16)
    return pl.pallas_call(
        _merge_kernel,
        grid=(bsz, s // tm),
        in_specs=[
            tok(d), tok(w), tok(w),
            pl.BlockSpec((1, tm, d), lambda b, i: (b, i, gate_blk)),
            pl.BlockSpec((1, tm, d), lambda b, i: (b, i, gate_blk + 1)),
            pl.BlockSpec((1, N_MOD, d), lambda b, i: (b, 0, 0)),
            const((w, d)), const((w, d)), const((d, d)), const((1, d)),
            const((d, LANES)), const((d, LANES)), const((1, LANES)),
        ],
        out_specs=[tok(d), tok(d), tok(LANES)],
        out_shape=[
            jax.ShapeDtypeStruct((bsz, s, d), F32),
            jax.ShapeDtypeStruct((bsz, s, d), BF16),
            jax.ShapeDtypeStruct((bsz, s, LANES), F32),
        ],
        compiler_params=_params("arbitrary", "arbitrary"),
        name="merge",
    )(x, o_a, o_b, proj, proj, mod, wbm, wbs, wo, g_ffn.reshape(1, d), wr_hi, wr_lo, b_router)


ROUTE_COLS = 8
SEG_ALIGN = 8
SEG_SHIFT = 3


def _local_rows(tm):
    return 2 * tm + N_EXPERTS * SEG_ALIGN


def _route_kernel(lg_ref, rt_ref, slot_t_ref, cnt_ref, before_ref, carry_ref):
    i = pl.program_id(0)
    tm = lg_ref.shape[0]
    lane = _lane_index()

    @pl.when(i == 0)
    def _init():
        carry_ref[...] = jnp.zeros_like(carry_ref)

    lg = lg_ref[...]

    def first_max(vals):
        top = jnp.max(vals, axis=-1, keepdims=True)
        idx = jnp.min(jnp.where(vals == top, lane, float(LANES)), axis=-1, keepdims=True)
        return top, idx

    g_logit = jnp.where(lane < N_GROUPS, lg, NEG_INF)
    g_top, g_idx = first_max(g_logit)
    g_w = 1.0 / jnp.sum(jnp.exp(g_logit - g_top), axis=-1, keepdims=True)
    lo = N_GROUPS + g_idx * EXPERTS_PER_GROUP
    e_logit = jnp.where(jnp.logical_and(lane >= lo, lane < lo + EXPERTS_PER_GROUP), lg, NEG_INF)
    e_top0, lane0 = first_max(e_logit)
    e_top1, lane1 = first_max(jnp.where(lane == lane0, NEG_INF, e_logit))
    z = jnp.exp(e_top1 - e_top0)
    w0 = g_w / (1.0 + z)
    w1 = g_w * z / (1.0 + z)

    hot = jnp.logical_or(lane == lane0, lane == lane1)
    row = lax.broadcasted_iota(jnp.int32, (tm, tm), 0)
    col = lax.broadcasted_iota(jnp.int32, (tm, tm), 1)
    earlier = jnp.dot(_indicator(col < row, BF16), _indicator(hot, BF16), preferred_element_type=F32)
    cnt = jnp.sum(_indicator(hot, F32), axis=0, keepdims=True)
    chunks = jnp.ceil(cnt * (1.0 / SEG_ALIGN))
    src = lax.broadcasted_iota(jnp.int32, (LANES, LANES), 0)
    dst = lax.broadcasted_iota(jnp.int32, (LANES, LANES), 1)
    seg_start = SEG_ALIGN * jnp.dot(jnp.broadcast_to(chunks, (8, LANES)).astype(BF16), _indicator(src < dst, BF16),
                                    preferred_element_type=F32)[0:1]
    local = seg_start + earlier
    slot0 = jnp.sum(jnp.where(lane == lane0, local, 0.0), axis=-1, keepdims=True)
    slot1 = jnp.sum(jnp.where(lane == lane1, local, 0.0), axis=-1, keepdims=True)
    cnt_ref[0] = SEG_ALIGN * chunks
    before_ref[0] = carry_ref[...]
    carry_ref[...] += SEG_ALIGN * chunks

    packed = jnp.zeros((tm, LANES), F32)
    for c, v in enumerate([slot0, slot1, w0, w1]):
        packed = jnp.where(lane == float(c), v, packed)
    rt_ref[...] = packed[:, :ROUTE_COLS]
    slot_t_ref[0] = packed.T[:ROUTE_COLS, :]


def _route(logits):
    n = logits.shape[0]
    tm = TOKEN_TILE
    tiles = n // tm
    per_tile = lambda rows, cols: (pl.BlockSpec((1, rows, cols), lambda i: (i, 0, 0)),
                                   jax.ShapeDtypeStruct((tiles, rows, cols), F32))
    specs, shapes = zip((pl.BlockSpec((tm, ROUTE_COLS), lambda i: (i, 0)), jax.ShapeDtypeStruct((n, ROUTE_COLS), F32)),
                        per_tile(ROUTE_COLS, tm), per_tile(1, LANES), per_tile(1, LANES))
    return pl.pallas_call(
        _route_kernel,
        grid=(tiles,),
        in_specs=[pl.BlockSpec((tm, LANES), lambda i: (i, 0))],
        out_specs=list(specs),
        out_shape=list(shapes),
        scratch_shapes=[pltpu.VMEM((1, LANES), F32)],
        compiler_params=_params("arbitrary"),
        name="route",
    )(logits)


def _for_each_chunk(tile, n_ref, rows_ref, visit):
    max_chunks = rows_ref.shape[0] // n_ref.shape[0]

    def per_chunk(c, carry):
        visit(pl.multiple_of(c * SEG_ALIGN, SEG_ALIGN), pl.multiple_of(rows_ref[tile * max_chunks + c], SEG_ALIGN))
        return carry

    lax.fori_loop(0, n_ref[tile], per_chunk, 0)


def _dispatch_kernel(n_ref, rows_ref, slot_t_ref, h_ref, xs_in_ref, xs_ref, loc_ref, sems):
    del xs_in_ref
    tile = pl.program_id(0)
    n_loc, tm = loc_ref.shape[1], h_ref.shape[0]
    buf = tile & 1

    slot = lax.broadcasted_iota(jnp.int32, (n_loc, tm), 0).astype(F32)
    place = jnp.logical_or(slot == slot_t_ref[0, 0:1, :], slot == slot_t_ref[0, 1:2, :])
    loc_ref[buf] = _pack_rows(jnp.dot(_indicator(place, BF16), h_ref[...], preferred_element_type=F32))

    def chunk_copy(b):
        return lambda local, row: pltpu.make_async_copy(
            loc_ref.at[b, pl.ds(local, SEG_ALIGN), :], xs_ref.at[pl.ds(row, SEG_ALIGN), :], sems.at[b])

    _for_each_chunk(tile, n_ref, rows_ref, lambda l, r: chunk_copy(buf)(l, r).start())

    @pl.when(tile > 0)
    def _previous_done():
        _for_each_chunk(tile - 1, n_ref, rows_ref, lambda l, r: chunk_copy(1 - buf)(l, r).wait())

    @pl.when(tile == pl.num_programs(0) - 1)
    def _last_done():
        _for_each_chunk(tile, n_ref, rows_ref, lambda l, r: chunk_copy(buf)(l, r).wait())


def _dispatch(n_chunks, chunk_rows, slot_t, h2, n_rows):
    n, d = h2.shape
    tm = TOKEN_TILE
    xs0 = jnp.zeros((n_rows, d // 2), U32)
    grid_spec = pltpu.PrefetchScalarGridSpec(
        num_scalar_prefetch=2,
        grid=(n // tm,),
        in_specs=[
            pl.BlockSpec((1, ROUTE_COLS, tm), lambda i, c, r: (i, 0, 0)),
            pl.BlockSpec((tm, d), lambda i, c, r: (i, 0)),
            pl.BlockSpec(memory_space=pl.ANY),
        ],
        out_specs=pl.BlockSpec(memory_space=pl.ANY),
        scratch_shapes=[pltpu.VMEM((2, _local_rows(tm), d // 2), U32), pltpu.SemaphoreType.DMA((2,))],
    )
    return pl.pallas_call(
        _dispatch_kernel,
        grid_spec=grid_spec,
        out_shape=jax.ShapeDtypeStruct((n_rows, d // 2), U32),
        input_output_aliases={4: 0},
        compiler_params=_params("arbitrary"),
        name="dispatch",
    )(n_chunks, chunk_rows, slot_t, h2, xs0)


def _expert_kernel(blk_e_ref, n_used_ref, xs_ref, wg_ref, wu_ref, wd_ref, ys_ref, wg_bf, wu_bf, wd_bf):
    i = pl.program_id(0)
    used = i < n_used_ref[0]

    @pl.when(jnp.logical_and(used, jnp.logical_or(i == 0, blk_e_ref[i] != blk_e_ref[jnp.maximum(i - 1, 0)])))
    def _new_expert():
        wg_bf[...] = wg_ref[0].astype(BF16)
        wu_bf[...] = wu_ref[0].astype(BF16)
        wd_bf[...] = wd_ref[0].astype(BF16)

    @pl.when(used)
    def _compute():
        xb = _unpack_rows(xs_ref[...]).astype(BF16)
        g = jnp.dot(xb, wg_bf[...], preferred_element_type=F32)
        u = jnp.dot(xb, wu_bf[...], preferred_element_type=F32)
        hid = (g * jax.nn.sigmoid(g) * u).astype(BF16)
        ys_ref[...] = _pack_rows(jnp.dot(hid, wd_bf[...], preferred_element_type=F32))

    @pl.when(i >= n_used_ref[0])
    def _unused():
        ys_ref[...] = jnp.zeros_like(ys_ref)


def _experts(blk_e, n_used, xs, wg, wu, wd):
    n_rows, half = xs.shape
    _, d, de = wg.shape
    grid_spec = pltpu.PrefetchScalarGridSpec(
        num_scalar_prefetch=2,
        grid=(n_rows // ROW_BLOCK,),
        in_specs=[
            pl.BlockSpec((ROW_BLOCK, half), lambda i, e, u: (i, 0)),
            pl.BlockSpec((1, d, de), lambda i, e, u: (e[i], 0, 0)),
            pl.BlockSpec((1, d, de), lambda i, e, u: (e[i], 0, 0)),
            pl.BlockSpec((1, de, d), lambda i, e, u: (e[i], 0, 0)),
        ],
        out_specs=pl.BlockSpec((ROW_BLOCK, half), lambda i, e, u: (i, 0)),
        scratch_shapes=[pltpu.VMEM((d, de), BF16), pltpu.VMEM((d, de), BF16), pltpu.VMEM((de, d), BF16)],
    )
    return pl.pallas_call(
        _expert_kernel,
        grid_spec=grid_spec,
        out_shape=jax.ShapeDtypeStruct((n_rows, half), U32),
        compiler_params=_params("arbitrary"),
        name="experts",
    )(blk_e, n_used, xs, wg, wu, wd)


def _combine_kernel(n_ref, rows_ref, x1_ref, mod_ref, rt_ref, ys_ref, o_ref, loc_ref, sems):
    tm = x1_ref.shape[1]
    n_loc = loc_ref.shape[1]
    tile = pl.program_id(0) * pl.num_programs(1) + pl.program_id(1)
    n_tiles = pl.num_programs(0) * pl.num_programs(1)
    buf = tile & 1

    def chunk_copy(b):
        return lambda local, row: pltpu.make_async_copy(
            ys_ref.at[pl.ds(row, SEG_ALIGN), :], loc_ref.at[b, pl.ds(local, SEG_ALIGN), :], sems.at[b])

    @pl.when(tile == 0)
    def _init():
        loc_ref[...] = jnp.zeros_like(loc_ref)
        _for_each_chunk(tile, n_ref, rows_ref, lambda l, r: chunk_copy(buf)(l, r).start())

    @pl.when(tile + 1 < n_tiles)
    def _fetch_next():
        _for_each_chunk(tile + 1, n_ref, rows_ref, lambda l, r: chunk_copy(1 - buf)(l, r).start())

    _for_each_chunk(tile, n_ref, rows_ref, lambda l, r: chunk_copy(buf)(l, r).wait())

    y = _unpack_rows(loc_ref[buf]).astype(BF16)
    rt = rt_ref[0]
    slot = lax.broadcasted_iota(jnp.int32, (tm, n_loc), 1).astype(F32)
    y0 = jnp.dot(_indicator(slot == rt[:, 0:1], BF16), y, preferred_element_type=F32)
    y1 = jnp.dot(_indicator(slot == rt[:, 1:2], BF16), y, preferred_element_type=F32)
    o_ref[0] = x1_ref[0] + mod_ref[0, 5:6, :] * (rt[:, 2:3] * y0 + rt[:, 3:4] * y1)


def _combine(n_chunks, chunk_rows, x1, mod, rt, ys):
    bsz, s, d = x1.shape
    tm = TOKEN_TILE
    grid_spec = pltpu.PrefetchScalarGridSpec(
        num_scalar_prefetch=2,
        grid=(bsz, s // tm),
        in_specs=[
            pl.BlockSpec((1, tm, d), lambda b, i, c, r: (b, i, 0)),
            pl.BlockSpec((1, N_MOD, d), lambda b, i, c, r: (b, 0, 0)),
            pl.BlockSpec((1, tm, ROUTE_COLS), lambda b, i, c, r: (b, i, 0)),
            pl.BlockSpec(memory_space=pl.ANY),
        ],
        out_specs=pl.BlockSpec((1, tm, d), lambda b, i, c, r: (b, i, 0)),
        scratch_shapes=[pltpu.VMEM((2, _local_rows(tm), d // 2), U32), pltpu.SemaphoreType.DMA((2,))],
    )
    return pl.pallas_call(
        _combine_kernel,
        grid_spec=grid_spec,
        out_shape=jax.ShapeDtypeStruct((bsz, s, d), F32),
        compiler_params=_params("arbitrary", "arbitrary"),
        name="combine",
    )(n_chunks, chunk_rows, x1, mod, rt, ys)


def _layer(x, mod, g_mix, w_in, g_q, g_k, rel_bias, w_br_moba, w_br_sb, w_out, g_ffn,
           w_rg, b_rg, w_re, b_re, w_gate, w_up, w_down):
    bsz, s, d = x.shape
    n = bsz * s
    n_blk = s // ATT_BLOCK

    proj = _inproj(x, mod, g_mix, w_in.astype(BF16))

    top = (n_blk + 1) * ATT_BLOCK - 1
    dist = np.maximum(top - np.arange((n_blk + 2) * ATT_BLOCK), 0)
    rel_rev = rel_bias[:, _rel_bucket_table(top)[dist]].reshape(N_HEADS // HEADS_PER_STEP, HEADS_PER_STEP, -1)

    o_a = _moba(proj, g_q, g_k, rel_rev)
    o_b = _stick_breaking(proj)

    pad = LANES - N_GROUPS - N_EXPERTS
    w_router = jnp.concatenate([w_rg, w_re, jnp.zeros((d, pad), F32)], axis=1)
    b_router = jnp.concatenate([b_rg, b_re, jnp.zeros((pad,), F32)]).reshape(1, LANES)
    x1, h2, logits = _merge(x, o_a, o_b, proj, mod, w_br_moba.astype(BF16), w_br_sb.astype(BF16),
                            w_out.astype(BF16), g_ffn, w_router, b_router)

    rt, slot_t, tile_cnt, tile_before = _route(logits.reshape(n, LANES))

    experts = slice(N_GROUPS, N_GROUPS + N_EXPERTS)
    cnt = tile_cnt[:, 0, experts].astype(jnp.int32)
    total = jnp.sum(cnt, axis=0)
    padded = (total + ROW_BLOCK - 1) // ROW_BLOCK * ROW_BLOCK
    pend = jnp.cumsum(padded)
    pstart = pend - padded
    max_rows = 2 * n + cnt.shape[0] * N_EXPERTS * (SEG_ALIGN - 1) + N_EXPERTS * (ROW_BLOCK - 1)
    n_blocks = -(-max_rows // ROW_BLOCK)
    blk_row0 = jnp.arange(n_blocks, dtype=jnp.int32) * ROW_BLOCK
    blk_e = jnp.minimum(jnp.sum((blk_row0[:, None] >= pend[None, :]).astype(jnp.int32), axis=1), N_EXPERTS - 1)
    n_used = (pend[-1] // ROW_BLOCK).astype(jnp.int32).reshape(1)
    seg_row0 = pstart[None, :] + tile_before[:, 0, experts].astype(jnp.int32)
    chunks = cnt // SEG_ALIGN
    chunk_end = jnp.cumsum(chunks, axis=1)
    chunk_start = (chunk_end - chunks)[:, None, :]
    c = jnp.arange(_local_rows(TOKEN_TILE) // SEG_ALIGN, dtype=jnp.int32)[None, :, None]
    inside = jnp.logical_and(c >= chunk_start, c < chunk_end[:, None, :])
    chunk_rows = jnp.sum(jnp.where(inside, seg_row0[:, None, :] + SEG_ALIGN * (c - chunk_start), 0), axis=2).reshape(-1)
    n_chunks = chunk_end[:, -1]

    xs = _dispatch(n_chunks, chunk_rows, slot_t, h2.reshape(n, d), n_blocks * ROW_BLOCK)
    ys = _experts(blk_e, n_used, xs, w_gate, w_up, w_down)
    return _combine(n_chunks, chunk_rows, x1, mod, rt.reshape(bsz, s, ROUTE_COLS), ys)


def kernel(x, c, w_ada, b_ada, g_mix, w_in, g_q, g_k, rel_bias, w_br_moba, w_br_sb, w_out, g_ffn,
           w_rg, b_rg, w_re, b_re, w_gate, w_up, w_down):
    bsz, s, d = x.shape
    assert s % TOKEN_TILE == 0 and s % MOBA_Q_TILE == 0 and d % (2 * LANES) == 0
    for l in range(w_ada.shape[0]):
        mod = _adaln(c, w_ada[l], b_ada[l]).reshape(bsz, N_MOD, d)
        x = _layer(x, mod, g_mix[l], w_in[l], g_q[l], g_k[l], rel_bias, w_br_moba[l], w_br_sb[l], w_out[l],
                   g_ffn[l], w_rg[l], b_rg[l], w_re[l], b_re[l], w_gate[l], w_up[l], w_down[l])
    return x
```

```python
import functools
import math

import numpy as np
import jax
import jax.numpy as jnp
from jax import lax
from jax.experimental import pallas as pl
from jax.experimental.pallas import tpu as pltpu

F32 = jnp.float32
BF16 = jnp.bfloat16
U32 = jnp.uint32
HIGHEST = lax.Precision.HIGHEST

LANES = 128
VMEM_LIMIT_BYTES = 56 * 1024 * 1024

HEAD_DIM = 64
N_HEADS = 8
HEADS_PER_STEP = LANES // HEAD_DIM
ATT_BLOCK = 256
MOBA_Q_TILE = 2 * ATT_BLOCK
SB_Q_BLOCKS = 2
MOBA_TOPK = 3
REL_BUCKETS = 32
REL_MAX_DIST = 1024
N_GROUPS = 4
EXPERTS_PER_GROUP = 8
N_EXPERTS = N_GROUPS * EXPERTS_PER_GROUP
N_MOD = 6
RMS_EPS = 1e-6
ROW_BLOCK = 512
TOKEN_TILE = 512
NEG_INF = float("-inf")
MASK_LOGIT = -1e30
SB_EXIT_LOG = -110.0

_NT = (((1,), (1,)), ((), ()))


def _params(*semantics):
    return pltpu.CompilerParams(dimension_semantics=semantics, vmem_limit_bytes=VMEM_LIMIT_BYTES)


def _adaln_kernel(c_ref, w_ref, b_ref, o_ref):
    c = c_ref[...]
    s = c * jax.nn.sigmoid(c)
    o_ref[...] = jnp.dot(s, w_ref[...], preferred_element_type=F32, precision=HIGHEST) + b_ref[...]


def _adaln(c, w, b):
    bsz, d = c.shape
    n = w.shape[1]
    tn = 1536
    return pl.pallas_call(
        _adaln_kernel,
        grid=(n // tn,),
        in_specs=[
            pl.BlockSpec((bsz, d), lambda j: (0, 0)),
            pl.BlockSpec((d, tn), lambda j: (0, j)),
            pl.BlockSpec((1, tn), lambda j: (0, j)),
        ],
        out_specs=pl.BlockSpec((bsz, tn), lambda j: (0, j)),
        out_shape=jax.ShapeDtypeStruct((bsz, n), F32),
        compiler_params=_params("arbitrary"),
        name="adaln",
    )(c, w, b.reshape(1, n))


def _rms_modulate(x, g, shift, scale):
    y = x * lax.rsqrt(jnp.mean(x * x, axis=-1, keepdims=True) + RMS_EPS) * g
    return y * (1.0 + scale) + shift


def _inproj_kernel(x_ref, mod_ref, g_ref, w_ref, gk_ref, o_ref, kaug_ref, vaug_ref, kmean_ref, *, col_chunk):
    i = pl.program_id(1)
    tm = x_ref.shape[1]
    h = _rms_modulate(x_ref[0], g_ref[...], mod_ref[0, 0:1, :], mod_ref[0, 1:2, :]).astype(BF16)

    w_moba = N_HEADS * HEAD_DIM
    blocks = tm // ATT_BLOCK
    first = _head_lane_mask()
    row_blk = blocks * i + lax.shift_right_logical(lax.broadcasted_iota(jnp.int32, (tm, LANES), 0),
                                                   ATT_BLOCK.bit_length() - 1)
    tag = jnp.where(_lane_index() - float(HEAD_DIM) == row_blk.astype(F32), 1.0, 0.0)
    mean_row = lax.broadcasted_iota(jnp.int32, kmean_ref.shape[2:], 0)

    @pl.when(i == 0)
    def _init():
        kmean_ref[...] = jnp.zeros_like(kmean_ref)

    for n in range(w_ref.shape[1] // col_chunk):
        cols = slice(n * col_chunk, (n + 1) * col_chunk)
        r = jnp.dot(h, w_ref[:, cols], preferred_element_type=F32)
        o_ref[0, :, cols] = r.astype(BF16)
        for c in range(col_chunk // LANES):
            col0 = n * col_chunk + c * LANES
            part = r[:, c * LANES:(c + 1) * LANES]
            if w_moba <= col0 < 2 * w_moba:
                kn = _head_rmsnorm(part, gk_ref[...])
                for hh, kh in enumerate((kn, pltpu.roll(kn, HEAD_DIM, 1))):
                    head = (col0 - w_moba) // HEAD_DIM + hh
                    kaug_ref[0, head] = jnp.where(first, kh, tag).astype(BF16)
                    means = kmean_ref[0, head]
                    for jb in range(blocks):
                        mean = jnp.mean(kh[jb * ATT_BLOCK:(jb + 1) * ATT_BLOCK], axis=0, keepdims=True)
                        means = jnp.where(mean_row == blocks * i + jb, jnp.where(first, mean, 0.0), means)
                    kmean_ref[0, head] = means
            elif 2 * w_moba <= col0 < 3 * w_moba:
                for hh, vh in enumerate((part, pltpu.roll(part, HEAD_DIM, 1))):
                    head = (col0 - 2 * w_moba) // HEAD_DIM + hh
                    vaug_ref[0, head] = jnp.where(first, vh, 1.0).astype(BF16)


def _inproj(x, mod, g, w_bf16, g_k):
    bsz, s, d = x.shape
    n = w_bf16.shape[1]
    tm = TOKEN_TILE
    n_blk = s // ATT_BLOCK
    per_head = lambda rows: pl.BlockSpec((1, N_HEADS, rows, LANES), lambda b, i: (b, 0, i, 0))
    return pl.pallas_call(
        functools.partial(_inproj_kernel, col_chunk=1024),
        grid=(bsz, s // tm),
        in_specs=[
            pl.BlockSpec((1, tm, d), lambda b, i: (b, i, 0)),
            pl.BlockSpec((1, N_MOD, d), lambda b, i: (b, 0, 0)),
            pl.BlockSpec((1, d), lambda b, i: (0, 0)),
            pl.BlockSpec((d, n), lambda b, i: (0, 0)),
            pl.BlockSpec((1, LANES), lambda b, i: (0, 0)),
        ],
        out_specs=[
            pl.BlockSpec((1, tm, n), lambda b, i: (b, i, 0)),
            per_head(tm),
            per_head(tm),
            pl.BlockSpec((1, N_HEADS, -(-n_blk // 8) * 8, LANES), lambda b, i: (b, 0, 0, 0)),
        ],
        out_shape=[
            jax.ShapeDtypeStruct((bsz, s, n), BF16),
            jax.ShapeDtypeStruct((bsz, N_HEADS, s, LANES), BF16),
            jax.ShapeDtypeStruct((bsz, N_HEADS, s, LANES), BF16),
            jax.ShapeDtypeStruct((bsz, N_HEADS, -(-n_blk // 8) * 8, LANES), F32),
        ],
        compiler_params=_params("arbitrary", "arbitrary"),
        name="inproj",
    )(x, mod, g.reshape(1, d), w_bf16, jnp.tile(g_k.reshape(1, HEAD_DIM), (1, HEADS_PER_STEP)))


def _indicator(cond, dtype):
    return jnp.where(cond, 1.0, 0.0).astype(dtype)


def _lane_index():
    return lax.broadcasted_iota(jnp.int32, (1, LANES), 1).astype(F32)


def _head_lane_mask():
    return lax.broadcasted_iota(jnp.int32, (1, LANES), 1) < HEAD_DIM


def _head_rmsnorm(t, g):
    first = _head_lane_mask()
    sq = t * t
    ss0 = jnp.sum(jnp.where(first, sq, 0.0), axis=-1, keepdims=True)
    ss1 = jnp.sum(jnp.where(first, 0.0, sq), axis=-1, keepdims=True)
    inv = jnp.where(first, lax.rsqrt(ss0 / HEAD_DIM + RMS_EPS), lax.rsqrt(ss1 / HEAD_DIM + RMS_EPS))
    return t * inv * g


def _split_heads(t):
    first = _head_lane_mask()
    zero = jnp.zeros_like(t)
    return jnp.where(first, t, zero), jnp.where(first, zero, t)


def _rel_bucket_table(max_dist):
    n = np.arange(max_dist + 1)
    max_exact = REL_BUCKETS // 2
    nf = np.maximum(n, 1).astype(np.float64)
    large = max_exact + (np.log(nf / max_exact) / math.log(REL_MAX_DIST / max_exact)
                         * (REL_BUCKETS - max_exact)).astype(np.int64)
    large = np.minimum(large, REL_BUCKETS - 1)
    return np.where(n < max_exact, n, large).astype(np.int32)


def _moba_kernel(q_ref, kaug_ref, vaug_ref, kmean_ref, gq_ref, rb_ref, o_ref,
                 toep_ref, s_ref, mvec_ref, acc_ref, *, n_blk):
    b = pl.program_id(1)
    qi = pl.program_id(2)
    blk = ATT_BLOCK
    qt = q_ref.shape[1]
    halves = qt // blk
    first = _head_lane_mask()

    @pl.when(jnp.logical_and(b == 0, qi == 0))
    def _build_bias_tiles():
        for h in range(HEADS_PER_STEP):
            for d in range(n_blk):
                r = rb_ref[0, h:h + 1, (n_blk - d) * blk:(n_blk - d + 2) * blk]
                rolled = pltpu.roll(jnp.broadcast_to(r, (blk, 2 * blk)), blk + 1, 1, stride=1, stride_axis=0)
                toep_ref[h, d] = rolled[:, :blk]

    qn = _head_rmsnorm(q_ref[0].astype(F32), gq_ref[...])
    gate_rows = kmean_ref.shape[2]
    blk_row = lax.broadcasted_iota(jnp.int32, (gate_rows, qt), 0).astype(F32)
    own = (halves * qi + lax.shift_right_logical(lax.broadcasted_iota(jnp.int32, (gate_rows, qt), 1),
                                                 blk.bit_length() - 1)).astype(F32)
    q_aug = []
    for h, qh in enumerate((qn, pltpu.roll(qn, HEAD_DIM, 1))):
        qh = jnp.where(first, qh, 0.0)
        km = kmean_ref[0, h]
        km_hi, q_hi = km.astype(BF16), qh.astype(BF16)
        km_lo, q_lo = (km - km_hi.astype(F32)).astype(BF16), (qh - q_hi.astype(F32)).astype(BF16)
        g = (lax.dot_general(km_hi, q_hi, _NT, preferred_element_type=F32)
             + lax.dot_general(km_lo, q_hi, _NT, preferred_element_type=F32)
             + lax.dot_general(km_hi, q_lo, _NT, preferred_element_type=F32))
        g = jnp.where(blk_row < own, g, NEG_INF)
        keep = jnp.where(blk_row == own, 1.0, 0.0)
        for _ in range(MOBA_TOPK):
            top = jnp.max(g, axis=0, keepdims=True)
            is_top = jnp.logical_and(g == top, top > NEG_INF)
            idx = jnp.min(jnp.where(is_top, blk_row, float(gate_rows)), axis=0, keepdims=True)
            pick = blk_row == idx
            keep = jnp.where(pick, 1.0, keep)
            g = jnp.where(pick, NEG_INF, g)
        mask_t = jnp.where(keep > 0.0, 0.0, MASK_LOGIT)
        mask_logit = jnp.concatenate([jnp.zeros((HEAD_DIM, qt), F32), mask_t,
                                      jnp.zeros((LANES - HEAD_DIM - gate_rows, qt), F32)], axis=0).T
        q_aug.append(jnp.where(first, qh * (HEAD_DIM ** -0.5), mask_logit).astype(BF16))

    def lane_max(s):
        return jnp.maximum(s[:, :LANES], s[:, LANES:])

    def scores(h, t, dist):
        k_t = kaug_ref[0, h, pl.ds(pl.multiple_of(t * qt, qt), qt), :]
        s = lax.dot_general(q_aug[h], k_t, _NT, preferred_element_type=F32)
        tiles = []
        for kb in range(halves):
            rows = []
            for qb in range(halves):
                part = s[qb * blk:(qb + 1) * blk, kb * blk:(kb + 1) * blk]
                if isinstance(dist, int) and halves * dist + qb - kb < 0:
                    rows.append(part)
                else:
                    rows.append(part + toep_ref[h, halves * dist + qb - kb])
            tiles.append(jnp.concatenate(rows, axis=0))
        return tiles

    row = lax.broadcasted_iota(jnp.int32, (qt, blk), 0)
    col = lax.broadcasted_iota(jnp.int32, (qt, blk), 1)
    for h in range(HEADS_PER_STEP):
        top = None
        for kb, s in enumerate(scores(h, qi, 0)):
            s = jnp.where(col + kb * blk <= row, s, NEG_INF)
            s_ref[h, halves * qi + kb] = s
            top = lane_max(s) if top is None else jnp.maximum(top, lane_max(s))
        mvec_ref[h] = top

    def in_pairs(count, visit):
        def pair(p, carry):
            visit(2 * p, 2)
            return carry

        lax.fori_loop(0, lax.shift_right_logical(count, 1), pair, 0)

        @pl.when((count & 1) == 1)
        def _last():
            visit(count - 1, 1)

    def pass1(t0, n_tiles):
        for h in range(HEADS_PER_STEP):
            top = mvec_ref[h]
            for t in [t0 + u for u in range(n_tiles)]:
                for kb, s in enumerate(scores(h, t, qi - t)):
                    s_ref[h, halves * t + kb] = s
                    top = jnp.maximum(top, lane_max(s))
            mvec_ref[h] = top

    in_pairs(qi, pass1)

    for h in range(HEADS_PER_STEP):
        m = jnp.max(mvec_ref[h], axis=-1, keepdims=True)
        mvec_ref[h] = jnp.broadcast_to(m, (qt, LANES))
        acc_ref[h] = jnp.zeros((qt, LANES), F32)

    def pass2(t0, n_tiles):
        keys = pl.ds(pl.multiple_of(t0 * qt, qt), n_tiles * qt)
        for h in range(HEADS_PER_STEP):
            m = mvec_ref[h]
            parts = []
            for kb in range(n_tiles * halves):
                s = s_ref[h, halves * t0 + kb]
                parts += [jnp.exp(s[:, :LANES] - m), jnp.exp(s[:, LANES:] - m)]
            p = jnp.concatenate(parts, axis=1).astype(BF16)
            acc_ref[h] += jnp.dot(p, vaug_ref[0, h, keys, :], preferred_element_type=F32)

    in_pairs(qi + 1, pass2)

    out = [acc_ref[h] / acc_ref[h][:, HEAD_DIM:HEAD_DIM + 1] for h in range(HEADS_PER_STEP)]
    o_ref[0] = jnp.where(first, out[0], pltpu.roll(out[1], HEAD_DIM, 1)).astype(o_ref.dtype)


def _moba(proj, kaug, vaug, kmean, g_q, rel_rev):
    bsz, s, _ = proj.shape
    blk = ATT_BLOCK
    n_blk = s // blk
    n_hp = N_HEADS // HEADS_PER_STEP
    w = N_HEADS * HEAD_DIM
    qt = MOBA_Q_TILE
    per_pair = lambda arr: pl.BlockSpec((1, HEADS_PER_STEP) + arr.shape[2:], lambda hp, b, i: (b, hp, 0, 0))
    return pl.pallas_call(
        functools.partial(_moba_kernel, n_blk=n_blk),
        grid=(n_hp, bsz, s // qt),
        in_specs=[
            pl.BlockSpec((1, qt, LANES), lambda hp, b, i: (b, i, hp)),
            per_pair(kaug), per_pair(vaug), per_pair(kmean),
            pl.BlockSpec((1, LANES), lambda hp, b, i: (0, 0)),
            pl.BlockSpec((1, HEADS_PER_STEP, rel_rev.shape[-1]), lambda hp, b, i: (hp, 0, 0)),
        ],
        out_specs=pl.BlockSpec((1, qt, LANES), lambda hp, b, i: (b, i, hp)),
        out_shape=jax.ShapeDtypeStruct((bsz, s, w), BF16),
        scratch_shapes=[
            pltpu.VMEM((HEADS_PER_STEP, n_blk, blk, blk), F32),
            pltpu.VMEM((HEADS_PER_STEP, n_blk, qt, blk), F32),
            pltpu.VMEM((HEADS_PER_STEP, qt, LANES), F32),
            pltpu.VMEM((HEADS_PER_STEP, qt, LANES), F32),
        ],
        compiler_params=_params("arbitrary", "arbitrary", "arbitrary"),
        name="moba",
    )(proj, kaug, vaug, kmean, jnp.tile(g_q.reshape(1, HEAD_DIM), (1, HEADS_PER_STEP)), rel_rev)


def _sb_kernel(q_ref, k_ref, v_ref, o_ref, acc_ref, carry_ref):
    ti = pl.program_id(2)
    blk = ATT_BLOCK
    n_sub = q_ref.shape[1] // blk
    first = n_sub * ti
    row = lax.broadcasted_iota(jnp.int32, (blk, blk), 0)
    col = lax.broadcasted_iota(jnp.int32, (blk, blk), 1)
    past = col < row
    later = _indicator(row > col, BF16)

    q_heads = []
    for sub in range(n_sub):
        heads = _split_heads(q_ref[0, sub * blk:(sub + 1) * blk, :].astype(F32) * (HEAD_DIM ** -0.5))
        q_heads.append([t.astype(BF16) for t in heads])

    def walk(jobs, fresh):
        for sub, blocks in jobs:
            for h in range(HEADS_PER_STEP):
                carry = None if fresh else carry_ref[sub, h]
                total = None if fresh else acc_ref[sub, h]
                for j, diagonal in blocks:
                    start = j * blk if isinstance(j, int) else pl.multiple_of(j * blk, blk)
                    z = lax.dot_general(q_heads[sub][h], k_ref[0, pl.ds(start, blk), :], _NT,
                                        preferred_element_type=F32)
                    log_1m = -(jnp.maximum(z, 0.0) + jnp.log(1.0 + jnp.exp(-jnp.abs(z))))
                    masked = jnp.where(past, log_1m, 0.0) if diagonal else log_1m
                    hi = masked.astype(BF16)
                    lo = (masked - hi.astype(F32)).astype(BF16)
                    after = (jnp.dot(hi, later, preferred_element_type=F32)
                             + jnp.dot(lo, later, preferred_element_type=F32))
                    row_sum = after[:, 0:1] + masked[:, 0:1]
                    if carry is not None:
                        after = after + carry
                    a = jnp.exp(z + log_1m + after)
                    if diagonal:
                        a = jnp.where(past, a, 0.0)
                    pv = jnp.dot(a.astype(BF16), v_ref[0, pl.ds(start, blk), :], preferred_element_type=F32)
                    total = pv if total is None else total + pv
                    carry = row_sum if carry is None else carry + row_sum
                carry_ref[sub, h] = carry
                acc_ref[sub, h] = total

    @pl.when(ti == 0)
    def _first_step():
        walk([(sub, [(sub, True)] + ([(sub - 1, False)] if sub else [])) for sub in range(n_sub)], True)

    @pl.when(ti > 0)
    def _own_and_previous():
        walk([(sub, [(first + sub, True), (first + sub - 1, False)]) for sub in range(n_sub)], True)

    for sub in range(n_sub):
        def slowest_decay(sub=sub):
            return jnp.maximum(jnp.max(carry_ref[sub, 0]), jnp.max(carry_ref[sub, 1]))

        def cond(state):
            j, worst = state
            return jnp.logical_and(j >= 0, worst > SB_EXIT_LOG)

        def body(state, sub=sub, slowest_decay=slowest_decay):
            j, _ = state
            walk([(sub, [(j, False)])], False)
            return j - 1, slowest_decay()

        lax.while_loop(cond, body, (first + sub - 2, slowest_decay()))
        o_ref[0, sub * blk:(sub + 1) * blk, :] = jnp.where(
            _head_lane_mask(), acc_ref[sub, 0], acc_ref[sub, 1]).astype(o_ref.dtype)


def _stick_breaking(proj):
    bsz, s, _ = proj.shape
    blk = ATT_BLOCK
    qt = SB_Q_BLOCKS * blk
    n_hp = N_HEADS // HEADS_PER_STEP
    w = N_HEADS * HEAD_DIM
    base = 3 * n_hp
    return pl.pallas_call(
        _sb_kernel,
        grid=(n_hp, bsz, s // qt),
        in_specs=[
            pl.BlockSpec((1, qt, LANES), lambda hp, b, i: (b, i, base + hp)),
            pl.BlockSpec((1, s, LANES), lambda hp, b, i: (b, 0, base + n_hp + hp)),
            pl.BlockSpec((1, s, LANES), lambda hp, b, i: (b, 0, base + 2 * n_hp + hp)),
        ],
        out_specs=pl.BlockSpec((1, qt, LANES), lambda hp, b, i: (b, i, hp)),
        out_shape=jax.ShapeDtypeStruct((bsz, s, w), BF16),
        scratch_shapes=[
            pltpu.VMEM((SB_Q_BLOCKS, HEADS_PER_STEP, blk, LANES), F32),
            pltpu.VMEM((SB_Q_BLOCKS, HEADS_PER_STEP, blk, 1), F32),
        ],
        compiler_params=_params("arbitrary", "arbitrary", "arbitrary"),
        name="stickbrk",
    )(proj, proj, proj)


def _pack_rows(t):
    n = t.shape[1] // 2
    bits = pltpu.bitcast(t.astype(BF16).astype(F32), U32)
    return bits[:, :n] | (bits[:, n:] >> 16)


def _unpack_rows(p):
    hi = pltpu.bitcast(p & jnp.uint32(0xFFFF0000), F32)
    lo = pltpu.bitcast(p << 16, F32)
    return jnp.concatenate([hi, lo], axis=1)


def _merge_kernel(x_ref, oa_ref, ob_ref, ga_ref, gb_ref, mod_ref, wbm_ref, wbs_ref, wo_ref, gf_ref,
                  wrh_ref, wrl_ref, br_ref, x1_ref, h2_ref, lg_ref):
    ma = jnp.dot(oa_ref[0], wbm_ref[...], preferred_element_type=F32)
    mb = jnp.dot(ob_ref[0], wbs_ref[...], preferred_element_type=F32)
    merged = (jax.nn.sigmoid(ga_ref[0].astype(F32)) * ma + jax.nn.sigmoid(gb_ref[0].astype(F32)) * mb)
    t = jnp.dot(merged.astype(BF16), wo_ref[...], preferred_element_type=F32)
    x1 = x_ref[0] + mod_ref[0, 2:3, :] * t
    x1_ref[0] = x1
    h2 = _rms_modulate(x1, gf_ref[...], mod_ref[0, 3:4, :], mod_ref[0, 4:5, :])
    hi = h2.astype(BF16)
    h2_ref[0] = hi
    lo = (h2 - hi.astype(F32)).astype(BF16)
    lg_ref[0] = (jnp.dot(hi, wrh_ref[...], preferred_element_type=F32)
                 + jnp.dot(lo, wrh_ref[...], preferred_element_type=F32)
                 + jnp.dot(hi, wrl_ref[...], preferred_element_type=F32) + br_ref[...])


def _merge(x, o_a, o_b, proj, mod, wbm, wbs, wo, g_ffn, w_router, b_router):
    bsz, s, d = x.shape
    w = o_a.shape[-1]
    tm = TOKEN_TILE
    gate_blk = (proj.shape[-1] - 2 * d) // d
    const = lambda shape: pl.BlockSpec(shape, lambda b, i: (0,) * len(shape))
    tok = lambda width: pl.BlockSpec((1, tm, width), lambda b, i: (b, i, 0))
    wr_hi = w_router.astype(BF16)
    wr_lo = (w_router - wr_hi.astype(F32)).astype(BF16)
    return pl.pallas_call(
        _merge_kernel,
        grid=(bsz, s // tm),
        in_specs=[
            tok(d), tok(w), tok(w),
            pl.BlockSpec((1, tm, d), lambda b, i: (b, i, gate_blk)),
            pl.BlockSpec((1, tm, d), lambda b, i: (b, i, gate_blk + 1)),
            pl.BlockSpec((1, N_MOD, d), lambda b, i: (b, 0, 0)),
            const((w, d)), const((w, d)), const((d, d)), const((1, d)),
            const((d, LANES)), const((d, LANES)), const((1, LANES)),
        ],
        out_specs=[tok(d), tok(d), tok(LANES)],
        out_shape=[
            jax.ShapeDtypeStruct((bsz, s, d), F32),
            jax.ShapeDtypeStruct((bsz, s, d), BF16),
            jax.ShapeDtypeStruct((bsz, s, LANES), F32),
        ],
        compiler_params=_params("arbitrary", "arbitrary"),
        name="merge",
    )(x, o_a, o_b, proj, proj, mod, wbm, wbs, wo, g_ffn.reshape(1, d), wr_hi, wr_lo, b_router)


ROUTE_COLS = 8
SEG_ALIGN = 8


def _local_rows(tm):
    return 2 * tm + N_EXPERTS * SEG_ALIGN


def _route_kernel(lg_ref, rt_ref, slot_t_ref, cnt_ref, before_ref, carry_ref):
    i = pl.program_id(0)
    tm = lg_ref.shape[0]
    lane = _lane_index()

    @pl.when(i == 0)
    def _init():
        carry_ref[...] = jnp.zeros_like(carry_ref)

    lg = lg_ref[...]

    def first_max(vals):
        top = jnp.max(vals, axis=-1, keepdims=True)
        idx = jnp.min(jnp.where(vals == top, lane, float(LANES)), axis=-1, keepdims=True)
        return top, idx

    g_logit = jnp.where(lane < N_GROUPS, lg, NEG_INF)
    g_top, g_idx = first_max(g_logit)
    g_w = 1.0 / jnp.sum(jnp.exp(g_logit - g_top), axis=-1, keepdims=True)
    lo = N_GROUPS + g_idx * EXPERTS_PER_GROUP
    e_logit = jnp.where(jnp.logical_and(lane >= lo, lane < lo + EXPERTS_PER_GROUP), lg, NEG_INF)
    e_top0, lane0 = first_max(e_logit)
    e_top1, lane1 = first_max(jnp.where(lane == lane0, NEG_INF, e_logit))
    z = jnp.exp(e_top1 - e_top0)
    w0 = g_w / (1.0 + z)
    w1 = g_w * z / (1.0 + z)

    hot = jnp.logical_or(lane == lane0, lane == lane1)
    row = lax.broadcasted_iota(jnp.int32, (tm, tm), 0)
    col = lax.broadcasted_iota(jnp.int32, (tm, tm), 1)
    earlier = jnp.dot(_indicator(col < row, BF16), _indicator(hot, BF16), preferred_element_type=F32)
    cnt = jnp.sum(_indicator(hot, F32), axis=0, keepdims=True)
    chunks = jnp.ceil(cnt * (1.0 / SEG_ALIGN))
    src = lax.broadcasted_iota(jnp.int32, (LANES, LANES), 0)
    dst = lax.broadcasted_iota(jnp.int32, (LANES, LANES), 1)
    seg_start = SEG_ALIGN * jnp.dot(jnp.broadcast_to(chunks, (8, LANES)).astype(BF16), _indicator(src < dst, BF16),
                                    preferred_element_type=F32)[0:1]
    local = seg_start + earlier
    slot0 = jnp.sum(jnp.where(lane == lane0, local, 0.0), axis=-1, keepdims=True)
    slot1 = jnp.sum(jnp.where(lane == lane1, local, 0.0), axis=-1, keepdims=True)
    cnt_ref[0] = SEG_ALIGN * chunks
    before_ref[0] = carry_ref[...]
    carry_ref[...] += SEG_ALIGN * chunks

    packed = jnp.zeros((tm, LANES), F32)
    for c, v in enumerate([slot0, slot1, w0, w1]):
        packed = jnp.where(lane == float(c), v, packed)
    rt_ref[...] = packed[:, :ROUTE_COLS]
    slot_t_ref[0] = packed.T[:ROUTE_COLS, :]


def _route(logits):
    n = logits.shape[0]
    tm = TOKEN_TILE
    tiles = n // tm
    per_tile = lambda rows, cols: (pl.BlockSpec((1, rows, cols), lambda i: (i, 0, 0)),
                                   jax.ShapeDtypeStruct((tiles, rows, cols), F32))
    specs, shapes = zip((pl.BlockSpec((tm, ROUTE_COLS), lambda i: (i, 0)), jax.ShapeDtypeStruct((n, ROUTE_COLS), F32)),
                        per_tile(ROUTE_COLS, tm), per_tile(1, LANES), per_tile(1, LANES))
    return pl.pallas_call(
        _route_kernel,
        grid=(tiles,),
        in_specs=[pl.BlockSpec((tm, LANES), lambda i: (i, 0))],
        out_specs=list(specs),
        out_shape=list(shapes),
        scratch_shapes=[pltpu.VMEM((1, LANES), F32)],
        compiler_params=_params("arbitrary"),
        name="route",
    )(logits)


def _for_each_chunk(tile, n_ref, rows_ref, visit):
    max_chunks = rows_ref.shape[0] // n_ref.shape[0]

    def per_chunk(c, carry):
        visit(pl.multiple_of(c * SEG_ALIGN, SEG_ALIGN), pl.multiple_of(rows_ref[tile * max_chunks + c], SEG_ALIGN))
        return carry

    lax.fori_loop(0, n_ref[tile], per_chunk, 0)


def _dispatch_kernel(n_ref, rows_ref, slot_t_ref, h_ref, xs_in_ref, xs_ref, loc_ref, sems):
    del xs_in_ref
    tile = pl.program_id(0)
    n_loc, tm = loc_ref.shape[1], h_ref.shape[0]
    buf = tile & 1

    slot = lax.broadcasted_iota(jnp.int32, (n_loc, tm), 0).astype(F32)
    place = jnp.logical_or(slot == slot_t_ref[0, 0:1, :], slot == slot_t_ref[0, 1:2, :])
    loc_ref[buf] = _pack_rows(jnp.dot(_indicator(place, BF16), h_ref[...], preferred_element_type=F32))

    def chunk_copy(b):
        return lambda local, row: pltpu.make_async_copy(
            loc_ref.at[b, pl.ds(local, SEG_ALIGN), :], xs_ref.at[pl.ds(row, SEG_ALIGN), :], sems.at[b])

    _for_each_chunk(tile, n_ref, rows_ref, lambda l, r: chunk_copy(buf)(l, r).start())

    @pl.when(tile > 0)
    def _previous_done():
        _for_each_chunk(tile - 1, n_ref, rows_ref, lambda l, r: chunk_copy(1 - buf)(l, r).wait())

    @pl.when(tile == pl.num_programs(0) - 1)
    def _last_done():
        _for_each_chunk(tile, n_ref, rows_ref, lambda l, r: chunk_copy(buf)(l, r).wait())


def _dispatch(n_chunks, chunk_rows, slot_t, h2, n_rows):
    n, d = h2.shape
    tm = TOKEN_TILE
    xs0 = jnp.zeros((n_rows, d // 2), U32)
    grid_spec = pltpu.PrefetchScalarGridSpec(
        num_scalar_prefetch=2,
        grid=(n // tm,),
        in_specs=[
            pl.BlockSpec((1, ROUTE_COLS, tm), lambda i, c, r: (i, 0, 0)),
            pl.BlockSpec((tm, d), lambda i, c, r: (i, 0)),
            pl.BlockSpec(memory_space=pl.ANY),
        ],
        out_specs=pl.BlockSpec(memory_space=pl.ANY),
        scratch_shapes=[pltpu.VMEM((2, _local_rows(tm), d // 2), U32), pltpu.SemaphoreType.DMA((2,))],
    )
    return pl.pallas_call(
        _dispatch_kernel,
        grid_spec=grid_spec,
        out_shape=jax.ShapeDtypeStruct((n_rows, d // 2), U32),
        input_output_aliases={4: 0},
        compiler_params=_params("arbitrary"),
        name="dispatch",
    )(n_chunks, chunk_rows, slot_t, h2, xs0)


def _expert_kernel(blk_e_ref, n_used_ref, xs_ref, wg_ref, wu_ref, wd_ref, ys_ref, wg_bf, wu_bf, wd_bf):
    i = pl.program_id(0)
    used = i < n_used_ref[0]

    @pl.when(jnp.logical_and(used, jnp.logical_or(i == 0, blk_e_ref[i] != blk_e_ref[jnp.maximum(i - 1, 0)])))
    def _new_expert():
        wg_bf[...] = wg_ref[0].astype(BF16)
        wu_bf[...] = wu_ref[0].astype(BF16)
        wd_bf[...] = wd_ref[0].astype(BF16)

    @pl.when(used)
    def _compute():
        xb = _unpack_rows(xs_ref[...]).astype(BF16)
        g = jnp.dot(xb, wg_bf[...], preferred_element_type=F32)
        u = jnp.dot(xb, wu_bf[...], preferred_element_type=F32)
        hid = (g * jax.nn.sigmoid(g) * u).astype(BF16)
        ys_ref[...] = _pack_rows(jnp.dot(hid, wd_bf[...], preferred_element_type=F32))

    @pl.when(i >= n_used_ref[0])
    def _unused():
        ys_ref[...] = jnp.zeros_like(ys_ref)


def _experts(blk_e, n_used, xs, wg, wu, wd):
    n_rows, half = xs.shape
    _, d, de = wg.shape
    grid_spec = pltpu.PrefetchScalarGridSpec(
        num_scalar_prefetch=2,
        grid=(n_rows // ROW_BLOCK,),
        in_specs=[
            pl.BlockSpec((ROW_BLOCK, half), lambda i, e, u: (i, 0)),
            pl.BlockSpec((1, d, de), lambda i, e, u: (e[i], 0, 0)),
            pl.BlockSpec((1, d, de), lambda i, e, u: (e[i], 0, 0)),
            pl.BlockSpec((1, de, d), lambda i, e, u: (e[i], 0, 0)),
        ],
        out_specs=pl.BlockSpec((ROW_BLOCK, half), lambda i, e, u: (i, 0)),
        scratch_shapes=[pltpu.VMEM((d, de), BF16), pltpu.VMEM((d, de), BF16), pltpu.VMEM((de, d), BF16)],
    )
    return pl.pallas_call(
        _expert_kernel,
        grid_spec=grid_spec,
        out_shape=jax.ShapeDtypeStruct((n_rows, half), U32),
        compiler_params=_params("arbitrary"),
        name="experts",
    )(blk_e, n_used, xs, wg, wu, wd)


def _combine_kernel(n_ref, rows_ref, x1_ref, mod_ref, rt_ref, ys_ref, o_ref, loc_ref, sems):
    tm = x1_ref.shape[1]
    n_loc = loc_ref.shape[1]
    tile = pl.program_id(0) * pl.num_programs(1) + pl.program_id(1)
    n_tiles = pl.num_programs(0) * pl.num_programs(1)
    buf = tile & 1

    def chunk_copy(b):
        return lambda local, row: pltpu.make_async_copy(
            ys_ref.at[pl.ds(row, SEG_ALIGN), :], loc_ref.at[b, pl.ds(local, SEG_ALIGN), :], sems.at[b])

    @pl.when(tile == 0)
    def _init():
        loc_ref[...] = jnp.zeros_like(loc_ref)
        _for_each_chunk(tile, n_ref, rows_ref, lambda l, r: chunk_copy(buf)(l, r).start())

    @pl.when(tile + 1 < n_tiles)
    def _fetch_next():
        _for_each_chunk(tile + 1, n_ref, rows_ref, lambda l, r: chunk_copy(1 - buf)(l, r).start())

    _for_each_chunk(tile, n_ref, rows_ref, lambda l, r: chunk_copy(buf)(l, r).wait())

    y = _unpack_rows(loc_ref[buf]).astype(BF16)
    rt = rt_ref[0]
    slot = lax.broadcasted_iota(jnp.int32, (tm, n_loc), 1).astype(F32)
    y0 = jnp.dot(_indicator(slot == rt[:, 0:1], BF16), y, preferred_element_type=F32)
    y1 = jnp.dot(_indicator(slot == rt[:, 1:2], BF16), y, preferred_element_type=F32)
    o_ref[0] = x1_ref[0] + mod_ref[0, 5:6, :] * (rt[:, 2:3] * y0 + rt[:, 3:4] * y1)


def _combine(n_chunks, chunk_rows, x1, mod, rt, ys):
    bsz, s, d = x1.shape
    tm = TOKEN_TILE
    grid_spec = pltpu.PrefetchScalarGridSpec(
        num_scalar_prefetch=2,
        grid=(bsz, s // tm),
        in_specs=[
            pl.BlockSpec((1, tm, d), lambda b, i, c, r: (b, i, 0)),
            pl.BlockSpec((1, N_MOD, d), lambda b, i, c, r: (b, 0, 0)),
            pl.BlockSpec((1, tm, ROUTE_COLS), lambda b, i, c, r: (b, i, 0)),
            pl.BlockSpec(memory_space=pl.ANY),
        ],
        out_specs=pl.BlockSpec((1, tm, d), lambda b, i, c, r: (b, i, 0)),
        scratch_shapes=[pltpu.VMEM((2, _local_rows(tm), d // 2), U32), pltpu.SemaphoreType.DMA((2,))],
    )
    return pl.pallas_call(
        _combine_kernel,
        grid_spec=grid_spec,
        out_shape=jax.ShapeDtypeStruct((bsz, s, d), F32),
        compiler_params=_params("arbitrary", "arbitrary"),
        name="combine",
    )(n_chunks, chunk_rows, x1, mod, rt, ys)


def _layer(x, mod, g_mix, w_in, g_q, g_k, rel_bias, w_br_moba, w_br_sb, w_out, g_ffn,
           w_rg, b_rg, w_re, b_re, w_gate, w_up, w_down):
    bsz, s, d = x.shape
    n = bsz * s
    n_blk = s // ATT_BLOCK

    proj, kaug, vaug, kmean = _inproj(x, mod, g_mix, w_in.astype(BF16), g_k)

    top = (n_blk + 1) * ATT_BLOCK - 1
    dist = np.maximum(top - np.arange((n_blk + 2) * ATT_BLOCK), 0)
    rel_rev = rel_bias[:, _rel_bucket_table(top)[dist]].reshape(N_HEADS // HEADS_PER_STEP, HEADS_PER_STEP, -1)

    o_a = _moba(proj, kaug, vaug, kmean, g_q, rel_rev)
    o_b = _stick_breaking(proj)

    pad = LANES - N_GROUPS - N_EXPERTS
    w_router = jnp.concatenate([w_rg, w_re, jnp.zeros((d, pad), F32)], axis=1)
    b_router = jnp.concatenate([b_rg, b_re, jnp.zeros((pad,), F32)]).reshape(1, LANES)
    x1, h2, logits = _merge(x, o_a, o_b, proj, mod, w_br_moba.astype(BF16), w_br_sb.astype(BF16),
                            w_out.astype(BF16), g_ffn, w_router, b_router)

    rt, slot_t, tile_cnt, tile_before = _route(logits.reshape(n, LANES))

    experts = slice(N_GROUPS, N_GROUPS + N_EXPERTS)
    cnt = tile_cnt[:, 0, experts].astype(jnp.int32)
    total = jnp.sum(cnt, axis=0)
    padded = (total + ROW_BLOCK - 1) // ROW_BLOCK * ROW_BLOCK
    pend = jnp.cumsum(padded)
    pstart = pend - padded
    max_rows = 2 * n + cnt.shape[0] * N_EXPERTS * (SEG_ALIGN - 1) + N_EXPERTS * (ROW_BLOCK - 1)
    n_blocks = -(-max_rows // ROW_BLOCK)
    blk_row0 = jnp.arange(n_blocks, dtype=jnp.int32) * ROW_BLOCK
    blk_e = jnp.minimum(jnp.sum((blk_row0[:, None] >= pend[None, :]).astype(jnp.int32), axis=1), N_EXPERTS - 1)
    n_used = (pend[-1] // ROW_BLOCK).astype(jnp.int32).reshape(1)
    seg_row0 = pstart[None, :] + tile_before[:, 0, experts].astype(jnp.int32)
    chunks = cnt // SEG_ALIGN
    chunk_end = jnp.cumsum(chunks, axis=1)
    chunk_start = (chunk_end - chunks)[:, None, :]
    c = jnp.arange(_local_rows(TOKEN_TILE) // SEG_ALIGN, dtype=jnp.int32)[None, :, None]
    inside = jnp.logical_and(c >= chunk_start, c < chunk_end[:, None, :])
    chunk_rows = jnp.sum(jnp.where(inside, seg_row0[:, None, :] + SEG_ALIGN * (c - chunk_start), 0), axis=2).reshape(-1)
    n_chunks = chunk_end[:, -1]

    xs = _dispatch(n_chunks, chunk_rows, slot_t, h2.reshape(n, d), n_blocks * ROW_BLOCK)
    ys = _experts(blk_e, n_used, xs, w_gate, w_up, w_down)
    return _combine(n_chunks, chunk_rows, x1, mod, rt.reshape(bsz, s, ROUTE_COLS), ys)


def kernel(x, c, w_ada, b_ada, g_mix, w_in, g_q, g_k, rel_bias, w_br_moba, w_br_sb, w_out, g_ffn,
           w_rg, b_rg, w_re, b_re, w_gate, w_up, w_down):
    bsz, s, d = x.shape
    assert s % TOKEN_TILE == 0 and s % MOBA_Q_TILE == 0 and d % (2 * LANES) == 0
    for l in range(w_ada.shape[0]):
        mod = _adaln(c, w_ada[l], b_ada[l]).reshape(bsz, N_MOD, d)
        x = _layer(x, mod, g_mix[l], w_in[l], g_q[l], g_k[l], rel_bias, w_br_moba[l], w_br_sb[l], w_out[l],
                   g_ffn[l], w_rg[l], b_rg[l], w_re[l], b_re[l], w_gate[l], w_up[l], w_down[l])
    return x
```

```python
import functools
import math

import numpy as np
import jax
import jax.numpy as jnp
from jax import lax
from jax.experimental import pallas as pl
from jax.experimental.pallas import tpu as pltpu

F32 = jnp.float32
BF16 = jnp.bfloat16
U32 = jnp.uint32
HIGHEST = lax.Precision.HIGHEST

LANES = 128
VMEM_LIMIT_BYTES = 56 * 1024 * 1024

HEAD_DIM = 64
N_HEADS = 8
HEADS_PER_STEP = LANES // HEAD_DIM
ATT_BLOCK = 256
MOBA_Q_TILE = 2 * ATT_BLOCK
SB_Q_BLOCKS = 2
MOBA_TOPK = 3
REL_BUCKETS = 32
REL_MAX_DIST = 1024
N_GROUPS = 4
EXPERTS_PER_GROUP = 8
N_EXPERTS = N_GROUPS * EXPERTS_PER_GROUP
N_MOD = 6
RMS_EPS = 1e-6
ROW_BLOCK = 512
TOKEN_TILE = 512
NEG_INF = float("-inf")
MASK_LOGIT = -1e30
SB_EXIT_LOG = -110.0

_NT = (((1,), (1,)), ((), ()))


def _params(*semantics):
    return pltpu.CompilerParams(dimension_semantics=semantics, vmem_limit_bytes=VMEM_LIMIT_BYTES)


def _adaln_kernel(c_ref, w_ref, b_ref, o_ref):
    c = c_ref[...]
    s = c * jax.nn.sigmoid(c)
    o_ref[...] = jnp.dot(s, w_ref[...], preferred_element_type=F32, precision=HIGHEST) + b_ref[...]


def _adaln(c, w, b):
    bsz, d = c.shape
    n = w.shape[1]
    tn = 1536
    return pl.pallas_call(
        _adaln_kernel,
        grid=(n // tn,),
        in_specs=[
            pl.BlockSpec((bsz, d), lambda j: (0, 0)),
            pl.BlockSpec((d, tn), lambda j: (0, j)),
            pl.BlockSpec((1, tn), lambda j: (0, j)),
        ],
        out_specs=pl.BlockSpec((bsz, tn), lambda j: (0, j)),
        out_shape=jax.ShapeDtypeStruct((bsz, n), F32),
        compiler_params=_params("arbitrary"),
        name="adaln",
    )(c, w, b.reshape(1, n))


def _rms_modulate(x, g, shift, scale):
    y = x * lax.rsqrt(jnp.mean(x * x, axis=-1, keepdims=True) + RMS_EPS) * g
    return y * (1.0 + scale) + shift


def _inproj_kernel(x_ref, mod_ref, g_ref, w_ref, gk_ref, o_ref, kaug_ref, vaug_ref, kmean_ref, *, col_chunk):
    i = pl.program_id(1)
    tm = x_ref.shape[1]
    h = _rms_modulate(x_ref[0], g_ref[...], mod_ref[0, 0:1, :], mod_ref[0, 1:2, :]).astype(BF16)

    w_moba = N_HEADS * HEAD_DIM
    blocks = tm // ATT_BLOCK
    first = _head_lane_mask()
    row_blk = blocks * i + lax.shift_right_logical(lax.broadcasted_iota(jnp.int32, (tm, LANES), 0),
                                                   ATT_BLOCK.bit_length() - 1)
    tag = jnp.where(_lane_index() - float(HEAD_DIM) == row_blk.astype(F32), 1.0, 0.0)
    mean_row = lax.broadcasted_iota(jnp.int32, kmean_ref.shape[2:], 0)

    @pl.when(i == 0)
    def _init():
        kmean_ref[...] = jnp.zeros_like(kmean_ref)

    for n in range(w_ref.shape[1] // col_chunk):
        cols = slice(n * col_chunk, (n + 1) * col_chunk)
        r = jnp.dot(h, w_ref[:, cols], preferred_element_type=F32)
        o_ref[0, :, cols] = r.astype(BF16)
        for c in range(col_chunk // LANES):
            col0 = n * col_chunk + c * LANES
            part = r[:, c * LANES:(c + 1) * LANES]
            if w_moba <= col0 < 2 * w_moba:
                kn = _head_rmsnorm(part, gk_ref[...])
                for hh, kh in enumerate((kn, pltpu.roll(kn, HEAD_DIM, 1))):
                    head = (col0 - w_moba) // HEAD_DIM + hh
                    kaug_ref[0, head] = jnp.where(first, kh, tag).astype(BF16)
                    means = kmean_ref[0, head]
                    for jb in range(blocks):
                        mean = jnp.mean(kh[jb * ATT_BLOCK:(jb + 1) * ATT_BLOCK], axis=0, keepdims=True)
                        means = jnp.where(mean_row == blocks * i + jb, jnp.where(first, mean, 0.0), means)
                    kmean_ref[0, head] = means
            elif 2 * w_moba <= col0 < 3 * w_moba:
                for hh, vh in enumerate((part, pltpu.roll(part, HEAD_DIM, 1))):
                    head = (col0 - 2 * w_moba) // HEAD_DIM + hh
                    vaug_ref[0, head] = jnp.where(first, vh, 1.0).astype(BF16)


def _inproj(x, mod, g, w_bf16, g_k):
    bsz, s, d = x.shape
    n = w_bf16.shape[1]
    tm = TOKEN_TILE
    n_blk = s // ATT_BLOCK
    per_head = lambda rows: pl.BlockSpec((1, N_HEADS, rows, LANES), lambda b, i: (b, 0, i, 0))
    return pl.pallas_call(
        functools.partial(_inproj_kernel, col_chunk=1024),
        grid=(bsz, s // tm),
        in_specs=[
            pl.BlockSpec((1, tm, d), lambda b, i: (b, i, 0)),
            pl.BlockSpec((1, N_MOD, d), lambda b, i: (b, 0, 0)),
            pl.BlockSpec((1, d), lambda b, i: (0, 0)),
            pl.BlockSpec((d, n), lambda b, i: (0, 0)),
            pl.BlockSpec((1, LANES), lambda b, i: (0, 0)),
        ],
        out_specs=[
            pl.BlockSpec((1, tm, n), lambda b, i: (b, i, 0)),
            per_head(tm),
            per_head(tm),
            pl.BlockSpec((1, N_HEADS, -(-n_blk // 8) * 8, LANES), lambda b, i: (b, 0, 0, 0)),
        ],
        out_shape=[
            jax.ShapeDtypeStruct((bsz, s, n), BF16),
            jax.ShapeDtypeStruct((bsz, N_HEADS, s, LANES), BF16),
            jax.ShapeDtypeStruct((bsz, N_HEADS, s, LANES), BF16),
            jax.ShapeDtypeStruct((bsz, N_HEADS, -(-n_blk // 8) * 8, LANES), F32),
        ],
        compiler_params=_params("arbitrary", "arbitrary"),
        name="inproj",
    )(x, mod, g.reshape(1, d), w_bf16, jnp.tile(g_k.reshape(1, HEAD_DIM), (1, HEADS_PER_STEP)))


def _indicator(cond, dtype):
    return jnp.where(cond, 1.0, 0.0).astype(dtype)


def _lane_index():
    return lax.broadcasted_iota(jnp.int32, (1, LANES), 1).astype(F32)


def _head_lane_mask():
    return lax.broadcasted_iota(jnp.int32, (1, LANES), 1) < HEAD_DIM


def _head_rmsnorm(t, g):
    first = _head_lane_mask()
    sq = t * t
    ss0 = jnp.sum(jnp.where(first, sq, 0.0), axis=-1, keepdims=True)
    ss1 = jnp.sum(jnp.where(first, 0.0, sq), axis=-1, keepdims=True)
    inv = jnp.where(first, lax.rsqrt(ss0 / HEAD_DIM + RMS_EPS), lax.rsqrt(ss1 / HEAD_DIM + RMS_EPS))
    return t * inv * g


def _split_heads(t):
    first = _head_lane_mask()
    zero = jnp.zeros_like(t)
    return jnp.where(first, t, zero), jnp.where(first, zero, t)


def _rel_bucket_table(max_dist):
    n = np.arange(max_dist + 1)
    max_exact = REL_BUCKETS // 2
    nf = np.maximum(n, 1).astype(np.float64)
    large = max_exact + (np.log(nf / max_exact) / math.log(REL_MAX_DIST / max_exact)
                         * (REL_BUCKETS - max_exact)).astype(np.int64)
    large = np.minimum(large, REL_BUCKETS - 1)
    return np.where(n < max_exact, n, large).astype(np.int32)


def _moba_kernel(q_ref, kaug_ref, vaug_ref, kmean_ref, gq_ref, rb_ref, o_ref,
                 toep_ref, s_ref, mvec_ref, acc_ref, *, n_blk):
    b = pl.program_id(1)
    qi = pl.program_id(2)
    blk = ATT_BLOCK
    qt = q_ref.shape[1]
    halves = qt // blk
    first = _head_lane_mask()

    @pl.when(jnp.logical_and(b == 0, qi == 0))
    def _build_bias_tiles():
        for h in range(HEADS_PER_STEP):
            for d in range(n_blk):
                r = rb_ref[0, h:h + 1, (n_blk - d) * blk:(n_blk - d + 2) * blk]
                rolled = pltpu.roll(jnp.broadcast_to(r, (blk, 2 * blk)), blk + 1, 1, stride=1, stride_axis=0)
                toep_ref[h, d] = rolled[:, :blk]

    qn = _head_rmsnorm(q_ref[0].astype(F32), gq_ref[...])
    gate_rows = kmean_ref.shape[2]
    blk_row = lax.broadcasted_iota(jnp.int32, (gate_rows, qt), 0).astype(F32)
    own = (halves * qi + lax.shift_right_logical(lax.broadcasted_iota(jnp.int32, (gate_rows, qt), 1),
                                                 blk.bit_length() - 1)).astype(F32)
    q_aug = []
    for h, qh in enumerate((qn, pltpu.roll(qn, HEAD_DIM, 1))):
        qh = jnp.where(first, qh, 0.0)
        km = kmean_ref[0, h]
        km_hi, q_hi = km.astype(BF16), qh.astype(BF16)
        km_lo, q_lo = (km - km_hi.astype(F32)).astype(BF16), (qh - q_hi.astype(F32)).astype(BF16)
        g = (lax.dot_general(km_hi, q_hi, _NT, preferred_element_type=F32)
             + lax.dot_general(km_lo, q_hi, _NT, preferred_element_type=F32)
             + lax.dot_general(km_hi, q_lo, _NT, preferred_element_type=F32))
        g = jnp.where(blk_row < own, g, NEG_INF)
        keep = jnp.where(blk_row == own, 1.0, 0.0)
        for _ in range(MOBA_TOPK):
            top = jnp.max(g, axis=0, keepdims=True)
            is_top = jnp.logical_and(g == top, top > NEG_INF)
            idx = jnp.min(jnp.where(is_top, blk_row, float(gate_rows)), axis=0, keepdims=True)
            pick = blk_row == idx
            keep = jnp.where(pick, 1.0, keep)
            g = jnp.where(pick, NEG_INF, g)
        mask_t = jnp.where(keep > 0.0, 0.0, MASK_LOGIT)
        mask_logit = jnp.concatenate([jnp.zeros((HEAD_DIM, qt), F32), mask_t,
                                      jnp.zeros((LANES - HEAD_DIM - gate_rows, qt), F32)], axis=0).T
        q_aug.append(jnp.where(first, qh * (HEAD_DIM ** -0.5), mask_logit).astype(BF16))

    def lane_max(s):
        return jnp.maximum(s[:, :LANES], s[:, LANES:])

    def scores(h, t, dist):
        k_t = kaug_ref[0, h, pl.ds(pl.multiple_of(t * qt, qt), qt), :]
        s = lax.dot_general(q_aug[h], k_t, _NT, preferred_element_type=F32)
        tiles = []
        for kb in range(halves):
            rows = []
            for qb in range(halves):
                part = s[qb * blk:(qb + 1) * blk, kb * blk:(kb + 1) * blk]
                if isinstance(dist, int) and halves * dist + qb - kb < 0:
                    rows.append(part)
                else:
                    rows.append(part + toep_ref[h, halves * dist + qb - kb])
            tiles.append(jnp.concatenate(rows, axis=0))
        return tiles

    row = lax.broadcasted_iota(jnp.int32, (qt, blk), 0)
    col = lax.broadcasted_iota(jnp.int32, (qt, blk), 1)
    for h in range(HEADS_PER_STEP):
        top = None
        for kb, s in enumerate(scores(h, qi, 0)):
            s = jnp.where(col + kb * blk <= row, s, NEG_INF)
            s_ref[h, halves * qi + kb] = s
            top = lane_max(s) if top is None else jnp.maximum(top, lane_max(s))
        mvec_ref[h] = top

    def in_pairs(count, visit):
        def pair(p, carry):
            visit(2 * p, 2)
            return carry

        lax.fori_loop(0, lax.shift_right_logical(count, 1), pair, 0)

        @pl.when((count & 1) == 1)
        def _last():
            visit(count - 1, 1)

    def pass1(t0, n_tiles):
        for h in range(HEADS_PER_STEP):
            top = mvec_ref[h]
            for t in [t0 + u for u in range(n_tiles)]:
                for kb, s in enumerate(scores(h, t, qi - t)):
                    s_ref[h, halves * t + kb] = s
                    top = jnp.maximum(top, lane_max(s))
            mvec_ref[h] = top

    in_pairs(qi, pass1)

    for h in range(HEADS_PER_STEP):
        m = jnp.max(mvec_ref[h], axis=-1, keepdims=True)
        mvec_ref[h] = jnp.broadcast_to(m, (qt, LANES))
        acc_ref[h] = jnp.zeros((qt, LANES), F32)

    def pass2(t0, n_tiles):
        keys = pl.ds(pl.multiple_of(t0 * qt, qt), n_tiles * qt)
        for h in range(HEADS_PER_STEP):
            m = mvec_ref[h]
            parts = []
            for kb in range(n_tiles * halves):
                s = s_ref[h, halves * t0 + kb]
                parts += [jnp.exp(s[:, :LANES] - m), jnp.exp(s[:, LANES:] - m)]
            p = jnp.concatenate(parts, axis=1).astype(BF16)
            acc_ref[h] += jnp.dot(p, vaug_ref[0, h, keys, :], preferred_element_type=F32)

    in_pairs(qi + 1, pass2)

    out = [acc_ref[h] / acc_ref[h][:, HEAD_DIM:HEAD_DIM + 1] for h in range(HEADS_PER_STEP)]
    o_ref[0] = jnp.where(first, out[0], pltpu.roll(out[1], HEAD_DIM, 1)).astype(o_ref.dtype)


def _moba(proj, kaug, vaug, kmean, g_q, rel_rev):
    bsz, s, _ = proj.shape
    blk = ATT_BLOCK
    n_blk = s // blk
    n_hp = N_HEADS // HEADS_PER_STEP
    w = N_HEADS * HEAD_DIM
    qt = MOBA_Q_TILE
    per_pair = lambda arr: pl.BlockSpec((1, HEADS_PER_STEP) + arr.shape[2:], lambda hp, b, i: (b, hp, 0, 0))
    return pl.pallas_call(
        functools.partial(_moba_kernel, n_blk=n_blk),
        grid=(n_hp, bsz, s // qt),
        in_specs=[
            pl.BlockSpec((1, qt, LANES), lambda hp, b, i: (b, i, hp)),
            per_pair(kaug), per_pair(vaug), per_pair(kmean),
            pl.BlockSpec((1, LANES), lambda hp, b, i: (0, 0)),
            pl.BlockSpec((1, HEADS_PER_STEP, rel_rev.shape[-1]), lambda hp, b, i: (hp, 0, 0)),
        ],
        out_specs=pl.BlockSpec((1, qt, LANES), lambda hp, b, i: (b, i, hp)),
        out_shape=jax.ShapeDtypeStruct((bsz, s, w), BF16),
        scratch_shapes=[
            pltpu.VMEM((HEADS_PER_STEP, n_blk, blk, blk), F32),
            pltpu.VMEM((HEADS_PER_STEP, n_blk, qt, blk), F32),
            pltpu.VMEM((HEADS_PER_STEP, qt, LANES), F32),
            pltpu.VMEM((HEADS_PER_STEP, qt, LANES), F32),
        ],
        compiler_params=_params("arbitrary", "arbitrary", "arbitrary"),
        name="moba",
    )(proj, kaug, vaug, kmean, jnp.tile(g_q.reshape(1, HEAD_DIM), (1, HEADS_PER_STEP)), rel_rev)


def _sb_kernel(q_ref, k_ref, v_ref, o_ref, acc_ref, carry_ref):
    ti = pl.program_id(2)
    blk = ATT_BLOCK
    n_sub = q_ref.shape[1] // blk
    first = n_sub * ti
    row = lax.broadcasted_iota(jnp.int32, (blk, blk), 0)
    col = lax.broadcasted_iota(jnp.int32, (blk, blk), 1)
    past = col < row
    later = _indicator(row > col, BF16)

    q_heads = []
    for sub in range(n_sub):
        heads = _split_heads(q_ref[0, sub * blk:(sub + 1) * blk, :].astype(F32) * (HEAD_DIM ** -0.5))
        q_heads.append([t.astype(BF16) for t in heads])

    def walk(jobs, fresh):
        for sub, blocks in jobs:
            for h in range(HEADS_PER_STEP):
                carry = None if fresh else carry_ref[sub, h]
                total = None if fresh else acc_ref[sub, h]
                for j, diagonal in blocks:
                    start = j * blk if isinstance(j, int) else pl.multiple_of(j * blk, blk)
                    z = lax.dot_general(q_heads[sub][h], k_ref[0, pl.ds(start, blk), :], _NT,
                                        preferred_element_type=F32)
                    log_1m = -(jnp.maximum(z, 0.0) + jnp.log(1.0 + jnp.exp(-jnp.abs(z))))
                    masked = jnp.where(past, log_1m, 0.0) if diagonal else log_1m
                    hi = masked.astype(BF16)
                    lo = (masked - hi.astype(F32)).astype(BF16)
                    after = (jnp.dot(hi, later, preferred_element_type=F32)
                             + jnp.dot(lo, later, preferred_element_type=F32))
                    row_sum = after[:, 0:1] + masked[:, 0:1]
                    if carry is not None:
                        after = after + carry
                    a = jnp.exp(z + log_1m + after)
                    if diagonal:
                        a = jnp.where(past, a, 0.0)
                    pv = jnp.dot(a.astype(BF16), v_ref[0, pl.ds(start, blk), :], preferred_element_type=F32)
                    total = pv if total is None else total + pv
                    carry = row_sum if carry is None else carry + row_sum
                carry_ref[sub, h] = carry
                acc_ref[sub, h] = total

    @pl.when(ti == 0)
    def _first_step():
        walk([(sub, [(sub, True)] + ([(sub - 1, False)] if sub else [])) for sub in range(n_sub)], True)

    @pl.when(ti > 0)
    def _own_and_previous():
        walk([(sub, [(first + sub, True), (first + sub - 1, False)]) for sub in range(n_sub)], True)

    def slowest_decay(subs):
        worst = functools.reduce(jnp.maximum, [carry_ref[sub, h] for sub in subs for h in range(HEADS_PER_STEP)])
        return jnp.max(worst)

    @pl.when(slowest_decay(range(n_sub)) > SB_EXIT_LOG)
    def _walk_further_back():
        for sub in range(n_sub):
            def cond(state):
                j, worst = state
                return jnp.logical_and(j >= 0, worst > SB_EXIT_LOG)

            def body(state, sub=sub):
                j, _ = state
                walk([(sub, [(j, False)])], False)
                return j - 1, slowest_decay([sub])

            lax.while_loop(cond, body, (first + sub - 2, slowest_decay([sub])))

    for sub in range(n_sub):
        o_ref[0, sub * blk:(sub + 1) * blk, :] = jnp.where(
            _head_lane_mask(), acc_ref[sub, 0], acc_ref[sub, 1]).astype(o_ref.dtype)


def _stick_breaking(proj):
    bsz, s, _ = proj.shape
    blk = ATT_BLOCK
    qt = SB_Q_BLOCKS * blk
    n_hp = N_HEADS // HEADS_PER_STEP
    w = N_HEADS * HEAD_DIM
    base = 3 * n_hp
    return pl.pallas_call(
        _sb_kernel,
        grid=(n_hp, bsz, s // qt),
        in_specs=[
            pl.BlockSpec((1, qt, LANES), lambda hp, b, i: (b, i, base + hp)),
            pl.BlockSpec((1, s, LANES), lambda hp, b, i: (b, 0, base + n_hp + hp)),
            pl.BlockSpec((1, s, LANES), lambda hp, b, i: (b, 0, base + 2 * n_hp + hp)),
        ],
        out_specs=pl.BlockSpec((1, qt, LANES), lambda hp, b, i: (b, i, hp)),
        out_shape=jax.ShapeDtypeStruct((bsz, s, w), BF16),
        scratch_shapes=[
            pltpu.VMEM((SB_Q_BLOCKS, HEADS_PER_STEP, blk, LANES), F32),
            pltpu.VMEM((SB_Q_BLOCKS, HEADS_PER_STEP, blk, 1), F32),
        ],
        compiler_params=_params("arbitrary", "arbitrary", "arbitrary"),
        name="stickbrk",
    )(proj, proj, proj)


def _pack_rows(t):
    n = t.shape[1] // 2
    bits = pltpu.bitcast(t.astype(BF16).astype(F32), U32)
    return bits[:, :n] | (bits[:, n:] >> 16)


def _unpack_rows(p):
    hi = pltpu.bitcast(p & jnp.uint32(0xFFFF0000), F32)
    lo = pltpu.bitcast(p << 16, F32)
    return jnp.concatenate([hi, lo], axis=1)


def _merge_kernel(x_ref, oa_ref, ob_ref, ga_ref, gb_ref, mod_ref, wbm_ref, wbs_ref, wo_ref, gf_ref,
                  wrh_ref, wrl_ref, br_ref, x1_ref, h2_ref, rt_ref, slot_t_ref, cnt_ref, before_ref, carry_ref):
    ma = jnp.dot(oa_ref[0], wbm_ref[...], preferred_element_type=F32)
    mb = jnp.dot(ob_ref[0], wbs_ref[...], preferred_element_type=F32)
    merged = (jax.nn.sigmoid(ga_ref[0].astype(F32)) * ma + jax.nn.sigmoid(gb_ref[0].astype(F32)) * mb)
    t = jnp.dot(merged.astype(BF16), wo_ref[...], preferred_element_type=F32)
    x1 = x_ref[0] + mod_ref[0, 2:3, :] * t
    x1_ref[0] = x1
    h2 = _rms_modulate(x1, gf_ref[...], mod_ref[0, 3:4, :], mod_ref[0, 4:5, :])
    hi = h2.astype(BF16)
    h2_ref[0] = hi
    lo = (h2 - hi.astype(F32)).astype(BF16)
    logits = (jnp.dot(hi, wrh_ref[...], preferred_element_type=F32)
              + jnp.dot(lo, wrh_ref[...], preferred_element_type=F32)
              + jnp.dot(hi, wrl_ref[...], preferred_element_type=F32) + br_ref[...])
    first_tile = jnp.logical_and(pl.program_id(0) == 0, pl.program_id(1) == 0)
    _route_tile(logits, first_tile, rt_ref, slot_t_ref, cnt_ref, before_ref, carry_ref)


def _merge(x, o_a, o_b, proj, mod, wbm, wbs, wo, g_ffn, w_router, b_router):
    bsz, s, d = x.shape
    w = o_a.shape[-1]
    tm = TOKEN_TILE
    gate_blk = (proj.shape[-1] - 2 * d) // d
    const = lambda shape: pl.BlockSpec(shape, lambda b, i: (0,) * len(shape))
    tok = lambda width: pl.BlockSpec((1, tm, width), lambda b, i: (b, i, 0))
    wr_hi = w_router.astype(BF16)
    wr_lo = (w_router - wr_hi.astype(F32)).astype(BF16)
    per_b = s // tm
    tiles = bsz * per_b
    per_tile = lambda rows, cols: pl.BlockSpec((1, rows, cols), lambda b, i: (b * per_b + i, 0, 0))
    return pl.pallas_call(
        _merge_kernel,
        grid=(bsz, s // tm),
        in_specs=[
            tok(d), tok(w), tok(w),
            pl.BlockSpec((1, tm, d), lambda b, i: (b, i, gate_blk)),
            pl.BlockSpec((1, tm, d), lambda b, i: (b, i, gate_blk + 1)),
            pl.BlockSpec((1, N_MOD, d), lambda b, i: (b, 0, 0)),
            const((w, d)), const((w, d)), const((d, d)), const((1, d)),
            const((d, LANES)), const((d, LANES)), const((1, LANES)),
        ],
        out_specs=[tok(d), tok(d), tok(ROUTE_COLS), per_tile(ROUTE_COLS, tm), per_tile(1, LANES), per_tile(1, LANES)],
        out_shape=[
            jax.ShapeDtypeStruct((bsz, s, d), F32),
            jax.ShapeDtypeStruct((bsz, s, d), BF16),
            jax.ShapeDtypeStruct((bsz, s, ROUTE_COLS), F32),
            jax.ShapeDtypeStruct((tiles, ROUTE_COLS, tm), F32),
            jax.ShapeDtypeStruct((tiles, 1, LANES), F32),
            jax.ShapeDtypeStruct((tiles, 1, LANES), F32),
        ],
        scratch_shapes=[pltpu.VMEM((1, LANES), F32)],
        compiler_params=_params("arbitrary", "arbitrary"),
        name="merge",
    )(x, o_a, o_b, proj, proj, mod, wbm, wbs, wo, g_ffn.reshape(1, d), wr_hi, wr_lo, b_router)


ROUTE_COLS = 8
SEG_ALIGN = 8


def _local_rows(tm):
    return 2 * tm + N_EXPERTS * SEG_ALIGN


def _route_tile(lg, first_tile, rt_ref, slot_t_ref, cnt_ref, before_ref, carry_ref):
    tm = lg.shape[0]
    lane = _lane_index()

    @pl.when(first_tile)
    def _init():
        carry_ref[...] = jnp.zeros_like(carry_ref)

    def first_max(vals):
        top = jnp.max(vals, axis=-1, keepdims=True)
        idx = jnp.min(jnp.where(vals == top, lane, float(LANES)), axis=-1, keepdims=True)
        return top, idx

    g_logit = jnp.where(lane < N_GROUPS, lg, NEG_INF)
    g_top, g_idx = first_max(g_logit)
    g_w = 1.0 / jnp.sum(jnp.exp(g_logit - g_top), axis=-1, keepdims=True)
    lo = N_GROUPS + g_idx * EXPERTS_PER_GROUP
    e_logit = jnp.where(jnp.logical_and(lane >= lo, lane < lo + EXPERTS_PER_GROUP), lg, NEG_INF)
    e_top0, lane0 = first_max(e_logit)
    e_top1, lane1 = first_max(jnp.where(lane == lane0, NEG_INF, e_logit))
    z = jnp.exp(e_top1 - e_top0)
    w0 = g_w / (1.0 + z)
    w1 = g_w * z / (1.0 + z)

    hot = jnp.logical_or(lane == lane0, lane == lane1)
    row = lax.broadcasted_iota(jnp.int32, (tm, tm), 0)
    col = lax.broadcasted_iota(jnp.int32, (tm, tm), 1)
    earlier = jnp.dot(_indicator(col < row, BF16), _indicator(hot, BF16), preferred_element_type=F32)
    cnt = jnp.sum(_indicator(hot, F32), axis=0, keepdims=True)
    chunks = jnp.ceil(cnt * (1.0 / SEG_ALIGN))
    src = lax.broadcasted_iota(jnp.int32, (LANES, LANES), 0)
    dst = lax.broadcasted_iota(jnp.int32, (LANES, LANES), 1)
    seg_start = SEG_ALIGN * jnp.dot(jnp.broadcast_to(chunks, (8, LANES)).astype(BF16), _indicator(src < dst, BF16),
                                    preferred_element_type=F32)[0:1]
    local = seg_start + earlier
    slot0 = jnp.sum(jnp.where(lane == lane0, local, 0.0), axis=-1, keepdims=True)
    slot1 = jnp.sum(jnp.where(lane == lane1, local, 0.0), axis=-1, keepdims=True)
    cnt_ref[0] = SEG_ALIGN * chunks
    before_ref[0] = carry_ref[...]
    carry_ref[...] += SEG_ALIGN * chunks

    packed = jnp.zeros((tm, LANES), F32)
    for c, v in enumerate([slot0, slot1, w0, w1]):
        packed = jnp.where(lane == float(c), v, packed)
    rt_ref[0] = packed[:, :ROUTE_COLS]
    slot_t_ref[0] = packed.T[:ROUTE_COLS, :]


def _for_each_chunk(tile, n_ref, rows_ref, visit):
    max_chunks = rows_ref.shape[0] // n_ref.shape[0]

    def per_chunk(c, carry):
        visit(pl.multiple_of(c * SEG_ALIGN, SEG_ALIGN), pl.multiple_of(rows_ref[tile * max_chunks + c], SEG_ALIGN))
        return carry

    lax.fori_loop(0, n_ref[tile], per_chunk, 0)


def _dispatch_kernel(n_ref, rows_ref, slot_t_ref, h_ref, xs_in_ref, xs_ref, loc_ref, sems):
    del xs_in_ref
    tile = pl.program_id(0)
    n_loc, tm = loc_ref.shape[1], h_ref.shape[0]
    buf = tile & 1

    slot = lax.broadcasted_iota(jnp.int32, (n_loc, tm), 0).astype(F32)
    place = jnp.logical_or(slot == slot_t_ref[0, 0:1, :], slot == slot_t_ref[0, 1:2, :])
    loc_ref[buf] = _pack_rows(jnp.dot(_indicator(place, BF16), h_ref[...], preferred_element_type=F32))

    def chunk_copy(b):
        return lambda local, row: pltpu.make_async_copy(
            loc_ref.at[b, pl.ds(local, SEG_ALIGN), :], xs_ref.at[pl.ds(row, SEG_ALIGN), :], sems.at[b])

    _for_each_chunk(tile, n_ref, rows_ref, lambda l, r: chunk_copy(buf)(l, r).start())

    @pl.when(tile > 0)
    def _previous_done():
        _for_each_chunk(tile - 1, n_ref, rows_ref, lambda l, r: chunk_copy(1 - buf)(l, r).wait())

    @pl.when(tile == pl.num_programs(0) - 1)
    def _last_done():
        _for_each_chunk(tile, n_ref, rows_ref, lambda l, r: chunk_copy(buf)(l, r).wait())


def _dispatch(n_chunks, chunk_rows, slot_t, h2, n_rows):
    n, d = h2.shape
    tm = TOKEN_TILE
    xs0 = jnp.zeros((n_rows, d // 2), U32)
    grid_spec = pltpu.PrefetchScalarGridSpec(
        num_scalar_prefetch=2,
        grid=(n // tm,),
        in_specs=[
            pl.BlockSpec((1, ROUTE_COLS, tm), lambda i, c, r: (i, 0, 0)),
            pl.BlockSpec((tm, d), lambda i, c, r: (i, 0)),
            pl.BlockSpec(memory_space=pl.ANY),
        ],
        out_specs=pl.BlockSpec(memory_space=pl.ANY),
        scratch_shapes=[pltpu.VMEM((2, _local_rows(tm), d // 2), U32), pltpu.SemaphoreType.DMA((2,))],
    )
    return pl.pallas_call(
        _dispatch_kernel,
        grid_spec=grid_spec,
        out_shape=jax.ShapeDtypeStruct((n_rows, d // 2), U32),
        input_output_aliases={4: 0},
        compiler_params=_params("arbitrary"),
        name="dispatch",
    )(n_chunks, chunk_rows, slot_t, h2, xs0)


def _expert_kernel(blk_e_ref, n_used_ref, xs_ref, wg_ref, wu_ref, wd_ref, ys_ref, wg_bf, wu_bf, wd_bf):
    i = pl.program_id(0)
    used = i < n_used_ref[0]

    @pl.when(jnp.logical_and(used, jnp.logical_or(i == 0, blk_e_ref[i] != blk_e_ref[jnp.maximum(i - 1, 0)])))
    def _new_expert():
        wg_bf[...] = wg_ref[0].astype(BF16)
        wu_bf[...] = wu_ref[0].astype(BF16)
        wd_bf[...] = wd_ref[0].astype(BF16)

    @pl.when(used)
    def _compute():
        xb = _unpack_rows(xs_ref[...]).astype(BF16)
        g = jnp.dot(xb, wg_bf[...], preferred_element_type=F32)
        u = jnp.dot(xb, wu_bf[...], preferred_element_type=F32)
        hid = (g * jax.nn.sigmoid(g) * u).astype(BF16)
        ys_ref[...] = _pack_rows(jnp.dot(hid, wd_bf[...], preferred_element_type=F32))

    @pl.when(i >= n_used_ref[0])
    def _unused():
        ys_ref[...] = jnp.zeros_like(ys_ref)


def _experts(blk_e, n_used, xs, wg, wu, wd):
    n_rows, half = xs.shape
    _, d, de = wg.shape
    grid_spec = pltpu.PrefetchScalarGridSpec(
        num_scalar_prefetch=2,
        grid=(n_rows // ROW_BLOCK,),
        in_specs=[
            pl.BlockSpec((ROW_BLOCK, half), lambda i, e, u: (i, 0)),
            pl.BlockSpec((1, d, de), lambda i, e, u: (e[i], 0, 0)),
            pl.BlockSpec((1, d, de), lambda i, e, u: (e[i], 0, 0)),
            pl.BlockSpec((1, de, d), lambda i, e, u: (e[i], 0, 0)),
        ],
        out_specs=pl.BlockSpec((ROW_BLOCK, half), lambda i, e, u: (i, 0)),
        scratch_shapes=[pltpu.VMEM((d, de), BF16), pltpu.VMEM((d, de), BF16), pltpu.VMEM((de, d), BF16)],
    )
    return pl.pallas_call(
        _expert_kernel,
        grid_spec=grid_spec,
        out_shape=jax.ShapeDtypeStruct((n_rows, half), U32),
        compiler_params=_params("arbitrary"),
        name="experts",
    )(blk_e, n_used, xs, wg, wu, wd)


def _combine_kernel(n_ref, rows_ref, x1_ref, mod_ref, rt_ref, ys_ref, o_ref, loc_ref, sems):
    tm = x1_ref.shape[1]
    n_loc = loc_ref.shape[1]
    tile = pl.program_id(0) * pl.num_programs(1) + pl.program_id(1)
    n_tiles = pl.num_programs(0) * pl.num_programs(1)
    buf = tile & 1

    def chunk_copy(b):
        return lambda local, row: pltpu.make_async_copy(
            ys_ref.at[pl.ds(row, SEG_ALIGN), :], loc_ref.at[b, pl.ds(local, SEG_ALIGN), :], sems.at[b])

    @pl.when(tile == 0)
    def _init():
        loc_ref[...] = jnp.zeros_like(loc_ref)
        _for_each_chunk(tile, n_ref, rows_ref, lambda l, r: chunk_copy(buf)(l, r).start())

    @pl.when(tile + 1 < n_tiles)
    def _fetch_next():
        _for_each_chunk(tile + 1, n_ref, rows_ref, lambda l, r: chunk_copy(1 - buf)(l, r).start())

    _for_each_chunk(tile, n_ref, rows_ref, lambda l, r: chunk_copy(buf)(l, r).wait())

    y = _unpack_rows(loc_ref[buf]).astype(BF16)
    rt = rt_ref[0]
    slot = lax.broadcasted_iota(jnp.int32, (tm, n_loc), 1).astype(F32)
    mix = jnp.where(slot == rt[:, 0:1], rt[:, 2:3], jnp.where(slot == rt[:, 1:2], rt[:, 3:4], 0.0)).astype(BF16)
    o_ref[0] = x1_ref[0] + mod_ref[0, 5:6, :] * jnp.dot(mix, y, preferred_element_type=F32)


def _combine(n_chunks, chunk_rows, x1, mod, rt, ys):
    bsz, s, d = x1.shape
    tm = TOKEN_TILE
    grid_spec = pltpu.PrefetchScalarGridSpec(
        num_scalar_prefetch=2,
        grid=(bsz, s // tm),
        in_specs=[
            pl.BlockSpec((1, tm, d), lambda b, i, c, r: (b, i, 0)),
            pl.BlockSpec((1, N_MOD, d), lambda b, i, c, r: (b, 0, 0)),
            pl.BlockSpec((1, tm, ROUTE_COLS), lambda b, i, c, r: (b, i, 0)),
            pl.BlockSpec(memory_space=pl.ANY),
        ],
        out_specs=pl.BlockSpec((1, tm, d), lambda b, i, c, r: (b, i, 0)),
        scratch_shapes=[pltpu.VMEM((2, _local_rows(tm), d // 2), U32), pltpu.SemaphoreType.DMA((2,))],
    )
    return pl.pallas_call(
        _combine_kernel,
        grid_spec=grid_spec,
        out_shape=jax.ShapeDtypeStruct((bsz, s, d), F32),
        compiler_params=_params("arbitrary", "arbitrary"),
        name="combine",
    )(n_chunks, chunk_rows, x1, mod, rt, ys)


def _layer(x, mod, g_mix, w_in, g_q, g_k, rel_bias, w_br_moba, w_br_sb, w_out, g_ffn,
           w_rg, b_rg, w_re, b_re, w_gate, w_up, w_down):
    bsz, s, d = x.shape
    n = bsz * s
    n_blk = s // ATT_BLOCK

    proj, kaug, vaug, kmean = _inproj(x, mod, g_mix, w_in.astype(BF16), g_k)

    top = (n_blk + 1) * ATT_BLOCK - 1
    dist = np.maximum(top - np.arange((n_blk + 2) * ATT_BLOCK), 0)
    rel_rev = rel_bias[:, _rel_bucket_table(top)[dist]].reshape(N_HEADS // HEADS_PER_STEP, HEADS_PER_STEP, -1)

    o_a = _moba(proj, kaug, vaug, kmean, g_q, rel_rev)
    o_b = _stick_breaking(proj)

    pad = LANES - N_GROUPS - N_EXPERTS
    w_router = jnp.concatenate([w_rg, w_re, jnp.zeros((d, pad), F32)], axis=1)
    b_router = jnp.concatenate([b_rg, b_re, jnp.zeros((pad,), F32)]).reshape(1, LANES)
    x1, h2, rt, slot_t, tile_cnt, tile_before = _merge(x, o_a, o_b, proj, mod, w_br_moba.astype(BF16),
                                                       w_br_sb.astype(BF16), w_out.astype(BF16), g_ffn,
                                                       w_router, b_router)

    experts = slice(N_GROUPS, N_GROUPS + N_EXPERTS)
    cnt = tile_cnt[:, 0, experts].astype(jnp.int32)
    total = jnp.sum(cnt, axis=0)
    padded = (total + ROW_BLOCK - 1) // ROW_BLOCK * ROW_BLOCK
    pend = jnp.cumsum(padded)
    pstart = pend - padded
    max_rows = 2 * n + cnt.shape[0] * N_EXPERTS * (SEG_ALIGN - 1) + N_EXPERTS * (ROW_BLOCK - 1)
    n_blocks = -(-max_rows // ROW_BLOCK)
    blk_row0 = jnp.arange(n_blocks, dtype=jnp.int32) * ROW_BLOCK
    blk_e = jnp.minimum(jnp.sum((blk_row0[:, None] >= pend[None, :]).astype(jnp.int32), axis=1), N_EXPERTS - 1)
    n_used = (pend[-1] // ROW_BLOCK).astype(jnp.int32).reshape(1)
    seg_row0 = pstart[None, :] + tile_before[:, 0, experts].astype(jnp.int32)
    chunks = cnt // SEG_ALIGN
    chunk_end = jnp.cumsum(chunks, axis=1)
    chunk_start = (chunk_end - chunks)[:, None, :]
    c = jnp.arange(_local_rows(TOKEN_TILE) // SEG_ALIGN, dtype=jnp.int32)[None, :, None]
    inside = jnp.logical_and(c >= chunk_start, c < chunk_end[:, None, :])
    chunk_rows = jnp.sum(jnp.where(inside, seg_row0[:, None, :] + SEG_ALIGN * (c - chunk_start), 0), axis=2).reshape(-1)
    n_chunks = chunk_end[:, -1]

    xs = _dispatch(n_chunks, chunk_rows, slot_t, h2.reshape(n, d), n_blocks * ROW_BLOCK)
    ys = _experts(blk_e, n_used, xs, w_gate, w_up, w_down)
    return _combine(n_chunks, chunk_rows, x1, mod, rt, ys)


def kernel(x, c, w_ada, b_ada, g_mix, w_in, g_q, g_k, rel_bias, w_br_moba, w_br_sb, w_out, g_ffn,
           w_rg, b_rg, w_re, b_re, w_gate, w_up, w_down):
    bsz, s, d = x.shape
    assert s % TOKEN_TILE == 0 and s % MOBA_Q_TILE == 0 and d % (2 * LANES) == 0
    for l in range(w_ada.shape[0]):
        mod = _adaln(c, w_ada[l], b_ada[l]).reshape(bsz, N_MOD, d)
        x = _layer(x, mod, g_mix[l], w_in[l], g_q[l], g_k[l], rel_bias, w_br_moba[l], w_br_sb[l], w_out[l],
                   g_ffn[l], w_rg[l], b_rg[l], w_re[l], b_re[l], w_gate[l], w_up[l], w_down[l])
    return x
```

```python
import functools
import math

import numpy as np
import jax
import jax.numpy as jnp
from jax import lax
from jax.experimental import pallas as pl
from jax.experimental.pallas import tpu as pltpu

F32 = jnp.float32
BF16 = jnp.bfloat16
U32 = jnp.uint32
HIGHEST = lax.Precision.HIGHEST

LANES = 128
VMEM_LIMIT_BYTES = 56 * 1024 * 1024

HEAD_DIM = 64
N_HEADS = 8
HEADS_PER_STEP = LANES // HEAD_DIM
ATT_BLOCK = 256
MOBA_Q_TILE = 2 * ATT_BLOCK
SB_Q_BLOCKS = 2
MOBA_TOPK = 3
REL_BUCKETS = 32
REL_MAX_DIST = 1024
N_GROUPS = 4
EXPERTS_PER_GROUP = 8
N_EXPERTS = N_GROUPS * EXPERTS_PER_GROUP
N_MOD = 6
RMS_EPS = 1e-6
ROW_BLOCK = 512
TOKEN_TILE = 512
NEG_INF = float("-inf")
MASK_LOGIT = -1e30
SB_EXIT_LOG = -110.0

_NT = (((1,), (1,)), ((), ()))


def _params(*semantics):
    return pltpu.CompilerParams(dimension_semantics=semantics, vmem_limit_bytes=VMEM_LIMIT_BYTES)


def _adaln_kernel(c_ref, w_ref, b_ref, o_ref):
    c = c_ref[...]
    s = c * jax.nn.sigmoid(c)
    o_ref[...] = jnp.dot(s, w_ref[...], preferred_element_type=F32, precision=HIGHEST) + b_ref[...]


def _adaln(c, w, b):
    bsz, d = c.shape
    n = w.shape[1]
    tn = 1536
    return pl.pallas_call(
        _adaln_kernel,
        grid=(n // tn,),
        in_specs=[
            pl.BlockSpec((bsz, d), lambda j: (0, 0)),
            pl.BlockSpec((d, tn), lambda j: (0, j)),
            pl.BlockSpec((1, tn), lambda j: (0, j)),
        ],
        out_specs=pl.BlockSpec((bsz, tn), lambda j: (0, j)),
        out_shape=jax.ShapeDtypeStruct((bsz, n), F32),
        compiler_params=_params("arbitrary"),
        name="adaln",
    )(c, w, b.reshape(1, n))


def _rms_modulate(x, g, shift, scale):
    y = x * lax.rsqrt(jnp.mean(x * x, axis=-1, keepdims=True) + RMS_EPS) * g
    return y * (1.0 + scale) + shift


def _inproj_kernel(x_ref, mod_ref, g_ref, w_ref, gk_ref, o_ref, kaug_ref, vaug_ref, kmean_ref, *, col_chunk):
    i = pl.program_id(1)
    tm = x_ref.shape[1]
    h = _rms_modulate(x_ref[0], g_ref[...], mod_ref[0, 0:1, :], mod_ref[0, 1:2, :]).astype(BF16)

    w_moba = N_HEADS * HEAD_DIM
    blocks = tm // ATT_BLOCK
    first = _head_lane_mask()
    row_blk = blocks * i + lax.shift_right_logical(lax.broadcasted_iota(jnp.int32, (tm, LANES), 0),
                                                   ATT_BLOCK.bit_length() - 1)
    tag = jnp.where(_lane_index() - float(HEAD_DIM) == row_blk.astype(F32), 1.0, 0.0)
    mean_row = lax.broadcasted_iota(jnp.int32, kmean_ref.shape[2:], 0)

    @pl.when(i == 0)
    def _init():
        kmean_ref[...] = jnp.zeros_like(kmean_ref)

    for n in range(w_ref.shape[1] // col_chunk):
        cols = slice(n * col_chunk, (n + 1) * col_chunk)
        r = jnp.dot(h, w_ref[:, cols], preferred_element_type=F32)
        o_ref[0, :, cols] = r.astype(BF16)
        for c in range(col_chunk // LANES):
            col0 = n * col_chunk + c * LANES
            part = r[:, c * LANES:(c + 1) * LANES]
            if w_moba <= col0 < 2 * w_moba:
                kn = _head_rmsnorm(part, gk_ref[...])
                for hh, kh in enumerate((kn, pltpu.roll(kn, HEAD_DIM, 1))):
                    head = (col0 - w_moba) // HEAD_DIM + hh
                    kaug_ref[0, head] = jnp.where(first, kh, tag).astype(BF16)
                    means = kmean_ref[0, head]
                    for jb in range(blocks):
                        mean = jnp.mean(kh[jb * ATT_BLOCK:(jb + 1) * ATT_BLOCK], axis=0, keepdims=True)
                        means = jnp.where(mean_row == blocks * i + jb, jnp.where(first, mean, 0.0), means)
                    kmean_ref[0, head] = means
            elif 2 * w_moba <= col0 < 3 * w_moba:
                for hh, vh in enumerate((part, pltpu.roll(part, HEAD_DIM, 1))):
                    head = (col0 - 2 * w_moba) // HEAD_DIM + hh
                    vaug_ref[0, head] = jnp.where(first, vh, 1.0).astype(BF16)


def _inproj(x, mod, g, w_bf16, g_k):
    bsz, s, d = x.shape
    n = w_bf16.shape[1]
    tm = TOKEN_TILE
    n_blk = s // ATT_BLOCK
    per_head = lambda rows: pl.BlockSpec((1, N_HEADS, rows, LANES), lambda b, i: (b, 0, i, 0))
    return pl.pallas_call(
        functools.partial(_inproj_kernel, col_chunk=1024),
        grid=(bsz, s // tm),
        in_specs=[
            pl.BlockSpec((1, tm, d), lambda b, i: (b, i, 0)),
            pl.BlockSpec((1, N_MOD, d), lambda b, i: (b, 0, 0)),
            pl.BlockSpec((1, d), lambda b, i: (0, 0)),
            pl.BlockSpec((d, n), lambda b, i: (0, 0)),
            pl.BlockSpec((1, LANES), lambda b, i: (0, 0)),
        ],
        out_specs=[
            pl.BlockSpec((1, tm, n), lambda b, i: (b, i, 0)),
            per_head(tm),
            per_head(tm),
            pl.BlockSpec((1, N_HEADS, -(-n_blk // 8) * 8, LANES), lambda b, i: (b, 0, 0, 0)),
        ],
        out_shape=[
            jax.ShapeDtypeStruct((bsz, s, n), BF16),
            jax.ShapeDtypeStruct((bsz, N_HEADS, s, LANES), BF16),
            jax.ShapeDtypeStruct((bsz, N_HEADS, s, LANES), BF16),
            jax.ShapeDtypeStruct((bsz, N_HEADS, -(-n_blk // 8) * 8, LANES), F32),
        ],
        compiler_params=_params("arbitrary", "arbitrary"),
        name="inproj",
    )(x, mod, g.reshape(1, d), w_bf16, jnp.tile(g_k.reshape(1, HEAD_DIM), (1, HEADS_PER_STEP)))


def _indicator(cond, dtype):
    return jnp.where(cond, 1.0, 0.0).astype(dtype)


def _lane_index():
    return lax.broadcasted_iota(jnp.int32, (1, LANES), 1).astype(F32)


def _head_lane_mask():
    return lax.broadcasted_iota(jnp.int32, (1, LANES), 1) < HEAD_DIM


def _head_rmsnorm(t, g):
    first = _head_lane_mask()
    sq = t * t
    ss0 = jnp.sum(jnp.where(first, sq, 0.0), axis=-1, keepdims=True)
    ss1 = jnp.sum(jnp.where(first, 0.0, sq), axis=-1, keepdims=True)
    inv = jnp.where(first, lax.rsqrt(ss0 / HEAD_DIM + RMS_EPS), lax.rsqrt(ss1 / HEAD_DIM + RMS_EPS))
    return t * inv * g


def _split_heads(t):
    first = _head_lane_mask()
    zero = jnp.zeros_like(t)
    return jnp.where(first, t, zero), jnp.where(first, zero, t)


def _rel_bucket_table(max_dist):
    n = np.arange(max_dist + 1)
    max_exact = REL_BUCKETS // 2
    nf = np.maximum(n, 1).astype(np.float64)
    large = max_exact + (np.log(nf / max_exact) / math.log(REL_MAX_DIST / max_exact)
                         * (REL_BUCKETS - max_exact)).astype(np.int64)
    large = np.minimum(large, REL_BUCKETS - 1)
    return np.where(n < max_exact, n, large).astype(np.int32)


def _moba_kernel(q_ref, kaug_ref, vaug_ref, kmean_ref, gq_ref, rb_ref, o_ref,
                 toep_ref, s_ref, mvec_ref, acc_ref, *, n_blk):
    b = pl.program_id(1)
    qi = pl.program_id(2)
    blk = ATT_BLOCK
    qt = q_ref.shape[1]
    halves = qt // blk
    first = _head_lane_mask()

    @pl.when(jnp.logical_and(b == 0, qi == 0))
    def _build_bias_tiles():
        for h in range(HEADS_PER_STEP):
            for d in range(n_blk):
                r = rb_ref[0, h:h + 1, (n_blk - d) * blk:(n_blk - d + 2) * blk]
                rolled = pltpu.roll(jnp.broadcast_to(r, (blk, 2 * blk)), blk + 1, 1, stride=1, stride_axis=0)
                toep_ref[h, d] = rolled[:, :blk]

    qn = _head_rmsnorm(q_ref[0].astype(F32), gq_ref[...])
    gate_rows = kmean_ref.shape[2]
    blk_row = lax.broadcasted_iota(jnp.int32, (gate_rows, qt), 0).astype(F32)
    own = (halves * qi + lax.shift_right_logical(lax.broadcasted_iota(jnp.int32, (gate_rows, qt), 1),
                                                 blk.bit_length() - 1)).astype(F32)
    q_aug = []
    for h, qh in enumerate((qn, pltpu.roll(qn, HEAD_DIM, 1))):
        qh = jnp.where(first, qh, 0.0)
        km = kmean_ref[0, h]
        km_hi, q_hi = km.astype(BF16), qh.astype(BF16)
        km_lo, q_lo = (km - km_hi.astype(F32)).astype(BF16), (qh - q_hi.astype(F32)).astype(BF16)
        g = (lax.dot_general(km_hi, q_hi, _NT, preferred_element_type=F32)
             + lax.dot_general(km_lo, q_hi, _NT, preferred_element_type=F32)
             + lax.dot_general(km_hi, q_lo, _NT, preferred_element_type=F32))
        g = jnp.where(blk_row < own, g, NEG_INF)
        keep = jnp.where(blk_row == own, 1.0, 0.0)
        for _ in range(MOBA_TOPK):
            top = jnp.max(g, axis=0, keepdims=True)
            is_top = jnp.logical_and(g == top, top > NEG_INF)
            idx = jnp.min(jnp.where(is_top, blk_row, float(gate_rows)), axis=0, keepdims=True)
            pick = blk_row == idx
            keep = jnp.where(pick, 1.0, keep)
            g = jnp.where(pick, NEG_INF, g)
        mask_t = jnp.where(keep > 0.0, 0.0, MASK_LOGIT)
        mask_logit = jnp.concatenate([jnp.zeros((HEAD_DIM, qt), F32), mask_t,
                                      jnp.zeros((LANES - HEAD_DIM - gate_rows, qt), F32)], axis=0).T
        q_aug.append(jnp.where(first, qh * (HEAD_DIM ** -0.5), mask_logit).astype(BF16))

    def lane_max(s):
        return jnp.maximum(s[:, :LANES], s[:, LANES:])

    def scores(h, t, dist):
        k_t = kaug_ref[0, h, pl.ds(pl.multiple_of(t * qt, qt), qt), :]
        s = lax.dot_general(q_aug[h], k_t, _NT, preferred_element_type=F32)
        tiles = []
        for kb in range(halves):
            rows = []
            for qb in range(halves):
                part = s[qb * blk:(qb + 1) * blk, kb * blk:(kb + 1) * blk]
                if isinstance(dist, int) and halves * dist + qb - kb < 0:
                    rows.append(part)
                else:
                    rows.append(part + toep_ref[h, halves * dist + qb - kb])
            tiles.append(jnp.concatenate(rows, axis=0))
        return tiles

    row = lax.broadcasted_iota(jnp.int32, (qt, blk), 0)
    col = lax.broadcasted_iota(jnp.int32, (qt, blk), 1)
    for h in range(HEADS_PER_STEP):
        top = None
        for kb, s in enumerate(scores(h, qi, 0)):
            s = jnp.where(col + kb * blk <= row, s, NEG_INF)
            s_ref[h, halves * qi + kb] = s
            top = lane_max(s) if top is None else jnp.maximum(top, lane_max(s))
        mvec_ref[h] = top

    def in_pairs(count, visit):
        def quad(p, carry):
            visit(4 * p, 4)
            return carry

        lax.fori_loop(0, lax.shift_right_logical(count, 2), quad, 0)

        @pl.when((count & 2) == 2)
        def _pair():
            visit(count & ~3, 2)

        @pl.when((count & 1) == 1)
        def _last():
            visit(count - 1, 1)

    def pass1(t0, n_tiles):
        for h in range(HEADS_PER_STEP):
            top = mvec_ref[h]
            for t in [t0 + u for u in range(n_tiles)]:
                for kb, s in enumerate(scores(h, t, qi - t)):
                    s_ref[h, halves * t + kb] = s
                    top = jnp.maximum(top, lane_max(s))
            mvec_ref[h] = top

    in_pairs(qi, pass1)

    for h in range(HEADS_PER_STEP):
        m = jnp.max(mvec_ref[h], axis=-1, keepdims=True)
        mvec_ref[h] = jnp.broadcast_to(m, (qt, LANES))
        acc_ref[h] = jnp.zeros((qt, LANES), F32)

    def pass2(t0, n_tiles):
        keys = pl.ds(pl.multiple_of(t0 * qt, qt), n_tiles * qt)
        for h in range(HEADS_PER_STEP):
            m = mvec_ref[h]
            parts = []
            for kb in range(n_tiles * halves):
                s = s_ref[h, halves * t0 + kb]
                parts += [jnp.exp(s[:, :LANES] - m), jnp.exp(s[:, LANES:] - m)]
            p = jnp.concatenate(parts, axis=1).astype(BF16)
            acc_ref[h] += jnp.dot(p, vaug_ref[0, h, keys, :], preferred_element_type=F32)

    in_pairs(qi + 1, pass2)

    out = [acc_ref[h] / acc_ref[h][:, HEAD_DIM:HEAD_DIM + 1] for h in range(HEADS_PER_STEP)]
    o_ref[0] = jnp.where(first, out[0], pltpu.roll(out[1], HEAD_DIM, 1)).astype(o_ref.dtype)


def _moba(proj, kaug, vaug, kmean, g_q, rel_rev):
    bsz, s, _ = proj.shape
    blk = ATT_BLOCK
    n_blk = s // blk
    n_hp = N_HEADS // HEADS_PER_STEP
    w = N_HEADS * HEAD_DIM
    qt = MOBA_Q_TILE
    per_pair = lambda arr: pl.BlockSpec((1, HEADS_PER_STEP) + arr.shape[2:], lambda hp, b, i: (b, hp, 0, 0))
    return pl.pallas_call(
        functools.partial(_moba_kernel, n_blk=n_blk),
        grid=(n_hp, bsz, s // qt),
        in_specs=[
            pl.BlockSpec((1, qt, LANES), lambda hp, b, i: (b, i, hp)),
            per_pair(kaug), per_pair(vaug), per_pair(kmean),
            pl.BlockSpec((1, LANES), lambda hp, b, i: (0, 0)),
            pl.BlockSpec((1, HEADS_PER_STEP, rel_rev.shape[-1]), lambda hp, b, i: (hp, 0, 0)),
        ],
        out_specs=pl.BlockSpec((1, qt, LANES), lambda hp, b, i: (b, i, hp)),
        out_shape=jax.ShapeDtypeStruct((bsz, s, w), BF16),
        scratch_shapes=[
            pltpu.VMEM((HEADS_PER_STEP, n_blk, blk, blk), F32),
            pltpu.VMEM((HEADS_PER_STEP, n_blk, qt, blk), F32),
            pltpu.VMEM((HEADS_PER_STEP, qt, LANES), F32),
            pltpu.VMEM((HEADS_PER_STEP, qt, LANES), F32),
        ],
        compiler_params=_params("arbitrary", "arbitrary", "arbitrary"),
        name="moba",
    )(proj, kaug, vaug, kmean, jnp.tile(g_q.reshape(1, HEAD_DIM), (1, HEADS_PER_STEP)), rel_rev)


def _sb_kernel(q_ref, k_ref, v_ref, o_ref, acc_ref, carry_ref):
    ti = pl.program_id(2)
    blk = ATT_BLOCK
    n_sub = q_ref.shape[1] // blk
    first = n_sub * ti
    row = lax.broadcasted_iota(jnp.int32, (blk, blk), 0)
    col = lax.broadcasted_iota(jnp.int32, (blk, blk), 1)
    past = col < row
    later = _indicator(row > col, BF16)

    q_heads = []
    for sub in range(n_sub):
        heads = _split_heads(q_ref[0, sub * blk:(sub + 1) * blk, :].astype(F32) * (HEAD_DIM ** -0.5))
        q_heads.append([t.astype(BF16) for t in heads])

    def walk(jobs, fresh):
        for sub, blocks in jobs:
            for h in range(HEADS_PER_STEP):
                carry = None if fresh else carry_ref[sub, h]
                total = None if fresh else acc_ref[sub, h]
                for j, diagonal in blocks:
                    start = j * blk if isinstance(j, int) else pl.multiple_of(j * blk, blk)
                    z = lax.dot_general(q_heads[sub][h], k_ref[0, pl.ds(start, blk), :], _NT,
                                        preferred_element_type=F32)
                    log_1m = -(jnp.maximum(z, 0.0) + jnp.log(1.0 + jnp.exp(-jnp.abs(z))))
                    masked = jnp.where(past, log_1m, 0.0) if diagonal else log_1m
                    hi = masked.astype(BF16)
                    lo = (masked - hi.astype(F32)).astype(BF16)
                    after = (jnp.dot(hi, later, preferred_element_type=F32)
                             + jnp.dot(lo, later, preferred_element_type=F32))
                    row_sum = after[:, 0:1] + masked[:, 0:1]
                    if carry is not None:
                        after = after + carry
                    a = jnp.exp(z + log_1m + after)
                    if diagonal:
                        a = jnp.where(past, a, 0.0)
                    pv = jnp.dot(a.astype(BF16), v_ref[0, pl.ds(start, blk), :], preferred_element_type=F32)
                    total = pv if total is None else total + pv
                    carry = row_sum if carry is None else carry + row_sum
                carry_ref[sub, h] = carry
                acc_ref[sub, h] = total

    @pl.when(ti == 0)
    def _first_step():
        walk([(sub, [(sub, True)] + ([(sub - 1, False)] if sub else [])) for sub in range(n_sub)], True)

    @pl.when(ti > 0)
    def _own_and_previous():
        walk([(sub, [(first + sub, True), (first + sub - 1, False)]) for sub in range(n_sub)], True)

    def slowest_decay(subs):
        worst = functools.reduce(jnp.maximum, [carry_ref[sub, h] for sub in subs for h in range(HEADS_PER_STEP)])
        return jnp.max(worst)

    @pl.when(slowest_decay(range(n_sub)) > SB_EXIT_LOG)
    def _walk_further_back():
        for sub in range(n_sub):
            def cond(state):
                j, worst = state
                return jnp.logical_and(j >= 0, worst > SB_EXIT_LOG)

            def body(state, sub=sub):
                j, _ = state
                walk([(sub, [(j, False)])], False)
                return j - 1, slowest_decay([sub])

            lax.while_loop(cond, body, (first + sub - 2, slowest_decay([sub])))

    for sub in range(n_sub):
        o_ref[0, sub * blk:(sub + 1) * blk, :] = jnp.where(
            _head_lane_mask(), acc_ref[sub, 0], acc_ref[sub, 1]).astype(o_ref.dtype)


def _stick_breaking(proj):
    bsz, s, _ = proj.shape
    blk = ATT_BLOCK
    qt = SB_Q_BLOCKS * blk
    n_hp = N_HEADS // HEADS_PER_STEP
    w = N_HEADS * HEAD_DIM
    base = 3 * n_hp
    return pl.pallas_call(
        _sb_kernel,
        grid=(n_hp, bsz, s // qt),
        in_specs=[
            pl.BlockSpec((1, qt, LANES), lambda hp, b, i: (b, i, base + hp)),
            pl.BlockSpec((1, s, LANES), lambda hp, b, i: (b, 0, base + n_hp + hp)),
            pl.BlockSpec((1, s, LANES), lambda hp, b, i: (b, 0, base + 2 * n_hp + hp)),
        ],
        out_specs=pl.BlockSpec((1, qt, LANES), lambda hp, b, i: (b, i, hp)),
        out_shape=jax.ShapeDtypeStruct((bsz, s, w), BF16),
        scratch_shapes=[
            pltpu.VMEM((SB_Q_BLOCKS, HEADS_PER_STEP, blk, LANES), F32),
            pltpu.VMEM((SB_Q_BLOCKS, HEADS_PER_STEP, blk, 1), F32),
        ],
        compiler_params=_params("arbitrary", "arbitrary", "arbitrary"),
        name="stickbrk",
    )(proj, proj, proj)


def _pack_rows(t):
    n = t.shape[1] // 2
    bits = pltpu.bitcast(t.astype(BF16).astype(F32), U32)
    return bits[:, :n] | (bits[:, n:] >> 16)


def _unpack_rows(p):
    hi = pltpu.bitcast(p & jnp.uint32(0xFFFF0000), F32)
    lo = pltpu.bitcast(p << 16, F32)
    return jnp.concatenate([hi, lo], axis=1)


def _merge_kernel(x_ref, oa_ref, ob_ref, ga_ref, gb_ref, mod_ref, wbm_ref, wbs_ref, wo_ref, gf_ref,
                  wrh_ref, wrl_ref, br_ref, x1_ref, h2_ref, rt_ref, slot_t_ref, cnt_ref, before_ref, carry_ref):
    ma = jnp.dot(oa_ref[0], wbm_ref[...], preferred_element_type=F32)
    mb = jnp.dot(ob_ref[0], wbs_ref[...], preferred_element_type=F32)
    merged = (jax.nn.sigmoid(ga_ref[0].astype(F32)) * ma + jax.nn.sigmoid(gb_ref[0].astype(F32)) * mb)
    t = jnp.dot(merged.astype(BF16), wo_ref[...], preferred_element_type=F32)
    x1 = x_ref[0] + mod_ref[0, 2:3, :] * t
    x1_ref[0] = x1
    h2 = _rms_modulate(x1, gf_ref[...], mod_ref[0, 3:4, :], mod_ref[0, 4:5, :])
    hi = h2.astype(BF16)
    h2_ref[0] = hi
    lo = (h2 - hi.astype(F32)).astype(BF16)
    logits = (jnp.dot(hi, wrh_ref[...], preferred_element_type=F32)
              + jnp.dot(lo, wrh_ref[...], preferred_element_type=F32)
              + jnp.dot(hi, wrl_ref[...], preferred_element_type=F32) + br_ref[...])
    first_tile = jnp.logical_and(pl.program_id(0) == 0, pl.program_id(1) == 0)
    _route_tile(logits, first_tile, rt_ref, slot_t_ref, cnt_ref, before_ref, carry_ref)


def _merge(x, o_a, o_b, proj, mod, wbm, wbs, wo, g_ffn, w_router, b_router):
    bsz, s, d = x.shape
    w = o_a.shape[-1]
    tm = TOKEN_TILE
    gate_blk = (proj.shape[-1] - 2 * d) // d
    const = lambda shape: pl.BlockSpec(shape, lambda b, i: (0,) * len(shape))
    tok = lambda width: pl.BlockSpec((1, tm, width), lambda b, i: (b, i, 0))
    wr_hi = w_router.astype(BF16)
    wr_lo = (w_router - wr_hi.astype(F32)).astype(BF16)
    per_b = s // tm
    tiles = bsz * per_b
    per_tile = lambda rows, cols: pl.BlockSpec((1, rows, cols), lambda b, i: (b * per_b + i, 0, 0))
    return pl.pallas_call(
        _merge_kernel,
        grid=(bsz, s // tm),
        in_specs=[
            tok(d), tok(w), tok(w),
            pl.BlockSpec((1, tm, d), lambda b, i: (b, i, gate_blk)),
            pl.BlockSpec((1, tm, d), lambda b, i: (b, i, gate_blk + 1)),
            pl.BlockSpec((1, N_MOD, d), lambda b, i: (b, 0, 0)),
            const((w, d)), const((w, d)), const((d, d)), const((1, d)),
            const((d, LANES)), const((d, LANES)), const((1, LANES)),
        ],
        out_specs=[tok(d), tok(d), tok(ROUTE_COLS), per_tile(ROUTE_COLS, tm), per_tile(1, LANES), per_tile(1, LANES)],
        out_shape=[
            jax.ShapeDtypeStruct((bsz, s, d), F32),
            jax.ShapeDtypeStruct((bsz, s, d), BF16),
            jax.ShapeDtypeStruct((bsz, s, ROUTE_COLS), F32),
            jax.ShapeDtypeStruct((tiles, ROUTE_COLS, tm), F32),
            jax.ShapeDtypeStruct((tiles, 1, LANES), F32),
            jax.ShapeDtypeStruct((tiles, 1, LANES), F32),
        ],
        scratch_shapes=[pltpu.VMEM((1, LANES), F32)],
        compiler_params=_params("arbitrary", "arbitrary"),
        name="merge",
    )(x, o_a, o_b, proj, proj, mod, wbm, wbs, wo, g_ffn.reshape(1, d), wr_hi, wr_lo, b_router)


ROUTE_COLS = 8
SEG_ALIGN = 8


def _local_rows(tm):
    return 2 * tm + N_EXPERTS * SEG_ALIGN


def _route_tile(lg, first_tile, rt_ref, slot_t_ref, cnt_ref, before_ref, carry_ref):
    tm = lg.shape[0]
    lane = _lane_index()

    @pl.when(first_tile)
    def _init():
        carry_ref[...] = jnp.zeros_like(carry_ref)

    def first_max(vals):
        top = jnp.max(vals, axis=-1, keepdims=True)
        idx = jnp.min(jnp.where(vals == top, lane, float(LANES)), axis=-1, keepdims=True)
        return top, idx

    g_logit = jnp.where(lane < N_GROUPS, lg, NEG_INF)
    g_top, g_idx = first_max(g_logit)
    g_w = 1.0 / jnp.sum(jnp.exp(g_logit - g_top), axis=-1, keepdims=True)
    lo = N_GROUPS + g_idx * EXPERTS_PER_GROUP
    e_logit = jnp.where(jnp.logical_and(lane >= lo, lane < lo + EXPERTS_PER_GROUP), lg, NEG_INF)
    e_top0, lane0 = first_max(e_logit)
    e_top1, lane1 = first_max(jnp.where(lane == lane0, NEG_INF, e_logit))
    z = jnp.exp(e_top1 - e_top0)
    w0 = g_w / (1.0 + z)
    w1 = g_w * z / (1.0 + z)

    hot = jnp.logical_or(lane == lane0, lane == lane1)
    row = lax.broadcasted_iota(jnp.int32, (tm, tm), 0)
    col = lax.broadcasted_iota(jnp.int32, (tm, tm), 1)
    earlier = jnp.dot(_indicator(col < row, BF16), _indicator(hot, BF16), preferred_element_type=F32)
    cnt = jnp.sum(_indicator(hot, F32), axis=0, keepdims=True)
    chunks = jnp.ceil(cnt * (1.0 / SEG_ALIGN))
    src = lax.broadcasted_iota(jnp.int32, (LANES, LANES), 0)
    dst = lax.broadcasted_iota(jnp.int32, (LANES, LANES), 1)
    seg_start = SEG_ALIGN * jnp.dot(jnp.broadcast_to(chunks, (8, LANES)).astype(BF16), _indicator(src < dst, BF16),
                                    preferred_element_type=F32)[0:1]
    local = seg_start + earlier
    slot0 = jnp.sum(jnp.where(lane == lane0, local, 0.0), axis=-1, keepdims=True)
    slot1 = jnp.sum(jnp.where(lane == lane1, local, 0.0), axis=-1, keepdims=True)
    cnt_ref[0] = SEG_ALIGN * chunks
    before_ref[0] = carry_ref[...]
    carry_ref[...] += SEG_ALIGN * chunks

    packed = jnp.zeros((tm, LANES), F32)
    for c, v in enumerate([slot0, slot1, w0, w1]):
        packed = jnp.where(lane == float(c), v, packed)
    rt_ref[0] = packed[:, :ROUTE_COLS]
    slot_t_ref[0] = packed.T[:ROUTE_COLS, :]


def _for_each_chunk(tile, n_ref, rows_ref, visit):
    max_chunks = rows_ref.shape[0] // n_ref.shape[0]

    def per_chunk(c, carry):
        visit(pl.multiple_of(c * SEG_ALIGN, SEG_ALIGN), pl.multiple_of(rows_ref[tile * max_chunks + c], SEG_ALIGN))
        return carry

    lax.fori_loop(0, n_ref[tile], per_chunk, 0)


def _dispatch_kernel(n_ref, rows_ref, tail_row_ref, tail_n_ref, n_used_ref, slot_t_ref, h_ref, xs_ref,
                     loc_ref, zero_ref, sems, zero_sem):
    tile = pl.program_id(0)
    n_loc, tm = loc_ref.shape[1], h_ref.shape[0]
    buf = tile & 1

    slot = lax.broadcasted_iota(jnp.int32, (n_loc, tm), 0).astype(F32)
    place = jnp.logical_or(slot == slot_t_ref[0, 0:1, :], slot == slot_t_ref[0, 1:2, :])
    loc_ref[buf] = _pack_rows(jnp.dot(_indicator(place, BF16), h_ref[...], preferred_element_type=F32))

    def chunk_copy(b):
        return lambda local, row: pltpu.make_async_copy(
            loc_ref.at[b, pl.ds(local, SEG_ALIGN), :], xs_ref.at[pl.ds(row, SEG_ALIGN), :], sems.at[b])

    _for_each_chunk(tile, n_ref, rows_ref, lambda l, r: chunk_copy(buf)(l, r).start())

    @pl.when(tile > 0)
    def _previous_done():
        _for_each_chunk(tile - 1, n_ref, rows_ref, lambda l, r: chunk_copy(1 - buf)(l, r).wait())

    @pl.when(tile == pl.num_programs(0) - 1)
    def _last_done():
        _for_each_chunk(tile, n_ref, rows_ref, lambda l, r: chunk_copy(buf)(l, r).wait())
        zero_ref[...] = jnp.zeros_like(zero_ref)
        block_rows = zero_ref.shape[0]

        def fill(action):
            def per_expert(e, carry):
                def per_chunk(c, inner):
                    row = pl.multiple_of(tail_row_ref[e] + c * SEG_ALIGN, SEG_ALIGN)
                    action(pltpu.make_async_copy(zero_ref.at[pl.ds(0, SEG_ALIGN), :],
                                                 xs_ref.at[pl.ds(row, SEG_ALIGN), :], zero_sem))
                    return inner

                lax.fori_loop(0, tail_n_ref[e], per_chunk, 0)
                return carry

            lax.fori_loop(0, N_EXPERTS, per_expert, 0)

            def per_block(blk, carry):
                row = pl.multiple_of(blk * block_rows, block_rows)
                action(pltpu.make_async_copy(zero_ref, xs_ref.at[pl.ds(row, block_rows), :], zero_sem))
                return carry

            lax.fori_loop(n_used_ref[0], xs_ref.shape[0] // block_rows, per_block, 0)

        fill(lambda copy: copy.start())
        fill(lambda copy: copy.wait())


def _dispatch(n_chunks, chunk_rows, tail_row, tail_chunks, n_used, slot_t, h2, n_rows):
    n, d = h2.shape
    tm = TOKEN_TILE
    grid_spec = pltpu.PrefetchScalarGridSpec(
        num_scalar_prefetch=5,
        grid=(n // tm,),
        in_specs=[
            pl.BlockSpec((1, ROUTE_COLS, tm), lambda i, *_: (i, 0, 0)),
            pl.BlockSpec((tm, d), lambda i, *_: (i, 0)),
        ],
        out_specs=pl.BlockSpec(memory_space=pl.ANY),
        scratch_shapes=[pltpu.VMEM((2, _local_rows(tm), d // 2), U32), pltpu.VMEM((ROW_BLOCK, d // 2), U32),
                        pltpu.SemaphoreType.DMA((2,)), pltpu.SemaphoreType.DMA(())],
    )
    return pl.pallas_call(
        _dispatch_kernel,
        grid_spec=grid_spec,
        out_shape=jax.ShapeDtypeStruct((n_rows, d // 2), U32),
        compiler_params=_params("arbitrary"),
        name="dispatch",
    )(n_chunks, chunk_rows, tail_row, tail_chunks, n_used, slot_t, h2)


def _expert_kernel(blk_e_ref, n_used_ref, xs_ref, wg_ref, wu_ref, wd_ref, ys_ref, wg_bf, wu_bf, wd_bf):
    i = pl.program_id(0)
    used = i < n_used_ref[0]

    @pl.when(jnp.logical_and(used, jnp.logical_or(i == 0, blk_e_ref[i] != blk_e_ref[jnp.maximum(i - 1, 0)])))
    def _new_expert():
        wg_bf[...] = wg_ref[0].astype(BF16)
        wu_bf[...] = wu_ref[0].astype(BF16)
        wd_bf[...] = wd_ref[0].astype(BF16)

    @pl.when(used)
    def _compute():
        xb = _unpack_rows(xs_ref[...]).astype(BF16)
        g = jnp.dot(xb, wg_bf[...], preferred_element_type=F32)
        u = jnp.dot(xb, wu_bf[...], preferred_element_type=F32)
        hid = (g * jax.nn.sigmoid(g) * u).astype(BF16)
        ys_ref[...] = _pack_rows(jnp.dot(hid, wd_bf[...], preferred_element_type=F32))

    @pl.when(i >= n_used_ref[0])
    def _unused():
        ys_ref[...] = jnp.zeros_like(ys_ref)


def _experts(blk_e, n_used, xs, wg, wu, wd):
    n_rows, half = xs.shape
    _, d, de = wg.shape
    grid_spec = pltpu.PrefetchScalarGridSpec(
        num_scalar_prefetch=2,
        grid=(n_rows // ROW_BLOCK,),
        in_specs=[
            pl.BlockSpec((ROW_BLOCK, half), lambda i, e, u: (jnp.minimum(i, u[0] - 1), 0)),
            pl.BlockSpec((1, d, de), lambda i, e, u: (e[i], 0, 0)),
            pl.BlockSpec((1, d, de), lambda i, e, u: (e[i], 0, 0)),
            pl.BlockSpec((1, de, d), lambda i, e, u: (e[i], 0, 0)),
        ],
        out_specs=pl.BlockSpec((ROW_BLOCK, half), lambda i, e, u: (i, 0)),
        scratch_shapes=[pltpu.VMEM((d, de), BF16), pltpu.VMEM((d, de), BF16), pltpu.VMEM((de, d), BF16)],
    )
    return pl.pallas_call(
        _expert_kernel,
        grid_spec=grid_spec,
        out_shape=jax.ShapeDtypeStruct((n_rows, half), U32),
        compiler_params=_params("arbitrary"),
        name="experts",
    )(blk_e, n_used, xs, wg, wu, wd)


def _combine_kernel(n_ref, rows_ref, x1_ref, mod_ref, rt_ref, ys_ref, o_ref, loc_ref, sems):
    tm = x1_ref.shape[1]
    n_loc = loc_ref.shape[1]
    tile = pl.program_id(0) * pl.num_programs(1) + pl.program_id(1)
    n_tiles = pl.num_programs(0) * pl.num_programs(1)
    buf = tile & 1

    def chunk_copy(b):
        return lambda local, row: pltpu.make_async_copy(
            ys_ref.at[pl.ds(row, SEG_ALIGN), :], loc_ref.at[b, pl.ds(local, SEG_ALIGN), :], sems.at[b])

    @pl.when(tile == 0)
    def _init():
        loc_ref[...] = jnp.zeros_like(loc_ref)
        _for_each_chunk(tile, n_ref, rows_ref, lambda l, r: chunk_copy(buf)(l, r).start())

    @pl.when(tile + 1 < n_tiles)
    def _fetch_next():
        _for_each_chunk(tile + 1, n_ref, rows_ref, lambda l, r: chunk_copy(1 - buf)(l, r).start())

    _for_each_chunk(tile, n_ref, rows_ref, lambda l, r: chunk_copy(buf)(l, r).wait())

    y = _unpack_rows(loc_ref[buf]).astype(BF16)
    rt = rt_ref[0]
    slot = lax.broadcasted_iota(jnp.int32, (tm, n_loc), 1).astype(F32)
    mix = jnp.where(slot == rt[:, 0:1], rt[:, 2:3], jnp.where(slot == rt[:, 1:2], rt[:, 3:4], 0.0)).astype(BF16)
    o_ref[0] = x1_ref[0] + mod_ref[0, 5:6, :] * jnp.dot(mix, y, preferred_element_type=F32)


def _combine(n_chunks, chunk_rows, x1, mod, rt, ys):
    bsz, s, d = x1.shape
    tm = TOKEN_TILE
    grid_spec = pltpu.PrefetchScalarGridSpec(
        num_scalar_prefetch=2,
        grid=(bsz, s // tm),
        in_specs=[
            pl.BlockSpec((1, tm, d), lambda b, i, c, r: (b, i, 0)),
            pl.BlockSpec((1, N_MOD, d), lambda b, i, c, r: (b, 0, 0)),
            pl.BlockSpec((1, tm, ROUTE_COLS), lambda b, i, c, r: (b, i, 0)),
            pl.BlockSpec(memory_space=pl.ANY),
        ],
        out_specs=pl.BlockSpec((1, tm, d), lambda b, i, c, r: (b, i, 0)),
        scratch_shapes=[pltpu.VMEM((2, _local_rows(tm), d // 2), U32), pltpu.SemaphoreType.DMA((2,))],
    )
    return pl.pallas_call(
        _combine_kernel,
        grid_spec=grid_spec,
        out_shape=jax.ShapeDtypeStruct((bsz, s, d), F32),
        compiler_params=_params("arbitrary", "arbitrary"),
        name="combine",
    )(n_chunks, chunk_rows, x1, mod, rt, ys)


def _layer(x, mod, g_mix, w_in, g_q, g_k, rel_bias, w_br_moba, w_br_sb, w_out, g_ffn,
           w_rg, b_rg, w_re, b_re, w_gate, w_up, w_down):
    bsz, s, d = x.shape
    n = bsz * s
    n_blk = s // ATT_BLOCK

    proj, kaug, vaug, kmean = _inproj(x, mod, g_mix, w_in.astype(BF16), g_k)

    top = (n_blk + 1) * ATT_BLOCK - 1
    dist = np.maximum(top - np.arange((n_blk + 2) * ATT_BLOCK), 0)
    rel_rev = rel_bias[:, _rel_bucket_table(top)[dist]].reshape(N_HEADS // HEADS_PER_STEP, HEADS_PER_STEP, -1)

    o_a = _moba(proj, kaug, vaug, kmean, g_q, rel_rev)
    o_b = _stick_breaking(proj)

    pad = LANES - N_GROUPS - N_EXPERTS
    w_router = jnp.concatenate([w_rg, w_re, jnp.zeros((d, pad), F32)], axis=1)
    b_router = jnp.concatenate([b_rg, b_re, jnp.zeros((pad,), F32)]).reshape(1, LANES)
    x1, h2, rt, slot_t, tile_cnt, tile_before = _merge(x, o_a, o_b, proj, mod, w_br_moba.astype(BF16),
                                                       w_br_sb.astype(BF16), w_out.astype(BF16), g_ffn,
                                                       w_router, b_router)

    experts = slice(N_GROUPS, N_GROUPS + N_EXPERTS)
    cnt = tile_cnt[:, 0, experts].astype(jnp.int32)
    total = jnp.sum(cnt, axis=0)
    padded = (total + ROW_BLOCK - 1) // ROW_BLOCK * ROW_BLOCK
    pend = jnp.cumsum(padded)
    pstart = pend - padded
    max_rows = 2 * n + cnt.shape[0] * N_EXPERTS * (SEG_ALIGN - 1) + N_EXPERTS * (ROW_BLOCK - 1)
    n_blocks = -(-max_rows // ROW_BLOCK)
    blk_row0 = jnp.arange(n_blocks, dtype=jnp.int32) * ROW_BLOCK
    blk_e = jnp.minimum(jnp.sum((blk_row0[:, None] >= pend[None, :]).astype(jnp.int32), axis=1), N_EXPERTS - 1)
    n_used = (pend[-1] // ROW_BLOCK).astype(jnp.int32).reshape(1)
    seg_row0 = pstart[None, :] + tile_before[:, 0, experts].astype(jnp.int32)
    chunks = cnt // SEG_ALIGN
    chunk_end = jnp.cumsum(chunks, axis=1)
    chunk_start = (chunk_end - chunks)[:, None, :]
    c = jnp.arange(_local_rows(TOKEN_TILE) // SEG_ALIGN, dtype=jnp.int32)[None, :, None]
    inside = jnp.logical_and(c >= chunk_start, c < chunk_end[:, None, :])
    chunk_rows = jnp.sum(jnp.where(inside, seg_row0[:, None, :] + SEG_ALIGN * (c - chunk_start), 0), axis=2).reshape(-1)
    n_chunks = chunk_end[:, -1]

    xs = _dispatch(n_chunks, chunk_rows, pstart + total, (padded - total) // SEG_ALIGN, n_used, slot_t,
                   h2.reshape(n, d), n_blocks * ROW_BLOCK)
    ys = _experts(blk_e, n_used, xs, w_gate, w_up, w_down)
    return _combine(n_chunks, chunk_rows, x1, mod, rt, ys)


def kernel(x, c, w_ada, b_ada, g_mix, w_in, g_q, g_k, rel_bias, w_br_moba, w_br_sb, w_out, g_ffn,
           w_rg, b_rg, w_re, b_re, w_gate, w_up, w_down):
    bsz, s, d = x.shape
    assert s % TOKEN_TILE == 0 and s % MOBA_Q_TILE == 0 and d % (2 * LANES) == 0
    for l in range(w_ada.shape[0]):
        mod = _adaln(c, w_ada[l], b_ada[l]).reshape(bsz, N_MOD, d)
        x = _layer(x, mod, g_mix[l], w_in[l], g_q[l], g_k[l], rel_bias, w_br_moba[l], w_br_sb[l], w_out[l],
                   g_ffn[l], w_rg[l], b_rg[l], w_re[l], b_re[l], w_gate[l], w_up[l], w_down[l])
    return x
```

```python
import functools
import math

import numpy as np
import jax
import jax.numpy as jnp
from jax import lax
from jax.experimental import pallas as pl
from jax.experimental.pallas import tpu as pltpu

F32 = jnp.float32
BF16 = jnp.bfloat16
U32 = jnp.uint32
HIGHEST = lax.Precision.HIGHEST

LANES = 128
VMEM_LIMIT_BYTES = 56 * 1024 * 1024

HEAD_DIM = 64
N_HEADS = 8
HEADS_PER_STEP = LANES // HEAD_DIM
ATT_BLOCK = 256
MOBA_Q_TILE = 2 * ATT_BLOCK
SB_Q_BLOCKS = 2
MOBA_TOPK = 3
REL_BUCKETS = 32
REL_MAX_DIST = 1024
N_GROUPS = 4
EXPERTS_PER_GROUP = 8
N_EXPERTS = N_GROUPS * EXPERTS_PER_GROUP
N_MOD = 6
RMS_EPS = 1e-6
ROW_BLOCK = 512
TOKEN_TILE = 512
NEG_INF = float("-inf")
MASK_LOGIT = -1e30
SB_EXIT_SUM = 110.0

_NT = (((1,), (1,)), ((), ()))


def _params(*semantics):
    return pltpu.CompilerParams(dimension_semantics=semantics, vmem_limit_bytes=VMEM_LIMIT_BYTES)


def _adaln_kernel(c_ref, w_ref, b_ref, o_ref):
    c = c_ref[...]
    s = c * jax.nn.sigmoid(c)
    o_ref[...] = jnp.dot(s, w_ref[...], preferred_element_type=F32, precision=HIGHEST) + b_ref[...]


def _adaln(c, w, b):
    bsz, d = c.shape
    n = w.shape[1]
    tn = 1536
    return pl.pallas_call(
        _adaln_kernel,
        grid=(n // tn,),
        in_specs=[
            pl.BlockSpec((bsz, d), lambda j: (0, 0)),
            pl.BlockSpec((d, tn), lambda j: (0, j)),
            pl.BlockSpec((1, tn), lambda j: (0, j)),
        ],
        out_specs=pl.BlockSpec((bsz, tn), lambda j: (0, j)),
        out_shape=jax.ShapeDtypeStruct((bsz, n), F32),
        compiler_params=_params("arbitrary"),
        name="adaln",
    )(c, w, b.reshape(1, n))


def _rms_modulate(x, g, shift, scale):
    y = x * lax.rsqrt(jnp.mean(x * x, axis=-1, keepdims=True) + RMS_EPS) * g
    return y * (1.0 + scale) + shift


def _inproj_kernel(x_ref, mod_ref, g_ref, w_ref, gk_ref, o_ref, kaug_ref, vaug_ref, kmean_ref, *, col_chunk):
    i = pl.program_id(1)
    tm = x_ref.shape[1]
    h = _rms_modulate(x_ref[0], g_ref[...], mod_ref[0, 0:1, :], mod_ref[0, 1:2, :]).astype(BF16)

    w_moba = N_HEADS * HEAD_DIM
    blocks = tm // ATT_BLOCK
    first = _head_lane_mask()
    row_blk = blocks * i + lax.shift_right_logical(lax.broadcasted_iota(jnp.int32, (tm, LANES), 0),
                                                   ATT_BLOCK.bit_length() - 1)
    tag = jnp.where(_lane_index() - float(HEAD_DIM) == row_blk.astype(F32), 1.0, 0.0)
    mean_row = lax.broadcasted_iota(jnp.int32, kmean_ref.shape[2:], 0)

    @pl.when(i == 0)
    def _init():
        kmean_ref[...] = jnp.zeros_like(kmean_ref)

    for n in range(w_ref.shape[1] // col_chunk):
        cols = slice(n * col_chunk, (n + 1) * col_chunk)
        r = jnp.dot(h, w_ref[:, cols], preferred_element_type=F32)
        o_ref[0, :, cols] = r.astype(BF16)
        for c in range(col_chunk // LANES):
            col0 = n * col_chunk + c * LANES
            part = r[:, c * LANES:(c + 1) * LANES]
            if w_moba <= col0 < 2 * w_moba:
                kn = _head_rmsnorm(part, gk_ref[...])
                for hh, kh in enumerate((kn, pltpu.roll(kn, HEAD_DIM, 1))):
                    head = (col0 - w_moba) // HEAD_DIM + hh
                    kaug_ref[0, head] = jnp.where(first, kh, tag).astype(BF16)
                    means = kmean_ref[0, head]
                    for jb in range(blocks):
                        mean = jnp.mean(kh[jb * ATT_BLOCK:(jb + 1) * ATT_BLOCK], axis=0, keepdims=True)
                        means = jnp.where(mean_row == blocks * i + jb, jnp.where(first, mean, 0.0), means)
                    kmean_ref[0, head] = means
            elif 2 * w_moba <= col0 < 3 * w_moba:
                for hh, vh in enumerate((part, pltpu.roll(part, HEAD_DIM, 1))):
                    head = (col0 - 2 * w_moba) // HEAD_DIM + hh
                    vaug_ref[0, head] = jnp.where(first, vh, 1.0).astype(BF16)


def _inproj(x, mod, g, w_bf16, g_k):
    bsz, s, d = x.shape
    n = w_bf16.shape[1]
    tm = TOKEN_TILE
    n_blk = s // ATT_BLOCK
    per_head = lambda rows: pl.BlockSpec((1, N_HEADS, rows, LANES), lambda b, i: (b, 0, i, 0))
    return pl.pallas_call(
        functools.partial(_inproj_kernel, col_chunk=1024),
        grid=(bsz, s // tm),
        in_specs=[
            pl.BlockSpec((1, tm, d), lambda b, i: (b, i, 0)),
            pl.BlockSpec((1, N_MOD, d), lambda b, i: (b, 0, 0)),
            pl.BlockSpec((1, d), lambda b, i: (0, 0)),
            pl.BlockSpec((d, n), lambda b, i: (0, 0)),
            pl.BlockSpec((1, LANES), lambda b, i: (0, 0)),
        ],
        out_specs=[
            pl.BlockSpec((1, tm, n), lambda b, i: (b, i, 0)),
            per_head(tm),
            per_head(tm),
            pl.BlockSpec((1, N_HEADS, -(-n_blk // 8) * 8, LANES), lambda b, i: (b, 0, 0, 0)),
        ],
        out_shape=[
            jax.ShapeDtypeStruct((bsz, s, n), BF16),
            jax.ShapeDtypeStruct((bsz, N_HEADS, s, LANES), BF16),
            jax.ShapeDtypeStruct((bsz, N_HEADS, s, LANES), BF16),
            jax.ShapeDtypeStruct((bsz, N_HEADS, -(-n_blk // 8) * 8, LANES), F32),
        ],
        compiler_params=_params("arbitrary", "arbitrary"),
        name="inproj",
    )(x, mod, g.reshape(1, d), w_bf16, jnp.tile(g_k.reshape(1, HEAD_DIM), (1, HEADS_PER_STEP)))


def _indicator(cond, dtype):
    return jnp.where(cond, 1.0, 0.0).astype(dtype)


def _lane_index():
    return lax.broadcasted_iota(jnp.int32, (1, LANES), 1).astype(F32)


def _head_lane_mask():
    return lax.broadcasted_iota(jnp.int32, (1, LANES), 1) < HEAD_DIM


def _head_rmsnorm(t, g):
    first = _head_lane_mask()
    sq = t * t
    ss0 = jnp.sum(jnp.where(first, sq, 0.0), axis=-1, keepdims=True)
    ss1 = jnp.sum(jnp.where(first, 0.0, sq), axis=-1, keepdims=True)
    inv = jnp.where(first, lax.rsqrt(ss0 / HEAD_DIM + RMS_EPS), lax.rsqrt(ss1 / HEAD_DIM + RMS_EPS))
    return t * inv * g


def _split_heads(t):
    first = _head_lane_mask()
    zero = jnp.zeros_like(t)
    return jnp.where(first, t, zero), jnp.where(first, zero, t)


def _rel_bucket_table(max_dist):
    n = np.arange(max_dist + 1)
    max_exact = REL_BUCKETS // 2
    nf = np.maximum(n, 1).astype(np.float64)
    large = max_exact + (np.log(nf / max_exact) / math.log(REL_MAX_DIST / max_exact)
                         * (REL_BUCKETS - max_exact)).astype(np.int64)
    large = np.minimum(large, REL_BUCKETS - 1)
    return np.where(n < max_exact, n, large).astype(np.int32)


def _moba_kernel(q_ref, kaug_ref, vaug_ref, kmean_ref, gq_ref, rb_ref, o_ref,
                 toep_ref, s_ref, mvec_ref, acc_ref, *, n_blk):
    b = pl.program_id(1)
    qi = pl.program_id(2)
    blk = ATT_BLOCK
    qt = q_ref.shape[1]
    halves = qt // blk
    first = _head_lane_mask()

    @pl.when(jnp.logical_and(b == 0, qi == 0))
    def _build_bias_tiles():
        for h in range(HEADS_PER_STEP):
            for d in range(n_blk):
                r = rb_ref[0, h:h + 1, (n_blk - d) * blk:(n_blk - d + 2) * blk]
                rolled = pltpu.roll(jnp.broadcast_to(r, (blk, 2 * blk)), blk + 1, 1, stride=1, stride_axis=0)
                toep_ref[h, d] = rolled[:, :blk]

    qn = _head_rmsnorm(q_ref[0].astype(F32), gq_ref[...])
    gate_rows = kmean_ref.shape[2]
    blk_row = lax.broadcasted_iota(jnp.int32, (gate_rows, qt), 0).astype(F32)
    own = (halves * qi + lax.shift_right_logical(lax.broadcasted_iota(jnp.int32, (gate_rows, qt), 1),
                                                 blk.bit_length() - 1)).astype(F32)
    q_aug = []
    for h, qh in enumerate((qn, pltpu.roll(qn, HEAD_DIM, 1))):
        qh = jnp.where(first, qh, 0.0)
        km = kmean_ref[0, h]
        km_hi, q_hi = km.astype(BF16), qh.astype(BF16)
        km_lo, q_lo = (km - km_hi.astype(F32)).astype(BF16), (qh - q_hi.astype(F32)).astype(BF16)
        g = (lax.dot_general(km_hi, q_hi, _NT, preferred_element_type=F32)
             + lax.dot_general(km_lo, q_hi, _NT, preferred_element_type=F32)
             + lax.dot_general(km_hi, q_lo, _NT, preferred_element_type=F32))
        g = jnp.where(blk_row < own, g, NEG_INF)
        keep = jnp.where(blk_row == own, 1.0, 0.0)
        for _ in range(MOBA_TOPK):
            top = jnp.max(g, axis=0, keepdims=True)
            is_top = jnp.logical_and(g == top, top > NEG_INF)
            idx = jnp.min(jnp.where(is_top, blk_row, float(gate_rows)), axis=0, keepdims=True)
            pick = blk_row == idx
            keep = jnp.where(pick, 1.0, keep)
            g = jnp.where(pick, NEG_INF, g)
        mask_t = jnp.where(keep > 0.0, 0.0, MASK_LOGIT)
        mask_logit = jnp.concatenate([jnp.zeros((HEAD_DIM, qt), F32), mask_t,
                                      jnp.zeros((LANES - HEAD_DIM - gate_rows, qt), F32)], axis=0).T
        q_aug.append(jnp.where(first, qh * (HEAD_DIM ** -0.5), mask_logit).astype(BF16))

    def lane_max(s):
        return jnp.maximum(s[:, :LANES], s[:, LANES:])

    def scores(h, t, dist):
        k_t = kaug_ref[0, h, pl.ds(pl.multiple_of(t * qt, qt), qt), :]
        s = lax.dot_general(q_aug[h], k_t, _NT, preferred_element_type=F32)
        tiles = []
        for kb in range(halves):
            rows = []
            for qb in range(halves):
                part = s[qb * blk:(qb + 1) * blk, kb * blk:(kb + 1) * blk]
                if isinstance(dist, int) and halves * dist + qb - kb < 0:
                    rows.append(part)
                else:
                    rows.append(part + toep_ref[h, halves * dist + qb - kb])
            tiles.append(jnp.concatenate(rows, axis=0))
        return tiles

    row = lax.broadcasted_iota(jnp.int32, (qt, blk), 0)
    col = lax.broadcasted_iota(jnp.int32, (qt, blk), 1)
    for h in range(HEADS_PER_STEP):
        top = None
        for kb, s in enumerate(scores(h, qi, 0)):
            s = jnp.where(col + kb * blk <= row, s, NEG_INF)
            s_ref[h, halves * qi + kb] = s
            top = lane_max(s) if top is None else jnp.maximum(top, lane_max(s))
        mvec_ref[h] = top

    def in_pairs(count, visit):
        def quad(p, carry):
            visit(4 * p, 4)
            return carry

        lax.fori_loop(0, lax.shift_right_logical(count, 2), quad, 0)

        @pl.when((count & 2) == 2)
        def _pair():
            visit(count & ~3, 2)

        @pl.when((count & 1) == 1)
        def _last():
            visit(count - 1, 1)

    def pass1(t0, n_tiles):
        for h in range(HEADS_PER_STEP):
            top = mvec_ref[h]
            for t in [t0 + u for u in range(n_tiles)]:
                for kb, s in enumerate(scores(h, t, qi - t)):
                    s_ref[h, halves * t + kb] = s
                    top = jnp.maximum(top, lane_max(s))
            mvec_ref[h] = top

    in_pairs(qi, pass1)

    for h in range(HEADS_PER_STEP):
        m = jnp.max(mvec_ref[h], axis=-1, keepdims=True)
        mvec_ref[h] = jnp.broadcast_to(m, (qt, LANES))
        acc_ref[h] = jnp.zeros((qt, LANES), F32)

    def pass2(t0, n_tiles):
        keys = pl.ds(pl.multiple_of(t0 * qt, qt), n_tiles * qt)
        for h in range(HEADS_PER_STEP):
            m = mvec_ref[h]
            parts = []
            for kb in range(n_tiles * halves):
                s = s_ref[h, halves * t0 + kb]
                parts += [jnp.exp(s[:, :LANES] - m), jnp.exp(s[:, LANES:] - m)]
            p = jnp.concatenate(parts, axis=1).astype(BF16)
            acc_ref[h] += jnp.dot(p, vaug_ref[0, h, keys, :], preferred_element_type=F32)

    in_pairs(qi + 1, pass2)

    out = [acc_ref[h] / acc_ref[h][:, HEAD_DIM:HEAD_DIM + 1] for h in range(HEADS_PER_STEP)]
    o_ref[0] = jnp.where(first, out[0], pltpu.roll(out[1], HEAD_DIM, 1)).astype(o_ref.dtype)


def _moba(proj, kaug, vaug, kmean, g_q, rel_rev):
    bsz, s, _ = proj.shape
    blk = ATT_BLOCK
    n_blk = s // blk
    n_hp = N_HEADS // HEADS_PER_STEP
    w = N_HEADS * HEAD_DIM
    qt = MOBA_Q_TILE
    per_pair = lambda arr: pl.BlockSpec((1, HEADS_PER_STEP) + arr.shape[2:], lambda hp, b, i: (b, hp, 0, 0))
    return pl.pallas_call(
        functools.partial(_moba_kernel, n_blk=n_blk),
        grid=(n_hp, bsz, s // qt),
        in_specs=[
            pl.BlockSpec((1, qt, LANES), lambda hp, b, i: (b, i, hp)),
            per_pair(kaug), per_pair(vaug), per_pair(kmean),
            pl.BlockSpec((1, LANES), lambda hp, b, i: (0, 0)),
            pl.BlockSpec((1, HEADS_PER_STEP, rel_rev.shape[-1]), lambda hp, b, i: (hp, 0, 0)),
        ],
        out_specs=pl.BlockSpec((1, qt, LANES), lambda hp, b, i: (b, i, hp)),
        out_shape=jax.ShapeDtypeStruct((bsz, s, w), BF16),
        scratch_shapes=[
            pltpu.VMEM((HEADS_PER_STEP, n_blk, blk, blk), F32),
            pltpu.VMEM((HEADS_PER_STEP, n_blk, qt, blk), F32),
            pltpu.VMEM((HEADS_PER_STEP, qt, LANES), F32),
            pltpu.VMEM((HEADS_PER_STEP, qt, LANES), F32),
        ],
        compiler_params=_params("arbitrary", "arbitrary", "arbitrary"),
        name="moba",
    )(proj, kaug, vaug, kmean, jnp.tile(g_q.reshape(1, HEAD_DIM), (1, HEADS_PER_STEP)), rel_rev)


def _sb_kernel(q_ref, k_ref, v_ref, o_ref, acc_ref, carry_ref):
    ti = pl.program_id(2)
    blk = ATT_BLOCK
    n_sub = q_ref.shape[1] // blk
    first = n_sub * ti
    row = lax.broadcasted_iota(jnp.int32, (blk, blk), 0)
    col = lax.broadcasted_iota(jnp.int32, (blk, blk), 1)
    past = col < row
    later = _indicator(row > col, BF16)

    q_heads = []
    for sub in range(n_sub):
        heads = _split_heads(q_ref[0, sub * blk:(sub + 1) * blk, :].astype(F32) * (HEAD_DIM ** -0.5))
        q_heads.append([t.astype(BF16) for t in heads])

    def walk(jobs, fresh):
        for sub, blocks in jobs:
            for h in range(HEADS_PER_STEP):
                carry = None if fresh else carry_ref[sub, h]
                total = None if fresh else acc_ref[sub, h]
                for j, diagonal in blocks:
                    start = j * blk if isinstance(j, int) else pl.multiple_of(j * blk, blk)
                    z = lax.dot_general(q_heads[sub][h], k_ref[0, pl.ds(start, blk), :], _NT,
                                        preferred_element_type=F32)
                    sp = jnp.maximum(z, 0.0) + jnp.log(1.0 + jnp.exp(-jnp.abs(z)))
                    masked = jnp.where(past, sp, 0.0) if diagonal else sp
                    hi = masked.astype(BF16)
                    lo = (masked - hi.astype(F32)).astype(BF16)
                    after = (jnp.dot(hi, later, preferred_element_type=F32)
                             + jnp.dot(lo, later, preferred_element_type=F32))
                    row_sum = after[:, 0:1] + masked[:, 0:1]
                    a = jnp.exp(z - sp - after)
                    if diagonal:
                        a = jnp.where(past, a, 0.0)
                    pv = jnp.dot(a.astype(BF16), v_ref[0, pl.ds(start, blk), :], preferred_element_type=F32)
                    if carry is not None:
                        pv = pv * jnp.exp(-carry)
                    total = pv if total is None else total + pv
                    carry = row_sum if carry is None else carry + row_sum
                carry_ref[sub, h] = carry
                acc_ref[sub, h] = total

    @pl.when(ti == 0)
    def _first_step():
        walk([(sub, [(sub, True)] + ([(sub - 1, False)] if sub else [])) for sub in range(n_sub)], True)

    @pl.when(ti > 0)
    def _own_and_previous():
        walk([(sub, [(first + sub, True), (first + sub - 1, False)]) for sub in range(n_sub)], True)

    def slowest_decay(subs):
        worst = functools.reduce(jnp.minimum, [carry_ref[sub, h] for sub in subs for h in range(HEADS_PER_STEP)])
        return jnp.min(worst)

    @pl.when(slowest_decay(range(n_sub)) < SB_EXIT_SUM)
    def _walk_further_back():
        for sub in range(n_sub):
            def cond(state):
                j, worst = state
                return jnp.logical_and(j >= 0, worst < SB_EXIT_SUM)

            def body(state, sub=sub):
                j, _ = state
                walk([(sub, [(j, False)])], False)
                return j - 1, slowest_decay([sub])

            lax.while_loop(cond, body, (first + sub - 2, slowest_decay([sub])))

    for sub in range(n_sub):
        o_ref[0, sub * blk:(sub + 1) * blk, :] = jnp.where(
            _head_lane_mask(), acc_ref[sub, 0], acc_ref[sub, 1]).astype(o_ref.dtype)


def _stick_breaking(proj):
    bsz, s, _ = proj.shape
    blk = ATT_BLOCK
    qt = SB_Q_BLOCKS * blk
    n_hp = N_HEADS // HEADS_PER_STEP
    w = N_HEADS * HEAD_DIM
    base = 3 * n_hp
    return pl.pallas_call(
        _sb_kernel,
        grid=(n_hp, bsz, s // qt),
        in_specs=[
            pl.BlockSpec((1, qt, LANES), lambda hp, b, i: (b, i, base + hp)),
            pl.BlockSpec((1, s, LANES), lambda hp, b, i: (b, 0, base + n_hp + hp)),
            pl.BlockSpec((1, s, LANES), lambda hp, b, i: (b, 0, base + 2 * n_hp + hp)),
        ],
        out_specs=pl.BlockSpec((1, qt, LANES), lambda hp, b, i: (b, i, hp)),
        out_shape=jax.ShapeDtypeStruct((bsz, s, w), BF16),
        scratch_shapes=[
            pltpu.VMEM((SB_Q_BLOCKS, HEADS_PER_STEP, blk, LANES), F32),
            pltpu.VMEM((SB_Q_BLOCKS, HEADS_PER_STEP, blk, 1), F32),
        ],
        compiler_params=_params("arbitrary", "arbitrary", "arbitrary"),
        name="stickbrk",
    )(proj, proj, proj)


def _pack_rows(t):
    n = t.shape[1] // 2
    bits = pltpu.bitcast(t.astype(BF16).astype(F32), U32)
    return bits[:, :n] | (bits[:, n:] >> 16)


def _unpack_rows(p):
    hi = pltpu.bitcast(p & jnp.uint32(0xFFFF0000), F32)
    lo = pltpu.bitcast(p << 16, F32)
    return jnp.concatenate([hi, lo], axis=1)


def _merge_kernel(x_ref, oa_ref, ob_ref, ga_ref, gb_ref, mod_ref, wbm_ref, wbs_ref, wo_ref, gf_ref,
                  wrh_ref, wrl_ref, br_ref, x1_ref, h2_ref, rt_ref, slot_t_ref, cnt_ref, before_ref, carry_ref):
    ma = jnp.dot(oa_ref[0], wbm_ref[...], preferred_element_type=F32)
    mb = jnp.dot(ob_ref[0], wbs_ref[...], preferred_element_type=F32)
    merged = (jax.nn.sigmoid(ga_ref[0].astype(F32)) * ma + jax.nn.sigmoid(gb_ref[0].astype(F32)) * mb)
    t = jnp.dot(merged.astype(BF16), wo_ref[...], preferred_element_type=F32)
    x1 = x_ref[0] + mod_ref[0, 2:3, :] * t
    x1_ref[0] = x1
    h2 = _rms_modulate(x1, gf_ref[...], mod_ref[0, 3:4, :], mod_ref[0, 4:5, :])
    hi = h2.astype(BF16)
    h2_ref[0] = hi
    lo = (h2 - hi.astype(F32)).astype(BF16)
    logits = (lax.dot_general(wrh_ref[...], hi, _NT, preferred_element_type=F32)
              + lax.dot_general(wrh_ref[...], lo, _NT, preferred_element_type=F32)
              + lax.dot_general(wrl_ref[...], hi, _NT, preferred_element_type=F32) + br_ref[...])
    first_tile = jnp.logical_and(pl.program_id(0) == 0, pl.program_id(1) == 0)
    _route_tile(logits, first_tile, rt_ref, slot_t_ref, cnt_ref, before_ref, carry_ref)


def _merge(x, o_a, o_b, proj, mod, wbm, wbs, wo, g_ffn, w_router, b_router):
    bsz, s, d = x.shape
    w = o_a.shape[-1]
    tm = TOKEN_TILE
    gate_blk = (proj.shape[-1] - 2 * d) // d
    const = lambda shape: pl.BlockSpec(shape, lambda b, i: (0,) * len(shape))
    tok = lambda width: pl.BlockSpec((1, tm, width), lambda b, i: (b, i, 0))
    wr_hi = w_router.T.astype(BF16)
    wr_lo = (w_router.T - wr_hi.astype(F32)).astype(BF16)
    per_b = s // tm
    tiles = bsz * per_b
    per_tile = lambda rows, cols: pl.BlockSpec((1, rows, cols), lambda b, i: (b * per_b + i, 0, 0))
    return pl.pallas_call(
        _merge_kernel,
        grid=(bsz, s // tm),
        in_specs=[
            tok(d), tok(w), tok(w),
            pl.BlockSpec((1, tm, d), lambda b, i: (b, i, gate_blk)),
            pl.BlockSpec((1, tm, d), lambda b, i: (b, i, gate_blk + 1)),
            pl.BlockSpec((1, N_MOD, d), lambda b, i: (b, 0, 0)),
            const((w, d)), const((w, d)), const((d, d)), const((1, d)),
            const((LANES, d)), const((LANES, d)), const((LANES, 1)),
        ],
        out_specs=[tok(d), tok(d), tok(ROUTE_COLS), per_tile(ROUTE_COLS, tm), per_tile(LANES, 1), per_tile(LANES, 1)],
        out_shape=[
            jax.ShapeDtypeStruct((bsz, s, d), F32),
            jax.ShapeDtypeStruct((bsz, s, d), BF16),
            jax.ShapeDtypeStruct((bsz, s, ROUTE_COLS), F32),
            jax.ShapeDtypeStruct((tiles, ROUTE_COLS, tm), F32),
            jax.ShapeDtypeStruct((tiles, LANES, 1), F32),
            jax.ShapeDtypeStruct((tiles, LANES, 1), F32),
        ],
        scratch_shapes=[pltpu.VMEM((LANES, 1), F32)],
        compiler_params=_params("arbitrary", "arbitrary"),
        name="merge",
    )(x, o_a, o_b, proj, proj, mod, wbm, wbs, wo, g_ffn.reshape(1, d), wr_hi, wr_lo, b_router)


ROUTE_COLS = 8
SEG_ALIGN = 8


def _local_rows(tm):
    return 2 * tm + N_EXPERTS * SEG_ALIGN


def _route_tile(lg_t, first_tile, rt_ref, slot_t_ref, cnt_ref, before_ref, carry_ref):
    tm = lg_t.shape[1]
    rows = -(-(N_GROUPS + N_EXPERTS) // 8) * 8
    lg = lg_t[:rows]
    row = lax.broadcasted_iota(jnp.int32, (rows, tm), 0).astype(F32)

    @pl.when(first_tile)
    def _init():
        carry_ref[...] = jnp.zeros_like(carry_ref)

    def first_max(vals):
        top = jnp.max(vals, axis=0, keepdims=True)
        idx = jnp.min(jnp.where(vals == top, row, float(rows)), axis=0, keepdims=True)
        return top, idx

    g_logit = jnp.where(row < N_GROUPS, lg, NEG_INF)
    g_top, g_idx = first_max(g_logit)
    g_w = 1.0 / jnp.sum(jnp.exp(g_logit - g_top), axis=0, keepdims=True)
    lo = N_GROUPS + g_idx * EXPERTS_PER_GROUP
    e_logit = jnp.where(jnp.logical_and(row >= lo, row < lo + EXPERTS_PER_GROUP), lg, NEG_INF)
    e_top0, row0 = first_max(e_logit)
    e_top1, row1 = first_max(jnp.where(row == row0, NEG_INF, e_logit))
    z = jnp.exp(e_top1 - e_top0)
    w0 = g_w / (1.0 + z)
    w1 = g_w * z / (1.0 + z)

    hot = jnp.logical_or(row == row0, row == row1)
    t_src = lax.broadcasted_iota(jnp.int32, (tm, tm), 0)
    t_dst = lax.broadcasted_iota(jnp.int32, (tm, tm), 1)
    earlier = jnp.dot(_indicator(hot, BF16), _indicator(t_src < t_dst, BF16), preferred_element_type=F32)
    cnt = jnp.sum(_indicator(hot, F32), axis=1, keepdims=True)
    chunks = jnp.ceil(cnt * (1.0 / SEG_ALIGN))
    spare = jnp.zeros((LANES - rows, LANES), F32)
    e_dst = lax.broadcasted_iota(jnp.int32, (LANES, LANES), 0)
    e_src = lax.broadcasted_iota(jnp.int32, (LANES, LANES), 1)
    chunks_wide = jnp.concatenate([jnp.broadcast_to(chunks, (rows, LANES)), spare], axis=0).astype(BF16)
    seg_start = SEG_ALIGN * jnp.dot(_indicator(e_src < e_dst, BF16), chunks_wide,
                                    preferred_element_type=F32)[:rows, 0:1]
    local = seg_start + earlier
    slot0 = jnp.sum(jnp.where(row == row0, local, 0.0), axis=0, keepdims=True)
    slot1 = jnp.sum(jnp.where(row == row1, local, 0.0), axis=0, keepdims=True)
    seg_rows = jnp.concatenate([SEG_ALIGN * chunks, spare[:, 0:1]], axis=0)
    cnt_ref[0] = seg_rows
    before_ref[0] = carry_ref[...]
    carry_ref[...] += seg_rows

    out_row = lax.broadcasted_iota(jnp.int32, (ROUTE_COLS, tm), 0)
    packed = jnp.zeros((ROUTE_COLS, tm), F32)
    for c, v in enumerate([slot0, slot1, w0, w1]):
        packed = jnp.where(out_row == c, v, packed)
    slot_t_ref[0] = packed
    rt_ref[0] = jnp.concatenate([packed, jnp.zeros((LANES - ROUTE_COLS, tm), F32)], axis=0).T[:, :ROUTE_COLS]


def _for_each_chunk(tile, n_ref, rows_ref, visit):
    max_chunks = rows_ref.shape[0] // n_ref.shape[0]

    def per_chunk(c, carry):
        visit(pl.multiple_of(c * SEG_ALIGN, SEG_ALIGN), pl.multiple_of(rows_ref[tile * max_chunks + c], SEG_ALIGN))
        return carry

    lax.fori_loop(0, n_ref[tile], per_chunk, 0)


def _dispatch_kernel(n_ref, rows_ref, tail_row_ref, tail_n_ref, n_used_ref, slot_t_ref, h_ref, xs_ref,
                     loc_ref, zero_ref, sems, zero_sem):
    tile = pl.program_id(0)
    n_loc, tm = loc_ref.shape[1], h_ref.shape[0]
    buf = tile & 1

    slot = lax.broadcasted_iota(jnp.int32, (n_loc, tm), 0).astype(F32)
    place = jnp.logical_or(slot == slot_t_ref[0, 0:1, :], slot == slot_t_ref[0, 1:2, :])
    loc_ref[buf] = _pack_rows(jnp.dot(_indicator(place, BF16), h_ref[...], preferred_element_type=F32))

    def chunk_copy(b):
        return lambda local, row: pltpu.make_async_copy(
            loc_ref.at[b, pl.ds(local, SEG_ALIGN), :], xs_ref.at[pl.ds(row, SEG_ALIGN), :], sems.at[b])

    _for_each_chunk(tile, n_ref, rows_ref, lambda l, r: chunk_copy(buf)(l, r).start())

    @pl.when(tile > 0)
    def _previous_done():
        _for_each_chunk(tile - 1, n_ref, rows_ref, lambda l, r: chunk_copy(1 - buf)(l, r).wait())

    @pl.when(tile == pl.num_programs(0) - 1)
    def _last_done():
        _for_each_chunk(tile, n_ref, rows_ref, lambda l, r: chunk_copy(buf)(l, r).wait())
        zero_ref[...] = jnp.zeros_like(zero_ref)
        block_rows = zero_ref.shape[0]

        def fill(action):
            def per_expert(e, carry):
                def per_chunk(c, inner):
                    row = pl.multiple_of(tail_row_ref[e] + c * SEG_ALIGN, SEG_ALIGN)
                    action(pltpu.make_async_copy(zero_ref.at[pl.ds(0, SEG_ALIGN), :],
                                                 xs_ref.at[pl.ds(row, SEG_ALIGN), :], zero_sem))
                    return inner

                lax.fori_loop(0, tail_n_ref[e], per_chunk, 0)
                return carry

            lax.fori_loop(0, N_EXPERTS, per_expert, 0)

            def per_block(blk, carry):
                row = pl.multiple_of(blk * block_rows, block_rows)
                action(pltpu.make_async_copy(zero_ref, xs_ref.at[pl.ds(row, block_rows), :], zero_sem))
                return carry

            lax.fori_loop(n_used_ref[0], xs_ref.shape[0] // block_rows, per_block, 0)

        fill(lambda copy: copy.start())
        fill(lambda copy: copy.wait())


def _dispatch(n_chunks, chunk_rows, tail_row, tail_chunks, n_used, slot_t, h2, n_rows):
    n, d = h2.shape
    tm = TOKEN_TILE
    grid_spec = pltpu.PrefetchScalarGridSpec(
        num_scalar_prefetch=5,
        grid=(n // tm,),
        in_specs=[
            pl.BlockSpec((1, ROUTE_COLS, tm), lambda i, *_: (i, 0, 0)),
            pl.BlockSpec((tm, d), lambda i, *_: (i, 0)),
        ],
        out_specs=pl.BlockSpec(memory_space=pl.ANY),
        scratch_shapes=[pltpu.VMEM((2, _local_rows(tm), d // 2), U32), pltpu.VMEM((ROW_BLOCK, d // 2), U32),
                        pltpu.SemaphoreType.DMA((2,)), pltpu.SemaphoreType.DMA(())],
    )
    return pl.pallas_call(
        _dispatch_kernel,
        grid_spec=grid_spec,
        out_shape=jax.ShapeDtypeStruct((n_rows, d // 2), U32),
        compiler_params=_params("arbitrary"),
        name="dispatch",
    )(n_chunks, chunk_rows, tail_row, tail_chunks, n_used, slot_t, h2)


def _expert_kernel(blk_e_ref, n_used_ref, xs_ref, wg_ref, wu_ref, wd_ref, ys_ref, wg_bf, wu_bf, wd_bf):
    i = pl.program_id(0)
    used = i < n_used_ref[0]

    @pl.when(jnp.logical_and(used, jnp.logical_or(i == 0, blk_e_ref[i] != blk_e_ref[jnp.maximum(i - 1, 0)])))
    def _new_expert():
        wg_bf[...] = wg_ref[0].astype(BF16)
        wu_bf[...] = wu_ref[0].astype(BF16)
        wd_bf[...] = wd_ref[0].astype(BF16)

    @pl.when(used)
    def _compute():
        xb = _unpack_rows(xs_ref[...]).astype(BF16)
        g = jnp.dot(xb, wg_bf[...], preferred_element_type=F32)
        u = jnp.dot(xb, wu_bf[...], preferred_element_type=F32)
        hid = (g * jax.nn.sigmoid(g) * u).astype(BF16)
        ys_ref[...] = _pack_rows(jnp.dot(hid, wd_bf[...], preferred_element_type=F32))

    @pl.when(i >= n_used_ref[0])
    def _unused():
        ys_ref[...] = jnp.zeros_like(ys_ref)


def _experts(blk_e, n_used, xs, wg, wu, wd):
    n_rows, half = xs.shape
    _, d, de = wg.shape
    grid_spec = pltpu.PrefetchScalarGridSpec(
        num_scalar_prefetch=2,
        grid=(n_rows // ROW_BLOCK,),
        in_specs=[
            pl.BlockSpec((ROW_BLOCK, half), lambda i, e, u: (jnp.minimum(i, u[0] - 1), 0)),
            pl.BlockSpec((1, d, de), lambda i, e, u: (e[i], 0, 0)),
            pl.BlockSpec((1, d, de), lambda i, e, u: (e[i], 0, 0)),
            pl.BlockSpec((1, de, d), lambda i, e, u: (e[i], 0, 0)),
        ],
        out_specs=pl.BlockSpec((ROW_BLOCK, half), lambda i, e, u: (i, 0)),
        scratch_shapes=[pltpu.VMEM((d, de), BF16), pltpu.VMEM((d, de), BF16), pltpu.VMEM((de, d), BF16)],
    )
    return pl.pallas_call(
        _expert_kernel,
        grid_spec=grid_spec,
        out_shape=jax.ShapeDtypeStruct((n_rows, half), U32),
        compiler_params=_params("arbitrary"),
        name="experts",
    )(blk_e, n_used, xs, wg, wu, wd)


def _combine_kernel(n_ref, rows_ref, x1_ref, mod_ref, rt_ref, ys_ref, o_ref, loc_ref, sems):
    tm = x1_ref.shape[1]
    n_loc = loc_ref.shape[1]
    tile = pl.program_id(0) * pl.num_programs(1) + pl.program_id(1)
    n_tiles = pl.num_programs(0) * pl.num_programs(1)
    buf = tile & 1

    def chunk_copy(b):
        return lambda local, row: pltpu.make_async_copy(
            ys_ref.at[pl.ds(row, SEG_ALIGN), :], loc_ref.at[b, pl.ds(local, SEG_ALIGN), :], sems.at[b])

    @pl.when(tile == 0)
    def _init():
        loc_ref[...] = jnp.zeros_like(loc_ref)
        _for_each_chunk(tile, n_ref, rows_ref, lambda l, r: chunk_copy(buf)(l, r).start())

    @pl.when(tile + 1 < n_tiles)
    def _fetch_next():
        _for_each_chunk(tile + 1, n_ref, rows_ref, lambda l, r: chunk_copy(1 - buf)(l, r).start())

    _for_each_chunk(tile, n_ref, rows_ref, lambda l, r: chunk_copy(buf)(l, r).wait())

    y = _unpack_rows(loc_ref[buf]).astype(BF16)
    rt = rt_ref[0]
    slot = lax.broadcasted_iota(jnp.int32, (tm, n_loc), 1).astype(F32)
    mix = jnp.where(slot == rt[:, 0:1], rt[:, 2:3], jnp.where(slot == rt[:, 1:2], rt[:, 3:4], 0.0)).astype(BF16)
    o_ref[0] = x1_ref[0] + mod_ref[0, 5:6, :] * jnp.dot(mix, y, preferred_element_type=F32)


def _combine(n_chunks, chunk_rows, x1, mod, rt, ys):
    bsz, s, d = x1.shape
    tm = TOKEN_TILE
    grid_spec = pltpu.PrefetchScalarGridSpec(
        num_scalar_prefetch=2,
        grid=(bsz, s // tm),
        in_specs=[
            pl.BlockSpec((1, tm, d), lambda b, i, c, r: (b, i, 0)),
            pl.BlockSpec((1, N_MOD, d), lambda b, i, c, r: (b, 0, 0)),
            pl.BlockSpec((1, tm, ROUTE_COLS), lambda b, i, c, r: (b, i, 0)),
            pl.BlockSpec(memory_space=pl.ANY),
        ],
        out_specs=pl.BlockSpec((1, tm, d), lambda b, i, c, r: (b, i, 0)),
        scratch_shapes=[pltpu.VMEM((2, _local_rows(tm), d // 2), U32), pltpu.SemaphoreType.DMA((2,))],
    )
    return pl.pallas_call(
        _combine_kernel,
        grid_spec=grid_spec,
        out_shape=jax.ShapeDtypeStruct((bsz, s, d), F32),
        compiler_params=_params("arbitrary", "arbitrary"),
        name="combine",
    )(n_chunks, chunk_rows, x1, mod, rt, ys)


def _layer(x, mod, g_mix, w_in, g_q, g_k, rel_bias, w_br_moba, w_br_sb, w_out, g_ffn,
           w_rg, b_rg, w_re, b_re, w_gate, w_up, w_down):
    bsz, s, d = x.shape
    n = bsz * s
    n_blk = s // ATT_BLOCK

    proj, kaug, vaug, kmean = _inproj(x, mod, g_mix, w_in.astype(BF16), g_k)

    top = (n_blk + 1) * ATT_BLOCK - 1
    dist = np.maximum(top - np.arange((n_blk + 2) * ATT_BLOCK), 0)
    rel_rev = rel_bias[:, _rel_bucket_table(top)[dist]].reshape(N_HEADS // HEADS_PER_STEP, HEADS_PER_STEP, -1)

    o_a = _moba(proj, kaug, vaug, kmean, g_q, rel_rev)
    o_b = _stick_breaking(proj)

    pad = LANES - N_GROUPS - N_EXPERTS
    w_router = jnp.concatenate([w_rg, w_re, jnp.zeros((d, pad), F32)], axis=1)
    b_router = jnp.concatenate([b_rg, b_re, jnp.zeros((pad,), F32)]).reshape(LANES, 1)
    x1, h2, rt, slot_t, tile_cnt, tile_before = _merge(x, o_a, o_b, proj, mod, w_br_moba.astype(BF16),
                                                       w_br_sb.astype(BF16), w_out.astype(BF16), g_ffn,
                                                       w_router, b_router)

    experts = slice(N_GROUPS, N_GROUPS + N_EXPERTS)
    cnt = tile_cnt[:, experts, 0].astype(jnp.int32)
    total = jnp.sum(cnt, axis=0)
    padded = (total + ROW_BLOCK - 1) // ROW_BLOCK * ROW_BLOCK
    pend = jnp.cumsum(padded)
    pstart = pend - padded
    max_rows = 2 * n + cnt.shape[0] * N_EXPERTS * (SEG_ALIGN - 1) + N_EXPERTS * (ROW_BLOCK - 1)
    n_blocks = -(-max_rows // ROW_BLOCK)
    blk_row0 = jnp.arange(n_blocks, dtype=jnp.int32) * ROW_BLOCK
    blk_e = jnp.minimum(jnp.sum((blk_row0[:, None] >= pend[None, :]).astype(jnp.int32), axis=1), N_EXPERTS - 1)
    n_used = (pend[-1] // ROW_BLOCK).astype(jnp.int32).reshape(1)
    seg_row0 = pstart[None, :] + tile_before[:, experts, 0].astype(jnp.int32)
    chunks = cnt // SEG_ALIGN
    chunk_end = jnp.cumsum(chunks, axis=1)
    chunk_start = (chunk_end - chunks)[:, None, :]
    c = jnp.arange(_local_rows(TOKEN_TILE) // SEG_ALIGN, dtype=jnp.int32)[None, :, None]
    inside = jnp.logical_and(c >= chunk_start, c < chunk_end[:, None, :])
    chunk_rows = jnp.sum(jnp.where(inside, seg_row0[:, None, :] + SEG_ALIGN * (c - chunk_start), 0), axis=2).reshape(-1)
    n_chunks = chunk_end[:, -1]

    xs = _dispatch(n_chunks, chunk_rows, pstart + total, (padded - total) // SEG_ALIGN, n_used, slot_t,
                   h2.reshape(n, d), n_blocks * ROW_BLOCK)
    ys = _experts(blk_e, n_used, xs, w_gate, w_up, w_down)
    return _combine(n_chunks, chunk_rows, x1, mod, rt, ys)


def kernel(x, c, w_ada, b_ada, g_mix, w_in, g_q, g_k, rel_bias, w_br_moba, w_br_sb, w_out, g_ffn,
           w_rg, b_rg, w_re, b_re, w_gate, w_up, w_down):
    bsz, s, d = x.shape
    assert s % TOKEN_TILE == 0 and s % MOBA_Q_TILE == 0 and d % (2 * LANES) == 0
    for l in range(w_ada.shape[0]):
        mod = _adaln(c, w_ada[l], b_ada[l]).reshape(bsz, N_MOD, d)
        x = _layer(x, mod, g_mix[l], w_in[l], g_q[l], g_k[l], rel_bias, w_br_moba[l], w_br_sb[l], w_out[l],
                   g_ffn[l], w_rg[l], b_rg[l], w_re[l], b_re[l], w_gate[l], w_up[l], w_down[l])
    return x
```

```python
import functools
import math

import numpy as np
import jax
import jax.numpy as jnp
from jax import lax
from jax.experimental import pallas as pl
from jax.experimental.pallas import tpu as pltpu

F32 = jnp.float32
BF16 = jnp.bfloat16
U32 = jnp.uint32
HIGHEST = lax.Precision.HIGHEST

LANES = 128
VMEM_LIMIT_BYTES = 56 * 1024 * 1024

HEAD_DIM = 64
N_HEADS = 8
HEADS_PER_STEP = LANES // HEAD_DIM
ATT_BLOCK = 256
MOBA_Q_TILE = 2 * ATT_BLOCK
SB_Q_BLOCKS = 2
MOBA_TOPK = 3
REL_BUCKETS = 32
REL_MAX_DIST = 1024
N_GROUPS = 4
EXPERTS_PER_GROUP = 8
N_EXPERTS = N_GROUPS * EXPERTS_PER_GROUP
N_MOD = 6
RMS_EPS = 1e-6
ROW_BLOCK = 512
TOKEN_TILE = 512
NEG_INF = float("-inf")
MASK_LOGIT = -1e30
SB_EXIT_SUM = 110.0

_NT = (((1,), (1,)), ((), ()))


def _params(*semantics):
    return pltpu.CompilerParams(dimension_semantics=semantics, vmem_limit_bytes=VMEM_LIMIT_BYTES)


def _adaln_kernel(c_ref, w_ref, b_ref, o_ref):
    c = c_ref[...]
    s = c * jax.nn.sigmoid(c)
    o_ref[...] = jnp.dot(s, w_ref[...], preferred_element_type=F32, precision=HIGHEST) + b_ref[...]


def _adaln(c, w, b):
    bsz, d = c.shape
    n = w.shape[1]
    tn = 1536
    return pl.pallas_call(
        _adaln_kernel,
        grid=(n // tn,),
        in_specs=[
            pl.BlockSpec((bsz, d), lambda j: (0, 0)),
            pl.BlockSpec((d, tn), lambda j: (0, j)),
            pl.BlockSpec((1, tn), lambda j: (0, j)),
        ],
        out_specs=pl.BlockSpec((bsz, tn), lambda j: (0, j)),
        out_shape=jax.ShapeDtypeStruct((bsz, n), F32),
        compiler_params=_params("arbitrary"),
        name="adaln",
    )(c, w, b.reshape(1, n))


def _rms_modulate(x, g, shift, scale):
    y = x * lax.rsqrt(jnp.mean(x * x, axis=-1, keepdims=True) + RMS_EPS) * g
    return y * (1.0 + scale) + shift


def _inproj_kernel(x_ref, mod_ref, g_ref, w_ref, gk_ref, o_ref, kaug_ref, vaug_ref, kmean_ref, *, col_chunk):
    i = pl.program_id(1)
    tm = x_ref.shape[1]
    h = _rms_modulate(x_ref[0], g_ref[...], mod_ref[0, 0:1, :], mod_ref[0, 1:2, :]).astype(BF16)

    w_moba = N_HEADS * HEAD_DIM
    blocks = tm // ATT_BLOCK
    first = _head_lane_mask()
    row_blk = blocks * i + lax.shift_right_logical(lax.broadcasted_iota(jnp.int32, (tm, LANES), 0),
                                                   ATT_BLOCK.bit_length() - 1)
    tag = jnp.where(_lane_index() - float(HEAD_DIM) == row_blk.astype(F32), 1.0, 0.0)
    mean_row = lax.broadcasted_iota(jnp.int32, kmean_ref.shape[2:], 0)

    @pl.when(i == 0)
    def _init():
        kmean_ref[...] = jnp.zeros_like(kmean_ref)

    for n in range(w_ref.shape[1] // col_chunk):
        cols = slice(n * col_chunk, (n + 1) * col_chunk)
        r = jnp.dot(h, w_ref[:, cols], preferred_element_type=F32)
        o_ref[0, :, cols] = r.astype(BF16)
        for c in range(col_chunk // LANES):
            col0 = n * col_chunk + c * LANES
            part = r[:, c * LANES:(c + 1) * LANES]
            if w_moba <= col0 < 2 * w_moba:
                kn = _head_rmsnorm(part, gk_ref[...])
                for hh, kh in enumerate((kn, pltpu.roll(kn, HEAD_DIM, 1))):
                    head = (col0 - w_moba) // HEAD_DIM + hh
                    kaug_ref[0, head] = jnp.where(first, kh, tag).astype(BF16)
                    means = kmean_ref[0, head]
                    for jb in range(blocks):
                        mean = jnp.mean(kh[jb * ATT_BLOCK:(jb + 1) * ATT_BLOCK], axis=0, keepdims=True)
                        means = jnp.where(mean_row == blocks * i + jb, jnp.where(first, mean, 0.0), means)
                    kmean_ref[0, head] = means
            elif 2 * w_moba <= col0 < 3 * w_moba:
                for hh, vh in enumerate((part, pltpu.roll(part, HEAD_DIM, 1))):
                    head = (col0 - 2 * w_moba) // HEAD_DIM + hh
                    vaug_ref[0, head] = jnp.where(first, vh, 1.0).astype(BF16)


def _inproj(x, mod, g, w_bf16, g_k):
    bsz, s, d = x.shape
    n = w_bf16.shape[1]
    tm = TOKEN_TILE
    n_blk = s // ATT_BLOCK
    per_head = lambda rows: pl.BlockSpec((1, N_HEADS, rows, LANES), lambda b, i: (b, 0, i, 0))
    return pl.pallas_call(
        functools.partial(_inproj_kernel, col_chunk=1024),
        grid=(bsz, s // tm),
        in_specs=[
            pl.BlockSpec((1, tm, d), lambda b, i: (b, i, 0)),
            pl.BlockSpec((1, N_MOD, d), lambda b, i: (b, 0, 0)),
            pl.BlockSpec((1, d), lambda b, i: (0, 0)),
            pl.BlockSpec((d, n), lambda b, i: (0, 0)),
            pl.BlockSpec((1, LANES), lambda b, i: (0, 0)),
        ],
        out_specs=[
            pl.BlockSpec((1, tm, n), lambda b, i: (b, i, 0)),
            per_head(tm),
            per_head(tm),
            pl.BlockSpec((1, N_HEADS, -(-n_blk // 8) * 8, LANES), lambda b, i: (b, 0, 0, 0)),
        ],
        out_shape=[
            jax.ShapeDtypeStruct((bsz, s, n), BF16),
            jax.ShapeDtypeStruct((bsz, N_HEADS, s, LANES), BF16),
            jax.ShapeDtypeStruct((bsz, N_HEADS, s, LANES), BF16),
            jax.ShapeDtypeStruct((bsz, N_HEADS, -(-n_blk // 8) * 8, LANES), F32),
        ],
        compiler_params=_params("arbitrary", "arbitrary"),
        name="inproj",
    )(x, mod, g.reshape(1, d), w_bf16, jnp.tile(g_k.reshape(1, HEAD_DIM), (1, HEADS_PER_STEP)))


def _indicator(cond, dtype):
    return jnp.where(cond, 1.0, 0.0).astype(dtype)


def _lane_index():
    return lax.broadcasted_iota(jnp.int32, (1, LANES), 1).astype(F32)


def _head_lane_mask():
    return lax.broadcasted_iota(jnp.int32, (1, LANES), 1) < HEAD_DIM


def _head_rmsnorm(t, g):
    first = _head_lane_mask()
    sq = t * t
    ss0 = jnp.sum(jnp.where(first, sq, 0.0), axis=-1, keepdims=True)
    ss1 = jnp.sum(jnp.where(first, 0.0, sq), axis=-1, keepdims=True)
    inv = jnp.where(first, lax.rsqrt(ss0 / HEAD_DIM + RMS_EPS), lax.rsqrt(ss1 / HEAD_DIM + RMS_EPS))
    return t * inv * g


def _split_heads(t):
    first = _head_lane_mask()
    zero = jnp.zeros_like(t)
    return jnp.where(first, t, zero), jnp.where(first, zero, t)


def _rel_bucket_table(max_dist):
    n = np.arange(max_dist + 1)
    max_exact = REL_BUCKETS // 2
    nf = np.maximum(n, 1).astype(np.float64)
    large = max_exact + (np.log(nf / max_exact) / math.log(REL_MAX_DIST / max_exact)
                         * (REL_BUCKETS - max_exact)).astype(np.int64)
    large = np.minimum(large, REL_BUCKETS - 1)
    return np.where(n < max_exact, n, large).astype(np.int32)


def _moba_kernel(q_ref, kaug_ref, vaug_ref, kmean_ref, gq_ref, rb_ref, o_ref,
                 toep_ref, s_ref, mvec_ref, acc_ref, *, n_blk):
    b = pl.program_id(1)
    qi = pl.program_id(2)
    blk = ATT_BLOCK
    qt = q_ref.shape[1]
    halves = qt // blk
    first = _head_lane_mask()

    @pl.when(jnp.logical_and(b == 0, qi == 0))
    def _build_bias_tiles():
        for h in range(HEADS_PER_STEP):
            for d in range(n_blk):
                r = rb_ref[0, h:h + 1, (n_blk - d) * blk:(n_blk - d + 2) * blk]
                rolled = pltpu.roll(jnp.broadcast_to(r, (blk, 2 * blk)), blk + 1, 1, stride=1, stride_axis=0)
                toep_ref[h, d] = rolled[:, :blk]

    qn = _head_rmsnorm(q_ref[0].astype(F32), gq_ref[...])
    gate_rows = kmean_ref.shape[2]
    blk_row = lax.broadcasted_iota(jnp.int32, (gate_rows, qt), 0).astype(F32)
    own = (halves * qi + lax.shift_right_logical(lax.broadcasted_iota(jnp.int32, (gate_rows, qt), 1),
                                                 blk.bit_length() - 1)).astype(F32)
    q_aug = []
    for h, qh in enumerate((qn, pltpu.roll(qn, HEAD_DIM, 1))):
        qh = jnp.where(first, qh, 0.0)
        km = kmean_ref[0, h]
        km_hi, q_hi = km.astype(BF16), qh.astype(BF16)
        km_lo, q_lo = (km - km_hi.astype(F32)).astype(BF16), (qh - q_hi.astype(F32)).astype(BF16)
        g = (lax.dot_general(km_hi, q_hi, _NT, preferred_element_type=F32)
             + lax.dot_general(km_lo, q_hi, _NT, preferred_element_type=F32)
             + lax.dot_general(km_hi, q_lo, _NT, preferred_element_type=F32))
        g = jnp.where(blk_row < own, g, NEG_INF)
        keep = jnp.where(blk_row == own, 1.0, 0.0)
        for _ in range(MOBA_TOPK):
            top = jnp.max(g, axis=0, keepdims=True)
            is_top = jnp.logical_and(g == top, top > NEG_INF)
            idx = jnp.min(jnp.where(is_top, blk_row, float(gate_rows)), axis=0, keepdims=True)
            pick = blk_row == idx
            keep = jnp.where(pick, 1.0, keep)
            g = jnp.where(pick, NEG_INF, g)
        mask_t = jnp.where(keep > 0.0, 0.0, MASK_LOGIT)
        mask_logit = jnp.concatenate([jnp.zeros((HEAD_DIM, qt), F32), mask_t,
                                      jnp.zeros((LANES - HEAD_DIM - gate_rows, qt), F32)], axis=0).T
        q_aug.append(jnp.where(first, qh * (HEAD_DIM ** -0.5), mask_logit).astype(BF16))

    def lane_max(s):
        return jnp.maximum(s[:, :LANES], s[:, LANES:])

    def scores(h, t, dist):
        k_t = kaug_ref[0, h, pl.ds(pl.multiple_of(t * qt, qt), qt), :]
        s = lax.dot_general(q_aug[h], k_t, _NT, preferred_element_type=F32)
        tiles = []
        for kb in range(halves):
            rows = []
            for qb in range(halves):
                part = s[qb * blk:(qb + 1) * blk, kb * blk:(kb + 1) * blk]
                if isinstance(dist, int) and halves * dist + qb - kb < 0:
                    rows.append(part)
                else:
                    rows.append(part + toep_ref[h, halves * dist + qb - kb])
            tiles.append(jnp.concatenate(rows, axis=0))
        return tiles

    row = lax.broadcasted_iota(jnp.int32, (qt, blk), 0)
    col = lax.broadcasted_iota(jnp.int32, (qt, blk), 1)
    for h in range(HEADS_PER_STEP):
        top = None
        for kb, s in enumerate(scores(h, qi, 0)):
            s = jnp.where(col + kb * blk <= row, s, NEG_INF)
            s_ref[h, halves * qi + kb] = s
            top = lane_max(s) if top is None else jnp.maximum(top, lane_max(s))
        mvec_ref[h] = top

    def in_pairs(count, visit):
        def quad(p, carry):
            visit(4 * p, 4)
            return carry

        lax.fori_loop(0, lax.shift_right_logical(count, 2), quad, 0)

        @pl.when((count & 2) == 2)
        def _pair():
            visit(count & ~3, 2)

        @pl.when((count & 1) == 1)
        def _last():
            visit(count - 1, 1)

    def pass1(t0, n_tiles):
        for h in range(HEADS_PER_STEP):
            top = mvec_ref[h]
            for t in [t0 + u for u in range(n_tiles)]:
                for kb, s in enumerate(scores(h, t, qi - t)):
                    s_ref[h, halves * t + kb] = s
                    top = jnp.maximum(top, lane_max(s))
            mvec_ref[h] = top

    in_pairs(qi, pass1)

    for h in range(HEADS_PER_STEP):
        m = jnp.max(mvec_ref[h], axis=-1, keepdims=True)
        mvec_ref[h] = jnp.broadcast_to(m, (qt, LANES))
        acc_ref[h] = jnp.zeros((qt, LANES), F32)

    def pass2(t0, n_tiles):
        keys = pl.ds(pl.multiple_of(t0 * qt, qt), n_tiles * qt)
        for h in range(HEADS_PER_STEP):
            m = mvec_ref[h]
            parts = []
            for kb in range(n_tiles * halves):
                s = s_ref[h, halves * t0 + kb]
                parts += [jnp.exp(s[:, :LANES] - m), jnp.exp(s[:, LANES:] - m)]
            p = jnp.concatenate(parts, axis=1).astype(BF16)
            acc_ref[h] += jnp.dot(p, vaug_ref[0, h, keys, :], preferred_element_type=F32)

    in_pairs(qi + 1, pass2)

    out = [acc_ref[h] / acc_ref[h][:, HEAD_DIM:HEAD_DIM + 1] for h in range(HEADS_PER_STEP)]
    o_ref[0] = jnp.where(first, out[0], pltpu.roll(out[1], HEAD_DIM, 1)).astype(o_ref.dtype)


def _moba(proj, kaug, vaug, kmean, g_q, rel_rev):
    bsz, s, _ = proj.shape
    blk = ATT_BLOCK
    n_blk = s // blk
    n_hp = N_HEADS // HEADS_PER_STEP
    w = N_HEADS * HEAD_DIM
    qt = MOBA_Q_TILE
    per_pair = lambda arr: pl.BlockSpec((1, HEADS_PER_STEP) + arr.shape[2:], lambda hp, b, i: (b, hp, 0, 0))
    return pl.pallas_call(
        functools.partial(_moba_kernel, n_blk=n_blk),
        grid=(n_hp, bsz, s // qt),
        in_specs=[
            pl.BlockSpec((1, qt, LANES), lambda hp, b, i: (b, i, hp)),
            per_pair(kaug), per_pair(vaug), per_pair(kmean),
            pl.BlockSpec((1, LANES), lambda hp, b, i: (0, 0)),
            pl.BlockSpec((1, HEADS_PER_STEP, rel_rev.shape[-1]), lambda hp, b, i: (hp, 0, 0)),
        ],
        out_specs=pl.BlockSpec((1, qt, LANES), lambda hp, b, i: (b, i, hp)),
        out_shape=jax.ShapeDtypeStruct((bsz, s, w), BF16),
        scratch_shapes=[
            pltpu.VMEM((HEADS_PER_STEP, n_blk, blk, blk), F32),
            pltpu.VMEM((HEADS_PER_STEP, n_blk, qt, blk), F32),
            pltpu.VMEM((HEADS_PER_STEP, qt, LANES), F32),
            pltpu.VMEM((HEADS_PER_STEP, qt, LANES), F32),
        ],
        compiler_params=_params("arbitrary", "arbitrary", "arbitrary"),
        name="moba",
    )(proj, kaug, vaug, kmean, jnp.tile(g_q.reshape(1, HEAD_DIM), (1, HEADS_PER_STEP)), rel_rev)


def _sb_kernel(q_ref, k_ref, v_ref, o_ref, acc_ref, carry_ref):
    ti = pl.program_id(2)
    blk = ATT_BLOCK
    n_sub = q_ref.shape[1] // blk
    first = n_sub * ti
    row = lax.broadcasted_iota(jnp.int32, (blk, blk), 0)
    col = lax.broadcasted_iota(jnp.int32, (blk, blk), 1)
    past = col < row
    later = _indicator(row > col, BF16)

    q_heads = []
    for sub in range(n_sub):
        heads = _split_heads(q_ref[0, sub * blk:(sub + 1) * blk, :].astype(F32) * (HEAD_DIM ** -0.5))
        q_heads.append([t.astype(BF16) for t in heads])

    def walk(jobs, fresh):
        for sub, blocks in jobs:
            for h in range(HEADS_PER_STEP):
                carry = None if fresh else carry_ref[sub, h]
                total = None if fresh else acc_ref[sub, h]
                for j, diagonal in blocks:
                    start = j * blk if isinstance(j, int) else pl.multiple_of(j * blk, blk)
                    z = lax.dot_general(q_heads[sub][h], k_ref[0, pl.ds(start, blk), :], _NT,
                                        preferred_element_type=F32)
                    sp = jnp.maximum(z, 0.0) + jnp.log(1.0 + jnp.exp(-jnp.abs(z)))
                    masked = jnp.where(past, sp, 0.0) if diagonal else sp
                    hi = masked.astype(BF16)
                    lo = (masked - hi.astype(F32)).astype(BF16)
                    after = (jnp.dot(hi, later, preferred_element_type=F32)
                             + jnp.dot(lo, later, preferred_element_type=F32))
                    row_sum = after[:, 0:1] + masked[:, 0:1]
                    a = jnp.exp(z - sp - after)
                    if diagonal:
                        a = jnp.where(past, a, 0.0)
                    pv = jnp.dot(a.astype(BF16), v_ref[0, pl.ds(start, blk), :], preferred_element_type=F32)
                    if carry is not None:
                        pv = pv * jnp.exp(-carry)
                    total = pv if total is None else total + pv
                    carry = row_sum if carry is None else carry + row_sum
                carry_ref[sub, h] = carry
                acc_ref[sub, h] = total

    @pl.when(ti == 0)
    def _first_step():
        walk([(sub, [(sub, True)] + ([(sub - 1, False)] if sub else [])) for sub in range(n_sub)], True)

    @pl.when(ti > 0)
    def _own_and_previous():
        walk([(sub, [(first + sub, True), (first + sub - 1, False)]) for sub in range(n_sub)], True)

    def slowest_decay(subs):
        worst = functools.reduce(jnp.minimum, [carry_ref[sub, h] for sub in subs for h in range(HEADS_PER_STEP)])
        return jnp.min(worst)

    @pl.when(slowest_decay(range(n_sub)) < SB_EXIT_SUM)
    def _walk_further_back():
        for sub in range(n_sub):
            def cond(state):
                j, worst = state
                return jnp.logical_and(j >= 0, worst < SB_EXIT_SUM)

            def body(state, sub=sub):
                j, _ = state
                walk([(sub, [(j, False)])], False)
                return j - 1, slowest_decay([sub])

            lax.while_loop(cond, body, (first + sub - 2, slowest_decay([sub])))

    for sub in range(n_sub):
        o_ref[0, sub * blk:(sub + 1) * blk, :] = jnp.where(
            _head_lane_mask(), acc_ref[sub, 0], acc_ref[sub, 1]).astype(o_ref.dtype)


def _stick_breaking(proj):
    bsz, s, _ = proj.shape
    blk = ATT_BLOCK
    qt = SB_Q_BLOCKS * blk
    n_hp = N_HEADS // HEADS_PER_STEP
    w = N_HEADS * HEAD_DIM
    base = 3 * n_hp
    return pl.pallas_call(
        _sb_kernel,
        grid=(n_hp, bsz, s // qt),
        in_specs=[
            pl.BlockSpec((1, qt, LANES), lambda hp, b, i: (b, i, base + hp)),
            pl.BlockSpec((1, s, LANES), lambda hp, b, i: (b, 0, base + n_hp + hp)),
            pl.BlockSpec((1, s, LANES), lambda hp, b, i: (b, 0, base + 2 * n_hp + hp)),
        ],
        out_specs=pl.BlockSpec((1, qt, LANES), lambda hp, b, i: (b, i, hp)),
        out_shape=jax.ShapeDtypeStruct((bsz, s, w), BF16),
        scratch_shapes=[
            pltpu.VMEM((SB_Q_BLOCKS, HEADS_PER_STEP, blk, LANES), F32),
            pltpu.VMEM((SB_Q_BLOCKS, HEADS_PER_STEP, blk, 1), F32),
        ],
        compiler_params=_params("arbitrary", "arbitrary", "arbitrary"),
        name="stickbrk",
    )(proj, proj, proj)


def _pack_rows(t):
    n = t.shape[1] // 2
    bits = pltpu.bitcast(t.astype(BF16).astype(F32), U32)
    return bits[:, :n] | (bits[:, n:] >> 16)


def _unpack_rows(p):
    hi = pltpu.bitcast(p & jnp.uint32(0xFFFF0000), F32)
    lo = pltpu.bitcast(p << 16, F32)
    return jnp.concatenate([hi, lo], axis=1)


def _merge_kernel(x_ref, oa_ref, ob_ref, ga_ref, gb_ref, mod_ref, wbm_ref, wbs_ref, wo_ref, gf_ref,
                  wrh_ref, wrl_ref, br_ref, x1_ref, h2_ref, rt_ref, slot_t_ref, cnt_ref, before_ref, carry_ref):
    ma = jnp.dot(oa_ref[0], wbm_ref[...], preferred_element_type=F32)
    mb = jnp.dot(ob_ref[0], wbs_ref[...], preferred_element_type=F32)
    merged = (jax.nn.sigmoid(ga_ref[0].astype(F32)) * ma + jax.nn.sigmoid(gb_ref[0].astype(F32)) * mb)
    t = jnp.dot(merged.astype(BF16), wo_ref[...], preferred_element_type=F32)
    x1 = x_ref[0] + mod_ref[0, 2:3, :] * t
    x1_ref[0] = x1
    h2 = _rms_modulate(x1, gf_ref[...], mod_ref[0, 3:4, :], mod_ref[0, 4:5, :])
    hi = h2.astype(BF16)
    h2_ref[0] = hi
    lo = (h2 - hi.astype(F32)).astype(BF16)
    logits = (lax.dot_general(wrh_ref[...], hi, _NT, preferred_element_type=F32)
              + lax.dot_general(wrh_ref[...], lo, _NT, preferred_element_type=F32)
              + lax.dot_general(wrl_ref[...], hi, _NT, preferred_element_type=F32) + br_ref[...])
    first_tile = jnp.logical_and(pl.program_id(0) == 0, pl.program_id(1) == 0)
    _route_tile(logits, first_tile, rt_ref, slot_t_ref, cnt_ref, before_ref, carry_ref)


def _merge(x, o_a, o_b, proj, mod, wbm, wbs, wo, g_ffn, w_router, b_router):
    bsz, s, d = x.shape
    w = o_a.shape[-1]
    tm = TOKEN_TILE
    gate_blk = (proj.shape[-1] - 2 * d) // d
    const = lambda shape: pl.BlockSpec(shape, lambda b, i: (0,) * len(shape))
    tok = lambda width: pl.BlockSpec((1, tm, width), lambda b, i: (b, i, 0))
    wr_hi = w_router.T.astype(BF16)
    wr_lo = (w_router.T - wr_hi.astype(F32)).astype(BF16)
    per_b = s // tm
    tiles = bsz * per_b
    per_tile = lambda rows, cols: pl.BlockSpec((1, rows, cols), lambda b, i: (b * per_b + i, 0, 0))
    return pl.pallas_call(
        _merge_kernel,
        grid=(bsz, s // tm),
        in_specs=[
            tok(d), tok(w), tok(w),
            pl.BlockSpec((1, tm, d), lambda b, i: (b, i, gate_blk)),
            pl.BlockSpec((1, tm, d), lambda b, i: (b, i, gate_blk + 1)),
            pl.BlockSpec((1, N_MOD, d), lambda b, i: (b, 0, 0)),
            const((w, d)), const((w, d)), const((d, d)), const((1, d)),
            const((LANES, d)), const((LANES, d)), const((LANES, 1)),
        ],
        out_specs=[tok(d), tok(d), tok(ROUTE_COLS), per_tile(ROUTE_COLS, tm), per_tile(LANES, 1), per_tile(LANES, 1)],
        out_shape=[
            jax.ShapeDtypeStruct((bsz, s, d), F32),
            jax.ShapeDtypeStruct((bsz, s, d), BF16),
            jax.ShapeDtypeStruct((bsz, s, ROUTE_COLS), F32),
            jax.ShapeDtypeStruct((tiles, ROUTE_COLS, tm), F32),
            jax.ShapeDtypeStruct((tiles, LANES, 1), F32),
            jax.ShapeDtypeStruct((tiles, LANES, 1), F32),
        ],
        scratch_shapes=[pltpu.VMEM((LANES, 1), F32)],
        compiler_params=_params("arbitrary", "arbitrary"),
        name="merge",
    )(x, o_a, o_b, proj, proj, mod, wbm, wbs, wo, g_ffn.reshape(1, d), wr_hi, wr_lo, b_router)


ROUTE_COLS = 8
SEG_ALIGN = 8


def _local_rows(tm):
    return 2 * tm + N_EXPERTS * SEG_ALIGN


def _route_tile(lg_t, first_tile, rt_ref, slot_t_ref, cnt_ref, before_ref, carry_ref):
    tm = lg_t.shape[1]
    rows = -(-(N_GROUPS + N_EXPERTS) // 8) * 8
    lg = lg_t[:rows]
    row = lax.broadcasted_iota(jnp.int32, (rows, tm), 0).astype(F32)

    @pl.when(first_tile)
    def _init():
        carry_ref[...] = jnp.zeros_like(carry_ref)

    def first_max(vals):
        top = jnp.max(vals, axis=0, keepdims=True)
        idx = jnp.min(jnp.where(vals == top, row, float(rows)), axis=0, keepdims=True)
        return top, idx

    g_logit = jnp.where(row < N_GROUPS, lg, NEG_INF)
    g_top, g_idx = first_max(g_logit)
    g_w = 1.0 / jnp.sum(jnp.exp(g_logit - g_top), axis=0, keepdims=True)
    lo = N_GROUPS + g_idx * EXPERTS_PER_GROUP
    e_logit = jnp.where(jnp.logical_and(row >= lo, row < lo + EXPERTS_PER_GROUP), lg, NEG_INF)
    e_top0, row0 = first_max(e_logit)
    e_top1, row1 = first_max(jnp.where(row == row0, NEG_INF, e_logit))
    z = jnp.exp(e_top1 - e_top0)
    w0 = g_w / (1.0 + z)
    w1 = g_w * z / (1.0 + z)

    hot = jnp.logical_or(row == row0, row == row1)
    t_src = lax.broadcasted_iota(jnp.int32, (tm, tm), 0)
    t_dst = lax.broadcasted_iota(jnp.int32, (tm, tm), 1)
    earlier = jnp.dot(_indicator(hot, BF16), _indicator(t_src < t_dst, BF16), preferred_element_type=F32)
    cnt = jnp.sum(_indicator(hot, F32), axis=1, keepdims=True)
    chunks = jnp.ceil(cnt * (1.0 / SEG_ALIGN))
    spare = jnp.zeros((LANES - rows, LANES), F32)
    e_dst = lax.broadcasted_iota(jnp.int32, (LANES, LANES), 0)
    e_src = lax.broadcasted_iota(jnp.int32, (LANES, LANES), 1)
    chunks_wide = jnp.concatenate([jnp.broadcast_to(chunks, (rows, LANES)), spare], axis=0).astype(BF16)
    seg_start = SEG_ALIGN * jnp.dot(_indicator(e_src < e_dst, BF16), chunks_wide,
                                    preferred_element_type=F32)[:rows, 0:1]
    local = seg_start + earlier
    slot0 = jnp.sum(jnp.where(row == row0, local, 0.0), axis=0, keepdims=True)
    slot1 = jnp.sum(jnp.where(row == row1, local, 0.0), axis=0, keepdims=True)
    seg_rows = jnp.concatenate([SEG_ALIGN * chunks, spare[:, 0:1]], axis=0)
    cnt_ref[0] = seg_rows
    before_ref[0] = carry_ref[...]
    carry_ref[...] += seg_rows

    out_row = lax.broadcasted_iota(jnp.int32, (ROUTE_COLS, tm), 0)
    packed = jnp.zeros((ROUTE_COLS, tm), F32)
    for c, v in enumerate([slot0, slot1, w0, w1]):
        packed = jnp.where(out_row == c, v, packed)
    slot_t_ref[0] = packed
    rt_ref[0] = jnp.concatenate([packed, jnp.zeros((LANES - ROUTE_COLS, tm), F32)], axis=0).T[:, :ROUTE_COLS]


def _for_each_chunk(tile, n_ref, rows_ref, visit):
    max_chunks = rows_ref.shape[0] // n_ref.shape[0]

    def per_chunk(c, carry):
        visit(pl.multiple_of(c * SEG_ALIGN, SEG_ALIGN), pl.multiple_of(rows_ref[tile * max_chunks + c], SEG_ALIGN))
        return carry

    lax.fori_loop(0, n_ref[tile], per_chunk, 0)


def _dispatch_kernel(n_ref, rows_ref, tail_row_ref, tail_n_ref, n_used_ref, slot_t_ref, h_ref, xs_ref,
                     loc_ref, zero_ref, sems, zero_sem):
    tile = pl.program_id(0)
    n_loc, tm = loc_ref.shape[1], h_ref.shape[0]
    buf = tile & 1

    slot = lax.broadcasted_iota(jnp.int32, (n_loc, tm), 0).astype(F32)
    place = jnp.logical_or(slot == slot_t_ref[0, 0:1, :], slot == slot_t_ref[0, 1:2, :])
    loc_ref[buf] = _pack_rows(jnp.dot(_indicator(place, BF16), h_ref[...], preferred_element_type=F32))

    def chunk_copy(b):
        return lambda local, row: pltpu.make_async_copy(
            loc_ref.at[b, pl.ds(local, SEG_ALIGN), :], xs_ref.at[pl.ds(row, SEG_ALIGN), :], sems.at[b])

    _for_each_chunk(tile, n_ref, rows_ref, lambda l, r: chunk_copy(buf)(l, r).start())

    @pl.when(tile > 0)
    def _previous_done():
        _for_each_chunk(tile - 1, n_ref, rows_ref, lambda l, r: chunk_copy(1 - buf)(l, r).wait())

    @pl.when(tile == pl.num_programs(0) - 1)
    def _last_done():
        _for_each_chunk(tile, n_ref, rows_ref, lambda l, r: chunk_copy(buf)(l, r).wait())
        zero_ref[...] = jnp.zeros_like(zero_ref)
        block_rows = zero_ref.shape[0]

        def fill(action):
            def per_expert(e, carry):
                def per_chunk(c, inner):
                    row = pl.multiple_of(tail_row_ref[e] + c * SEG_ALIGN, SEG_ALIGN)
                    action(pltpu.make_async_copy(zero_ref.at[pl.ds(0, SEG_ALIGN), :],
                                                 xs_ref.at[pl.ds(row, SEG_ALIGN), :], zero_sem))
                    return inner

                lax.fori_loop(0, tail_n_ref[e], per_chunk, 0)
                return carry

            lax.fori_loop(0, N_EXPERTS, per_expert, 0)

            def per_block(blk, carry):
                row = pl.multiple_of(blk * block_rows, block_rows)
                action(pltpu.make_async_copy(zero_ref, xs_ref.at[pl.ds(row, block_rows), :], zero_sem))
                return carry

            lax.fori_loop(n_used_ref[0], xs_ref.shape[0] // block_rows, per_block, 0)

        fill(lambda copy: copy.start())
        fill(lambda copy: copy.wait())


def _dispatch(n_chunks, chunk_rows, tail_row, tail_chunks, n_used, slot_t, h2, n_rows):
    n, d = h2.shape
    tm = TOKEN_TILE
    grid_spec = pltpu.PrefetchScalarGridSpec(
        num_scalar_prefetch=5,
        grid=(n // tm,),
        in_specs=[
            pl.BlockSpec((1, ROUTE_COLS, tm), lambda i, *_: (i, 0, 0)),
            pl.BlockSpec((tm, d), lambda i, *_: (i, 0)),
        ],
        out_specs=pl.BlockSpec(memory_space=pl.ANY),
        scratch_shapes=[pltpu.VMEM((2, _local_rows(tm), d // 2), U32), pltpu.VMEM((ROW_BLOCK, d // 2), U32),
                        pltpu.SemaphoreType.DMA((2,)), pltpu.SemaphoreType.DMA(())],
    )
    return pl.pallas_call(
        _dispatch_kernel,
        grid_spec=grid_spec,
        out_shape=jax.ShapeDtypeStruct((n_rows, d // 2), U32),
        compiler_params=_params("arbitrary"),
        name="dispatch",
    )(n_chunks, chunk_rows, tail_row, tail_chunks, n_used, slot_t, h2)


def _expert_kernel(row0_ref, nblk_ref, n_used_ref, xs_ref, wg_ref, wu_ref, wd_ref, ys_ref,
                   wg_bf, wu_bf, wd_bf, x_buf, y_buf, in_sems, out_sems):
    e = pl.program_id(0)
    rb = x_buf.shape[1]
    n_blk = nblk_ref[e]
    base = row0_ref[e]

    def rows(j):
        return pl.ds(pl.multiple_of(base + j * rb, rb), rb)

    def x_copy(j, slot):
        return pltpu.make_async_copy(xs_ref.at[rows(j), :], x_buf.at[slot], in_sems.at[slot])

    def y_copy(j, slot):
        return pltpu.make_async_copy(y_buf.at[slot], ys_ref.at[rows(j), :], out_sems.at[slot])

    @pl.when(n_blk > 0)
    def _first_rows():
        x_copy(0, 0).start()

    wg_bf[...] = wg_ref[0].astype(BF16)
    wu_bf[...] = wu_ref[0].astype(BF16)
    wd_bf[...] = wd_ref[0].astype(BF16)

    def row_block(j, carry):
        slot = j & 1
        x_copy(j, slot).wait()

        @pl.when(j + 1 < n_blk)
        def _next_rows():
            x_copy(j + 1, 1 - slot).start()

        @pl.when(j >= 2)
        def _slot_free():
            y_copy(j - 2, slot).wait()

        xb = _unpack_rows(x_buf[slot]).astype(BF16)
        g = jnp.dot(xb, wg_bf[...], preferred_element_type=F32)
        u = jnp.dot(xb, wu_bf[...], preferred_element_type=F32)
        hid = (g * jax.nn.sigmoid(g) * u).astype(BF16)
        y_buf[slot] = _pack_rows(jnp.dot(hid, wd_bf[...], preferred_element_type=F32))
        y_copy(j, slot).start()
        return carry

    lax.fori_loop(0, n_blk, row_block, 0)

    @pl.when(n_blk >= 2)
    def _drain_older():
        y_copy(n_blk - 2, n_blk & 1).wait()

    @pl.when(n_blk >= 1)
    def _drain_last():
        y_copy(n_blk - 1, (n_blk - 1) & 1).wait()

    @pl.when(e == pl.num_programs(0) - 1)
    def _unused_blocks():
        y_buf[0] = jnp.zeros((rb, y_buf.shape[2]), y_buf.dtype)

        def fill(action):
            def per_block(blk, carry):
                dst = ys_ref.at[pl.ds(pl.multiple_of(blk * rb, rb), rb), :]
                action(pltpu.make_async_copy(y_buf.at[0], dst, out_sems.at[0]))
                return carry

            lax.fori_loop(n_used_ref[0], ys_ref.shape[0] // rb, per_block, 0)

        fill(lambda copy: copy.start())
        fill(lambda copy: copy.wait())


def _experts(row0, n_blk, n_used, xs, wg, wu, wd):
    n_rows, half = xs.shape
    n_experts, d, de = wg.shape
    grid_spec = pltpu.PrefetchScalarGridSpec(
        num_scalar_prefetch=3,
        grid=(n_experts,),
        in_specs=[
            pl.BlockSpec(memory_space=pl.ANY),
            pl.BlockSpec((1, d, de), lambda e, *_: (e, 0, 0)),
            pl.BlockSpec((1, d, de), lambda e, *_: (e, 0, 0)),
            pl.BlockSpec((1, de, d), lambda e, *_: (e, 0, 0)),
        ],
        out_specs=pl.BlockSpec(memory_space=pl.ANY),
        scratch_shapes=[pltpu.VMEM((d, de), BF16), pltpu.VMEM((d, de), BF16), pltpu.VMEM((de, d), BF16),
                        pltpu.VMEM((2, ROW_BLOCK, half), U32), pltpu.VMEM((2, ROW_BLOCK, half), U32),
                        pltpu.SemaphoreType.DMA((2,)), pltpu.SemaphoreType.DMA((2,))],
    )
    return pl.pallas_call(
        _expert_kernel,
        grid_spec=grid_spec,
        out_shape=jax.ShapeDtypeStruct((n_rows, half), U32),
        compiler_params=_params("arbitrary"),
        name="experts",
    )(row0, n_blk, n_used, xs, wg, wu, wd)


def _combine_kernel(n_ref, rows_ref, x1_ref, mod_ref, rt_ref, ys_ref, o_ref, loc_ref, sems):
    tm = x1_ref.shape[1]
    n_loc = loc_ref.shape[1]
    tile = pl.program_id(0) * pl.num_programs(1) + pl.program_id(1)
    n_tiles = pl.num_programs(0) * pl.num_programs(1)
    buf = tile & 1

    def chunk_copy(b):
        return lambda local, row: pltpu.make_async_copy(
            ys_ref.at[pl.ds(row, SEG_ALIGN), :], loc_ref.at[b, pl.ds(local, SEG_ALIGN), :], sems.at[b])

    @pl.when(tile == 0)
    def _init():
        loc_ref[...] = jnp.zeros_like(loc_ref)
        _for_each_chunk(tile, n_ref, rows_ref, lambda l, r: chunk_copy(buf)(l, r).start())

    @pl.when(tile + 1 < n_tiles)
    def _fetch_next():
        _for_each_chunk(tile + 1, n_ref, rows_ref, lambda l, r: chunk_copy(1 - buf)(l, r).start())

    _for_each_chunk(tile, n_ref, rows_ref, lambda l, r: chunk_copy(buf)(l, r).wait())

    y = _unpack_rows(loc_ref[buf]).astype(BF16)
    rt = rt_ref[0]
    slot = lax.broadcasted_iota(jnp.int32, (tm, n_loc), 1).astype(F32)
    mix = jnp.where(slot == rt[:, 0:1], rt[:, 2:3], jnp.where(slot == rt[:, 1:2], rt[:, 3:4], 0.0)).astype(BF16)
    o_ref[0] = x1_ref[0] + mod_ref[0, 5:6, :] * jnp.dot(mix, y, preferred_element_type=F32)


def _combine(n_chunks, chunk_rows, x1, mod, rt, ys):
    bsz, s, d = x1.shape
    tm = TOKEN_TILE
    grid_spec = pltpu.PrefetchScalarGridSpec(
        num_scalar_prefetch=2,
        grid=(bsz, s // tm),
        in_specs=[
            pl.BlockSpec((1, tm, d), lambda b, i, c, r: (b, i, 0)),
            pl.BlockSpec((1, N_MOD, d), lambda b, i, c, r: (b, 0, 0)),
            pl.BlockSpec((1, tm, ROUTE_COLS), lambda b, i, c, r: (b, i, 0)),
            pl.BlockSpec(memory_space=pl.ANY),
        ],
        out_specs=pl.BlockSpec((1, tm, d), lambda b, i, c, r: (b, i, 0)),
        scratch_shapes=[pltpu.VMEM((2, _local_rows(tm), d // 2), U32), pltpu.SemaphoreType.DMA((2,))],
    )
    return pl.pallas_call(
        _combine_kernel,
        grid_spec=grid_spec,
        out_shape=jax.ShapeDtypeStruct((bsz, s, d), F32),
        compiler_params=_params("arbitrary", "arbitrary"),
        name="combine",
    )(n_chunks, chunk_rows, x1, mod, rt, ys)


def _layer(x, mod, g_mix, w_in, g_q, g_k, rel_bias, w_br_moba, w_br_sb, w_out, g_ffn,
           w_rg, b_rg, w_re, b_re, w_gate, w_up, w_down):
    bsz, s, d = x.shape
    n = bsz * s
    n_blk = s // ATT_BLOCK

    proj, kaug, vaug, kmean = _inproj(x, mod, g_mix, w_in.astype(BF16), g_k)

    top = (n_blk + 1) * ATT_BLOCK - 1
    dist = np.maximum(top - np.arange((n_blk + 2) * ATT_BLOCK), 0)
    rel_rev = rel_bias[:, _rel_bucket_table(top)[dist]].reshape(N_HEADS // HEADS_PER_STEP, HEADS_PER_STEP, -1)

    o_a = _moba(proj, kaug, vaug, kmean, g_q, rel_rev)
    o_b = _stick_breaking(proj)

    pad = LANES - N_GROUPS - N_EXPERTS
    w_router = jnp.concatenate([w_rg, w_re, jnp.zeros((d, pad), F32)], axis=1)
    b_router = jnp.concatenate([b_rg, b_re, jnp.zeros((pad,), F32)]).reshape(LANES, 1)
    x1, h2, rt, slot_t, tile_cnt, tile_before = _merge(x, o_a, o_b, proj, mod, w_br_moba.astype(BF16),
                                                       w_br_sb.astype(BF16), w_out.astype(BF16), g_ffn,
                                                       w_router, b_router)

    experts = slice(N_GROUPS, N_GROUPS + N_EXPERTS)
    cnt = tile_cnt[:, experts, 0].astype(jnp.int32)
    total = jnp.sum(cnt, axis=0)
    padded = (total + ROW_BLOCK - 1) // ROW_BLOCK * ROW_BLOCK
    pend = jnp.cumsum(padded)
    pstart = pend - padded
    max_rows = 2 * n + cnt.shape[0] * N_EXPERTS * (SEG_ALIGN - 1) + N_EXPERTS * (ROW_BLOCK - 1)
    n_blocks = -(-max_rows // ROW_BLOCK)
    n_used = (pend[-1] // ROW_BLOCK).astype(jnp.int32).reshape(1)
    seg_row0 = pstart[None, :] + tile_before[:, experts, 0].astype(jnp.int32)
    chunks = cnt // SEG_ALIGN
    chunk_end = jnp.cumsum(chunks, axis=1)
    chunk_start = (chunk_end - chunks)[:, None, :]
    c = jnp.arange(_local_rows(TOKEN_TILE) // SEG_ALIGN, dtype=jnp.int32)[None, :, None]
    inside = jnp.logical_and(c >= chunk_start, c < chunk_end[:, None, :])
    chunk_rows = jnp.sum(jnp.where(inside, seg_row0[:, None, :] + SEG_ALIGN * (c - chunk_start), 0), axis=2).reshape(-1)
    n_chunks = chunk_end[:, -1]

    xs = _dispatch(n_chunks, chunk_rows, pstart + total, (padded - total) // SEG_ALIGN, n_used, slot_t,
                   h2.reshape(n, d), n_blocks * ROW_BLOCK)
    ys = _experts(pstart, padded // ROW_BLOCK, n_used, xs, w_gate, w_up, w_down)
    return _combine(n_chunks, chunk_rows, x1, mod, rt, ys)


def kernel(x, c, w_ada, b_ada, g_mix, w_in, g_q, g_k, rel_bias, w_br_moba, w_br_sb, w_out, g_ffn,
           w_rg, b_rg, w_re, b_re, w_gate, w_up, w_down):
    bsz, s, d = x.shape
    assert s % TOKEN_TILE == 0 and s % MOBA_Q_TILE == 0 and d % (2 * LANES) == 0
    for l in range(w_ada.shape[0]):
        mod = _adaln(c, w_ada[l], b_ada[l]).reshape(bsz, N_MOD, d)
        x = _layer(x, mod, g_mix[l], w_in[l], g_q[l], g_k[l], rel_bias, w_br_moba[l], w_br_sb[l], w_out[l],
                   g_ffn[l], w_rg[l], b_rg[l], w_re[l], b_re[l], w_gate[l], w_up[l], w_down[l])
    return x
```

```python
import functools
import math

import numpy as np
import jax
import jax.numpy as jnp
from jax import lax
from jax.experimental import pallas as pl
from jax.experimental.pallas import tpu as pltpu

F32 = jnp.float32
BF16 = jnp.bfloat16
U32 = jnp.uint32
HIGHEST = lax.Precision.HIGHEST

LANES = 128
VMEM_LIMIT_BYTES = 56 * 1024 * 1024

HEAD_DIM = 64
N_HEADS = 8
HEADS_PER_STEP = LANES // HEAD_DIM
ATT_BLOCK = 256
MOBA_Q_TILE = 2 * ATT_BLOCK
SB_Q_BLOCKS = 2
MOBA_TOPK = 3
REL_BUCKETS = 32
REL_MAX_DIST = 1024
N_GROUPS = 4
EXPERTS_PER_GROUP = 8
N_EXPERTS = N_GROUPS * EXPERTS_PER_GROUP
N_MOD = 6
RMS_EPS = 1e-6
ROW_BLOCK = 512
TOKEN_TILE = 512
NEG_INF = float("-inf")
MASK_LOGIT = -1e30
SB_EXIT_SUM = 110.0

_NT = (((1,), (1,)), ((), ()))


def _params(*semantics):
    return pltpu.CompilerParams(dimension_semantics=semantics, vmem_limit_bytes=VMEM_LIMIT_BYTES)


def _adaln_kernel(c_ref, w_ref, b_ref, o_ref):
    c = c_ref[...]
    s = c * jax.nn.sigmoid(c)
    o_ref[...] = jnp.dot(s, w_ref[...], preferred_element_type=F32, precision=HIGHEST) + b_ref[...]


def _adaln(c, w, b):
    bsz, d = c.shape
    n = w.shape[1]
    tn = 1536
    return pl.pallas_call(
        _adaln_kernel,
        grid=(n // tn,),
        in_specs=[
            pl.BlockSpec((bsz, d), lambda j: (0, 0)),
            pl.BlockSpec((d, tn), lambda j: (0, j)),
            pl.BlockSpec((1, tn), lambda j: (0, j)),
        ],
        out_specs=pl.BlockSpec((bsz, tn), lambda j: (0, j)),
        out_shape=jax.ShapeDtypeStruct((bsz, n), F32),
        compiler_params=_params("arbitrary"),
        name="adaln",
    )(c, w, b.reshape(1, n))


def _rms_modulate(x, g, shift, scale):
    y = x * lax.rsqrt(jnp.mean(x * x, axis=-1, keepdims=True) + RMS_EPS) * g
    return y * (1.0 + scale) + shift


def _inproj_kernel(x_ref, mod_ref, g_ref, w_ref, gk_ref, o_ref, kaug_ref, vaug_ref, kmean_ref, *, col_chunk):
    i = pl.program_id(1)
    tm = x_ref.shape[1]
    h = _rms_modulate(x_ref[0], g_ref[...], mod_ref[0, 0:1, :], mod_ref[0, 1:2, :]).astype(BF16)

    w_moba = N_HEADS * HEAD_DIM
    blocks = tm // ATT_BLOCK
    first = _head_lane_mask()
    row_blk = blocks * i + lax.shift_right_logical(lax.broadcasted_iota(jnp.int32, (tm, LANES), 0),
                                                   ATT_BLOCK.bit_length() - 1)
    tag = jnp.where(_lane_index() - float(HEAD_DIM) == row_blk.astype(F32), 1.0, 0.0)
    mean_row = lax.broadcasted_iota(jnp.int32, kmean_ref.shape[2:], 0)

    @pl.when(i == 0)
    def _init():
        kmean_ref[...] = jnp.zeros_like(kmean_ref)

    for n in range(w_ref.shape[1] // col_chunk):
        cols = slice(n * col_chunk, (n + 1) * col_chunk)
        r = jnp.dot(h, w_ref[:, cols], preferred_element_type=F32)
        o_ref[0, :, cols] = r.astype(BF16)
        for c in range(col_chunk // LANES):
            col0 = n * col_chunk + c * LANES
            part = r[:, c * LANES:(c + 1) * LANES]
            if w_moba <= col0 < 2 * w_moba:
                kn = _head_rmsnorm(part, gk_ref[...])
                for hh, kh in enumerate((kn, pltpu.roll(kn, HEAD_DIM, 1))):
                    head = (col0 - w_moba) // HEAD_DIM + hh
                    kaug_ref[0, head] = jnp.where(first, kh, tag).astype(BF16)
                    means = kmean_ref[0, head]
                    for jb in range(blocks):
                        mean = jnp.mean(kh[jb * ATT_BLOCK:(jb + 1) * ATT_BLOCK], axis=0, keepdims=True)
                        means = jnp.where(mean_row == blocks * i + jb, jnp.where(first, mean, 0.0), means)
                    kmean_ref[0, head] = means
            elif 2 * w_moba <= col0 < 3 * w_moba:
                for hh, vh in enumerate((part, pltpu.roll(part, HEAD_DIM, 1))):
                    head = (col0 - 2 * w_moba) // HEAD_DIM + hh
                    vaug_ref[0, head] = jnp.where(first, vh, 1.0).astype(BF16)


def _inproj(x, mod, g, w_bf16, g_k):
    bsz, s, d = x.shape
    n = w_bf16.shape[1]
    tm = TOKEN_TILE
    n_blk = s // ATT_BLOCK
    per_head = lambda rows: pl.BlockSpec((1, N_HEADS, rows, LANES), lambda b, i: (b, 0, i, 0))
    return pl.pallas_call(
        functools.partial(_inproj_kernel, col_chunk=1024),
        grid=(bsz, s // tm),
        in_specs=[
            pl.BlockSpec((1, tm, d), lambda b, i: (b, i, 0)),
            pl.BlockSpec((1, N_MOD, d), lambda b, i: (b, 0, 0)),
            pl.BlockSpec((1, d), lambda b, i: (0, 0)),
            pl.BlockSpec((d, n), lambda b, i: (0, 0)),
            pl.BlockSpec((1, LANES), lambda b, i: (0, 0)),
        ],
        out_specs=[
            pl.BlockSpec((1, tm, n), lambda b, i: (b, i, 0)),
            per_head(tm),
            per_head(tm),
            pl.BlockSpec((1, N_HEADS, -(-n_blk // 8) * 8, LANES), lambda b, i: (b, 0, 0, 0)),
        ],
        out_shape=[
            jax.ShapeDtypeStruct((bsz, s, n), BF16),
            jax.ShapeDtypeStruct((bsz, N_HEADS, s, LANES), BF16),
            jax.ShapeDtypeStruct((bsz, N_HEADS, s, LANES), BF16),
            jax.ShapeDtypeStruct((bsz, N_HEADS, -(-n_blk // 8) * 8, LANES), F32),
        ],
        compiler_params=_params("arbitrary", "arbitrary"),
        name="inproj",
    )(x, mod, g.reshape(1, d), w_bf16, jnp.tile(g_k.reshape(1, HEAD_DIM), (1, HEADS_PER_STEP)))


def _indicator(cond, dtype):
    return jnp.where(cond, 1.0, 0.0).astype(dtype)


def _lane_index():
    return lax.broadcasted_iota(jnp.int32, (1, LANES), 1).astype(F32)


def _head_lane_mask():
    return lax.broadcasted_iota(jnp.int32, (1, LANES), 1) < HEAD_DIM


def _head_rmsnorm(t, g):
    first = _head_lane_mask()
    sq = t * t
    ss0 = jnp.sum(jnp.where(first, sq, 0.0), axis=-1, keepdims=True)
    ss1 = jnp.sum(jnp.where(first, 0.0, sq), axis=-1, keepdims=True)
    inv = jnp.where(first, lax.rsqrt(ss0 / HEAD_DIM + RMS_EPS), lax.rsqrt(ss1 / HEAD_DIM + RMS_EPS))
    return t * inv * g


def _split_heads(t):
    first = _head_lane_mask()
    zero = jnp.zeros_like(t)
    return jnp.where(first, t, zero), jnp.where(first, zero, t)


def _rel_bucket_table(max_dist):
    n = np.arange(max_dist + 1)
    max_exact = REL_BUCKETS // 2
    nf = np.maximum(n, 1).astype(np.float64)
    large = max_exact + (np.log(nf / max_exact) / math.log(REL_MAX_DIST / max_exact)
                         * (REL_BUCKETS - max_exact)).astype(np.int64)
    large = np.minimum(large, REL_BUCKETS - 1)
    return np.where(n < max_exact, n, large).astype(np.int32)


def _moba_kernel(q_ref, kaug_ref, vaug_ref, kmean_ref, gq_ref, rb_ref, o_ref,
                 toep_ref, s_ref, mvec_ref, acc_ref, *, n_blk):
    b = pl.program_id(1)
    qi = pl.program_id(2)
    blk = ATT_BLOCK
    qt = q_ref.shape[1]
    halves = qt // blk
    first = _head_lane_mask()

    @pl.when(jnp.logical_and(b == 0, qi == 0))
    def _build_bias_tiles():
        for h in range(HEADS_PER_STEP):
            for d in range(n_blk):
                r = rb_ref[0, h:h + 1, (n_blk - d) * blk:(n_blk - d + 2) * blk]
                rolled = pltpu.roll(jnp.broadcast_to(r, (blk, 2 * blk)), blk + 1, 1, stride=1, stride_axis=0)
                toep_ref[h, d] = rolled[:, :blk]

    qn = _head_rmsnorm(q_ref[0].astype(F32), gq_ref[...])
    gate_rows = kmean_ref.shape[2]
    blk_row = lax.broadcasted_iota(jnp.int32, (gate_rows, qt), 0).astype(F32)
    own = (halves * qi + lax.shift_right_logical(lax.broadcasted_iota(jnp.int32, (gate_rows, qt), 1),
                                                 blk.bit_length() - 1)).astype(F32)
    q_aug = []
    for h, qh in enumerate((qn, pltpu.roll(qn, HEAD_DIM, 1))):
        qh = jnp.where(first, qh, 0.0)
        km = kmean_ref[0, h]
        km_hi, q_hi = km.astype(BF16), qh.astype(BF16)
        km_lo, q_lo = (km - km_hi.astype(F32)).astype(BF16), (qh - q_hi.astype(F32)).astype(BF16)
        g = (lax.dot_general(km_hi, q_hi, _NT, preferred_element_type=F32)
             + lax.dot_general(km_lo, q_hi, _NT, preferred_element_type=F32)
             + lax.dot_general(km_hi, q_lo, _NT, preferred_element_type=F32))
        g = jnp.where(blk_row < own, g, NEG_INF)
        keep = jnp.where(blk_row == own, 1.0, 0.0)
        for _ in range(MOBA_TOPK):
            top = jnp.max(g, axis=0, keepdims=True)
            is_top = jnp.logical_and(g == top, top > NEG_INF)
            idx = jnp.min(jnp.where(is_top, blk_row, float(gate_rows)), axis=0, keepdims=True)
            pick = blk_row == idx
            keep = jnp.where(pick, 1.0, keep)
            g = jnp.where(pick, NEG_INF, g)
        mask_t = jnp.where(keep > 0.0, 0.0, MASK_LOGIT)
        mask_logit = jnp.concatenate([jnp.zeros((HEAD_DIM, qt), F32), mask_t,
                                      jnp.zeros((LANES - HEAD_DIM - gate_rows, qt), F32)], axis=0).T
        q_aug.append(jnp.where(first, qh * (HEAD_DIM ** -0.5), mask_logit).astype(BF16))

    def lane_max(s):
        return jnp.maximum(s[:, :LANES], s[:, LANES:])

    def scores(h, t, dist):
        k_t = kaug_ref[0, h, pl.ds(pl.multiple_of(t * qt, qt), qt), :]
        s = lax.dot_general(q_aug[h], k_t, _NT, preferred_element_type=F32)
        tiles = []
        for kb in range(halves):
            rows = []
            for qb in range(halves):
                part = s[qb * blk:(qb + 1) * blk, kb * blk:(kb + 1) * blk]
                if isinstance(dist, int) and halves * dist + qb - kb < 0:
                    rows.append(part)
                else:
                    rows.append(part + toep_ref[h, halves * dist + qb - kb])
            tiles.append(jnp.concatenate(rows, axis=0))
        return tiles

    row = lax.broadcasted_iota(jnp.int32, (qt, blk), 0)
    col = lax.broadcasted_iota(jnp.int32, (qt, blk), 1)
    for h in range(HEADS_PER_STEP):
        top = None
        for kb, s in enumerate(scores(h, qi, 0)):
            s = jnp.where(col + kb * blk <= row, s, NEG_INF)
            s_ref[h, halves * qi + kb] = s
            top = lane_max(s) if top is None else jnp.maximum(top, lane_max(s))
        mvec_ref[h] = top

    def in_pairs(count, visit):
        def quad(p, carry):
            visit(4 * p, 4)
            return carry

        lax.fori_loop(0, lax.shift_right_logical(count, 2), quad, 0)

        @pl.when((count & 2) == 2)
        def _pair():
            visit(count & ~3, 2)

        @pl.when((count & 1) == 1)
        def _last():
            visit(count - 1, 1)

    def pass1(t0, n_tiles):
        for h in range(HEADS_PER_STEP):
            top = mvec_ref[h]
            for t in [t0 + u for u in range(n_tiles)]:
                for kb, s in enumerate(scores(h, t, qi - t)):
                    s_ref[h, halves * t + kb] = s
                    top = jnp.maximum(top, lane_max(s))
            mvec_ref[h] = top

    in_pairs(qi, pass1)

    for h in range(HEADS_PER_STEP):
        m = jnp.max(mvec_ref[h], axis=-1, keepdims=True)
        mvec_ref[h] = jnp.broadcast_to(m, (qt, LANES))
        acc_ref[h] = jnp.zeros((qt, LANES), F32)

    def pass2(t0, n_tiles):
        keys = pl.ds(pl.multiple_of(t0 * qt, qt), n_tiles * qt)
        for h in range(HEADS_PER_STEP):
            m = mvec_ref[h]
            parts = []
            for kb in range(n_tiles * halves):
                s = s_ref[h, halves * t0 + kb]
                parts += [jnp.exp(s[:, :LANES] - m), jnp.exp(s[:, LANES:] - m)]
            p = jnp.concatenate(parts, axis=1).astype(BF16)
            acc_ref[h] += jnp.dot(p, vaug_ref[0, h, keys, :], preferred_element_type=F32)

    in_pairs(qi + 1, pass2)

    out = [acc_ref[h] / acc_ref[h][:, HEAD_DIM:HEAD_DIM + 1] for h in range(HEADS_PER_STEP)]
    o_ref[0] = jnp.where(first, out[0], pltpu.roll(out[1], HEAD_DIM, 1)).astype(o_ref.dtype)


def _moba(proj, kaug, vaug, kmean, g_q, rel_rev):
    bsz, s, _ = proj.shape
    blk = ATT_BLOCK
    n_blk = s // blk
    n_hp = N_HEADS // HEADS_PER_STEP
    w = N_HEADS * HEAD_DIM
    qt = MOBA_Q_TILE
    per_pair = lambda arr: pl.BlockSpec((1, HEADS_PER_STEP) + arr.shape[2:], lambda hp, b, i: (b, hp, 0, 0))
    return pl.pallas_call(
        functools.partial(_moba_kernel, n_blk=n_blk),
        grid=(n_hp, bsz, s // qt),
        in_specs=[
            pl.BlockSpec((1, qt, LANES), lambda hp, b, i: (b, i, hp)),
            per_pair(kaug), per_pair(vaug), per_pair(kmean),
            pl.BlockSpec((1, LANES), lambda hp, b, i: (0, 0)),
            pl.BlockSpec((1, HEADS_PER_STEP, rel_rev.shape[-1]), lambda hp, b, i: (hp, 0, 0)),
        ],
        out_specs=pl.BlockSpec((1, qt, LANES), lambda hp, b, i: (b, i, hp)),
        out_shape=jax.ShapeDtypeStruct((bsz, s, w), BF16),
        scratch_shapes=[
            pltpu.VMEM((HEADS_PER_STEP, n_blk, blk, blk), F32),
            pltpu.VMEM((HEADS_PER_STEP, n_blk, qt, blk), F32),
            pltpu.VMEM((HEADS_PER_STEP, qt, LANES), F32),
            pltpu.VMEM((HEADS_PER_STEP, qt, LANES), F32),
        ],
        compiler_params=_params("arbitrary", "arbitrary", "arbitrary"),
        name="moba",
    )(proj, kaug, vaug, kmean, jnp.tile(g_q.reshape(1, HEAD_DIM), (1, HEADS_PER_STEP)), rel_rev)


def _sb_kernel(q_ref, k_ref, v_ref, o_ref, acc_ref, carry_ref):
    ti = pl.program_id(2)
    blk = ATT_BLOCK
    n_sub = q_ref.shape[1] // blk
    first = n_sub * ti
    row = lax.broadcasted_iota(jnp.int32, (blk, blk), 0)
    col = lax.broadcasted_iota(jnp.int32, (blk, blk), 1)
    past = col < row
    later = _indicator(row > col, BF16)

    q_heads = []
    for sub in range(n_sub):
        heads = _split_heads(q_ref[0, sub * blk:(sub + 1) * blk, :].astype(F32) * (HEAD_DIM ** -0.5))
        q_heads.append([t.astype(BF16) for t in heads])

    def walk(jobs, fresh):
        for sub, blocks in jobs:
            for h in range(HEADS_PER_STEP):
                carry = None if fresh else carry_ref[sub, h]
                total = None if fresh else acc_ref[sub, h]
                for j, diagonal in blocks:
                    start = j * blk if isinstance(j, int) else pl.multiple_of(j * blk, blk)
                    z = lax.dot_general(q_heads[sub][h], k_ref[0, pl.ds(start, blk), :], _NT,
                                        preferred_element_type=F32)
                    sp = jnp.maximum(z, 0.0) + jnp.log(1.0 + jnp.exp(-jnp.abs(z)))
                    masked = jnp.where(past, sp, 0.0) if diagonal else sp
                    hi = masked.astype(BF16)
                    lo = (masked - hi.astype(F32)).astype(BF16)
                    after = (jnp.dot(hi, later, preferred_element_type=F32)
                             + jnp.dot(lo, later, preferred_element_type=F32))
                    row_sum = after[:, 0:1] + masked[:, 0:1]
                    a = jnp.exp(z - sp - after)
                    if diagonal:
                        a = jnp.where(past, a, 0.0)
                    pv = jnp.dot(a.astype(BF16), v_ref[0, pl.ds(start, blk), :], preferred_element_type=F32)
                    if carry is not None:
                        pv = pv * jnp.exp(-carry)
                    total = pv if total is None else total + pv
                    carry = row_sum if carry is None else carry + row_sum
                carry_ref[sub, h] = carry
                acc_ref[sub, h] = total

    @pl.when(ti == 0)
    def _first_step():
        walk([(sub, [(sub, True)] + ([(sub - 1, False)] if sub else [])) for sub in range(n_sub)], True)

    @pl.when(ti > 0)
    def _own_and_previous():
        walk([(sub, [(first + sub, True), (first + sub - 1, False)]) for sub in range(n_sub)], True)

    def slowest_decay(subs):
        worst = functools.reduce(jnp.minimum, [carry_ref[sub, h] for sub in subs for h in range(HEADS_PER_STEP)])
        return jnp.min(worst)

    @pl.when(slowest_decay(range(n_sub)) < SB_EXIT_SUM)
    def _walk_further_back():
        for sub in range(n_sub):
            def cond(state):
                j, worst = state
                return jnp.logical_and(j >= 0, worst < SB_EXIT_SUM)

            def body(state, sub=sub):
                j, _ = state
                walk([(sub, [(j, False)])], False)
                return j - 1, slowest_decay([sub])

            lax.while_loop(cond, body, (first + sub - 2, slowest_decay([sub])))

    for sub in range(n_sub):
        o_ref[0, sub * blk:(sub + 1) * blk, :] = jnp.where(
            _head_lane_mask(), acc_ref[sub, 0], acc_ref[sub, 1]).astype(o_ref.dtype)


def _stick_breaking(proj):
    bsz, s, _ = proj.shape
    blk = ATT_BLOCK
    qt = SB_Q_BLOCKS * blk
    n_hp = N_HEADS // HEADS_PER_STEP
    w = N_HEADS * HEAD_DIM
    base = 3 * n_hp
    return pl.pallas_call(
        _sb_kernel,
        grid=(n_hp, bsz, s // qt),
        in_specs=[
            pl.BlockSpec((1, qt, LANES), lambda hp, b, i: (b, i, base + hp)),
            pl.BlockSpec((1, s, LANES), lambda hp, b, i: (b, 0, base + n_hp + hp)),
            pl.BlockSpec((1, s, LANES), lambda hp, b, i: (b, 0, base + 2 * n_hp + hp)),
        ],
        out_specs=pl.BlockSpec((1, qt, LANES), lambda hp, b, i: (b, i, hp)),
        out_shape=jax.ShapeDtypeStruct((bsz, s, w), BF16),
        scratch_shapes=[
            pltpu.VMEM((SB_Q_BLOCKS, HEADS_PER_STEP, blk, LANES), F32),
            pltpu.VMEM((SB_Q_BLOCKS, HEADS_PER_STEP, blk, 1), F32),
        ],
        compiler_params=_params("arbitrary", "arbitrary", "arbitrary"),
        name="stickbrk",
    )(proj, proj, proj)


def _pack_rows(t):
    n = t.shape[1] // 2
    bits = pltpu.bitcast(t.astype(BF16).astype(F32), U32)
    return bits[:, :n] | (bits[:, n:] >> 16)


def _unpack_rows(p):
    hi = pltpu.bitcast(p & jnp.uint32(0xFFFF0000), F32)
    lo = pltpu.bitcast(p << 16, F32)
    return jnp.concatenate([hi, lo], axis=1)


def _merge_kernel(x_ref, oa_ref, ob_ref, ga_ref, gb_ref, mod_ref, wbm_ref, wbs_ref, wo_ref, gf_ref,
                  wrh_ref, wrl_ref, br_ref, x1_ref, h2_ref, rt_ref, slot_t_ref, cnt_ref, before_ref, carry_ref):
    ma = jnp.dot(oa_ref[0], wbm_ref[...], preferred_element_type=F32)
    mb = jnp.dot(ob_ref[0], wbs_ref[...], preferred_element_type=F32)
    merged = (jax.nn.sigmoid(ga_ref[0].astype(F32)) * ma + jax.nn.sigmoid(gb_ref[0].astype(F32)) * mb)
    t = jnp.dot(merged.astype(BF16), wo_ref[...], preferred_element_type=F32)
    x1 = x_ref[0] + mod_ref[0, 2:3, :] * t
    x1_ref[0] = x1
    h2 = _rms_modulate(x1, gf_ref[...], mod_ref[0, 3:4, :], mod_ref[0, 4:5, :])
    hi = h2.astype(BF16)
    h2_ref[0] = hi
    lo = (h2 - hi.astype(F32)).astype(BF16)
    logits = (lax.dot_general(wrh_ref[...], hi, _NT, preferred_element_type=F32)
              + lax.dot_general(wrh_ref[...], lo, _NT, preferred_element_type=F32)
              + lax.dot_general(wrl_ref[...], hi, _NT, preferred_element_type=F32) + br_ref[...])
    first_tile = jnp.logical_and(pl.program_id(0) == 0, pl.program_id(1) == 0)
    _route_tile(logits, first_tile, rt_ref, slot_t_ref, cnt_ref, before_ref, carry_ref)


def _merge(x, o_a, o_b, proj, mod, wbm, wbs, wo, g_ffn, w_router, b_router):
    bsz, s, d = x.shape
    w = o_a.shape[-1]
    tm = TOKEN_TILE
    gate_blk = (proj.shape[-1] - 2 * d) // d
    const = lambda shape: pl.BlockSpec(shape, lambda b, i: (0,) * len(shape))
    tok = lambda width: pl.BlockSpec((1, tm, width), lambda b, i: (b, i, 0))
    wr_hi = w_router.T.astype(BF16)
    wr_lo = (w_router.T - wr_hi.astype(F32)).astype(BF16)
    per_b = s // tm
    tiles = bsz * per_b
    per_tile = lambda rows, cols: pl.BlockSpec((1, rows, cols), lambda b, i: (b * per_b + i, 0, 0))
    return pl.pallas_call(
        _merge_kernel,
        grid=(bsz, s // tm),
        in_specs=[
            tok(d), tok(w), tok(w),
            pl.BlockSpec((1, tm, d), lambda b, i: (b, i, gate_blk)),
            pl.BlockSpec((1, tm, d), lambda b, i: (b, i, gate_blk + 1)),
            pl.BlockSpec((1, N_MOD, d), lambda b, i: (b, 0, 0)),
            const((w, d)), const((w, d)), const((d, d)), const((1, d)),
            const((LANES, d)), const((LANES, d)), const((LANES, 1)),
        ],
        out_specs=[tok(d), tok(d), tok(ROUTE_COLS), per_tile(ROUTE_COLS, tm), per_tile(LANES, 1), per_tile(LANES, 1)],
        out_shape=[
            jax.ShapeDtypeStruct((bsz, s, d), F32),
            jax.ShapeDtypeStruct((bsz, s, d), BF16),
            jax.ShapeDtypeStruct((bsz, s, ROUTE_COLS), F32),
            jax.ShapeDtypeStruct((tiles, ROUTE_COLS, tm), F32),
            jax.ShapeDtypeStruct((tiles, LANES, 1), F32),
            jax.ShapeDtypeStruct((tiles, LANES, 1), F32),
        ],
        scratch_shapes=[pltpu.VMEM((LANES, 1), F32)],
        compiler_params=_params("arbitrary", "arbitrary"),
        name="merge",
    )(x, o_a, o_b, proj, proj, mod, wbm, wbs, wo, g_ffn.reshape(1, d), wr_hi, wr_lo, b_router)


ROUTE_COLS = 8
SEG_ALIGN = 8


def _local_rows(tm):
    return 2 * tm + N_EXPERTS * SEG_ALIGN


def _route_tile(lg_t, first_tile, rt_ref, slot_t_ref, cnt_ref, before_ref, carry_ref):
    tm = lg_t.shape[1]
    rows = -(-(N_GROUPS + N_EXPERTS) // 8) * 8
    lg = lg_t[:rows]
    row = lax.broadcasted_iota(jnp.int32, (rows, tm), 0).astype(F32)

    @pl.when(first_tile)
    def _init():
        carry_ref[...] = jnp.zeros_like(carry_ref)

    def first_max(vals):
        top = jnp.max(vals, axis=0, keepdims=True)
        idx = jnp.min(jnp.where(vals == top, row, float(rows)), axis=0, keepdims=True)
        return top, idx

    g_logit = jnp.where(row < N_GROUPS, lg, NEG_INF)
    g_top, g_idx = first_max(g_logit)
    g_w = 1.0 / jnp.sum(jnp.exp(g_logit - g_top), axis=0, keepdims=True)
    lo = N_GROUPS + g_idx * EXPERTS_PER_GROUP
    e_logit = jnp.where(jnp.logical_and(row >= lo, row < lo + EXPERTS_PER_GROUP), lg, NEG_INF)
    e_top0, row0 = first_max(e_logit)
    e_top1, row1 = first_max(jnp.where(row == row0, NEG_INF, e_logit))
    z = jnp.exp(e_top1 - e_top0)
    w0 = g_w / (1.0 + z)
    w1 = g_w * z / (1.0 + z)

    hot = jnp.logical_or(row == row0, row == row1)
    t_src = lax.broadcasted_iota(jnp.int32, (tm, tm), 0)
    t_dst = lax.broadcasted_iota(jnp.int32, (tm, tm), 1)
    earlier = jnp.dot(_indicator(hot, BF16), _indicator(t_src < t_dst, BF16), preferred_element_type=F32)
    cnt = jnp.sum(_indicator(hot, F32), axis=1, keepdims=True)
    chunks = jnp.ceil(cnt * (1.0 / SEG_ALIGN))
    spare = jnp.zeros((LANES - rows, LANES), F32)
    e_dst = lax.broadcasted_iota(jnp.int32, (LANES, LANES), 0)
    e_src = lax.broadcasted_iota(jnp.int32, (LANES, LANES), 1)
    chunks_wide = jnp.concatenate([jnp.broadcast_to(chunks, (rows, LANES)), spare], axis=0).astype(BF16)
    seg_start = SEG_ALIGN * jnp.dot(_indicator(e_src < e_dst, BF16), chunks_wide,
                                    preferred_element_type=F32)[:rows, 0:1]
    local = seg_start + earlier
    slot0 = jnp.sum(jnp.where(row == row0, local, 0.0), axis=0, keepdims=True)
    slot1 = jnp.sum(jnp.where(row == row1, local, 0.0), axis=0, keepdims=True)
    seg_rows = jnp.concatenate([SEG_ALIGN * chunks, spare[:, 0:1]], axis=0)
    cnt_ref[0] = seg_rows
    before_ref[0] = carry_ref[...]
    carry_ref[...] += seg_rows

    out_row = lax.broadcasted_iota(jnp.int32, (ROUTE_COLS, tm), 0)
    packed = jnp.zeros((ROUTE_COLS, tm), F32)
    for c, v in enumerate([slot0, slot1, w0, w1]):
        packed = jnp.where(out_row == c, v, packed)
    slot_t_ref[0] = packed
    rt_ref[0] = jnp.concatenate([packed, jnp.zeros((LANES - ROUTE_COLS, tm), F32)], axis=0).T[:, :ROUTE_COLS]


def _for_each_chunk(tile, n_ref, rows_ref, visit):
    max_chunks = rows_ref.shape[0] // n_ref.shape[0]

    def per_chunk(c, carry):
        visit(pl.multiple_of(c * SEG_ALIGN, SEG_ALIGN), pl.multiple_of(rows_ref[tile * max_chunks + c], SEG_ALIGN))
        return carry

    lax.fori_loop(0, n_ref[tile], per_chunk, 0)


def _dispatch_kernel(n_ref, rows_ref, tail_row_ref, tail_n_ref, n_used_ref, slot_t_ref, h_ref, xs_ref,
                     loc_ref, zero_ref, sems, zero_sem):
    tile = pl.program_id(0)
    n_loc, tm = loc_ref.shape[1], h_ref.shape[0]
    buf = tile & 1

    slot = lax.broadcasted_iota(jnp.int32, (n_loc, tm), 0).astype(F32)
    place = jnp.logical_or(slot == slot_t_ref[0, 0:1, :], slot == slot_t_ref[0, 1:2, :])
    loc_ref[buf] = _pack_rows(jnp.dot(_indicator(place, BF16), h_ref[...], preferred_element_type=F32))

    def chunk_copy(b):
        return lambda local, row: pltpu.make_async_copy(
            loc_ref.at[b, pl.ds(local, SEG_ALIGN), :], xs_ref.at[pl.ds(row, SEG_ALIGN), :], sems.at[b])

    _for_each_chunk(tile, n_ref, rows_ref, lambda l, r: chunk_copy(buf)(l, r).start())

    @pl.when(tile > 0)
    def _previous_done():
        _for_each_chunk(tile - 1, n_ref, rows_ref, lambda l, r: chunk_copy(1 - buf)(l, r).wait())

    @pl.when(tile == pl.num_programs(0) - 1)
    def _last_done():
        _for_each_chunk(tile, n_ref, rows_ref, lambda l, r: chunk_copy(buf)(l, r).wait())
        zero_ref[...] = jnp.zeros_like(zero_ref)
        block_rows = zero_ref.shape[0]

        def fill(action):
            def per_expert(e, carry):
                def per_chunk(c, inner):
                    row = pl.multiple_of(tail_row_ref[e] + c * SEG_ALIGN, SEG_ALIGN)
                    action(pltpu.make_async_copy(zero_ref.at[pl.ds(0, SEG_ALIGN), :],
                                                 xs_ref.at[pl.ds(row, SEG_ALIGN), :], zero_sem))
                    return inner

                lax.fori_loop(0, tail_n_ref[e], per_chunk, 0)
                return carry

            lax.fori_loop(0, N_EXPERTS, per_expert, 0)

            def per_block(blk, carry):
                row = pl.multiple_of(blk * block_rows, block_rows)
                action(pltpu.make_async_copy(zero_ref, xs_ref.at[pl.ds(row, block_rows), :], zero_sem))
                return carry

            lax.fori_loop(n_used_ref[0], xs_ref.shape[0] // block_rows, per_block, 0)

        fill(lambda copy: copy.start())
        fill(lambda copy: copy.wait())


def _dispatch(n_chunks, chunk_rows, tail_row, tail_chunks, n_used, slot_t, h2, n_rows):
    n, d = h2.shape
    tm = TOKEN_TILE
    grid_spec = pltpu.PrefetchScalarGridSpec(
        num_scalar_prefetch=5,
        grid=(n // tm,),
        in_specs=[
            pl.BlockSpec((1, ROUTE_COLS, tm), lambda i, *_: (i, 0, 0)),
            pl.BlockSpec((tm, d), lambda i, *_: (i, 0)),
        ],
        out_specs=pl.BlockSpec(memory_space=pl.ANY),
        scratch_shapes=[pltpu.VMEM((2, _local_rows(tm), d // 2), U32), pltpu.VMEM((ROW_BLOCK, d // 2), U32),
                        pltpu.SemaphoreType.DMA((2,)), pltpu.SemaphoreType.DMA(())],
    )
    return pl.pallas_call(
        _dispatch_kernel,
        grid_spec=grid_spec,
        out_shape=jax.ShapeDtypeStruct((n_rows, d // 2), U32),
        compiler_params=_params("arbitrary"),
        name="dispatch",
    )(n_chunks, chunk_rows, tail_row, tail_chunks, n_used, slot_t, h2)


def _expert_kernel(row0_ref, nblk_ref, n_used_ref, xs_ref, wg_ref, wu_ref, wd_ref, ys_ref,
                   wg_bf, wu_bf, wd_bf, x_buf, y_buf, in_sems, out_sems):
    e = pl.program_id(0)
    rb = x_buf.shape[1]
    n_blk = nblk_ref[e]
    base = row0_ref[e]

    def rows(j):
        return pl.ds(pl.multiple_of(base + j * rb, rb), rb)

    def x_copy(j, slot):
        return pltpu.make_async_copy(xs_ref.at[rows(j), :], x_buf.at[slot], in_sems.at[slot])

    def y_copy(j, slot):
        return pltpu.make_async_copy(y_buf.at[slot], ys_ref.at[rows(j), :], out_sems.at[slot])

    @pl.when(n_blk > 0)
    def _first_rows():
        x_copy(0, 0).start(priority=1)

    wg_bf[...] = wg_ref[0].astype(BF16)
    wu_bf[...] = wu_ref[0].astype(BF16)
    wd_bf[...] = wd_ref[0].astype(BF16)

    def row_block(j, carry):
        slot = j & 1
        x_copy(j, slot).wait()

        @pl.when(j + 1 < n_blk)
        def _next_rows():
            x_copy(j + 1, 1 - slot).start(priority=1)

        @pl.when(j >= 2)
        def _slot_free():
            y_copy(j - 2, slot).wait()

        xb = _unpack_rows(x_buf[slot]).astype(BF16)
        g = jnp.dot(xb, wg_bf[...], preferred_element_type=F32)
        u = jnp.dot(xb, wu_bf[...], preferred_element_type=F32)
        hid = (g * jax.nn.sigmoid(g) * u).astype(BF16)
        y_buf[slot] = _pack_rows(jnp.dot(hid, wd_bf[...], preferred_element_type=F32))
        y_copy(j, slot).start()
        return carry

    lax.fori_loop(0, n_blk, row_block, 0)

    @pl.when(n_blk >= 2)
    def _drain_older():
        y_copy(n_blk - 2, n_blk & 1).wait()

    @pl.when(n_blk >= 1)
    def _drain_last():
        y_copy(n_blk - 1, (n_blk - 1) & 1).wait()

    @pl.when(e == pl.num_programs(0) - 1)
    def _unused_blocks():
        y_buf[0] = jnp.zeros((rb, y_buf.shape[2]), y_buf.dtype)

        def fill(action):
            def per_block(blk, carry):
                dst = ys_ref.at[pl.ds(pl.multiple_of(blk * rb, rb), rb), :]
                action(pltpu.make_async_copy(y_buf.at[0], dst, out_sems.at[0]))
                return carry

            lax.fori_loop(n_used_ref[0], ys_ref.shape[0] // rb, per_block, 0)

        fill(lambda copy: copy.start())
        fill(lambda copy: copy.wait())


def _experts(row0, n_blk, n_used, xs, wg, wu, wd):
    n_rows, half = xs.shape
    n_experts, d, de = wg.shape
    grid_spec = pltpu.PrefetchScalarGridSpec(
        num_scalar_prefetch=3,
        grid=(n_experts,),
        in_specs=[
            pl.BlockSpec(memory_space=pl.ANY),
            pl.BlockSpec((1, d, de), lambda e, *_: (e, 0, 0)),
            pl.BlockSpec((1, d, de), lambda e, *_: (e, 0, 0)),
            pl.BlockSpec((1, de, d), lambda e, *_: (e, 0, 0)),
        ],
        out_specs=pl.BlockSpec(memory_space=pl.ANY),
        scratch_shapes=[pltpu.VMEM((d, de), BF16), pltpu.VMEM((d, de), BF16), pltpu.VMEM((de, d), BF16),
                        pltpu.VMEM((2, ROW_BLOCK, half), U32), pltpu.VMEM((2, ROW_BLOCK, half), U32),
                        pltpu.SemaphoreType.DMA((2,)), pltpu.SemaphoreType.DMA((2,))],
    )
    return pl.pallas_call(
        _expert_kernel,
        grid_spec=grid_spec,
        out_shape=jax.ShapeDtypeStruct((n_rows, half), U32),
        compiler_params=_params("arbitrary"),
        name="experts",
    )(row0, n_blk, n_used, xs, wg, wu, wd)


def _combine_kernel(n_ref, rows_ref, x1_ref, mod_ref, rt_ref, ys_ref, o_ref, loc_ref, sems):
    tm = x1_ref.shape[1]
    n_loc = loc_ref.shape[1]
    tile = pl.program_id(0) * pl.num_programs(1) + pl.program_id(1)
    n_tiles = pl.num_programs(0) * pl.num_programs(1)
    buf = tile & 1

    def chunk_copy(b):
        return lambda local, row: pltpu.make_async_copy(
            ys_ref.at[pl.ds(row, SEG_ALIGN), :], loc_ref.at[b, pl.ds(local, SEG_ALIGN), :], sems.at[b])

    @pl.when(tile == 0)
    def _init():
        loc_ref[...] = jnp.zeros_like(loc_ref)
        _for_each_chunk(tile, n_ref, rows_ref, lambda l, r: chunk_copy(buf)(l, r).start())

    @pl.when(tile + 1 < n_tiles)
    def _fetch_next():
        _for_each_chunk(tile + 1, n_ref, rows_ref, lambda l, r: chunk_copy(1 - buf)(l, r).start())

    _for_each_chunk(tile, n_ref, rows_ref, lambda l, r: chunk_copy(buf)(l, r).wait())

    y = _unpack_rows(loc_ref[buf]).astype(BF16)
    rt = rt_ref[0]
    slot = lax.broadcasted_iota(jnp.int32, (tm, n_loc), 1).astype(F32)
    mix = jnp.where(slot == rt[:, 0:1], rt[:, 2:3], jnp.where(slot == rt[:, 1:2], rt[:, 3:4], 0.0)).astype(BF16)
    o_ref[0] = x1_ref[0] + mod_ref[0, 5:6, :] * jnp.dot(mix, y, preferred_element_type=F32)


def _combine(n_chunks, chunk_rows, x1, mod, rt, ys):
    bsz, s, d = x1.shape
    tm = TOKEN_TILE
    grid_spec = pltpu.PrefetchScalarGridSpec(
        num_scalar_prefetch=2,
        grid=(bsz, s // tm),
        in_specs=[
            pl.BlockSpec((1, tm, d), lambda b, i, c, r: (b, i, 0)),
            pl.BlockSpec((1, N_MOD, d), lambda b, i, c, r: (b, 0, 0)),
            pl.BlockSpec((1, tm, ROUTE_COLS), lambda b, i, c, r: (b, i, 0)),
            pl.BlockSpec(memory_space=pl.ANY),
        ],
        out_specs=pl.BlockSpec((1, tm, d), lambda b, i, c, r: (b, i, 0)),
        scratch_shapes=[pltpu.VMEM((2, _local_rows(tm), d // 2), U32), pltpu.SemaphoreType.DMA((2,))],
    )
    return pl.pallas_call(
        _combine_kernel,
        grid_spec=grid_spec,
        out_shape=jax.ShapeDtypeStruct((bsz, s, d), F32),
        compiler_params=_params("arbitrary", "arbitrary"),
        name="combine",
    )(n_chunks, chunk_rows, x1, mod, rt, ys)


def _layer(x, mod, g_mix, w_in, g_q, g_k, rel_bias, w_br_moba, w_br_sb, w_out, g_ffn,
           w_rg, b_rg, w_re, b_re, w_gate, w_up, w_down):
    bsz, s, d = x.shape
    n = bsz * s
    n_blk = s // ATT_BLOCK

    proj, kaug, vaug, kmean = _inproj(x, mod, g_mix, w_in.astype(BF16), g_k)

    top = (n_blk + 1) * ATT_BLOCK - 1
    dist = np.maximum(top - np.arange((n_blk + 2) * ATT_BLOCK), 0)
    rel_rev = rel_bias[:, _rel_bucket_table(top)[dist]].reshape(N_HEADS // HEADS_PER_STEP, HEADS_PER_STEP, -1)

    o_a = _moba(proj, kaug, vaug, kmean, g_q, rel_rev)
    o_b = _stick_breaking(proj)

    pad = LANES - N_GROUPS - N_EXPERTS
    w_router = jnp.concatenate([w_rg, w_re, jnp.zeros((d, pad), F32)], axis=1)
    b_router = jnp.concatenate([b_rg, b_re, jnp.zeros((pad,), F32)]).reshape(LANES, 1)
    x1, h2, rt, slot_t, tile_cnt, tile_before = _merge(x, o_a, o_b, proj, mod, w_br_moba.astype(BF16),
                                                       w_br_sb.astype(BF16), w_out.astype(BF16), g_ffn,
                                                       w_router, b_router)

    experts = slice(N_GROUPS, N_GROUPS + N_EXPERTS)
    cnt = tile_cnt[:, experts, 0].astype(jnp.int32)
    total = jnp.sum(cnt, axis=0)
    padded = (total + ROW_BLOCK - 1) // ROW_BLOCK * ROW_BLOCK
    pend = jnp.cumsum(padded)
    pstart = pend - padded
    max_rows = 2 * n + cnt.shape[0] * N_EXPERTS * (SEG_ALIGN - 1) + N_EXPERTS * (ROW_BLOCK - 1)
    n_blocks = -(-max_rows // ROW_BLOCK)
    n_used = (pend[-1] // ROW_BLOCK).astype(jnp.int32).reshape(1)
    seg_row0 = pstart[None, :] + tile_before[:, experts, 0].astype(jnp.int32)
    chunks = cnt // SEG_ALIGN
    chunk_end = jnp.cumsum(chunks, axis=1)
    chunk_start = (chunk_end - chunks)[:, None, :]
    c = jnp.arange(_local_rows(TOKEN_TILE) // SEG_ALIGN, dtype=jnp.int32)[None, :, None]
    inside = jnp.logical_and(c >= chunk_start, c < chunk_end[:, None, :])
    chunk_rows = jnp.sum(jnp.where(inside, seg_row0[:, None, :] + SEG_ALIGN * (c - chunk_start), 0), axis=2).reshape(-1)
    n_chunks = chunk_end[:, -1]

    xs = _dispatch(n_chunks, chunk_rows, pstart + total, (padded - total) // SEG_ALIGN, n_used, slot_t,
                   h2.reshape(n, d), n_blocks * ROW_BLOCK)
    ys = _experts(pstart, padded // ROW_BLOCK, n_used, xs, w_gate, w_up, w_down)
    return _combine(n_chunks, chunk_rows, x1, mod, rt, ys)


def kernel(x, c, w_ada, b_ada, g_mix, w_in, g_q, g_k, rel_bias, w_br_moba, w_br_sb, w_out, g_ffn,
           w_rg, b_rg, w_re, b_re, w_gate, w_up, w_down):
    bsz, s, d = x.shape
    assert s % TOKEN_TILE == 0 and s % MOBA_Q_TILE == 0 and d % (2 * LANES) == 0
    for l in range(w_ada.shape[0]):
        mod = _adaln(c, w_ada[l], b_ada[l]).reshape(bsz, N_MOD, d)
        x = _layer(x, mod, g_mix[l], w_in[l], g_q[l], g_k[l], rel_bias, w_br_moba[l], w_br_sb[l], w_out[l],
                   g_ffn[l], w_rg[l], b_rg[l], w_re[l], b_re[l], w_gate[l], w_up[l], w_down[l])
    return x
```

```python
import functools
import math

import numpy as np
import jax
import jax.numpy as jnp
from jax import lax
from jax.experimental import pallas as pl
from jax.experimental.pallas import tpu as pltpu

F32 = jnp.float32
BF16 = jnp.bfloat16
U32 = jnp.uint32
HIGHEST = lax.Precision.HIGHEST

LANES = 128
VMEM_LIMIT_BYTES = 56 * 1024 * 1024

HEAD_DIM = 64
N_HEADS = 8
HEADS_PER_STEP = LANES // HEAD_DIM
ATT_BLOCK = 256
MOBA_Q_TILE = 2 * ATT_BLOCK
SB_Q_BLOCKS = 2
MOBA_TOPK = 3
REL_BUCKETS = 32
REL_MAX_DIST = 1024
N_GROUPS = 4
EXPERTS_PER_GROUP = 8
N_EXPERTS = N_GROUPS * EXPERTS_PER_GROUP
N_MOD = 6
RMS_EPS = 1e-6
ROW_BLOCK = 512
TOKEN_TILE = 512
NEG_INF = float("-inf")
MASK_LOGIT = -1e30
SB_EXIT_SUM = 110.0

_NT = (((1,), (1,)), ((), ()))


def _params(*semantics):
    return pltpu.CompilerParams(dimension_semantics=semantics, vmem_limit_bytes=VMEM_LIMIT_BYTES)


def _adaln_kernel(c_ref, w_ref, b_ref, o_ref):
    c = c_ref[...]
    s = c * jax.nn.sigmoid(c)
    o_ref[...] = jnp.dot(s, w_ref[...], preferred_element_type=F32, precision=HIGHEST) + b_ref[...]


def _adaln(c, w, b):
    bsz, d = c.shape
    n = w.shape[1]
    tn = 1536
    return pl.pallas_call(
        _adaln_kernel,
        grid=(n // tn,),
        in_specs=[
            pl.BlockSpec((bsz, d), lambda j: (0, 0)),
            pl.BlockSpec((d, tn), lambda j: (0, j)),
            pl.BlockSpec((1, tn), lambda j: (0, j)),
        ],
        out_specs=pl.BlockSpec((bsz, tn), lambda j: (0, j)),
        out_shape=jax.ShapeDtypeStruct((bsz, n), F32),
        compiler_params=_params("arbitrary"),
        name="adaln",
    )(c, w, b.reshape(1, n))


def _rms_modulate(x, g, shift, scale):
    y = x * lax.rsqrt(jnp.mean(x * x, axis=-1, keepdims=True) + RMS_EPS) * g
    return y * (1.0 + scale) + shift


def _inproj_kernel(x_ref, mod_ref, g_ref, w_ref, gk_ref, o_ref, kaug_ref, vaug_ref, kmean_ref, *, col_chunk):
    i = pl.program_id(1)
    tm = x_ref.shape[1]
    h = _rms_modulate(x_ref[0], g_ref[...], mod_ref[0, 0:1, :], mod_ref[0, 1:2, :]).astype(BF16)

    w_moba = N_HEADS * HEAD_DIM
    blocks = tm // ATT_BLOCK
    first = _head_lane_mask()
    row_blk = blocks * i + lax.shift_right_logical(lax.broadcasted_iota(jnp.int32, (tm, LANES), 0),
                                                   ATT_BLOCK.bit_length() - 1)
    tag = jnp.where(_lane_index() - float(HEAD_DIM) == row_blk.astype(F32), 1.0, 0.0)
    mean_row = lax.broadcasted_iota(jnp.int32, kmean_ref.shape[2:], 0)

    @pl.when(i == 0)
    def _init():
        kmean_ref[...] = jnp.zeros_like(kmean_ref)

    for n in range(w_ref.shape[1] // col_chunk):
        cols = slice(n * col_chunk, (n + 1) * col_chunk)
        r = jnp.dot(h, w_ref[:, cols], preferred_element_type=F32)
        o_ref[0, :, cols] = r.astype(BF16)
        for c in range(col_chunk // LANES):
            col0 = n * col_chunk + c * LANES
            part = r[:, c * LANES:(c + 1) * LANES]
            if w_moba <= col0 < 2 * w_moba:
                kn = _head_rmsnorm(part, gk_ref[...])
                for hh, kh in enumerate((kn, pltpu.roll(kn, HEAD_DIM, 1))):
                    head = (col0 - w_moba) // HEAD_DIM + hh
                    kaug_ref[0, head] = jnp.where(first, kh, tag).astype(BF16)
                    means = kmean_ref[0, head]
                    for jb in range(blocks):
                        mean = jnp.mean(kh[jb * ATT_BLOCK:(jb + 1) * ATT_BLOCK], axis=0, keepdims=True)
                        means = jnp.where(mean_row == blocks * i + jb, jnp.where(first, mean, 0.0), means)
                    kmean_ref[0, head] = means
            elif 2 * w_moba <= col0 < 3 * w_moba:
                for hh, vh in enumerate((part, pltpu.roll(part, HEAD_DIM, 1))):
                    head = (col0 - 2 * w_moba) // HEAD_DIM + hh
                    vaug_ref[0, head] = jnp.where(first, vh, 1.0).astype(BF16)


def _inproj(x, mod, g, w_bf16, g_k):
    bsz, s, d = x.shape
    n = w_bf16.shape[1]
    tm = TOKEN_TILE
    n_blk = s // ATT_BLOCK
    per_head = lambda rows: pl.BlockSpec((1, N_HEADS, rows, LANES), lambda b, i: (b, 0, i, 0))
    return pl.pallas_call(
        functools.partial(_inproj_kernel, col_chunk=1024),
        grid=(bsz, s // tm),
        in_specs=[
            pl.BlockSpec((1, tm, d), lambda b, i: (b, i, 0)),
            pl.BlockSpec((1, N_MOD, d), lambda b, i: (b, 0, 0)),
            pl.BlockSpec((1, d), lambda b, i: (0, 0)),
            pl.BlockSpec((d, n), lambda b, i: (0, 0)),
            pl.BlockSpec((1, LANES), lambda b, i: (0, 0)),
        ],
        out_specs=[
            pl.BlockSpec((1, tm, n), lambda b, i: (b, i, 0)),
            per_head(tm),
            per_head(tm),
            pl.BlockSpec((1, N_HEADS, -(-n_blk // 8) * 8, LANES), lambda b, i: (b, 0, 0, 0)),
        ],
        out_shape=[
            jax.ShapeDtypeStruct((bsz, s, n), BF16),
            jax.ShapeDtypeStruct((bsz, N_HEADS, s, LANES), BF16),
            jax.ShapeDtypeStruct((bsz, N_HEADS, s, LANES), BF16),
            jax.ShapeDtypeStruct((bsz, N_HEADS, -(-n_blk // 8) * 8, LANES), F32),
        ],
        compiler_params=_params("arbitrary", "arbitrary"),
        name="inproj",
    )(x, mod, g.reshape(1, d), w_bf16, jnp.tile(g_k.reshape(1, HEAD_DIM), (1, HEADS_PER_STEP)))


def _indicator(cond, dtype):
    return jnp.where(cond, 1.0, 0.0).astype(dtype)


def _lane_index():
    return lax.broadcasted_iota(jnp.int32, (1, LANES), 1).astype(F32)


def _head_lane_mask():
    return lax.broadcasted_iota(jnp.int32, (1, LANES), 1) < HEAD_DIM


def _head_rmsnorm(t, g):
    first = _head_lane_mask()
    sq = t * t
    ss0 = jnp.sum(jnp.where(first, sq, 0.0), axis=-1, keepdims=True)
    ss1 = jnp.sum(jnp.where(first, 0.0, sq), axis=-1, keepdims=True)
    inv = jnp.where(first, lax.rsqrt(ss0 / HEAD_DIM + RMS_EPS), lax.rsqrt(ss1 / HEAD_DIM + RMS_EPS))
    return t * inv * g


def _split_heads(t):
    first = _head_lane_mask()
    zero = jnp.zeros_like(t)
    return jnp.where(first, t, zero), jnp.where(first, zero, t)


def _rel_bucket_table(max_dist):
    n = np.arange(max_dist + 1)
    max_exact = REL_BUCKETS // 2
    nf = np.maximum(n, 1).astype(np.float64)
    large = max_exact + (np.log(nf / max_exact) / math.log(REL_MAX_DIST / max_exact)
                         * (REL_BUCKETS - max_exact)).astype(np.int64)
    large = np.minimum(large, REL_BUCKETS - 1)
    return np.where(n < max_exact, n, large).astype(np.int32)


def _moba_kernel(q_ref, kaug_ref, vaug_ref, kmean_ref, gq_ref, rb_ref, o_ref,
                 toep_ref, s_ref, mvec_ref, acc_ref, *, n_blk):
    b = pl.program_id(1)
    qi = pl.program_id(2)
    blk = ATT_BLOCK
    qt = q_ref.shape[1]
    halves = qt // blk
    first = _head_lane_mask()

    @pl.when(jnp.logical_and(b == 0, qi == 0))
    def _build_bias_tiles():
        for h in range(HEADS_PER_STEP):
            for d in range(n_blk):
                r = rb_ref[0, h:h + 1, (n_blk - d) * blk:(n_blk - d + 2) * blk]
                rolled = pltpu.roll(jnp.broadcast_to(r, (blk, 2 * blk)), blk + 1, 1, stride=1, stride_axis=0)
                toep_ref[h, d] = rolled[:, :blk]

    qn = _head_rmsnorm(q_ref[0].astype(F32), gq_ref[...])
    gate_rows = kmean_ref.shape[2]
    blk_row = lax.broadcasted_iota(jnp.int32, (gate_rows, qt), 0).astype(F32)
    own = (halves * qi + lax.shift_right_logical(lax.broadcasted_iota(jnp.int32, (gate_rows, qt), 1),
                                                 blk.bit_length() - 1)).astype(F32)
    q_aug = []
    for h, qh in enumerate((qn, pltpu.roll(qn, HEAD_DIM, 1))):
        qh = jnp.where(first, qh, 0.0)
        km = kmean_ref[0, h]
        km_hi, q_hi = km.astype(BF16), qh.astype(BF16)
        km_lo, q_lo = (km - km_hi.astype(F32)).astype(BF16), (qh - q_hi.astype(F32)).astype(BF16)
        g = (lax.dot_general(km_hi, q_hi, _NT, preferred_element_type=F32)
             + lax.dot_general(km_lo, q_hi, _NT, preferred_element_type=F32)
             + lax.dot_general(km_hi, q_lo, _NT, preferred_element_type=F32))
        g = jnp.where(blk_row < own, g, NEG_INF)
        keep = jnp.where(blk_row == own, 1.0, 0.0)
        for _ in range(MOBA_TOPK):
            top = jnp.max(g, axis=0, keepdims=True)
            is_top = jnp.logical_and(g == top, top > NEG_INF)
            idx = jnp.min(jnp.where(is_top, blk_row, float(gate_rows)), axis=0, keepdims=True)
            pick = blk_row == idx
            keep = jnp.where(pick, 1.0, keep)
            g = jnp.where(pick, NEG_INF, g)
        mask_t = jnp.where(keep > 0.0, 0.0, MASK_LOGIT)
        mask_logit = jnp.concatenate([jnp.zeros((HEAD_DIM, qt), F32), mask_t,
                                      jnp.zeros((LANES - HEAD_DIM - gate_rows, qt), F32)], axis=0).T
        q_aug.append(jnp.where(first, qh * (HEAD_DIM ** -0.5), mask_logit).astype(BF16))

    def lane_max(s):
        return jnp.maximum(s[:, :LANES], s[:, LANES:])

    def scores(h, t, dist):
        k_t = kaug_ref[0, h, pl.ds(pl.multiple_of(t * qt, qt), qt), :]
        s = lax.dot_general(q_aug[h], k_t, _NT, preferred_element_type=F32)
        tiles = []
        for kb in range(halves):
            rows = []
            for qb in range(halves):
                part = s[qb * blk:(qb + 1) * blk, kb * blk:(kb + 1) * blk]
                if isinstance(dist, int) and halves * dist + qb - kb < 0:
                    rows.append(part)
                else:
                    rows.append(part + toep_ref[h, halves * dist + qb - kb])
            tiles.append(jnp.concatenate(rows, axis=0))
        return tiles

    row = lax.broadcasted_iota(jnp.int32, (qt, blk), 0)
    col = lax.broadcasted_iota(jnp.int32, (qt, blk), 1)
    for h in range(HEADS_PER_STEP):
        top = None
        for kb, s in enumerate(scores(h, qi, 0)):
            s = jnp.where(col + kb * blk <= row, s, NEG_INF)
            s_ref[h, halves * qi + kb] = s
            top = lane_max(s) if top is None else jnp.maximum(top, lane_max(s))
        mvec_ref[h] = top

    def in_pairs(count, visit):
        def quad(p, carry):
            visit(4 * p, 4)
            return carry

        lax.fori_loop(0, lax.shift_right_logical(count, 2), quad, 0)

        @pl.when((count & 2) == 2)
        def _pair():
            visit(count & ~3, 2)

        @pl.when((count & 1) == 1)
        def _last():
            visit(count - 1, 1)

    def pass1(t0, n_tiles):
        for h in range(HEADS_PER_STEP):
            top = mvec_ref[h]
            for t in [t0 + u for u in range(n_tiles)]:
                for kb, s in enumerate(scores(h, t, qi - t)):
                    s_ref[h, halves * t + kb] = s
                    top = jnp.maximum(top, lane_max(s))
            mvec_ref[h] = top

    in_pairs(qi, pass1)

    for h in range(HEADS_PER_STEP):
        m = jnp.max(mvec_ref[h], axis=-1, keepdims=True)
        mvec_ref[h] = jnp.broadcast_to(m, (qt, LANES))
        acc_ref[h] = jnp.zeros((qt, LANES), F32)

    def pass2(t0, n_tiles):
        keys = pl.ds(pl.multiple_of(t0 * qt, qt), n_tiles * qt)
        for h in range(HEADS_PER_STEP):
            m = mvec_ref[h]
            parts = []
            for kb in range(n_tiles * halves):
                s = s_ref[h, halves * t0 + kb]
                parts += [jnp.exp(s[:, :LANES] - m), jnp.exp(s[:, LANES:] - m)]
            p = jnp.concatenate(parts, axis=1).astype(BF16)
            acc_ref[h] += jnp.dot(p, vaug_ref[0, h, keys, :], preferred_element_type=F32)

    in_pairs(qi + 1, pass2)

    out = [acc_ref[h] / acc_ref[h][:, HEAD_DIM:HEAD_DIM + 1] for h in range(HEADS_PER_STEP)]
    o_ref[0] = jnp.where(first, out[0], pltpu.roll(out[1], HEAD_DIM, 1)).astype(o_ref.dtype)


def _moba(proj, kaug, vaug, kmean, g_q, rel_rev):
    bsz, s, _ = proj.shape
    blk = ATT_BLOCK
    n_blk = s // blk
    n_hp = N_HEADS // HEADS_PER_STEP
    w = N_HEADS * HEAD_DIM
    qt = MOBA_Q_TILE
    per_pair = lambda arr: pl.BlockSpec((1, HEADS_PER_STEP) + arr.shape[2:], lambda hp, b, i: (b, hp, 0, 0))
    return pl.pallas_call(
        functools.partial(_moba_kernel, n_blk=n_blk),
        grid=(n_hp, bsz, s // qt),
        in_specs=[
            pl.BlockSpec((1, qt, LANES), lambda hp, b, i: (b, i, hp)),
            per_pair(kaug), per_pair(vaug), per_pair(kmean),
            pl.BlockSpec((1, LANES), lambda hp, b, i: (0, 0)),
            pl.BlockSpec((1, HEADS_PER_STEP, rel_rev.shape[-1]), lambda hp, b, i: (hp, 0, 0)),
        ],
        out_specs=pl.BlockSpec((1, qt, LANES), lambda hp, b, i: (b, i, hp)),
        out_shape=jax.ShapeDtypeStruct((bsz, s, w), BF16),
        scratch_shapes=[
            pltpu.VMEM((HEADS_PER_STEP, n_blk, blk, blk), F32),
            pltpu.VMEM((HEADS_PER_STEP, n_blk, qt, blk), F32),
            pltpu.VMEM((HEADS_PER_STEP, qt, LANES), F32),
            pltpu.VMEM((HEADS_PER_STEP, qt, LANES), F32),
        ],
        compiler_params=_params("arbitrary", "arbitrary", "arbitrary"),
        name="moba",
    )(proj, kaug, vaug, kmean, jnp.tile(g_q.reshape(1, HEAD_DIM), (1, HEADS_PER_STEP)), rel_rev)


def _sb_kernel(q_ref, k_ref, v_ref, o_ref, acc_ref, carry_ref):
    ti = pl.program_id(2)
    blk = ATT_BLOCK
    n_sub = q_ref.shape[1] // blk
    first = n_sub * ti
    row = lax.broadcasted_iota(jnp.int32, (blk, blk), 0)
    col = lax.broadcasted_iota(jnp.int32, (blk, blk), 1)
    past = col < row
    later = _indicator(row > col, BF16)

    q_heads = []
    for sub in range(n_sub):
        heads = _split_heads(q_ref[0, sub * blk:(sub + 1) * blk, :].astype(F32) * (HEAD_DIM ** -0.5))
        q_heads.append([t.astype(BF16) for t in heads])

    def walk(jobs, fresh):
        for sub, blocks in jobs:
            for h in range(HEADS_PER_STEP):
                carry = None if fresh else carry_ref[sub, h]
                total = None if fresh else acc_ref[sub, h]
                for j, diagonal in blocks:
                    start = j * blk if isinstance(j, int) else pl.multiple_of(j * blk, blk)
                    z = lax.dot_general(q_heads[sub][h], k_ref[0, pl.ds(start, blk), :], _NT,
                                        preferred_element_type=F32)
                    sp = jnp.maximum(z, 0.0) + jnp.log(1.0 + jnp.exp(-jnp.abs(z)))
                    masked = jnp.where(past, sp, 0.0) if diagonal else sp
                    after = jnp.dot(masked.astype(BF16), later, preferred_element_type=F32)
                    row_sum = after[:, 0:1] + masked[:, 0:1]
                    a = jnp.exp(z - sp - after)
                    if diagonal:
                        a = jnp.where(past, a, 0.0)
                    pv = jnp.dot(a.astype(BF16), v_ref[0, pl.ds(start, blk), :], preferred_element_type=F32)
                    if carry is not None:
                        pv = pv * jnp.exp(-carry)
                    total = pv if total is None else total + pv
                    carry = row_sum if carry is None else carry + row_sum
                carry_ref[sub, h] = carry
                acc_ref[sub, h] = total

    @pl.when(ti == 0)
    def _first_step():
        walk([(sub, [(sub, True)] + ([(sub - 1, False)] if sub else [])) for sub in range(n_sub)], True)

    @pl.when(ti > 0)
    def _own_and_previous():
        walk([(sub, [(first + sub, True), (first + sub - 1, False)]) for sub in range(n_sub)], True)

    def slowest_decay(subs):
        worst = functools.reduce(jnp.minimum, [carry_ref[sub, h] for sub in subs for h in range(HEADS_PER_STEP)])
        return jnp.min(worst)

    @pl.when(slowest_decay(range(n_sub)) < SB_EXIT_SUM)
    def _walk_further_back():
        for sub in range(n_sub):
            def cond(state):
                j, worst = state
                return jnp.logical_and(j >= 0, worst < SB_EXIT_SUM)

            def body(state, sub=sub):
                j, _ = state
                walk([(sub, [(j, False)])], False)
                return j - 1, slowest_decay([sub])

            lax.while_loop(cond, body, (first + sub - 2, slowest_decay([sub])))

    for sub in range(n_sub):
        o_ref[0, sub * blk:(sub + 1) * blk, :] = jnp.where(
            _head_lane_mask(), acc_ref[sub, 0], acc_ref[sub, 1]).astype(o_ref.dtype)


def _stick_breaking(proj):
    bsz, s, _ = proj.shape
    blk = ATT_BLOCK
    qt = SB_Q_BLOCKS * blk
    n_hp = N_HEADS // HEADS_PER_STEP
    w = N_HEADS * HEAD_DIM
    base = 3 * n_hp
    return pl.pallas_call(
        _sb_kernel,
        grid=(n_hp, bsz, s // qt),
        in_specs=[
            pl.BlockSpec((1, qt, LANES), lambda hp, b, i: (b, i, base + hp)),
            pl.BlockSpec((1, s, LANES), lambda hp, b, i: (b, 0, base + n_hp + hp)),
            pl.BlockSpec((1, s, LANES), lambda hp, b, i: (b, 0, base + 2 * n_hp + hp)),
        ],
        out_specs=pl.BlockSpec((1, qt, LANES), lambda hp, b, i: (b, i, hp)),
        out_shape=jax.ShapeDtypeStruct((bsz, s, w), BF16),
        scratch_shapes=[
            pltpu.VMEM((SB_Q_BLOCKS, HEADS_PER_STEP, blk, LANES), F32),
            pltpu.VMEM((SB_Q_BLOCKS, HEADS_PER_STEP, blk, 1), F32),
        ],
        compiler_params=_params("arbitrary", "arbitrary", "arbitrary"),
        name="stickbrk",
    )(proj, proj, proj)


def _pack_rows(t):
    n = t.shape[1] // 2
    bits = pltpu.bitcast(t.astype(BF16).astype(F32), U32)
    return bits[:, :n] | (bits[:, n:] >> 16)


def _unpack_rows(p):
    hi = pltpu.bitcast(p & jnp.uint32(0xFFFF0000), F32)
    lo = pltpu.bitcast(p << 16, F32)
    return jnp.concatenate([hi, lo], axis=1)


def _merge_kernel(x_ref, oa_ref, ob_ref, ga_ref, gb_ref, mod_ref, wbm_ref, wbs_ref, wo_ref, gf_ref,
                  wrh_ref, wrl_ref, br_ref, x1_ref, h2_ref, rt_ref, slot_t_ref, cnt_ref, before_ref, carry_ref):
    ma = jnp.dot(oa_ref[0], wbm_ref[...], preferred_element_type=F32)
    mb = jnp.dot(ob_ref[0], wbs_ref[...], preferred_element_type=F32)
    merged = (jax.nn.sigmoid(ga_ref[0].astype(F32)) * ma + jax.nn.sigmoid(gb_ref[0].astype(F32)) * mb)
    t = jnp.dot(merged.astype(BF16), wo_ref[...], preferred_element_type=F32)
    x1 = x_ref[0] + mod_ref[0, 2:3, :] * t
    x1_ref[0] = x1
    h2 = _rms_modulate(x1, gf_ref[...], mod_ref[0, 3:4, :], mod_ref[0, 4:5, :])
    hi = h2.astype(BF16)
    h2_ref[0] = hi
    lo = (h2 - hi.astype(F32)).astype(BF16)
    logits = (lax.dot_general(wrh_ref[...], hi, _NT, preferred_element_type=F32)
              + lax.dot_general(wrh_ref[...], lo, _NT, preferred_element_type=F32)
              + lax.dot_general(wrl_ref[...], hi, _NT, preferred_element_type=F32) + br_ref[...])
    first_tile = jnp.logical_and(pl.program_id(0) == 0, pl.program_id(1) == 0)
    _route_tile(logits, first_tile, rt_ref, slot_t_ref, cnt_ref, before_ref, carry_ref)


def _merge(x, o_a, o_b, proj, mod, wbm, wbs, wo, g_ffn, w_router, b_router):
    bsz, s, d = x.shape
    w = o_a.shape[-1]
    tm = TOKEN_TILE
    gate_blk = (proj.shape[-1] - 2 * d) // d
    const = lambda shape: pl.BlockSpec(shape, lambda b, i: (0,) * len(shape))
    tok = lambda width: pl.BlockSpec((1, tm, width), lambda b, i: (b, i, 0))
    wr_hi = w_router.T.astype(BF16)
    wr_lo = (w_router.T - wr_hi.astype(F32)).astype(BF16)
    per_b = s // tm
    tiles = bsz * per_b
    per_tile = lambda rows, cols: pl.BlockSpec((1, rows, cols), lambda b, i: (b * per_b + i, 0, 0))
    return pl.pallas_call(
        _merge_kernel,
        grid=(bsz, s // tm),
        in_specs=[
            tok(d), tok(w), tok(w),
            pl.BlockSpec((1, tm, d), lambda b, i: (b, i, gate_blk)),
            pl.BlockSpec((1, tm, d), lambda b, i: (b, i, gate_blk + 1)),
            pl.BlockSpec((1, N_MOD, d), lambda b, i: (b, 0, 0)),
            const((w, d)), const((w, d)), const((d, d)), const((1, d)),
            const((LANES, d)), const((LANES, d)), const((LANES, 1)),
        ],
        out_specs=[tok(d), tok(d), tok(ROUTE_COLS), per_tile(ROUTE_COLS, tm), per_tile(LANES, 1), per_tile(LANES, 1)],
        out_shape=[
            jax.ShapeDtypeStruct((bsz, s, d), F32),
            jax.ShapeDtypeStruct((bsz, s, d), BF16),
            jax.ShapeDtypeStruct((bsz, s, ROUTE_COLS), F32),
            jax.ShapeDtypeStruct((tiles, ROUTE_COLS, tm), F32),
            jax.ShapeDtypeStruct((tiles, LANES, 1), F32),
            jax.ShapeDtypeStruct((tiles, LANES, 1), F32),
        ],
        scratch_shapes=[pltpu.VMEM((LANES, 1), F32)],
        compiler_params=_params("arbitrary", "arbitrary"),
        name="merge",
    )(x, o_a, o_b, proj, proj, mod, wbm, wbs, wo, g_ffn.reshape(1, d), wr_hi, wr_lo, b_router)


ROUTE_COLS = 8
SEG_ALIGN = 8


def _local_rows(tm):
    return 2 * tm + N_EXPERTS * SEG_ALIGN


def _route_tile(lg_t, first_tile, rt_ref, slot_t_ref, cnt_ref, before_ref, carry_ref):
    tm = lg_t.shape[1]
    rows = -(-(N_GROUPS + N_EXPERTS) // 8) * 8
    lg = lg_t[:rows]
    row = lax.broadcasted_iota(jnp.int32, (rows, tm), 0).astype(F32)

    @pl.when(first_tile)
    def _init():
        carry_ref[...] = jnp.zeros_like(carry_ref)

    def first_max(vals):
        top = jnp.max(vals, axis=0, keepdims=True)
        idx = jnp.min(jnp.where(vals == top, row, float(rows)), axis=0, keepdims=True)
        return top, idx

    g_logit = jnp.where(row < N_GROUPS, lg, NEG_INF)
    g_top, g_idx = first_max(g_logit)
    g_w = 1.0 / jnp.sum(jnp.exp(g_logit - g_top), axis=0, keepdims=True)
    lo = N_GROUPS + g_idx * EXPERTS_PER_GROUP
    e_logit = jnp.where(jnp.logical_and(row >= lo, row < lo + EXPERTS_PER_GROUP), lg, NEG_INF)
    e_top0, row0 = first_max(e_logit)
    e_top1, row1 = first_max(jnp.where(row == row0, NEG_INF, e_logit))
    z = jnp.exp(e_top1 - e_top0)
    w0 = g_w / (1.0 + z)
    w1 = g_w * z / (1.0 + z)

    hot = jnp.logical_or(row == row0, row == row1)
    t_src = lax.broadcasted_iota(jnp.int32, (tm, tm), 0)
    t_dst = lax.broadcasted_iota(jnp.int32, (tm, tm), 1)
    earlier = jnp.dot(_indicator(hot, BF16), _indicator(t_src < t_dst, BF16), preferred_element_type=F32)
    cnt = jnp.sum(_indicator(hot, F32), axis=1, keepdims=True)
    chunks = jnp.ceil(cnt * (1.0 / SEG_ALIGN))
    spare = jnp.zeros((LANES - rows, LANES), F32)
    e_dst = lax.broadcasted_iota(jnp.int32, (LANES, LANES), 0)
    e_src = lax.broadcasted_iota(jnp.int32, (LANES, LANES), 1)
    chunks_wide = jnp.concatenate([jnp.broadcast_to(chunks, (rows, LANES)), spare], axis=0).astype(BF16)
    seg_start = SEG_ALIGN * jnp.dot(_indicator(e_src < e_dst, BF16), chunks_wide,
                                    preferred_element_type=F32)[:rows, 0:1]
    local = seg_start + earlier
    slot0 = jnp.sum(jnp.where(row == row0, local, 0.0), axis=0, keepdims=True)
    slot1 = jnp.sum(jnp.where(row == row1, local, 0.0), axis=0, keepdims=True)
    seg_rows = jnp.concatenate([SEG_ALIGN * chunks, spare[:, 0:1]], axis=0)
    cnt_ref[0] = seg_rows
    before_ref[0] = carry_ref[...]
    carry_ref[...] += seg_rows

    out_row = lax.broadcasted_iota(jnp.int32, (ROUTE_COLS, tm), 0)
    packed = jnp.zeros((ROUTE_COLS, tm), F32)
    for c, v in enumerate([slot0, slot1, w0, w1]):
        packed = jnp.where(out_row == c, v, packed)
    slot_t_ref[0] = packed
    rt_ref[0] = jnp.concatenate([packed, jnp.zeros((LANES - ROUTE_COLS, tm), F32)], axis=0).T[:, :ROUTE_COLS]


def _for_each_chunk(tile, n_ref, rows_ref, visit):
    max_chunks = rows_ref.shape[0] // n_ref.shape[0]

    def per_chunk(c, carry):
        visit(pl.multiple_of(c * SEG_ALIGN, SEG_ALIGN), pl.multiple_of(rows_ref[tile * max_chunks + c], SEG_ALIGN))
        return carry

    lax.fori_loop(0, n_ref[tile], per_chunk, 0)


def _dispatch_kernel(n_ref, rows_ref, tail_row_ref, tail_n_ref, n_used_ref, slot_t_ref, h_ref, xs_ref,
                     loc_ref, zero_ref, sems, zero_sem):
    tile = pl.program_id(0)
    n_loc, tm = loc_ref.shape[1], h_ref.shape[0]
    buf = tile & 1

    slot = lax.broadcasted_iota(jnp.int32, (n_loc, tm), 0).astype(F32)
    place = jnp.logical_or(slot == slot_t_ref[0, 0:1, :], slot == slot_t_ref[0, 1:2, :])
    loc_ref[buf] = _pack_rows(jnp.dot(_indicator(place, BF16), h_ref[...], preferred_element_type=F32))

    def chunk_copy(b):
        return lambda local, row: pltpu.make_async_copy(
            loc_ref.at[b, pl.ds(local, SEG_ALIGN), :], xs_ref.at[pl.ds(row, SEG_ALIGN), :], sems.at[b])

    _for_each_chunk(tile, n_ref, rows_ref, lambda l, r: chunk_copy(buf)(l, r).start())

    @pl.when(tile > 0)
    def _previous_done():
        _for_each_chunk(tile - 1, n_ref, rows_ref, lambda l, r: chunk_copy(1 - buf)(l, r).wait())

    @pl.when(tile == pl.num_programs(0) - 1)
    def _last_done():
        _for_each_chunk(tile, n_ref, rows_ref, lambda l, r: chunk_copy(buf)(l, r).wait())
        zero_ref[...] = jnp.zeros_like(zero_ref)
        block_rows = zero_ref.shape[0]

        def fill(action):
            def per_expert(e, carry):
                def per_chunk(c, inner):
                    row = pl.multiple_of(tail_row_ref[e] + c * SEG_ALIGN, SEG_ALIGN)
                    action(pltpu.make_async_copy(zero_ref.at[pl.ds(0, SEG_ALIGN), :],
                                                 xs_ref.at[pl.ds(row, SEG_ALIGN), :], zero_sem))
                    return inner

                lax.fori_loop(0, tail_n_ref[e], per_chunk, 0)
                return carry

            lax.fori_loop(0, N_EXPERTS, per_expert, 0)

            def per_block(blk, carry):
                row = pl.multiple_of(blk * block_rows, block_rows)
                action(pltpu.make_async_copy(zero_ref, xs_ref.at[pl.ds(row, block_rows), :], zero_sem))
                return carry

            lax.fori_loop(n_used_ref[0], xs_ref.shape[0] // block_rows, per_block, 0)

        fill(lambda copy: copy.start())
        fill(lambda copy: copy.wait())


def _dispatch(n_chunks, chunk_rows, tail_row, tail_chunks, n_used, slot_t, h2, n_rows):
    n, d = h2.shape
    tm = TOKEN_TILE
    grid_spec = pltpu.PrefetchScalarGridSpec(
        num_scalar_prefetch=5,
        grid=(n // tm,),
        in_specs=[
            pl.BlockSpec((1, ROUTE_COLS, tm), lambda i, *_: (i, 0, 0)),
            pl.BlockSpec((tm, d), lambda i, *_: (i, 0)),
        ],
        out_specs=pl.BlockSpec(memory_space=pl.ANY),
        scratch_shapes=[pltpu.VMEM((2, _local_rows(tm), d // 2), U32), pltpu.VMEM((ROW_BLOCK, d // 2), U32),
                        pltpu.SemaphoreType.DMA((2,)), pltpu.SemaphoreType.DMA(())],
    )
    return pl.pallas_call(
        _dispatch_kernel,
        grid_spec=grid_spec,
        out_shape=jax.ShapeDtypeStruct((n_rows, d // 2), U32),
        compiler_params=_params("arbitrary"),
        name="dispatch",
    )(n_chunks, chunk_rows, tail_row, tail_chunks, n_used, slot_t, h2)


def _expert_kernel(blk_e_ref, n_used_ref, xs_ref, wg_ref, wu_ref, wd_ref, ys_ref, wg_bf, wu_bf, wd_bf):
    i = pl.program_id(0)
    used = i < n_used_ref[0]

    @pl.when(jnp.logical_and(used, jnp.logical_or(i == 0, blk_e_ref[i] != blk_e_ref[jnp.maximum(i - 1, 0)])))
    def _new_expert():
        wg_bf[...] = wg_ref[0].astype(BF16)
        wu_bf[...] = wu_ref[0].astype(BF16)
        wd_bf[...] = wd_ref[0].astype(BF16)

    @pl.when(used)
    def _compute():
        xb = _unpack_rows(xs_ref[...]).astype(BF16)
        g = jnp.dot(xb, wg_bf[...], preferred_element_type=F32)
        u = jnp.dot(xb, wu_bf[...], preferred_element_type=F32)
        hid = (g * jax.nn.sigmoid(g) * u).astype(BF16)
        ys_ref[...] = _pack_rows(jnp.dot(hid, wd_bf[...], preferred_element_type=F32))

    @pl.when(i >= n_used_ref[0])
    def _unused():
        ys_ref[...] = jnp.zeros_like(ys_ref)


def _experts(blk_e, n_used, xs, wg, wu, wd):
    n_rows, half = xs.shape
    _, d, de = wg.shape
    grid_spec = pltpu.PrefetchScalarGridSpec(
        num_scalar_prefetch=2,
        grid=(n_rows // ROW_BLOCK,),
        in_specs=[
            pl.BlockSpec((ROW_BLOCK, half), lambda i, e, u: (jnp.minimum(i, u[0] - 1), 0)),
            pl.BlockSpec((1, d, de), lambda i, e, u: (e[i], 0, 0)),
            pl.BlockSpec((1, d, de), lambda i, e, u: (e[i], 0, 0)),
            pl.BlockSpec((1, de, d), lambda i, e, u: (e[i], 0, 0)),
        ],
        out_specs=pl.BlockSpec((ROW_BLOCK, half), lambda i, e, u: (i, 0)),
        scratch_shapes=[pltpu.VMEM((d, de), BF16), pltpu.VMEM((d, de), BF16), pltpu.VMEM((de, d), BF16)],
    )
    return pl.pallas_call(
        _expert_kernel,
        grid_spec=grid_spec,
        out_shape=jax.ShapeDtypeStruct((n_rows, half), U32),
        compiler_params=_params("arbitrary"),
        name="experts",
    )(blk_e, n_used, xs, wg, wu, wd)


def _combine_kernel(n_ref, rows_ref, x1_ref, mod_ref, rt_ref, ys_ref, o_ref, loc_ref, sems):
    tm = x1_ref.shape[1]
    n_loc = loc_ref.shape[1]
    tile = pl.program_id(0) * pl.num_programs(1) + pl.program_id(1)
    n_tiles = pl.num_programs(0) * pl.num_programs(1)
    buf = tile & 1

    def chunk_copy(b):
        return lambda local, row: pltpu.make_async_copy(
            ys_ref.at[pl.ds(row, SEG_ALIGN), :], loc_ref.at[b, pl.ds(local, SEG_ALIGN), :], sems.at[b])

    @pl.when(tile == 0)
    def _init():
        loc_ref[...] = jnp.zeros_like(loc_ref)
        _for_each_chunk(tile, n_ref, rows_ref, lambda l, r: chunk_copy(buf)(l, r).start())

    @pl.when(tile + 1 < n_tiles)
    def _fetch_next():
        _for_each_chunk(tile + 1, n_ref, rows_ref, lambda l, r: chunk_copy(1 - buf)(l, r).start())

    _for_each_chunk(tile, n_ref, rows_ref, lambda l, r: chunk_copy(buf)(l, r).wait())

    y = _unpack_rows(loc_ref[buf]).astype(BF16)
    rt = rt_ref[0]
    slot = lax.broadcasted_iota(jnp.int32, (tm, n_loc), 1).astype(F32)
    mix = jnp.where(slot == rt[:, 0:1], rt[:, 2:3], jnp.where(slot == rt[:, 1:2], rt[:, 3:4], 0.0)).astype(BF16)
    o_ref[0] = x1_ref[0] + mod_ref[0, 5:6, :] * jnp.dot(mix, y, preferred_element_type=F32)


def _combine(n_chunks, chunk_rows, x1, mod, rt, ys):
    bsz, s, d = x1.shape
    tm = TOKEN_TILE
    grid_spec = pltpu.PrefetchScalarGridSpec(
        num_scalar_prefetch=2,
        grid=(bsz, s // tm),
        in_specs=[
            pl.BlockSpec((1, tm, d), lambda b, i, c, r: (b, i, 0)),
            pl.BlockSpec((1, N_MOD, d), lambda b, i, c, r: (b, 0, 0)),
            pl.BlockSpec((1, tm, ROUTE_COLS), lambda b, i, c, r: (b, i, 0)),
            pl.BlockSpec(memory_space=pl.ANY),
        ],
        out_specs=pl.BlockSpec((1, tm, d), lambda b, i, c, r: (b, i, 0)),
        scratch_shapes=[pltpu.VMEM((2, _local_rows(tm), d // 2), U32), pltpu.SemaphoreType.DMA((2,))],
    )
    return pl.pallas_call(
        _combine_kernel,
        grid_spec=grid_spec,
        out_shape=jax.ShapeDtypeStruct((bsz, s, d), F32),
        compiler_params=_params("arbitrary", "arbitrary"),
        name="combine",
    )(n_chunks, chunk_rows, x1, mod, rt, ys)


def _layer(x, mod, g_mix, w_in, g_q, g_k, rel_bias, w_br_moba, w_br_sb, w_out, g_ffn,
           w_rg, b_rg, w_re, b_re, w_gate, w_up, w_down):
    bsz, s, d = x.shape
    n = bsz * s
    n_blk = s // ATT_BLOCK

    proj, kaug, vaug, kmean = _inproj(x, mod, g_mix, w_in.astype(BF16), g_k)

    top = (n_blk + 1) * ATT_BLOCK - 1
    dist = np.maximum(top - np.arange((n_blk + 2) * ATT_BLOCK), 0)
    rel_rev = rel_bias[:, _rel_bucket_table(top)[dist]].reshape(N_HEADS // HEADS_PER_STEP, HEADS_PER_STEP, -1)

    o_a = _moba(proj, kaug, vaug, kmean, g_q, rel_rev)
    o_b = _stick_breaking(proj)

    pad = LANES - N_GROUPS - N_EXPERTS
    w_router = jnp.concatenate([w_rg, w_re, jnp.zeros((d, pad), F32)], axis=1)
    b_router = jnp.concatenate([b_rg, b_re, jnp.zeros((pad,), F32)]).reshape(LANES, 1)
    x1, h2, rt, slot_t, tile_cnt, tile_before = _merge(x, o_a, o_b, proj, mod, w_br_moba.astype(BF16),
                                                       w_br_sb.astype(BF16), w_out.astype(BF16), g_ffn,
                                                       w_router, b_router)

    experts = slice(N_GROUPS, N_GROUPS + N_EXPERTS)
    cnt = tile_cnt[:, experts, 0].astype(jnp.int32)
    total = jnp.sum(cnt, axis=0)
    padded = (total + ROW_BLOCK - 1) // ROW_BLOCK * ROW_BLOCK
    pend = jnp.cumsum(padded)
    pstart = pend - padded
    max_rows = 2 * n + cnt.shape[0] * N_EXPERTS * (SEG_ALIGN - 1) + N_EXPERTS * (ROW_BLOCK - 1)
    n_blocks = -(-max_rows // ROW_BLOCK)
    blk_row0 = jnp.arange(n_blocks, dtype=jnp.int32) * ROW_BLOCK
    blk_e = jnp.minimum(jnp.sum((blk_row0[:, None] >= pend[None, :]).astype(jnp.int32), axis=1), N_EXPERTS - 1)
    n_used = (pend[-1] // ROW_BLOCK).astype(jnp.int32).reshape(1)
    seg_row0 = pstart[None, :] + tile_before[:, experts, 0].astype(jnp.int32)
    chunks = cnt // SEG_ALIGN
    chunk_end = jnp.cumsum(chunks, axis=1)
    chunk_start = (chunk_end - chunks)[:, None, :]
    c = jnp.arange(_local_rows(TOKEN_TILE) // SEG_ALIGN, dtype=jnp.int32)[None, :, None]
    inside = jnp.logical_and(c >= chunk_start, c < chunk_end[:, None, :])
    chunk_rows = jnp.sum(jnp.where(inside, seg_row0[:, None, :] + SEG_ALIGN * (c - chunk_start), 0), axis=2).reshape(-1)
    n_chunks = chunk_end[:, -1]

    xs = _dispatch(n_chunks, chunk_rows, pstart + total, (padded - total) // SEG_ALIGN, n_used, slot_t,
                   h2.reshape(n, d), n_blocks * ROW_BLOCK)
    ys = _experts(blk_e, n_used, xs, w_gate, w_up, w_down)
    return _combine(n_chunks, chunk_rows, x1, mod, rt, ys)


def kernel(x, c, w_ada, b_ada, g_mix, w_in, g_q, g_k, rel_bias, w_br_moba, w_br_sb, w_out, g_ffn,
           w_rg, b_rg, w_re, b_re, w_gate, w_up, w_down):
    bsz, s, d = x.shape
    assert s % TOKEN_TILE == 0 and s % MOBA_Q_TILE == 0 and d % (2 * LANES) == 0
    for l in range(w_ada.shape[0]):
        mod = _adaln(c, w_ada[l], b_ada[l]).reshape(bsz, N_MOD, d)
        x = _layer(x, mod, g_mix[l], w_in[l], g_q[l], g_k[l], rel_bias, w_br_moba[l], w_br_sb[l], w_out[l],
                   g_ffn[l], w_rg[l], b_rg[l], w_re[l], b_re[l], w_gate[l], w_up[l], w_down[l])
    return x
```

```python
import functools
import math

import numpy as np
import jax
import jax.numpy as jnp
from jax import lax
from jax.experimental import pallas as pl
from jax.experimental.pallas import tpu as pltpu

F32 = jnp.float32
BF16 = jnp.bfloat16
U32 = jnp.uint32
HIGHEST = lax.Precision.HIGHEST

LANES = 128
VMEM_LIMIT_BYTES = 56 * 1024 * 1024

HEAD_DIM = 64
N_HEADS = 8
HEADS_PER_STEP = LANES // HEAD_DIM
ATT_BLOCK = 256
MOBA_Q_TILE = 2 * ATT_BLOCK
SB_Q_BLOCKS = 4
MOBA_TOPK = 3
REL_BUCKETS = 32
REL_MAX_DIST = 1024
N_GROUPS = 4
EXPERTS_PER_GROUP = 8
N_EXPERTS = N_GROUPS * EXPERTS_PER_GROUP
N_MOD = 6
RMS_EPS = 1e-6
ROW_BLOCK = 512
TOKEN_TILE = 512
NEG_INF = float("-inf")
MASK_LOGIT = -1e30
SB_EXIT_SUM = 110.0

_NT = (((1,), (1,)), ((), ()))


def _params(*semantics):
    return pltpu.CompilerParams(dimension_semantics=semantics, vmem_limit_bytes=VMEM_LIMIT_BYTES)


def _adaln_kernel(c_ref, w_ref, b_ref, o_ref):
    c = c_ref[...]
    s = c * jax.nn.sigmoid(c)
    o_ref[...] = jnp.dot(s, w_ref[...], preferred_element_type=F32, precision=HIGHEST) + b_ref[...]


def _adaln(c, w, b):
    bsz, d = c.shape
    n = w.shape[1]
    tn = 1536
    return pl.pallas_call(
        _adaln_kernel,
        grid=(n // tn,),
        in_specs=[
            pl.BlockSpec((bsz, d), lambda j: (0, 0)),
            pl.BlockSpec((d, tn), lambda j: (0, j)),
            pl.BlockSpec((1, tn), lambda j: (0, j)),
        ],
        out_specs=pl.BlockSpec((bsz, tn), lambda j: (0, j)),
        out_shape=jax.ShapeDtypeStruct((bsz, n), F32),
        compiler_params=_params("arbitrary"),
        name="adaln",
    )(c, w, b.reshape(1, n))


def _rms_modulate(x, g, shift, scale):
    y = x * lax.rsqrt(jnp.mean(x * x, axis=-1, keepdims=True) + RMS_EPS) * g
    return y * (1.0 + scale) + shift


def _inproj_kernel(x_ref, mod_ref, g_ref, w_ref, gk_ref, o_ref, kaug_ref, vaug_ref, kmean_ref, *, col_chunk):
    i = pl.program_id(1)
    tm = x_ref.shape[1]
    h = _rms_modulate(x_ref[0], g_ref[...], mod_ref[0, 0:1, :], mod_ref[0, 1:2, :]).astype(BF16)

    w_moba = N_HEADS * HEAD_DIM
    blocks = tm // ATT_BLOCK
    first = _head_lane_mask()
    row_blk = blocks * i + lax.shift_right_logical(lax.broadcasted_iota(jnp.int32, (tm, LANES), 0),
                                                   ATT_BLOCK.bit_length() - 1)
    tag = jnp.where(_lane_index() - float(HEAD_DIM) == row_blk.astype(F32), 1.0, 0.0)
    mean_row = lax.broadcasted_iota(jnp.int32, kmean_ref.shape[2:], 0)

    @pl.when(i == 0)
    def _init():
        kmean_ref[...] = jnp.zeros_like(kmean_ref)

    for n in range(w_ref.shape[1] // col_chunk):
        cols = slice(n * col_chunk, (n + 1) * col_chunk)
        r = jnp.dot(h, w_ref[:, cols], preferred_element_type=F32)
        o_ref[0, :, cols] = r.astype(BF16)
        for c in range(col_chunk // LANES):
            col0 = n * col_chunk + c * LANES
            part = r[:, c * LANES:(c + 1) * LANES]
            if w_moba <= col0 < 2 * w_moba:
                kn = _head_rmsnorm(part, gk_ref[...])
                for hh, kh in enumerate((kn, pltpu.roll(kn, HEAD_DIM, 1))):
                    head = (col0 - w_moba) // HEAD_DIM + hh
                    kaug_ref[0, head] = jnp.where(first, kh, tag).astype(BF16)
                    means = kmean_ref[0, head]
                    for jb in range(blocks):
                        mean = jnp.mean(kh[jb * ATT_BLOCK:(jb + 1) * ATT_BLOCK], axis=0, keepdims=True)
                        means = jnp.where(mean_row == blocks * i + jb, jnp.where(first, mean, 0.0), means)
                    kmean_ref[0, head] = means
            elif 2 * w_moba <= col0 < 3 * w_moba:
                for hh, vh in enumerate((part, pltpu.roll(part, HEAD_DIM, 1))):
                    head = (col0 - 2 * w_moba) // HEAD_DIM + hh
                    vaug_ref[0, head] = jnp.where(first, vh, 1.0).astype(BF16)


def _inproj(x, mod, g, w_bf16, g_k):
    bsz, s, d = x.shape
    n = w_bf16.shape[1]
    tm = TOKEN_TILE
    n_blk = s // ATT_BLOCK
    per_head = lambda rows: pl.BlockSpec((1, N_HEADS, rows, LANES), lambda b, i: (b, 0, i, 0))
    return pl.pallas_call(
        functools.partial(_inproj_kernel, col_chunk=1024),
        grid=(bsz, s // tm),
        in_specs=[
            pl.BlockSpec((1, tm, d), lambda b, i: (b, i, 0)),
            pl.BlockSpec((1, N_MOD, d), lambda b, i: (b, 0, 0)),
            pl.BlockSpec((1, d), lambda b, i: (0, 0)),
            pl.BlockSpec((d, n), lambda b, i: (0, 0)),
            pl.BlockSpec((1, LANES), lambda b, i: (0, 0)),
        ],
        out_specs=[
            pl.BlockSpec((1, tm, n), lambda b, i: (b, i, 0)),
            per_head(tm),
            per_head(tm),
            pl.BlockSpec((1, N_HEADS, -(-n_blk // 8) * 8, LANES), lambda b, i: (b, 0, 0, 0)),
        ],
        out_shape=[
            jax.ShapeDtypeStruct((bsz, s, n), BF16),
            jax.ShapeDtypeStruct((bsz, N_HEADS, s, LANES), BF16),
            jax.ShapeDtypeStruct((bsz, N_HEADS, s, LANES), BF16),
            jax.ShapeDtypeStruct((bsz, N_HEADS, -(-n_blk // 8) * 8, LANES), F32),
        ],
        compiler_params=_params("arbitrary", "arbitrary"),
        name="inproj",
    )(x, mod, g.reshape(1, d), w_bf16, jnp.tile(g_k.reshape(1, HEAD_DIM), (1, HEADS_PER_STEP)))


def _indicator(cond, dtype):
    return jnp.where(cond, 1.0, 0.0).astype(dtype)


def _lane_index():
    return lax.broadcasted_iota(jnp.int32, (1, LANES), 1).astype(F32)


def _head_lane_mask():
    return lax.broadcasted_iota(jnp.int32, (1, LANES), 1) < HEAD_DIM


def _head_rmsnorm(t, g):
    first = _head_lane_mask()
    sq = t * t
    ss0 = jnp.sum(jnp.where(first, sq, 0.0), axis=-1, keepdims=True)
    ss1 = jnp.sum(jnp.where(first, 0.0, sq), axis=-1, keepdims=True)
    inv = jnp.where(first, lax.rsqrt(ss0 / HEAD_DIM + RMS_EPS), lax.rsqrt(ss1 / HEAD_DIM + RMS_EPS))
    return t * inv * g


def _split_heads(t):
    first = _head_lane_mask()
    zero = jnp.zeros_like(t)
    return jnp.where(first, t, zero), jnp.where(first, zero, t)


def _rel_bucket_table(max_dist):
    n = np.arange(max_dist + 1)
    max_exact = REL_BUCKETS // 2
    nf = np.maximum(n, 1).astype(np.float64)
    large = max_exact + (np.log(nf / max_exact) / math.log(REL_MAX_DIST / max_exact)
                         * (REL_BUCKETS - max_exact)).astype(np.int64)
    large = np.minimum(large, REL_BUCKETS - 1)
    return np.where(n < max_exact, n, large).astype(np.int32)


def _moba_kernel(q_ref, kaug_ref, vaug_ref, kmean_ref, gq_ref, rb_ref, o_ref,
                 toep_ref, s_ref, mvec_ref, acc_ref, *, n_blk):
    b = pl.program_id(1)
    qi = pl.program_id(2)
    blk = ATT_BLOCK
    qt = q_ref.shape[1]
    halves = qt // blk
    first = _head_lane_mask()

    @pl.when(jnp.logical_and(b == 0, qi == 0))
    def _build_bias_tiles():
        for h in range(HEADS_PER_STEP):
            for d in range(n_blk):
                r = rb_ref[0, h:h + 1, (n_blk - d) * blk:(n_blk - d + 2) * blk]
                rolled = pltpu.roll(jnp.broadcast_to(r, (blk, 2 * blk)), blk + 1, 1, stride=1, stride_axis=0)
                toep_ref[h, d] = rolled[:, :blk]

    qn = _head_rmsnorm(q_ref[0].astype(F32), gq_ref[...])
    gate_rows = kmean_ref.shape[2]
    blk_row = lax.broadcasted_iota(jnp.int32, (gate_rows, qt), 0).astype(F32)
    own = (halves * qi + lax.shift_right_logical(lax.broadcasted_iota(jnp.int32, (gate_rows, qt), 1),
                                                 blk.bit_length() - 1)).astype(F32)
    q_aug = []
    for h, qh in enumerate((qn, pltpu.roll(qn, HEAD_DIM, 1))):
        qh = jnp.where(first, qh, 0.0)
        km = kmean_ref[0, h]
        km_hi, q_hi = km.astype(BF16), qh.astype(BF16)
        km_lo, q_lo = (km - km_hi.astype(F32)).astype(BF16), (qh - q_hi.astype(F32)).astype(BF16)
        g = (lax.dot_general(km_hi, q_hi, _NT, preferred_element_type=F32)
             + lax.dot_general(km_lo, q_hi, _NT, preferred_element_type=F32)
             + lax.dot_general(km_hi, q_lo, _NT, preferred_element_type=F32))
        g = jnp.where(blk_row < own, g, NEG_INF)
        keep = jnp.where(blk_row == own, 1.0, 0.0)
        for _ in range(MOBA_TOPK):
            top = jnp.max(g, axis=0, keepdims=True)
            is_top = jnp.logical_and(g == top, top > NEG_INF)
            idx = jnp.min(jnp.where(is_top, blk_row, float(gate_rows)), axis=0, keepdims=True)
            pick = blk_row == idx
            keep = jnp.where(pick, 1.0, keep)
            g = jnp.where(pick, NEG_INF, g)
        mask_t = jnp.where(keep > 0.0, 0.0, MASK_LOGIT)
        mask_logit = jnp.concatenate([jnp.zeros((HEAD_DIM, qt), F32), mask_t,
                                      jnp.zeros((LANES - HEAD_DIM - gate_rows, qt), F32)], axis=0).T
        q_aug.append(jnp.where(first, qh * (HEAD_DIM ** -0.5), mask_logit).astype(BF16))

    def lane_max(s):
        return jnp.maximum(s[:, :LANES], s[:, LANES:])

    def scores(h, t, dist):
        k_t = kaug_ref[0, h, pl.ds(pl.multiple_of(t * qt, qt), qt), :]
        s = lax.dot_general(q_aug[h], k_t, _NT, preferred_element_type=F32)
        tiles = []
        for kb in range(halves):
            rows = []
            for qb in range(halves):
                part = s[qb * blk:(qb + 1) * blk, kb * blk:(kb + 1) * blk]
                if isinstance(dist, int) and halves * dist + qb - kb < 0:
                    rows.append(part)
                else:
                    rows.append(part + toep_ref[h, halves * dist + qb - kb])
            tiles.append(jnp.concatenate(rows, axis=0))
        return tiles

    row = lax.broadcasted_iota(jnp.int32, (qt, blk), 0)
    col = lax.broadcasted_iota(jnp.int32, (qt, blk), 1)
    for h in range(HEADS_PER_STEP):
        top = None
        for kb, s in enumerate(scores(h, qi, 0)):
            s = jnp.where(col + kb * blk <= row, s, NEG_INF)
            s_ref[h, halves * qi + kb] = s
            top = lane_max(s) if top is None else jnp.maximum(top, lane_max(s))
        mvec_ref[h] = top

    def in_pairs(count, visit):
        def quad(p, carry):
            visit(4 * p, 4)
            return carry

        lax.fori_loop(0, lax.shift_right_logical(count, 2), quad, 0)

        @pl.when((count & 2) == 2)
        def _pair():
            visit(count & ~3, 2)

        @pl.when((count & 1) == 1)
        def _last():
            visit(count - 1, 1)

    def pass1(t0, n_tiles):
        for h in range(HEADS_PER_STEP):
            top = mvec_ref[h]
            for t in [t0 + u for u in range(n_tiles)]:
                for kb, s in enumerate(scores(h, t, qi - t)):
                    s_ref[h, halves * t + kb] = s
                    top = jnp.maximum(top, lane_max(s))
            mvec_ref[h] = top

    in_pairs(qi, pass1)

    for h in range(HEADS_PER_STEP):
        m = jnp.max(mvec_ref[h], axis=-1, keepdims=True)
        mvec_ref[h] = jnp.broadcast_to(m, (qt, LANES))
        acc_ref[h] = jnp.zeros((qt, LANES), F32)

    def pass2(t0, n_tiles):
        keys = pl.ds(pl.multiple_of(t0 * qt, qt), n_tiles * qt)
        for h in range(HEADS_PER_STEP):
            m = mvec_ref[h]
            parts = []
            for kb in range(n_tiles * halves):
                s = s_ref[h, halves * t0 + kb]
                parts += [jnp.exp(s[:, :LANES] - m), jnp.exp(s[:, LANES:] - m)]
            p = jnp.concatenate(parts, axis=1).astype(BF16)
            acc_ref[h] += jnp.dot(p, vaug_ref[0, h, keys, :], preferred_element_type=F32)

    in_pairs(qi + 1, pass2)

    out = [acc_ref[h] / acc_ref[h][:, HEAD_DIM:HEAD_DIM + 1] for h in range(HEADS_PER_STEP)]
    o_ref[0] = jnp.where(first, out[0], pltpu.roll(out[1], HEAD_DIM, 1)).astype(o_ref.dtype)


def _moba(proj, kaug, vaug, kmean, g_q, rel_rev):
    bsz, s, _ = proj.shape
    blk = ATT_BLOCK
    n_blk = s // blk
    n_hp = N_HEADS // HEADS_PER_STEP
    w = N_HEADS * HEAD_DIM
    qt = MOBA_Q_TILE
    per_pair = lambda arr: pl.BlockSpec((1, HEADS_PER_STEP) + arr.shape[2:], lambda hp, b, i: (b, hp, 0, 0))
    return pl.pallas_call(
        functools.partial(_moba_kernel, n_blk=n_blk),
        grid=(n_hp, bsz, s // qt),
        in_specs=[
            pl.BlockSpec((1, qt, LANES), lambda hp, b, i: (b, i, hp)),
            per_pair(kaug), per_pair(vaug), per_pair(kmean),
            pl.BlockSpec((1, LANES), lambda hp, b, i: (0, 0)),
            pl.BlockSpec((1, HEADS_PER_STEP, rel_rev.shape[-1]), lambda hp, b, i: (hp, 0, 0)),
        ],
        out_specs=pl.BlockSpec((1, qt, LANES), lambda hp, b, i: (b, i, hp)),
        out_shape=jax.ShapeDtypeStruct((bsz, s, w), BF16),
        scratch_shapes=[
            pltpu.VMEM((HEADS_PER_STEP, n_blk, blk, blk), F32),
            pltpu.VMEM((HEADS_PER_STEP, n_blk, qt, blk), F32),
            pltpu.VMEM((HEADS_PER_STEP, qt, LANES), F32),
            pltpu.VMEM((HEADS_PER_STEP, qt, LANES), F32),
        ],
        compiler_params=_params("arbitrary", "arbitrary", "arbitrary"),
        name="moba",
    )(proj, kaug, vaug, kmean, jnp.tile(g_q.reshape(1, HEAD_DIM), (1, HEADS_PER_STEP)), rel_rev)


def _sb_kernel(q_ref, k_ref, v_ref, o_ref, acc_ref, carry_ref):
    ti = pl.program_id(2)
    blk = ATT_BLOCK
    n_sub = q_ref.shape[1] // blk
    first = n_sub * ti
    row = lax.broadcasted_iota(jnp.int32, (blk, blk), 0)
    col = lax.broadcasted_iota(jnp.int32, (blk, blk), 1)
    past = col < row
    later = _indicator(row > col, BF16)

    q_heads = []
    for sub in range(n_sub):
        heads = _split_heads(q_ref[0, sub * blk:(sub + 1) * blk, :].astype(F32) * (HEAD_DIM ** -0.5))
        q_heads.append([t.astype(BF16) for t in heads])

    def walk(jobs, fresh):
        for sub, blocks in jobs:
            for h in range(HEADS_PER_STEP):
                carry = None if fresh else carry_ref[sub, h]
                total = None if fresh else acc_ref[sub, h]
                for j, diagonal in blocks:
                    start = j * blk if isinstance(j, int) else pl.multiple_of(j * blk, blk)
                    z = lax.dot_general(q_heads[sub][h], k_ref[0, pl.ds(start, blk), :], _NT,
                                        preferred_element_type=F32)
                    sp = jnp.maximum(z, 0.0) + jnp.log(1.0 + jnp.exp(-jnp.abs(z)))
                    masked = jnp.where(past, sp, 0.0) if diagonal else sp
                    after = jnp.dot(masked.astype(BF16), later, preferred_element_type=F32)
                    row_sum = after[:, 0:1] + masked[:, 0:1]
                    a = jnp.exp(z - sp - after)
                    if diagonal:
                        a = jnp.where(past, a, 0.0)
                    pv = jnp.dot(a.astype(BF16), v_ref[0, pl.ds(start, blk), :], preferred_element_type=F32)
                    if carry is not None:
                        pv = pv * jnp.exp(-carry)
                    total = pv if total is None else total + pv
                    carry = row_sum if carry is None else carry + row_sum
                carry_ref[sub, h] = carry
                acc_ref[sub, h] = total

    @pl.when(ti == 0)
    def _first_step():
        walk([(sub, [(sub, True)] + ([(sub - 1, False)] if sub else [])) for sub in range(n_sub)], True)

    @pl.when(ti > 0)
    def _own_and_previous():
        walk([(sub, [(first + sub, True), (first + sub - 1, False)]) for sub in range(n_sub)], True)

    def slowest_decay(subs):
        worst = functools.reduce(jnp.minimum, [carry_ref[sub, h] for sub in subs for h in range(HEADS_PER_STEP)])
        return jnp.min(worst)

    @pl.when(slowest_decay(range(n_sub)) < SB_EXIT_SUM)
    def _walk_further_back():
        for sub in range(n_sub):
            def cond(state):
                j, worst = state
                return jnp.logical_and(j >= 0, worst < SB_EXIT_SUM)

            def body(state, sub=sub):
                j, _ = state
                walk([(sub, [(j, False)])], False)
                return j - 1, slowest_decay([sub])

            lax.while_loop(cond, body, (first + sub - 2, slowest_decay([sub])))

    for sub in range(n_sub):
        o_ref[0, sub * blk:(sub + 1) * blk, :] = jnp.where(
            _head_lane_mask(), acc_ref[sub, 0], acc_ref[sub, 1]).astype(o_ref.dtype)


def _stick_breaking(proj):
    bsz, s, _ = proj.shape
    blk = ATT_BLOCK
    qt = SB_Q_BLOCKS * blk
    n_hp = N_HEADS // HEADS_PER_STEP
    w = N_HEADS * HEAD_DIM
    base = 3 * n_hp
    return pl.pallas_call(
        _sb_kernel,
        grid=(n_hp, bsz, s // qt),
        in_specs=[
            pl.BlockSpec((1, qt, LANES), lambda hp, b, i: (b, i, base + hp)),
            pl.BlockSpec((1, s, LANES), lambda hp, b, i: (b, 0, base + n_hp + hp)),
            pl.BlockSpec((1, s, LANES), lambda hp, b, i: (b, 0, base + 2 * n_hp + hp)),
        ],
        out_specs=pl.BlockSpec((1, qt, LANES), lambda hp, b, i: (b, i, hp)),
        out_shape=jax.ShapeDtypeStruct((bsz, s, w), BF16),
        scratch_shapes=[
            pltpu.VMEM((SB_Q_BLOCKS, HEADS_PER_STEP, blk, LANES), F32),
            pltpu.VMEM((SB_Q_BLOCKS, HEADS_PER_STEP, blk, 1), F32),
        ],
        compiler_params=_params("arbitrary", "arbitrary", "arbitrary"),
        name="stickbrk",
    )(proj, proj, proj)


def _pack_rows(t):
    n = t.shape[1] // 2
    bits = pltpu.bitcast(t.astype(BF16).astype(F32), U32)
    return bits[:, :n] | (bits[:, n:] >> 16)


def _unpack_rows(p):
    hi = pltpu.bitcast(p & jnp.uint32(0xFFFF0000), F32)
    lo = pltpu.bitcast(p << 16, F32)
    return jnp.concatenate([hi, lo], axis=1)


def _merge_kernel(x_ref, oa_ref, ob_ref, ga_ref, gb_ref, mod_ref, wbm_ref, wbs_ref, wo_ref, gf_ref,
                  wrh_ref, wrl_ref, br_ref, x1_ref, h2_ref, rt_ref, slot_t_ref, cnt_ref, before_ref, carry_ref):
    ma = jnp.dot(oa_ref[0], wbm_ref[...], preferred_element_type=F32)
    mb = jnp.dot(ob_ref[0], wbs_ref[...], preferred_element_type=F32)
    merged = (jax.nn.sigmoid(ga_ref[0].astype(F32)) * ma + jax.nn.sigmoid(gb_ref[0].astype(F32)) * mb)
    t = jnp.dot(merged.astype(BF16), wo_ref[...], preferred_element_type=F32)
    x1 = x_ref[0] + mod_ref[0, 2:3, :] * t
    x1_ref[0] = x1
    h2 = _rms_modulate(x1, gf_ref[...], mod_ref[0, 3:4, :], mod_ref[0, 4:5, :])
    hi = h2.astype(BF16)
    h2_ref[0] = hi
    lo = (h2 - hi.astype(F32)).astype(BF16)
    logits = (lax.dot_general(wrh_ref[...], hi, _NT, preferred_element_type=F32)
              + lax.dot_general(wrh_ref[...], lo, _NT, preferred_element_type=F32)
              + lax.dot_general(wrl_ref[...], hi, _NT, preferred_element_type=F32) + br_ref[...])
    first_tile = jnp.logical_and(pl.program_id(0) == 0, pl.program_id(1) == 0)
    _route_tile(logits, first_tile, rt_ref, slot_t_ref, cnt_ref, before_ref, carry_ref)


def _merge(x, o_a, o_b, proj, mod, wbm, wbs, wo, g_ffn, w_router, b_router):
    bsz, s, d = x.shape
    w = o_a.shape[-1]
    tm = TOKEN_TILE
    gate_blk = (proj.shape[-1] - 2 * d) // d
    const = lambda shape: pl.BlockSpec(shape, lambda b, i: (0,) * len(shape))
    tok = lambda width: pl.BlockSpec((1, tm, width), lambda b, i: (b, i, 0))
    wr_hi = w_router.T.astype(BF16)
    wr_lo = (w_router.T - wr_hi.astype(F32)).astype(BF16)
    per_b = s // tm
    tiles = bsz * per_b
    per_tile = lambda rows, cols: pl.BlockSpec((1, rows, cols), lambda b, i: (b * per_b + i, 0, 0))
    return pl.pallas_call(
        _merge_kernel,
        grid=(bsz, s // tm),
        in_specs=[
            tok(d), tok(w), tok(w),
            pl.BlockSpec((1, tm, d), lambda b, i: (b, i, gate_blk)),
            pl.BlockSpec((1, tm, d), lambda b, i: (b, i, gate_blk + 1)),
            pl.BlockSpec((1, N_MOD, d), lambda b, i: (b, 0, 0)),
            const((w, d)), const((w, d)), const((d, d)), const((1, d)),
            const((LANES, d)), const((LANES, d)), const((LANES, 1)),
        ],
        out_specs=[tok(d), tok(d), tok(ROUTE_COLS), per_tile(ROUTE_COLS, tm), per_tile(LANES, 1), per_tile(LANES, 1)],
        out_shape=[
            jax.ShapeDtypeStruct((bsz, s, d), F32),
            jax.ShapeDtypeStruct((bsz, s, d), BF16),
            jax.ShapeDtypeStruct((bsz, s, ROUTE_COLS), F32),
            jax.ShapeDtypeStruct((tiles, ROUTE_COLS, tm), F32),
            jax.ShapeDtypeStruct((tiles, LANES, 1), F32),
            jax.ShapeDtypeStruct((tiles, LANES, 1), F32),
        ],
        scratch_shapes=[pltpu.VMEM((LANES, 1), F32)],
        compiler_params=_params("arbitrary", "arbitrary"),
        name="merge",
    )(x, o_a, o_b, proj, proj, mod, wbm, wbs, wo, g_ffn.reshape(1, d), wr_hi, wr_lo, b_router)


ROUTE_COLS = 8
SEG_ALIGN = 8


def _local_rows(tm):
    return 2 * tm + N_EXPERTS * SEG_ALIGN


def _route_tile(lg_t, first_tile, rt_ref, slot_t_ref, cnt_ref, before_ref, carry_ref):
    tm = lg_t.shape[1]
    rows = -(-(N_GROUPS + N_EXPERTS) // 8) * 8
    lg = lg_t[:rows]
    row = lax.broadcasted_iota(jnp.int32, (rows, tm), 0).astype(F32)

    @pl.when(first_tile)
    def _init():
        carry_ref[...] = jnp.zeros_like(carry_ref)

    def first_max(vals):
        top = jnp.max(vals, axis=0, keepdims=True)
        idx = jnp.min(jnp.where(vals == top, row, float(rows)), axis=0, keepdims=True)
        return top, idx

    g_logit = jnp.where(row < N_GROUPS, lg, NEG_INF)
    g_top, g_idx = first_max(g_logit)
    g_w = 1.0 / jnp.sum(jnp.exp(g_logit - g_top), axis=0, keepdims=True)
    lo = N_GROUPS + g_idx * EXPERTS_PER_GROUP
    e_logit = jnp.where(jnp.logical_and(row >= lo, row < lo + EXPERTS_PER_GROUP), lg, NEG_INF)
    e_top0, row0 = first_max(e_logit)
    e_top1, row1 = first_max(jnp.where(row == row0, NEG_INF, e_logit))
    z = jnp.exp(e_top1 - e_top0)
    w0 = g_w / (1.0 + z)
    w1 = g_w * z / (1.0 + z)

    hot = jnp.logical_or(row == row0, row == row1)
    t_src = lax.broadcasted_iota(jnp.int32, (tm, tm), 0)
    t_dst = lax.broadcasted_iota(jnp.int32, (tm, tm), 1)
    earlier = jnp.dot(_indicator(hot, BF16), _indicator(t_src < t_dst, BF16), preferred_element_type=F32)
    cnt = jnp.sum(_indicator(hot, F32), axis=1, keepdims=True)
    chunks = jnp.ceil(cnt * (1.0 / SEG_ALIGN))
    spare = jnp.zeros((LANES - rows, LANES), F32)
    e_dst = lax.broadcasted_iota(jnp.int32, (LANES, LANES), 0)
    e_src = lax.broadcasted_iota(jnp.int32, (LANES, LANES), 1)
    chunks_wide = jnp.concatenate([jnp.broadcast_to(chunks, (rows, LANES)), spare], axis=0).astype(BF16)
    seg_start = SEG_ALIGN * jnp.dot(_indicator(e_src < e_dst, BF16), chunks_wide,
                                    preferred_element_type=F32)[:rows, 0:1]
    local = seg_start + earlier
    slot0 = jnp.sum(jnp.where(row == row0, local, 0.0), axis=0, keepdims=True)
    slot1 = jnp.sum(jnp.where(row == row1, local, 0.0), axis=0, keepdims=True)
    seg_rows = jnp.concatenate([SEG_ALIGN * chunks, spare[:, 0:1]], axis=0)
    cnt_ref[0] = seg_rows
    before_ref[0] = carry_ref[...]
    carry_ref[...] += seg_rows

    out_row = lax.broadcasted_iota(jnp.int32, (ROUTE_COLS, tm), 0)
    packed = jnp.zeros((ROUTE_COLS, tm), F32)
    for c, v in enumerate([slot0, slot1, w0, w1]):
        packed = jnp.where(out_row == c, v, packed)
    slot_t_ref[0] = packed
    rt_ref[0] = jnp.concatenate([packed, jnp.zeros((LANES - ROUTE_COLS, tm), F32)], axis=0).T[:, :ROUTE_COLS]


def _for_each_chunk(tile, n_ref, rows_ref, visit):
    max_chunks = rows_ref.shape[0] // n_ref.shape[0]

    def per_chunk(c, carry):
        visit(pl.multiple_of(c * SEG_ALIGN, SEG_ALIGN), pl.multiple_of(rows_ref[tile * max_chunks + c], SEG_ALIGN))
        return carry

    lax.fori_loop(0, n_ref[tile], per_chunk, 0)


def _dispatch_kernel(n_ref, rows_ref, tail_row_ref, tail_n_ref, n_used_ref, slot_t_ref, h_ref, xs_ref,
                     loc_ref, zero_ref, sems, zero_sem):
    tile = pl.program_id(0)
    n_loc, tm = loc_ref.shape[1], h_ref.shape[0]
    buf = tile & 1

    slot = lax.broadcasted_iota(jnp.int32, (n_loc, tm), 0).astype(F32)
    place = jnp.logical_or(slot == slot_t_ref[0, 0:1, :], slot == slot_t_ref[0, 1:2, :])
    loc_ref[buf] = _pack_rows(jnp.dot(_indicator(place, BF16), h_ref[...], preferred_element_type=F32))

    def chunk_copy(b):
        return lambda local, row: pltpu.make_async_copy(
            loc_ref.at[b, pl.ds(local, SEG_ALIGN), :], xs_ref.at[pl.ds(row, SEG_ALIGN), :], sems.at[b])

    _for_each_chunk(tile, n_ref, rows_ref, lambda l, r: chunk_copy(buf)(l, r).start())

    @pl.when(tile > 0)
    def _previous_done():
        _for_each_chunk(tile - 1, n_ref, rows_ref, lambda l, r: chunk_copy(1 - buf)(l, r).wait())

    @pl.when(tile == pl.num_programs(0) - 1)
    def _last_done():
        _for_each_chunk(tile, n_ref, rows_ref, lambda l, r: chunk_copy(buf)(l, r).wait())
        zero_ref[...] = jnp.zeros_like(zero_ref)
        block_rows = zero_ref.shape[0]

        def fill(action):
            def per_expert(e, carry):
                def per_chunk(c, inner):
                    row = pl.multiple_of(tail_row_ref[e] + c * SEG_ALIGN, SEG_ALIGN)
                    action(pltpu.make_async_copy(zero_ref.at[pl.ds(0, SEG_ALIGN), :],
                                                 xs_ref.at[pl.ds(row, SEG_ALIGN), :], zero_sem))
                    return inner

                lax.fori_loop(0, tail_n_ref[e], per_chunk, 0)
                return carry

            lax.fori_loop(0, N_EXPERTS, per_expert, 0)

            def per_block(blk, carry):
                row = pl.multiple_of(blk * block_rows, block_rows)
                action(pltpu.make_async_copy(zero_ref, xs_ref.at[pl.ds(row, block_rows), :], zero_sem))
                return carry

            lax.fori_loop(n_used_ref[0], xs_ref.shape[0] // block_rows, per_block, 0)

        fill(lambda copy: copy.start())
        fill(lambda copy: copy.wait())


def _dispatch(n_chunks, chunk_rows, tail_row, tail_chunks, n_used, slot_t, h2, n_rows):
    n, d = h2.shape
    tm = TOKEN_TILE
    grid_spec = pltpu.PrefetchScalarGridSpec(
        num_scalar_prefetch=5,
        grid=(n // tm,),
        in_specs=[
            pl.BlockSpec((1, ROUTE_COLS, tm), lambda i, *_: (i, 0, 0)),
            pl.BlockSpec((tm, d), lambda i, *_: (i, 0)),
        ],
        out_specs=pl.BlockSpec(memory_space=pl.ANY),
        scratch_shapes=[pltpu.VMEM((2, _local_rows(tm), d // 2), U32), pltpu.VMEM((ROW_BLOCK, d // 2), U32),
                        pltpu.SemaphoreType.DMA((2,)), pltpu.SemaphoreType.DMA(())],
    )
    return pl.pallas_call(
        _dispatch_kernel,
        grid_spec=grid_spec,
        out_shape=jax.ShapeDtypeStruct((n_rows, d // 2), U32),
        compiler_params=_params("arbitrary"),
        name="dispatch",
    )(n_chunks, chunk_rows, tail_row, tail_chunks, n_used, slot_t, h2)


def _expert_kernel(blk_e_ref, n_used_ref, xs_ref, wg_ref, wu_ref, wd_ref, ys_ref, wg_bf, wu_bf, wd_bf):
    i = pl.program_id(0)
    used = i < n_used_ref[0]

    @pl.when(jnp.logical_and(used, jnp.logical_or(i == 0, blk_e_ref[i] != blk_e_ref[jnp.maximum(i - 1, 0)])))
    def _new_expert():
        wg_bf[...] = wg_ref[0].astype(BF16)
        wu_bf[...] = wu_ref[0].astype(BF16)
        wd_bf[...] = wd_ref[0].astype(BF16)

    @pl.when(used)
    def _compute():
        xb = _unpack_rows(xs_ref[...]).astype(BF16)
        g = jnp.dot(xb, wg_bf[...], preferred_element_type=F32)
        u = jnp.dot(xb, wu_bf[...], preferred_element_type=F32)
        hid = (g * jax.nn.sigmoid(g) * u).astype(BF16)
        ys_ref[...] = _pack_rows(jnp.dot(hid, wd_bf[...], preferred_element_type=F32))

    @pl.when(i >= n_used_ref[0])
    def _unused():
        ys_ref[...] = jnp.zeros_like(ys_ref)


def _experts(blk_e, n_used, xs, wg, wu, wd):
    n_rows, half = xs.shape
    _, d, de = wg.shape
    grid_spec = pltpu.PrefetchScalarGridSpec(
        num_scalar_prefetch=2,
        grid=(n_rows // ROW_BLOCK,),
        in_specs=[
            pl.BlockSpec((ROW_BLOCK, half), lambda i, e, u: (jnp.minimum(i, u[0] - 1), 0)),
            pl.BlockSpec((1, d, de), lambda i, e, u: (e[i], 0, 0)),
            pl.BlockSpec((1, d, de), lambda i, e, u: (e[i], 0, 0)),
            pl.BlockSpec((1, de, d), lambda i, e, u: (e[i], 0, 0)),
        ],
        out_specs=pl.BlockSpec((ROW_BLOCK, half), lambda i, e, u: (i, 0)),
        scratch_shapes=[pltpu.VMEM((d, de), BF16), pltpu.VMEM((d, de), BF16), pltpu.VMEM((de, d), BF16)],
    )
    return pl.pallas_call(
        _expert_kernel,
        grid_spec=grid_spec,
        out_shape=jax.ShapeDtypeStruct((n_rows, half), U32),
        compiler_params=_params("arbitrary"),
        name="experts",
    )(blk_e, n_used, xs, wg, wu, wd)


def _combine_kernel(n_ref, rows_ref, x1_ref, mod_ref, rt_ref, ys_ref, o_ref, loc_ref, sems):
    tm = x1_ref.shape[1]
    n_loc = loc_ref.shape[1]
    tile = pl.program_id(0) * pl.num_programs(1) + pl.program_id(1)
    n_tiles = pl.num_programs(0) * pl.num_programs(1)
    buf = tile & 1

    def chunk_copy(b):
        return lambda local, row: pltpu.make_async_copy(
            ys_ref.at[pl.ds(row, SEG_ALIGN), :], loc_ref.at[b, pl.ds(local, SEG_ALIGN), :], sems.at[b])

    @pl.when(tile == 0)
    def _init():
        loc_ref[...] = jnp.zeros_like(loc_ref)
        _for_each_chunk(tile, n_ref, rows_ref, lambda l, r: chunk_copy(buf)(l, r).start())

    @pl.when(tile + 1 < n_tiles)
    def _fetch_next():
        _for_each_chunk(tile + 1, n_ref, rows_ref, lambda l, r: chunk_copy(1 - buf)(l, r).start())

    _for_each_chunk(tile, n_ref, rows_ref, lambda l, r: chunk_copy(buf)(l, r).wait())

    y = _unpack_rows(loc_ref[buf]).astype(BF16)
    rt = rt_ref[0]
    slot = lax.broadcasted_iota(jnp.int32, (tm, n_loc), 1).astype(F32)
    mix = jnp.where(slot == rt[:, 0:1], rt[:, 2:3], jnp.where(slot == rt[:, 1:2], rt[:, 3:4], 0.0)).astype(BF16)
    o_ref[0] = x1_ref[0] + mod_ref[0, 5:6, :] * jnp.dot(mix, y, preferred_element_type=F32)


def _combine(n_chunks, chunk_rows, x1, mod, rt, ys):
    bsz, s, d = x1.shape
    tm = TOKEN_TILE
    grid_spec = pltpu.PrefetchScalarGridSpec(
        num_scalar_prefetch=2,
        grid=(bsz, s // tm),
        in_specs=[
            pl.BlockSpec((1, tm, d), lambda b, i, c, r: (b, i, 0)),
            pl.BlockSpec((1, N_MOD, d), lambda b, i, c, r: (b, 0, 0)),
            pl.BlockSpec((1, tm, ROUTE_COLS), lambda b, i, c, r: (b, i, 0)),
            pl.BlockSpec(memory_space=pl.ANY),
        ],
        out_specs=pl.BlockSpec((1, tm, d), lambda b, i, c, r: (b, i, 0)),
        scratch_shapes=[pltpu.VMEM((2, _local_rows(tm), d // 2), U32), pltpu.SemaphoreType.DMA((2,))],
    )
    return pl.pallas_call(
        _combine_kernel,
        grid_spec=grid_spec,
        out_shape=jax.ShapeDtypeStruct((bsz, s, d), F32),
        compiler_params=_params("arbitrary", "arbitrary"),
        name="combine",
    )(n_chunks, chunk_rows, x1, mod, rt, ys)


def _layer(x, mod, g_mix, w_in, g_q, g_k, rel_bias, w_br_moba, w_br_sb, w_out, g_ffn,
           w_rg, b_rg, w_re, b_re, w_gate, w_up, w_down):
    bsz, s, d = x.shape
    n = bsz * s
    n_blk = s // ATT_BLOCK

    proj, kaug, vaug, kmean = _inproj(x, mod, g_mix, w_in.astype(BF16), g_k)

    top = (n_blk + 1) * ATT_BLOCK - 1
    dist = np.maximum(top - np.arange((n_blk + 2) * ATT_BLOCK), 0)
    rel_rev = rel_bias[:, _rel_bucket_table(top)[dist]].reshape(N_HEADS // HEADS_PER_STEP, HEADS_PER_STEP, -1)

    o_a = _moba(proj, kaug, vaug, kmean, g_q, rel_rev)
    o_b = _stick_breaking(proj)

    pad = LANES - N_GROUPS - N_EXPERTS
    w_router = jnp.concatenate([w_rg, w_re, jnp.zeros((d, pad), F32)], axis=1)
    b_router = jnp.concatenate([b_rg, b_re, jnp.zeros((pad,), F32)]).reshape(LANES, 1)
    x1, h2, rt, slot_t, tile_cnt, tile_before = _merge(x, o_a, o_b, proj, mod, w_br_moba.astype(BF16),
                                                       w_br_sb.astype(BF16), w_out.astype(BF16), g_ffn,
                                                       w_router, b_router)

    experts = slice(N_GROUPS, N_GROUPS + N_EXPERTS)
    cnt = tile_cnt[:, experts, 0].astype(jnp.int32)
    total = jnp.sum(cnt, axis=0)
    padded = (total + ROW_BLOCK - 1) // ROW_BLOCK * ROW_BLOCK
    pend = jnp.cumsum(padded)
    pstart = pend - padded
    max_rows = 2 * n + cnt.shape[0] * N_EXPERTS * (SEG_ALIGN - 1) + N_EXPERTS * (ROW_BLOCK - 1)
    n_blocks = -(-max_rows // ROW_BLOCK)
    blk_row0 = jnp.arange(n_blocks, dtype=jnp.int32) * ROW_BLOCK
    blk_e = jnp.minimum(jnp.sum((blk_row0[:, None] >= pend[None, :]).astype(jnp.int32), axis=1), N_EXPERTS - 1)
    n_used = (pend[-1] // ROW_BLOCK).astype(jnp.int32).reshape(1)
    seg_row0 = pstart[None, :] + tile_before[:, experts, 0].astype(jnp.int32)
    chunks = cnt // SEG_ALIGN
    chunk_end = jnp.cumsum(chunks, axis=1)
    chunk_start = (chunk_end - chunks)[:, None, :]
    c = jnp.arange(_local_rows(TOKEN_TILE) // SEG_ALIGN, dtype=jnp.int32)[None, :, None]
    inside = jnp.logical_and(c >= chunk_start, c < chunk_end[:, None, :])
    chunk_rows = jnp.sum(jnp.where(inside, seg_row0[:, None, :] + SEG_ALIGN * (c - chunk_start), 0), axis=2).reshape(-1)
    n_chunks = chunk_end[:, -1]

    xs = _dispatch(n_chunks, chunk_rows, pstart + total, (padded - total) // SEG_ALIGN, n_used, slot_t,
                   h2.reshape(n, d), n_blocks * ROW_BLOCK)
    ys = _experts(blk_e, n_used, xs, w_gate, w_up, w_down)
    return _combine(n_chunks, chunk_rows, x1, mod, rt, ys)


def kernel(x, c, w_ada, b_ada, g_mix, w_in, g_q, g_k, rel_bias, w_br_moba, w_br_sb, w_out, g_ffn,
           w_rg, b_rg, w_re, b_re, w_gate, w_up, w_down):
    bsz, s, d = x.shape
    assert s % TOKEN_TILE == 0 and s % MOBA_Q_TILE == 0 and s % (SB_Q_BLOCKS * ATT_BLOCK) == 0
    assert d % (2 * LANES) == 0
    for l in range(w_ada.shape[0]):
        mod = _adaln(c, w_ada[l], b_ada[l]).reshape(bsz, N_MOD, d)
        x = _layer(x, mod, g_mix[l], w_in[l], g_q[l], g_k[l], rel_bias, w_br_moba[l], w_br_sb[l], w_out[l],
                   g_ffn[l], w_rg[l], b_rg[l], w_re[l], b_re[l], w_gate[l], w_up[l], w_down[l])
    return x
```

```python
import functools
import math

import numpy as np
import jax
import jax.numpy as jnp
from jax import lax
from jax.experimental import pallas as pl
from jax.experimental.pallas import tpu as pltpu

F32 = jnp.float32
BF16 = jnp.bfloat16
U32 = jnp.uint32
HIGHEST = lax.Precision.HIGHEST

LANES = 128
VMEM_LIMIT_BYTES = 56 * 1024 * 1024

HEAD_DIM = 64
N_HEADS = 8
HEADS_PER_STEP = LANES // HEAD_DIM
ATT_BLOCK = 256
MOBA_Q_TILE = 2 * ATT_BLOCK
SB_Q_BLOCKS = 4
MOBA_TOPK = 3
REL_BUCKETS = 32
REL_MAX_DIST = 1024
N_GROUPS = 4
EXPERTS_PER_GROUP = 8
N_EXPERTS = N_GROUPS * EXPERTS_PER_GROUP
N_MOD = 6
RMS_EPS = 1e-6
ROW_BLOCK = 512
TOKEN_TILE = 512
NEG_INF = float("-inf")
MASK_LOGIT = -1e30
SB_EXIT_SUM = 110.0

_NT = (((1,), (1,)), ((), ()))


def _params(*semantics):
    return pltpu.CompilerParams(dimension_semantics=semantics, vmem_limit_bytes=VMEM_LIMIT_BYTES)


def _adaln_kernel(c_ref, w_ref, b_ref, o_ref):
    c = c_ref[...]
    s = c * jax.nn.sigmoid(c)
    o_ref[...] = jnp.dot(s, w_ref[...], preferred_element_type=F32, precision=HIGHEST) + b_ref[...]


def _adaln(c, w, b):
    bsz, d = c.shape
    n = w.shape[1]
    tn = 1536
    return pl.pallas_call(
        _adaln_kernel,
        grid=(n // tn,),
        in_specs=[
            pl.BlockSpec((bsz, d), lambda j: (0, 0)),
            pl.BlockSpec((d, tn), lambda j: (0, j)),
            pl.BlockSpec((1, tn), lambda j: (0, j)),
        ],
        out_specs=pl.BlockSpec((bsz, tn), lambda j: (0, j)),
        out_shape=jax.ShapeDtypeStruct((bsz, n), F32),
        compiler_params=_params("arbitrary"),
        name="adaln",
    )(c, w, b.reshape(1, n))


def _rms_modulate(x, g, shift, scale):
    y = x * lax.rsqrt(jnp.mean(x * x, axis=-1, keepdims=True) + RMS_EPS) * g
    return y * (1.0 + scale) + shift


def _inproj_kernel(x_ref, mod_ref, g_ref, w_ref, gk_ref, o_ref, kaug_ref, vaug_ref, kmean_ref, *, col_chunk):
    i = pl.program_id(1)
    tm = x_ref.shape[1]
    h = _rms_modulate(x_ref[0], g_ref[...], mod_ref[0, 0:1, :], mod_ref[0, 1:2, :]).astype(BF16)

    w_moba = N_HEADS * HEAD_DIM
    blocks = tm // ATT_BLOCK
    first = _head_lane_mask()
    row_blk = blocks * i + lax.shift_right_logical(lax.broadcasted_iota(jnp.int32, (tm, LANES), 0),
                                                   ATT_BLOCK.bit_length() - 1)
    tag = jnp.where(_lane_index() - float(HEAD_DIM) == row_blk.astype(F32), 1.0, 0.0)
    mean_row = lax.broadcasted_iota(jnp.int32, kmean_ref.shape[2:], 0)

    @pl.when(i == 0)
    def _init():
        kmean_ref[...] = jnp.zeros_like(kmean_ref)

    for n in range(w_ref.shape[1] // col_chunk):
        cols = slice(n * col_chunk, (n + 1) * col_chunk)
        r = jnp.dot(h, w_ref[:, cols], preferred_element_type=F32)
        o_ref[0, :, cols] = r.astype(BF16)
        for c in range(col_chunk // LANES):
            col0 = n * col_chunk + c * LANES
            part = r[:, c * LANES:(c + 1) * LANES]
            if w_moba <= col0 < 2 * w_moba:
                kn = _head_rmsnorm(part, gk_ref[...])
                for hh, kh in enumerate((kn, pltpu.roll(kn, HEAD_DIM, 1))):
                    head = (col0 - w_moba) // HEAD_DIM + hh
                    kaug_ref[0, head] = jnp.where(first, kh, tag).astype(BF16)
                    means = kmean_ref[0, head]
                    for jb in range(blocks):
                        mean = jnp.mean(kh[jb * ATT_BLOCK:(jb + 1) * ATT_BLOCK], axis=0, keepdims=True)
                        means = jnp.where(mean_row == blocks * i + jb, jnp.where(first, mean, 0.0), means)
                    kmean_ref[0, head] = means
            elif 2 * w_moba <= col0 < 3 * w_moba:
                for hh, vh in enumerate((part, pltpu.roll(part, HEAD_DIM, 1))):
                    head = (col0 - 2 * w_moba) // HEAD_DIM + hh
                    vaug_ref[0, head] = jnp.where(first, vh, 1.0).astype(BF16)


def _inproj(x, mod, g, w_bf16, g_k):
    bsz, s, d = x.shape
    n = w_bf16.shape[1]
    tm = TOKEN_TILE
    n_blk = s // ATT_BLOCK
    per_head = lambda rows: pl.BlockSpec((1, N_HEADS, rows, LANES), lambda b, i: (b, 0, i, 0))
    return pl.pallas_call(
        functools.partial(_inproj_kernel, col_chunk=1024),
        grid=(bsz, s // tm),
        in_specs=[
            pl.BlockSpec((1, tm, d), lambda b, i: (b, i, 0)),
            pl.BlockSpec((1, N_MOD, d), lambda b, i: (b, 0, 0)),
            pl.BlockSpec((1, d), lambda b, i: (0, 0)),
            pl.BlockSpec((d, n), lambda b, i: (0, 0)),
            pl.BlockSpec((1, LANES), lambda b, i: (0, 0)),
        ],
        out_specs=[
            pl.BlockSpec((1, tm, n), lambda b, i: (b, i, 0)),
            per_head(tm),
            per_head(tm),
            pl.BlockSpec((1, N_HEADS, -(-n_blk // 8) * 8, LANES), lambda b, i: (b, 0, 0, 0)),
        ],
        out_shape=[
            jax.ShapeDtypeStruct((bsz, s, n), BF16),
            jax.ShapeDtypeStruct((bsz, N_HEADS, s, LANES), BF16),
            jax.ShapeDtypeStruct((bsz, N_HEADS, s, LANES), BF16),
            jax.ShapeDtypeStruct((bsz, N_HEADS, -(-n_blk // 8) * 8, LANES), F32),
        ],
        compiler_params=_params("arbitrary", "arbitrary"),
        name="inproj",
    )(x, mod, g.reshape(1, d), w_bf16, jnp.tile(g_k.reshape(1, HEAD_DIM), (1, HEADS_PER_STEP)))


def _indicator(cond, dtype):
    return jnp.where(cond, 1.0, 0.0).astype(dtype)


def _lane_index():
    return lax.broadcasted_iota(jnp.int32, (1, LANES), 1).astype(F32)


def _head_lane_mask():
    return lax.broadcasted_iota(jnp.int32, (1, LANES), 1) < HEAD_DIM


def _head_rmsnorm(t, g):
    first = _head_lane_mask()
    sq = t * t
    ss0 = jnp.sum(jnp.where(first, sq, 0.0), axis=-1, keepdims=True)
    ss1 = jnp.sum(jnp.where(first, 0.0, sq), axis=-1, keepdims=True)
    inv = jnp.where(first, lax.rsqrt(ss0 / HEAD_DIM + RMS_EPS), lax.rsqrt(ss1 / HEAD_DIM + RMS_EPS))
    return t * inv * g


def _split_heads(t):
    first = _head_lane_mask()
    zero = jnp.zeros_like(t)
    return jnp.where(first, t, zero), jnp.where(first, zero, t)


def _rel_bucket_table(max_dist):
    n = np.arange(max_dist + 1)
    max_exact = REL_BUCKETS // 2
    nf = np.maximum(n, 1).astype(np.float64)
    large = max_exact + (np.log(nf / max_exact) / math.log(REL_MAX_DIST / max_exact)
                         * (REL_BUCKETS - max_exact)).astype(np.int64)
    large = np.minimum(large, REL_BUCKETS - 1)
    return np.where(n < max_exact, n, large).astype(np.int32)


def _moba_kernel(q_ref, kaug_ref, vaug_ref, kmean_ref, gq_ref, rb_ref, o_ref,
                 toep_ref, s_ref, mvec_ref, acc_ref, *, n_blk):
    b = pl.program_id(1)
    qi = pl.program_id(2)
    blk = ATT_BLOCK
    qt = q_ref.shape[1]
    halves = qt // blk
    first = _head_lane_mask()

    @pl.when(jnp.logical_and(b == 0, qi == 0))
    def _build_bias_tiles():
        for h in range(HEADS_PER_STEP):
            for d in range(n_blk):
                r = rb_ref[0, h:h + 1, (n_blk - d) * blk:(n_blk - d + 2) * blk]
                rolled = pltpu.roll(jnp.broadcast_to(r, (blk, 2 * blk)), blk + 1, 1, stride=1, stride_axis=0)
                toep_ref[h, d] = rolled[:, :blk]

    qn = _head_rmsnorm(q_ref[0].astype(F32), gq_ref[...])
    gate_rows = kmean_ref.shape[2]
    blk_row = lax.broadcasted_iota(jnp.int32, (gate_rows, qt), 0).astype(F32)
    own = (halves * qi + lax.shift_right_logical(lax.broadcasted_iota(jnp.int32, (gate_rows, qt), 1),
                                                 blk.bit_length() - 1)).astype(F32)
    q_aug = []
    for h, qh in enumerate((qn, pltpu.roll(qn, HEAD_DIM, 1))):
        qh = jnp.where(first, qh, 0.0)
        km = kmean_ref[0, h]
        km_hi, q_hi = km.astype(BF16), qh.astype(BF16)
        km_lo, q_lo = (km - km_hi.astype(F32)).astype(BF16), (qh - q_hi.astype(F32)).astype(BF16)
        g = (lax.dot_general(km_hi, q_hi, _NT, preferred_element_type=F32)
             + lax.dot_general(km_lo, q_hi, _NT, preferred_element_type=F32)
             + lax.dot_general(km_hi, q_lo, _NT, preferred_element_type=F32))
        g = jnp.where(blk_row < own, g, NEG_INF)
        keep = jnp.where(blk_row == own, 1.0, 0.0)
        for _ in range(MOBA_TOPK):
            top = jnp.max(g, axis=0, keepdims=True)
            is_top = jnp.logical_and(g == top, top > NEG_INF)
            idx = jnp.min(jnp.where(is_top, blk_row, float(gate_rows)), axis=0, keepdims=True)
            pick = blk_row == idx
            keep = jnp.where(pick, 1.0, keep)
            g = jnp.where(pick, NEG_INF, g)
        mask_t = jnp.where(keep > 0.0, 0.0, MASK_LOGIT)
        mask_logit = jnp.concatenate([jnp.zeros((HEAD_DIM, qt), F32), mask_t,
                                      jnp.zeros((LANES - HEAD_DIM - gate_rows, qt), F32)], axis=0).T
        q_aug.append(jnp.where(first, qh * (HEAD_DIM ** -0.5), mask_logit).astype(BF16))

    def lane_max(s):
        return jnp.maximum(s[:, :LANES], s[:, LANES:])

    def scores(h, t, dist):
        k_t = kaug_ref[0, h, pl.ds(pl.multiple_of(t * qt, qt), qt), :]
        s = lax.dot_general(q_aug[h], k_t, _NT, preferred_element_type=F32)
        tiles = []
        for kb in range(halves):
            rows = []
            for qb in range(halves):
                part = s[qb * blk:(qb + 1) * blk, kb * blk:(kb + 1) * blk]
                if isinstance(dist, int) and halves * dist + qb - kb < 0:
                    rows.append(part)
                else:
                    rows.append(part + toep_ref[h, halves * dist + qb - kb])
            tiles.append(jnp.concatenate(rows, axis=0))
        return tiles

    row = lax.broadcasted_iota(jnp.int32, (qt, blk), 0)
    col = lax.broadcasted_iota(jnp.int32, (qt, blk), 1)
    for h in range(HEADS_PER_STEP):
        top = None
        for kb, s in enumerate(scores(h, qi, 0)):
            s = jnp.where(col + kb * blk <= row, s, NEG_INF)
            s_ref[h, halves * qi + kb] = s
            top = lane_max(s) if top is None else jnp.maximum(top, lane_max(s))
        mvec_ref[h] = top

    def in_pairs(count, visit):
        def quad(p, carry):
            visit(4 * p, 4)
            return carry

        lax.fori_loop(0, lax.shift_right_logical(count, 2), quad, 0)

        @pl.when((count & 2) == 2)
        def _pair():
            visit(count & ~3, 2)

        @pl.when((count & 1) == 1)
        def _last():
            visit(count - 1, 1)

    def pass1(t0, n_tiles):
        for h in range(HEADS_PER_STEP):
            top = mvec_ref[h]
            for t in [t0 + u for u in range(n_tiles)]:
                for kb, s in enumerate(scores(h, t, qi - t)):
                    s_ref[h, halves * t + kb] = s
                    top = jnp.maximum(top, lane_max(s))
            mvec_ref[h] = top

    in_pairs(qi, pass1)

    for h in range(HEADS_PER_STEP):
        m = jnp.max(mvec_ref[h], axis=-1, keepdims=True)
        mvec_ref[h] = jnp.broadcast_to(m, (qt, LANES))
        acc_ref[h] = jnp.zeros((qt, LANES), F32)

    def pass2(t0, n_tiles):
        keys = pl.ds(pl.multiple_of(t0 * qt, qt), n_tiles * qt)
        for h in range(HEADS_PER_STEP):
            m = mvec_ref[h]
            parts = []
            for kb in range(n_tiles * halves):
                s = s_ref[h, halves * t0 + kb]
                parts += [jnp.exp(s[:, :LANES] - m), jnp.exp(s[:, LANES:] - m)]
            p = jnp.concatenate(parts, axis=1).astype(BF16)
            acc_ref[h] += jnp.dot(p, vaug_ref[0, h, keys, :], preferred_element_type=F32)

    in_pairs(qi + 1, pass2)

    out = [acc_ref[h] / acc_ref[h][:, HEAD_DIM:HEAD_DIM + 1] for h in range(HEADS_PER_STEP)]
    o_ref[0] = jnp.where(first, out[0], pltpu.roll(out[1], HEAD_DIM, 1)).astype(o_ref.dtype)


def _moba(proj, kaug, vaug, kmean, g_q, rel_rev):
    bsz, s, _ = proj.shape
    blk = ATT_BLOCK
    n_blk = s // blk
    n_hp = N_HEADS // HEADS_PER_STEP
    w = N_HEADS * HEAD_DIM
    qt = MOBA_Q_TILE
    per_pair = lambda arr: pl.BlockSpec((1, HEADS_PER_STEP) + arr.shape[2:], lambda hp, b, i: (b, hp, 0, 0))
    return pl.pallas_call(
        functools.partial(_moba_kernel, n_blk=n_blk),
        grid=(n_hp, bsz, s // qt),
        in_specs=[
            pl.BlockSpec((1, qt, LANES), lambda hp, b, i: (b, i, hp)),
            per_pair(kaug), per_pair(vaug), per_pair(kmean),
            pl.BlockSpec((1, LANES), lambda hp, b, i: (0, 0)),
            pl.BlockSpec((1, HEADS_PER_STEP, rel_rev.shape[-1]), lambda hp, b, i: (hp, 0, 0)),
        ],
        out_specs=pl.BlockSpec((1, qt, LANES), lambda hp, b, i: (b, i, hp)),
        out_shape=jax.ShapeDtypeStruct((bsz, s, w), BF16),
        scratch_shapes=[
            pltpu.VMEM((HEADS_PER_STEP, n_blk, blk, blk), F32),
            pltpu.VMEM((HEADS_PER_STEP, n_blk, qt, blk), F32),
            pltpu.VMEM((HEADS_PER_STEP, qt, LANES), F32),
            pltpu.VMEM((HEADS_PER_STEP, qt, LANES), F32),
        ],
        compiler_params=_params("arbitrary", "arbitrary", "arbitrary"),
        name="moba",
    )(proj, kaug, vaug, kmean, jnp.tile(g_q.reshape(1, HEAD_DIM), (1, HEADS_PER_STEP)), rel_rev)


def _sb_kernel(q_ref, k_ref, v_ref, o_ref, acc_ref, carry_ref):
    ti = pl.program_id(2)
    blk = ATT_BLOCK
    n_sub = q_ref.shape[1] // blk
    first = n_sub * ti
    row = lax.broadcasted_iota(jnp.int32, (blk, blk), 0)
    col = lax.broadcasted_iota(jnp.int32, (blk, blk), 1)
    past = col < row
    later = _indicator(row > col, BF16)

    q_heads = []
    for sub in range(n_sub):
        heads = _split_heads(q_ref[0, sub * blk:(sub + 1) * blk, :].astype(F32) * (HEAD_DIM ** -0.5))
        q_heads.append([t.astype(BF16) for t in heads])

    def walk(jobs, fresh):
        for sub, blocks in jobs:
            for h in range(HEADS_PER_STEP):
                carry = None if fresh else carry_ref[sub, h]
                total = None if fresh else acc_ref[sub, h]
                for j, diagonal in blocks:
                    start = j * blk if isinstance(j, int) else pl.multiple_of(j * blk, blk)
                    z = lax.dot_general(q_heads[sub][h], k_ref[0, pl.ds(start, blk), :], _NT,
                                        preferred_element_type=F32)
                    sp = jnp.maximum(z, 0.0) + jnp.log(1.0 + jnp.exp(-jnp.abs(z)))
                    masked = jnp.where(past, sp, 0.0) if diagonal else sp
                    after = jnp.dot(masked.astype(BF16), later, preferred_element_type=F32)
                    row_sum = after[:, 0:1] + masked[:, 0:1]
                    a = jnp.exp(z - sp - after)
                    if diagonal:
                        a = jnp.where(past, a, 0.0)
                    pv = jnp.dot(a.astype(BF16), v_ref[0, pl.ds(start, blk), :], preferred_element_type=F32)
                    if carry is not None:
                        pv = pv * jnp.exp(-carry)
                    total = pv if total is None else total + pv
                    carry = row_sum if carry is None else carry + row_sum
                carry_ref[sub, h] = carry
                acc_ref[sub, h] = total

    @pl.when(ti == 0)
    def _first_step():
        walk([(sub, [(sub, True)] + ([(sub - 1, False)] if sub else [])) for sub in range(n_sub)], True)

    @pl.when(ti > 0)
    def _own_and_previous():
        walk([(sub, [(first + sub, True), (first + sub - 1, False)]) for sub in range(n_sub)], True)

    def slowest_decay(subs):
        worst = functools.reduce(jnp.minimum, [carry_ref[sub, h] for sub in subs for h in range(HEADS_PER_STEP)])
        return jnp.min(worst)

    @pl.when(slowest_decay(range(n_sub)) < SB_EXIT_SUM)
    def _walk_further_back():
        for sub in range(n_sub):
            def cond(state):
                j, worst = state
                return jnp.logical_and(j >= 0, worst < SB_EXIT_SUM)

            def body(state, sub=sub):
                j, _ = state
                walk([(sub, [(j, False)])], False)
                return j - 1, slowest_decay([sub])

            lax.while_loop(cond, body, (first + sub - 2, slowest_decay([sub])))

    for sub in range(n_sub):
        o_ref[0, sub * blk:(sub + 1) * blk, :] = jnp.where(
            _head_lane_mask(), acc_ref[sub, 0], acc_ref[sub, 1]).astype(o_ref.dtype)


def _stick_breaking(proj):
    bsz, s, _ = proj.shape
    blk = ATT_BLOCK
    qt = SB_Q_BLOCKS * blk
    n_hp = N_HEADS // HEADS_PER_STEP
    w = N_HEADS * HEAD_DIM
    base = 3 * n_hp
    return pl.pallas_call(
        _sb_kernel,
        grid=(n_hp, bsz, s // qt),
        in_specs=[
            pl.BlockSpec((1, qt, LANES), lambda hp, b, i: (b, i, base + hp)),
            pl.BlockSpec((1, s, LANES), lambda hp, b, i: (b, 0, base + n_hp + hp)),
            pl.BlockSpec((1, s, LANES), lambda hp, b, i: (b, 0, base + 2 * n_hp + hp)),
        ],
        out_specs=pl.BlockSpec((1, qt, LANES), lambda hp, b, i: (b, i, hp)),
        out_shape=jax.ShapeDtypeStruct((bsz, s, w), BF16),
        scratch_shapes=[
            pltpu.VMEM((SB_Q_BLOCKS, HEADS_PER_STEP, blk, LANES), F32),
            pltpu.VMEM((SB_Q_BLOCKS, HEADS_PER_STEP, blk, 1), F32),
        ],
        compiler_params=_params("arbitrary", "arbitrary", "arbitrary"),
        name="stickbrk",
    )(proj, proj, proj)


def _pack_rows(t):
    n = t.shape[1] // 2
    bits = pltpu.bitcast(t.astype(BF16).astype(F32), U32)
    return bits[:, :n] | (bits[:, n:] >> 16)


def _unpack_rows(p):
    hi = pltpu.bitcast(p & jnp.uint32(0xFFFF0000), F32)
    lo = pltpu.bitcast(p << 16, F32)
    return jnp.concatenate([hi, lo], axis=1)


def _merge_kernel(x_ref, oa_ref, ob_ref, ga_ref, gb_ref, mod_ref, wbm_ref, wbs_ref, wo_ref, gf_ref,
                  wrh_ref, wrl_ref, br_ref, x1_ref, h2_ref, rt_ref, slot_t_ref, cnt_ref, before_ref, carry_ref):
    ma = jnp.dot(oa_ref[0], wbm_ref[...], preferred_element_type=F32)
    mb = jnp.dot(ob_ref[0], wbs_ref[...], preferred_element_type=F32)
    merged = (jax.nn.sigmoid(ga_ref[0].astype(F32)) * ma + jax.nn.sigmoid(gb_ref[0].astype(F32)) * mb)
    t = jnp.dot(merged.astype(BF16), wo_ref[...], preferred_element_type=F32)
    x1 = x_ref[0] + mod_ref[0, 2:3, :] * t
    x1_ref[0] = x1
    h2 = _rms_modulate(x1, gf_ref[...], mod_ref[0, 3:4, :], mod_ref[0, 4:5, :])
    hi = h2.astype(BF16)
    h2_ref[0] = hi
    lo = (h2 - hi.astype(F32)).astype(BF16)
    logits = (lax.dot_general(wrh_ref[...], hi, _NT, preferred_element_type=F32)
              + lax.dot_general(wrh_ref[...], lo, _NT, preferred_element_type=F32)
              + lax.dot_general(wrl_ref[...], hi, _NT, preferred_element_type=F32) + br_ref[...])
    first_tile = jnp.logical_and(pl.program_id(0) == 0, pl.program_id(1) == 0)
    _route_tile(logits, first_tile, rt_ref, slot_t_ref, cnt_ref, before_ref, carry_ref)


def _merge(x, o_a, o_b, proj, mod, wbm, wbs, wo, g_ffn, w_router, b_router):
    bsz, s, d = x.shape
    w = o_a.shape[-1]
    tm = TOKEN_TILE
    gate_blk = (proj.shape[-1] - 2 * d) // d
    const = lambda shape: pl.BlockSpec(shape, lambda b, i: (0,) * len(shape))
    tok = lambda width: pl.BlockSpec((1, tm, width), lambda b, i: (b, i, 0))
    wr_hi = w_router.T.astype(BF16)
    wr_lo = (w_router.T - wr_hi.astype(F32)).astype(BF16)
    per_b = s // tm
    tiles = bsz * per_b
    per_tile = lambda rows, cols: pl.BlockSpec((1, rows, cols), lambda b, i: (b * per_b + i, 0, 0))
    return pl.pallas_call(
        _merge_kernel,
        grid=(bsz, s // tm),
        in_specs=[
            tok(d), tok(w), tok(w),
            pl.BlockSpec((1, tm, d), lambda b, i: (b, i, gate_blk)),
            pl.BlockSpec((1, tm, d), lambda b, i: (b, i, gate_blk + 1)),
            pl.BlockSpec((1, N_MOD, d), lambda b, i: (b, 0, 0)),
            const((w, d)), const((w, d)), const((d, d)), const((1, d)),
            const((LANES, d)), const((LANES, d)), const((LANES, 1)),
        ],
        out_specs=[tok(d), tok(d), tok(ROUTE_COLS), per_tile(ROUTE_COLS, tm), per_tile(LANES, 1), per_tile(LANES, 1)],
        out_shape=[
            jax.ShapeDtypeStruct((bsz, s, d), F32),
            jax.ShapeDtypeStruct((bsz, s, d), BF16),
            jax.ShapeDtypeStruct((bsz, s, ROUTE_COLS), F32),
            jax.ShapeDtypeStruct((tiles, ROUTE_COLS, tm), F32),
            jax.ShapeDtypeStruct((tiles, LANES, 1), F32),
            jax.ShapeDtypeStruct((tiles, LANES, 1), F32),
        ],
        scratch_shapes=[pltpu.VMEM((LANES, 1), F32)],
        compiler_params=_params("arbitrary", "arbitrary"),
        name="merge",
    )(x, o_a, o_b, proj, proj, mod, wbm, wbs, wo, g_ffn.reshape(1, d), wr_hi, wr_lo, b_router)


ROUTE_COLS = 8
SEG_ALIGN = 8


def _local_rows(tm):
    return 2 * tm + N_EXPERTS * SEG_ALIGN


def _route_tile(lg_t, first_tile, rt_ref, slot_t_ref, cnt_ref, before_ref, carry_ref):
    tm = lg_t.shape[1]
    rows = -(-(N_GROUPS + N_EXPERTS) // 8) * 8
    lg = lg_t[:rows]
    row = lax.broadcasted_iota(jnp.int32, (rows, tm), 0).astype(F32)

    @pl.when(first_tile)
    def _init():
        carry_ref[...] = jnp.zeros_like(carry_ref)

    def first_max(vals):
        top = jnp.max(vals, axis=0, keepdims=True)
        idx = jnp.min(jnp.where(vals == top, row, float(rows)), axis=0, keepdims=True)
        return top, idx

    g_logit = jnp.where(row < N_GROUPS, lg, NEG_INF)
    g_top, g_idx = first_max(g_logit)
    g_w = 1.0 / jnp.sum(jnp.exp(g_logit - g_top), axis=0, keepdims=True)
    lo = N_GROUPS + g_idx * EXPERTS_PER_GROUP
    e_logit = jnp.where(jnp.logical_and(row >= lo, row < lo + EXPERTS_PER_GROUP), lg, NEG_INF)
    e_top0, row0 = first_max(e_logit)
    e_top1, row1 = first_max(jnp.where(row == row0, NEG_INF, e_logit))
    z = jnp.exp(e_top1 - e_top0)
    w0 = g_w / (1.0 + z)
    w1 = g_w * z / (1.0 + z)

    hot = jnp.logical_or(row == row0, row == row1)
    t_src = lax.broadcasted_iota(jnp.int32, (tm, tm), 0)
    t_dst = lax.broadcasted_iota(jnp.int32, (tm, tm), 1)
    earlier = jnp.dot(_indicator(hot, BF16), _indicator(t_src < t_dst, BF16), preferred_element_type=F32)
    cnt = jnp.sum(_indicator(hot, F32), axis=1, keepdims=True)
    chunks = jnp.ceil(cnt * (1.0 / SEG_ALIGN))
    spare = jnp.zeros((LANES - rows, LANES), F32)
    e_dst = lax.broadcasted_iota(jnp.int32, (LANES, LANES), 0)
    e_src = lax.broadcasted_iota(jnp.int32, (LANES, LANES), 1)
    chunks_wide = jnp.concatenate([jnp.broadcast_to(chunks, (rows, LANES)), spare], axis=0).astype(BF16)
    seg_start = SEG_ALIGN * jnp.dot(_indicator(e_src < e_dst, BF16), chunks_wide,
                                    preferred_element_type=F32)[:rows, 0:1]
    local = seg_start + earlier
    slot0 = jnp.sum(jnp.where(row == row0, local, 0.0), axis=0, keepdims=True)
    slot1 = jnp.sum(jnp.where(row == row1, local, 0.0), axis=0, keepdims=True)
    seg_rows = jnp.concatenate([SEG_ALIGN * chunks, spare[:, 0:1]], axis=0)
    cnt_ref[0] = seg_rows
    before_ref[0] = carry_ref[...]
    carry_ref[...] += seg_rows

    out_row = lax.broadcasted_iota(jnp.int32, (ROUTE_COLS, tm), 0)
    packed = jnp.zeros((ROUTE_COLS, tm), F32)
    for c, v in enumerate([slot0, slot1, w0, w1]):
        packed = jnp.where(out_row == c, v, packed)
    slot_t_ref[0] = packed
    rt_ref[0] = jnp.concatenate([packed, jnp.zeros((LANES - ROUTE_COLS, tm), F32)], axis=0).T[:, :ROUTE_COLS]


BIG_CHUNK = 4 * SEG_ALIGN


def _for_each_chunk(tile, plan, visit):
    for (n_ref, local_ref, global_ref), n_rows in zip(plan, (BIG_CHUNK, SEG_ALIGN)):
        per_tile = local_ref.shape[0] // n_ref.shape[0]

        def one(c, carry, local_ref=local_ref, global_ref=global_ref, per_tile=per_tile, n_rows=n_rows):
            k = tile * per_tile + c
            visit(pl.multiple_of(local_ref[k], SEG_ALIGN), pl.multiple_of(global_ref[k], SEG_ALIGN), n_rows)
            return carry

        lax.fori_loop(0, n_ref[tile], one, 0)


def _copy_plan(cnt, seg_row0, local_rows):
    per_big = BIG_CHUNK // SEG_ALIGN
    units = cnt // SEG_ALIGN
    local0 = SEG_ALIGN * (jnp.cumsum(units, axis=1) - units)
    n_big = units // per_big
    n_small = units - per_big * n_big

    def lists(n, local_base, global_base, size, max_entries):
        end = jnp.cumsum(n, axis=1)
        start = (end - n)[:, None, :]
        c = jnp.arange(max_entries, dtype=jnp.int32)[None, :, None]
        inside = jnp.logical_and(c >= start, c < end[:, None, :])
        pick = lambda base: jnp.sum(jnp.where(inside, base[:, None, :] + size * (c - start), 0), axis=2).reshape(-1)
        return end[:, -1], pick(local_base), pick(global_base)

    big = lists(n_big, local0, seg_row0, BIG_CHUNK, local_rows // BIG_CHUNK)
    small = lists(n_small, local0 + BIG_CHUNK * n_big, seg_row0 + BIG_CHUNK * n_big, SEG_ALIGN,
                  N_EXPERTS * (per_big - 1))
    return big + small


def _dispatch_kernel(nb_ref, bl_ref, bg_ref, ns_ref, sl_ref, sg_ref, tail_row_ref, tail_n_ref, n_used_ref,
                     slot_t_ref, h_ref, xs_ref, loc_ref, zero_ref, sems, zero_sem):
    plan = ((nb_ref, bl_ref, bg_ref), (ns_ref, sl_ref, sg_ref))
    tile = pl.program_id(0)
    n_loc, tm = loc_ref.shape[1], h_ref.shape[0]
    buf = tile & 1

    slot = lax.broadcasted_iota(jnp.int32, (n_loc, tm), 0).astype(F32)
    place = jnp.logical_or(slot == slot_t_ref[0, 0:1, :], slot == slot_t_ref[0, 1:2, :])
    loc_ref[buf] = _pack_rows(jnp.dot(_indicator(place, BF16), h_ref[...], preferred_element_type=F32))

    def chunk_copy(b):
        return lambda local, row, n: pltpu.make_async_copy(
            loc_ref.at[b, pl.ds(local, n), :], xs_ref.at[pl.ds(row, n), :], sems.at[b])

    _for_each_chunk(tile, plan, lambda l, r, n: chunk_copy(buf)(l, r, n).start())

    @pl.when(tile > 0)
    def _previous_done():
        _for_each_chunk(tile - 1, plan, lambda l, r, n: chunk_copy(1 - buf)(l, r, n).wait())

    @pl.when(tile == pl.num_programs(0) - 1)
    def _last_done():
        _for_each_chunk(tile, plan, lambda l, r, n: chunk_copy(buf)(l, r, n).wait())
        zero_ref[...] = jnp.zeros_like(zero_ref)
        block_rows = zero_ref.shape[0]

        def fill(action):
            def per_expert(e, carry):
                def per_chunk(c, inner):
                    row = pl.multiple_of(tail_row_ref[e] + c * SEG_ALIGN, SEG_ALIGN)
                    action(pltpu.make_async_copy(zero_ref.at[pl.ds(0, SEG_ALIGN), :],
                                                 xs_ref.at[pl.ds(row, SEG_ALIGN), :], zero_sem))
                    return inner

                lax.fori_loop(0, tail_n_ref[e], per_chunk, 0)
                return carry

            lax.fori_loop(0, N_EXPERTS, per_expert, 0)

            def per_block(blk, carry):
                row = pl.multiple_of(blk * block_rows, block_rows)
                action(pltpu.make_async_copy(zero_ref, xs_ref.at[pl.ds(row, block_rows), :], zero_sem))
                return carry

            lax.fori_loop(n_used_ref[0], xs_ref.shape[0] // block_rows, per_block, 0)

        fill(lambda copy: copy.start())
        fill(lambda copy: copy.wait())


def _dispatch(plan, tail_row, tail_chunks, n_used, slot_t, h2, n_rows):
    n, d = h2.shape
    tm = TOKEN_TILE
    grid_spec = pltpu.PrefetchScalarGridSpec(
        num_scalar_prefetch=len(plan) + 3,
        grid=(n // tm,),
        in_specs=[
            pl.BlockSpec((1, ROUTE_COLS, tm), lambda i, *_: (i, 0, 0)),
            pl.BlockSpec((tm, d), lambda i, *_: (i, 0)),
        ],
        out_specs=pl.BlockSpec(memory_space=pl.ANY),
        scratch_shapes=[pltpu.VMEM((2, _local_rows(tm), d // 2), U32), pltpu.VMEM((ROW_BLOCK, d // 2), U32),
                        pltpu.SemaphoreType.DMA((2,)), pltpu.SemaphoreType.DMA(())],
    )
    return pl.pallas_call(
        _dispatch_kernel,
        grid_spec=grid_spec,
        out_shape=jax.ShapeDtypeStruct((n_rows, d // 2), U32),
        compiler_params=_params("arbitrary"),
        name="dispatch",
    )(*plan, tail_row, tail_chunks, n_used, slot_t, h2)


def _expert_kernel(blk_e_ref, n_used_ref, xs_ref, wg_ref, wu_ref, wd_ref, ys_ref, wg_bf, wu_bf, wd_bf):
    i = pl.program_id(0)
    used = i < n_used_ref[0]

    @pl.when(jnp.logical_and(used, jnp.logical_or(i == 0, blk_e_ref[i] != blk_e_ref[jnp.maximum(i - 1, 0)])))
    def _new_expert():
        wg_bf[...] = wg_ref[0].astype(BF16)
        wu_bf[...] = wu_ref[0].astype(BF16)
        wd_bf[...] = wd_ref[0].astype(BF16)

    @pl.when(used)
    def _compute():
        xb = _unpack_rows(xs_ref[...]).astype(BF16)
        g = jnp.dot(xb, wg_bf[...], preferred_element_type=F32)
        u = jnp.dot(xb, wu_bf[...], preferred_element_type=F32)
        hid = (g * jax.nn.sigmoid(g) * u).astype(BF16)
        ys_ref[...] = _pack_rows(jnp.dot(hid, wd_bf[...], preferred_element_type=F32))

    @pl.when(i >= n_used_ref[0])
    def _unused():
        ys_ref[...] = jnp.zeros_like(ys_ref)


def _experts(blk_e, n_used, xs, wg, wu, wd):
    n_rows, half = xs.shape
    _, d, de = wg.shape
    grid_spec = pltpu.PrefetchScalarGridSpec(
        num_scalar_prefetch=2,
        grid=(n_rows // ROW_BLOCK,),
        in_specs=[
            pl.BlockSpec((ROW_BLOCK, half), lambda i, e, u: (jnp.minimum(i, u[0] - 1), 0)),
            pl.BlockSpec((1, d, de), lambda i, e, u: (e[i], 0, 0)),
            pl.BlockSpec((1, d, de), lambda i, e, u: (e[i], 0, 0)),
            pl.BlockSpec((1, de, d), lambda i, e, u: (e[i], 0, 0)),
        ],
        out_specs=pl.BlockSpec((ROW_BLOCK, half), lambda i, e, u: (i, 0)),
        scratch_shapes=[pltpu.VMEM((d, de), BF16), pltpu.VMEM((d, de), BF16), pltpu.VMEM((de, d), BF16)],
    )
    return pl.pallas_call(
        _expert_kernel,
        grid_spec=grid_spec,
        out_shape=jax.ShapeDtypeStruct((n_rows, half), U32),
        compiler_params=_params("arbitrary"),
        name="experts",
    )(blk_e, n_used, xs, wg, wu, wd)


def _combine_kernel(nb_ref, bl_ref, bg_ref, ns_ref, sl_ref, sg_ref, x1_ref, mod_ref, rt_ref, ys_ref, o_ref,
                    loc_ref, sems):
    plan = ((nb_ref, bl_ref, bg_ref), (ns_ref, sl_ref, sg_ref))
    tm = x1_ref.shape[1]
    n_loc = loc_ref.shape[1]
    tile = pl.program_id(0) * pl.num_programs(1) + pl.program_id(1)
    n_tiles = pl.num_programs(0) * pl.num_programs(1)
    buf = tile & 1

    def chunk_copy(b):
        return lambda local, row, n: pltpu.make_async_copy(
            ys_ref.at[pl.ds(row, n), :], loc_ref.at[b, pl.ds(local, n), :], sems.at[b])

    @pl.when(tile == 0)
    def _init():
        loc_ref[...] = jnp.zeros_like(loc_ref)
        _for_each_chunk(tile, plan, lambda l, r, n: chunk_copy(buf)(l, r, n).start())

    @pl.when(tile + 1 < n_tiles)
    def _fetch_next():
        _for_each_chunk(tile + 1, plan, lambda l, r, n: chunk_copy(1 - buf)(l, r, n).start())

    _for_each_chunk(tile, plan, lambda l, r, n: chunk_copy(buf)(l, r, n).wait())

    y = _unpack_rows(loc_ref[buf]).astype(BF16)
    rt = rt_ref[0]
    slot = lax.broadcasted_iota(jnp.int32, (tm, n_loc), 1).astype(F32)
    mix = jnp.where(slot == rt[:, 0:1], rt[:, 2:3], jnp.where(slot == rt[:, 1:2], rt[:, 3:4], 0.0)).astype(BF16)
    o_ref[0] = x1_ref[0] + mod_ref[0, 5:6, :] * jnp.dot(mix, y, preferred_element_type=F32)


def _combine(plan, x1, mod, rt, ys):
    bsz, s, d = x1.shape
    tm = TOKEN_TILE
    grid_spec = pltpu.PrefetchScalarGridSpec(
        num_scalar_prefetch=len(plan),
        grid=(bsz, s // tm),
        in_specs=[
            pl.BlockSpec((1, tm, d), lambda b, i, *_: (b, i, 0)),
            pl.BlockSpec((1, N_MOD, d), lambda b, i, *_: (b, 0, 0)),
            pl.BlockSpec((1, tm, ROUTE_COLS), lambda b, i, *_: (b, i, 0)),
            pl.BlockSpec(memory_space=pl.ANY),
        ],
        out_specs=pl.BlockSpec((1, tm, d), lambda b, i, *_: (b, i, 0)),
        scratch_shapes=[pltpu.VMEM((2, _local_rows(tm), d // 2), U32), pltpu.SemaphoreType.DMA((2,))],
    )
    return pl.pallas_call(
        _combine_kernel,
        grid_spec=grid_spec,
        out_shape=jax.ShapeDtypeStruct((bsz, s, d), F32),
        compiler_params=_params("arbitrary", "arbitrary"),
        name="combine",
    )(*plan, x1, mod, rt, ys)


def _layer(x, mod, g_mix, w_in, g_q, g_k, rel_bias, w_br_moba, w_br_sb, w_out, g_ffn,
           w_rg, b_rg, w_re, b_re, w_gate, w_up, w_down):
    bsz, s, d = x.shape
    n = bsz * s
    n_blk = s // ATT_BLOCK

    proj, kaug, vaug, kmean = _inproj(x, mod, g_mix, w_in.astype(BF16), g_k)

    top = (n_blk + 1) * ATT_BLOCK - 1
    dist = np.maximum(top - np.arange((n_blk + 2) * ATT_BLOCK), 0)
    rel_rev = rel_bias[:, _rel_bucket_table(top)[dist]].reshape(N_HEADS // HEADS_PER_STEP, HEADS_PER_STEP, -1)

    o_a = _moba(proj, kaug, vaug, kmean, g_q, rel_rev)
    o_b = _stick_breaking(proj)

    pad = LANES - N_GROUPS - N_EXPERTS
    w_router = jnp.concatenate([w_rg, w_re, jnp.zeros((d, pad), F32)], axis=1)
    b_router = jnp.concatenate([b_rg, b_re, jnp.zeros((pad,), F32)]).reshape(LANES, 1)
    x1, h2, rt, slot_t, tile_cnt, tile_before = _merge(x, o_a, o_b, proj, mod, w_br_moba.astype(BF16),
                                                       w_br_sb.astype(BF16), w_out.astype(BF16), g_ffn,
                                                       w_router, b_router)

    experts = slice(N_GROUPS, N_GROUPS + N_EXPERTS)
    cnt = tile_cnt[:, experts, 0].astype(jnp.int32)
    total = jnp.sum(cnt, axis=0)
    padded = (total + ROW_BLOCK - 1) // ROW_BLOCK * ROW_BLOCK
    pend = jnp.cumsum(padded)
    pstart = pend - padded
    max_rows = 2 * n + cnt.shape[0] * N_EXPERTS * (SEG_ALIGN - 1) + N_EXPERTS * (ROW_BLOCK - 1)
    n_blocks = -(-max_rows // ROW_BLOCK)
    blk_row0 = jnp.arange(n_blocks, dtype=jnp.int32) * ROW_BLOCK
    blk_e = jnp.minimum(jnp.sum((blk_row0[:, None] >= pend[None, :]).astype(jnp.int32), axis=1), N_EXPERTS - 1)
    n_used = (pend[-1] // ROW_BLOCK).astype(jnp.int32).reshape(1)
    seg_row0 = pstart[None, :] + tile_before[:, experts, 0].astype(jnp.int32)
    plan = _copy_plan(cnt, seg_row0, _local_rows(TOKEN_TILE))

    xs = _dispatch(plan, pstart + total, (padded - total) // SEG_ALIGN, n_used, slot_t,
                   h2.reshape(n, d), n_blocks * ROW_BLOCK)
    ys = _experts(blk_e, n_used, xs, w_gate, w_up, w_down)
    return _combine(plan, x1, mod, rt, ys)


def kernel(x, c, w_ada, b_ada, g_mix, w_in, g_q, g_k, rel_bias, w_br_moba, w_br_sb, w_out, g_ffn,
           w_rg, b_rg, w_re, b_re, w_gate, w_up, w_down):
    bsz, s, d = x.shape
    assert s % TOKEN_TILE == 0 and s % MOBA_Q_TILE == 0 and s % (SB_Q_BLOCKS * ATT_BLOCK) == 0
    assert d % (2 * LANES) == 0
    for l in range(w_ada.shape[0]):
        mod = _adaln(c, w_ada[l], b_ada[l]).reshape(bsz, N_MOD, d)
        x = _layer(x, mod, g_mix[l], w_in[l], g_q[l], g_k[l], rel_bias, w_br_moba[l], w_br_sb[l], w_out[l],
                   g_ffn[l], w_rg[l], b_rg[l], w_re[l], b_re[l], w_gate[l], w_up[l], w_down[l])
    return x
```

```python
import functools
import math

import numpy as np
import jax
import jax.numpy as jnp
from jax import lax
from jax.experimental import pallas as pl
from jax.experimental.pallas import tpu as pltpu

F32 = jnp.float32
BF16 = jnp.bfloat16
U32 = jnp.uint32
HIGHEST = lax.Precision.HIGHEST

LANES = 128
VMEM_LIMIT_BYTES = 56 * 1024 * 1024

HEAD_DIM = 64
N_HEADS = 8
HEADS_PER_STEP = LANES // HEAD_DIM
ATT_BLOCK = 256
MOBA_Q_TILE = 2 * ATT_BLOCK
SB_Q_BLOCKS = 4
MOBA_TOPK = 3
REL_BUCKETS = 32
REL_MAX_DIST = 1024
N_GROUPS = 4
EXPERTS_PER_GROUP = 8
N_EXPERTS = N_GROUPS * EXPERTS_PER_GROUP
N_MOD = 6
RMS_EPS = 1e-6
ROW_BLOCK = 512
TOKEN_TILE = 512
NEG_INF = float("-inf")
MASK_LOGIT = -1e30
SB_EXIT_SUM = 110.0

_NT = (((1,), (1,)), ((), ()))


def _params(*semantics):
    return pltpu.CompilerParams(dimension_semantics=semantics, vmem_limit_bytes=VMEM_LIMIT_BYTES)


def _adaln_kernel(c_ref, w_ref, b_ref, o_ref):
    c = c_ref[...]
    s = c * jax.nn.sigmoid(c)
    o_ref[...] = jnp.dot(s, w_ref[...], preferred_element_type=F32, precision=HIGHEST) + b_ref[...]


def _adaln(c, w, b):
    bsz, d = c.shape
    n = w.shape[1]
    tn = 1536
    return pl.pallas_call(
        _adaln_kernel,
        grid=(n // tn,),
        in_specs=[
            pl.BlockSpec((bsz, d), lambda j: (0, 0)),
            pl.BlockSpec((d, tn), lambda j: (0, j)),
            pl.BlockSpec((1, tn), lambda j: (0, j)),
        ],
        out_specs=pl.BlockSpec((bsz, tn), lambda j: (0, j)),
        out_shape=jax.ShapeDtypeStruct((bsz, n), F32),
        compiler_params=_params("arbitrary"),
        name="adaln",
    )(c, w, b.reshape(1, n))


def _rms_modulate(x, g, shift, scale):
    y = x * lax.rsqrt(jnp.mean(x * x, axis=-1, keepdims=True) + RMS_EPS) * g
    return y * (1.0 + scale) + shift


def _inproj_kernel(x_ref, mod_ref, g_ref, w_ref, gk_ref, o_ref, kaug_ref, vaug_ref, kmean_ref, *, col_chunk):
    i = pl.program_id(1)
    tm = x_ref.shape[1]
    h = _rms_modulate(x_ref[0], g_ref[...], mod_ref[0, 0:1, :], mod_ref[0, 1:2, :]).astype(BF16)

    w_moba = N_HEADS * HEAD_DIM
    blocks = tm // ATT_BLOCK
    first = _head_lane_mask()
    row_blk = blocks * i + lax.shift_right_logical(lax.broadcasted_iota(jnp.int32, (tm, LANES), 0),
                                                   ATT_BLOCK.bit_length() - 1)
    tag = jnp.where(_lane_index() - float(HEAD_DIM) == row_blk.astype(F32), 1.0, 0.0)
    mean_row = lax.broadcasted_iota(jnp.int32, kmean_ref.shape[2:], 0)

    @pl.when(i == 0)
    def _init():
        kmean_ref[...] = jnp.zeros_like(kmean_ref)

    for n in range(w_ref.shape[1] // col_chunk):
        cols = slice(n * col_chunk, (n + 1) * col_chunk)
        r = jnp.dot(h, w_ref[:, cols], preferred_element_type=F32)
        o_ref[0, :, cols] = r.astype(BF16)
        for c in range(col_chunk // LANES):
            col0 = n * col_chunk + c * LANES
            part = r[:, c * LANES:(c + 1) * LANES]
            if w_moba <= col0 < 2 * w_moba:
                kn = _head_rmsnorm(part, gk_ref[...])
                for hh, kh in enumerate((kn, pltpu.roll(kn, HEAD_DIM, 1))):
                    head = (col0 - w_moba) // HEAD_DIM + hh
                    kaug_ref[0, head] = jnp.where(first, kh, tag).astype(BF16)
                    means = kmean_ref[0, head]
                    for jb in range(blocks):
                        mean = jnp.mean(kh[jb * ATT_BLOCK:(jb + 1) * ATT_BLOCK], axis=0, keepdims=True)
                        means = jnp.where(mean_row == blocks * i + jb, jnp.where(first, mean, 0.0), means)
                    kmean_ref[0, head] = means
            elif 2 * w_moba <= col0 < 3 * w_moba:
                for hh, vh in enumerate((part, pltpu.roll(part, HEAD_DIM, 1))):
                    head = (col0 - 2 * w_moba) // HEAD_DIM + hh
                    vaug_ref[0, head] = jnp.where(first, vh, 1.0).astype(BF16)


def _inproj(x, mod, g, w_bf16, g_k):
    bsz, s, d = x.shape
    n = w_bf16.shape[1]
    tm = TOKEN_TILE
    n_blk = s // ATT_BLOCK
    per_head = lambda rows: pl.BlockSpec((1, N_HEADS, rows, LANES), lambda b, i: (b, 0, i, 0))
    return pl.pallas_call(
        functools.partial(_inproj_kernel, col_chunk=1024),
        grid=(bsz, s // tm),
        in_specs=[
            pl.BlockSpec((1, tm, d), lambda b, i: (b, i, 0)),
            pl.BlockSpec((1, N_MOD, d), lambda b, i: (b, 0, 0)),
            pl.BlockSpec((1, d), lambda b, i: (0, 0)),
            pl.BlockSpec((d, n), lambda b, i: (0, 0)),
            pl.BlockSpec((1, LANES), lambda b, i: (0, 0)),
        ],
        out_specs=[
            pl.BlockSpec((1, tm, n), lambda b, i: (b, i, 0)),
            per_head(tm),
            per_head(tm),
            pl.BlockSpec((1, N_HEADS, -(-n_blk // 8) * 8, LANES), lambda b, i: (b, 0, 0, 0)),
        ],
        out_shape=[
            jax.ShapeDtypeStruct((bsz, s, n), BF16),
            jax.ShapeDtypeStruct((bsz, N_HEADS, s, LANES), BF16),
            jax.ShapeDtypeStruct((bsz, N_HEADS, s, LANES), BF16),
            jax.ShapeDtypeStruct((bsz, N_HEADS, -(-n_blk // 8) * 8, LANES), F32),
        ],
        compiler_params=_params("arbitrary", "arbitrary"),
        name="inproj",
    )(x, mod, g.reshape(1, d), w_bf16, jnp.tile(g_k.reshape(1, HEAD_DIM), (1, HEADS_PER_STEP)))


def _indicator(cond, dtype):
    return jnp.where(cond, 1.0, 0.0).astype(dtype)


def _lane_index():
    return lax.broadcasted_iota(jnp.int32, (1, LANES), 1).astype(F32)


def _head_lane_mask():
    return lax.broadcasted_iota(jnp.int32, (1, LANES), 1) < HEAD_DIM


def _head_rmsnorm(t, g):
    first = _head_lane_mask()
    sq = t * t
    ss0 = jnp.sum(jnp.where(first, sq, 0.0), axis=-1, keepdims=True)
    ss1 = jnp.sum(jnp.where(first, 0.0, sq), axis=-1, keepdims=True)
    inv = jnp.where(first, lax.rsqrt(ss0 / HEAD_DIM + RMS_EPS), lax.rsqrt(ss1 / HEAD_DIM + RMS_EPS))
    return t * inv * g


def _split_heads(t):
    first = _head_lane_mask()
    zero = jnp.zeros_like(t)
    return jnp.where(first, t, zero), jnp.where(first, zero, t)


def _rel_bucket_table(max_dist):
    n = np.arange(max_dist + 1)
    max_exact = REL_BUCKETS // 2
    nf = np.maximum(n, 1).astype(np.float64)
    large = max_exact + (np.log(nf / max_exact) / math.log(REL_MAX_DIST / max_exact)
                         * (REL_BUCKETS - max_exact)).astype(np.int64)
    large = np.minimum(large, REL_BUCKETS - 1)
    return np.where(n < max_exact, n, large).astype(np.int32)


def _moba_kernel(q_ref, kaug_ref, vaug_ref, kmean_ref, gq_ref, rb_ref, o_ref,
                 toep_ref, s_ref, mvec_ref, acc_ref, *, n_blk):
    b = pl.program_id(1)
    qi = pl.program_id(2)
    blk = ATT_BLOCK
    qt = q_ref.shape[1]
    halves = qt // blk
    first = _head_lane_mask()

    @pl.when(jnp.logical_and(b == 0, qi == 0))
    def _build_bias_tiles():
        for h in range(HEADS_PER_STEP):
            for d in range(n_blk):
                r = rb_ref[0, h:h + 1, (n_blk - d) * blk:(n_blk - d + 2) * blk]
                rolled = pltpu.roll(jnp.broadcast_to(r, (blk, 2 * blk)), blk + 1, 1, stride=1, stride_axis=0)
                toep_ref[h, d] = rolled[:, :blk]

    qn = _head_rmsnorm(q_ref[0].astype(F32), gq_ref[...])
    gate_rows = kmean_ref.shape[2]
    blk_row = lax.broadcasted_iota(jnp.int32, (gate_rows, qt), 0).astype(F32)
    own = (halves * qi + lax.shift_right_logical(lax.broadcasted_iota(jnp.int32, (gate_rows, qt), 1),
                                                 blk.bit_length() - 1)).astype(F32)
    q_aug = []
    for h, qh in enumerate((qn, pltpu.roll(qn, HEAD_DIM, 1))):
        qh = jnp.where(first, qh, 0.0)
        km = kmean_ref[0, h]
        km_hi, q_hi = km.astype(BF16), qh.astype(BF16)
        km_lo, q_lo = (km - km_hi.astype(F32)).astype(BF16), (qh - q_hi.astype(F32)).astype(BF16)
        g = (lax.dot_general(km_hi, q_hi, _NT, preferred_element_type=F32)
             + lax.dot_general(km_lo, q_hi, _NT, preferred_element_type=F32)
             + lax.dot_general(km_hi, q_lo, _NT, preferred_element_type=F32))
        g = jnp.where(blk_row < own, g, NEG_INF)
        keep = jnp.where(blk_row == own, 1.0, 0.0)
        for _ in range(MOBA_TOPK):
            top = jnp.max(g, axis=0, keepdims=True)
            is_top = jnp.logical_and(g == top, top > NEG_INF)
            idx = jnp.min(jnp.where(is_top, blk_row, float(gate_rows)), axis=0, keepdims=True)
            pick = blk_row == idx
            keep = jnp.where(pick, 1.0, keep)
            g = jnp.where(pick, NEG_INF, g)
        mask_t = jnp.where(keep > 0.0, 0.0, MASK_LOGIT)
        mask_logit = jnp.concatenate([jnp.zeros((HEAD_DIM, qt), F32), mask_t,
                                      jnp.zeros((LANES - HEAD_DIM - gate_rows, qt), F32)], axis=0).T
        q_aug.append(jnp.where(first, qh * (HEAD_DIM ** -0.5), mask_logit).astype(BF16))

    def lane_max(s):
        return jnp.maximum(s[:, :LANES], s[:, LANES:])

    def scores(h, t, dist):
        k_t = kaug_ref[0, h, pl.ds(pl.multiple_of(t * qt, qt), qt), :]
        s = lax.dot_general(q_aug[h], k_t, _NT, preferred_element_type=F32)
        tiles = []
        for kb in range(halves):
            rows = []
            for qb in range(halves):
                part = s[qb * blk:(qb + 1) * blk, kb * blk:(kb + 1) * blk]
                if isinstance(dist, int) and halves * dist + qb - kb < 0:
                    rows.append(part)
                else:
                    rows.append(part + toep_ref[h, halves * dist + qb - kb])
            tiles.append(jnp.concatenate(rows, axis=0))
        return tiles

    row = lax.broadcasted_iota(jnp.int32, (qt, blk), 0)
    col = lax.broadcasted_iota(jnp.int32, (qt, blk), 1)
    for h in range(HEADS_PER_STEP):
        top = None
        for kb, s in enumerate(scores(h, qi, 0)):
            s = jnp.where(col + kb * blk <= row, s, NEG_INF)
            s_ref[h, halves * qi + kb] = s
            top = lane_max(s) if top is None else jnp.maximum(top, lane_max(s))
        mvec_ref[h] = top

    def in_pairs(count, visit):
        def quad(p, carry):
            visit(4 * p, 4)
            return carry

        lax.fori_loop(0, lax.shift_right_logical(count, 2), quad, 0)

        @pl.when((count & 2) == 2)
        def _pair():
            visit(count & ~3, 2)

        @pl.when((count & 1) == 1)
        def _last():
            visit(count - 1, 1)

    def pass1(t0, n_tiles):
        for h in range(HEADS_PER_STEP):
            top = mvec_ref[h]
            for t in [t0 + u for u in range(n_tiles)]:
                for kb, s in enumerate(scores(h, t, qi - t)):
                    s_ref[h, halves * t + kb] = s
                    top = jnp.maximum(top, lane_max(s))
            mvec_ref[h] = top

    in_pairs(qi, pass1)

    for h in range(HEADS_PER_STEP):
        m = jnp.max(mvec_ref[h], axis=-1, keepdims=True)
        mvec_ref[h] = jnp.broadcast_to(m, (qt, LANES))
        acc_ref[h] = jnp.zeros((qt, LANES), F32)

    def pass2(t0, n_tiles):
        keys = pl.ds(pl.multiple_of(t0 * qt, qt), n_tiles * qt)
        for h in range(HEADS_PER_STEP):
            m = mvec_ref[h]
            parts = []
            for kb in range(n_tiles * halves):
                s = s_ref[h, halves * t0 + kb]
                parts += [jnp.exp(s[:, :LANES] - m), jnp.exp(s[:, LANES:] - m)]
            p = jnp.concatenate(parts, axis=1).astype(BF16)
            acc_ref[h] += jnp.dot(p, vaug_ref[0, h, keys, :], preferred_element_type=F32)

    in_pairs(qi + 1, pass2)

    out = [acc_ref[h] / acc_ref[h][:, HEAD_DIM:HEAD_DIM + 1] for h in range(HEADS_PER_STEP)]
    o_ref[0] = jnp.where(first, out[0], pltpu.roll(out[1], HEAD_DIM, 1)).astype(o_ref.dtype)


def _moba(proj, kaug, vaug, kmean, g_q, rel_rev):
    bsz, s, _ = proj.shape
    blk = ATT_BLOCK
    n_blk = s // blk
    n_hp = N_HEADS // HEADS_PER_STEP
    w = N_HEADS * HEAD_DIM
    qt = MOBA_Q_TILE
    per_pair = lambda arr: pl.BlockSpec((1, HEADS_PER_STEP) + arr.shape[2:], lambda hp, b, i: (b, hp, 0, 0))
    return pl.pallas_call(
        functools.partial(_moba_kernel, n_blk=n_blk),
        grid=(n_hp, bsz, s // qt),
        in_specs=[
            pl.BlockSpec((1, qt, LANES), lambda hp, b, i: (b, i, hp)),
            per_pair(kaug), per_pair(vaug), per_pair(kmean),
            pl.BlockSpec((1, LANES), lambda hp, b, i: (0, 0)),
            pl.BlockSpec((1, HEADS_PER_STEP, rel_rev.shape[-1]), lambda hp, b, i: (hp, 0, 0)),
        ],
        out_specs=pl.BlockSpec((1, qt, LANES), lambda hp, b, i: (b, i, hp)),
        out_shape=jax.ShapeDtypeStruct((bsz, s, w), BF16),
        scratch_shapes=[
            pltpu.VMEM((HEADS_PER_STEP, n_blk, blk, blk), F32),
            pltpu.VMEM((HEADS_PER_STEP, n_blk, qt, blk), F32),
            pltpu.VMEM((HEADS_PER_STEP, qt, LANES), F32),
            pltpu.VMEM((HEADS_PER_STEP, qt, LANES), F32),
        ],
        compiler_params=_params("arbitrary", "arbitrary", "arbitrary"),
        name="moba",
    )(proj, kaug, vaug, kmean, jnp.tile(g_q.reshape(1, HEAD_DIM), (1, HEADS_PER_STEP)), rel_rev)


def _sb_kernel(q_ref, k_ref, v_ref, o_ref, acc_ref, carry_ref):
    ti = pl.program_id(2)
    blk = ATT_BLOCK
    n_sub = q_ref.shape[1] // blk
    first = n_sub * ti
    row = lax.broadcasted_iota(jnp.int32, (blk, blk), 0)
    col = lax.broadcasted_iota(jnp.int32, (blk, blk), 1)
    past = col < row
    later = _indicator(row > col, BF16)

    q_heads = []
    for sub in range(n_sub):
        heads = _split_heads(q_ref[0, sub * blk:(sub + 1) * blk, :].astype(F32) * (HEAD_DIM ** -0.5))
        q_heads.append([t.astype(BF16) for t in heads])

    def walk(jobs, fresh):
        for sub, blocks in jobs:
            for h in range(HEADS_PER_STEP):
                carry = None if fresh else carry_ref[sub, h]
                total = None if fresh else acc_ref[sub, h]
                for j, diagonal in blocks:
                    start = j * blk if isinstance(j, int) else pl.multiple_of(j * blk, blk)
                    z = lax.dot_general(q_heads[sub][h], k_ref[0, pl.ds(start, blk), :], _NT,
                                        preferred_element_type=F32)
                    sp = jnp.maximum(z, 0.0) + jnp.log(1.0 + jnp.exp(-jnp.abs(z)))
                    masked = jnp.where(past, sp, 0.0) if diagonal else sp
                    after = jnp.dot(masked.astype(BF16), later, preferred_element_type=F32)
                    row_sum = after[:, 0:1] + masked[:, 0:1]
                    a = jnp.exp(z - sp - after)
                    if diagonal:
                        a = jnp.where(past, a, 0.0)
                    pv = jnp.dot(a.astype(BF16), v_ref[0, pl.ds(start, blk), :], preferred_element_type=F32)
                    if carry is not None:
                        pv = pv * jnp.exp(-carry)
                    total = pv if total is None else total + pv
                    carry = row_sum if carry is None else carry + row_sum
                carry_ref[sub, h] = carry
                acc_ref[sub, h] = total

    @pl.when(ti == 0)
    def _first_step():
        walk([(sub, [(sub, True)] + ([(sub - 1, False)] if sub else [])) for sub in range(n_sub)], True)

    @pl.when(ti > 0)
    def _own_and_previous():
        walk([(sub, [(first + sub, True), (first + sub - 1, False)]) for sub in range(n_sub)], True)

    def slowest_decay(subs):
        worst = functools.reduce(jnp.minimum, [carry_ref[sub, h] for sub in subs for h in range(HEADS_PER_STEP)])
        return jnp.min(worst)

    @pl.when(slowest_decay(range(n_sub)) < SB_EXIT_SUM)
    def _walk_further_back():
        for sub in range(n_sub):
            def cond(state):
                j, worst = state
                return jnp.logical_and(j >= 0, worst < SB_EXIT_SUM)

            def body(state, sub=sub):
                j, _ = state
                walk([(sub, [(j, False)])], False)
                return j - 1, slowest_decay([sub])

            lax.while_loop(cond, body, (first + sub - 2, slowest_decay([sub])))

    for sub in range(n_sub):
        o_ref[0, sub * blk:(sub + 1) * blk, :] = jnp.where(
            _head_lane_mask(), acc_ref[sub, 0], acc_ref[sub, 1]).astype(o_ref.dtype)


def _stick_breaking(proj):
    bsz, s, _ = proj.shape
    blk = ATT_BLOCK
    qt = SB_Q_BLOCKS * blk
    n_hp = N_HEADS // HEADS_PER_STEP
    w = N_HEADS * HEAD_DIM
    base = 3 * n_hp
    return pl.pallas_call(
        _sb_kernel,
        grid=(n_hp, bsz, s // qt),
        in_specs=[
            pl.BlockSpec((1, qt, LANES), lambda hp, b, i: (b, i, base + hp)),
            pl.BlockSpec((1, s, LANES), lambda hp, b, i: (b, 0, base + n_hp + hp)),
            pl.BlockSpec((1, s, LANES), lambda hp, b, i: (b, 0, base + 2 * n_hp + hp)),
        ],
        out_specs=pl.BlockSpec((1, qt, LANES), lambda hp, b, i: (b, i, hp)),
        out_shape=jax.ShapeDtypeStruct((bsz, s, w), BF16),
        scratch_shapes=[
            pltpu.VMEM((SB_Q_BLOCKS, HEADS_PER_STEP, blk, LANES), F32),
            pltpu.VMEM((SB_Q_BLOCKS, HEADS_PER_STEP, blk, 1), F32),
        ],
        compiler_params=_params("arbitrary", "arbitrary", "arbitrary"),
        name="stickbrk",
    )(proj, proj, proj)


def _pack_rows(t):
    n = t.shape[1] // 2
    bits = pltpu.bitcast(t.astype(BF16).astype(F32), U32)
    return bits[:, :n] | (bits[:, n:] >> 16)


def _unpack_rows(p):
    hi = pltpu.bitcast(p & jnp.uint32(0xFFFF0000), F32)
    lo = pltpu.bitcast(p << 16, F32)
    return jnp.concatenate([hi, lo], axis=1)


def _merge_kernel(x_ref, oa_ref, ob_ref, ga_ref, gb_ref, mod_ref, wbm_ref, wbs_ref, wo_ref, gf_ref,
                  wrh_ref, wrl_ref, br_ref, x1_ref, h2_ref, rt_ref, slot_t_ref, cnt_ref, before_ref, carry_ref):
    ma = jnp.dot(oa_ref[0], wbm_ref[...], preferred_element_type=F32)
    mb = jnp.dot(ob_ref[0], wbs_ref[...], preferred_element_type=F32)
    merged = (jax.nn.sigmoid(ga_ref[0].astype(F32)) * ma + jax.nn.sigmoid(gb_ref[0].astype(F32)) * mb)
    t = jnp.dot(merged.astype(BF16), wo_ref[...], preferred_element_type=F32)
    x1 = x_ref[0] + mod_ref[0, 2:3, :] * t
    x1_ref[0] = x1
    h2 = _rms_modulate(x1, gf_ref[...], mod_ref[0, 3:4, :], mod_ref[0, 4:5, :])
    hi = h2.astype(BF16)
    h2_ref[0] = hi
    lo = (h2 - hi.astype(F32)).astype(BF16)
    logits = (lax.dot_general(wrh_ref[...], hi, _NT, preferred_element_type=F32)
              + lax.dot_general(wrh_ref[...], lo, _NT, preferred_element_type=F32)
              + lax.dot_general(wrl_ref[...], hi, _NT, preferred_element_type=F32) + br_ref[...])
    first_tile = jnp.logical_and(pl.program_id(0) == 0, pl.program_id(1) == 0)
    _route_tile(logits, first_tile, rt_ref, slot_t_ref, cnt_ref, before_ref, carry_ref)


def _merge(x, o_a, o_b, proj, mod, wbm, wbs, wo, g_ffn, w_router, b_router):
    bsz, s, d = x.shape
    w = o_a.shape[-1]
    tm = TOKEN_TILE
    gate_blk = (proj.shape[-1] - 2 * d) // d
    const = lambda shape: pl.BlockSpec(shape, lambda b, i: (0,) * len(shape))
    tok = lambda width: pl.BlockSpec((1, tm, width), lambda b, i: (b, i, 0))
    wr_hi = w_router.T.astype(BF16)
    wr_lo = (w_router.T - wr_hi.astype(F32)).astype(BF16)
    per_b = s // tm
    tiles = bsz * per_b
    per_tile = lambda rows, cols: pl.BlockSpec((1, rows, cols), lambda b, i: (b * per_b + i, 0, 0))
    return pl.pallas_call(
        _merge_kernel,
        grid=(bsz, s // tm),
        in_specs=[
            tok(d), tok(w), tok(w),
            pl.BlockSpec((1, tm, d), lambda b, i: (b, i, gate_blk)),
            pl.BlockSpec((1, tm, d), lambda b, i: (b, i, gate_blk + 1)),
            pl.BlockSpec((1, N_MOD, d), lambda b, i: (b, 0, 0)),
            const((w, d)), const((w, d)), const((d, d)), const((1, d)),
            const((LANES, d)), const((LANES, d)), const((LANES, 1)),
        ],
        out_specs=[tok(d), tok(d), tok(ROUTE_COLS), per_tile(ROUTE_COLS, tm), per_tile(LANES, 1), per_tile(LANES, 1)],
        out_shape=[
            jax.ShapeDtypeStruct((bsz, s, d), F32),
            jax.ShapeDtypeStruct((bsz, s, d), BF16),
            jax.ShapeDtypeStruct((bsz, s, ROUTE_COLS), F32),
            jax.ShapeDtypeStruct((tiles, ROUTE_COLS, tm), F32),
            jax.ShapeDtypeStruct((tiles, LANES, 1), F32),
            jax.ShapeDtypeStruct((tiles, LANES, 1), F32),
        ],
        scratch_shapes=[pltpu.VMEM((LANES, 1), F32)],
        compiler_params=_params("arbitrary", "arbitrary"),
        name="merge",
    )(x, o_a, o_b, proj, proj, mod, wbm, wbs, wo, g_ffn.reshape(1, d), wr_hi, wr_lo, b_router)


ROUTE_COLS = 8
SEG_ALIGN = 8


def _local_rows(tm):
    return 2 * tm + N_EXPERTS * SEG_ALIGN


def _route_tile(lg_t, first_tile, rt_ref, slot_t_ref, cnt_ref, before_ref, carry_ref):
    tm = lg_t.shape[1]
    rows = -(-(N_GROUPS + N_EXPERTS) // 8) * 8
    lg = lg_t[:rows]
    row = lax.broadcasted_iota(jnp.int32, (rows, tm), 0).astype(F32)

    @pl.when(first_tile)
    def _init():
        carry_ref[...] = jnp.zeros_like(carry_ref)

    def first_max(vals):
        top = jnp.max(vals, axis=0, keepdims=True)
        idx = jnp.min(jnp.where(vals == top, row, float(rows)), axis=0, keepdims=True)
        return top, idx

    g_logit = jnp.where(row < N_GROUPS, lg, NEG_INF)
    g_top, g_idx = first_max(g_logit)
    g_w = 1.0 / jnp.sum(jnp.exp(g_logit - g_top), axis=0, keepdims=True)
    lo = N_GROUPS + g_idx * EXPERTS_PER_GROUP
    e_logit = jnp.where(jnp.logical_and(row >= lo, row < lo + EXPERTS_PER_GROUP), lg, NEG_INF)
    e_top0, row0 = first_max(e_logit)
    e_top1, row1 = first_max(jnp.where(row == row0, NEG_INF, e_logit))
    z = jnp.exp(e_top1 - e_top0)
    w0 = g_w / (1.0 + z)
    w1 = g_w * z / (1.0 + z)

    hot = jnp.logical_or(row == row0, row == row1)
    t_src = lax.broadcasted_iota(jnp.int32, (tm, tm), 0)
    t_dst = lax.broadcasted_iota(jnp.int32, (tm, tm), 1)
    earlier = jnp.dot(_indicator(hot, BF16), _indicator(t_src < t_dst, BF16), preferred_element_type=F32)
    cnt = jnp.sum(_indicator(hot, F32), axis=1, keepdims=True)
    chunks = jnp.ceil(cnt * (1.0 / SEG_ALIGN))
    spare = jnp.zeros((LANES - rows, LANES), F32)
    e_dst = lax.broadcasted_iota(jnp.int32, (LANES, LANES), 0)
    e_src = lax.broadcasted_iota(jnp.int32, (LANES, LANES), 1)
    chunks_wide = jnp.concatenate([jnp.broadcast_to(chunks, (rows, LANES)), spare], axis=0).astype(BF16)
    seg_start = SEG_ALIGN * jnp.dot(_indicator(e_src < e_dst, BF16), chunks_wide,
                                    preferred_element_type=F32)[:rows, 0:1]
    local = seg_start + earlier
    slot0 = jnp.sum(jnp.where(row == row0, local, 0.0), axis=0, keepdims=True)
    slot1 = jnp.sum(jnp.where(row == row1, local, 0.0), axis=0, keepdims=True)
    seg_rows = jnp.concatenate([SEG_ALIGN * chunks, spare[:, 0:1]], axis=0)
    cnt_ref[0] = seg_rows
    before_ref[0] = carry_ref[...]
    carry_ref[...] += seg_rows

    out_row = lax.broadcasted_iota(jnp.int32, (ROUTE_COLS, tm), 0)
    packed = jnp.zeros((ROUTE_COLS, tm), F32)
    for c, v in enumerate([slot0, slot1, w0, w1]):
        packed = jnp.where(out_row == c, v, packed)
    slot_t_ref[0] = packed
    rt_ref[0] = jnp.concatenate([packed, jnp.zeros((LANES - ROUTE_COLS, tm), F32)], axis=0).T[:, :ROUTE_COLS]


CHUNK_ROWS = (8 * SEG_ALIGN, 4 * SEG_ALIGN, 2 * SEG_ALIGN, SEG_ALIGN)


def _plan_refs(refs):
    return tuple(tuple(refs[3 * k:3 * k + 3]) for k in range(len(CHUNK_ROWS)))


def _for_each_chunk(tile, plan, visit):
    for (n_ref, local_ref, global_ref), n_rows in zip(plan, CHUNK_ROWS):
        per_tile = local_ref.shape[0] // n_ref.shape[0]

        def one(c, carry, local_ref=local_ref, global_ref=global_ref, per_tile=per_tile, n_rows=n_rows):
            k = tile * per_tile + c
            visit(pl.multiple_of(local_ref[k], SEG_ALIGN), pl.multiple_of(global_ref[k], SEG_ALIGN), n_rows)
            return carry

        lax.fori_loop(0, n_ref[tile], one, 0)


def _copy_plan(cnt, seg_row0, local_rows):
    done = jnp.zeros_like(cnt)
    local0 = jnp.cumsum(cnt, axis=1) - cnt
    plan = ()
    for k, size in enumerate(CHUNK_ROWS):
        left = cnt - done
        n = left // size if k == 0 else (left // size) % 2
        max_entries = local_rows // size if k == 0 else N_EXPERTS
        end = jnp.cumsum(n, axis=1)
        start = (end - n)[:, None, :]
        c = jnp.arange(max_entries, dtype=jnp.int32)[None, :, None]
        inside = jnp.logical_and(c >= start, c < end[:, None, :])
        pick = lambda base: jnp.sum(jnp.where(inside, base[:, None, :] + size * (c - start), 0), axis=2).reshape(-1)
        plan += (end[:, -1], pick(local0 + done), pick(seg_row0 + done))
        done = done + n * size
    return plan


def _dispatch_kernel(*refs):
    plan = _plan_refs(refs)
    (tail_row_ref, tail_n_ref, n_used_ref, slot_t_ref, h_ref, xs_ref,
     loc_ref, zero_ref, sems, zero_sem) = refs[3 * len(CHUNK_ROWS):]
    tile = pl.program_id(0)
    n_loc, tm = loc_ref.shape[1], h_ref.shape[0]
    buf = tile & 1

    slot = lax.broadcasted_iota(jnp.int32, (n_loc, tm), 0).astype(F32)
    place = jnp.logical_or(slot == slot_t_ref[0, 0:1, :], slot == slot_t_ref[0, 1:2, :])
    loc_ref[buf] = _pack_rows(jnp.dot(_indicator(place, BF16), h_ref[...], preferred_element_type=F32))

    def chunk_copy(b):
        return lambda local, row, n: pltpu.make_async_copy(
            loc_ref.at[b, pl.ds(local, n), :], xs_ref.at[pl.ds(row, n), :], sems.at[b])

    _for_each_chunk(tile, plan, lambda l, r, n: chunk_copy(buf)(l, r, n).start())

    @pl.when(tile > 0)
    def _previous_done():
        _for_each_chunk(tile - 1, plan, lambda l, r, n: chunk_copy(1 - buf)(l, r, n).wait())

    @pl.when(tile == pl.num_programs(0) - 1)
    def _last_done():
        _for_each_chunk(tile, plan, lambda l, r, n: chunk_copy(buf)(l, r, n).wait())
        zero_ref[...] = jnp.zeros_like(zero_ref)
        block_rows = zero_ref.shape[0]

        def fill(action):
            def per_expert(e, carry):
                def per_chunk(c, inner):
                    row = pl.multiple_of(tail_row_ref[e] + c * SEG_ALIGN, SEG_ALIGN)
                    action(pltpu.make_async_copy(zero_ref.at[pl.ds(0, SEG_ALIGN), :],
                                                 xs_ref.at[pl.ds(row, SEG_ALIGN), :], zero_sem))
                    return inner

                lax.fori_loop(0, tail_n_ref[e], per_chunk, 0)
                return carry

            lax.fori_loop(0, N_EXPERTS, per_expert, 0)

            def per_block(blk, carry):
                row = pl.multiple_of(blk * block_rows, block_rows)
                action(pltpu.make_async_copy(zero_ref, xs_ref.at[pl.ds(row, block_rows), :], zero_sem))
                return carry

            lax.fori_loop(n_used_ref[0], xs_ref.shape[0] // block_rows, per_block, 0)

        fill(lambda copy: copy.start())
        fill(lambda copy: copy.wait())


def _dispatch(plan, tail_row, tail_chunks, n_used, slot_t, h2, n_rows):
    n, d = h2.shape
    tm = TOKEN_TILE
    grid_spec = pltpu.PrefetchScalarGridSpec(
        num_scalar_prefetch=len(plan) + 3,
        grid=(n // tm,),
        in_specs=[
            pl.BlockSpec((1, ROUTE_COLS, tm), lambda i, *_: (i, 0, 0)),
            pl.BlockSpec((tm, d), lambda i, *_: (i, 0)),
        ],
        out_specs=pl.BlockSpec(memory_space=pl.ANY),
        scratch_shapes=[pltpu.VMEM((2, _local_rows(tm), d // 2), U32), pltpu.VMEM((ROW_BLOCK, d // 2), U32),
                        pltpu.SemaphoreType.DMA((2,)), pltpu.SemaphoreType.DMA(())],
    )
    return pl.pallas_call(
        _dispatch_kernel,
        grid_spec=grid_spec,
        out_shape=jax.ShapeDtypeStruct((n_rows, d // 2), U32),
        compiler_params=_params("arbitrary"),
        name="dispatch",
    )(*plan, tail_row, tail_chunks, n_used, slot_t, h2)


def _expert_kernel(blk_e_ref, n_used_ref, xs_ref, wg_ref, wu_ref, wd_ref, ys_ref, wg_bf, wu_bf, wd_bf):
    i = pl.program_id(0)
    used = i < n_used_ref[0]

    @pl.when(jnp.logical_and(used, jnp.logical_or(i == 0, blk_e_ref[i] != blk_e_ref[jnp.maximum(i - 1, 0)])))
    def _new_expert():
        wg_bf[...] = wg_ref[0].astype(BF16)
        wu_bf[...] = wu_ref[0].astype(BF16)
        wd_bf[...] = wd_ref[0].astype(BF16)

    @pl.when(used)
    def _compute():
        xb = _unpack_rows(xs_ref[...]).astype(BF16)
        g = jnp.dot(xb, wg_bf[...], preferred_element_type=F32)
        u = jnp.dot(xb, wu_bf[...], preferred_element_type=F32)
        hid = (g * jax.nn.sigmoid(g) * u).astype(BF16)
        ys_ref[...] = _pack_rows(jnp.dot(hid, wd_bf[...], preferred_element_type=F32))

    @pl.when(i >= n_used_ref[0])
    def _unused():
        ys_ref[...] = jnp.zeros_like(ys_ref)


def _experts(blk_e, n_used, xs, wg, wu, wd):
    n_rows, half = xs.shape
    _, d, de = wg.shape
    grid_spec = pltpu.PrefetchScalarGridSpec(
        num_scalar_prefetch=2,
        grid=(n_rows // ROW_BLOCK,),
        in_specs=[
            pl.BlockSpec((ROW_BLOCK, half), lambda i, e, u: (jnp.minimum(i, u[0] - 1), 0)),
            pl.BlockSpec((1, d, de), lambda i, e, u: (e[i], 0, 0)),
            pl.BlockSpec((1, d, de), lambda i, e, u: (e[i], 0, 0)),
            pl.BlockSpec((1, de, d), lambda i, e, u: (e[i], 0, 0)),
        ],
        out_specs=pl.BlockSpec((ROW_BLOCK, half), lambda i, e, u: (i, 0)),
        scratch_shapes=[pltpu.VMEM((d, de), BF16), pltpu.VMEM((d, de), BF16), pltpu.VMEM((de, d), BF16)],
    )
    return pl.pallas_call(
        _expert_kernel,
        grid_spec=grid_spec,
        out_shape=jax.ShapeDtypeStruct((n_rows, half), U32),
        compiler_params=_params("arbitrary"),
        name="experts",
    )(blk_e, n_used, xs, wg, wu, wd)


def _combine_kernel(*refs):
    plan = _plan_refs(refs)
    x1_ref, mod_ref, rt_ref, ys_ref, o_ref, loc_ref, sems = refs[3 * len(CHUNK_ROWS):]
    tm = x1_ref.shape[1]
    n_loc = loc_ref.shape[1]
    tile = pl.program_id(0) * pl.num_programs(1) + pl.program_id(1)
    n_tiles = pl.num_programs(0) * pl.num_programs(1)
    buf = tile & 1

    def chunk_copy(b):
        return lambda local, row, n: pltpu.make_async_copy(
            ys_ref.at[pl.ds(row, n), :], loc_ref.at[b, pl.ds(local, n), :], sems.at[b])

    @pl.when(tile == 0)
    def _init():
        loc_ref[...] = jnp.zeros_like(loc_ref)
        _for_each_chunk(tile, plan, lambda l, r, n: chunk_copy(buf)(l, r, n).start())

    @pl.when(tile + 1 < n_tiles)
    def _fetch_next():
        _for_each_chunk(tile + 1, plan, lambda l, r, n: chunk_copy(1 - buf)(l, r, n).start())

    _for_each_chunk(tile, plan, lambda l, r, n: chunk_copy(buf)(l, r, n).wait())

    y = _unpack_rows(loc_ref[buf]).astype(BF16)
    rt = rt_ref[0]
    slot = lax.broadcasted_iota(jnp.int32, (tm, n_loc), 1).astype(F32)
    mix = jnp.where(slot == rt[:, 0:1], rt[:, 2:3], jnp.where(slot == rt[:, 1:2], rt[:, 3:4], 0.0)).astype(BF16)
    o_ref[0] = x1_ref[0] + mod_ref[0, 5:6, :] * jnp.dot(mix, y, preferred_element_type=F32)


def _combine(plan, x1, mod, rt, ys):
    bsz, s, d = x1.shape
    tm = TOKEN_TILE
    grid_spec = pltpu.PrefetchScalarGridSpec(
        num_scalar_prefetch=len(plan),
        grid=(bsz, s // tm),
        in_specs=[
            pl.BlockSpec((1, tm, d), lambda b, i, *_: (b, i, 0)),
            pl.BlockSpec((1, N_MOD, d), lambda b, i, *_: (b, 0, 0)),
            pl.BlockSpec((1, tm, ROUTE_COLS), lambda b, i, *_: (b, i, 0)),
            pl.BlockSpec(memory_space=pl.ANY),
        ],
        out_specs=pl.BlockSpec((1, tm, d), lambda b, i, *_: (b, i, 0)),
        scratch_shapes=[pltpu.VMEM((2, _local_rows(tm), d // 2), U32), pltpu.SemaphoreType.DMA((2,))],
    )
    return pl.pallas_call(
        _combine_kernel,
        grid_spec=grid_spec,
        out_shape=jax.ShapeDtypeStruct((bsz, s, d), F32),
        compiler_params=_params("arbitrary", "arbitrary"),
        name="combine",
    )(*plan, x1, mod, rt, ys)


def _layer(x, mod, g_mix, w_in, g_q, g_k, rel_bias, w_br_moba, w_br_sb, w_out, g_ffn,
           w_rg, b_rg, w_re, b_re, w_gate, w_up, w_down):
    bsz, s, d = x.shape
    n = bsz * s
    n_blk = s // ATT_BLOCK

    proj, kaug, vaug, kmean = _inproj(x, mod, g_mix, w_in.astype(BF16), g_k)

    top = (n_blk + 1) * ATT_BLOCK - 1
    dist = np.maximum(top - np.arange((n_blk + 2) * ATT_BLOCK), 0)
    rel_rev = rel_bias[:, _rel_bucket_table(top)[dist]].reshape(N_HEADS // HEADS_PER_STEP, HEADS_PER_STEP, -1)

    o_a = _moba(proj, kaug, vaug, kmean, g_q, rel_rev)
    o_b = _stick_breaking(proj)

    pad = LANES - N_GROUPS - N_EXPERTS
    w_router = jnp.concatenate([w_rg, w_re, jnp.zeros((d, pad), F32)], axis=1)
    b_router = jnp.concatenate([b_rg, b_re, jnp.zeros((pad,), F32)]).reshape(LANES, 1)
    x1, h2, rt, slot_t, tile_cnt, tile_before = _merge(x, o_a, o_b, proj, mod, w_br_moba.astype(BF16),
                                                       w_br_sb.astype(BF16), w_out.astype(BF16), g_ffn,
                                                       w_router, b_router)

    experts = slice(N_GROUPS, N_GROUPS + N_EXPERTS)
    cnt = tile_cnt[:, experts, 0].astype(jnp.int32)
    total = jnp.sum(cnt, axis=0)
    padded = (total + ROW_BLOCK - 1) // ROW_BLOCK * ROW_BLOCK
    pend = jnp.cumsum(padded)
    pstart = pend - padded
    max_rows = 2 * n + cnt.shape[0] * N_EXPERTS * (SEG_ALIGN - 1) + N_EXPERTS * (ROW_BLOCK - 1)
    n_blocks = -(-max_rows // ROW_BLOCK)
    blk_row0 = jnp.arange(n_blocks, dtype=jnp.int32) * ROW_BLOCK
    blk_e = jnp.minimum(jnp.sum((blk_row0[:, None] >= pend[None, :]).astype(jnp.int32), axis=1), N_EXPERTS - 1)
    n_used = (pend[-1] // ROW_BLOCK).astype(jnp.int32).reshape(1)
    seg_row0 = pstart[None, :] + tile_before[:, experts, 0].astype(jnp.int32)
    plan = _copy_plan(cnt, seg_row0, _local_rows(TOKEN_TILE))

    xs = _dispatch(plan, pstart + total, (padded - total) // SEG_ALIGN, n_used, slot_t,
                   h2.reshape(n, d), n_blocks * ROW_BLOCK)
    ys = _experts(blk_e, n_used, xs, w_gate, w_up, w_down)
    return _combine(plan, x1, mod, rt, ys)


def kernel(x, c, w_ada, b_ada, g_mix, w_in, g_q, g_k, rel_bias, w_br_moba, w_br_sb, w_out, g_ffn,
           w_rg, b_rg, w_re, b_re, w_gate, w_up, w_down):
    bsz, s, d = x.shape
    assert s % TOKEN_TILE == 0 and s % MOBA_Q_TILE == 0 and s % (SB_Q_BLOCKS * ATT_BLOCK) == 0
    assert d % (2 * LANES) == 0
    for l in range(w_ada.shape[0]):
        mod = _adaln(c, w_ada[l], b_ada[l]).reshape(bsz, N_MOD, d)
        x = _layer(x, mod, g_mix[l], w_in[l], g_q[l], g_k[l], rel_bias, w_br_moba[l], w_br_sb[l], w_out[l],
                   g_ffn[l], w_rg[l], b_rg[l], w_re[l], b_re[l], w_gate[l], w_up[l], w_down[l])
    return x
```

```python
import functools
import math

import numpy as np
import jax
import jax.numpy as jnp
from jax import lax
from jax.experimental import pallas as pl
from jax.experimental.pallas import tpu as pltpu

F32 = jnp.float32
BF16 = jnp.bfloat16
U32 = jnp.uint32
HIGHEST = lax.Precision.HIGHEST

LANES = 128
VMEM_LIMIT_BYTES = 56 * 1024 * 1024

HEAD_DIM = 64
N_HEADS = 8
HEADS_PER_STEP = LANES // HEAD_DIM
ATT_BLOCK = 256
MOBA_Q_TILE = 2 * ATT_BLOCK
SB_Q_BLOCKS = 4
MOBA_TOPK = 3
REL_BUCKETS = 32
REL_MAX_DIST = 1024
N_GROUPS = 4
EXPERTS_PER_GROUP = 8
N_EXPERTS = N_GROUPS * EXPERTS_PER_GROUP
N_MOD = 6
RMS_EPS = 1e-6
ROW_BLOCK = 1024
TOKEN_TILE = 512
NEG_INF = float("-inf")
MASK_LOGIT = -1e30
SB_EXIT_SUM = 110.0

_NT = (((1,), (1,)), ((), ()))


def _params(*semantics):
    return pltpu.CompilerParams(dimension_semantics=semantics, vmem_limit_bytes=VMEM_LIMIT_BYTES)


def _adaln_kernel(c_ref, w_ref, b_ref, o_ref):
    c = c_ref[...]
    s = c * jax.nn.sigmoid(c)
    o_ref[...] = jnp.dot(s, w_ref[...], preferred_element_type=F32, precision=HIGHEST) + b_ref[...]


def _adaln(c, w, b):
    bsz, d = c.shape
    n = w.shape[1]
    tn = 1536
    return pl.pallas_call(
        _adaln_kernel,
        grid=(n // tn,),
        in_specs=[
            pl.BlockSpec((bsz, d), lambda j: (0, 0)),
            pl.BlockSpec((d, tn), lambda j: (0, j)),
            pl.BlockSpec((1, tn), lambda j: (0, j)),
        ],
        out_specs=pl.BlockSpec((bsz, tn), lambda j: (0, j)),
        out_shape=jax.ShapeDtypeStruct((bsz, n), F32),
        compiler_params=_params("arbitrary"),
        name="adaln",
    )(c, w, b.reshape(1, n))


def _rms_modulate(x, g, shift, scale):
    y = x * lax.rsqrt(jnp.mean(x * x, axis=-1, keepdims=True) + RMS_EPS) * g
    return y * (1.0 + scale) + shift


def _inproj_kernel(x_ref, mod_ref, g_ref, w_ref, gk_ref, o_ref, kaug_ref, vaug_ref, kmean_ref, *, col_chunk):
    i = pl.program_id(1)
    tm = x_ref.shape[1]
    h = _rms_modulate(x_ref[0], g_ref[...], mod_ref[0, 0:1, :], mod_ref[0, 1:2, :]).astype(BF16)

    w_moba = N_HEADS * HEAD_DIM
    blocks = tm // ATT_BLOCK
    first = _head_lane_mask()
    row_blk = blocks * i + lax.shift_right_logical(lax.broadcasted_iota(jnp.int32, (tm, LANES), 0),
                                                   ATT_BLOCK.bit_length() - 1)
    tag = jnp.where(_lane_index() - float(HEAD_DIM) == row_blk.astype(F32), 1.0, 0.0)
    mean_row = lax.broadcasted_iota(jnp.int32, kmean_ref.shape[2:], 0)

    @pl.when(i == 0)
    def _init():
        kmean_ref[...] = jnp.zeros_like(kmean_ref)

    for n in range(w_ref.shape[1] // col_chunk):
        cols = slice(n * col_chunk, (n + 1) * col_chunk)
        r = jnp.dot(h, w_ref[:, cols], preferred_element_type=F32)
        o_ref[0, :, cols] = r.astype(BF16)
        for c in range(col_chunk // LANES):
            col0 = n * col_chunk + c * LANES
            part = r[:, c * LANES:(c + 1) * LANES]
            if w_moba <= col0 < 2 * w_moba:
                kn = _head_rmsnorm(part, gk_ref[...])
                for hh, kh in enumerate((kn, pltpu.roll(kn, HEAD_DIM, 1))):
                    head = (col0 - w_moba) // HEAD_DIM + hh
                    kaug_ref[0, head] = jnp.where(first, kh, tag).astype(BF16)
                    means = kmean_ref[0, head]
                    for jb in range(blocks):
                        mean = jnp.mean(kh[jb * ATT_BLOCK:(jb + 1) * ATT_BLOCK], axis=0, keepdims=True)
                        means = jnp.where(mean_row == blocks * i + jb, jnp.where(first, mean, 0.0), means)
                    kmean_ref[0, head] = means
            elif 2 * w_moba <= col0 < 3 * w_moba:
                for hh, vh in enumerate((part, pltpu.roll(part, HEAD_DIM, 1))):
                    head = (col0 - 2 * w_moba) // HEAD_DIM + hh
                    vaug_ref[0, head] = jnp.where(first, vh, 1.0).astype(BF16)


def _inproj(x, mod, g, w_bf16, g_k):
    bsz, s, d = x.shape
    n = w_bf16.shape[1]
    tm = TOKEN_TILE
    n_blk = s // ATT_BLOCK
    per_head = lambda rows: pl.BlockSpec((1, N_HEADS, rows, LANES), lambda b, i: (b, 0, i, 0))
    return pl.pallas_call(
        functools.partial(_inproj_kernel, col_chunk=1024),
        grid=(bsz, s // tm),
        in_specs=[
            pl.BlockSpec((1, tm, d), lambda b, i: (b, i, 0)),
            pl.BlockSpec((1, N_MOD, d), lambda b, i: (b, 0, 0)),
            pl.BlockSpec((1, d), lambda b, i: (0, 0)),
            pl.BlockSpec((d, n), lambda b, i: (0, 0)),
            pl.BlockSpec((1, LANES), lambda b, i: (0, 0)),
        ],
        out_specs=[
            pl.BlockSpec((1, tm, n), lambda b, i: (b, i, 0)),
            per_head(tm),
            per_head(tm),
            pl.BlockSpec((1, N_HEADS, -(-n_blk // 8) * 8, LANES), lambda b, i: (b, 0, 0, 0)),
        ],
        out_shape=[
            jax.ShapeDtypeStruct((bsz, s, n), BF16),
            jax.ShapeDtypeStruct((bsz, N_HEADS, s, LANES), BF16),
            jax.ShapeDtypeStruct((bsz, N_HEADS, s, LANES), BF16),
            jax.ShapeDtypeStruct((bsz, N_HEADS, -(-n_blk // 8) * 8, LANES), F32),
        ],
        compiler_params=_params("arbitrary", "arbitrary"),
        name="inproj",
    )(x, mod, g.reshape(1, d), w_bf16, jnp.tile(g_k.reshape(1, HEAD_DIM), (1, HEADS_PER_STEP)))


def _indicator(cond, dtype):
    return jnp.where(cond, 1.0, 0.0).astype(dtype)


def _lane_index():
    return lax.broadcasted_iota(jnp.int32, (1, LANES), 1).astype(F32)


def _head_lane_mask():
    return lax.broadcasted_iota(jnp.int32, (1, LANES), 1) < HEAD_DIM


def _head_rmsnorm(t, g):
    first = _head_lane_mask()
    sq = t * t
    ss0 = jnp.sum(jnp.where(first, sq, 0.0), axis=-1, keepdims=True)
    ss1 = jnp.sum(jnp.where(first, 0.0, sq), axis=-1, keepdims=True)
    inv = jnp.where(first, lax.rsqrt(ss0 / HEAD_DIM + RMS_EPS), lax.rsqrt(ss1 / HEAD_DIM + RMS_EPS))
    return t * inv * g


def _split_heads(t):
    first = _head_lane_mask()
    zero = jnp.zeros_like(t)
    return jnp.where(first, t, zero), jnp.where(first, zero, t)


def _rel_bucket_table(max_dist):
    n = np.arange(max_dist + 1)
    max_exact = REL_BUCKETS // 2
    nf = np.maximum(n, 1).astype(np.float64)
    large = max_exact + (np.log(nf / max_exact) / math.log(REL_MAX_DIST / max_exact)
                         * (REL_BUCKETS - max_exact)).astype(np.int64)
    large = np.minimum(large, REL_BUCKETS - 1)
    return np.where(n < max_exact, n, large).astype(np.int32)


def _moba_kernel(q_ref, kaug_ref, vaug_ref, kmean_ref, gq_ref, rb_ref, o_ref,
                 toep_ref, s_ref, mvec_ref, acc_ref, *, n_blk):
    b = pl.program_id(1)
    qi = pl.program_id(2)
    blk = ATT_BLOCK
    qt = q_ref.shape[1]
    halves = qt // blk
    first = _head_lane_mask()

    @pl.when(jnp.logical_and(b == 0, qi == 0))
    def _build_bias_tiles():
        for h in range(HEADS_PER_STEP):
            for d in range(n_blk):
                r = rb_ref[0, h:h + 1, (n_blk - d) * blk:(n_blk - d + 2) * blk]
                rolled = pltpu.roll(jnp.broadcast_to(r, (blk, 2 * blk)), blk + 1, 1, stride=1, stride_axis=0)
                toep_ref[h, d] = rolled[:, :blk]

    qn = _head_rmsnorm(q_ref[0].astype(F32), gq_ref[...])
    gate_rows = kmean_ref.shape[2]
    blk_row = lax.broadcasted_iota(jnp.int32, (gate_rows, qt), 0).astype(F32)
    own = (halves * qi + lax.shift_right_logical(lax.broadcasted_iota(jnp.int32, (gate_rows, qt), 1),
                                                 blk.bit_length() - 1)).astype(F32)
    q_aug = []
    for h, qh in enumerate((qn, pltpu.roll(qn, HEAD_DIM, 1))):
        qh = jnp.where(first, qh, 0.0)
        km = kmean_ref[0, h]
        km_hi, q_hi = km.astype(BF16), qh.astype(BF16)
        km_lo, q_lo = (km - km_hi.astype(F32)).astype(BF16), (qh - q_hi.astype(F32)).astype(BF16)
        g = (lax.dot_general(km_hi, q_hi, _NT, preferred_element_type=F32)
             + lax.dot_general(km_lo, q_hi, _NT, preferred_element_type=F32)
             + lax.dot_general(km_hi, q_lo, _NT, preferred_element_type=F32))
        g = jnp.where(blk_row < own, g, NEG_INF)
        keep = jnp.where(blk_row == own, 1.0, 0.0)
        for _ in range(MOBA_TOPK):
            top = jnp.max(g, axis=0, keepdims=True)
            is_top = jnp.logical_and(g == top, top > NEG_INF)
            idx = jnp.min(jnp.where(is_top, blk_row, float(gate_rows)), axis=0, keepdims=True)
            pick = blk_row == idx
            keep = jnp.where(pick, 1.0, keep)
            g = jnp.where(pick, NEG_INF, g)
        mask_t = jnp.where(keep > 0.0, 0.0, MASK_LOGIT)
        mask_logit = jnp.concatenate([jnp.zeros((HEAD_DIM, qt), F32), mask_t,
                                      jnp.zeros((LANES - HEAD_DIM - gate_rows, qt), F32)], axis=0).T
        q_aug.append(jnp.where(first, qh * (HEAD_DIM ** -0.5), mask_logit).astype(BF16))

    def lane_max(s):
        return jnp.maximum(s[:, :LANES], s[:, LANES:])

    def scores(h, t, dist):
        k_t = kaug_ref[0, h, pl.ds(pl.multiple_of(t * qt, qt), qt), :]
        s = lax.dot_general(q_aug[h], k_t, _NT, preferred_element_type=F32)
        tiles = []
        for kb in range(halves):
            rows = []
            for qb in range(halves):
                part = s[qb * blk:(qb + 1) * blk, kb * blk:(kb + 1) * blk]
                if isinstance(dist, int) and halves * dist + qb - kb < 0:
                    rows.append(part)
                else:
                    rows.append(part + toep_ref[h, halves * dist + qb - kb])
            tiles.append(jnp.concatenate(rows, axis=0))
        return tiles

    row = lax.broadcasted_iota(jnp.int32, (qt, blk), 0)
    col = lax.broadcasted_iota(jnp.int32, (qt, blk), 1)
    for h in range(HEADS_PER_STEP):
        top = None
        for kb, s in enumerate(scores(h, qi, 0)):
            s = jnp.where(col + kb * blk <= row, s, NEG_INF)
            s_ref[h, halves * qi + kb] = s
            top = lane_max(s) if top is None else jnp.maximum(top, lane_max(s))
        mvec_ref[h] = top

    def in_pairs(count, visit):
        def quad(p, carry):
            visit(4 * p, 4)
            return carry

        lax.fori_loop(0, lax.shift_right_logical(count, 2), quad, 0)

        @pl.when((count & 2) == 2)
        def _pair():
            visit(count & ~3, 2)

        @pl.when((count & 1) == 1)
        def _last():
            visit(count - 1, 1)

    def pass1(t0, n_tiles):
        for h in range(HEADS_PER_STEP):
            top = mvec_ref[h]
            for t in [t0 + u for u in range(n_tiles)]:
                for kb, s in enumerate(scores(h, t, qi - t)):
                    s_ref[h, halves * t + kb] = s
                    top = jnp.maximum(top, lane_max(s))
            mvec_ref[h] = top

    in_pairs(qi, pass1)

    for h in range(HEADS_PER_STEP):
        m = jnp.max(mvec_ref[h], axis=-1, keepdims=True)
        mvec_ref[h] = jnp.broadcast_to(m, (qt, LANES))
        acc_ref[h] = jnp.zeros((qt, LANES), F32)

    def pass2(t0, n_tiles):
        keys = pl.ds(pl.multiple_of(t0 * qt, qt), n_tiles * qt)
        for h in range(HEADS_PER_STEP):
            m = mvec_ref[h]
            parts = []
            for kb in range(n_tiles * halves):
                s = s_ref[h, halves * t0 + kb]
                parts += [jnp.exp(s[:, :LANES] - m), jnp.exp(s[:, LANES:] - m)]
            p = jnp.concatenate(parts, axis=1).astype(BF16)
            acc_ref[h] += jnp.dot(p, vaug_ref[0, h, keys, :], preferred_element_type=F32)

    in_pairs(qi + 1, pass2)

    out = [acc_ref[h] / acc_ref[h][:, HEAD_DIM:HEAD_DIM + 1] for h in range(HEADS_PER_STEP)]
    o_ref[0] = jnp.where(first, out[0], pltpu.roll(out[1], HEAD_DIM, 1)).astype(o_ref.dtype)


def _moba(proj, kaug, vaug, kmean, g_q, rel_rev):
    bsz, s, _ = proj.shape
    blk = ATT_BLOCK
    n_blk = s // blk
    n_hp = N_HEADS // HEADS_PER_STEP
    w = N_HEADS * HEAD_DIM
    qt = MOBA_Q_TILE
    per_pair = lambda arr: pl.BlockSpec((1, HEADS_PER_STEP) + arr.shape[2:], lambda hp, b, i: (b, hp, 0, 0))
    return pl.pallas_call(
        functools.partial(_moba_kernel, n_blk=n_blk),
        grid=(n_hp, bsz, s // qt),
        in_specs=[
            pl.BlockSpec((1, qt, LANES), lambda hp, b, i: (b, i, hp)),
            per_pair(kaug), per_pair(vaug), per_pair(kmean),
            pl.BlockSpec((1, LANES), lambda hp, b, i: (0, 0)),
            pl.BlockSpec((1, HEADS_PER_STEP, rel_rev.shape[-1]), lambda hp, b, i: (hp, 0, 0)),
        ],
        out_specs=pl.BlockSpec((1, qt, LANES), lambda hp, b, i: (b, i, hp)),
        out_shape=jax.ShapeDtypeStruct((bsz, s, w), BF16),
        scratch_shapes=[
            pltpu.VMEM((HEADS_PER_STEP, n_blk, blk, blk), F32),
            pltpu.VMEM((HEADS_PER_STEP, n_blk, qt, blk), F32),
            pltpu.VMEM((HEADS_PER_STEP, qt, LANES), F32),
            pltpu.VMEM((HEADS_PER_STEP, qt, LANES), F32),
        ],
        compiler_params=_params("arbitrary", "arbitrary", "arbitrary"),
        name="moba",
    )(proj, kaug, vaug, kmean, jnp.tile(g_q.reshape(1, HEAD_DIM), (1, HEADS_PER_STEP)), rel_rev)


def _sb_kernel(q_ref, k_ref, v_ref, o_ref, acc_ref, carry_ref):
    ti = pl.program_id(2)
    blk = ATT_BLOCK
    n_sub = q_ref.shape[1] // blk
    first = n_sub * ti
    row = lax.broadcasted_iota(jnp.int32, (blk, blk), 0)
    col = lax.broadcasted_iota(jnp.int32, (blk, blk), 1)
    past = col < row
    later = _indicator(row > col, BF16)

    q_heads = []
    for sub in range(n_sub):
        heads = _split_heads(q_ref[0, sub * blk:(sub + 1) * blk, :].astype(F32) * (HEAD_DIM ** -0.5))
        q_heads.append([t.astype(BF16) for t in heads])

    def walk(jobs, fresh):
        for sub, blocks in jobs:
            for h in range(HEADS_PER_STEP):
                carry = None if fresh else carry_ref[sub, h]
                total = None if fresh else acc_ref[sub, h]
                for j, diagonal in blocks:
                    start = j * blk if isinstance(j, int) else pl.multiple_of(j * blk, blk)
                    z = lax.dot_general(q_heads[sub][h], k_ref[0, pl.ds(start, blk), :], _NT,
                                        preferred_element_type=F32)
                    sp = jnp.maximum(z, 0.0) + jnp.log(1.0 + jnp.exp(-jnp.abs(z)))
                    masked = jnp.where(past, sp, 0.0) if diagonal else sp
                    after = jnp.dot(masked.astype(BF16), later, preferred_element_type=F32)
                    row_sum = after[:, 0:1] + masked[:, 0:1]
                    a = jnp.exp(z - sp - after)
                    if diagonal:
                        a = jnp.where(past, a, 0.0)
                    pv = jnp.dot(a.astype(BF16), v_ref[0, pl.ds(start, blk), :], preferred_element_type=F32)
                    if carry is not None:
                        pv = pv * jnp.exp(-carry)
                    total = pv if total is None else total + pv
                    carry = row_sum if carry is None else carry + row_sum
                carry_ref[sub, h] = carry
                acc_ref[sub, h] = total

    @pl.when(ti == 0)
    def _first_step():
        walk([(sub, [(sub, True)] + ([(sub - 1, False)] if sub else [])) for sub in range(n_sub)], True)

    @pl.when(ti > 0)
    def _own_and_previous():
        walk([(sub, [(first + sub, True), (first + sub - 1, False)]) for sub in range(n_sub)], True)

    def slowest_decay(subs):
        worst = functools.reduce(jnp.minimum, [carry_ref[sub, h] for sub in subs for h in range(HEADS_PER_STEP)])
        return jnp.min(worst)

    @pl.when(slowest_decay(range(n_sub)) < SB_EXIT_SUM)
    def _walk_further_back():
        for sub in range(n_sub):
            def cond(state):
                j, worst = state
                return jnp.logical_and(j >= 0, worst < SB_EXIT_SUM)

            def body(state, sub=sub):
                j, _ = state
                walk([(sub, [(j, False)])], False)
                return j - 1, slowest_decay([sub])

            lax.while_loop(cond, body, (first + sub - 2, slowest_decay([sub])))

    for sub in range(n_sub):
        o_ref[0, sub * blk:(sub + 1) * blk, :] = jnp.where(
            _head_lane_mask(), acc_ref[sub, 0], acc_ref[sub, 1]).astype(o_ref.dtype)


def _stick_breaking(proj):
    bsz, s, _ = proj.shape
    blk = ATT_BLOCK
    qt = SB_Q_BLOCKS * blk
    n_hp = N_HEADS // HEADS_PER_STEP
    w = N_HEADS * HEAD_DIM
    base = 3 * n_hp
    return pl.pallas_call(
        _sb_kernel,
        grid=(n_hp, bsz, s // qt),
        in_specs=[
            pl.BlockSpec((1, qt, LANES), lambda hp, b, i: (b, i, base + hp)),
            pl.BlockSpec((1, s, LANES), lambda hp, b, i: (b, 0, base + n_hp + hp)),
            pl.BlockSpec((1, s, LANES), lambda hp, b, i: (b, 0, base + 2 * n_hp + hp)),
        ],
        out_specs=pl.BlockSpec((1, qt, LANES), lambda hp, b, i: (b, i, hp)),
        out_shape=jax.ShapeDtypeStruct((bsz, s, w), BF16),
        scratch_shapes=[
            pltpu.VMEM((SB_Q_BLOCKS, HEADS_PER_STEP, blk, LANES), F32),
            pltpu.VMEM((SB_Q_BLOCKS, HEADS_PER_STEP, blk, 1), F32),
        ],
        compiler_params=_params("arbitrary", "arbitrary", "arbitrary"),
        name="stickbrk",
    )(proj, proj, proj)


def _pack_rows(t):
    n = t.shape[1] // 2
    bits = pltpu.bitcast(t.astype(BF16).astype(F32), U32)
    return bits[:, :n] | (bits[:, n:] >> 16)


def _unpack_rows(p):
    hi = pltpu.bitcast(p & jnp.uint32(0xFFFF0000), F32)
    lo = pltpu.bitcast(p << 16, F32)
    return jnp.concatenate([hi, lo], axis=1)


def _merge_kernel(x_ref, oa_ref, ob_ref, ga_ref, gb_ref, mod_ref, wbm_ref, wbs_ref, wo_ref, gf_ref,
                  wrh_ref, wrl_ref, br_ref, x1_ref, h2_ref, rt_ref, slot_t_ref, cnt_ref, before_ref, carry_ref):
    ma = jnp.dot(oa_ref[0], wbm_ref[...], preferred_element_type=F32)
    mb = jnp.dot(ob_ref[0], wbs_ref[...], preferred_element_type=F32)
    merged = (jax.nn.sigmoid(ga_ref[0].astype(F32)) * ma + jax.nn.sigmoid(gb_ref[0].astype(F32)) * mb)
    t = jnp.dot(merged.astype(BF16), wo_ref[...], preferred_element_type=F32)
    x1 = x_ref[0] + mod_ref[0, 2:3, :] * t
    x1_ref[0] = x1
    h2 = _rms_modulate(x1, gf_ref[...], mod_ref[0, 3:4, :], mod_ref[0, 4:5, :])
    hi = h2.astype(BF16)
    h2_ref[0] = hi
    lo = (h2 - hi.astype(F32)).astype(BF16)
    logits = (lax.dot_general(wrh_ref[...], hi, _NT, preferred_element_type=F32)
              + lax.dot_general(wrh_ref[...], lo, _NT, preferred_element_type=F32)
              + lax.dot_general(wrl_ref[...], hi, _NT, preferred_element_type=F32) + br_ref[...])
    first_tile = jnp.logical_and(pl.program_id(0) == 0, pl.program_id(1) == 0)
    _route_tile(logits, first_tile, rt_ref, slot_t_ref, cnt_ref, before_ref, carry_ref)


def _merge(x, o_a, o_b, proj, mod, wbm, wbs, wo, g_ffn, w_router, b_router):
    bsz, s, d = x.shape
    w = o_a.shape[-1]
    tm = TOKEN_TILE
    gate_blk = (proj.shape[-1] - 2 * d) // d
    const = lambda shape: pl.BlockSpec(shape, lambda b, i: (0,) * len(shape))
    tok = lambda width: pl.BlockSpec((1, tm, width), lambda b, i: (b, i, 0))
    wr_hi = w_router.T.astype(BF16)
    wr_lo = (w_router.T - wr_hi.astype(F32)).astype(BF16)
    per_b = s // tm
    tiles = bsz * per_b
    per_tile = lambda rows, cols: pl.BlockSpec((1, rows, cols), lambda b, i: (b * per_b + i, 0, 0))
    return pl.pallas_call(
        _merge_kernel,
        grid=(bsz, s // tm),
        in_specs=[
            tok(d), tok(w), tok(w),
            pl.BlockSpec((1, tm, d), lambda b, i: (b, i, gate_blk)),
            pl.BlockSpec((1, tm, d), lambda b, i: (b, i, gate_blk + 1)),
            pl.BlockSpec((1, N_MOD, d), lambda b, i: (b, 0, 0)),
            const((w, d)), const((w, d)), const((d, d)), const((1, d)),
            const((LANES, d)), const((LANES, d)), const((LANES, 1)),
        ],
        out_specs=[tok(d), tok(d), tok(ROUTE_COLS), per_tile(ROUTE_COLS, tm), per_tile(LANES, 1), per_tile(LANES, 1)],
        out_shape=[
            jax.ShapeDtypeStruct((bsz, s, d), F32),
            jax.ShapeDtypeStruct((bsz, s, d), BF16),
            jax.ShapeDtypeStruct((bsz, s, ROUTE_COLS), F32),
            jax.ShapeDtypeStruct((tiles, ROUTE_COLS, tm), F32),
            jax.ShapeDtypeStruct((tiles, LANES, 1), F32),
            jax.ShapeDtypeStruct((tiles, LANES, 1), F32),
        ],
        scratch_shapes=[pltpu.VMEM((LANES, 1), F32)],
        compiler_params=_params("arbitrary", "arbitrary"),
        name="merge",
    )(x, o_a, o_b, proj, proj, mod, wbm, wbs, wo, g_ffn.reshape(1, d), wr_hi, wr_lo, b_router)


ROUTE_COLS = 8
SEG_ALIGN = 8


def _local_rows(tm):
    return 2 * tm + N_EXPERTS * SEG_ALIGN


def _route_tile(lg_t, first_tile, rt_ref, slot_t_ref, cnt_ref, before_ref, carry_ref):
    tm = lg_t.shape[1]
    rows = -(-(N_GROUPS + N_EXPERTS) // 8) * 8
    lg = lg_t[:rows]
    row = lax.broadcasted_iota(jnp.int32, (rows, tm), 0).astype(F32)

    @pl.when(first_tile)
    def _init():
        carry_ref[...] = jnp.zeros_like(carry_ref)

    def first_max(vals):
        top = jnp.max(vals, axis=0, keepdims=True)
        idx = jnp.min(jnp.where(vals == top, row, float(rows)), axis=0, keepdims=True)
        return top, idx

    g_logit = jnp.where(row < N_GROUPS, lg, NEG_INF)
    g_top, g_idx = first_max(g_logit)
    g_w = 1.0 / jnp.sum(jnp.exp(g_logit - g_top), axis=0, keepdims=True)
    lo = N_GROUPS + g_idx * EXPERTS_PER_GROUP
    e_logit = jnp.where(jnp.logical_and(row >= lo, row < lo + EXPERTS_PER_GROUP), lg, NEG_INF)
    e_top0, row0 = first_max(e_logit)
    e_top1, row1 = first_max(jnp.where(row == row0, NEG_INF, e_logit))
    z = jnp.exp(e_top1 - e_top0)
    w0 = g_w / (1.0 + z)
    w1 = g_w * z / (1.0 + z)

    hot = jnp.logical_or(row == row0, row == row1)
    t_src = lax.broadcasted_iota(jnp.int32, (tm, tm), 0)
    t_dst = lax.broadcasted_iota(jnp.int32, (tm, tm), 1)
    earlier = jnp.dot(_indicator(hot, BF16), _indicator(t_src < t_dst, BF16), preferred_element_type=F32)
    cnt = jnp.sum(_indicator(hot, F32), axis=1, keepdims=True)
    chunks = jnp.ceil(cnt * (1.0 / SEG_ALIGN))
    spare = jnp.zeros((LANES - rows, LANES), F32)
    e_dst = lax.broadcasted_iota(jnp.int32, (LANES, LANES), 0)
    e_src = lax.broadcasted_iota(jnp.int32, (LANES, LANES), 1)
    chunks_wide = jnp.concatenate([jnp.broadcast_to(chunks, (rows, LANES)), spare], axis=0).astype(BF16)
    seg_start = SEG_ALIGN * jnp.dot(_indicator(e_src < e_dst, BF16), chunks_wide,
                                    preferred_element_type=F32)[:rows, 0:1]
    local = seg_start + earlier
    slot0 = jnp.sum(jnp.where(row == row0, local, 0.0), axis=0, keepdims=True)
    slot1 = jnp.sum(jnp.where(row == row1, local, 0.0), axis=0, keepdims=True)
    seg_rows = jnp.concatenate([SEG_ALIGN * chunks, spare[:, 0:1]], axis=0)
    cnt_ref[0] = seg_rows
    before_ref[0] = carry_ref[...]
    carry_ref[...] += seg_rows

    out_row = lax.broadcasted_iota(jnp.int32, (ROUTE_COLS, tm), 0)
    packed = jnp.zeros((ROUTE_COLS, tm), F32)
    for c, v in enumerate([slot0, slot1, w0, w1]):
        packed = jnp.where(out_row == c, v, packed)
    slot_t_ref[0] = packed
    rt_ref[0] = jnp.concatenate([packed, jnp.zeros((LANES - ROUTE_COLS, tm), F32)], axis=0).T[:, :ROUTE_COLS]


CHUNK_ROWS = (8 * SEG_ALIGN, 4 * SEG_ALIGN, 2 * SEG_ALIGN, SEG_ALIGN)


def _plan_refs(refs):
    return tuple(tuple(refs[3 * k:3 * k + 3]) for k in range(len(CHUNK_ROWS)))


def _for_each_chunk(tile, plan, visit):
    for (n_ref, local_ref, global_ref), n_rows in zip(plan, CHUNK_ROWS):
        per_tile = local_ref.shape[0] // n_ref.shape[0]

        def one(c, carry, local_ref=local_ref, global_ref=global_ref, per_tile=per_tile, n_rows=n_rows):
            k = tile * per_tile + c
            visit(pl.multiple_of(local_ref[k], SEG_ALIGN), pl.multiple_of(global_ref[k], SEG_ALIGN), n_rows)
            return carry

        lax.fori_loop(0, n_ref[tile], one, 0)


def _copy_plan(cnt, seg_row0, local_rows):
    done = jnp.zeros_like(cnt)
    local0 = jnp.cumsum(cnt, axis=1) - cnt
    plan = ()
    for k, size in enumerate(CHUNK_ROWS):
        left = cnt - done
        n = left // size if k == 0 else (left // size) % 2
        max_entries = local_rows // size if k == 0 else N_EXPERTS
        end = jnp.cumsum(n, axis=1)
        start = (end - n)[:, None, :]
        c = jnp.arange(max_entries, dtype=jnp.int32)[None, :, None]
        inside = jnp.logical_and(c >= start, c < end[:, None, :])
        pick = lambda base: jnp.sum(jnp.where(inside, base[:, None, :] + size * (c - start), 0), axis=2).reshape(-1)
        plan += (end[:, -1], pick(local0 + done), pick(seg_row0 + done))
        done = done + n * size
    return plan


def _dispatch_kernel(*refs):
    plan = _plan_refs(refs)
    (tail_row_ref, tail_n_ref, n_used_ref, slot_t_ref, h_ref, xs_ref,
     loc_ref, zero_ref, sems, zero_sem) = refs[3 * len(CHUNK_ROWS):]
    tile = pl.program_id(0)
    n_loc, tm = loc_ref.shape[1], h_ref.shape[0]
    buf = tile & 1

    slot = lax.broadcasted_iota(jnp.int32, (n_loc, tm), 0).astype(F32)
    place = jnp.logical_or(slot == slot_t_ref[0, 0:1, :], slot == slot_t_ref[0, 1:2, :])
    loc_ref[buf] = _pack_rows(jnp.dot(_indicator(place, BF16), h_ref[...], preferred_element_type=F32))

    def chunk_copy(b):
        return lambda local, row, n: pltpu.make_async_copy(
            loc_ref.at[b, pl.ds(local, n), :], xs_ref.at[pl.ds(row, n), :], sems.at[b])

    _for_each_chunk(tile, plan, lambda l, r, n: chunk_copy(buf)(l, r, n).start())

    @pl.when(tile > 0)
    def _previous_done():
        _for_each_chunk(tile - 1, plan, lambda l, r, n: chunk_copy(1 - buf)(l, r, n).wait())

    @pl.when(tile == pl.num_programs(0) - 1)
    def _last_done():
        _for_each_chunk(tile, plan, lambda l, r, n: chunk_copy(buf)(l, r, n).wait())
        zero_ref[...] = jnp.zeros_like(zero_ref)
        block_rows = zero_ref.shape[0]

        def fill(action):
            def per_expert(e, carry):
                def per_chunk(c, inner):
                    row = pl.multiple_of(tail_row_ref[e] + c * SEG_ALIGN, SEG_ALIGN)
                    action(pltpu.make_async_copy(zero_ref.at[pl.ds(0, SEG_ALIGN), :],
                                                 xs_ref.at[pl.ds(row, SEG_ALIGN), :], zero_sem))
                    return inner

                lax.fori_loop(0, tail_n_ref[e], per_chunk, 0)
                return carry

            lax.fori_loop(0, N_EXPERTS, per_expert, 0)

            def per_block(blk, carry):
                row = pl.multiple_of(blk * block_rows, block_rows)
                action(pltpu.make_async_copy(zero_ref, xs_ref.at[pl.ds(row, block_rows), :], zero_sem))
                return carry

            lax.fori_loop(n_used_ref[0], xs_ref.shape[0] // block_rows, per_block, 0)

        fill(lambda copy: copy.start())
        fill(lambda copy: copy.wait())


def _dispatch(plan, tail_row, tail_chunks, n_used, slot_t, h2, n_rows):
    n, d = h2.shape
    tm = TOKEN_TILE
    grid_spec = pltpu.PrefetchScalarGridSpec(
        num_scalar_prefetch=len(plan) + 3,
        grid=(n // tm,),
        in_specs=[
            pl.BlockSpec((1, ROUTE_COLS, tm), lambda i, *_: (i, 0, 0)),
            pl.BlockSpec((tm, d), lambda i, *_: (i, 0)),
        ],
        out_specs=pl.BlockSpec(memory_space=pl.ANY),
        scratch_shapes=[pltpu.VMEM((2, _local_rows(tm), d // 2), U32), pltpu.VMEM((ROW_BLOCK, d // 2), U32),
                        pltpu.SemaphoreType.DMA((2,)), pltpu.SemaphoreType.DMA(())],
    )
    return pl.pallas_call(
        _dispatch_kernel,
        grid_spec=grid_spec,
        out_shape=jax.ShapeDtypeStruct((n_rows, d // 2), U32),
        compiler_params=_params("arbitrary"),
        name="dispatch",
    )(*plan, tail_row, tail_chunks, n_used, slot_t, h2)


def _expert_kernel(blk_e_ref, n_used_ref, xs_ref, wg_ref, wu_ref, wd_ref, ys_ref, wg_bf, wu_bf, wd_bf):
    i = pl.program_id(0)
    used = i < n_used_ref[0]

    @pl.when(jnp.logical_and(used, jnp.logical_or(i == 0, blk_e_ref[i] != blk_e_ref[jnp.maximum(i - 1, 0)])))
    def _new_expert():
        wg_bf[...] = wg_ref[0].astype(BF16)
        wu_bf[...] = wu_ref[0].astype(BF16)
        wd_bf[...] = wd_ref[0].astype(BF16)

    @pl.when(used)
    def _compute():
        xb = _unpack_rows(xs_ref[...]).astype(BF16)
        g = jnp.dot(xb, wg_bf[...], preferred_element_type=F32)
        u = jnp.dot(xb, wu_bf[...], preferred_element_type=F32)
        hid = (g * jax.nn.sigmoid(g) * u).astype(BF16)
        ys_ref[...] = _pack_rows(jnp.dot(hid, wd_bf[...], preferred_element_type=F32))

    @pl.when(i >= n_used_ref[0])
    def _unused():
        ys_ref[...] = jnp.zeros_like(ys_ref)


def _experts(blk_e, n_used, xs, wg, wu, wd):
    n_rows, half = xs.shape
    _, d, de = wg.shape
    grid_spec = pltpu.PrefetchScalarGridSpec(
        num_scalar_prefetch=2,
        grid=(n_rows // ROW_BLOCK,),
        in_specs=[
            pl.BlockSpec((ROW_BLOCK, half), lambda i, e, u: (jnp.minimum(i, u[0] - 1), 0)),
            pl.BlockSpec((1, d, de), lambda i, e, u: (e[i], 0, 0)),
            pl.BlockSpec((1, d, de), lambda i, e, u: (e[i], 0, 0)),
            pl.BlockSpec((1, de, d), lambda i, e, u: (e[i], 0, 0)),
        ],
        out_specs=pl.BlockSpec((ROW_BLOCK, half), lambda i, e, u: (i, 0)),
        scratch_shapes=[pltpu.VMEM((d, de), BF16), pltpu.VMEM((d, de), BF16), pltpu.VMEM((de, d), BF16)],
    )
    return pl.pallas_call(
        _expert_kernel,
        grid_spec=grid_spec,
        out_shape=jax.ShapeDtypeStruct((n_rows, half), U32),
        compiler_params=_params("arbitrary"),
        name="experts",
    )(blk_e, n_used, xs, wg, wu, wd)


def _combine_kernel(*refs):
    plan = _plan_refs(refs)
    x1_ref, mod_ref, rt_ref, ys_ref, o_ref, loc_ref, sems = refs[3 * len(CHUNK_ROWS):]
    tm = x1_ref.shape[1]
    n_loc = loc_ref.shape[1]
    tile = pl.program_id(0) * pl.num_programs(1) + pl.program_id(1)
    n_tiles = pl.num_programs(0) * pl.num_programs(1)
    buf = tile & 1

    def chunk_copy(b):
        return lambda local, row, n: pltpu.make_async_copy(
            ys_ref.at[pl.ds(row, n), :], loc_ref.at[b, pl.ds(local, n), :], sems.at[b])

    @pl.when(tile == 0)
    def _init():
        loc_ref[...] = jnp.zeros_like(loc_ref)
        _for_each_chunk(tile, plan, lambda l, r, n: chunk_copy(buf)(l, r, n).start())

    @pl.when(tile + 1 < n_tiles)
    def _fetch_next():
        _for_each_chunk(tile + 1, plan, lambda l, r, n: chunk_copy(1 - buf)(l, r, n).start())

    _for_each_chunk(tile, plan, lambda l, r, n: chunk_copy(buf)(l, r, n).wait())

    y = _unpack_rows(loc_ref[buf]).astype(BF16)
    rt = rt_ref[0]
    slot = lax.broadcasted_iota(jnp.int32, (tm, n_loc), 1).astype(F32)
    mix = jnp.where(slot == rt[:, 0:1], rt[:, 2:3], jnp.where(slot == rt[:, 1:2], rt[:, 3:4], 0.0)).astype(BF16)
    o_ref[0] = x1_ref[0] + mod_ref[0, 5:6, :] * jnp.dot(mix, y, preferred_element_type=F32)


def _combine(plan, x1, mod, rt, ys):
    bsz, s, d = x1.shape
    tm = TOKEN_TILE
    grid_spec = pltpu.PrefetchScalarGridSpec(
        num_scalar_prefetch=len(plan),
        grid=(bsz, s // tm),
        in_specs=[
            pl.BlockSpec((1, tm, d), lambda b, i, *_: (b, i, 0)),
            pl.BlockSpec((1, N_MOD, d), lambda b, i, *_: (b, 0, 0)),
            pl.BlockSpec((1, tm, ROUTE_COLS), lambda b, i, *_: (b, i, 0)),
            pl.BlockSpec(memory_space=pl.ANY),
        ],
        out_specs=pl.BlockSpec((1, tm, d), lambda b, i, *_: (b, i, 0)),
        scratch_shapes=[pltpu.VMEM((2, _local_rows(tm), d // 2), U32), pltpu.SemaphoreType.DMA((2,))],
    )
    return pl.pallas_call(
        _combine_kernel,
        grid_spec=grid_spec,
        out_shape=jax.ShapeDtypeStruct((bsz, s, d), F32),
        compiler_params=_params("arbitrary", "arbitrary"),
        name="combine",
    )(*plan, x1, mod, rt, ys)


def _layer(x, mod, g_mix, w_in, g_q, g_k, rel_bias, w_br_moba, w_br_sb, w_out, g_ffn,
           w_rg, b_rg, w_re, b_re, w_gate, w_up, w_down):
    bsz, s, d = x.shape
    n = bsz * s
    n_blk = s // ATT_BLOCK

    proj, kaug, vaug, kmean = _inproj(x, mod, g_mix, w_in.astype(BF16), g_k)

    top = (n_blk + 1) * ATT_BLOCK - 1
    dist = np.maximum(top - np.arange((n_blk + 2) * ATT_BLOCK), 0)
    rel_rev = rel_bias[:, _rel_bucket_table(top)[dist]].reshape(N_HEADS // HEADS_PER_STEP, HEADS_PER_STEP, -1)

    o_a = _moba(proj, kaug, vaug, kmean, g_q, rel_rev)
    o_b = _stick_breaking(proj)

    pad = LANES - N_GROUPS - N_EXPERTS
    w_router = jnp.concatenate([w_rg, w_re, jnp.zeros((d, pad), F32)], axis=1)
    b_router = jnp.concatenate([b_rg, b_re, jnp.zeros((pad,), F32)]).reshape(LANES, 1)
    x1, h2, rt, slot_t, tile_cnt, tile_before = _merge(x, o_a, o_b, proj, mod, w_br_moba.astype(BF16),
                                                       w_br_sb.astype(BF16), w_out.astype(BF16), g_ffn,
                                                       w_router, b_router)

    experts = slice(N_GROUPS, N_GROUPS + N_EXPERTS)
    cnt = tile_cnt[:, experts, 0].astype(jnp.int32)
    total = jnp.sum(cnt, axis=0)
    padded = (total + ROW_BLOCK - 1) // ROW_BLOCK * ROW_BLOCK
    pend = jnp.cumsum(padded)
    pstart = pend - padded
    max_rows = 2 * n + cnt.shape[0] * N_EXPERTS * (SEG_ALIGN - 1) + N_EXPERTS * (ROW_BLOCK - 1)
    n_blocks = -(-max_rows // ROW_BLOCK)
    blk_row0 = jnp.arange(n_blocks, dtype=jnp.int32) * ROW_BLOCK
    blk_e = jnp.minimum(jnp.sum((blk_row0[:, None] >= pend[None, :]).astype(jnp.int32), axis=1), N_EXPERTS - 1)
    n_used = (pend[-1] // ROW_BLOCK).astype(jnp.int32).reshape(1)
    seg_row0 = pstart[None, :] + tile_before[:, experts, 0].astype(jnp.int32)
    plan = _copy_plan(cnt, seg_row0, _local_rows(TOKEN_TILE))

    xs = _dispatch(plan, pstart + total, (padded - total) // SEG_ALIGN, n_used, slot_t,
                   h2.reshape(n, d), n_blocks * ROW_BLOCK)
    ys = _experts(blk_e, n_used, xs, w_gate, w_up, w_down)
    return _combine(plan, x1, mod, rt, ys)


def kernel(x, c, w_ada, b_ada, g_mix, w_in, g_q, g_k, rel_bias, w_br_moba, w_br_sb, w_out, g_ffn,
           w_rg, b_rg, w_re, b_re, w_gate, w_up, w_down):
    bsz, s, d = x.shape
    assert s % TOKEN_TILE == 0 and s % MOBA_Q_TILE == 0 and s % (SB_Q_BLOCKS * ATT_BLOCK) == 0
    assert d % (2 * LANES) == 0
    for l in range(w_ada.shape[0]):
        mod = _adaln(c, w_ada[l], b_ada[l]).reshape(bsz, N_MOD, d)
        x = _layer(x, mod, g_mix[l], w_in[l], g_q[l], g_k[l], rel_bias, w_br_moba[l], w_br_sb[l], w_out[l],
                   g_ffn[l], w_rg[l], b_rg[l], w_re[l], b_re[l], w_gate[l], w_up[l], w_down[l])
    return x
```

```python
import functools
import math

import numpy as np
import jax
import jax.numpy as jnp
from jax import lax
from jax.experimental import pallas as pl
from jax.experimental.pallas import tpu as pltpu

F32 = jnp.float32
BF16 = jnp.bfloat16
U32 = jnp.uint32
HIGHEST = lax.Precision.HIGHEST

LANES = 128
VMEM_LIMIT_BYTES = 56 * 1024 * 1024

HEAD_DIM = 64
N_HEADS = 8
HEADS_PER_STEP = LANES // HEAD_DIM
ATT_BLOCK = 256
MOBA_Q_TILE = 2 * ATT_BLOCK
SB_Q_BLOCKS = 4
MOBA_TOPK = 3
REL_BUCKETS = 32
REL_MAX_DIST = 1024
N_GROUPS = 4
EXPERTS_PER_GROUP = 8
N_EXPERTS = N_GROUPS * EXPERTS_PER_GROUP
N_MOD = 6
RMS_EPS = 1e-6
ROW_BLOCK = 1024
TOKEN_TILE = 512
NEG_INF = float("-inf")
MASK_LOGIT = -1e30
SB_EXIT_SUM = 110.0

_NT = (((1,), (1,)), ((), ()))


def _params(*semantics):
    return pltpu.CompilerParams(dimension_semantics=semantics, vmem_limit_bytes=VMEM_LIMIT_BYTES)


def _adaln_kernel(c_ref, w_ref, b_ref, o_ref):
    c = c_ref[...]
    s = c * jax.nn.sigmoid(c)
    o_ref[...] = jnp.dot(s, w_ref[...], preferred_element_type=F32, precision=HIGHEST) + b_ref[...]


def _adaln(c, w, b):
    bsz, d = c.shape
    n = w.shape[1]
    tn = 1536
    return pl.pallas_call(
        _adaln_kernel,
        grid=(n // tn,),
        in_specs=[
            pl.BlockSpec((bsz, d), lambda j: (0, 0)),
            pl.BlockSpec((d, tn), lambda j: (0, j)),
            pl.BlockSpec((1, tn), lambda j: (0, j)),
        ],
        out_specs=pl.BlockSpec((bsz, tn), lambda j: (0, j)),
        out_shape=jax.ShapeDtypeStruct((bsz, n), F32),
        compiler_params=_params("arbitrary"),
        name="adaln",
    )(c, w, b.reshape(1, n))


def _rms_modulate(x, g, shift, scale):
    y = x * lax.rsqrt(jnp.mean(x * x, axis=-1, keepdims=True) + RMS_EPS) * g
    return y * (1.0 + scale) + shift


def _inproj_kernel(x_ref, mod_ref, g_ref, w_ref, gk_ref, o_ref, kaug_ref, vaug_ref, kmean_ref, *, col_chunk):
    i = pl.program_id(1)
    tm = x_ref.shape[1]
    h = _rms_modulate(x_ref[0], g_ref[...], mod_ref[0, 0:1, :], mod_ref[0, 1:2, :]).astype(BF16)

    w_moba = N_HEADS * HEAD_DIM
    blocks = tm // ATT_BLOCK
    first = _head_lane_mask()
    row_blk = blocks * i + lax.shift_right_logical(lax.broadcasted_iota(jnp.int32, (tm, LANES), 0),
                                                   ATT_BLOCK.bit_length() - 1)
    tag = jnp.where(_lane_index() - float(HEAD_DIM) == row_blk.astype(F32), 1.0, 0.0)
    mean_row = lax.broadcasted_iota(jnp.int32, kmean_ref.shape[2:], 0)

    @pl.when(i == 0)
    def _init():
        kmean_ref[...] = jnp.zeros_like(kmean_ref)

    for n in range(w_ref.shape[1] // col_chunk):
        cols = slice(n * col_chunk, (n + 1) * col_chunk)
        r = jnp.dot(h, w_ref[:, cols], preferred_element_type=F32)
        o_ref[0, :, cols] = r.astype(BF16)
        for c in range(col_chunk // LANES):
            col0 = n * col_chunk + c * LANES
            part = r[:, c * LANES:(c + 1) * LANES]
            if w_moba <= col0 < 2 * w_moba:
                kn = _head_rmsnorm(part, gk_ref[...])
                for hh, kh in enumerate((kn, pltpu.roll(kn, HEAD_DIM, 1))):
                    head = (col0 - w_moba) // HEAD_DIM + hh
                    kaug_ref[0, head] = jnp.where(first, kh, tag).astype(BF16)
                    means = kmean_ref[0, head]
                    for jb in range(blocks):
                        mean = jnp.mean(kh[jb * ATT_BLOCK:(jb + 1) * ATT_BLOCK], axis=0, keepdims=True)
                        means = jnp.where(mean_row == blocks * i + jb, jnp.where(first, mean, 0.0), means)
                    kmean_ref[0, head] = means
            elif 2 * w_moba <= col0 < 3 * w_moba:
                for hh, vh in enumerate((part, pltpu.roll(part, HEAD_DIM, 1))):
                    head = (col0 - 2 * w_moba) // HEAD_DIM + hh
                    vaug_ref[0, head] = jnp.where(first, vh, 1.0).astype(BF16)


def _inproj(x, mod, g, w_bf16, g_k):
    bsz, s, d = x.shape
    n = w_bf16.shape[1]
    tm = TOKEN_TILE
    n_blk = s // ATT_BLOCK
    per_head = lambda rows: pl.BlockSpec((1, N_HEADS, rows, LANES), lambda b, i: (b, 0, i, 0))
    return pl.pallas_call(
        functools.partial(_inproj_kernel, col_chunk=1024),
        grid=(bsz, s // tm),
        in_specs=[
            pl.BlockSpec((1, tm, d), lambda b, i: (b, i, 0)),
            pl.BlockSpec((1, N_MOD, d), lambda b, i: (b, 0, 0)),
            pl.BlockSpec((1, d), lambda b, i: (0, 0)),
            pl.BlockSpec((d, n), lambda b, i: (0, 0)),
            pl.BlockSpec((1, LANES), lambda b, i: (0, 0)),
        ],
        out_specs=[
            pl.BlockSpec((1, tm, n), lambda b, i: (b, i, 0)),
            per_head(tm),
            per_head(tm),
            pl.BlockSpec((1, N_HEADS, -(-n_blk // 8) * 8, LANES), lambda b, i: (b, 0, 0, 0)),
        ],
        out_shape=[
            jax.ShapeDtypeStruct((bsz, s, n), BF16),
            jax.ShapeDtypeStruct((bsz, N_HEADS, s, LANES), BF16),
            jax.ShapeDtypeStruct((bsz, N_HEADS, s, LANES), BF16),
            jax.ShapeDtypeStruct((bsz, N_HEADS, -(-n_blk // 8) * 8, LANES), F32),
        ],
        compiler_params=_params("arbitrary", "arbitrary"),
        name="inproj",
    )(x, mod, g.reshape(1, d), w_bf16, jnp.tile(g_k.reshape(1, HEAD_DIM), (1, HEADS_PER_STEP)))


def _indicator(cond, dtype):
    return jnp.where(cond, 1.0, 0.0).astype(dtype)


def _lane_index():
    return lax.broadcasted_iota(jnp.int32, (1, LANES), 1).astype(F32)


def _head_lane_mask():
    return lax.broadcasted_iota(jnp.int32, (1, LANES), 1) < HEAD_DIM


def _head_rmsnorm(t, g):
    first = _head_lane_mask()
    sq = t * t
    ss0 = jnp.sum(jnp.where(first, sq, 0.0), axis=-1, keepdims=True)
    ss1 = jnp.sum(jnp.where(first, 0.0, sq), axis=-1, keepdims=True)
    inv = jnp.where(first, lax.rsqrt(ss0 / HEAD_DIM + RMS_EPS), lax.rsqrt(ss1 / HEAD_DIM + RMS_EPS))
    return t * inv * g


def _split_heads(t):
    first = _head_lane_mask()
    zero = jnp.zeros_like(t)
    return jnp.where(first, t, zero), jnp.where(first, zero, t)


def _rel_bucket_table(max_dist):
    n = np.arange(max_dist + 1)
    max_exact = REL_BUCKETS // 2
    nf = np.maximum(n, 1).astype(np.float64)
    large = max_exact + (np.log(nf / max_exact) / math.log(REL_MAX_DIST / max_exact)
                         * (REL_BUCKETS - max_exact)).astype(np.int64)
    large = np.minimum(large, REL_BUCKETS - 1)
    return np.where(n < max_exact, n, large).astype(np.int32)


def _moba_kernel(q_ref, kaug_ref, vaug_ref, kmean_ref, gq_ref, rb_ref, o_ref,
                 toep_ref, s_ref, mvec_ref, acc_ref, *, n_blk):
    b = pl.program_id(1)
    qi = pl.program_id(2)
    blk = ATT_BLOCK
    qt = q_ref.shape[1]
    halves = qt // blk
    first = _head_lane_mask()

    @pl.when(jnp.logical_and(b == 0, qi == 0))
    def _build_bias_tiles():
        for h in range(HEADS_PER_STEP):
            for d in range(n_blk):
                r = rb_ref[0, h:h + 1, (n_blk - d) * blk:(n_blk - d + 2) * blk]
                rolled = pltpu.roll(jnp.broadcast_to(r, (blk, 2 * blk)), blk + 1, 1, stride=1, stride_axis=0)
                toep_ref[h, d] = rolled[:, :blk]

    qn = _head_rmsnorm(q_ref[0].astype(F32), gq_ref[...])
    gate_rows = kmean_ref.shape[2]
    blk_row = lax.broadcasted_iota(jnp.int32, (gate_rows, qt), 0).astype(F32)
    own = (halves * qi + lax.shift_right_logical(lax.broadcasted_iota(jnp.int32, (gate_rows, qt), 1),
                                                 blk.bit_length() - 1)).astype(F32)
    q_aug = []
    for h, qh in enumerate((qn, pltpu.roll(qn, HEAD_DIM, 1))):
        qh = jnp.where(first, qh, 0.0)
        km = kmean_ref[0, h]
        km_hi, q_hi = km.astype(BF16), qh.astype(BF16)
        km_lo, q_lo = (km - km_hi.astype(F32)).astype(BF16), (qh - q_hi.astype(F32)).astype(BF16)
        g = (lax.dot_general(km_hi, q_hi, _NT, preferred_element_type=F32)
             + lax.dot_general(km_lo, q_hi, _NT, preferred_element_type=F32)
             + lax.dot_general(km_hi, q_lo, _NT, preferred_element_type=F32))
        g = jnp.where(blk_row < own, g, NEG_INF)
        keep = jnp.where(blk_row == own, 1.0, 0.0)
        for _ in range(MOBA_TOPK):
            top = jnp.max(g, axis=0, keepdims=True)
            is_top = jnp.logical_and(g == top, top > NEG_INF)
            idx = jnp.min(jnp.where(is_top, blk_row, float(gate_rows)), axis=0, keepdims=True)
            pick = blk_row == idx
            keep = jnp.where(pick, 1.0, keep)
            g = jnp.where(pick, NEG_INF, g)
        mask_t = jnp.where(keep > 0.0, 0.0, MASK_LOGIT)
        mask_logit = jnp.concatenate([jnp.zeros((HEAD_DIM, qt), F32), mask_t,
                                      jnp.zeros((LANES - HEAD_DIM - gate_rows, qt), F32)], axis=0).T
        q_aug.append(jnp.where(first, qh * (HEAD_DIM ** -0.5), mask_logit).astype(BF16))

    def lane_max(s):
        return jnp.maximum(s[:, :LANES], s[:, LANES:])

    def scores(h, t, dist):
        k_t = kaug_ref[0, h, pl.ds(pl.multiple_of(t * qt, qt), qt), :]
        s = lax.dot_general(q_aug[h], k_t, _NT, preferred_element_type=F32)
        tiles = []
        for kb in range(halves):
            rows = []
            for qb in range(halves):
                part = s[qb * blk:(qb + 1) * blk, kb * blk:(kb + 1) * blk]
                if isinstance(dist, int) and halves * dist + qb - kb < 0:
                    rows.append(part)
                else:
                    rows.append(part + toep_ref[h, halves * dist + qb - kb])
            tiles.append(jnp.concatenate(rows, axis=0))
        return tiles

    row = lax.broadcasted_iota(jnp.int32, (qt, blk), 0)
    col = lax.broadcasted_iota(jnp.int32, (qt, blk), 1)
    for h in range(HEADS_PER_STEP):
        top = None
        for kb, s in enumerate(scores(h, qi, 0)):
            s = jnp.where(col + kb * blk <= row, s, NEG_INF)
            s_ref[h, halves * qi + kb] = s
            top = lane_max(s) if top is None else jnp.maximum(top, lane_max(s))
        mvec_ref[h] = top

    def in_pairs(count, visit):
        def quad(p, carry):
            visit(4 * p, 4)
            return carry

        lax.fori_loop(0, lax.shift_right_logical(count, 2), quad, 0)

        @pl.when((count & 2) == 2)
        def _pair():
            visit(count & ~3, 2)

        @pl.when((count & 1) == 1)
        def _last():
            visit(count - 1, 1)

    def pass1(t0, n_tiles):
        for h in range(HEADS_PER_STEP):
            top = mvec_ref[h]
            for t in [t0 + u for u in range(n_tiles)]:
                for kb, s in enumerate(scores(h, t, qi - t)):
                    s_ref[h, halves * t + kb] = s
                    top = jnp.maximum(top, lane_max(s))
            mvec_ref[h] = top

    in_pairs(qi, pass1)

    for h in range(HEADS_PER_STEP):
        m = jnp.max(mvec_ref[h], axis=-1, keepdims=True)
        mvec_ref[h] = jnp.broadcast_to(m, (qt, LANES))
        acc_ref[h] = jnp.zeros((qt, LANES), F32)

    def pass2(t0, n_tiles):
        keys = pl.ds(pl.multiple_of(t0 * qt, qt), n_tiles * qt)
        for h in range(HEADS_PER_STEP):
            m = mvec_ref[h]
            parts = []
            for kb in range(n_tiles * halves):
                s = s_ref[h, halves * t0 + kb]
                parts += [jnp.exp(s[:, :LANES] - m), jnp.exp(s[:, LANES:] - m)]
            p = jnp.concatenate(parts, axis=1).astype(BF16)
            acc_ref[h] += jnp.dot(p, vaug_ref[0, h, keys, :], preferred_element_type=F32)

    in_pairs(qi + 1, pass2)

    out = [acc_ref[h] / acc_ref[h][:, HEAD_DIM:HEAD_DIM + 1] for h in range(HEADS_PER_STEP)]
    o_ref[0] = jnp.where(first, out[0], pltpu.roll(out[1], HEAD_DIM, 1)).astype(o_ref.dtype)


def _moba(proj, kaug, vaug, kmean, g_q, rel_rev):
    bsz, s, _ = proj.shape
    blk = ATT_BLOCK
    n_blk = s // blk
    n_hp = N_HEADS // HEADS_PER_STEP
    w = N_HEADS * HEAD_DIM
    qt = MOBA_Q_TILE
    per_pair = lambda arr: pl.BlockSpec((1, HEADS_PER_STEP) + arr.shape[2:], lambda hp, b, i: (b, hp, 0, 0))
    return pl.pallas_call(
        functools.partial(_moba_kernel, n_blk=n_blk),
        grid=(n_hp, bsz, s // qt),
        in_specs=[
            pl.BlockSpec((1, qt, LANES), lambda hp, b, i: (b, i, hp)),
            per_pair(kaug), per_pair(vaug), per_pair(kmean),
            pl.BlockSpec((1, LANES), lambda hp, b, i: (0, 0)),
            pl.BlockSpec((1, HEADS_PER_STEP, rel_rev.shape[-1]), lambda hp, b, i: (hp, 0, 0)),
        ],
        out_specs=pl.BlockSpec((1, qt, LANES), lambda hp, b, i: (b, i, hp)),
        out_shape=jax.ShapeDtypeStruct((bsz, s, w), BF16),
        scratch_shapes=[
            pltpu.VMEM((HEADS_PER_STEP, n_blk, blk, blk), F32),
            pltpu.VMEM((HEADS_PER_STEP, n_blk, qt, blk), F32),
            pltpu.VMEM((HEADS_PER_STEP, qt, LANES), F32),
            pltpu.VMEM((HEADS_PER_STEP, qt, LANES), F32),
        ],
        compiler_params=_params("arbitrary", "arbitrary", "arbitrary"),
        name="moba",
    )(proj, kaug, vaug, kmean, jnp.tile(g_q.reshape(1, HEAD_DIM), (1, HEADS_PER_STEP)), rel_rev)


def _sb_kernel(q_ref, k_ref, v_ref, o_ref, acc_ref, carry_ref):
    ti = pl.program_id(2)
    blk = ATT_BLOCK
    n_sub = q_ref.shape[1] // blk
    first = n_sub * ti
    row = lax.broadcasted_iota(jnp.int32, (blk, blk), 0)
    col = lax.broadcasted_iota(jnp.int32, (blk, blk), 1)
    past = col < row
    later = _indicator(row > col, BF16)

    q_heads = []
    for sub in range(n_sub):
        heads = _split_heads(q_ref[0, sub * blk:(sub + 1) * blk, :].astype(F32) * (HEAD_DIM ** -0.5))
        q_heads.append([t.astype(BF16) for t in heads])

    def walk(jobs, fresh):
        for sub, blocks in jobs:
            for h in range(HEADS_PER_STEP):
                carry = None if fresh else carry_ref[sub, h]
                total = None if fresh else acc_ref[sub, h]
                for j, diagonal in blocks:
                    start = j * blk if isinstance(j, int) else pl.multiple_of(j * blk, blk)
                    z = lax.dot_general(q_heads[sub][h], k_ref[0, pl.ds(start, blk), :], _NT,
                                        preferred_element_type=F32)
                    sp = jnp.maximum(z, 0.0) + jnp.log(1.0 + jnp.exp(-jnp.abs(z)))
                    masked = jnp.where(past, sp, 0.0) if diagonal else sp
                    after = jnp.dot(masked.astype(BF16), later, preferred_element_type=F32)
                    row_sum = after[:, 0:1] + masked[:, 0:1]
                    a = jnp.exp(z - sp - after)
                    if diagonal:
                        a = jnp.where(past, a, 0.0)
                    pv = jnp.dot(a.astype(BF16), v_ref[0, pl.ds(start, blk), :], preferred_element_type=F32)
                    if carry is not None:
                        pv = pv * jnp.exp(-carry)
                    total = pv if total is None else total + pv
                    carry = row_sum if carry is None else carry + row_sum
                carry_ref[sub, h] = carry
                acc_ref[sub, h] = total

    @pl.when(ti == 0)
    def _first_step():
        walk([(sub, [(sub, True)] + ([(sub - 1, False)] if sub else [])) for sub in range(n_sub)], True)

    @pl.when(ti > 0)
    def _own_and_previous():
        walk([(sub, [(first + sub, True), (first + sub - 1, False)]) for sub in range(n_sub)], True)

    def slowest_decay(subs):
        worst = functools.reduce(jnp.minimum, [carry_ref[sub, h] for sub in subs for h in range(HEADS_PER_STEP)])
        return jnp.min(worst)

    @pl.when(slowest_decay(range(n_sub)) < SB_EXIT_SUM)
    def _walk_further_back():
        for sub in range(n_sub):
            def cond(state):
                j, worst = state
                return jnp.logical_and(j >= 0, worst < SB_EXIT_SUM)

            def body(state, sub=sub):
                j, _ = state
                walk([(sub, [(j, False)])], False)
                return j - 1, slowest_decay([sub])

            lax.while_loop(cond, body, (first + sub - 2, slowest_decay([sub])))

    for sub in range(n_sub):
        o_ref[0, sub * blk:(sub + 1) * blk, :] = jnp.where(
            _head_lane_mask(), acc_ref[sub, 0], acc_ref[sub, 1]).astype(o_ref.dtype)


def _stick_breaking(proj):
    bsz, s, _ = proj.shape
    blk = ATT_BLOCK
    qt = SB_Q_BLOCKS * blk
    n_hp = N_HEADS // HEADS_PER_STEP
    w = N_HEADS * HEAD_DIM
    base = 3 * n_hp
    return pl.pallas_call(
        _sb_kernel,
        grid=(n_hp, bsz, s // qt),
        in_specs=[
            pl.BlockSpec((1, qt, LANES), lambda hp, b, i: (b, i, base + hp)),
            pl.BlockSpec((1, s, LANES), lambda hp, b, i: (b, 0, base + n_hp + hp)),
            pl.BlockSpec((1, s, LANES), lambda hp, b, i: (b, 0, base + 2 * n_hp + hp)),
        ],
        out_specs=pl.BlockSpec((1, qt, LANES), lambda hp, b, i: (b, i, hp)),
        out_shape=jax.ShapeDtypeStruct((bsz, s, w), BF16),
        scratch_shapes=[
            pltpu.VMEM((SB_Q_BLOCKS, HEADS_PER_STEP, blk, LANES), F32),
            pltpu.VMEM((SB_Q_BLOCKS, HEADS_PER_STEP, blk, 1), F32),
        ],
        compiler_params=_params("arbitrary", "arbitrary", "arbitrary"),
        name="stickbrk",
    )(proj, proj, proj)


def _pack_rows(t):
    n = t.shape[1] // 2
    bits = pltpu.bitcast(t.astype(BF16).astype(F32), U32)
    return bits[:, :n] | (bits[:, n:] >> 16)


def _unpack_rows(p):
    hi = pltpu.bitcast(p & jnp.uint32(0xFFFF0000), F32)
    lo = pltpu.bitcast(p << 16, F32)
    return jnp.concatenate([hi, lo], axis=1)


def _merge_kernel(x_ref, oa_ref, ob_ref, ga_ref, gb_ref, mod_ref, wbm_ref, wbs_ref, wo_ref, gf_ref,
                  wrh_ref, wrl_ref, br_ref, x1_ref, h2_ref, rt_ref, slot_t_ref, cnt_ref, before_ref, carry_ref):
    ma = jnp.dot(oa_ref[0], wbm_ref[...], preferred_element_type=F32)
    mb = jnp.dot(ob_ref[0], wbs_ref[...], preferred_element_type=F32)
    merged = (jax.nn.sigmoid(ga_ref[0].astype(F32)) * ma + jax.nn.sigmoid(gb_ref[0].astype(F32)) * mb)
    t = jnp.dot(merged.astype(BF16), wo_ref[...], preferred_element_type=F32)
    x1 = x_ref[0] + mod_ref[0, 2:3, :] * t
    x1_ref[0] = x1
    h2 = _rms_modulate(x1, gf_ref[...], mod_ref[0, 3:4, :], mod_ref[0, 4:5, :])
    hi = h2.astype(BF16)
    h2_ref[0] = hi
    lo = (h2 - hi.astype(F32)).astype(BF16)
    logits = (lax.dot_general(wrh_ref[...], hi, _NT, preferred_element_type=F32)
              + lax.dot_general(wrh_ref[...], lo, _NT, preferred_element_type=F32)
              + lax.dot_general(wrl_ref[...], hi, _NT, preferred_element_type=F32) + br_ref[...])
    first_tile = jnp.logical_and(pl.program_id(0) == 0, pl.program_id(1) == 0)
    _route_tile(logits, first_tile, rt_ref, slot_t_ref, cnt_ref, before_ref, carry_ref)


def _merge(x, o_a, o_b, proj, mod, wbm, wbs, wo, g_ffn, w_router, b_router):
    bsz, s, d = x.shape
    w = o_a.shape[-1]
    tm = TOKEN_TILE
    gate_blk = (proj.shape[-1] - 2 * d) // d
    const = lambda shape: pl.BlockSpec(shape, lambda b, i: (0,) * len(shape))
    tok = lambda width: pl.BlockSpec((1, tm, width), lambda b, i: (b, i, 0))
    wr_hi = w_router.T.astype(BF16)
    wr_lo = (w_router.T - wr_hi.astype(F32)).astype(BF16)
    per_b = s // tm
    tiles = bsz * per_b
    per_tile = lambda rows, cols: pl.BlockSpec((1, rows, cols), lambda b, i: (b * per_b + i, 0, 0))
    return pl.pallas_call(
        _merge_kernel,
        grid=(bsz, s // tm),
        in_specs=[
            tok(d), tok(w), tok(w),
            pl.BlockSpec((1, tm, d), lambda b, i: (b, i, gate_blk)),
            pl.BlockSpec((1, tm, d), lambda b, i: (b, i, gate_blk + 1)),
            pl.BlockSpec((1, N_MOD, d), lambda b, i: (b, 0, 0)),
            const((w, d)), const((w, d)), const((d, d)), const((1, d)),
            const((LANES, d)), const((LANES, d)), const((LANES, 1)),
        ],
        out_specs=[tok(d), tok(d), tok(ROUTE_COLS), per_tile(ROUTE_COLS, tm), per_tile(LANES, 1), per_tile(LANES, 1)],
        out_shape=[
            jax.ShapeDtypeStruct((bsz, s, d), F32),
            jax.ShapeDtypeStruct((bsz, s, d), BF16),
            jax.ShapeDtypeStruct((bsz, s, ROUTE_COLS), F32),
            jax.ShapeDtypeStruct((tiles, ROUTE_COLS, tm), F32),
            jax.ShapeDtypeStruct((tiles, LANES, 1), F32),
            jax.ShapeDtypeStruct((tiles, LANES, 1), F32),
        ],
        scratch_shapes=[pltpu.VMEM((LANES, 1), F32)],
        compiler_params=_params("arbitrary", "arbitrary"),
        name="merge",
    )(x, o_a, o_b, proj, proj, mod, wbm, wbs, wo, g_ffn.reshape(1, d), wr_hi, wr_lo, b_router)


ROUTE_COLS = 8
SEG_ALIGN = 8


def _local_rows(tm):
    return 2 * tm + N_EXPERTS * SEG_ALIGN


def _route_tile(lg_t, first_tile, rt_ref, slot_t_ref, cnt_ref, before_ref, carry_ref):
    tm = lg_t.shape[1]
    rows = -(-(N_GROUPS + N_EXPERTS) // 8) * 8
    lg = lg_t[:rows]
    row = lax.broadcasted_iota(jnp.int32, (rows, tm), 0).astype(F32)

    @pl.when(first_tile)
    def _init():
        carry_ref[...] = jnp.zeros_like(carry_ref)

    def first_max(vals):
        top = jnp.max(vals, axis=0, keepdims=True)
        idx = jnp.min(jnp.where(vals == top, row, float(rows)), axis=0, keepdims=True)
        return top, idx

    g_logit = jnp.where(row < N_GROUPS, lg, NEG_INF)
    g_top, g_idx = first_max(g_logit)
    g_w = 1.0 / jnp.sum(jnp.exp(g_logit - g_top), axis=0, keepdims=True)
    lo = N_GROUPS + g_idx * EXPERTS_PER_GROUP
    e_logit = jnp.where(jnp.logical_and(row >= lo, row < lo + EXPERTS_PER_GROUP), lg, NEG_INF)
    e_top0, row0 = first_max(e_logit)
    e_top1, row1 = first_max(jnp.where(row == row0, NEG_INF, e_logit))
    z = jnp.exp(e_top1 - e_top0)
    w0 = g_w / (1.0 + z)
    w1 = g_w * z / (1.0 + z)

    hot = jnp.logical_or(row == row0, row == row1)
    t_src = lax.broadcasted_iota(jnp.int32, (tm, tm), 0)
    t_dst = lax.broadcasted_iota(jnp.int32, (tm, tm), 1)
    earlier = jnp.dot(_indicator(hot, BF16), _indicator(t_src < t_dst, BF16), preferred_element_type=F32)
    cnt = jnp.sum(_indicator(hot, F32), axis=1, keepdims=True)
    chunks = jnp.ceil(cnt * (1.0 / SEG_ALIGN))
    spare = jnp.zeros((LANES - rows, LANES), F32)
    e_dst = lax.broadcasted_iota(jnp.int32, (LANES, LANES), 0)
    e_src = lax.broadcasted_iota(jnp.int32, (LANES, LANES), 1)
    chunks_wide = jnp.concatenate([jnp.broadcast_to(chunks, (rows, LANES)), spare], axis=0).astype(BF16)
    seg_start = SEG_ALIGN * jnp.dot(_indicator(e_src < e_dst, BF16), chunks_wide,
                                    preferred_element_type=F32)[:rows, 0:1]
    local = seg_start + earlier
    slot0 = jnp.sum(jnp.where(row == row0, local, 0.0), axis=0, keepdims=True)
    slot1 = jnp.sum(jnp.where(row == row1, local, 0.0), axis=0, keepdims=True)
    seg_rows = jnp.concatenate([SEG_ALIGN * chunks, spare[:, 0:1]], axis=0)
    cnt_ref[0] = seg_rows
    before_ref[0] = carry_ref[...]
    carry_ref[...] += seg_rows

    out_row = lax.broadcasted_iota(jnp.int32, (ROUTE_COLS, tm), 0)
    packed = jnp.zeros((ROUTE_COLS, tm), F32)
    for c, v in enumerate([slot0, slot1, w0, w1]):
        packed = jnp.where(out_row == c, v, packed)
    slot_t_ref[0] = packed
    rt_ref[0] = jnp.concatenate([packed, jnp.zeros((LANES - ROUTE_COLS, tm), F32)], axis=0).T[:, :ROUTE_COLS]


CHUNK_ROWS = (8 * SEG_ALIGN, 4 * SEG_ALIGN, 2 * SEG_ALIGN, SEG_ALIGN)


def _plan_refs(refs):
    return tuple(tuple(refs[3 * k:3 * k + 3]) for k in range(len(CHUNK_ROWS)))


def _for_each_chunk(tile, plan, visit):
    for (n_ref, local_ref, global_ref), n_rows in zip(plan, CHUNK_ROWS):
        per_tile = local_ref.shape[0] // n_ref.shape[0]

        def one(c, carry, local_ref=local_ref, global_ref=global_ref, per_tile=per_tile, n_rows=n_rows):
            k = tile * per_tile + c
            visit(pl.multiple_of(local_ref[k], SEG_ALIGN), pl.multiple_of(global_ref[k], SEG_ALIGN), n_rows)
            return carry

        lax.fori_loop(0, n_ref[tile], one, 0)


def _copy_plan(cnt, seg_row0, local_rows):
    done = jnp.zeros_like(cnt)
    local0 = jnp.cumsum(cnt, axis=1) - cnt
    plan = ()
    for k, size in enumerate(CHUNK_ROWS):
        left = cnt - done
        n = left // size if k == 0 else (left // size) % 2
        max_entries = local_rows // size if k == 0 else N_EXPERTS
        end = jnp.cumsum(n, axis=1)
        start = (end - n)[:, None, :]
        c = jnp.arange(max_entries, dtype=jnp.int32)[None, :, None]
        inside = jnp.logical_and(c >= start, c < end[:, None, :])
        pick = lambda base: jnp.sum(jnp.where(inside, base[:, None, :] + size * (c - start), 0), axis=2).reshape(-1)
        plan += (end[:, -1], pick(local0 + done), pick(seg_row0 + done))
        done = done + n * size
    return plan


def _dispatch_kernel(*refs):
    plan = _plan_refs(refs)
    (tail_row_ref, tail_n_ref, n_used_ref, slot_t_ref, h_ref, xs_ref,
     loc_ref, zero_ref, sems, zero_sem) = refs[3 * len(CHUNK_ROWS):]
    tile = pl.program_id(0)
    n_loc, tm = loc_ref.shape[1], h_ref.shape[0]
    buf = tile & 1

    slot = lax.broadcasted_iota(jnp.int32, (n_loc, tm), 0).astype(F32)
    place = jnp.logical_or(slot == slot_t_ref[0, 0:1, :], slot == slot_t_ref[0, 1:2, :])
    loc_ref[buf] = _pack_rows(jnp.dot(_indicator(place, BF16), h_ref[...], preferred_element_type=F32))

    def chunk_copy(b):
        return lambda local, row, n: pltpu.make_async_copy(
            loc_ref.at[b, pl.ds(local, n), :], xs_ref.at[pl.ds(row, n), :], sems.at[b])

    _for_each_chunk(tile, plan, lambda l, r, n: chunk_copy(buf)(l, r, n).start())

    @pl.when(tile > 0)
    def _previous_done():
        _for_each_chunk(tile - 1, plan, lambda l, r, n: chunk_copy(1 - buf)(l, r, n).wait())

    @pl.when(tile == pl.num_programs(0) - 1)
    def _last_done():
        _for_each_chunk(tile, plan, lambda l, r, n: chunk_copy(buf)(l, r, n).wait())
        zero_ref[...] = jnp.zeros_like(zero_ref)
        block_rows = zero_ref.shape[0]

        def fill(action):
            def per_expert(e, carry):
                per_big = CHUNK_ROWS[0] // SEG_ALIGN
                n_big = tail_n_ref[e] // per_big
                first_small = tail_row_ref[e] + n_big * CHUNK_ROWS[0]

                def zero_rows(row0, n_rows):
                    def one(c, inner):
                        row = pl.multiple_of(row0 + c * n_rows, SEG_ALIGN)
                        action(pltpu.make_async_copy(zero_ref.at[pl.ds(0, n_rows), :],
                                                     xs_ref.at[pl.ds(row, n_rows), :], zero_sem))
                        return inner
                    return one

                lax.fori_loop(0, n_big, zero_rows(tail_row_ref[e], CHUNK_ROWS[0]), 0)
                lax.fori_loop(0, tail_n_ref[e] - n_big * per_big, zero_rows(first_small, SEG_ALIGN), 0)
                return carry

            lax.fori_loop(0, N_EXPERTS, per_expert, 0)

            def per_block(blk, carry):
                row = pl.multiple_of(blk * block_rows, block_rows)
                action(pltpu.make_async_copy(zero_ref, xs_ref.at[pl.ds(row, block_rows), :], zero_sem))
                return carry

            lax.fori_loop(n_used_ref[0], xs_ref.shape[0] // block_rows, per_block, 0)

        fill(lambda copy: copy.start())
        fill(lambda copy: copy.wait())


def _dispatch(plan, tail_row, tail_chunks, n_used, slot_t, h2, n_rows):
    n, d = h2.shape
    tm = TOKEN_TILE
    grid_spec = pltpu.PrefetchScalarGridSpec(
        num_scalar_prefetch=len(plan) + 3,
        grid=(n // tm,),
        in_specs=[
            pl.BlockSpec((1, ROUTE_COLS, tm), lambda i, *_: (i, 0, 0)),
            pl.BlockSpec((tm, d), lambda i, *_: (i, 0)),
        ],
        out_specs=pl.BlockSpec(memory_space=pl.ANY),
        scratch_shapes=[pltpu.VMEM((2, _local_rows(tm), d // 2), U32), pltpu.VMEM((ROW_BLOCK, d // 2), U32),
                        pltpu.SemaphoreType.DMA((2,)), pltpu.SemaphoreType.DMA(())],
    )
    return pl.pallas_call(
        _dispatch_kernel,
        grid_spec=grid_spec,
        out_shape=jax.ShapeDtypeStruct((n_rows, d // 2), U32),
        compiler_params=_params("arbitrary"),
        name="dispatch",
    )(*plan, tail_row, tail_chunks, n_used, slot_t, h2)


def _expert_kernel(blk_e_ref, n_used_ref, xs_ref, wg_ref, wu_ref, wd_ref, ys_ref, wg_bf, wu_bf, wd_bf):
    i = pl.program_id(0)
    used = i < n_used_ref[0]

    @pl.when(jnp.logical_and(used, jnp.logical_or(i == 0, blk_e_ref[i] != blk_e_ref[jnp.maximum(i - 1, 0)])))
    def _new_expert():
        wg_bf[...] = wg_ref[0].astype(BF16)
        wu_bf[...] = wu_ref[0].astype(BF16)
        wd_bf[...] = wd_ref[0].astype(BF16)

    @pl.when(used)
    def _compute():
        xb = _unpack_rows(xs_ref[...]).astype(BF16)
        g = jnp.dot(xb, wg_bf[...], preferred_element_type=F32)
        u = jnp.dot(xb, wu_bf[...], preferred_element_type=F32)
        hid = (g * jax.nn.sigmoid(g) * u).astype(BF16)
        ys_ref[...] = _pack_rows(jnp.dot(hid, wd_bf[...], preferred_element_type=F32))

    @pl.when(i >= n_used_ref[0])
    def _unused():
        ys_ref[...] = jnp.zeros_like(ys_ref)


def _experts(blk_e, n_used, xs, wg, wu, wd):
    n_rows, half = xs.shape
    _, d, de = wg.shape
    grid_spec = pltpu.PrefetchScalarGridSpec(
        num_scalar_prefetch=2,
        grid=(n_rows // ROW_BLOCK,),
        in_specs=[
            pl.BlockSpec((ROW_BLOCK, half), lambda i, e, u: (jnp.minimum(i, u[0] - 1), 0)),
            pl.BlockSpec((1, d, de), lambda i, e, u: (e[i], 0, 0)),
            pl.BlockSpec((1, d, de), lambda i, e, u: (e[i], 0, 0)),
            pl.BlockSpec((1, de, d), lambda i, e, u: (e[i], 0, 0)),
        ],
        out_specs=pl.BlockSpec((ROW_BLOCK, half), lambda i, e, u: (i, 0)),
        scratch_shapes=[pltpu.VMEM((d, de), BF16), pltpu.VMEM((d, de), BF16), pltpu.VMEM((de, d), BF16)],
    )
    return pl.pallas_call(
        _expert_kernel,
        grid_spec=grid_spec,
        out_shape=jax.ShapeDtypeStruct((n_rows, half), U32),
        compiler_params=_params("arbitrary"),
        name="experts",
    )(blk_e, n_used, xs, wg, wu, wd)


def _combine_kernel(*refs):
    plan = _plan_refs(refs)
    x1_ref, mod_ref, rt_ref, ys_ref, o_ref, loc_ref, sems = refs[3 * len(CHUNK_ROWS):]
    tm = x1_ref.shape[1]
    n_loc = loc_ref.shape[1]
    tile = pl.program_id(0) * pl.num_programs(1) + pl.program_id(1)
    n_tiles = pl.num_programs(0) * pl.num_programs(1)
    buf = tile & 1

    def chunk_copy(b):
        return lambda local, row, n: pltpu.make_async_copy(
            ys_ref.at[pl.ds(row, n), :], loc_ref.at[b, pl.ds(local, n), :], sems.at[b])

    @pl.when(tile == 0)
    def _init():
        loc_ref[...] = jnp.zeros_like(loc_ref)
        _for_each_chunk(tile, plan, lambda l, r, n: chunk_copy(buf)(l, r, n).start())

    @pl.when(tile + 1 < n_tiles)
    def _fetch_next():
        _for_each_chunk(tile + 1, plan, lambda l, r, n: chunk_copy(1 - buf)(l, r, n).start())

    _for_each_chunk(tile, plan, lambda l, r, n: chunk_copy(buf)(l, r, n).wait())

    y = _unpack_rows(loc_ref[buf]).astype(BF16)
    rt = rt_ref[0]
    slot = lax.broadcasted_iota(jnp.int32, (tm, n_loc), 1).astype(F32)
    mix = jnp.where(slot == rt[:, 0:1], rt[:, 2:3], jnp.where(slot == rt[:, 1:2], rt[:, 3:4], 0.0)).astype(BF16)
    o_ref[0] = x1_ref[0] + mod_ref[0, 5:6, :] * jnp.dot(mix, y, preferred_element_type=F32)


def _combine(plan, x1, mod, rt, ys):
    bsz, s, d = x1.shape
    tm = TOKEN_TILE
    grid_spec = pltpu.PrefetchScalarGridSpec(
        num_scalar_prefetch=len(plan),
        grid=(bsz, s // tm),
        in_specs=[
            pl.BlockSpec((1, tm, d), lambda b, i, *_: (b, i, 0)),
            pl.BlockSpec((1, N_MOD, d), lambda b, i, *_: (b, 0, 0)),
            pl.BlockSpec((1, tm, ROUTE_COLS), lambda b, i, *_: (b, i, 0)),
            pl.BlockSpec(memory_space=pl.ANY),
        ],
        out_specs=pl.BlockSpec((1, tm, d), lambda b, i, *_: (b, i, 0)),
        scratch_shapes=[pltpu.VMEM((2, _local_rows(tm), d // 2), U32), pltpu.SemaphoreType.DMA((2,))],
    )
    return pl.pallas_call(
        _combine_kernel,
        grid_spec=grid_spec,
        out_shape=jax.ShapeDtypeStruct((bsz, s, d), F32),
        compiler_params=_params("arbitrary", "arbitrary"),
        name="combine",
    )(*plan, x1, mod, rt, ys)


def _layer(x, mod, g_mix, w_in, g_q, g_k, rel_bias, w_br_moba, w_br_sb, w_out, g_ffn,
           w_rg, b_rg, w_re, b_re, w_gate, w_up, w_down):
    bsz, s, d = x.shape
    n = bsz * s
    n_blk = s // ATT_BLOCK

    proj, kaug, vaug, kmean = _inproj(x, mod, g_mix, w_in.astype(BF16), g_k)

    top = (n_blk + 1) * ATT_BLOCK - 1
    dist = np.maximum(top - np.arange((n_blk + 2) * ATT_BLOCK), 0)
    rel_rev = rel_bias[:, _rel_bucket_table(top)[dist]].reshape(N_HEADS // HEADS_PER_STEP, HEADS_PER_STEP, -1)

    o_a = _moba(proj, kaug, vaug, kmean, g_q, rel_rev)
    o_b = _stick_breaking(proj)

    pad = LANES - N_GROUPS - N_EXPERTS
    w_router = jnp.concatenate([w_rg, w_re, jnp.zeros((d, pad), F32)], axis=1)
    b_router = jnp.concatenate([b_rg, b_re, jnp.zeros((pad,), F32)]).reshape(LANES, 1)
    x1, h2, rt, slot_t, tile_cnt, tile_before = _merge(x, o_a, o_b, proj, mod, w_br_moba.astype(BF16),
                                                       w_br_sb.astype(BF16), w_out.astype(BF16), g_ffn,
                                                       w_router, b_router)

    experts = slice(N_GROUPS, N_GROUPS + N_EXPERTS)
    cnt = tile_cnt[:, experts, 0].astype(jnp.int32)
    total = jnp.sum(cnt, axis=0)
    padded = (total + ROW_BLOCK - 1) // ROW_BLOCK * ROW_BLOCK
    pend = jnp.cumsum(padded)
    pstart = pend - padded
    max_rows = 2 * n + cnt.shape[0] * N_EXPERTS * (SEG_ALIGN - 1) + N_EXPERTS * (ROW_BLOCK - 1)
    n_blocks = -(-max_rows // ROW_BLOCK)
    blk_row0 = jnp.arange(n_blocks, dtype=jnp.int32) * ROW_BLOCK
    blk_e = jnp.minimum(jnp.sum((blk_row0[:, None] >= pend[None, :]).astype(jnp.int32), axis=1), N_EXPERTS - 1)
    n_used = (pend[-1] // ROW_BLOCK).astype(jnp.int32).reshape(1)
    seg_row0 = pstart[None, :] + tile_before[:, experts, 0].astype(jnp.int32)
    plan = _copy_plan(cnt, seg_row0, _local_rows(TOKEN_TILE))

    xs = _dispatch(plan, pstart + total, (padded - total) // SEG_ALIGN, n_used, slot_t,
                   h2.reshape(n, d), n_blocks * ROW_BLOCK)
    ys = _experts(blk_e, n_used, xs, w_gate, w_up, w_down)
    return _combine(plan, x1, mod, rt, ys)


def kernel(x, c, w_ada, b_ada, g_mix, w_in, g_q, g_k, rel_bias, w_br_moba, w_br_sb, w_out, g_ffn,
           w_rg, b_rg, w_re, b_re, w_gate, w_up, w_down):
    bsz, s, d = x.shape
    assert s % TOKEN_TILE == 0 and s % MOBA_Q_TILE == 0 and s % (SB_Q_BLOCKS * ATT_BLOCK) == 0
    assert d % (2 * LANES) == 0
    for l in range(w_ada.shape[0]):
        mod = _adaln(c, w_ada[l], b_ada[l]).reshape(bsz, N_MOD, d)
        x = _layer(x, mod, g_mix[l], w_in[l], g_q[l], g_k[l], rel_bias, w_br_moba[l], w_br_sb[l], w_out[l],
                   g_ffn[l], w_rg[l], b_rg[l], w_re[l], b_re[l], w_gate[l], w_up[l], w_down[l])
    return x
```

```python
import functools
import math

import numpy as np
import jax
import jax.numpy as jnp
from jax import lax
from jax.experimental import pallas as pl
from jax.experimental.pallas import tpu as pltpu

F32 = jnp.float32
BF16 = jnp.bfloat16
U32 = jnp.uint32
HIGHEST = lax.Precision.HIGHEST

LANES = 128
VMEM_LIMIT_BYTES = 56 * 1024 * 1024

HEAD_DIM = 64
N_HEADS = 8
HEADS_PER_STEP = LANES // HEAD_DIM
ATT_BLOCK = 256
MOBA_Q_TILE = 2 * ATT_BLOCK
SB_Q_BLOCKS = 4
MOBA_TOPK = 3
REL_BUCKETS = 32
REL_MAX_DIST = 1024
N_GROUPS = 4
EXPERTS_PER_GROUP = 8
N_EXPERTS = N_GROUPS * EXPERTS_PER_GROUP
N_MOD = 6
RMS_EPS = 1e-6
ROW_BLOCK = 1024
TOKEN_TILE = 512
NEG_INF = float("-inf")
MASK_LOGIT = -1e30
SB_EXIT_SUM = 110.0

_NT = (((1,), (1,)), ((), ()))


def _params(*semantics):
    return pltpu.CompilerParams(dimension_semantics=semantics, vmem_limit_bytes=VMEM_LIMIT_BYTES)


def _adaln_kernel(c_ref, w_ref, b_ref, o_ref):
    c = c_ref[...]
    s = c * jax.nn.sigmoid(c)
    o_ref[...] = jnp.dot(s, w_ref[...], preferred_element_type=F32, precision=HIGHEST) + b_ref[...]


def _adaln(c, w, b):
    bsz, d = c.shape
    n = w.shape[1]
    tn = 1536
    return pl.pallas_call(
        _adaln_kernel,
        grid=(n // tn,),
        in_specs=[
            pl.BlockSpec((bsz, d), lambda j: (0, 0)),
            pl.BlockSpec((d, tn), lambda j: (0, j)),
            pl.BlockSpec((1, tn), lambda j: (0, j)),
        ],
        out_specs=pl.BlockSpec((bsz, tn), lambda j: (0, j)),
        out_shape=jax.ShapeDtypeStruct((bsz, n), F32),
        compiler_params=_params("arbitrary"),
        name="adaln",
    )(c, w, b.reshape(1, n))


def _rms_modulate(x, g, shift, scale):
    y = x * lax.rsqrt(jnp.mean(x * x, axis=-1, keepdims=True) + RMS_EPS) * g
    return y * (1.0 + scale) + shift


def _inproj_kernel(x_ref, mod_ref, g_ref, w_ref, gk_ref, o_ref, kaug_ref, vaug_ref, kmean_ref, *, col_chunk):
    i = pl.program_id(1)
    tm = x_ref.shape[1]
    h = _rms_modulate(x_ref[0], g_ref[...], mod_ref[0, 0:1, :], mod_ref[0, 1:2, :]).astype(BF16)

    w_moba = N_HEADS * HEAD_DIM
    blocks = tm // ATT_BLOCK
    first = _head_lane_mask()
    row_blk = blocks * i + lax.shift_right_logical(lax.broadcasted_iota(jnp.int32, (tm, LANES), 0),
                                                   ATT_BLOCK.bit_length() - 1)
    halves_of = (first, jnp.logical_not(first))
    tags = [jnp.where(_lane_index() - float(off) == row_blk.astype(F32), 1.0, 0.0) for off in (HEAD_DIM, 0)]
    mean_row = lax.broadcasted_iota(jnp.int32, kmean_ref.shape[2:], 0)

    @pl.when(i == 0)
    def _init():
        kmean_ref[...] = jnp.zeros_like(kmean_ref)

    for n in range(w_ref.shape[1] // col_chunk):
        cols = slice(n * col_chunk, (n + 1) * col_chunk)
        r = jnp.dot(h, w_ref[:, cols], preferred_element_type=F32)
        o_ref[0, :, cols] = r.astype(BF16)
        for c in range(col_chunk // LANES):
            col0 = n * col_chunk + c * LANES
            part = r[:, c * LANES:(c + 1) * LANES]
            if w_moba <= col0 < 2 * w_moba:
                kn = _head_rmsnorm(part, gk_ref[...])
                for hh, mine in enumerate(halves_of):
                    head = (col0 - w_moba) // HEAD_DIM + hh
                    kaug_ref[0, head] = jnp.where(mine, kn, tags[hh]).astype(BF16)
                    means = kmean_ref[0, head]
                    for jb in range(blocks):
                        mean = jnp.mean(kn[jb * ATT_BLOCK:(jb + 1) * ATT_BLOCK], axis=0, keepdims=True)
                        means = jnp.where(mean_row == blocks * i + jb, jnp.where(mine, mean, 0.0), means)
                    kmean_ref[0, head] = means
            elif 2 * w_moba <= col0 < 3 * w_moba:
                for hh, mine in enumerate(halves_of):
                    head = (col0 - 2 * w_moba) // HEAD_DIM + hh
                    vaug_ref[0, head] = jnp.where(mine, part, 1.0).astype(BF16)


def _inproj(x, mod, g, w_bf16, g_k):
    bsz, s, d = x.shape
    n = w_bf16.shape[1]
    tm = TOKEN_TILE
    n_blk = s // ATT_BLOCK
    per_head = lambda rows: pl.BlockSpec((1, N_HEADS, rows, LANES), lambda b, i: (b, 0, i, 0))
    return pl.pallas_call(
        functools.partial(_inproj_kernel, col_chunk=1024),
        grid=(bsz, s // tm),
        in_specs=[
            pl.BlockSpec((1, tm, d), lambda b, i: (b, i, 0)),
            pl.BlockSpec((1, N_MOD, d), lambda b, i: (b, 0, 0)),
            pl.BlockSpec((1, d), lambda b, i: (0, 0)),
            pl.BlockSpec((d, n), lambda b, i: (0, 0)),
            pl.BlockSpec((1, LANES), lambda b, i: (0, 0)),
        ],
        out_specs=[
            pl.BlockSpec((1, tm, n), lambda b, i: (b, i, 0)),
            per_head(tm),
            per_head(tm),
            pl.BlockSpec((1, N_HEADS, -(-n_blk // 8) * 8, LANES), lambda b, i: (b, 0, 0, 0)),
        ],
        out_shape=[
            jax.ShapeDtypeStruct((bsz, s, n), BF16),
            jax.ShapeDtypeStruct((bsz, N_HEADS, s, LANES), BF16),
            jax.ShapeDtypeStruct((bsz, N_HEADS, s, LANES), BF16),
            jax.ShapeDtypeStruct((bsz, N_HEADS, -(-n_blk // 8) * 8, LANES), F32),
        ],
        compiler_params=_params("arbitrary", "arbitrary"),
        name="inproj",
    )(x, mod, g.reshape(1, d), w_bf16, jnp.tile(g_k.reshape(1, HEAD_DIM), (1, HEADS_PER_STEP)))


def _indicator(cond, dtype):
    return jnp.where(cond, 1.0, 0.0).astype(dtype)


def _lane_index():
    return lax.broadcasted_iota(jnp.int32, (1, LANES), 1).astype(F32)


def _head_lane_mask():
    return lax.broadcasted_iota(jnp.int32, (1, LANES), 1) < HEAD_DIM


def _head_rmsnorm(t, g):
    first = _head_lane_mask()
    sq = t * t
    ss0 = jnp.sum(jnp.where(first, sq, 0.0), axis=-1, keepdims=True)
    ss1 = jnp.sum(jnp.where(first, 0.0, sq), axis=-1, keepdims=True)
    inv = jnp.where(first, lax.rsqrt(ss0 / HEAD_DIM + RMS_EPS), lax.rsqrt(ss1 / HEAD_DIM + RMS_EPS))
    return t * inv * g


def _split_heads(t):
    first = _head_lane_mask()
    zero = jnp.zeros_like(t)
    return jnp.where(first, t, zero), jnp.where(first, zero, t)


def _rel_bucket_table(max_dist):
    n = np.arange(max_dist + 1)
    max_exact = REL_BUCKETS // 2
    nf = np.maximum(n, 1).astype(np.float64)
    large = max_exact + (np.log(nf / max_exact) / math.log(REL_MAX_DIST / max_exact)
                         * (REL_BUCKETS - max_exact)).astype(np.int64)
    large = np.minimum(large, REL_BUCKETS - 1)
    return np.where(n < max_exact, n, large).astype(np.int32)


def _moba_kernel(q_ref, kaug_ref, vaug_ref, kmean_ref, gq_ref, rb_ref, o_ref,
                 toep_ref, s_ref, mvec_ref, acc_ref, *, n_blk):
    b = pl.program_id(1)
    qi = pl.program_id(2)
    blk = ATT_BLOCK
    qt = q_ref.shape[1]
    halves = qt // blk
    first = _head_lane_mask()

    @pl.when(jnp.logical_and(b == 0, qi == 0))
    def _build_bias_tiles():
        for h in range(HEADS_PER_STEP):
            for d in range(n_blk):
                r = rb_ref[0, h:h + 1, (n_blk - d) * blk:(n_blk - d + 2) * blk]
                rolled = pltpu.roll(jnp.broadcast_to(r, (blk, 2 * blk)), blk + 1, 1, stride=1, stride_axis=0)
                toep_ref[h, d] = rolled[:, :blk]

    qn = _head_rmsnorm(q_ref[0].astype(F32), gq_ref[...])
    gate_rows = kmean_ref.shape[2]
    blk_row = lax.broadcasted_iota(jnp.int32, (gate_rows, qt), 0).astype(F32)
    own = (halves * qi + lax.shift_right_logical(lax.broadcasted_iota(jnp.int32, (gate_rows, qt), 1),
                                                 blk.bit_length() - 1)).astype(F32)
    q_aug = []
    for h, mine in enumerate((first, jnp.logical_not(first))):
        qh = jnp.where(mine, qn, 0.0)
        km = kmean_ref[0, h]
        km_hi, q_hi = km.astype(BF16), qh.astype(BF16)
        km_lo, q_lo = (km - km_hi.astype(F32)).astype(BF16), (qh - q_hi.astype(F32)).astype(BF16)
        g = (lax.dot_general(km_hi, q_hi, _NT, preferred_element_type=F32)
             + lax.dot_general(km_lo, q_hi, _NT, preferred_element_type=F32)
             + lax.dot_general(km_hi, q_lo, _NT, preferred_element_type=F32))
        g = jnp.where(blk_row < own, g, NEG_INF)
        keep = jnp.where(blk_row == own, 1.0, 0.0)
        for _ in range(MOBA_TOPK):
            top = jnp.max(g, axis=0, keepdims=True)
            is_top = jnp.logical_and(g == top, top > NEG_INF)
            idx = jnp.min(jnp.where(is_top, blk_row, float(gate_rows)), axis=0, keepdims=True)
            pick = blk_row == idx
            keep = jnp.where(pick, 1.0, keep)
            g = jnp.where(pick, NEG_INF, g)
        mask_t = jnp.where(keep > 0.0, 0.0, MASK_LOGIT)
        before = (HEAD_DIM, 0)[h]
        pieces = [jnp.zeros((before, qt), F32)] if before else []
        pieces += [mask_t, jnp.zeros((LANES - before - gate_rows, qt), F32)]
        mask_logit = jnp.concatenate(pieces, axis=0).T
        q_aug.append(jnp.where(mine, qh * (HEAD_DIM ** -0.5), mask_logit).astype(BF16))

    def lane_max(s):
        return jnp.maximum(s[:, :LANES], s[:, LANES:])

    def scores(h, t, dist):
        k_t = kaug_ref[0, h, pl.ds(pl.multiple_of(t * qt, qt), qt), :]
        s = lax.dot_general(q_aug[h], k_t, _NT, preferred_element_type=F32)
        tiles = []
        for kb in range(halves):
            rows = []
            for qb in range(halves):
                part = s[qb * blk:(qb + 1) * blk, kb * blk:(kb + 1) * blk]
                if isinstance(dist, int) and halves * dist + qb - kb < 0:
                    rows.append(part)
                else:
                    rows.append(part + toep_ref[h, halves * dist + qb - kb])
            tiles.append(jnp.concatenate(rows, axis=0))
        return tiles

    row = lax.broadcasted_iota(jnp.int32, (qt, blk), 0)
    col = lax.broadcasted_iota(jnp.int32, (qt, blk), 1)
    for h in range(HEADS_PER_STEP):
        top = None
        for kb, s in enumerate(scores(h, qi, 0)):
            s = jnp.where(col + kb * blk <= row, s, NEG_INF)
            s_ref[h, halves * qi + kb] = s
            top = lane_max(s) if top is None else jnp.maximum(top, lane_max(s))
        mvec_ref[h] = top

    def in_pairs(count, visit):
        def quad(p, carry):
            visit(4 * p, 4)
            return carry

        lax.fori_loop(0, lax.shift_right_logical(count, 2), quad, 0)

        @pl.when((count & 2) == 2)
        def _pair():
            visit(count & ~3, 2)

        @pl.when((count & 1) == 1)
        def _last():
            visit(count - 1, 1)

    def pass1(t0, n_tiles):
        for h in range(HEADS_PER_STEP):
            top = mvec_ref[h]
            for t in [t0 + u for u in range(n_tiles)]:
                for kb, s in enumerate(scores(h, t, qi - t)):
                    s_ref[h, halves * t + kb] = s
                    top = jnp.maximum(top, lane_max(s))
            mvec_ref[h] = top

    in_pairs(qi, pass1)

    for h in range(HEADS_PER_STEP):
        m = jnp.max(mvec_ref[h], axis=-1, keepdims=True)
        mvec_ref[h] = jnp.broadcast_to(m, (qt, LANES))
        acc_ref[h] = jnp.zeros((qt, LANES), F32)

    def pass2(t0, n_tiles):
        keys = pl.ds(pl.multiple_of(t0 * qt, qt), n_tiles * qt)
        for h in range(HEADS_PER_STEP):
            m = mvec_ref[h]
            parts = []
            for kb in range(n_tiles * halves):
                s = s_ref[h, halves * t0 + kb]
                parts += [jnp.exp(s[:, :LANES] - m), jnp.exp(s[:, LANES:] - m)]
            p = jnp.concatenate(parts, axis=1).astype(BF16)
            acc_ref[h] += jnp.dot(p, vaug_ref[0, h, keys, :], preferred_element_type=F32)

    in_pairs(qi + 1, pass2)

    out = [acc_ref[h] / pltpu.roll(acc_ref[h], HEAD_DIM, 1) for h in range(HEADS_PER_STEP)]
    o_ref[0] = jnp.where(first, out[0], out[1]).astype(o_ref.dtype)


def _moba(proj, kaug, vaug, kmean, g_q, rel_rev):
    bsz, s, _ = proj.shape
    blk = ATT_BLOCK
    n_blk = s // blk
    n_hp = N_HEADS // HEADS_PER_STEP
    w = N_HEADS * HEAD_DIM
    qt = MOBA_Q_TILE
    per_pair = lambda arr: pl.BlockSpec((1, HEADS_PER_STEP) + arr.shape[2:], lambda hp, b, i: (b, hp, 0, 0))
    return pl.pallas_call(
        functools.partial(_moba_kernel, n_blk=n_blk),
        grid=(n_hp, bsz, s // qt),
        in_specs=[
            pl.BlockSpec((1, qt, LANES), lambda hp, b, i: (b, i, hp)),
            per_pair(kaug), per_pair(vaug), per_pair(kmean),
            pl.BlockSpec((1, LANES), lambda hp, b, i: (0, 0)),
            pl.BlockSpec((1, HEADS_PER_STEP, rel_rev.shape[-1]), lambda hp, b, i: (hp, 0, 0)),
        ],
        out_specs=pl.BlockSpec((1, qt, LANES), lambda hp, b, i: (b, i, hp)),
        out_shape=jax.ShapeDtypeStruct((bsz, s, w), BF16),
        scratch_shapes=[
            pltpu.VMEM((HEADS_PER_STEP, n_blk, blk, blk), F32),
            pltpu.VMEM((HEADS_PER_STEP, n_blk, qt, blk), F32),
            pltpu.VMEM((HEADS_PER_STEP, qt, LANES), F32),
            pltpu.VMEM((HEADS_PER_STEP, qt, LANES), F32),
        ],
        compiler_params=_params("arbitrary", "arbitrary", "arbitrary"),
        name="moba",
    )(proj, kaug, vaug, kmean, jnp.tile(g_q.reshape(1, HEAD_DIM), (1, HEADS_PER_STEP)), rel_rev)


def _sb_kernel(q_ref, k_ref, v_ref, o_ref, acc_ref, carry_ref):
    ti = pl.program_id(2)
    blk = ATT_BLOCK
    n_sub = q_ref.shape[1] // blk
    first = n_sub * ti
    row = lax.broadcasted_iota(jnp.int32, (blk, blk), 0)
    col = lax.broadcasted_iota(jnp.int32, (blk, blk), 1)
    past = col < row
    later = _indicator(row > col, BF16)

    q_heads = []
    for sub in range(n_sub):
        heads = _split_heads(q_ref[0, sub * blk:(sub + 1) * blk, :].astype(F32) * (HEAD_DIM ** -0.5))
        q_heads.append([t.astype(BF16) for t in heads])

    def walk(jobs, fresh):
        for sub, blocks in jobs:
            for h in range(HEADS_PER_STEP):
                carry = None if fresh else carry_ref[sub, h]
                total = None if fresh else acc_ref[sub, h]
                for j, diagonal in blocks:
                    start = j * blk if isinstance(j, int) else pl.multiple_of(j * blk, blk)
                    z = lax.dot_general(q_heads[sub][h], k_ref[0, pl.ds(start, blk), :], _NT,
                                        preferred_element_type=F32)
                    sp = jnp.maximum(z, 0.0) + jnp.log(1.0 + jnp.exp(-jnp.abs(z)))
                    masked = jnp.where(past, sp, 0.0) if diagonal else sp
                    after = jnp.dot(masked.astype(BF16), later, preferred_element_type=F32)
                    row_sum = after[:, 0:1] + masked[:, 0:1]
                    a = jnp.exp(z - sp - after)
                    if diagonal:
                        a = jnp.where(past, a, 0.0)
                    pv = jnp.dot(a.astype(BF16), v_ref[0, pl.ds(start, blk), :], preferred_element_type=F32)
                    if carry is not None:
                        pv = pv * jnp.exp(-carry)
                    total = pv if total is None else total + pv
                    carry = row_sum if carry is None else carry + row_sum
                carry_ref[sub, h] = carry
                acc_ref[sub, h] = total

    @pl.when(ti == 0)
    def _first_step():
        walk([(sub, [(sub, True)] + ([(sub - 1, False)] if sub else [])) for sub in range(n_sub)], True)

    @pl.when(ti > 0)
    def _own_and_previous():
        walk([(sub, [(first + sub, True), (first + sub - 1, False)]) for sub in range(n_sub)], True)

    def slowest_decay(subs):
        worst = functools.reduce(jnp.minimum, [carry_ref[sub, h] for sub in subs for h in range(HEADS_PER_STEP)])
        return jnp.min(worst)

    @pl.when(slowest_decay(range(n_sub)) < SB_EXIT_SUM)
    def _walk_further_back():
        for sub in range(n_sub):
            def cond(state):
                j, worst = state
                return jnp.logical_and(j >= 0, worst < SB_EXIT_SUM)

            def body(state, sub=sub):
                j, _ = state
                walk([(sub, [(j, False)])], False)
                return j - 1, slowest_decay([sub])

            lax.while_loop(cond, body, (first + sub - 2, slowest_decay([sub])))

    for sub in range(n_sub):
        o_ref[0, sub * blk:(sub + 1) * blk, :] = jnp.where(
            _head_lane_mask(), acc_ref[sub, 0], acc_ref[sub, 1]).astype(o_ref.dtype)


def _stick_breaking(proj):
    bsz, s, _ = proj.shape
    blk = ATT_BLOCK
    qt = SB_Q_BLOCKS * blk
    n_hp = N_HEADS // HEADS_PER_STEP
    w = N_HEADS * HEAD_DIM
    base = 3 * n_hp
    return pl.pallas_call(
        _sb_kernel,
        grid=(n_hp, bsz, s // qt),
        in_specs=[
            pl.BlockSpec((1, qt, LANES), lambda hp, b, i: (b, i, base + hp)),
            pl.BlockSpec((1, s, LANES), lambda hp, b, i: (b, 0, base + n_hp + hp)),
            pl.BlockSpec((1, s, LANES), lambda hp, b, i: (b, 0, base + 2 * n_hp + hp)),
        ],
        out_specs=pl.BlockSpec((1, qt, LANES), lambda hp, b, i: (b, i, hp)),
        out_shape=jax.ShapeDtypeStruct((bsz, s, w), BF16),
        scratch_shapes=[
            pltpu.VMEM((SB_Q_BLOCKS, HEADS_PER_STEP, blk, LANES), F32),
            pltpu.VMEM((SB_Q_BLOCKS, HEADS_PER_STEP, blk, 1), F32),
        ],
        compiler_params=_params("arbitrary", "arbitrary", "arbitrary"),
        name="stickbrk",
    )(proj, proj, proj)


def _pack_rows(t):
    n = t.shape[1] // 2
    bits = pltpu.bitcast(t.astype(BF16).astype(F32), U32)
    return bits[:, :n] | (bits[:, n:] >> 16)


def _unpack_rows(p):
    hi = pltpu.bitcast(p & jnp.uint32(0xFFFF0000), F32)
    lo = pltpu.bitcast(p << 16, F32)
    return jnp.concatenate([hi, lo], axis=1)


def _merge_kernel(x_ref, oa_ref, ob_ref, ga_ref, gb_ref, mod_ref, wbm_ref, wbs_ref, wo_ref, gf_ref,
                  wrh_ref, wrl_ref, br_ref, x1_ref, h2_ref, rt_ref, slot_t_ref, cnt_ref, before_ref, carry_ref):
    ma = jnp.dot(oa_ref[0], wbm_ref[...], preferred_element_type=F32)
    mb = jnp.dot(ob_ref[0], wbs_ref[...], preferred_element_type=F32)
    merged = (jax.nn.sigmoid(ga_ref[0].astype(F32)) * ma + jax.nn.sigmoid(gb_ref[0].astype(F32)) * mb)
    t = jnp.dot(merged.astype(BF16), wo_ref[...], preferred_element_type=F32)
    x1 = x_ref[0] + mod_ref[0, 2:3, :] * t
    x1_ref[0] = x1
    h2 = _rms_modulate(x1, gf_ref[...], mod_ref[0, 3:4, :], mod_ref[0, 4:5, :])
    hi = h2.astype(BF16)
    h2_ref[0] = hi
    lo = (h2 - hi.astype(F32)).astype(BF16)
    logits = (lax.dot_general(wrh_ref[...], hi, _NT, preferred_element_type=F32)
              + lax.dot_general(wrh_ref[...], lo, _NT, preferred_element_type=F32)
              + lax.dot_general(wrl_ref[...], hi, _NT, preferred_element_type=F32) + br_ref[...])
    first_tile = jnp.logical_and(pl.program_id(0) == 0, pl.program_id(1) == 0)
    _route_tile(logits, first_tile, rt_ref, slot_t_ref, cnt_ref, before_ref, carry_ref)


def _merge(x, o_a, o_b, proj, mod, wbm, wbs, wo, g_ffn, w_router, b_router):
    bsz, s, d = x.shape
    w = o_a.shape[-1]
    tm = TOKEN_TILE
    gate_blk = (proj.shape[-1] - 2 * d) // d
    const = lambda shape: pl.BlockSpec(shape, lambda b, i: (0,) * len(shape))
    tok = lambda width: pl.BlockSpec((1, tm, width), lambda b, i: (b, i, 0))
    wr_hi = w_router.T.astype(BF16)
    wr_lo = (w_router.T - wr_hi.astype(F32)).astype(BF16)
    per_b = s // tm
    tiles = bsz * per_b
    per_tile = lambda rows, cols: pl.BlockSpec((1, rows, cols), lambda b, i: (b * per_b + i, 0, 0))
    return pl.pallas_call(
        _merge_kernel,
        grid=(bsz, s // tm),
        in_specs=[
            tok(d), tok(w), tok(w),
            pl.BlockSpec((1, tm, d), lambda b, i: (b, i, gate_blk)),
            pl.BlockSpec((1, tm, d), lambda b, i: (b, i, gate_blk + 1)),
            pl.BlockSpec((1, N_MOD, d), lambda b, i: (b, 0, 0)),
            const((w, d)), const((w, d)), const((d, d)), const((1, d)),
            const((LANES, d)), const((LANES, d)), const((LANES, 1)),
        ],
        out_specs=[tok(d), tok(d), tok(ROUTE_COLS), per_tile(ROUTE_COLS, tm), per_tile(LANES, 1), per_tile(LANES, 1)],
        out_shape=[
            jax.ShapeDtypeStruct((bsz, s, d), F32),
            jax.ShapeDtypeStruct((bsz, s, d), BF16),
            jax.ShapeDtypeStruct((bsz, s, ROUTE_COLS), F32),
            jax.ShapeDtypeStruct((tiles, ROUTE_COLS, tm), F32),
            jax.ShapeDtypeStruct((tiles, LANES, 1), F32),
            jax.ShapeDtypeStruct((tiles, LANES, 1), F32),
        ],
        scratch_shapes=[pltpu.VMEM((LANES, 1), F32)],
        compiler_params=_params("arbitrary", "arbitrary"),
        name="merge",
    )(x, o_a, o_b, proj, proj, mod, wbm, wbs, wo, g_ffn.reshape(1, d), wr_hi, wr_lo, b_router)


ROUTE_COLS = 8
SEG_ALIGN = 8


def _local_rows(tm):
    return 2 * tm + N_EXPERTS * SEG_ALIGN


def _route_tile(lg_t, first_tile, rt_ref, slot_t_ref, cnt_ref, before_ref, carry_ref):
    tm = lg_t.shape[1]
    rows = -(-(N_GROUPS + N_EXPERTS) // 8) * 8
    lg = lg_t[:rows]
    row = lax.broadcasted_iota(jnp.int32, (rows, tm), 0).astype(F32)

    @pl.when(first_tile)
    def _init():
        carry_ref[...] = jnp.zeros_like(carry_ref)

    def first_max(vals):
        top = jnp.max(vals, axis=0, keepdims=True)
        idx = jnp.min(jnp.where(vals == top, row, float(rows)), axis=0, keepdims=True)
        return top, idx

    g_logit = jnp.where(row < N_GROUPS, lg, NEG_INF)
    g_top, g_idx = first_max(g_logit)
    g_w = 1.0 / jnp.sum(jnp.exp(g_logit - g_top), axis=0, keepdims=True)
    lo = N_GROUPS + g_idx * EXPERTS_PER_GROUP
    e_logit = jnp.where(jnp.logical_and(row >= lo, row < lo + EXPERTS_PER_GROUP), lg, NEG_INF)
    e_top0, row0 = first_max(e_logit)
    e_top1, row1 = first_max(jnp.where(row == row0, NEG_INF, e_logit))
    z = jnp.exp(e_top1 - e_top0)
    w0 = g_w / (1.0 + z)
    w1 = g_w * z / (1.0 + z)

    hot = jnp.logical_or(row == row0, row == row1)
    t_src = lax.broadcasted_iota(jnp.int32, (tm, tm), 0)
    t_dst = lax.broadcasted_iota(jnp.int32, (tm, tm), 1)
    earlier = jnp.dot(_indicator(hot, BF16), _indicator(t_src < t_dst, BF16), preferred_element_type=F32)
    cnt = jnp.sum(_indicator(hot, F32), axis=1, keepdims=True)
    chunks = jnp.ceil(cnt * (1.0 / SEG_ALIGN))
    spare = jnp.zeros((LANES - rows, LANES), F32)
    e_dst = lax.broadcasted_iota(jnp.int32, (LANES, LANES), 0)
    e_src = lax.broadcasted_iota(jnp.int32, (LANES, LANES), 1)
    chunks_wide = jnp.concatenate([jnp.broadcast_to(chunks, (rows, LANES)), spare], axis=0).astype(BF16)
    seg_start = SEG_ALIGN * jnp.dot(_indicator(e_src < e_dst, BF16), chunks_wide,
                                    preferred_element_type=F32)[:rows, 0:1]
    local = seg_start + earlier
    slot0 = jnp.sum(jnp.where(row == row0, local, 0.0), axis=0, keepdims=True)
    slot1 = jnp.sum(jnp.where(row == row1, local, 0.0), axis=0, keepdims=True)
    seg_rows = jnp.concatenate([SEG_ALIGN * chunks, spare[:, 0:1]], axis=0)
    cnt_ref[0] = seg_rows
    before_ref[0] = carry_ref[...]
    carry_ref[...] += seg_rows

    out_row = lax.broadcasted_iota(jnp.int32, (ROUTE_COLS, tm), 0)
    packed = jnp.zeros((ROUTE_COLS, tm), F32)
    for c, v in enumerate([slot0, slot1, w0, w1]):
        packed = jnp.where(out_row == c, v, packed)
    slot_t_ref[0] = packed
    rt_ref[0] = jnp.concatenate([packed, jnp.zeros((LANES - ROUTE_COLS, tm), F32)], axis=0).T[:, :ROUTE_COLS]


CHUNK_ROWS = (8 * SEG_ALIGN, 4 * SEG_ALIGN, 2 * SEG_ALIGN, SEG_ALIGN)


def _plan_refs(refs):
    return tuple(tuple(refs[3 * k:3 * k + 3]) for k in range(len(CHUNK_ROWS)))


def _for_each_chunk(tile, plan, visit):
    for (n_ref, local_ref, global_ref), n_rows in zip(plan, CHUNK_ROWS):
        per_tile = local_ref.shape[0] // n_ref.shape[0]

        def one(c, carry, local_ref=local_ref, global_ref=global_ref, per_tile=per_tile, n_rows=n_rows):
            k = tile * per_tile + c
            visit(pl.multiple_of(local_ref[k], SEG_ALIGN), pl.multiple_of(global_ref[k], SEG_ALIGN), n_rows)
            return carry

        lax.fori_loop(0, n_ref[tile], one, 0)


def _copy_plan(cnt, seg_row0, local_rows):
    done = jnp.zeros_like(cnt)
    local0 = jnp.cumsum(cnt, axis=1) - cnt
    plan = ()
    for k, size in enumerate(CHUNK_ROWS):
        left = cnt - done
        n = left // size if k == 0 else (left // size) % 2
        max_entries = local_rows // size if k == 0 else N_EXPERTS
        end = jnp.cumsum(n, axis=1)
        start = (end - n)[:, None, :]
        c = jnp.arange(max_entries, dtype=jnp.int32)[None, :, None]
        inside = jnp.logical_and(c >= start, c < end[:, None, :])
        pick = lambda base: jnp.sum(jnp.where(inside, base[:, None, :] + size * (c - start), 0), axis=2).reshape(-1)
        plan += (end[:, -1], pick(local0 + done), pick(seg_row0 + done))
        done = done + n * size
    return plan


def _dispatch_kernel(*refs):
    plan = _plan_refs(refs)
    (tail_row_ref, tail_n_ref, n_used_ref, slot_t_ref, h_ref, xs_ref,
     loc_ref, zero_ref, sems, zero_sem) = refs[3 * len(CHUNK_ROWS):]
    tile = pl.program_id(0)
    n_loc, tm = loc_ref.shape[1], h_ref.shape[0]
    buf = tile & 1

    slot = lax.broadcasted_iota(jnp.int32, (n_loc, tm), 0).astype(F32)
    place = jnp.logical_or(slot == slot_t_ref[0, 0:1, :], slot == slot_t_ref[0, 1:2, :])
    loc_ref[buf] = _pack_rows(jnp.dot(_indicator(place, BF16), h_ref[...], preferred_element_type=F32))

    def chunk_copy(b):
        return lambda local, row, n: pltpu.make_async_copy(
            loc_ref.at[b, pl.ds(local, n), :], xs_ref.at[pl.ds(row, n), :], sems.at[b])

    _for_each_chunk(tile, plan, lambda l, r, n: chunk_copy(buf)(l, r, n).start())

    @pl.when(tile > 0)
    def _previous_done():
        _for_each_chunk(tile - 1, plan, lambda l, r, n: chunk_copy(1 - buf)(l, r, n).wait())

    @pl.when(tile == pl.num_programs(0) - 1)
    def _last_done():
        _for_each_chunk(tile, plan, lambda l, r, n: chunk_copy(buf)(l, r, n).wait())
        zero_ref[...] = jnp.zeros_like(zero_ref)
        block_rows = zero_ref.shape[0]

        def fill(action):
            def per_expert(e, carry):
                per_big = CHUNK_ROWS[0] // SEG_ALIGN
                n_big = tail_n_ref[e] // per_big
                first_small = tail_row_ref[e] + n_big * CHUNK_ROWS[0]

                def zero_rows(row0, n_rows):
                    def one(c, inner):
                        row = pl.multiple_of(row0 + c * n_rows, SEG_ALIGN)
                        action(pltpu.make_async_copy(zero_ref.at[pl.ds(0, n_rows), :],
                                                     xs_ref.at[pl.ds(row, n_rows), :], zero_sem))
                        return inner
                    return one

                lax.fori_loop(0, n_big, zero_rows(tail_row_ref[e], CHUNK_ROWS[0]), 0)
                lax.fori_loop(0, tail_n_ref[e] - n_big * per_big, zero_rows(first_small, SEG_ALIGN), 0)
                return carry

            lax.fori_loop(0, N_EXPERTS, per_expert, 0)

            def per_block(blk, carry):
                row = pl.multiple_of(blk * block_rows, block_rows)
                action(pltpu.make_async_copy(zero_ref, xs_ref.at[pl.ds(row, block_rows), :], zero_sem))
                return carry

            lax.fori_loop(n_used_ref[0], xs_ref.shape[0] // block_rows, per_block, 0)

        fill(lambda copy: copy.start())
        fill(lambda copy: copy.wait())


def _dispatch(plan, tail_row, tail_chunks, n_used, slot_t, h2, n_rows):
    n, d = h2.shape
    tm = TOKEN_TILE
    grid_spec = pltpu.PrefetchScalarGridSpec(
        num_scalar_prefetch=len(plan) + 3,
        grid=(n // tm,),
        in_specs=[
            pl.BlockSpec((1, ROUTE_COLS, tm), lambda i, *_: (i, 0, 0)),
            pl.BlockSpec((tm, d), lambda i, *_: (i, 0)),
        ],
        out_specs=pl.BlockSpec(memory_space=pl.ANY),
        scratch_shapes=[pltpu.VMEM((2, _local_rows(tm), d // 2), U32), pltpu.VMEM((ROW_BLOCK, d // 2), U32),
                        pltpu.SemaphoreType.DMA((2,)), pltpu.SemaphoreType.DMA(())],
    )
    return pl.pallas_call(
        _dispatch_kernel,
        grid_spec=grid_spec,
        out_shape=jax.ShapeDtypeStruct((n_rows, d // 2), U32),
        compiler_params=_params("arbitrary"),
        name="dispatch",
    )(*plan, tail_row, tail_chunks, n_used, slot_t, h2)


def _expert_kernel(blk_e_ref, n_used_ref, xs_ref, wg_ref, wu_ref, wd_ref, ys_ref, wg_bf, wu_bf, wd_bf):
    i = pl.program_id(0)
    used = i < n_used_ref[0]

    @pl.when(jnp.logical_and(used, jnp.logical_or(i == 0, blk_e_ref[i] != blk_e_ref[jnp.maximum(i - 1, 0)])))
    def _new_expert():
        wg_bf[...] = wg_ref[0].astype(BF16)
        wu_bf[...] = wu_ref[0].astype(BF16)
        wd_bf[...] = wd_ref[0].astype(BF16)

    @pl.when(used)
    def _compute():
        xb = _unpack_rows(xs_ref[...]).astype(BF16)
        g = jnp.dot(xb, wg_bf[...], preferred_element_type=F32)
        u = jnp.dot(xb, wu_bf[...], preferred_element_type=F32)
        hid = (g * jax.nn.sigmoid(g) * u).astype(BF16)
        ys_ref[...] = _pack_rows(jnp.dot(hid, wd_bf[...], preferred_element_type=F32))

    @pl.when(i >= n_used_ref[0])
    def _unused():
        ys_ref[...] = jnp.zeros_like(ys_ref)


def _experts(blk_e, n_used, xs, wg, wu, wd):
    n_rows, half = xs.shape
    _, d, de = wg.shape
    grid_spec = pltpu.PrefetchScalarGridSpec(
        num_scalar_prefetch=2,
        grid=(n_rows // ROW_BLOCK,),
        in_specs=[
            pl.BlockSpec((ROW_BLOCK, half), lambda i, e, u: (jnp.minimum(i, u[0] - 1), 0)),
            pl.BlockSpec((1, d, de), lambda i, e, u: (e[i], 0, 0)),
            pl.BlockSpec((1, d, de), lambda i, e, u: (e[i], 0, 0)),
            pl.BlockSpec((1, de, d), lambda i, e, u: (e[i], 0, 0)),
        ],
        out_specs=pl.BlockSpec((ROW_BLOCK, half), lambda i, e, u: (i, 0)),
        scratch_shapes=[pltpu.VMEM((d, de), BF16), pltpu.VMEM((d, de), BF16), pltpu.VMEM((de, d), BF16)],
    )
    return pl.pallas_call(
        _expert_kernel,
        grid_spec=grid_spec,
        out_shape=jax.ShapeDtypeStruct((n_rows, half), U32),
        compiler_params=_params("arbitrary"),
        name="experts",
    )(blk_e, n_used, xs, wg, wu, wd)


def _combine_kernel(*refs):
    plan = _plan_refs(refs)
    x1_ref, mod_ref, rt_ref, ys_ref, o_ref, loc_ref, sems = refs[3 * len(CHUNK_ROWS):]
    tm = x1_ref.shape[1]
    n_loc = loc_ref.shape[1]
    tile = pl.program_id(0) * pl.num_programs(1) + pl.program_id(1)
    n_tiles = pl.num_programs(0) * pl.num_programs(1)
    buf = tile & 1

    def chunk_copy(b):
        return lambda local, row, n: pltpu.make_async_copy(
            ys_ref.at[pl.ds(row, n), :], loc_ref.at[b, pl.ds(local, n), :], sems.at[b])

    @pl.when(tile == 0)
    def _init():
        loc_ref[...] = jnp.zeros_like(loc_ref)
        _for_each_chunk(tile, plan, lambda l, r, n: chunk_copy(buf)(l, r, n).start())

    @pl.when(tile + 1 < n_tiles)
    def _fetch_next():
        _for_each_chunk(tile + 1, plan, lambda l, r, n: chunk_copy(1 - buf)(l, r, n).start())

    _for_each_chunk(tile, plan, lambda l, r, n: chunk_copy(buf)(l, r, n).wait())

    y = _unpack_rows(loc_ref[buf]).astype(BF16)
    rt = rt_ref[0]
    slot = lax.broadcasted_iota(jnp.int32, (tm, n_loc), 1).astype(F32)
    mix = jnp.where(slot == rt[:, 0:1], rt[:, 2:3], jnp.where(slot == rt[:, 1:2], rt[:, 3:4], 0.0)).astype(BF16)
    o_ref[0] = x1_ref[0] + mod_ref[0, 5:6, :] * jnp.dot(mix, y, preferred_element_type=F32)


def _combine(plan, x1, mod, rt, ys):
    bsz, s, d = x1.shape
    tm = TOKEN_TILE
    grid_spec = pltpu.PrefetchScalarGridSpec(
        num_scalar_prefetch=len(plan),
        grid=(bsz, s // tm),
        in_specs=[
            pl.BlockSpec((1, tm, d), lambda b, i, *_: (b, i, 0)),
            pl.BlockSpec((1, N_MOD, d), lambda b, i, *_: (b, 0, 0)),
            pl.BlockSpec((1, tm, ROUTE_COLS), lambda b, i, *_: (b, i, 0)),
            pl.BlockSpec(memory_space=pl.ANY),
        ],
        out_specs=pl.BlockSpec((1, tm, d), lambda b, i, *_: (b, i, 0)),
        scratch_shapes=[pltpu.VMEM((2, _local_rows(tm), d // 2), U32), pltpu.SemaphoreType.DMA((2,))],
    )
    return pl.pallas_call(
        _combine_kernel,
        grid_spec=grid_spec,
        out_shape=jax.ShapeDtypeStruct((bsz, s, d), F32),
        compiler_params=_params("arbitrary", "arbitrary"),
        name="combine",
    )(*plan, x1, mod, rt, ys)


def _layer(x, mod, g_mix, w_in, g_q, g_k, rel_bias, w_br_moba, w_br_sb, w_out, g_ffn,
           w_rg, b_rg, w_re, b_re, w_gate, w_up, w_down):
    bsz, s, d = x.shape
    n = bsz * s
    n_blk = s // ATT_BLOCK

    proj, kaug, vaug, kmean = _inproj(x, mod, g_mix, w_in.astype(BF16), g_k)

    top = (n_blk + 1) * ATT_BLOCK - 1
    dist = np.maximum(top - np.arange((n_blk + 2) * ATT_BLOCK), 0)
    rel_rev = rel_bias[:, _rel_bucket_table(top)[dist]].reshape(N_HEADS // HEADS_PER_STEP, HEADS_PER_STEP, -1)

    o_a = _moba(proj, kaug, vaug, kmean, g_q, rel_rev)
    o_b = _stick_breaking(proj)

    pad = LANES - N_GROUPS - N_EXPERTS
    w_router = jnp.concatenate([w_rg, w_re, jnp.zeros((d, pad), F32)], axis=1)
    b_router = jnp.concatenate([b_rg, b_re, jnp.zeros((pad,), F32)]).reshape(LANES, 1)
    x1, h2, rt, slot_t, tile_cnt, tile_before = _merge(x, o_a, o_b, proj, mod, w_br_moba.astype(BF16),
                                                       w_br_sb.astype(BF16), w_out.astype(BF16), g_ffn,
                                                       w_router, b_router)

    experts = slice(N_GROUPS, N_GROUPS + N_EXPERTS)
    cnt = tile_cnt[:, experts, 0].astype(jnp.int32)
    total = jnp.sum(cnt, axis=0)
    padded = (total + ROW_BLOCK - 1) // ROW_BLOCK * ROW_BLOCK
    pend = jnp.cumsum(padded)
    pstart = pend - padded
    max_rows = 2 * n + cnt.shape[0] * N_EXPERTS * (SEG_ALIGN - 1) + N_EXPERTS * (ROW_BLOCK - 1)
    n_blocks = -(-max_rows // ROW_BLOCK)
    blk_row0 = jnp.arange(n_blocks, dtype=jnp.int32) * ROW_BLOCK
    blk_e = jnp.minimum(jnp.sum((blk_row0[:, None] >= pend[None, :]).astype(jnp.int32), axis=1), N_EXPERTS - 1)
    n_used = (pend[-1] // ROW_BLOCK).astype(jnp.int32).reshape(1)
    seg_row0 = pstart[None, :] + tile_before[:, experts, 0].astype(jnp.int32)
    plan = _copy_plan(cnt, seg_row0, _local_rows(TOKEN_TILE))

    xs = _dispatch(plan, pstart + total, (padded - total) // SEG_ALIGN, n_used, slot_t,
                   h2.reshape(n, d), n_blocks * ROW_BLOCK)
    ys = _experts(blk_e, n_used, xs, w_gate, w_up, w_down)
    return _combine(plan, x1, mod, rt, ys)


def kernel(x, c, w_ada, b_ada, g_mix, w_in, g_q, g_k, rel_bias, w_br_moba, w_br_sb, w_out, g_ffn,
           w_rg, b_rg, w_re, b_re, w_gate, w_up, w_down):
    bsz, s, d = x.shape
    assert s % TOKEN_TILE == 0 and s % MOBA_Q_TILE == 0 and s % (SB_Q_BLOCKS * ATT_BLOCK) == 0
    assert d % (2 * LANES) == 0
    for l in range(w_ada.shape[0]):
        mod = _adaln(c, w_ada[l], b_ada[l]).reshape(bsz, N_MOD, d)
        x = _layer(x, mod, g_mix[l], w_in[l], g_q[l], g_k[l], rel_bias, w_br_moba[l], w_br_sb[l], w_out[l],
                   g_ffn[l], w_rg[l], b_rg[l], w_re[l], b_re[l], w_gate[l], w_up[l], w_down[l])
    return x
```

```python
import functools
import math

import numpy as np
import jax
import jax.numpy as jnp
from jax import lax
from jax.experimental import pallas as pl
from jax.experimental.pallas import tpu as pltpu

F32 = jnp.float32
BF16 = jnp.bfloat16
U32 = jnp.uint32
HIGHEST = lax.Precision.HIGHEST

LANES = 128
VMEM_LIMIT_BYTES = 56 * 1024 * 1024

HEAD_DIM = 64
N_HEADS = 8
HEADS_PER_STEP = LANES // HEAD_DIM
ATT_BLOCK = 256
MOBA_Q_TILE = 2 * ATT_BLOCK
SB_Q_BLOCKS = 4
MOBA_TOPK = 3
REL_BUCKETS = 32
REL_MAX_DIST = 1024
N_GROUPS = 4
EXPERTS_PER_GROUP = 8
N_EXPERTS = N_GROUPS * EXPERTS_PER_GROUP
N_MOD = 6
RMS_EPS = 1e-6
ROW_BLOCK = 1024
TOKEN_TILE = 512
NEG_INF = float("-inf")
MASK_LOGIT = -1e30
SB_EXIT_SUM = 110.0

_NT = (((1,), (1,)), ((), ()))


def _params(*semantics):
    return pltpu.CompilerParams(dimension_semantics=semantics, vmem_limit_bytes=VMEM_LIMIT_BYTES)


def _adaln_kernel(c_ref, w_ref, b_ref, o_ref):
    c = c_ref[...]
    s = c * jax.nn.sigmoid(c)
    o_ref[...] = jnp.dot(s, w_ref[...], preferred_element_type=F32, precision=HIGHEST) + b_ref[...]


def _adaln(c, w, b):
    bsz, d = c.shape
    n = w.shape[1]
    tn = 1536
    return pl.pallas_call(
        _adaln_kernel,
        grid=(n // tn,),
        in_specs=[
            pl.BlockSpec((bsz, d), lambda j: (0, 0)),
            pl.BlockSpec((d, tn), lambda j: (0, j)),
            pl.BlockSpec((1, tn), lambda j: (0, j)),
        ],
        out_specs=pl.BlockSpec((bsz, tn), lambda j: (0, j)),
        out_shape=jax.ShapeDtypeStruct((bsz, n), F32),
        compiler_params=_params("arbitrary"),
        name="adaln",
    )(c, w, b.reshape(1, n))


def _rms_modulate(x, g, shift, scale):
    y = x * lax.rsqrt(jnp.mean(x * x, axis=-1, keepdims=True) + RMS_EPS) * g
    return y * (1.0 + scale) + shift


def _inproj_kernel(x_ref, mod_ref, g_ref, w_ref, gk_ref, o_ref, kaug_ref, vaug_ref, kmean_ref, *, col_chunk):
    i = pl.program_id(1)
    tm = x_ref.shape[1]
    h = _rms_modulate(x_ref[0], g_ref[...], mod_ref[0, 0:1, :], mod_ref[0, 1:2, :]).astype(BF16)

    w_moba = N_HEADS * HEAD_DIM
    blocks = tm // ATT_BLOCK
    first = _head_lane_mask()
    row_blk = blocks * i + lax.shift_right_logical(lax.broadcasted_iota(jnp.int32, (tm, LANES), 0),
                                                   ATT_BLOCK.bit_length() - 1)
    halves_of = (first, jnp.logical_not(first))
    tags = [jnp.where(_lane_index() - float(off) == row_blk.astype(F32), 1.0, 0.0) for off in (HEAD_DIM, 0)]
    mean_row = lax.broadcasted_iota(jnp.int32, kmean_ref.shape[2:], 0)

    @pl.when(i == 0)
    def _init():
        kmean_ref[...] = jnp.zeros_like(kmean_ref)

    for n in range(w_ref.shape[1] // col_chunk):
        cols = slice(n * col_chunk, (n + 1) * col_chunk)
        r = jnp.dot(h, w_ref[:, cols], preferred_element_type=F32)
        o_ref[0, :, cols] = r.astype(BF16)
        for c in range(col_chunk // LANES):
            col0 = n * col_chunk + c * LANES
            part = r[:, c * LANES:(c + 1) * LANES]
            if w_moba <= col0 < 2 * w_moba:
                kn = _head_rmsnorm(part, gk_ref[...])
                for hh, mine in enumerate(halves_of):
                    head = (col0 - w_moba) // HEAD_DIM + hh
                    kaug_ref[0, head] = jnp.where(mine, kn, tags[hh]).astype(BF16)
                    means = kmean_ref[0, head]
                    for jb in range(blocks):
                        mean = jnp.mean(kn[jb * ATT_BLOCK:(jb + 1) * ATT_BLOCK], axis=0, keepdims=True)
                        means = jnp.where(mean_row == blocks * i + jb, jnp.where(mine, mean, 0.0), means)
                    kmean_ref[0, head] = means
            elif 2 * w_moba <= col0 < 3 * w_moba:
                for hh, mine in enumerate(halves_of):
                    head = (col0 - 2 * w_moba) // HEAD_DIM + hh
                    vaug_ref[0, head] = jnp.where(mine, part, 1.0).astype(BF16)


def _inproj(x, mod, g, w_bf16, g_k):
    bsz, s, d = x.shape
    n = w_bf16.shape[1]
    tm = TOKEN_TILE
    n_blk = s // ATT_BLOCK
    per_head = lambda rows: pl.BlockSpec((1, N_HEADS, rows, LANES), lambda b, i: (b, 0, i, 0))
    return pl.pallas_call(
        functools.partial(_inproj_kernel, col_chunk=1024),
        grid=(bsz, s // tm),
        in_specs=[
            pl.BlockSpec((1, tm, d), lambda b, i: (b, i, 0)),
            pl.BlockSpec((1, N_MOD, d), lambda b, i: (b, 0, 0)),
            pl.BlockSpec((1, d), lambda b, i: (0, 0)),
            pl.BlockSpec((d, n), lambda b, i: (0, 0)),
            pl.BlockSpec((1, LANES), lambda b, i: (0, 0)),
        ],
        out_specs=[
            pl.BlockSpec((1, tm, n), lambda b, i: (b, i, 0)),
            per_head(tm),
            per_head(tm),
            pl.BlockSpec((1, N_HEADS, -(-n_blk // 8) * 8, LANES), lambda b, i: (b, 0, 0, 0)),
        ],
        out_shape=[
            jax.ShapeDtypeStruct((bsz, s, n), BF16),
            jax.ShapeDtypeStruct((bsz, N_HEADS, s, LANES), BF16),
            jax.ShapeDtypeStruct((bsz, N_HEADS, s, LANES), BF16),
            jax.ShapeDtypeStruct((bsz, N_HEADS, -(-n_blk // 8) * 8, LANES), F32),
        ],
        compiler_params=_params("arbitrary", "arbitrary"),
        name="inproj",
    )(x, mod, g.reshape(1, d), w_bf16, jnp.tile(g_k.reshape(1, HEAD_DIM), (1, HEADS_PER_STEP)))


def _indicator(cond, dtype):
    return jnp.where(cond, 1.0, 0.0).astype(dtype)


def _lane_index():
    return lax.broadcasted_iota(jnp.int32, (1, LANES), 1).astype(F32)


def _head_lane_mask():
    return lax.broadcasted_iota(jnp.int32, (1, LANES), 1) < HEAD_DIM


def _head_rmsnorm(t, g):
    first = _head_lane_mask()
    sq = t * t
    ss0 = jnp.sum(jnp.where(first, sq, 0.0), axis=-1, keepdims=True)
    ss1 = jnp.sum(jnp.where(first, 0.0, sq), axis=-1, keepdims=True)
    inv = jnp.where(first, lax.rsqrt(ss0 / HEAD_DIM + RMS_EPS), lax.rsqrt(ss1 / HEAD_DIM + RMS_EPS))
    return t * inv * g


def _split_heads(t):
    first = _head_lane_mask()
    zero = jnp.zeros_like(t)
    return jnp.where(first, t, zero), jnp.where(first, zero, t)


def _rel_bucket_table(max_dist):
    n = np.arange(max_dist + 1)
    max_exact = REL_BUCKETS // 2
    nf = np.maximum(n, 1).astype(np.float64)
    large = max_exact + (np.log(nf / max_exact) / math.log(REL_MAX_DIST / max_exact)
                         * (REL_BUCKETS - max_exact)).astype(np.int64)
    large = np.minimum(large, REL_BUCKETS - 1)
    return np.where(n < max_exact, n, large).astype(np.int32)


def _moba_kernel(q_ref, kaug_ref, vaug_ref, kmean_ref, gq_ref, rb_ref, o_ref,
                 toep_ref, s_ref, mvec_ref, acc_ref, *, n_blk):
    b = pl.program_id(1)
    qi = pl.program_id(2)
    blk = ATT_BLOCK
    qt = q_ref.shape[1]
    halves = qt // blk
    first = _head_lane_mask()

    @pl.when(jnp.logical_and(b == 0, qi == 0))
    def _build_bias_tiles():
        for h in range(HEADS_PER_STEP):
            for d in range(n_blk):
                r = rb_ref[0, h:h + 1, (n_blk - d) * blk:(n_blk - d + 2) * blk]
                rolled = pltpu.roll(jnp.broadcast_to(r, (blk, 2 * blk)), blk + 1, 1, stride=1, stride_axis=0)
                toep_ref[h, d] = rolled[:, :blk]

    qn = _head_rmsnorm(q_ref[0].astype(F32), gq_ref[...])
    gate_rows = kmean_ref.shape[2]
    blk_row = lax.broadcasted_iota(jnp.int32, (gate_rows, qt), 0).astype(F32)
    own = (halves * qi + lax.shift_right_logical(lax.broadcasted_iota(jnp.int32, (gate_rows, qt), 1),
                                                 blk.bit_length() - 1)).astype(F32)
    q_aug = []
    for h, mine in enumerate((first, jnp.logical_not(first))):
        qh = jnp.where(mine, qn, 0.0)
        km = kmean_ref[0, h]
        km_hi, q_hi = km.astype(BF16), qh.astype(BF16)
        km_lo, q_lo = (km - km_hi.astype(F32)).astype(BF16), (qh - q_hi.astype(F32)).astype(BF16)
        g = (lax.dot_general(km_hi, q_hi, _NT, preferred_element_type=F32)
             + lax.dot_general(km_lo, q_hi, _NT, preferred_element_type=F32)
             + lax.dot_general(km_hi, q_lo, _NT, preferred_element_type=F32))
        g = jnp.where(blk_row < own, g, NEG_INF)
        keep = jnp.where(blk_row == own, 1.0, 0.0)
        for _ in range(MOBA_TOPK):
            top = jnp.max(g, axis=0, keepdims=True)
            is_top = jnp.logical_and(g == top, top > NEG_INF)
            idx = jnp.min(jnp.where(is_top, blk_row, float(gate_rows)), axis=0, keepdims=True)
            pick = blk_row == idx
            keep = jnp.where(pick, 1.0, keep)
            g = jnp.where(pick, NEG_INF, g)
        mask_t = jnp.where(keep > 0.0, 0.0, MASK_LOGIT)
        before = (HEAD_DIM, 0)[h]
        pieces = [jnp.zeros((before, qt), F32)] if before else []
        pieces += [mask_t, jnp.zeros((LANES - before - gate_rows, qt), F32)]
        mask_logit = jnp.concatenate(pieces, axis=0).T
        q_aug.append(jnp.where(mine, qh * (HEAD_DIM ** -0.5), mask_logit).astype(BF16))

    def lane_max(s):
        return jnp.maximum(s[:, :LANES], s[:, LANES:])

    def scores(h, t, dist):
        k_t = kaug_ref[0, h, pl.ds(pl.multiple_of(t * qt, qt), qt), :]
        s = lax.dot_general(q_aug[h], k_t, _NT, preferred_element_type=F32)
        tiles = []
        for kb in range(halves):
            rows = []
            for qb in range(halves):
                part = s[qb * blk:(qb + 1) * blk, kb * blk:(kb + 1) * blk]
                if isinstance(dist, int) and halves * dist + qb - kb < 0:
                    rows.append(part)
                else:
                    rows.append(part + toep_ref[h, halves * dist + qb - kb])
            tiles.append(jnp.concatenate(rows, axis=0))
        return tiles

    row = lax.broadcasted_iota(jnp.int32, (qt, blk), 0)
    col = lax.broadcasted_iota(jnp.int32, (qt, blk), 1)
    for h in range(HEADS_PER_STEP):
        top = None
        for kb, s in enumerate(scores(h, qi, 0)):
            s = jnp.where(col + kb * blk <= row, s, NEG_INF)
            s_ref[h, halves * qi + kb] = s
            top = lane_max(s) if top is None else jnp.maximum(top, lane_max(s))
        mvec_ref[h] = top

    def in_pairs(count, visit):
        def quad(p, carry):
            visit(4 * p, 4)
            return carry

        lax.fori_loop(0, lax.shift_right_logical(count, 2), quad, 0)

        @pl.when((count & 2) == 2)
        def _pair():
            visit(count & ~3, 2)

        @pl.when((count & 1) == 1)
        def _last():
            visit(count - 1, 1)

    def pass1(t0, n_tiles):
        for h in range(HEADS_PER_STEP):
            top = mvec_ref[h]
            for t in [t0 + u for u in range(n_tiles)]:
                for kb, s in enumerate(scores(h, t, qi - t)):
                    s_ref[h, halves * t + kb] = s
                    top = jnp.maximum(top, lane_max(s))
            mvec_ref[h] = top

    in_pairs(qi, pass1)

    for h in range(HEADS_PER_STEP):
        m = jnp.max(mvec_ref[h], axis=-1, keepdims=True)
        mvec_ref[h] = jnp.broadcast_to(m, (qt, LANES))
        acc_ref[h] = jnp.zeros((qt, LANES), F32)

    def pass2(t0, n_tiles):
        keys = pl.ds(pl.multiple_of(t0 * qt, qt), n_tiles * qt)
        for h in range(HEADS_PER_STEP):
            m = mvec_ref[h]
            parts = []
            for kb in range(n_tiles * halves):
                s = s_ref[h, halves * t0 + kb]
                parts += [jnp.exp(s[:, :LANES] - m), jnp.exp(s[:, LANES:] - m)]
            p = jnp.concatenate(parts, axis=1).astype(BF16)
            acc_ref[h] += jnp.dot(p, vaug_ref[0, h, keys, :], preferred_element_type=F32)

    in_pairs(qi + 1, pass2)

    out = [acc_ref[h] / pltpu.roll(acc_ref[h], HEAD_DIM, 1) for h in range(HEADS_PER_STEP)]
    o_ref[0] = jnp.where(first, out[0], out[1]).astype(o_ref.dtype)


def _moba(proj, kaug, vaug, kmean, g_q, rel_rev):
    bsz, s, _ = proj.shape
    blk = ATT_BLOCK
    n_blk = s // blk
    n_hp = N_HEADS // HEADS_PER_STEP
    w = N_HEADS * HEAD_DIM
    qt = MOBA_Q_TILE
    per_pair = lambda arr: pl.BlockSpec((1, HEADS_PER_STEP) + arr.shape[2:], lambda hp, b, i: (b, hp, 0, 0))
    return pl.pallas_call(
        functools.partial(_moba_kernel, n_blk=n_blk),
        grid=(n_hp, bsz, s // qt),
        in_specs=[
            pl.BlockSpec((1, qt, LANES), lambda hp, b, i: (b, i, hp)),
            per_pair(kaug), per_pair(vaug), per_pair(kmean),
            pl.BlockSpec((1, LANES), lambda hp, b, i: (0, 0)),
            pl.BlockSpec((1, HEADS_PER_STEP, rel_rev.shape[-1]), lambda hp, b, i: (hp, 0, 0)),
        ],
        out_specs=pl.BlockSpec((1, qt, LANES), lambda hp, b, i: (b, i, hp)),
        out_shape=jax.ShapeDtypeStruct((bsz, s, w), BF16),
        scratch_shapes=[
            pltpu.VMEM((HEADS_PER_STEP, n_blk, blk, blk), F32),
            pltpu.VMEM((HEADS_PER_STEP, n_blk, qt, blk), F32),
            pltpu.VMEM((HEADS_PER_STEP, qt, LANES), F32),
            pltpu.VMEM((HEADS_PER_STEP, qt, LANES), F32),
        ],
        compiler_params=_params("arbitrary", "arbitrary", "arbitrary"),
        name="moba",
    )(proj, kaug, vaug, kmean, jnp.tile(g_q.reshape(1, HEAD_DIM), (1, HEADS_PER_STEP)), rel_rev)


def _sb_kernel(q_ref, k_ref, v_ref, o_ref, acc_ref, carry_ref):
    ti = pl.program_id(2)
    blk = ATT_BLOCK
    n_sub = q_ref.shape[1] // blk
    first = n_sub * ti
    row = lax.broadcasted_iota(jnp.int32, (blk, blk), 0)
    col = lax.broadcasted_iota(jnp.int32, (blk, blk), 1)
    past = col < row
    later = _indicator(row > col, BF16)

    q_heads = []
    for sub in range(n_sub):
        heads = _split_heads(q_ref[0, sub * blk:(sub + 1) * blk, :].astype(F32) * (HEAD_DIM ** -0.5))
        q_heads.append([t.astype(BF16) for t in heads])

    def walk(jobs, fresh):
        for sub, blocks in jobs:
            for h in range(HEADS_PER_STEP):
                carry = None if fresh else carry_ref[sub, h]
                total = None if fresh else acc_ref[sub, h]
                for j, diagonal in blocks:
                    start = j * blk if isinstance(j, int) else pl.multiple_of(j * blk, blk)
                    z = lax.dot_general(q_heads[sub][h], k_ref[0, pl.ds(start, blk), :], _NT,
                                        preferred_element_type=F32)
                    sp = jnp.maximum(z, 0.0) + jnp.log(1.0 + jnp.exp(-jnp.abs(z)))
                    masked = jnp.where(past, sp, 0.0) if diagonal else sp
                    after = jnp.dot(masked.astype(BF16), later, preferred_element_type=F32)
                    row_sum = after[:, 0:1] + masked[:, 0:1]
                    a = jnp.exp(z - sp - after)
                    if diagonal:
                        a = jnp.where(past, a, 0.0)
                    pv = jnp.dot(a.astype(BF16), v_ref[0, pl.ds(start, blk), :], preferred_element_type=F32)
                    if carry is not None:
                        pv = pv * jnp.exp(-carry)
                    total = pv if total is None else total + pv
                    carry = row_sum if carry is None else carry + row_sum
                carry_ref[sub, h] = carry
                acc_ref[sub, h] = total

    @pl.when(ti == 0)
    def _first_step():
        walk([(sub, [(sub, True)] + ([(sub - 1, False)] if sub else [])) for sub in range(n_sub)], True)

    @pl.when(ti > 0)
    def _own_and_previous():
        walk([(sub, [(first + sub, True), (first + sub - 1, False)]) for sub in range(n_sub)], True)

    def slowest_decay(subs):
        worst = functools.reduce(jnp.minimum, [carry_ref[sub, h] for sub in subs for h in range(HEADS_PER_STEP)])
        return jnp.min(worst)

    @pl.when(slowest_decay(range(n_sub)) < SB_EXIT_SUM)
    def _walk_further_back():
        for sub in range(n_sub):
            def cond(state):
                j, worst = state
                return jnp.logical_and(j >= 0, worst < SB_EXIT_SUM)

            def body(state, sub=sub):
                j, _ = state
                walk([(sub, [(j, False)])], False)
                return j - 1, slowest_decay([sub])

            lax.while_loop(cond, body, (first + sub - 2, slowest_decay([sub])))

    for sub in range(n_sub):
        o_ref[0, sub * blk:(sub + 1) * blk, :] = jnp.where(
            _head_lane_mask(), acc_ref[sub, 0], acc_ref[sub, 1]).astype(o_ref.dtype)


def _stick_breaking(proj):
    bsz, s, _ = proj.shape
    blk = ATT_BLOCK
    qt = SB_Q_BLOCKS * blk
    n_hp = N_HEADS // HEADS_PER_STEP
    w = N_HEADS * HEAD_DIM
    base = 3 * n_hp
    return pl.pallas_call(
        _sb_kernel,
        grid=(n_hp, bsz, s // qt),
        in_specs=[
            pl.BlockSpec((1, qt, LANES), lambda hp, b, i: (b, i, base + hp)),
            pl.BlockSpec((1, s, LANES), lambda hp, b, i: (b, 0, base + n_hp + hp)),
            pl.BlockSpec((1, s, LANES), lambda hp, b, i: (b, 0, base + 2 * n_hp + hp)),
        ],
        out_specs=pl.BlockSpec((1, qt, LANES), lambda hp, b, i: (b, i, hp)),
        out_shape=jax.ShapeDtypeStruct((bsz, s, w), BF16),
        scratch_shapes=[
            pltpu.VMEM((SB_Q_BLOCKS, HEADS_PER_STEP, blk, LANES), F32),
            pltpu.VMEM((SB_Q_BLOCKS, HEADS_PER_STEP, blk, 1), F32),
        ],
        compiler_params=_params("arbitrary", "arbitrary", "arbitrary"),
        name="stickbrk",
    )(proj, proj, proj)


def _pack_rows(t):
    n = t.shape[1] // 2
    bits = pltpu.bitcast(t.astype(BF16).astype(F32), U32)
    return bits[:, :n] | (bits[:, n:] >> 16)


def _unpack_rows(p):
    hi = pltpu.bitcast(p & jnp.uint32(0xFFFF0000), F32)
    lo = pltpu.bitcast(p << 16, F32)
    return jnp.concatenate([hi, lo], axis=1)


def _merge_kernel(x_ref, oa_ref, ob_ref, ga_ref, gb_ref, mod_ref, wbm_ref, wbs_ref, wo_ref, gf_ref,
                  wrh_ref, wrl_ref, br_ref, x1_ref, h2_ref, rt_ref, slot_t_ref, cnt_ref, before_ref, carry_ref):
    ma = jnp.dot(oa_ref[0], wbm_ref[...], preferred_element_type=F32)
    mb = jnp.dot(ob_ref[0], wbs_ref[...], preferred_element_type=F32)
    merged = (jax.nn.sigmoid(ga_ref[0].astype(F32)) * ma + jax.nn.sigmoid(gb_ref[0].astype(F32)) * mb)
    t = jnp.dot(merged.astype(BF16), wo_ref[...], preferred_element_type=F32)
    x1 = x_ref[0] + mod_ref[0, 2:3, :] * t
    x1_ref[0] = x1
    h2 = _rms_modulate(x1, gf_ref[...], mod_ref[0, 3:4, :], mod_ref[0, 4:5, :])
    hi = h2.astype(BF16)
    h2_ref[0] = hi
    lo = (h2 - hi.astype(F32)).astype(BF16)
    logits = (lax.dot_general(wrh_ref[...], hi, _NT, preferred_element_type=F32)
              + lax.dot_general(wrh_ref[...], lo, _NT, preferred_element_type=F32)
              + lax.dot_general(wrl_ref[...], hi, _NT, preferred_element_type=F32) + br_ref[...])
    first_tile = jnp.logical_and(pl.program_id(0) == 0, pl.program_id(1) == 0)
    _route_tile(logits, first_tile, rt_ref, slot_t_ref, cnt_ref, before_ref, carry_ref)


def _merge(x, o_a, o_b, proj, mod, wbm, wbs, wo, g_ffn, w_router, b_router):
    bsz, s, d = x.shape
    w = o_a.shape[-1]
    tm = TOKEN_TILE
    gate_blk = (proj.shape[-1] - 2 * d) // d
    const = lambda shape: pl.BlockSpec(shape, lambda b, i: (0,) * len(shape))
    tok = lambda width: pl.BlockSpec((1, tm, width), lambda b, i: (b, i, 0))
    wr_hi = w_router.T.astype(BF16)
    wr_lo = (w_router.T - wr_hi.astype(F32)).astype(BF16)
    per_b = s // tm
    tiles = bsz * per_b
    per_tile = lambda rows, cols: pl.BlockSpec((1, rows, cols), lambda b, i: (b * per_b + i, 0, 0))
    return pl.pallas_call(
        _merge_kernel,
        grid=(bsz, s // tm),
        in_specs=[
            tok(d), tok(w), tok(w),
            pl.BlockSpec((1, tm, d), lambda b, i: (b, i, gate_blk)),
            pl.BlockSpec((1, tm, d), lambda b, i: (b, i, gate_blk + 1)),
            pl.BlockSpec((1, N_MOD, d), lambda b, i: (b, 0, 0)),
            const((w, d)), const((w, d)), const((d, d)), const((1, d)),
            const((LANES, d)), const((LANES, d)), const((LANES, 1)),
        ],
        out_specs=[tok(d), tok(d), tok(ROUTE_COLS), per_tile(ROUTE_COLS, tm), per_tile(LANES, 1), per_tile(LANES, 1)],
        out_shape=[
            jax.ShapeDtypeStruct((bsz, s, d), F32),
            jax.ShapeDtypeStruct((bsz, s, d), BF16),
            jax.ShapeDtypeStruct((bsz, s, ROUTE_COLS), F32),
            jax.ShapeDtypeStruct((tiles, ROUTE_COLS, tm), F32),
            jax.ShapeDtypeStruct((tiles, LANES, 1), F32),
            jax.ShapeDtypeStruct((tiles, LANES, 1), F32),
        ],
        scratch_shapes=[pltpu.VMEM((LANES, 1), F32)],
        compiler_params=_params("arbitrary", "arbitrary"),
        name="merge",
    )(x, o_a, o_b, proj, proj, mod, wbm, wbs, wo, g_ffn.reshape(1, d), wr_hi, wr_lo, b_router)


ROUTE_COLS = 8
SEG_ALIGN = 8


def _local_rows(tm):
    return 2 * tm + N_EXPERTS * SEG_ALIGN


def _route_tile(lg_t, first_tile, rt_ref, slot_t_ref, cnt_ref, before_ref, carry_ref):
    tm = lg_t.shape[1]
    rows = -(-(N_GROUPS + N_EXPERTS) // 8) * 8
    lg = lg_t[:rows]
    row = lax.broadcasted_iota(jnp.int32, (rows, tm), 0).astype(F32)

    @pl.when(first_tile)
    def _init():
        carry_ref[...] = jnp.zeros_like(carry_ref)

    def first_max(vals):
        top = jnp.max(vals, axis=0, keepdims=True)
        idx = jnp.min(jnp.where(vals == top, row, float(rows)), axis=0, keepdims=True)
        return top, idx

    g_logit = jnp.where(row < N_GROUPS, lg, NEG_INF)
    g_top, g_idx = first_max(g_logit)
    g_w = 1.0 / jnp.sum(jnp.exp(g_logit - g_top), axis=0, keepdims=True)
    lo = N_GROUPS + g_idx * EXPERTS_PER_GROUP
    e_logit = jnp.where(jnp.logical_and(row >= lo, row < lo + EXPERTS_PER_GROUP), lg, NEG_INF)
    e_top0, row0 = first_max(e_logit)
    e_top1, row1 = first_max(jnp.where(row == row0, NEG_INF, e_logit))
    z = jnp.exp(e_top1 - e_top0)
    w0 = g_w / (1.0 + z)
    w1 = g_w * z / (1.0 + z)

    hot = jnp.logical_or(row == row0, row == row1)
    t_src = lax.broadcasted_iota(jnp.int32, (tm, tm), 0)
    t_dst = lax.broadcasted_iota(jnp.int32, (tm, tm), 1)
    earlier = jnp.dot(_indicator(hot, BF16), _indicator(t_src < t_dst, BF16), preferred_element_type=F32)
    cnt = jnp.sum(_indicator(hot, F32), axis=1, keepdims=True)
    chunks = jnp.ceil(cnt * (1.0 / SEG_ALIGN))
    spare = jnp.zeros((LANES - rows, LANES), F32)
    e_dst = lax.broadcasted_iota(jnp.int32, (LANES, LANES), 0)
    e_src = lax.broadcasted_iota(jnp.int32, (LANES, LANES), 1)
    chunks_wide = jnp.concatenate([jnp.broadcast_to(chunks, (rows, LANES)), spare], axis=0).astype(BF16)
    seg_start = SEG_ALIGN * jnp.dot(_indicator(e_src < e_dst, BF16), chunks_wide,
                                    preferred_element_type=F32)[:rows, 0:1]
    local = seg_start + earlier
    slot0 = jnp.sum(jnp.where(row == row0, local, 0.0), axis=0, keepdims=True)
    slot1 = jnp.sum(jnp.where(row == row1, local, 0.0), axis=0, keepdims=True)
    seg_rows = jnp.concatenate([SEG_ALIGN * chunks, spare[:, 0:1]], axis=0)
    cnt_ref[0] = seg_rows
    before_ref[0] = carry_ref[...]
    carry_ref[...] += seg_rows

    out_row = lax.broadcasted_iota(jnp.int32, (ROUTE_COLS, tm), 0)
    packed = jnp.zeros((ROUTE_COLS, tm), F32)
    for c, v in enumerate([slot0, slot1, w0, w1]):
        packed = jnp.where(out_row == c, v, packed)
    slot_t_ref[0] = packed
    rt_ref[0] = jnp.concatenate([packed, jnp.zeros((LANES - ROUTE_COLS, tm), F32)], axis=0).T[:, :ROUTE_COLS]


CHUNK_ROWS = (8 * SEG_ALIGN, 4 * SEG_ALIGN, 2 * SEG_ALIGN, SEG_ALIGN)


def _copy_queue(n_rows):
    return CHUNK_ROWS.index(n_rows) % 2


def _plan_refs(refs):
    return tuple(tuple(refs[3 * k:3 * k + 3]) for k in range(len(CHUNK_ROWS)))


def _for_each_chunk(tile, plan, visit):
    for (n_ref, local_ref, global_ref), n_rows in zip(plan, CHUNK_ROWS):
        per_tile = local_ref.shape[0] // n_ref.shape[0]

        def one(c, carry, local_ref=local_ref, global_ref=global_ref, per_tile=per_tile, n_rows=n_rows):
            k = tile * per_tile + c
            visit(pl.multiple_of(local_ref[k], SEG_ALIGN), pl.multiple_of(global_ref[k], SEG_ALIGN), n_rows)
            return carry

        lax.fori_loop(0, n_ref[tile], one, 0)


def _copy_plan(cnt, seg_row0, local_rows):
    done = jnp.zeros_like(cnt)
    local0 = jnp.cumsum(cnt, axis=1) - cnt
    plan = ()
    for k, size in enumerate(CHUNK_ROWS):
        left = cnt - done
        n = left // size if k == 0 else (left // size) % 2
        max_entries = local_rows // size if k == 0 else N_EXPERTS
        end = jnp.cumsum(n, axis=1)
        start = (end - n)[:, None, :]
        c = jnp.arange(max_entries, dtype=jnp.int32)[None, :, None]
        inside = jnp.logical_and(c >= start, c < end[:, None, :])
        pick = lambda base: jnp.sum(jnp.where(inside, base[:, None, :] + size * (c - start), 0), axis=2).reshape(-1)
        plan += (end[:, -1], pick(local0 + done), pick(seg_row0 + done))
        done = done + n * size
    return plan


def _dispatch_kernel(*refs):
    plan = _plan_refs(refs)
    (tail_row_ref, tail_n_ref, n_used_ref, slot_t_ref, h_ref, xs_ref,
     loc_ref, zero_ref, sems, zero_sem) = refs[3 * len(CHUNK_ROWS):]
    tile = pl.program_id(0)
    n_loc, tm = loc_ref.shape[1], h_ref.shape[0]
    buf = tile & 1

    slot = lax.broadcasted_iota(jnp.int32, (n_loc, tm), 0).astype(F32)
    place = jnp.logical_or(slot == slot_t_ref[0, 0:1, :], slot == slot_t_ref[0, 1:2, :])
    loc_ref[buf] = _pack_rows(jnp.dot(_indicator(place, BF16), h_ref[...], preferred_element_type=F32))

    def chunk_copy(b):
        return lambda local, row, n: pltpu.make_async_copy(
            loc_ref.at[b, pl.ds(local, n), :], xs_ref.at[pl.ds(row, n), :], sems.at[b])

    _for_each_chunk(tile, plan, lambda l, r, n: chunk_copy(buf)(l, r, n).start(priority=_copy_queue(n)))

    @pl.when(tile > 0)
    def _previous_done():
        _for_each_chunk(tile - 1, plan, lambda l, r, n: chunk_copy(1 - buf)(l, r, n).wait())

    @pl.when(tile == pl.num_programs(0) - 1)
    def _last_done():
        _for_each_chunk(tile, plan, lambda l, r, n: chunk_copy(buf)(l, r, n).wait())
        zero_ref[...] = jnp.zeros_like(zero_ref)
        block_rows = zero_ref.shape[0]

        def fill(action):
            def per_expert(e, carry):
                per_big = CHUNK_ROWS[0] // SEG_ALIGN
                n_big = tail_n_ref[e] // per_big
                first_small = tail_row_ref[e] + n_big * CHUNK_ROWS[0]

                def zero_rows(row0, n_rows):
                    def one(c, inner):
                        row = pl.multiple_of(row0 + c * n_rows, SEG_ALIGN)
                        action(pltpu.make_async_copy(zero_ref.at[pl.ds(0, n_rows), :],
                                                     xs_ref.at[pl.ds(row, n_rows), :], zero_sem))
                        return inner
                    return one

                lax.fori_loop(0, n_big, zero_rows(tail_row_ref[e], CHUNK_ROWS[0]), 0)
                lax.fori_loop(0, tail_n_ref[e] - n_big * per_big, zero_rows(first_small, SEG_ALIGN), 0)
                return carry

            lax.fori_loop(0, N_EXPERTS, per_expert, 0)

            def per_block(blk, carry):
                row = pl.multiple_of(blk * block_rows, block_rows)
                action(pltpu.make_async_copy(zero_ref, xs_ref.at[pl.ds(row, block_rows), :], zero_sem))
                return carry

            lax.fori_loop(n_used_ref[0], xs_ref.shape[0] // block_rows, per_block, 0)

        fill(lambda copy: copy.start())
        fill(lambda copy: copy.wait())


def _dispatch(plan, tail_row, tail_chunks, n_used, slot_t, h2, n_rows):
    n, d = h2.shape
    tm = TOKEN_TILE
    grid_spec = pltpu.PrefetchScalarGridSpec(
        num_scalar_prefetch=len(plan) + 3,
        grid=(n // tm,),
        in_specs=[
            pl.BlockSpec((1, ROUTE_COLS, tm), lambda i, *_: (i, 0, 0)),
            pl.BlockSpec((tm, d), lambda i, *_: (i, 0)),
        ],
        out_specs=pl.BlockSpec(memory_space=pl.ANY),
        scratch_shapes=[pltpu.VMEM((2, _local_rows(tm), d // 2), U32), pltpu.VMEM((ROW_BLOCK, d // 2), U32),
                        pltpu.SemaphoreType.DMA((2,)), pltpu.SemaphoreType.DMA(())],
    )
    return pl.pallas_call(
        _dispatch_kernel,
        grid_spec=grid_spec,
        out_shape=jax.ShapeDtypeStruct((n_rows, d // 2), U32),
        compiler_params=_params("arbitrary"),
        name="dispatch",
    )(*plan, tail_row, tail_chunks, n_used, slot_t, h2)


def _expert_kernel(blk_e_ref, n_used_ref, xs_ref, wg_ref, wu_ref, wd_ref, ys_ref, wg_bf, wu_bf, wd_bf):
    i = pl.program_id(0)
    used = i < n_used_ref[0]

    @pl.when(jnp.logical_and(used, jnp.logical_or(i == 0, blk_e_ref[i] != blk_e_ref[jnp.maximum(i - 1, 0)])))
    def _new_expert():
        wg_bf[...] = wg_ref[0].astype(BF16)
        wu_bf[...] = wu_ref[0].astype(BF16)
        wd_bf[...] = wd_ref[0].astype(BF16)

    @pl.when(used)
    def _compute():
        xb = _unpack_rows(xs_ref[...]).astype(BF16)
        g = jnp.dot(xb, wg_bf[...], preferred_element_type=F32)
        u = jnp.dot(xb, wu_bf[...], preferred_element_type=F32)
        hid = (g * jax.nn.sigmoid(g) * u).astype(BF16)
        ys_ref[...] = _pack_rows(jnp.dot(hid, wd_bf[...], preferred_element_type=F32))

    @pl.when(i >= n_used_ref[0])
    def _unused():
        ys_ref[...] = jnp.zeros_like(ys_ref)


def _experts(blk_e, n_used, xs, wg, wu, wd):
    n_rows, half = xs.shape
    _, d, de = wg.shape
    grid_spec = pltpu.PrefetchScalarGridSpec(
        num_scalar_prefetch=2,
        grid=(n_rows // ROW_BLOCK,),
        in_specs=[
            pl.BlockSpec((ROW_BLOCK, half), lambda i, e, u: (jnp.minimum(i, u[0] - 1), 0)),
            pl.BlockSpec((1, d, de), lambda i, e, u: (e[i], 0, 0)),
            pl.BlockSpec((1, d, de), lambda i, e, u: (e[i], 0, 0)),
            pl.BlockSpec((1, de, d), lambda i, e, u: (e[i], 0, 0)),
        ],
        out_specs=pl.BlockSpec((ROW_BLOCK, half), lambda i, e, u: (i, 0)),
        scratch_shapes=[pltpu.VMEM((d, de), BF16), pltpu.VMEM((d, de), BF16), pltpu.VMEM((de, d), BF16)],
    )
    return pl.pallas_call(
        _expert_kernel,
        grid_spec=grid_spec,
        out_shape=jax.ShapeDtypeStruct((n_rows, half), U32),
        compiler_params=_params("arbitrary"),
        name="experts",
    )(blk_e, n_used, xs, wg, wu, wd)


def _combine_kernel(*refs):
    plan = _plan_refs(refs)
    x1_ref, mod_ref, rt_ref, ys_ref, o_ref, loc_ref, sems = refs[3 * len(CHUNK_ROWS):]
    tm = x1_ref.shape[1]
    n_loc = loc_ref.shape[1]
    tile = pl.program_id(0) * pl.num_programs(1) + pl.program_id(1)
    n_tiles = pl.num_programs(0) * pl.num_programs(1)
    buf = tile & 1

    def chunk_copy(b):
        return lambda local, row, n: pltpu.make_async_copy(
            ys_ref.at[pl.ds(row, n), :], loc_ref.at[b, pl.ds(local, n), :], sems.at[b])

    @pl.when(tile == 0)
    def _init():
        loc_ref[...] = jnp.zeros_like(loc_ref)
        _for_each_chunk(tile, plan, lambda l, r, n: chunk_copy(buf)(l, r, n).start(priority=_copy_queue(n)))

    @pl.when(tile + 1 < n_tiles)
    def _fetch_next():
        _for_each_chunk(tile + 1, plan, lambda l, r, n: chunk_copy(1 - buf)(l, r, n).start(priority=_copy_queue(n)))

    _for_each_chunk(tile, plan, lambda l, r, n: chunk_copy(buf)(l, r, n).wait())

    y = _unpack_rows(loc_ref[buf]).astype(BF16)
    rt = rt_ref[0]
    slot = lax.broadcasted_iota(jnp.int32, (tm, n_loc), 1).astype(F32)
    mix = jnp.where(slot == rt[:, 0:1], rt[:, 2:3], jnp.where(slot == rt[:, 1:2], rt[:, 3:4], 0.0)).astype(BF16)
    o_ref[0] = x1_ref[0] + mod_ref[0, 5:6, :] * jnp.dot(mix, y, preferred_element_type=F32)


def _combine(plan, x1, mod, rt, ys):
    bsz, s, d = x1.shape
    tm = TOKEN_TILE
    grid_spec = pltpu.PrefetchScalarGridSpec(
        num_scalar_prefetch=len(plan),
        grid=(bsz, s // tm),
        in_specs=[
            pl.BlockSpec((1, tm, d), lambda b, i, *_: (b, i, 0)),
            pl.BlockSpec((1, N_MOD, d), lambda b, i, *_: (b, 0, 0)),
            pl.BlockSpec((1, tm, ROUTE_COLS), lambda b, i, *_: (b, i, 0)),
            pl.BlockSpec(memory_space=pl.ANY),
        ],
        out_specs=pl.BlockSpec((1, tm, d), lambda b, i, *_: (b, i, 0)),
        scratch_shapes=[pltpu.VMEM((2, _local_rows(tm), d // 2), U32), pltpu.SemaphoreType.DMA((2,))],
    )
    return pl.pallas_call(
        _combine_kernel,
        grid_spec=grid_spec,
        out_shape=jax.ShapeDtypeStruct((bsz, s, d), F32),
        compiler_params=_params("arbitrary", "arbitrary"),
        name="combine",
    )(*plan, x1, mod, rt, ys)


def _layer(x, mod, g_mix, w_in, g_q, g_k, rel_bias, w_br_moba, w_br_sb, w_out, g_ffn,
           w_rg, b_rg, w_re, b_re, w_gate, w_up, w_down):
    bsz, s, d = x.shape
    n = bsz * s
    n_blk = s // ATT_BLOCK

    proj, kaug, vaug, kmean = _inproj(x, mod, g_mix, w_in.astype(BF16), g_k)

    top = (n_blk + 1) * ATT_BLOCK - 1
    dist = np.maximum(top - np.arange((n_blk + 2) * ATT_BLOCK), 0)
    rel_rev = rel_bias[:, _rel_bucket_table(top)[dist]].reshape(N_HEADS // HEADS_PER_STEP, HEADS_PER_STEP, -1)

    o_a = _moba(proj, kaug, vaug, kmean, g_q, rel_rev)
    o_b = _stick_breaking(proj)

    pad = LANES - N_GROUPS - N_EXPERTS
    w_router = jnp.concatenate([w_rg, w_re, jnp.zeros((d, pad), F32)], axis=1)
    b_router = jnp.concatenate([b_rg, b_re, jnp.zeros((pad,), F32)]).reshape(LANES, 1)
    x1, h2, rt, slot_t, tile_cnt, tile_before = _merge(x, o_a, o_b, proj, mod, w_br_moba.astype(BF16),
                                                       w_br_sb.astype(BF16), w_out.astype(BF16), g_ffn,
                                                       w_router, b_router)

    experts = slice(N_GROUPS, N_GROUPS + N_EXPERTS)
    cnt = tile_cnt[:, experts, 0].astype(jnp.int32)
    total = jnp.sum(cnt, axis=0)
    padded = (total + ROW_BLOCK - 1) // ROW_BLOCK * ROW_BLOCK
    pend = jnp.cumsum(padded)
    pstart = pend - padded
    max_rows = 2 * n + cnt.shape[0] * N_EXPERTS * (SEG_ALIGN - 1) + N_EXPERTS * (ROW_BLOCK - 1)
    n_blocks = -(-max_rows // ROW_BLOCK)
    blk_row0 = jnp.arange(n_blocks, dtype=jnp.int32) * ROW_BLOCK
    blk_e = jnp.minimum(jnp.sum((blk_row0[:, None] >= pend[None, :]).astype(jnp.int32), axis=1), N_EXPERTS - 1)
    n_used = (pend[-1] // ROW_BLOCK).astype(jnp.int32).reshape(1)
    seg_row0 = pstart[None, :] + tile_before[:, experts, 0].astype(jnp.int32)
    plan = _copy_plan(cnt, seg_row0, _local_rows(TOKEN_TILE))

    xs = _dispatch(plan, pstart + total, (padded - total) // SEG_ALIGN, n_used, slot_t,
                   h2.reshape(n, d), n_blocks * ROW_BLOCK)
    ys = _experts(blk_e, n_used, xs, w_gate, w_up, w_down)
    return _combine(plan, x1, mod, rt, ys)


def kernel(x, c, w_ada, b_ada, g_mix, w_in, g_q, g_k, rel_bias, w_br_moba, w_br_sb, w_out, g_ffn,
           w_rg, b_rg, w_re, b_re, w_gate, w_up, w_down):
    bsz, s, d = x.shape
    assert s % TOKEN_TILE == 0 and s % MOBA_Q_TILE == 0 and s % (SB_Q_BLOCKS * ATT_BLOCK) == 0
    assert d % (2 * LANES) == 0
    for l in range(w_ada.shape[0]):
        mod = _adaln(c, w_ada[l], b_ada[l]).reshape(bsz, N_MOD, d)
        x = _layer(x, mod, g_mix[l], w_in[l], g_q[l], g_k[l], rel_bias, w_br_moba[l], w_br_sb[l], w_out[l],
                   g_ffn[l], w_rg[l], b_rg[l], w_re[l], b_re[l], w_gate[l], w_up[l], w_down[l])
    return x
```

```python
import functools
import math

import numpy as np
import jax
import jax.numpy as jnp
from jax import lax
from jax.experimental import pallas as pl
from jax.experimental.pallas import tpu as pltpu

F32 = jnp.float32
BF16 = jnp.bfloat16
U32 = jnp.uint32
HIGHEST = lax.Precision.HIGHEST

LANES = 128
VMEM_LIMIT_BYTES = 56 * 1024 * 1024

HEAD_DIM = 64
N_HEADS = 8
HEADS_PER_STEP = LANES // HEAD_DIM
ATT_BLOCK = 256
MOBA_Q_TILE = 2 * ATT_BLOCK
SB_Q_BLOCKS = 4
MOBA_TOPK = 3
REL_BUCKETS = 32
REL_MAX_DIST = 1024
N_GROUPS = 4
EXPERTS_PER_GROUP = 8
N_EXPERTS = N_GROUPS * EXPERTS_PER_GROUP
N_MOD = 6
RMS_EPS = 1e-6
ROW_BLOCK = 1024
TOKEN_TILE = 512
NEG_INF = float("-inf")
MASK_LOGIT = -1e30
SB_EXIT_SUM = 110.0

_NT = (((1,), (1,)), ((), ()))


def _params(*semantics):
    return pltpu.CompilerParams(dimension_semantics=semantics, vmem_limit_bytes=VMEM_LIMIT_BYTES)


def _adaln_kernel(c_ref, w_ref, b_ref, o_ref):
    c = c_ref[...]
    s = c * jax.nn.sigmoid(c)
    o_ref[...] = jnp.dot(s, w_ref[...], preferred_element_type=F32, precision=HIGHEST) + b_ref[...]


def _adaln(c, w, b):
    bsz, d = c.shape
    n = w.shape[1]
    tn = 1536
    return pl.pallas_call(
        _adaln_kernel,
        grid=(n // tn,),
        in_specs=[
            pl.BlockSpec((bsz, d), lambda j: (0, 0)),
            pl.BlockSpec((d, tn), lambda j: (0, j)),
            pl.BlockSpec((1, tn), lambda j: (0, j)),
        ],
        out_specs=pl.BlockSpec((bsz, tn), lambda j: (0, j)),
        out_shape=jax.ShapeDtypeStruct((bsz, n), F32),
        compiler_params=_params("arbitrary"),
        name="adaln",
    )(c, w, b.reshape(1, n))


def _rms_modulate(x, g, shift, scale):
    y = x * lax.rsqrt(jnp.mean(x * x, axis=-1, keepdims=True) + RMS_EPS) * g
    return y * (1.0 + scale) + shift


def _inproj_kernel(x_ref, mod_ref, g_ref, w_ref, gk_ref, o_ref, kaug_ref, vaug_ref, kmean_ref, *, col_chunk):
    i = pl.program_id(1)
    tm = x_ref.shape[1]
    h = _rms_modulate(x_ref[0], g_ref[...], mod_ref[0, 0:1, :], mod_ref[0, 1:2, :]).astype(BF16)

    w_moba = N_HEADS * HEAD_DIM
    blocks = tm // ATT_BLOCK
    first = _head_lane_mask()
    row_blk = blocks * i + lax.shift_right_logical(lax.broadcasted_iota(jnp.int32, (tm, LANES), 0),
                                                   ATT_BLOCK.bit_length() - 1)
    halves_of = (first, jnp.logical_not(first))
    tags = [jnp.where(_lane_index() - float(off) == row_blk.astype(F32), 1.0, 0.0) for off in (HEAD_DIM, 0)]
    mean_row = lax.broadcasted_iota(jnp.int32, kmean_ref.shape[2:], 0)

    @pl.when(i == 0)
    def _init():
        kmean_ref[...] = jnp.zeros_like(kmean_ref)

    for n in range(w_ref.shape[1] // col_chunk):
        cols = slice(n * col_chunk, (n + 1) * col_chunk)
        r = jnp.dot(h, w_ref[:, cols], preferred_element_type=F32)
        o_ref[0, :, cols] = r.astype(BF16)
        for c in range(col_chunk // LANES):
            col0 = n * col_chunk + c * LANES
            part = r[:, c * LANES:(c + 1) * LANES]
            if w_moba <= col0 < 2 * w_moba:
                kn = _head_rmsnorm(part, gk_ref[...])
                for hh, mine in enumerate(halves_of):
                    head = (col0 - w_moba) // HEAD_DIM + hh
                    kaug_ref[0, head] = jnp.where(mine, kn, tags[hh]).astype(BF16)
                    means = kmean_ref[0, head]
                    for jb in range(blocks):
                        mean = jnp.mean(kn[jb * ATT_BLOCK:(jb + 1) * ATT_BLOCK], axis=0, keepdims=True)
                        means = jnp.where(mean_row == blocks * i + jb, jnp.where(mine, mean, 0.0), means)
                    kmean_ref[0, head] = means
            elif 2 * w_moba <= col0 < 3 * w_moba:
                for hh, mine in enumerate(halves_of):
                    head = (col0 - 2 * w_moba) // HEAD_DIM + hh
                    vaug_ref[0, head] = jnp.where(mine, part, 1.0).astype(BF16)


def _inproj(x, mod, g, w_bf16, g_k):
    bsz, s, d = x.shape
    n = w_bf16.shape[1]
    tm = TOKEN_TILE
    n_blk = s // ATT_BLOCK
    per_head = lambda rows: pl.BlockSpec((1, N_HEADS, rows, LANES), lambda b, i: (b, 0, i, 0))
    return pl.pallas_call(
        functools.partial(_inproj_kernel, col_chunk=1024),
        grid=(bsz, s // tm),
        in_specs=[
            pl.BlockSpec((1, tm, d), lambda b, i: (b, i, 0)),
            pl.BlockSpec((1, N_MOD, d), lambda b, i: (b, 0, 0)),
            pl.BlockSpec((1, d), lambda b, i: (0, 0)),
            pl.BlockSpec((d, n), lambda b, i: (0, 0)),
            pl.BlockSpec((1, LANES), lambda b, i: (0, 0)),
        ],
        out_specs=[
            pl.BlockSpec((1, tm, n), lambda b, i: (b, i, 0)),
            per_head(tm),
            per_head(tm),
            pl.BlockSpec((1, N_HEADS, -(-n_blk // 8) * 8, LANES), lambda b, i: (b, 0, 0, 0)),
        ],
        out_shape=[
            jax.ShapeDtypeStruct((bsz, s, n), BF16),
            jax.ShapeDtypeStruct((bsz, N_HEADS, s, LANES), BF16),
            jax.ShapeDtypeStruct((bsz, N_HEADS, s, LANES), BF16),
            jax.ShapeDtypeStruct((bsz, N_HEADS, -(-n_blk // 8) * 8, LANES), F32),
        ],
        compiler_params=_params("arbitrary", "arbitrary"),
        name="inproj",
    )(x, mod, g.reshape(1, d), w_bf16, jnp.tile(g_k.reshape(1, HEAD_DIM), (1, HEADS_PER_STEP)))


def _indicator(cond, dtype):
    return jnp.where(cond, 1.0, 0.0).astype(dtype)


def _lane_index():
    return lax.broadcasted_iota(jnp.int32, (1, LANES), 1).astype(F32)


def _head_lane_mask():
    return lax.broadcasted_iota(jnp.int32, (1, LANES), 1) < HEAD_DIM


def _head_rmsnorm(t, g):
    first = _head_lane_mask()
    sq = t * t
    ss0 = jnp.sum(jnp.where(first, sq, 0.0), axis=-1, keepdims=True)
    ss1 = jnp.sum(jnp.where(first, 0.0, sq), axis=-1, keepdims=True)
    inv = jnp.where(first, lax.rsqrt(ss0 / HEAD_DIM + RMS_EPS), lax.rsqrt(ss1 / HEAD_DIM + RMS_EPS))
    return t * inv * g


def _split_heads(t):
    first = _head_lane_mask()
    zero = jnp.zeros_like(t)
    return jnp.where(first, t, zero), jnp.where(first, zero, t)


def _rel_bucket_table(max_dist):
    n = np.arange(max_dist + 1)
    max_exact = REL_BUCKETS // 2
    nf = np.maximum(n, 1).astype(np.float64)
    large = max_exact + (np.log(nf / max_exact) / math.log(REL_MAX_DIST / max_exact)
                         * (REL_BUCKETS - max_exact)).astype(np.int64)
    large = np.minimum(large, REL_BUCKETS - 1)
    return np.where(n < max_exact, n, large).astype(np.int32)


def _moba_kernel(q_ref, kaug_ref, vaug_ref, kmean_ref, gq_ref, rb_ref, o_ref,
                 toep_ref, s_ref, mvec_ref, acc_ref, *, n_blk):
    b = pl.program_id(1)
    qi = pl.program_id(2)
    blk = ATT_BLOCK
    qt = q_ref.shape[1]
    halves = qt // blk
    first = _head_lane_mask()

    @pl.when(jnp.logical_and(b == 0, qi == 0))
    def _build_bias_tiles():
        for h in range(HEADS_PER_STEP):
            for d in range(n_blk):
                r = rb_ref[0, h:h + 1, (n_blk - d) * blk:(n_blk - d + 2) * blk]
                rolled = pltpu.roll(jnp.broadcast_to(r, (blk, 2 * blk)), blk + 1, 1, stride=1, stride_axis=0)
                toep_ref[h, d] = rolled[:, :blk]

    qn = _head_rmsnorm(q_ref[0].astype(F32), gq_ref[...])
    gate_rows = kmean_ref.shape[2]
    blk_row = lax.broadcasted_iota(jnp.int32, (gate_rows, qt), 0).astype(F32)
    own = (halves * qi + lax.shift_right_logical(lax.broadcasted_iota(jnp.int32, (gate_rows, qt), 1),
                                                 blk.bit_length() - 1)).astype(F32)
    q_aug = []
    for h, mine in enumerate((first, jnp.logical_not(first))):
        qh = jnp.where(mine, qn, 0.0)
        km = kmean_ref[0, h]
        km_hi, q_hi = km.astype(BF16), qh.astype(BF16)
        km_lo, q_lo = (km - km_hi.astype(F32)).astype(BF16), (qh - q_hi.astype(F32)).astype(BF16)
        g = (lax.dot_general(km_hi, q_hi, _NT, preferred_element_type=F32)
             + lax.dot_general(km_lo, q_hi, _NT, preferred_element_type=F32)
             + lax.dot_general(km_hi, q_lo, _NT, preferred_element_type=F32))
        g = jnp.where(blk_row < own, g, NEG_INF)
        keep = jnp.where(blk_row == own, 1.0, 0.0)
        for _ in range(MOBA_TOPK):
            top = jnp.max(g, axis=0, keepdims=True)
            is_top = jnp.logical_and(g == top, top > NEG_INF)
            idx = jnp.min(jnp.where(is_top, blk_row, float(gate_rows)), axis=0, keepdims=True)
            pick = blk_row == idx
            keep = jnp.where(pick, 1.0, keep)
            g = jnp.where(pick, NEG_INF, g)
        mask_t = jnp.where(keep > 0.0, 0.0, MASK_LOGIT)
        before = (HEAD_DIM, 0)[h]
        pieces = [jnp.zeros((before, qt), F32)] if before else []
        pieces += [mask_t, jnp.zeros((LANES - before - gate_rows, qt), F32)]
        mask_logit = jnp.concatenate(pieces, axis=0).T
        q_aug.append(jnp.where(mine, qh * (HEAD_DIM ** -0.5), mask_logit).astype(BF16))

    def lane_max(s):
        return jnp.maximum(s[:, :LANES], s[:, LANES:])

    def scores(h, t, dist):
        k_t = kaug_ref[0, h, pl.ds(pl.multiple_of(t * qt, qt), qt), :]
        s = lax.dot_general(q_aug[h], k_t, _NT, preferred_element_type=F32)
        tiles = []
        for kb in range(halves):
            rows = []
            for qb in range(halves):
                part = s[qb * blk:(qb + 1) * blk, kb * blk:(kb + 1) * blk]
                if isinstance(dist, int) and halves * dist + qb - kb < 0:
                    rows.append(part)
                else:
                    rows.append(part + toep_ref[h, halves * dist + qb - kb])
            tiles.append(jnp.concatenate(rows, axis=0))
        return tiles

    row = lax.broadcasted_iota(jnp.int32, (qt, blk), 0)
    col = lax.broadcasted_iota(jnp.int32, (qt, blk), 1)
    for h in range(HEADS_PER_STEP):
        top = None
        for kb, s in enumerate(scores(h, qi, 0)):
            s = jnp.where(col + kb * blk <= row, s, NEG_INF)
            s_ref[h, halves * qi + kb] = s
            top = lane_max(s) if top is None else jnp.maximum(top, lane_max(s))
        mvec_ref[h] = top

    def in_pairs(count, visit):
        def quad(p, carry):
            visit(4 * p, 4)
            return carry

        lax.fori_loop(0, lax.shift_right_logical(count, 2), quad, 0)

        @pl.when((count & 2) == 2)
        def _pair():
            visit(count & ~3, 2)

        @pl.when((count & 1) == 1)
        def _last():
            visit(count - 1, 1)

    def pass1(t0, n_tiles):
        for h in range(HEADS_PER_STEP):
            top = mvec_ref[h]
            for t in [t0 + u for u in range(n_tiles)]:
                for kb, s in enumerate(scores(h, t, qi - t)):
                    s_ref[h, halves * t + kb] = s
                    top = jnp.maximum(top, lane_max(s))
            mvec_ref[h] = top

    in_pairs(qi, pass1)

    for h in range(HEADS_PER_STEP):
        m = jnp.max(mvec_ref[h], axis=-1, keepdims=True)
        mvec_ref[h] = jnp.broadcast_to(m, (qt, LANES))
        acc_ref[h] = jnp.zeros((qt, LANES), F32)

    def pass2(t0, n_tiles):
        keys = pl.ds(pl.multiple_of(t0 * qt, qt), n_tiles * qt)
        for h in range(HEADS_PER_STEP):
            m = mvec_ref[h]
            parts = []
            for kb in range(n_tiles * halves):
                s = s_ref[h, halves * t0 + kb]
                parts += [jnp.exp(s[:, :LANES] - m), jnp.exp(s[:, LANES:] - m)]
            p = jnp.concatenate(parts, axis=1).astype(BF16)
            acc_ref[h] += jnp.dot(p, vaug_ref[0, h, keys, :], preferred_element_type=F32)

    in_pairs(qi + 1, pass2)

    out = [acc_ref[h] / pltpu.roll(acc_ref[h], HEAD_DIM, 1) for h in range(HEADS_PER_STEP)]
    o_ref[0] = jnp.where(first, out[0], out[1]).astype(o_ref.dtype)


def _moba(proj, kaug, vaug, kmean, g_q, rel_rev):
    bsz, s, _ = proj.shape
    blk = ATT_BLOCK
    n_blk = s // blk
    n_hp = N_HEADS // HEADS_PER_STEP
    w = N_HEADS * HEAD_DIM
    qt = MOBA_Q_TILE
    per_pair = lambda arr: pl.BlockSpec((1, HEADS_PER_STEP) + arr.shape[2:], lambda hp, b, i: (b, hp, 0, 0))
    return pl.pallas_call(
        functools.partial(_moba_kernel, n_blk=n_blk),
        grid=(n_hp, bsz, s // qt),
        in_specs=[
            pl.BlockSpec((1, qt, LANES), lambda hp, b, i: (b, i, hp)),
            per_pair(kaug), per_pair(vaug), per_pair(kmean),
            pl.BlockSpec((1, LANES), lambda hp, b, i: (0, 0)),
            pl.BlockSpec((1, HEADS_PER_STEP, rel_rev.shape[-1]), lambda hp, b, i: (hp, 0, 0)),
        ],
        out_specs=pl.BlockSpec((1, qt, LANES), lambda hp, b, i: (b, i, hp)),
        out_shape=jax.ShapeDtypeStruct((bsz, s, w), BF16),
        scratch_shapes=[
            pltpu.VMEM((HEADS_PER_STEP, n_blk, blk, blk), F32),
            pltpu.VMEM((HEADS_PER_STEP, n_blk, qt, blk), F32),
            pltpu.VMEM((HEADS_PER_STEP, qt, LANES), F32),
            pltpu.VMEM((HEADS_PER_STEP, qt, LANES), F32),
        ],
        compiler_params=_params("arbitrary", "arbitrary", "arbitrary"),
        name="moba",
    )(proj, kaug, vaug, kmean, jnp.tile(g_q.reshape(1, HEAD_DIM), (1, HEADS_PER_STEP)), rel_rev)


def _sb_kernel(q_ref, k_ref, v_ref, o_ref, acc_ref, carry_ref):
    ti = pl.program_id(2)
    blk = ATT_BLOCK
    n_sub = q_ref.shape[1] // blk
    first = n_sub * ti
    row = lax.broadcasted_iota(jnp.int32, (blk, blk), 0)
    col = lax.broadcasted_iota(jnp.int32, (blk, blk), 1)
    past = col < row
    later = _indicator(row > col, BF16)

    q_heads = []
    for sub in range(n_sub):
        heads = _split_heads(q_ref[0, sub * blk:(sub + 1) * blk, :].astype(F32) * (HEAD_DIM ** -0.5))
        q_heads.append([t.astype(BF16) for t in heads])

    def walk(jobs, fresh):
        for sub, blocks in jobs:
            for h in range(HEADS_PER_STEP):
                carry = None if fresh else carry_ref[sub, h]
                total = None if fresh else acc_ref[sub, h]
                for j, diagonal in blocks:
                    start = j * blk if isinstance(j, int) else pl.multiple_of(j * blk, blk)
                    z = lax.dot_general(q_heads[sub][h], k_ref[0, pl.ds(start, blk), :], _NT,
                                        preferred_element_type=F32)
                    sp = jnp.maximum(z, 0.0) + jnp.log(1.0 + jnp.exp(-jnp.abs(z)))
                    masked = jnp.where(past, sp, 0.0) if diagonal else sp
                    after = jnp.dot(masked.astype(BF16), later, preferred_element_type=F32)
                    row_sum = after[:, 0:1] + masked[:, 0:1]
                    a = jnp.exp(z - sp - after)
                    if diagonal:
                        a = jnp.where(past, a, 0.0)
                    pv = jnp.dot(a.astype(BF16), v_ref[0, pl.ds(start, blk), :], preferred_element_type=F32)
                    if carry is not None:
                        pv = pv * jnp.exp(-carry)
                    total = pv if total is None else total + pv
                    carry = row_sum if carry is None else carry + row_sum
                carry_ref[sub, h] = carry
                acc_ref[sub, h] = total

    @pl.when(ti == 0)
    def _first_step():
        walk([(sub, [(sub, True)] + ([(sub - 1, False)] if sub else [])) for sub in range(n_sub)], True)

    @pl.when(ti > 0)
    def _own_and_previous():
        walk([(sub, [(first + sub, True), (first + sub - 1, False)]) for sub in range(n_sub)], True)

    def slowest_decay(subs):
        worst = functools.reduce(jnp.minimum, [carry_ref[sub, h] for sub in subs for h in range(HEADS_PER_STEP)])
        return jnp.min(worst)

    @pl.when(slowest_decay(range(n_sub)) < SB_EXIT_SUM)
    def _walk_further_back():
        for sub in range(n_sub):
            def cond(state):
                j, worst = state
                return jnp.logical_and(j >= 0, worst < SB_EXIT_SUM)

            def body(state, sub=sub):
                j, _ = state
                walk([(sub, [(j, False)])], False)
                return j - 1, slowest_decay([sub])

            lax.while_loop(cond, body, (first + sub - 2, slowest_decay([sub])))

    for sub in range(n_sub):
        o_ref[0, sub * blk:(sub + 1) * blk, :] = jnp.where(
            _head_lane_mask(), acc_ref[sub, 0], acc_ref[sub, 1]).astype(o_ref.dtype)


def _stick_breaking(proj):
    bsz, s, _ = proj.shape
    blk = ATT_BLOCK
    qt = SB_Q_BLOCKS * blk
    n_hp = N_HEADS // HEADS_PER_STEP
    w = N_HEADS * HEAD_DIM
    base = 3 * n_hp
    return pl.pallas_call(
        _sb_kernel,
        grid=(n_hp, bsz, s // qt),
        in_specs=[
            pl.BlockSpec((1, qt, LANES), lambda hp, b, i: (b, i, base + hp)),
            pl.BlockSpec((1, s, LANES), lambda hp, b, i: (b, 0, base + n_hp + hp)),
            pl.BlockSpec((1, s, LANES), lambda hp, b, i: (b, 0, base + 2 * n_hp + hp)),
        ],
        out_specs=pl.BlockSpec((1, qt, LANES), lambda hp, b, i: (b, i, hp)),
        out_shape=jax.ShapeDtypeStruct((bsz, s, w), BF16),
        scratch_shapes=[
            pltpu.VMEM((SB_Q_BLOCKS, HEADS_PER_STEP, blk, LANES), F32),
            pltpu.VMEM((SB_Q_BLOCKS, HEADS_PER_STEP, blk, 1), F32),
        ],
        compiler_params=_params("arbitrary", "arbitrary", "arbitrary"),
        name="stickbrk",
    )(proj, proj, proj)


def _pack_rows(t):
    n = t.shape[1] // 2
    bits = pltpu.bitcast(t.astype(BF16).astype(F32), U32)
    return bits[:, :n] | (bits[:, n:] >> 16)


def _unpack_rows(p):
    hi = pltpu.bitcast(p & jnp.uint32(0xFFFF0000), F32)
    lo = pltpu.bitcast(p << 16, F32)
    return jnp.concatenate([hi, lo], axis=1)


def _merge_kernel(x_ref, oa_ref, ob_ref, ga_ref, gb_ref, mod_ref, wbm_ref, wbs_ref, wo_ref, gf_ref,
                  wrh_ref, wrl_ref, br_ref, x1_ref, h2_ref, rt_ref, slot_t_ref, cnt_ref, before_ref, carry_ref,
                  wbm_bf, wbs_bf, wo_bf):
    first_tile = jnp.logical_and(pl.program_id(0) == 0, pl.program_id(1) == 0)

    @pl.when(first_tile)
    def _cast_weights():
        wbm_bf[...] = wbm_ref[...].astype(BF16)
        wbs_bf[...] = wbs_ref[...].astype(BF16)
        wo_bf[...] = wo_ref[...].astype(BF16)

    ma = jnp.dot(oa_ref[0], wbm_bf[...], preferred_element_type=F32)
    mb = jnp.dot(ob_ref[0], wbs_bf[...], preferred_element_type=F32)
    merged = (jax.nn.sigmoid(ga_ref[0].astype(F32)) * ma + jax.nn.sigmoid(gb_ref[0].astype(F32)) * mb)
    t = jnp.dot(merged.astype(BF16), wo_bf[...], preferred_element_type=F32)
    x1 = x_ref[0] + mod_ref[0, 2:3, :] * t
    x1_ref[0] = x1
    h2 = _rms_modulate(x1, gf_ref[...], mod_ref[0, 3:4, :], mod_ref[0, 4:5, :])
    hi = h2.astype(BF16)
    h2_ref[0] = hi
    lo = (h2 - hi.astype(F32)).astype(BF16)
    logits = (lax.dot_general(wrh_ref[...], hi, _NT, preferred_element_type=F32)
              + lax.dot_general(wrh_ref[...], lo, _NT, preferred_element_type=F32)
              + lax.dot_general(wrl_ref[...], hi, _NT, preferred_element_type=F32) + br_ref[...])
    _route_tile(logits, first_tile, rt_ref, slot_t_ref, cnt_ref, before_ref, carry_ref)


def _merge(x, o_a, o_b, proj, mod, wbm, wbs, wo, g_ffn, w_router, b_router):
    bsz, s, d = x.shape
    w = o_a.shape[-1]
    tm = TOKEN_TILE
    gate_blk = (proj.shape[-1] - 2 * d) // d
    const = lambda shape: pl.BlockSpec(shape, lambda b, i: (0,) * len(shape))
    tok = lambda width: pl.BlockSpec((1, tm, width), lambda b, i: (b, i, 0))
    wr_hi = w_router.T.astype(BF16)
    wr_lo = (w_router.T - wr_hi.astype(F32)).astype(BF16)
    per_b = s // tm
    tiles = bsz * per_b
    per_tile = lambda rows, cols: pl.BlockSpec((1, rows, cols), lambda b, i: (b * per_b + i, 0, 0))
    return pl.pallas_call(
        _merge_kernel,
        grid=(bsz, s // tm),
        in_specs=[
            tok(d), tok(w), tok(w),
            pl.BlockSpec((1, tm, d), lambda b, i: (b, i, gate_blk)),
            pl.BlockSpec((1, tm, d), lambda b, i: (b, i, gate_blk + 1)),
            pl.BlockSpec((1, N_MOD, d), lambda b, i: (b, 0, 0)),
            const((w, d)), const((w, d)), const((d, d)), const((1, d)),
            const((LANES, d)), const((LANES, d)), const((LANES, 1)),
        ],
        out_specs=[tok(d), tok(d), tok(ROUTE_COLS), per_tile(ROUTE_COLS, tm), per_tile(LANES, 1), per_tile(LANES, 1)],
        out_shape=[
            jax.ShapeDtypeStruct((bsz, s, d), F32),
            jax.ShapeDtypeStruct((bsz, s, d), BF16),
            jax.ShapeDtypeStruct((bsz, s, ROUTE_COLS), F32),
            jax.ShapeDtypeStruct((tiles, ROUTE_COLS, tm), F32),
            jax.ShapeDtypeStruct((tiles, LANES, 1), F32),
            jax.ShapeDtypeStruct((tiles, LANES, 1), F32),
        ],
        scratch_shapes=[pltpu.VMEM((LANES, 1), F32),
                        pltpu.VMEM((w, d), BF16), pltpu.VMEM((w, d), BF16), pltpu.VMEM((d, d), BF16)],
        compiler_params=_params("arbitrary", "arbitrary"),
        name="merge",
    )(x, o_a, o_b, proj, proj, mod, wbm, wbs, wo, g_ffn.reshape(1, d), wr_hi, wr_lo, b_router)


ROUTE_COLS = 8
SEG_ALIGN = 8


def _local_rows(tm):
    return 2 * tm + N_EXPERTS * SEG_ALIGN


def _route_tile(lg_t, first_tile, rt_ref, slot_t_ref, cnt_ref, before_ref, carry_ref):
    tm = lg_t.shape[1]
    rows = -(-(N_GROUPS + N_EXPERTS) // 8) * 8
    lg = lg_t[:rows]
    row = lax.broadcasted_iota(jnp.int32, (rows, tm), 0).astype(F32)

    @pl.when(first_tile)
    def _init():
        carry_ref[...] = jnp.zeros_like(carry_ref)

    def first_max(vals):
        top = jnp.max(vals, axis=0, keepdims=True)
        idx = jnp.min(jnp.where(vals == top, row, float(rows)), axis=0, keepdims=True)
        return top, idx

    g_logit = jnp.where(row < N_GROUPS, lg, NEG_INF)
    g_top, g_idx = first_max(g_logit)
    g_w = 1.0 / jnp.sum(jnp.exp(g_logit - g_top), axis=0, keepdims=True)
    lo = N_GROUPS + g_idx * EXPERTS_PER_GROUP
    e_logit = jnp.where(jnp.logical_and(row >= lo, row < lo + EXPERTS_PER_GROUP), lg, NEG_INF)
    e_top0, row0 = first_max(e_logit)
    e_top1, row1 = first_max(jnp.where(row == row0, NEG_INF, e_logit))
    z = jnp.exp(e_top1 - e_top0)
    w0 = g_w / (1.0 + z)
    w1 = g_w * z / (1.0 + z)

    hot = jnp.logical_or(row == row0, row == row1)
    t_src = lax.broadcasted_iota(jnp.int32, (tm, tm), 0)
    t_dst = lax.broadcasted_iota(jnp.int32, (tm, tm), 1)
    earlier = jnp.dot(_indicator(hot, BF16), _indicator(t_src < t_dst, BF16), preferred_element_type=F32)
    cnt = jnp.sum(_indicator(hot, F32), axis=1, keepdims=True)
    chunks = jnp.ceil(cnt * (1.0 / SEG_ALIGN))
    spare = jnp.zeros((LANES - rows, LANES), F32)
    e_dst = lax.broadcasted_iota(jnp.int32, (LANES, LANES), 0)
    e_src = lax.broadcasted_iota(jnp.int32, (LANES, LANES), 1)
    chunks_wide = jnp.concatenate([jnp.broadcast_to(chunks, (rows, LANES)), spare], axis=0).astype(BF16)
    seg_start = SEG_ALIGN * jnp.dot(_indicator(e_src < e_dst, BF16), chunks_wide,
                                    preferred_element_type=F32)[:rows, 0:1]
    local = seg_start + earlier
    slot0 = jnp.sum(jnp.where(row == row0, local, 0.0), axis=0, keepdims=True)
    slot1 = jnp.sum(jnp.where(row == row1, local, 0.0), axis=0, keepdims=True)
    seg_rows = jnp.concatenate([SEG_ALIGN * chunks, spare[:, 0:1]], axis=0)
    cnt_ref[0] = seg_rows
    before_ref[0] = carry_ref[...]
    carry_ref[...] += seg_rows

    out_row = lax.broadcasted_iota(jnp.int32, (ROUTE_COLS, tm), 0)
    packed = jnp.zeros((ROUTE_COLS, tm), F32)
    for c, v in enumerate([slot0, slot1, w0, w1]):
        packed = jnp.where(out_row == c, v, packed)
    slot_t_ref[0] = packed
    rt_ref[0] = jnp.concatenate([packed, jnp.zeros((LANES - ROUTE_COLS, tm), F32)], axis=0).T[:, :ROUTE_COLS]


CHUNK_ROWS = (8 * SEG_ALIGN, 4 * SEG_ALIGN, 2 * SEG_ALIGN, SEG_ALIGN)


def _plan_refs(refs):
    return tuple(tuple(refs[3 * k:3 * k + 3]) for k in range(len(CHUNK_ROWS)))


def _for_each_chunk(tile, plan, visit):
    for (n_ref, local_ref, global_ref), n_rows in zip(plan, CHUNK_ROWS):
        per_tile = local_ref.shape[0] // n_ref.shape[0]

        def one(c, carry, local_ref=local_ref, global_ref=global_ref, per_tile=per_tile, n_rows=n_rows):
            k = tile * per_tile + c
            visit(pl.multiple_of(local_ref[k], SEG_ALIGN), pl.multiple_of(global_ref[k], SEG_ALIGN), n_rows)
            return carry

        lax.fori_loop(0, n_ref[tile], one, 0)


def _copy_plan(cnt, seg_row0, local_rows):
    done = jnp.zeros_like(cnt)
    local0 = jnp.cumsum(cnt, axis=1) - cnt
    plan = ()
    for k, size in enumerate(CHUNK_ROWS):
        left = cnt - done
        n = left // size if k == 0 else (left // size) % 2
        max_entries = local_rows // size if k == 0 else N_EXPERTS
        end = jnp.cumsum(n, axis=1)
        start = (end - n)[:, None, :]
        c = jnp.arange(max_entries, dtype=jnp.int32)[None, :, None]
        inside = jnp.logical_and(c >= start, c < end[:, None, :])
        pick = lambda base: jnp.sum(jnp.where(inside, base[:, None, :] + size * (c - start), 0), axis=2).reshape(-1)
        plan += (end[:, -1], pick(local0 + done), pick(seg_row0 + done))
        done = done + n * size
    return plan


def _dispatch_kernel(*refs):
    plan = _plan_refs(refs)
    (tail_row_ref, tail_n_ref, n_used_ref, slot_t_ref, h_ref, xs_ref,
     loc_ref, zero_ref, sems, zero_sem) = refs[3 * len(CHUNK_ROWS):]
    tile = pl.program_id(0)
    n_loc, tm = loc_ref.shape[1], h_ref.shape[0]
    buf = tile & 1

    slot = lax.broadcasted_iota(jnp.int32, (n_loc, tm), 0).astype(F32)
    place = jnp.logical_or(slot == slot_t_ref[0, 0:1, :], slot == slot_t_ref[0, 1:2, :])
    loc_ref[buf] = _pack_rows(jnp.dot(_indicator(place, BF16), h_ref[...], preferred_element_type=F32))

    def chunk_copy(b):
        return lambda local, row, n: pltpu.make_async_copy(
            loc_ref.at[b, pl.ds(local, n), :], xs_ref.at[pl.ds(row, n), :], sems.at[b])

    _for_each_chunk(tile, plan, lambda l, r, n: chunk_copy(buf)(l, r, n).start())

    @pl.when(tile > 0)
    def _previous_done():
        _for_each_chunk(tile - 1, plan, lambda l, r, n: chunk_copy(1 - buf)(l, r, n).wait())

    @pl.when(tile == pl.num_programs(0) - 1)
    def _last_done():
        _for_each_chunk(tile, plan, lambda l, r, n: chunk_copy(buf)(l, r, n).wait())
        zero_ref[...] = jnp.zeros_like(zero_ref)
        block_rows = zero_ref.shape[0]

        def fill(action):
            def per_expert(e, carry):
                per_big = CHUNK_ROWS[0] // SEG_ALIGN
                n_big = tail_n_ref[e] // per_big
                first_small = tail_row_ref[e] + n_big * CHUNK_ROWS[0]

                def zero_rows(row0, n_rows):
                    def one(c, inner):
                        row = pl.multiple_of(row0 + c * n_rows, SEG_ALIGN)
                        action(pltpu.make_async_copy(zero_ref.at[pl.ds(0, n_rows), :],
                                                     xs_ref.at[pl.ds(row, n_rows), :], zero_sem))
                        return inner
                    return one

                lax.fori_loop(0, n_big, zero_rows(tail_row_ref[e], CHUNK_ROWS[0]), 0)
                lax.fori_loop(0, tail_n_ref[e] - n_big * per_big, zero_rows(first_small, SEG_ALIGN), 0)
                return carry

            lax.fori_loop(0, N_EXPERTS, per_expert, 0)

            def per_block(blk, carry):
                row = pl.multiple_of(blk * block_rows, block_rows)
                action(pltpu.make_async_copy(zero_ref, xs_ref.at[pl.ds(row, block_rows), :], zero_sem))
                return carry

            lax.fori_loop(n_used_ref[0], xs_ref.shape[0] // block_rows, per_block, 0)

        fill(lambda copy: copy.start())
        fill(lambda copy: copy.wait())


def _dispatch(plan, tail_row, tail_chunks, n_used, slot_t, h2, n_rows):
    n, d = h2.shape
    tm = TOKEN_TILE
    grid_spec = pltpu.PrefetchScalarGridSpec(
        num_scalar_prefetch=len(plan) + 3,
        grid=(n // tm,),
        in_specs=[
            pl.BlockSpec((1, ROUTE_COLS, tm), lambda i, *_: (i, 0, 0)),
            pl.BlockSpec((tm, d), lambda i, *_: (i, 0)),
        ],
        out_specs=pl.BlockSpec(memory_space=pl.ANY),
        scratch_shapes=[pltpu.VMEM((2, _local_rows(tm), d // 2), U32), pltpu.VMEM((ROW_BLOCK, d // 2), U32),
                        pltpu.SemaphoreType.DMA((2,)), pltpu.SemaphoreType.DMA(())],
    )
    return pl.pallas_call(
        _dispatch_kernel,
        grid_spec=grid_spec,
        out_shape=jax.ShapeDtypeStruct((n_rows, d // 2), U32),
        compiler_params=_params("arbitrary"),
        name="dispatch",
    )(*plan, tail_row, tail_chunks, n_used, slot_t, h2)


def _expert_kernel(blk_e_ref, n_used_ref, xs_ref, wg_ref, wu_ref, wd_ref, ys_ref, wg_bf, wu_bf, wd_bf):
    i = pl.program_id(0)
    used = i < n_used_ref[0]

    @pl.when(jnp.logical_and(used, jnp.logical_or(i == 0, blk_e_ref[i] != blk_e_ref[jnp.maximum(i - 1, 0)])))
    def _new_expert():
        wg_bf[...] = wg_ref[0].astype(BF16)
        wu_bf[...] = wu_ref[0].astype(BF16)
        wd_bf[...] = wd_ref[0].astype(BF16)

    @pl.when(used)
    def _compute():
        xb = _unpack_rows(xs_ref[...]).astype(BF16)
        g = jnp.dot(xb, wg_bf[...], preferred_element_type=F32)
        u = jnp.dot(xb, wu_bf[...], preferred_element_type=F32)
        hid = (g * jax.nn.sigmoid(g) * u).astype(BF16)
        ys_ref[...] = _pack_rows(jnp.dot(hid, wd_bf[...], preferred_element_type=F32))

    @pl.when(i >= n_used_ref[0])
    def _unused():
        ys_ref[...] = jnp.zeros_like(ys_ref)


def _experts(blk_e, n_used, xs, wg, wu, wd):
    n_rows, half = xs.shape
    _, d, de = wg.shape
    grid_spec = pltpu.PrefetchScalarGridSpec(
        num_scalar_prefetch=2,
        grid=(n_rows // ROW_BLOCK,),
        in_specs=[
            pl.BlockSpec((ROW_BLOCK, half), lambda i, e, u: (jnp.minimum(i, u[0] - 1), 0)),
            pl.BlockSpec((1, d, de), lambda i, e, u: (e[i], 0, 0)),
            pl.BlockSpec((1, d, de), lambda i, e, u: (e[i], 0, 0)),
            pl.BlockSpec((1, de, d), lambda i, e, u: (e[i], 0, 0)),
        ],
        out_specs=pl.BlockSpec((ROW_BLOCK, half), lambda i, e, u: (i, 0)),
        scratch_shapes=[pltpu.VMEM((d, de), BF16), pltpu.VMEM((d, de), BF16), pltpu.VMEM((de, d), BF16)],
    )
    return pl.pallas_call(
        _expert_kernel,
        grid_spec=grid_spec,
        out_shape=jax.ShapeDtypeStruct((n_rows, half), U32),
        compiler_params=_params("arbitrary"),
        name="experts",
    )(blk_e, n_used, xs, wg, wu, wd)


def _combine_kernel(*refs):
    plan = _plan_refs(refs)
    x1_ref, mod_ref, rt_ref, ys_ref, o_ref, loc_ref, sems = refs[3 * len(CHUNK_ROWS):]
    tm = x1_ref.shape[1]
    n_loc = loc_ref.shape[1]
    tile = pl.program_id(0) * pl.num_programs(1) + pl.program_id(1)
    n_tiles = pl.num_programs(0) * pl.num_programs(1)
    buf = tile & 1

    def chunk_copy(b):
        return lambda local, row, n: pltpu.make_async_copy(
            ys_ref.at[pl.ds(row, n), :], loc_ref.at[b, pl.ds(local, n), :], sems.at[b])

    @pl.when(tile == 0)
    def _init():
        loc_ref[...] = jnp.zeros_like(loc_ref)
        _for_each_chunk(tile, plan, lambda l, r, n: chunk_copy(buf)(l, r, n).start())

    @pl.when(tile + 1 < n_tiles)
    def _fetch_next():
        _for_each_chunk(tile + 1, plan, lambda l, r, n: chunk_copy(1 - buf)(l, r, n).start())

    _for_each_chunk(tile, plan, lambda l, r, n: chunk_copy(buf)(l, r, n).wait())

    y = _unpack_rows(loc_ref[buf]).astype(BF16)
    rt = rt_ref[0]
    slot = lax.broadcasted_iota(jnp.int32, (tm, n_loc), 1).astype(F32)
    mix = jnp.where(slot == rt[:, 0:1], rt[:, 2:3], jnp.where(slot == rt[:, 1:2], rt[:, 3:4], 0.0)).astype(BF16)
    o_ref[0] = x1_ref[0] + mod_ref[0, 5:6, :] * jnp.dot(mix, y, preferred_element_type=F32)


def _combine(plan, x1, mod, rt, ys):
    bsz, s, d = x1.shape
    tm = TOKEN_TILE
    grid_spec = pltpu.PrefetchScalarGridSpec(
        num_scalar_prefetch=len(plan),
        grid=(bsz, s // tm),
        in_specs=[
            pl.BlockSpec((1, tm, d), lambda b, i, *_: (b, i, 0)),
            pl.BlockSpec((1, N_MOD, d), lambda b, i, *_: (b, 0, 0)),
            pl.BlockSpec((1, tm, ROUTE_COLS), lambda b, i, *_: (b, i, 0)),
            pl.BlockSpec(memory_space=pl.ANY),
        ],
        out_specs=pl.BlockSpec((1, tm, d), lambda b, i, *_: (b, i, 0)),
        scratch_shapes=[pltpu.VMEM((2, _local_rows(tm), d // 2), U32), pltpu.SemaphoreType.DMA((2,))],
    )
    return pl.pallas_call(
        _combine_kernel,
        grid_spec=grid_spec,
        out_shape=jax.ShapeDtypeStruct((bsz, s, d), F32),
        compiler_params=_params("arbitrary", "arbitrary"),
        name="combine",
    )(*plan, x1, mod, rt, ys)


def _layer(x, mod, g_mix, w_in, g_q, g_k, rel_bias, w_br_moba, w_br_sb, w_out, g_ffn,
           w_rg, b_rg, w_re, b_re, w_gate, w_up, w_down):
    bsz, s, d = x.shape
    n = bsz * s
    n_blk = s // ATT_BLOCK

    proj, kaug, vaug, kmean = _inproj(x, mod, g_mix, w_in.astype(BF16), g_k)

    top = (n_blk + 1) * ATT_BLOCK - 1
    dist = np.maximum(top - np.arange((n_blk + 2) * ATT_BLOCK), 0)
    rel_rev = rel_bias[:, _rel_bucket_table(top)[dist]].reshape(N_HEADS // HEADS_PER_STEP, HEADS_PER_STEP, -1)

    o_a = _moba(proj, kaug, vaug, kmean, g_q, rel_rev)
    o_b = _stick_breaking(proj)

    pad = LANES - N_GROUPS - N_EXPERTS
    w_router = jnp.concatenate([w_rg, w_re, jnp.zeros((d, pad), F32)], axis=1)
    b_router = jnp.concatenate([b_rg, b_re, jnp.zeros((pad,), F32)]).reshape(LANES, 1)
    x1, h2, rt, slot_t, tile_cnt, tile_before = _merge(x, o_a, o_b, proj, mod, w_br_moba, w_br_sb, w_out, g_ffn,
                                                       w_router, b_router)

    experts = slice(N_GROUPS, N_GROUPS + N_EXPERTS)
    cnt = tile_cnt[:, experts, 0].astype(jnp.int32)
    total = jnp.sum(cnt, axis=0)
    padded = (total + ROW_BLOCK - 1) // ROW_BLOCK * ROW_BLOCK
    pend = jnp.cumsum(padded)
    pstart = pend - padded
    max_rows = 2 * n + cnt.shape[0] * N_EXPERTS * (SEG_ALIGN - 1) + N_EXPERTS * (ROW_BLOCK - 1)
    n_blocks = -(-max_rows // ROW_BLOCK)
    blk_row0 = jnp.arange(n_blocks, dtype=jnp.int32) * ROW_BLOCK
    blk_e = jnp.minimum(jnp.sum((blk_row0[:, None] >= pend[None, :]).astype(jnp.int32), axis=1), N_EXPERTS - 1)
    n_used = (pend[-1] // ROW_BLOCK).astype(jnp.int32).reshape(1)
    seg_row0 = pstart[None, :] + tile_before[:, experts, 0].astype(jnp.int32)
    plan = _copy_plan(cnt, seg_row0, _local_rows(TOKEN_TILE))

    xs = _dispatch(plan, pstart + total, (padded - total) // SEG_ALIGN, n_used, slot_t,
                   h2.reshape(n, d), n_blocks * ROW_BLOCK)
    ys = _experts(blk_e, n_used, xs, w_gate, w_up, w_down)
    return _combine(plan, x1, mod, rt, ys)


def kernel(x, c, w_ada, b_ada, g_mix, w_in, g_q, g_k, rel_bias, w_br_moba, w_br_sb, w_out, g_ffn,
           w_rg, b_rg, w_re, b_re, w_gate, w_up, w_down):
    bsz, s, d = x.shape
    assert s % TOKEN_TILE == 0 and s % MOBA_Q_TILE == 0 and s % (SB_Q_BLOCKS * ATT_BLOCK) == 0
    assert d % (2 * LANES) == 0
    for l in range(w_ada.shape[0]):
        mod = _adaln(c, w_ada[l], b_ada[l]).reshape(bsz, N_MOD, d)
        x = _layer(x, mod, g_mix[l], w_in[l], g_q[l], g_k[l], rel_bias, w_br_moba[l], w_br_sb[l], w_out[l],
                   g_ffn[l], w_rg[l], b_rg[l], w_re[l], b_re[l], w_gate[l], w_up[l], w_down[l])
    return x
```

```python
import functools
import math

import numpy as np
import jax
import jax.numpy as jnp
from jax import lax
from jax.experimental import pallas as pl
from jax.experimental.pallas import tpu as pltpu

F32 = jnp.float32
BF16 = jnp.bfloat16
U32 = jnp.uint32
HIGHEST = lax.Precision.HIGHEST

LANES = 128
VMEM_LIMIT_BYTES = 56 * 1024 * 1024

HEAD_DIM = 64
N_HEADS = 8
HEADS_PER_STEP = LANES // HEAD_DIM
ATT_BLOCK = 256
MOBA_Q_TILE = 2 * ATT_BLOCK
SB_Q_BLOCKS = 8
MOBA_TOPK = 3
REL_BUCKETS = 32
REL_MAX_DIST = 1024
N_GROUPS = 4
EXPERTS_PER_GROUP = 8
N_EXPERTS = N_GROUPS * EXPERTS_PER_GROUP
N_MOD = 6
RMS_EPS = 1e-6
ROW_BLOCK = 1024
TOKEN_TILE = 512
NEG_INF = float("-inf")
MASK_LOGIT = -1e30
SB_EXIT_SUM = 110.0

_NT = (((1,), (1,)), ((), ()))


def _params(*semantics):
    return pltpu.CompilerParams(dimension_semantics=semantics, vmem_limit_bytes=VMEM_LIMIT_BYTES)


def _adaln_kernel(c_ref, w_ref, b_ref, o_ref):
    c = c_ref[...]
    s = c * jax.nn.sigmoid(c)
    o_ref[...] = jnp.dot(s, w_ref[...], preferred_element_type=F32, precision=HIGHEST) + b_ref[...]


def _adaln(c, w, b):
    bsz, d = c.shape
    n = w.shape[1]
    tn = 1536
    return pl.pallas_call(
        _adaln_kernel,
        grid=(n // tn,),
        in_specs=[
            pl.BlockSpec((bsz, d), lambda j: (0, 0)),
            pl.BlockSpec((d, tn), lambda j: (0, j)),
            pl.BlockSpec((1, tn), lambda j: (0, j)),
        ],
        out_specs=pl.BlockSpec((bsz, tn), lambda j: (0, j)),
        out_shape=jax.ShapeDtypeStruct((bsz, n), F32),
        compiler_params=_params("arbitrary"),
        name="adaln",
    )(c, w, b.reshape(1, n))


def _rms_modulate(x, g, shift, scale):
    y = x * lax.rsqrt(jnp.mean(x * x, axis=-1, keepdims=True) + RMS_EPS) * g
    return y * (1.0 + scale) + shift


def _inproj_kernel(x_ref, mod_ref, g_ref, w_ref, gk_ref, o_ref, kaug_ref, vaug_ref, kmean_ref, *, col_chunk):
    i = pl.program_id(1)
    tm = x_ref.shape[1]
    h = _rms_modulate(x_ref[0], g_ref[...], mod_ref[0, 0:1, :], mod_ref[0, 1:2, :]).astype(BF16)

    w_moba = N_HEADS * HEAD_DIM
    blocks = tm // ATT_BLOCK
    first = _head_lane_mask()
    row_blk = blocks * i + lax.shift_right_logical(lax.broadcasted_iota(jnp.int32, (tm, LANES), 0),
                                                   ATT_BLOCK.bit_length() - 1)
    halves_of = (first, jnp.logical_not(first))
    tags = [jnp.where(_lane_index() - float(off) == row_blk.astype(F32), 1.0, 0.0) for off in (HEAD_DIM, 0)]
    mean_row = lax.broadcasted_iota(jnp.int32, kmean_ref.shape[2:], 0)

    @pl.when(i == 0)
    def _init():
        kmean_ref[...] = jnp.zeros_like(kmean_ref)

    for n in range(w_ref.shape[1] // col_chunk):
        cols = slice(n * col_chunk, (n + 1) * col_chunk)
        r = jnp.dot(h, w_ref[:, cols], preferred_element_type=F32)
        o_ref[0, :, cols] = r.astype(BF16)
        for c in range(col_chunk // LANES):
            col0 = n * col_chunk + c * LANES
            part = r[:, c * LANES:(c + 1) * LANES]
            if w_moba <= col0 < 2 * w_moba:
                kn = _head_rmsnorm(part, gk_ref[...])
                for hh, mine in enumerate(halves_of):
                    head = (col0 - w_moba) // HEAD_DIM + hh
                    kaug_ref[0, head] = jnp.where(mine, kn, tags[hh]).astype(BF16)
                    means = kmean_ref[0, head]
                    for jb in range(blocks):
                        mean = jnp.mean(kn[jb * ATT_BLOCK:(jb + 1) * ATT_BLOCK], axis=0, keepdims=True)
                        means = jnp.where(mean_row == blocks * i + jb, jnp.where(mine, mean, 0.0), means)
                    kmean_ref[0, head] = means
            elif 2 * w_moba <= col0 < 3 * w_moba:
                for hh, mine in enumerate(halves_of):
                    head = (col0 - 2 * w_moba) // HEAD_DIM + hh
                    vaug_ref[0, head] = jnp.where(mine, part, 1.0).astype(BF16)


def _inproj(x, mod, g, w_bf16, g_k):
    bsz, s, d = x.shape
    n = w_bf16.shape[1]
    tm = TOKEN_TILE
    n_blk = s // ATT_BLOCK
    per_head = lambda rows: pl.BlockSpec((1, N_HEADS, rows, LANES), lambda b, i: (b, 0, i, 0))
    return pl.pallas_call(
        functools.partial(_inproj_kernel, col_chunk=1024),
        grid=(bsz, s // tm),
        in_specs=[
            pl.BlockSpec((1, tm, d), lambda b, i: (b, i, 0)),
            pl.BlockSpec((1, N_MOD, d), lambda b, i: (b, 0, 0)),
            pl.BlockSpec((1, d), lambda b, i: (0, 0)),
            pl.BlockSpec((d, n), lambda b, i: (0, 0)),
            pl.BlockSpec((1, LANES), lambda b, i: (0, 0)),
        ],
        out_specs=[
            pl.BlockSpec((1, tm, n), lambda b, i: (b, i, 0)),
            per_head(tm),
            per_head(tm),
            pl.BlockSpec((1, N_HEADS, -(-n_blk // 8) * 8, LANES), lambda b, i: (b, 0, 0, 0)),
        ],
        out_shape=[
            jax.ShapeDtypeStruct((bsz, s, n), BF16),
            jax.ShapeDtypeStruct((bsz, N_HEADS, s, LANES), BF16),
            jax.ShapeDtypeStruct((bsz, N_HEADS, s, LANES), BF16),
            jax.ShapeDtypeStruct((bsz, N_HEADS, -(-n_blk // 8) * 8, LANES), F32),
        ],
        compiler_params=_params("arbitrary", "arbitrary"),
        name="inproj",
    )(x, mod, g.reshape(1, d), w_bf16, jnp.tile(g_k.reshape(1, HEAD_DIM), (1, HEADS_PER_STEP)))


def _indicator(cond, dtype):
    return jnp.where(cond, 1.0, 0.0).astype(dtype)


def _lane_index():
    return lax.broadcasted_iota(jnp.int32, (1, LANES), 1).astype(F32)


def _head_lane_mask():
    return lax.broadcasted_iota(jnp.int32, (1, LANES), 1) < HEAD_DIM


def _head_rmsnorm(t, g):
    first = _head_lane_mask()
    sq = t * t
    ss0 = jnp.sum(jnp.where(first, sq, 0.0), axis=-1, keepdims=True)
    ss1 = jnp.sum(jnp.where(first, 0.0, sq), axis=-1, keepdims=True)
    inv = jnp.where(first, lax.rsqrt(ss0 / HEAD_DIM + RMS_EPS), lax.rsqrt(ss1 / HEAD_DIM + RMS_EPS))
    return t * inv * g


def _split_heads(t):
    first = _head_lane_mask()
    zero = jnp.zeros_like(t)
    return jnp.where(first, t, zero), jnp.where(first, zero, t)


def _rel_bucket_table(max_dist):
    n = np.arange(max_dist + 1)
    max_exact = REL_BUCKETS // 2
    nf = np.maximum(n, 1).astype(np.float64)
    large = max_exact + (np.log(nf / max_exact) / math.log(REL_MAX_DIST / max_exact)
                         * (REL_BUCKETS - max_exact)).astype(np.int64)
    large = np.minimum(large, REL_BUCKETS - 1)
    return np.where(n < max_exact, n, large).astype(np.int32)


def _moba_kernel(q_ref, kaug_ref, vaug_ref, kmean_ref, gq_ref, rb_ref, o_ref,
                 toep_ref, s_ref, mvec_ref, acc_ref, *, n_blk):
    b = pl.program_id(1)
    qi = pl.program_id(2)
    blk = ATT_BLOCK
    qt = q_ref.shape[1]
    halves = qt // blk
    first = _head_lane_mask()

    @pl.when(jnp.logical_and(b == 0, qi == 0))
    def _build_bias_tiles():
        for h in range(HEADS_PER_STEP):
            for d in range(n_blk):
                r = rb_ref[0, h:h + 1, (n_blk - d) * blk:(n_blk - d + 2) * blk]
                rolled = pltpu.roll(jnp.broadcast_to(r, (blk, 2 * blk)), blk + 1, 1, stride=1, stride_axis=0)
                toep_ref[h, d] = rolled[:, :blk]

    qn = _head_rmsnorm(q_ref[0].astype(F32), gq_ref[...])
    gate_rows = kmean_ref.shape[2]
    blk_row = lax.broadcasted_iota(jnp.int32, (gate_rows, qt), 0).astype(F32)
    own = (halves * qi + lax.shift_right_logical(lax.broadcasted_iota(jnp.int32, (gate_rows, qt), 1),
                                                 blk.bit_length() - 1)).astype(F32)
    q_aug = []
    for h, mine in enumerate((first, jnp.logical_not(first))):
        qh = jnp.where(mine, qn, 0.0)
        km = kmean_ref[0, h]
        km_hi, q_hi = km.astype(BF16), qh.astype(BF16)
        km_lo, q_lo = (km - km_hi.astype(F32)).astype(BF16), (qh - q_hi.astype(F32)).astype(BF16)
        g = (lax.dot_general(km_hi, q_hi, _NT, preferred_element_type=F32)
             + lax.dot_general(km_lo, q_hi, _NT, preferred_element_type=F32)
             + lax.dot_general(km_hi, q_lo, _NT, preferred_element_type=F32))
        g = jnp.where(blk_row < own, g, NEG_INF)
        keep = jnp.where(blk_row == own, 1.0, 0.0)
        for _ in range(MOBA_TOPK):
            top = jnp.max(g, axis=0, keepdims=True)
            is_top = jnp.logical_and(g == top, top > NEG_INF)
            idx = jnp.min(jnp.where(is_top, blk_row, float(gate_rows)), axis=0, keepdims=True)
            pick = blk_row == idx
            keep = jnp.where(pick, 1.0, keep)
            g = jnp.where(pick, NEG_INF, g)
        mask_t = jnp.where(keep > 0.0, 0.0, MASK_LOGIT)
        before = (HEAD_DIM, 0)[h]
        pieces = [jnp.zeros((before, qt), F32)] if before else []
        pieces += [mask_t, jnp.zeros((LANES - before - gate_rows, qt), F32)]
        mask_logit = jnp.concatenate(pieces, axis=0).T
        q_aug.append(jnp.where(mine, qh * (HEAD_DIM ** -0.5), mask_logit).astype(BF16))

    def lane_max(s):
        return jnp.maximum(s[:, :LANES], s[:, LANES:])

    def scores(h, t, dist):
        k_t = kaug_ref[0, h, pl.ds(pl.multiple_of(t * qt, qt), qt), :]
        s = lax.dot_general(q_aug[h], k_t, _NT, preferred_element_type=F32)
        tiles = []
        for kb in range(halves):
            rows = []
            for qb in range(halves):
                part = s[qb * blk:(qb + 1) * blk, kb * blk:(kb + 1) * blk]
                if isinstance(dist, int) and halves * dist + qb - kb < 0:
                    rows.append(part)
                else:
                    rows.append(part + toep_ref[h, halves * dist + qb - kb])
            tiles.append(jnp.concatenate(rows, axis=0))
        return tiles

    row = lax.broadcasted_iota(jnp.int32, (qt, blk), 0)
    col = lax.broadcasted_iota(jnp.int32, (qt, blk), 1)
    for h in range(HEADS_PER_STEP):
        top = None
        for kb, s in enumerate(scores(h, qi, 0)):
            s = jnp.where(col + kb * blk <= row, s, NEG_INF)
            s_ref[h, halves * qi + kb] = s
            top = lane_max(s) if top is None else jnp.maximum(top, lane_max(s))
        mvec_ref[h] = top

    def in_pairs(count, visit):
        def quad(p, carry):
            visit(4 * p, 4)
            return carry

        lax.fori_loop(0, lax.shift_right_logical(count, 2), quad, 0)

        @pl.when((count & 2) == 2)
        def _pair():
            visit(count & ~3, 2)

        @pl.when((count & 1) == 1)
        def _last():
            visit(count - 1, 1)

    def pass1(t0, n_tiles):
        for h in range(HEADS_PER_STEP):
            top = mvec_ref[h]
            for t in [t0 + u for u in range(n_tiles)]:
                for kb, s in enumerate(scores(h, t, qi - t)):
                    s_ref[h, halves * t + kb] = s
                    top = jnp.maximum(top, lane_max(s))
            mvec_ref[h] = top

    in_pairs(qi, pass1)

    for h in range(HEADS_PER_STEP):
        m = jnp.max(mvec_ref[h], axis=-1, keepdims=True)
        mvec_ref[h] = jnp.broadcast_to(m, (qt, LANES))
        acc_ref[h] = jnp.zeros((qt, LANES), F32)

    def pass2(t0, n_tiles):
        keys = pl.ds(pl.multiple_of(t0 * qt, qt), n_tiles * qt)
        for h in range(HEADS_PER_STEP):
            m = mvec_ref[h]
            parts = []
            for kb in range(n_tiles * halves):
                s = s_ref[h, halves * t0 + kb]
                parts += [jnp.exp(s[:, :LANES] - m), jnp.exp(s[:, LANES:] - m)]
            p = jnp.concatenate(parts, axis=1).astype(BF16)
            acc_ref[h] += jnp.dot(p, vaug_ref[0, h, keys, :], preferred_element_type=F32)

    in_pairs(qi + 1, pass2)

    out = [acc_ref[h] / pltpu.roll(acc_ref[h], HEAD_DIM, 1) for h in range(HEADS_PER_STEP)]
    o_ref[0] = jnp.where(first, out[0], out[1]).astype(o_ref.dtype)


def _moba(proj, kaug, vaug, kmean, g_q, rel_rev):
    bsz, s, _ = proj.shape
    blk = ATT_BLOCK
    n_blk = s // blk
    n_hp = N_HEADS // HEADS_PER_STEP
    w = N_HEADS * HEAD_DIM
    qt = MOBA_Q_TILE
    per_pair = lambda arr: pl.BlockSpec((1, HEADS_PER_STEP) + arr.shape[2:], lambda hp, b, i: (b, hp, 0, 0))
    return pl.pallas_call(
        functools.partial(_moba_kernel, n_blk=n_blk),
        grid=(n_hp, bsz, s // qt),
        in_specs=[
            pl.BlockSpec((1, qt, LANES), lambda hp, b, i: (b, i, hp)),
            per_pair(kaug), per_pair(vaug), per_pair(kmean),
            pl.BlockSpec((1, LANES), lambda hp, b, i: (0, 0)),
            pl.BlockSpec((1, HEADS_PER_STEP, rel_rev.shape[-1]), lambda hp, b, i: (hp, 0, 0)),
        ],
        out_specs=pl.BlockSpec((1, qt, LANES), lambda hp, b, i: (b, i, hp)),
        out_shape=jax.ShapeDtypeStruct((bsz, s, w), BF16),
        scratch_shapes=[
            pltpu.VMEM((HEADS_PER_STEP, n_blk, blk, blk), F32),
            pltpu.VMEM((HEADS_PER_STEP, n_blk, qt, blk), F32),
            pltpu.VMEM((HEADS_PER_STEP, qt, LANES), F32),
            pltpu.VMEM((HEADS_PER_STEP, qt, LANES), F32),
        ],
        compiler_params=_params("arbitrary", "arbitrary", "arbitrary"),
        name="moba",
    )(proj, kaug, vaug, kmean, jnp.tile(g_q.reshape(1, HEAD_DIM), (1, HEADS_PER_STEP)), rel_rev)


def _sb_kernel(q_ref, k_ref, v_ref, o_ref, acc_ref, carry_ref):
    ti = pl.program_id(2)
    blk = ATT_BLOCK
    n_sub = q_ref.shape[1] // blk
    first = n_sub * ti
    row = lax.broadcasted_iota(jnp.int32, (blk, blk), 0)
    col = lax.broadcasted_iota(jnp.int32, (blk, blk), 1)
    past = col < row
    later = _indicator(row > col, BF16)

    q_heads = []
    for sub in range(n_sub):
        heads = _split_heads(q_ref[0, sub * blk:(sub + 1) * blk, :].astype(F32) * (HEAD_DIM ** -0.5))
        q_heads.append([t.astype(BF16) for t in heads])

    def walk(jobs, fresh):
        for sub, blocks in jobs:
            for h in range(HEADS_PER_STEP):
                carry = None if fresh else carry_ref[sub, h]
                total = None if fresh else acc_ref[sub, h]
                for j, diagonal in blocks:
                    start = j * blk if isinstance(j, int) else pl.multiple_of(j * blk, blk)
                    z = lax.dot_general(q_heads[sub][h], k_ref[0, pl.ds(start, blk), :], _NT,
                                        preferred_element_type=F32)
                    sp = jnp.maximum(z, 0.0) + jnp.log(1.0 + jnp.exp(-jnp.abs(z)))
                    masked = jnp.where(past, sp, 0.0) if diagonal else sp
                    after = jnp.dot(masked.astype(BF16), later, preferred_element_type=F32)
                    row_sum = after[:, 0:1] + masked[:, 0:1]
                    a = jnp.exp(z - sp - after)
                    if diagonal:
                        a = jnp.where(past, a, 0.0)
                    pv = jnp.dot(a.astype(BF16), v_ref[0, pl.ds(start, blk), :], preferred_element_type=F32)
                    if carry is not None:
                        pv = pv * jnp.exp(-carry)
                    total = pv if total is None else total + pv
                    carry = row_sum if carry is None else carry + row_sum
                carry_ref[sub, h] = carry
                acc_ref[sub, h] = total

    @pl.when(ti == 0)
    def _first_step():
        walk([(sub, [(sub, True)] + ([(sub - 1, False)] if sub else [])) for sub in range(n_sub)], True)

    @pl.when(ti > 0)
    def _own_and_previous():
        walk([(sub, [(first + sub, True), (first + sub - 1, False)]) for sub in range(n_sub)], True)

    def slowest_decay(subs):
        worst = functools.reduce(jnp.minimum, [carry_ref[sub, h] for sub in subs for h in range(HEADS_PER_STEP)])
        return jnp.min(worst)

    @pl.when(slowest_decay(range(n_sub)) < SB_EXIT_SUM)
    def _walk_further_back():
        for sub in range(n_sub):
            def cond(state):
                j, worst = state
                return jnp.logical_and(j >= 0, worst < SB_EXIT_SUM)

            def body(state, sub=sub):
                j, _ = state
                walk([(sub, [(j, False)])], False)
                return j - 1, slowest_decay([sub])

            lax.while_loop(cond, body, (first + sub - 2, slowest_decay([sub])))

    for sub in range(n_sub):
        o_ref[0, sub * blk:(sub + 1) * blk, :] = jnp.where(
            _head_lane_mask(), acc_ref[sub, 0], acc_ref[sub, 1]).astype(o_ref.dtype)


def _stick_breaking(proj):
    bsz, s, _ = proj.shape
    blk = ATT_BLOCK
    qt = SB_Q_BLOCKS * blk
    n_hp = N_HEADS // HEADS_PER_STEP
    w = N_HEADS * HEAD_DIM
    base = 3 * n_hp
    return pl.pallas_call(
        _sb_kernel,
        grid=(n_hp, bsz, s // qt),
        in_specs=[
            pl.BlockSpec((1, qt, LANES), lambda hp, b, i: (b, i, base + hp)),
            pl.BlockSpec((1, s, LANES), lambda hp, b, i: (b, 0, base + n_hp + hp)),
            pl.BlockSpec((1, s, LANES), lambda hp, b, i: (b, 0, base + 2 * n_hp + hp)),
        ],
        out_specs=pl.BlockSpec((1, qt, LANES), lambda hp, b, i: (b, i, hp)),
        out_shape=jax.ShapeDtypeStruct((bsz, s, w), BF16),
        scratch_shapes=[
            pltpu.VMEM((SB_Q_BLOCKS, HEADS_PER_STEP, blk, LANES), F32),
            pltpu.VMEM((SB_Q_BLOCKS, HEADS_PER_STEP, blk, 1), F32),
        ],
        compiler_params=_params("arbitrary", "arbitrary", "arbitrary"),
        name="stickbrk",
    )(proj, proj, proj)


def _pack_rows(t):
    n = t.shape[1] // 2
    bits = pltpu.bitcast(t.astype(BF16).astype(F32), U32)
    return bits[:, :n] | (bits[:, n:] >> 16)


def _unpack_rows(p):
    hi = pltpu.bitcast(p & jnp.uint32(0xFFFF0000), F32)
    lo = pltpu.bitcast(p << 16, F32)
    return jnp.concatenate([hi, lo], axis=1)


def _merge_kernel(x_ref, oa_ref, ob_ref, ga_ref, gb_ref, mod_ref, wbm_ref, wbs_ref, wo_ref, gf_ref,
                  wrh_ref, wrl_ref, br_ref, x1_ref, h2_ref, rt_ref, slot_t_ref, cnt_ref, before_ref, carry_ref):
    ma = jnp.dot(oa_ref[0], wbm_ref[...], preferred_element_type=F32)
    mb = jnp.dot(ob_ref[0], wbs_ref[...], preferred_element_type=F32)
    merged = (jax.nn.sigmoid(ga_ref[0].astype(F32)) * ma + jax.nn.sigmoid(gb_ref[0].astype(F32)) * mb)
    t = jnp.dot(merged.astype(BF16), wo_ref[...], preferred_element_type=F32)
    x1 = x_ref[0] + mod_ref[0, 2:3, :] * t
    x1_ref[0] = x1
    h2 = _rms_modulate(x1, gf_ref[...], mod_ref[0, 3:4, :], mod_ref[0, 4:5, :])
    hi = h2.astype(BF16)
    h2_ref[0] = hi
    lo = (h2 - hi.astype(F32)).astype(BF16)
    logits = (lax.dot_general(wrh_ref[...], hi, _NT, preferred_element_type=F32)
              + lax.dot_general(wrh_ref[...], lo, _NT, preferred_element_type=F32)
              + lax.dot_general(wrl_ref[...], hi, _NT, preferred_element_type=F32) + br_ref[...])
    first_tile = jnp.logical_and(pl.program_id(0) == 0, pl.program_id(1) == 0)
    _route_tile(logits, first_tile, rt_ref, slot_t_ref, cnt_ref, before_ref, carry_ref)


def _merge(x, o_a, o_b, proj, mod, wbm, wbs, wo, g_ffn, w_router, b_router):
    bsz, s, d = x.shape
    w = o_a.shape[-1]
    tm = TOKEN_TILE
    gate_blk = (proj.shape[-1] - 2 * d) // d
    const = lambda shape: pl.BlockSpec(shape, lambda b, i: (0,) * len(shape))
    tok = lambda width: pl.BlockSpec((1, tm, width), lambda b, i: (b, i, 0))
    wr_hi = w_router.T.astype(BF16)
    wr_lo = (w_router.T - wr_hi.astype(F32)).astype(BF16)
    per_b = s // tm
    tiles = bsz * per_b
    per_tile = lambda rows, cols: pl.BlockSpec((1, rows, cols), lambda b, i: (b * per_b + i, 0, 0))
    return pl.pallas_call(
        _merge_kernel,
        grid=(bsz, s // tm),
        in_specs=[
            tok(d), tok(w), tok(w),
            pl.BlockSpec((1, tm, d), lambda b, i: (b, i, gate_blk)),
            pl.BlockSpec((1, tm, d), lambda b, i: (b, i, gate_blk + 1)),
            pl.BlockSpec((1, N_MOD, d), lambda b, i: (b, 0, 0)),
            const((w, d)), const((w, d)), const((d, d)), const((1, d)),
            const((LANES, d)), const((LANES, d)), const((LANES, 1)),
        ],
        out_specs=[tok(d), tok(d), tok(ROUTE_COLS), per_tile(ROUTE_COLS, tm), per_tile(LANES, 1), per_tile(LANES, 1)],
        out_shape=[
            jax.ShapeDtypeStruct((bsz, s, d), F32),
            jax.ShapeDtypeStruct((bsz, s, d), BF16),
            jax.ShapeDtypeStruct((bsz, s, ROUTE_COLS), F32),
            jax.ShapeDtypeStruct((tiles, ROUTE_COLS, tm), F32),
            jax.ShapeDtypeStruct((tiles, LANES, 1), F32),
            jax.ShapeDtypeStruct((tiles, LANES, 1), F32),
        ],
        scratch_shapes=[pltpu.VMEM((LANES, 1), F32)],
        compiler_params=_params("arbitrary", "arbitrary"),
        name="merge",
    )(x, o_a, o_b, proj, proj, mod, wbm, wbs, wo, g_ffn.reshape(1, d), wr_hi, wr_lo, b_router)


ROUTE_COLS = 8
SEG_ALIGN = 8


def _local_rows(tm):
    return 2 * tm + N_EXPERTS * SEG_ALIGN


def _route_tile(lg_t, first_tile, rt_ref, slot_t_ref, cnt_ref, before_ref, carry_ref):
    tm = lg_t.shape[1]
    rows = -(-(N_GROUPS + N_EXPERTS) // 8) * 8
    lg = lg_t[:rows]
    row = lax.broadcasted_iota(jnp.int32, (rows, tm), 0).astype(F32)

    @pl.when(first_tile)
    def _init():
        carry_ref[...] = jnp.zeros_like(carry_ref)

    def first_max(vals):
        top = jnp.max(vals, axis=0, keepdims=True)
        idx = jnp.min(jnp.where(vals == top, row, float(rows)), axis=0, keepdims=True)
        return top, idx

    g_logit = jnp.where(row < N_GROUPS, lg, NEG_INF)
    g_top, g_idx = first_max(g_logit)
    g_w = 1.0 / jnp.sum(jnp.exp(g_logit - g_top), axis=0, keepdims=True)
    lo = N_GROUPS + g_idx * EXPERTS_PER_GROUP
    e_logit = jnp.where(jnp.logical_and(row >= lo, row < lo + EXPERTS_PER_GROUP), lg, NEG_INF)
    e_top0, row0 = first_max(e_logit)
    e_top1, row1 = first_max(jnp.where(row == row0, NEG_INF, e_logit))
    z = jnp.exp(e_top1 - e_top0)
    w0 = g_w / (1.0 + z)
    w1 = g_w * z / (1.0 + z)

    hot = jnp.logical_or(row == row0, row == row1)
    t_src = lax.broadcasted_iota(jnp.int32, (tm, tm), 0)
    t_dst = lax.broadcasted_iota(jnp.int32, (tm, tm), 1)
    earlier = jnp.dot(_indicator(hot, BF16), _indicator(t_src < t_dst, BF16), preferred_element_type=F32)
    cnt = jnp.sum(_indicator(hot, F32), axis=1, keepdims=True)
    chunks = jnp.ceil(cnt * (1.0 / SEG_ALIGN))
    spare = jnp.zeros((LANES - rows, LANES), F32)
    e_dst = lax.broadcasted_iota(jnp.int32, (LANES, LANES), 0)
    e_src = lax.broadcasted_iota(jnp.int32, (LANES, LANES), 1)
    chunks_wide = jnp.concatenate([jnp.broadcast_to(chunks, (rows, LANES)), spare], axis=0).astype(BF16)
    seg_start = SEG_ALIGN * jnp.dot(_indicator(e_src < e_dst, BF16), chunks_wide,
                                    preferred_element_type=F32)[:rows, 0:1]
    local = seg_start + earlier
    slot0 = jnp.sum(jnp.where(row == row0, local, 0.0), axis=0, keepdims=True)
    slot1 = jnp.sum(jnp.where(row == row1, local, 0.0), axis=0, keepdims=True)
    seg_rows = jnp.concatenate([SEG_ALIGN * chunks, spare[:, 0:1]], axis=0)
    cnt_ref[0] = seg_rows
    before_ref[0] = carry_ref[...]
    carry_ref[...] += seg_rows

    out_row = lax.broadcasted_iota(jnp.int32, (ROUTE_COLS, tm), 0)
    packed = jnp.zeros((ROUTE_COLS, tm), F32)
    for c, v in enumerate([slot0, slot1, w0, w1]):
        packed = jnp.where(out_row == c, v, packed)
    slot_t_ref[0] = packed
    rt_ref[0] = jnp.concatenate([packed, jnp.zeros((LANES - ROUTE_COLS, tm), F32)], axis=0).T[:, :ROUTE_COLS]


CHUNK_ROWS = (8 * SEG_ALIGN, 4 * SEG_ALIGN, 2 * SEG_ALIGN, SEG_ALIGN)


def _plan_refs(refs):
    return tuple(tuple(refs[3 * k:3 * k + 3]) for k in range(len(CHUNK_ROWS)))


def _for_each_chunk(tile, plan, visit):
    for (n_ref, local_ref, global_ref), n_rows in zip(plan, CHUNK_ROWS):
        per_tile = local_ref.shape[0] // n_ref.shape[0]

        def one(c, carry, local_ref=local_ref, global_ref=global_ref, per_tile=per_tile, n_rows=n_rows):
            k = tile * per_tile + c
            visit(pl.multiple_of(local_ref[k], SEG_ALIGN), pl.multiple_of(global_ref[k], SEG_ALIGN), n_rows)
            return carry

        lax.fori_loop(0, n_ref[tile], one, 0)


def _copy_plan(cnt, seg_row0, local_rows):
    done = jnp.zeros_like(cnt)
    local0 = jnp.cumsum(cnt, axis=1) - cnt
    plan = ()
    for k, size in enumerate(CHUNK_ROWS):
        left = cnt - done
        n = left // size if k == 0 else (left // size) % 2
        max_entries = local_rows // size if k == 0 else N_EXPERTS
        end = jnp.cumsum(n, axis=1)
        start = (end - n)[:, None, :]
        c = jnp.arange(max_entries, dtype=jnp.int32)[None, :, None]
        inside = jnp.logical_and(c >= start, c < end[:, None, :])
        pick = lambda base: jnp.sum(jnp.where(inside, base[:, None, :] + size * (c - start), 0), axis=2).reshape(-1)
        plan += (end[:, -1], pick(local0 + done), pick(seg_row0 + done))
        done = done + n * size
    return plan


def _dispatch_kernel(*refs):
    plan = _plan_refs(refs)
    (tail_row_ref, tail_n_ref, n_used_ref, slot_t_ref, h_ref, xs_ref,
     loc_ref, zero_ref, sems, zero_sem) = refs[3 * len(CHUNK_ROWS):]
    tile = pl.program_id(0)
    n_loc, tm = loc_ref.shape[1], h_ref.shape[0]
    buf = tile & 1

    slot = lax.broadcasted_iota(jnp.int32, (n_loc, tm), 0).astype(F32)
    place = jnp.logical_or(slot == slot_t_ref[0, 0:1, :], slot == slot_t_ref[0, 1:2, :])
    loc_ref[buf] = _pack_rows(jnp.dot(_indicator(place, BF16), h_ref[...], preferred_element_type=F32))

    def chunk_copy(b):
        return lambda local, row, n: pltpu.make_async_copy(
            loc_ref.at[b, pl.ds(local, n), :], xs_ref.at[pl.ds(row, n), :], sems.at[b])

    _for_each_chunk(tile, plan, lambda l, r, n: chunk_copy(buf)(l, r, n).start())

    @pl.when(tile > 0)
    def _previous_done():
        _for_each_chunk(tile - 1, plan, lambda l, r, n: chunk_copy(1 - buf)(l, r, n).wait())

    @pl.when(tile == pl.num_programs(0) - 1)
    def _last_done():
        _for_each_chunk(tile, plan, lambda l, r, n: chunk_copy(buf)(l, r, n).wait())
        zero_ref[...] = jnp.zeros_like(zero_ref)
        block_rows = zero_ref.shape[0]

        def fill(action):
            def per_expert(e, carry):
                per_big = CHUNK_ROWS[0] // SEG_ALIGN
                n_big = tail_n_ref[e] // per_big
                first_small = tail_row_ref[e] + n_big * CHUNK_ROWS[0]

                def zero_rows(row0, n_rows):
                    def one(c, inner):
                        row = pl.multiple_of(row0 + c * n_rows, SEG_ALIGN)
                        action(pltpu.make_async_copy(zero_ref.at[pl.ds(0, n_rows), :],
                                                     xs_ref.at[pl.ds(row, n_rows), :], zero_sem))
                        return inner
                    return one

                lax.fori_loop(0, n_big, zero_rows(tail_row_ref[e], CHUNK_ROWS[0]), 0)
                lax.fori_loop(0, tail_n_ref[e] - n_big * per_big, zero_rows(first_small, SEG_ALIGN), 0)
                return carry

            lax.fori_loop(0, N_EXPERTS, per_expert, 0)

            def per_block(blk, carry):
                row = pl.multiple_of(blk * block_rows, block_rows)
                action(pltpu.make_async_copy(zero_ref, xs_ref.at[pl.ds(row, block_rows), :], zero_sem))
                return carry

            lax.fori_loop(n_used_ref[0], xs_ref.shape[0] // block_rows, per_block, 0)

        fill(lambda copy: copy.start())
        fill(lambda copy: copy.wait())


def _dispatch(plan, tail_row, tail_chunks, n_used, slot_t, h2, n_rows):
    n, d = h2.shape
    tm = TOKEN_TILE
    grid_spec = pltpu.PrefetchScalarGridSpec(
        num_scalar_prefetch=len(plan) + 3,
        grid=(n // tm,),
        in_specs=[
            pl.BlockSpec((1, ROUTE_COLS, tm), lambda i, *_: (i, 0, 0)),
            pl.BlockSpec((tm, d), lambda i, *_: (i, 0)),
        ],
        out_specs=pl.BlockSpec(memory_space=pl.ANY),
        scratch_shapes=[pltpu.VMEM((2, _local_rows(tm), d // 2), U32), pltpu.VMEM((ROW_BLOCK, d // 2), U32),
                        pltpu.SemaphoreType.DMA((2,)), pltpu.SemaphoreType.DMA(())],
    )
    return pl.pallas_call(
        _dispatch_kernel,
        grid_spec=grid_spec,
        out_shape=jax.ShapeDtypeStruct((n_rows, d // 2), U32),
        compiler_params=_params("arbitrary"),
        name="dispatch",
    )(*plan, tail_row, tail_chunks, n_used, slot_t, h2)


def _expert_kernel(blk_e_ref, n_used_ref, xs_ref, wg_ref, wu_ref, wd_ref, ys_ref, wg_bf, wu_bf, wd_bf):
    i = pl.program_id(0)
    used = i < n_used_ref[0]

    @pl.when(jnp.logical_and(used, jnp.logical_or(i == 0, blk_e_ref[i] != blk_e_ref[jnp.maximum(i - 1, 0)])))
    def _new_expert():
        wg_bf[...] = wg_ref[0].astype(BF16)
        wu_bf[...] = wu_ref[0].astype(BF16)
        wd_bf[...] = wd_ref[0].astype(BF16)

    @pl.when(used)
    def _compute():
        xb = _unpack_rows(xs_ref[...]).astype(BF16)
        g = jnp.dot(xb, wg_bf[...], preferred_element_type=F32)
        u = jnp.dot(xb, wu_bf[...], preferred_element_type=F32)
        hid = (g * jax.nn.sigmoid(g) * u).astype(BF16)
        ys_ref[...] = _pack_rows(jnp.dot(hid, wd_bf[...], preferred_element_type=F32))

    @pl.when(i >= n_used_ref[0])
    def _unused():
        ys_ref[...] = jnp.zeros_like(ys_ref)


def _experts(blk_e, n_used, xs, wg, wu, wd):
    n_rows, half = xs.shape
    _, d, de = wg.shape
    grid_spec = pltpu.PrefetchScalarGridSpec(
        num_scalar_prefetch=2,
        grid=(n_rows // ROW_BLOCK,),
        in_specs=[
            pl.BlockSpec((ROW_BLOCK, half), lambda i, e, u: (jnp.minimum(i, u[0] - 1), 0)),
            pl.BlockSpec((1, d, de), lambda i, e, u: (e[i], 0, 0)),
            pl.BlockSpec((1, d, de), lambda i, e, u: (e[i], 0, 0)),
            pl.BlockSpec((1, de, d), lambda i, e, u: (e[i], 0, 0)),
        ],
        out_specs=pl.BlockSpec((ROW_BLOCK, half), lambda i, e, u: (i, 0)),
        scratch_shapes=[pltpu.VMEM((d, de), BF16), pltpu.VMEM((d, de), BF16), pltpu.VMEM((de, d), BF16)],
    )
    return pl.pallas_call(
        _expert_kernel,
        grid_spec=grid_spec,
        out_shape=jax.ShapeDtypeStruct((n_rows, half), U32),
        compiler_params=_params("arbitrary"),
        name="experts",
    )(blk_e, n_used, xs, wg, wu, wd)


def _combine_kernel(*refs):
    plan = _plan_refs(refs)
    x1_ref, mod_ref, rt_ref, ys_ref, o_ref, loc_ref, sems = refs[3 * len(CHUNK_ROWS):]
    tm = x1_ref.shape[1]
    n_loc = loc_ref.shape[1]
    tile = pl.program_id(0) * pl.num_programs(1) + pl.program_id(1)
    n_tiles = pl.num_programs(0) * pl.num_programs(1)
    buf = tile & 1

    def chunk_copy(b):
        return lambda local, row, n: pltpu.make_async_copy(
            ys_ref.at[pl.ds(row, n), :], loc_ref.at[b, pl.ds(local, n), :], sems.at[b])

    @pl.when(tile == 0)
    def _init():
        loc_ref[...] = jnp.zeros_like(loc_ref)
        _for_each_chunk(tile, plan, lambda l, r, n: chunk_copy(buf)(l, r, n).start())

    @pl.when(tile + 1 < n_tiles)
    def _fetch_next():
        _for_each_chunk(tile + 1, plan, lambda l, r, n: chunk_copy(1 - buf)(l, r, n).start())

    _for_each_chunk(tile, plan, lambda l, r, n: chunk_copy(buf)(l, r, n).wait())

    y = _unpack_rows(loc_ref[buf]).astype(BF16)
    rt = rt_ref[0]
    slot = lax.broadcasted_iota(jnp.int32, (tm, n_loc), 1).astype(F32)
    mix = jnp.where(slot == rt[:, 0:1], rt[:, 2:3], jnp.where(slot == rt[:, 1:2], rt[:, 3:4], 0.0)).astype(BF16)
    o_ref[0] = x1_ref[0] + mod_ref[0, 5:6, :] * jnp.dot(mix, y, preferred_element_type=F32)


def _combine(plan, x1, mod, rt, ys):
    bsz, s, d = x1.shape
    tm = TOKEN_TILE
    grid_spec = pltpu.PrefetchScalarGridSpec(
        num_scalar_prefetch=len(plan),
        grid=(bsz, s // tm),
        in_specs=[
            pl.BlockSpec((1, tm, d), lambda b, i, *_: (b, i, 0)),
            pl.BlockSpec((1, N_MOD, d), lambda b, i, *_: (b, 0, 0)),
            pl.BlockSpec((1, tm, ROUTE_COLS), lambda b, i, *_: (b, i, 0)),
            pl.BlockSpec(memory_space=pl.ANY),
        ],
        out_specs=pl.BlockSpec((1, tm, d), lambda b, i, *_: (b, i, 0)),
        scratch_shapes=[pltpu.VMEM((2, _local_rows(tm), d // 2), U32), pltpu.SemaphoreType.DMA((2,))],
    )
    return pl.pallas_call(
        _combine_kernel,
        grid_spec=grid_spec,
        out_shape=jax.ShapeDtypeStruct((bsz, s, d), F32),
        compiler_params=_params("arbitrary", "arbitrary"),
        name="combine",
    )(*plan, x1, mod, rt, ys)


def _layer(x, mod, g_mix, w_in, g_q, g_k, rel_bias, w_br_moba, w_br_sb, w_out, g_ffn,
           w_rg, b_rg, w_re, b_re, w_gate, w_up, w_down):
    bsz, s, d = x.shape
    n = bsz * s
    n_blk = s // ATT_BLOCK

    proj, kaug, vaug, kmean = _inproj(x, mod, g_mix, w_in.astype(BF16), g_k)

    top = (n_blk + 1) * ATT_BLOCK - 1
    dist = np.maximum(top - np.arange((n_blk + 2) * ATT_BLOCK), 0)
    rel_rev = rel_bias[:, _rel_bucket_table(top)[dist]].reshape(N_HEADS // HEADS_PER_STEP, HEADS_PER_STEP, -1)

    o_a = _moba(proj, kaug, vaug, kmean, g_q, rel_rev)
    o_b = _stick_breaking(proj)

    pad = LANES - N_GROUPS - N_EXPERTS
    w_router = jnp.concatenate([w_rg, w_re, jnp.zeros((d, pad), F32)], axis=1)
    b_router = jnp.concatenate([b_rg, b_re, jnp.zeros((pad,), F32)]).reshape(LANES, 1)
    x1, h2, rt, slot_t, tile_cnt, tile_before = _merge(x, o_a, o_b, proj, mod, w_br_moba.astype(BF16),
                                                       w_br_sb.astype(BF16), w_out.astype(BF16), g_ffn,
                                                       w_router, b_router)

    experts = slice(N_GROUPS, N_GROUPS + N_EXPERTS)
    cnt = tile_cnt[:, experts, 0].astype(jnp.int32)
    total = jnp.sum(cnt, axis=0)
    padded = (total + ROW_BLOCK - 1) // ROW_BLOCK * ROW_BLOCK
    pend = jnp.cumsum(padded)
    pstart = pend - padded
    max_rows = 2 * n + cnt.shape[0] * N_EXPERTS * (SEG_ALIGN - 1) + N_EXPERTS * (ROW_BLOCK - 1)
    n_blocks = -(-max_rows // ROW_BLOCK)
    blk_row0 = jnp.arange(n_blocks, dtype=jnp.int32) * ROW_BLOCK
    blk_e = jnp.minimum(jnp.sum((blk_row0[:, None] >= pend[None, :]).astype(jnp.int32), axis=1), N_EXPERTS - 1)
    n_used = (pend[-1] // ROW_BLOCK).astype(jnp.int32).reshape(1)
    seg_row0 = pstart[None, :] + tile_before[:, experts, 0].astype(jnp.int32)
    plan = _copy_plan(cnt, seg_row0, _local_rows(TOKEN_TILE))

    xs = _dispatch(plan, pstart + total, (padded - total) // SEG_ALIGN, n_used, slot_t,
                   h2.reshape(n, d), n_blocks * ROW_BLOCK)
    ys = _experts(blk_e, n_used, xs, w_gate, w_up, w_down)
    return _combine(plan, x1, mod, rt, ys)


def kernel(x, c, w_ada, b_ada, g_mix, w_in, g_q, g_k, rel_bias, w_br_moba, w_br_sb, w_out, g_ffn,
           w_rg, b_rg, w_re, b_re, w_gate, w_up, w_down):
    bsz, s, d = x.shape
    assert s % TOKEN_TILE == 0 and s % MOBA_Q_TILE == 0 and s % (SB_Q_BLOCKS * ATT_BLOCK) == 0
    assert d % (2 * LANES) == 0
    for l in range(w_ada.shape[0]):
        mod = _adaln(c, w_ada[l], b_ada[l]).reshape(bsz, N_MOD, d)
        x = _layer(x, mod, g_mix[l], w_in[l], g_q[l], g_k[l], rel_bias, w_br_moba[l], w_br_sb[l], w_out[l],
                   g_ffn[l], w_rg[l], b_rg[l], w_re[l], b_re[l], w_gate[l], w_up[l], w_down[l])
    return x
```
